```python
import jax, jax.numpy as jnp
from jax import lax
import numpy as np

D_MODEL = 1024
BATCH = 8
SEQ = 8192
DEPTH = 1

N_META = 16
D_MIX = D_MODEL
D_CONV = D_MIX // 2
CONV_HEADS = 8
CONV_WIDTH = 3
D_RET = D_MIX - D_CONV
RET_HEADS = 4
RET_HEAD_DIM = D_RET // RET_HEADS
CHUNK = 128
ROPE_BASE = 10000.0
EPS = 1e-6
N_PROJ = 8

kernel_name = "hymba_conv_retention_hybrid"


def rms_norm(x, g):
    xf = x.astype(jnp.float32)
    y = xf * lax.rsqrt(jnp.mean(xf * xf, axis=-1, keepdims=True) + EPS)
    return (y * g.astype(jnp.float32)).astype(x.dtype)


def rotary(t, pos):
    half = t.shape[-1] // 2
    freqs = 1.0 / (ROPE_BASE ** (jnp.arange(half, dtype=jnp.float32) / half))
    ang = pos.astype(jnp.float32)[:, None] * freqs[None, :]
    cos = jnp.cos(ang)[None, :, None, :]
    sin = jnp.sin(ang)[None, :, None, :]
    t1, t2 = t[..., :half], t[..., half:]
    return jnp.concatenate([t1 * cos - t2 * sin, t1 * sin + t2 * cos], axis=-1)


def short_conv_branch(h, b_gate, c_gate, conv_w):
    u = c_gate * h
    kern = conv_w.reshape(CONV_WIDTH, 1, D_CONV).astype(u.dtype)
    conv = lax.conv_general_dilated(
        u, kern, window_strides=(1,), padding=[(CONV_WIDTH - 1, 0)],
        dimension_numbers=('NWC', 'WIO', 'NWC'), feature_group_count=D_CONV)
    return b_gate * conv


def retention_chunkwise(q, k, v):
    bsz, L, H, d = q.shape
    pad = CHUNK - N_META
    P = L + pad
    n_chunks = P // CHUNK

    def to_chunks(t):
        t = jnp.pad(t, ((0, 0), (pad, 0), (0, 0), (0, 0)))
        return t.reshape(bsz, n_chunks, CHUNK, H, d).transpose(0, 3, 1, 2, 4)

    qc, kc, vc = to_chunks(q), to_chunks(k), to_chunks(v)
    log_g = jnp.log(1.0 - 2.0 ** (-5.0 - jnp.arange(H, dtype=jnp.float32)))
    idx = jnp.arange(CHUNK, dtype=jnp.float32)
    diff = idx[:, None] - idx[None, :]
    decay = jnp.where(diff[None] >= 0, jnp.exp(diff[None] * log_g[:, None, None]), 0.0)

    scores = jnp.einsum('bhnid,bhnjd->bhnij', qc, kc) * decay[None, :, None]
    inner = jnp.einsum('bhnij,bhnje->bhnie', scores, vc)

    zeta = jnp.exp((CHUNK - 1 - idx)[None, :] * log_g[:, None])
    upd = jnp.einsum('bhnjd,bhnje->nbhde', kc * zeta[None, :, None, :, None], vc)
    chunk_decay = jnp.exp(CHUNK * log_g)[None, :, None, None]

    def step(state, u):
        return chunk_decay * state + u, state

    init = jnp.zeros((bsz, H, d, d), jnp.float32)
    _, states = lax.scan(step, init, upd)
    xi = jnp.exp((idx + 1.0)[None, :] * log_g[:, None])
    cross = jnp.einsum('bhnid,nbhde->bhnie', qc * xi[None, :, None, :, None], states)

    out = (inner + cross).transpose(0, 2, 3, 1, 4).reshape(bsz, P, H, d)
    return out[:, pad:]


def head_group_norm(o, g):
    mu = jnp.mean(o, axis=-1, keepdims=True)
    var = jnp.mean(jnp.square(o - mu), axis=-1, keepdims=True)
    y = (o - mu) * lax.rsqrt(var + EPS)
    bsz, L = o.shape[:2]
    return y.reshape(bsz, L, D_RET) * g.astype(jnp.float32)


def mixer_layer(h, norm_g, w_in, conv_w, ret_norm_g, w_out, pos):
    bsz, L, _ = h.shape
    hn = rms_norm(h, norm_g)
    proj = jnp.einsum('bld,de->ble', hn, w_in)
    cx, cb, cc, cg, q, k, v, rg = jnp.split(proj, N_PROJ, axis=-1)

    conv_out = short_conv_branch(cx, cb, cc, conv_w) * jax.nn.silu(cg)

    shp = (bsz, L, RET_HEADS, RET_HEAD_DIM)
    qf = rotary(q.reshape(shp).astype(jnp.float32), pos) * (RET_HEAD_DIM ** -0.5)
    kf = rotary(k.reshape(shp).astype(jnp.float32), pos)
    vf = v.reshape(shp).astype(jnp.float32)
    ret = head_group_norm(retention_chunkwise(qf, kf, vf), ret_norm_g)
    ret_out = ret.astype(h.dtype) * jax.nn.silu(rg)

    mixed = jnp.concatenate([conv_out, ret_out], axis=-1)
    return h + jnp.einsum('ble,ed->bld', mixed, w_out)


def _fwd_setup_inputs(seed: int = 0) -> dict:
    key = jax.random.key(seed)
    ks = jax.random.split(key, 8)
    x = jax.random.normal(ks[0], (BATCH, SEQ, D_MODEL), jnp.float32)
    meta = jax.random.normal(ks[1], (N_META, D_MODEL), jnp.float32)
    norm1_g = 1.0 + 0.02 * jax.random.normal(ks[2], (D_MODEL,), jnp.float32)
    w_in = jax.random.normal(ks[3], (D_MODEL, N_PROJ * D_CONV), jnp.float32) * D_MODEL ** -0.5
    conv_w = jax.random.normal(ks[4], (CONV_WIDTH, D_CONV), jnp.float32) * CONV_WIDTH ** -0.5
    ret_norm_g = 1.0 + 0.02 * jax.random.normal(ks[5], (D_RET,), jnp.float32)
    w_out = jax.random.normal(ks[6], (D_MIX, D_MODEL), jnp.float32) * D_MIX ** -0.5
    final_g = 1.0 + 0.02 * jax.random.normal(ks[7], (D_MODEL,), jnp.float32)
    return {"x": x, "meta": meta, "norm1_g": norm1_g, "w_in": w_in, "conv_w": conv_w,
            "ret_norm_g": ret_norm_g, "w_out": w_out, "final_g": final_g}


def _fwd_reference(x, meta, norm1_g, w_in, conv_w, ret_norm_g, w_out, final_g):
    bsz = x.shape[0]
    meta_b = jnp.broadcast_to(meta.astype(x.dtype)[None], (bsz, N_META, D_MODEL))
    h = jnp.concatenate([meta_b, x], axis=1)
    pos = jnp.arange(h.shape[1], dtype=jnp.int32)
    for _ in range(DEPTH):
        h = mixer_layer(h, norm1_g, w_in, conv_w, ret_norm_g, w_out, pos)
    h = rms_norm(h, final_g)
    return h[:, N_META:]


import jax as _jax
import jax.numpy as _jnp

TWIN_FORMAT = 'train_step'
FWD_PARAMS = ['x', 'meta', 'norm1_g', 'w_in', 'conv_w', 'ret_norm_g', 'w_out', 'final_g']
TWIN_WEIGHTS = ['meta', 'norm1_g', 'w_in', 'conv_w', 'ret_norm_g', 'w_out', 'final_g']
TWIN_DIFF_INPUT = 'x'
TWIN_INPUTS = ['x', 'meta', 'norm1_g', 'w_in', 'conv_w', 'ret_norm_g', 'w_out', 'final_g', 'loss_target', 'm_meta', 'm_norm1_g', 'm_w_in', 'm_conv_w', 'm_ret_norm_g', 'm_w_out', 'm_final_g', 'v_meta', 'v_norm1_g', 'v_w_in', 'v_conv_w', 'v_ret_norm_g', 'v_w_out', 'v_final_g']
TWIN_OUTPUTS = ['loss', 'grad_x', 'grad_meta', 'grad_norm1_g', 'grad_w_in', 'grad_conv_w', 'grad_ret_norm_g', 'grad_w_out', 'grad_final_g', 'delta_meta', 'delta_norm1_g', 'delta_w_in', 'delta_conv_w', 'delta_ret_norm_g', 'delta_w_out', 'delta_final_g', 'new_m_meta', 'new_m_norm1_g', 'new_m_w_in', 'new_m_conv_w', 'new_m_ret_norm_g', 'new_m_w_out', 'new_m_final_g', 'new_v_meta', 'new_v_norm1_g', 'new_v_w_in', 'new_v_conv_w', 'new_v_ret_norm_g', 'new_v_w_out', 'new_v_final_g']
TWIN_LEAF_KINDS = {'loss': 'loss', 'grad_x': 'grad_x', 'grad_meta': 'grad_w', 'grad_norm1_g': 'grad_w', 'grad_w_in': 'grad_w', 'grad_conv_w': 'grad_w', 'grad_ret_norm_g': 'grad_w', 'grad_w_out': 'grad_w', 'grad_final_g': 'grad_w', 'delta_meta': 'delta_w', 'delta_norm1_g': 'delta_w', 'delta_w_in': 'delta_w', 'delta_conv_w': 'delta_w', 'delta_ret_norm_g': 'delta_w', 'delta_w_out': 'delta_w', 'delta_final_g': 'delta_w', 'new_m_meta': 'new_m', 'new_m_norm1_g': 'new_m', 'new_m_w_in': 'new_m', 'new_m_conv_w': 'new_m', 'new_m_ret_norm_g': 'new_m', 'new_m_w_out': 'new_m', 'new_m_final_g': 'new_m', 'new_v_meta': 'new_v', 'new_v_norm1_g': 'new_v', 'new_v_w_in': 'new_v', 'new_v_conv_w': 'new_v', 'new_v_ret_norm_g': 'new_v', 'new_v_w_out': 'new_v', 'new_v_final_g': 'new_v'}


def _forward(args):
    return _fwd_reference(*[args[k] for k in FWD_PARAMS])


def _output_shape():
    def fwd():
        inp = _fwd_setup_inputs(0)
        return _fwd_reference(*[inp[k] for k in FWD_PARAMS])
    out = _jax.eval_shape(fwd)
    return out.shape, out.dtype

N_MICROBATCH = 1
ADAM_LR = 0.001
ADAM_B1 = 0.9
ADAM_B2 = 0.999
ADAM_EPS = 1e-08
ADAM_WD = 0.01
ADAM_STEP = 10
PER_EXAMPLE_BATCH_AXIS = {'x': 0, 'loss_target': 0}
SHARED_INPUTS = []
_WEIGHT_DTYPES = {'meta': _jnp.float32, 'norm1_g': _jnp.float32, 'w_in': _jnp.float32, 'conv_w': _jnp.float32, 'ret_norm_g': _jnp.float32, 'w_out': _jnp.float32, 'final_g': _jnp.float32}
MOMENT_SCALE = {'meta': 9.664917e-03, 'norm1_g': 2.648857e-01, 'w_in': 1.285008e-01, 'conv_w': 1.306914e-01, 'ret_norm_g': 1.292127e-01, 'w_out': 1.272307e-01, 'final_g': 6.393932e+01}


def _to_microbatches(a, axis):
    t = _jnp.moveaxis(a, axis, 0)
    t = t.reshape((N_MICROBATCH, t.shape[0] // N_MICROBATCH) + t.shape[1:])
    return _jnp.moveaxis(t, 1, axis + 1)


def setup_inputs(seed: int = 0) -> dict:
    inp = _fwd_setup_inputs(seed)
    key = _jax.random.fold_in(_jax.random.key(seed), 7919)
    shape, _ = _output_shape()
    out = dict(inp)
    out["loss_target"] = _jax.random.normal(_jax.random.fold_in(key, 0), shape, _jnp.float32)
    for i, name in enumerate(TWIN_WEIGHTS):
        w = inp[name].astype(_jnp.float32)
        if MOMENT_SCALE is None:
            s = _jnp.sqrt(_jnp.mean(_jnp.square(w)) + 1e-30)
        else:
            s = MOMENT_SCALE[name]
        km, kv = _jax.random.split(_jax.random.fold_in(key, i + 1))
        out[name] = w
        out["m_" + name] = s * _jax.random.normal(km, w.shape, _jnp.float32)
        out["v_" + name] = (s * s) * _jax.random.uniform(kv, w.shape, _jnp.float32, 0.5, 1.5)
    if N_MICROBATCH > 1:
        for name, axis in PER_EXAMPLE_BATCH_AXIS.items():
            out[name] = _to_microbatches(out[name], axis)
    return {'x': out['x'], 'meta': out['meta'], 'norm1_g': out['norm1_g'], 'w_in': out['w_in'], 'conv_w': out['conv_w'], 'ret_norm_g': out['ret_norm_g'], 'w_out': out['w_out'], 'final_g': out['final_g'], 'loss_target': out['loss_target'], 'm_meta': out['m_meta'], 'm_norm1_g': out['m_norm1_g'], 'm_w_in': out['m_w_in'], 'm_conv_w': out['m_conv_w'], 'm_ret_norm_g': out['m_ret_norm_g'], 'm_w_out': out['m_w_out'], 'm_final_g': out['m_final_g'], 'v_meta': out['v_meta'], 'v_norm1_g': out['v_norm1_g'], 'v_w_in': out['v_w_in'], 'v_conv_w': out['v_conv_w'], 'v_ret_norm_g': out['v_ret_norm_g'], 'v_w_out': out['v_w_out'], 'v_final_g': out['v_final_g']}


def _loss(weights, diff, rest, loss_target):
    with _jax.named_scope("forward"):
        args = {**rest, TWIN_DIFF_INPUT: diff, **{k: w.astype(_WEIGHT_DTYPES[k]) for k, w in weights.items()}}
        y = _forward(args)
    with _jax.named_scope("loss_head"):
        err = _jnp.square(y.astype(_jnp.float32) - loss_target)
        return 0.5 * _jnp.sum(_jnp.mean(err, axis=-1)) if err.ndim else 0.5 * err


def _adamw(w, g, m, v):
    m = ADAM_B1 * m + (1.0 - ADAM_B1) * g
    v = ADAM_B2 * v + (1.0 - ADAM_B2) * _jnp.square(g)
    m_hat = m / (1.0 - ADAM_B1 ** ADAM_STEP)
    v_hat = v / (1.0 - ADAM_B2 ** ADAM_STEP)
    delta = -ADAM_LR * (m_hat / (_jnp.sqrt(v_hat) + ADAM_EPS) + ADAM_WD * w)
    return delta, m, v


def reference(x, meta, norm1_g, w_in, conv_w, ret_norm_g, w_out, final_g, loss_target, m_meta, m_norm1_g, m_w_in, m_conv_w, m_ret_norm_g, m_w_out, m_final_g, v_meta, v_norm1_g, v_w_in, v_conv_w, v_ret_norm_g, v_w_out, v_final_g):
    given = dict(x=x, meta=meta, norm1_g=norm1_g, w_in=w_in, conv_w=conv_w, ret_norm_g=ret_norm_g, w_out=w_out, final_g=final_g, loss_target=loss_target, m_meta=m_meta, m_norm1_g=m_norm1_g, m_w_in=m_w_in, m_conv_w=m_conv_w, m_ret_norm_g=m_ret_norm_g, m_w_out=m_w_out, m_final_g=m_final_g, v_meta=v_meta, v_norm1_g=v_norm1_g, v_w_in=v_w_in, v_conv_w=v_conv_w, v_ret_norm_g=v_ret_norm_g, v_w_out=v_w_out, v_final_g=v_final_g)
    weights = {n: given[n] for n in TWIN_WEIGHTS}
    shared = {n: given[n] for n in SHARED_INPUTS}
    per_example = {n: given[n] for n in ['x']}
    grad_fn = _jax.value_and_grad(_loss, argnums=(0, 1))

    def one_microbatch(ex, loss_target):
        ex = dict(ex)
        diff = ex.pop(TWIN_DIFF_INPUT)
        return grad_fn(weights, diff, {**shared, **ex}, loss_target)

    if N_MICROBATCH == 1:
        loss, (grad_w, grad_x) = one_microbatch(per_example, given["loss_target"])
    else:
        def body(carry, xs):
            loss_sum, grad_sum = carry
            l_k, (gw_k, gx_k) = one_microbatch(xs[0], xs[1])
            with _jax.named_scope("update"):
                return (loss_sum + l_k, _jax.tree.map(_jnp.add, grad_sum, gw_k)), gx_k

        init = (_jnp.zeros((), _jnp.float32), _jax.tree.map(_jnp.zeros_like, weights))
        (loss, grad_w), grad_x = _jax.lax.scan(body, init, (per_example, given["loss_target"]))
    with _jax.named_scope("update"):
        delta_w, new_m, new_v = {}, {}, {}
        for n in TWIN_WEIGHTS:
            delta_w[n], new_m[n], new_v[n] = _adamw(weights[n], grad_w[n], given["m_" + n], given["v_" + n])
    return (loss, grad_x, *[grad_w[n] for n in TWIN_WEIGHTS], *[delta_w[n] for n in TWIN_WEIGHTS],
            *[new_m[n] for n in TWIN_WEIGHTS], *[new_v[n] for n in TWIN_WEIGHTS])
```

```python
import functools

import jax
import jax.numpy as jnp
from jax import lax
from jax.experimental import pallas as pl
from jax.experimental.pallas import tpu as pltpu

F32 = jnp.float32
BF16 = jnp.bfloat16

D_MODEL = 1024
N_META = 16
D_CONV = 512
D_RET = 512
RET_HEADS = 4
HEAD_DIM = 128
CHUNK = 128
N_PROJ_COLS = 4096
ROPE_BASE = 10000.0
EPS = 1e-6
N_CHIPS = 4
N_DEV = 8

ADAM_LR = 0.001
ADAM_B1 = 0.9
ADAM_B2 = 0.999
ADAM_EPS = 1e-08
ADAM_WD = 0.01
ADAM_STEP = 10

TM = 512
NCH = TM // CHUNK
VMEM_LIMIT = 56 * 1024 * 1024

CX, CB, CC, CG, CQ, CK, CV, CR = (i * 512 for i in range(8))

MESH_ID = pl.DeviceIdType.MESH


def _cparams(sem=None, **kw):
    return pltpu.CompilerParams(dimension_semantics=sem, vmem_limit_bytes=VMEM_LIMIT, **kw)


def _sigmoid(x):
    return 1.0 / (1.0 + jnp.exp(-x))


def _dot(a, b):
    return jnp.dot(a, b, preferred_element_type=F32)


def _dot_tb(a, b):
    return lax.dot_general(a, b, (((1,), (1,)), ((), ())), preferred_element_type=F32)


def _dot_ta(a, b):
    return lax.dot_general(a, b, (((0,), (0,)), ((), ())), preferred_element_type=F32)


def _resident(shape):
    nd = len(shape)
    return pl.BlockSpec(shape, lambda *_: (0,) * nd)


def _tables(seq):
    nt = seq // TM
    rows = seq + TM
    r = jnp.arange(rows, dtype=jnp.int32)
    pos = jnp.where(r < seq, r + N_META, jnp.maximum(r - (seq + TM - N_META), 0))
    half = HEAD_DIM // 2
    freqs = 1.0 / (ROPE_BASE ** (jnp.arange(half, dtype=F32) / half))
    ang = pos.astype(F32)[:, None] * freqs[None, :]
    cos, sin = jnp.cos(ang), jnp.sin(ang)
    rc = jnp.concatenate([cos, cos], axis=-1)
    rs = jnp.concatenate([-sin, sin], axis=-1)
    log_g = jnp.log(1.0 - 2.0 ** (-5.0 - jnp.arange(RET_HEADS, dtype=F32)))
    idx = jnp.arange(CHUNK, dtype=F32)
    diff = idx[:, None] - idx[None, :]
    decay = jnp.where(diff[None] >= 0, jnp.exp(diff[None] * log_g[:, None, None]), 0.0)
    zeta = jnp.exp((CHUNK - 1 - idx)[None, :] * log_g[:, None])
    xi = jnp.exp((idx + 1.0)[None, :] * log_g[:, None])
    cd = jnp.exp(CHUNK * log_g)
    zeta_b = jnp.broadcast_to(zeta[:, :, None], (RET_HEADS, CHUNK, HEAD_DIM))
    xi_b = jnp.broadcast_to(xi[:, :, None], (RET_HEADS, CHUNK, HEAD_DIM))
    cd_b = jnp.broadcast_to(cd[:, None, None], (RET_HEADS, 8, HEAD_DIM))
    return dict(nt=nt, rows=rows, rc=rc, rs=rs, decay=decay, zeta=zeta_b, xi=xi_b, cd=cd_b)


def _rot(t, rc, rs):
    return t * rc + pltpu.roll(t, HEAD_DIM // 2, 1) * rs


def _rot_t(dt, rc, rs):
    return dt * rc + pltpu.roll(dt * rs, HEAD_DIM // 2, 1)


def _f1_call(x, meta_tile, g1, w_in_g, nt):
    rows = (nt + 1) * TM

    def body(x_ref, mt_ref, g_ref, w_ref, proj_ref, hnt_ref):
        i = pl.program_id(0)
        h = jnp.where(i == nt, mt_ref[...], x_ref[...])
        ms = jnp.mean(h * h, axis=-1, keepdims=True)
        hn = (h * lax.rsqrt(ms + EPS)) * g_ref[...]
        hb = hn.astype(BF16)
        for j in range(N_CHIPS):
            proj_ref[:, j * 1024:(j + 1) * 1024] = _dot(hb, w_ref[j]).astype(BF16)
        hnt_ref[...] = hn.T.astype(BF16)

    return pl.pallas_call(
        body, name="f1_norm_inproj",
        grid=(nt + 1,),
        in_specs=[pl.BlockSpec((TM, D_MODEL), lambda i: (jnp.minimum(i, nt - 1), 0)),
                  _resident((TM, D_MODEL)),
                  _resident((1, D_MODEL)),
                  _resident((N_CHIPS, D_MODEL, 1024))],
        out_specs=[pl.BlockSpec((TM, N_PROJ_COLS), lambda i: (i, 0)),
                   pl.BlockSpec((D_MODEL, TM), lambda i: (0, i))],
        out_shape=[jax.ShapeDtypeStruct((rows, N_PROJ_COLS), BF16),
                   jax.ShapeDtypeStruct((D_MODEL, rows), BF16)],
        compiler_params=_cparams(("arbitrary",)),
    )(x, meta_tile, g1, w_in_g)


def _f2_call(proj, conv_w8, gret, tb):
    nt, rows = tb["nt"], tb["rows"]

    def phys(s):
        return jnp.where(s == 0, nt, s - 1)

    def body(proj_ref, cw_ref, g_ref, rc_ref, rs_ref, dec_ref, xi_ref, zeta_ref, cd_ref,
             mixed_ref, conv_ref, states_ref, state, uhalo):
        s = pl.program_id(0)

        @pl.when(s == 0)
        def _():
            state[...] = jnp.zeros_like(state)
            uhalo[...] = jnp.zeros_like(uhalo)

        cx = proj_ref[:, CX:CX + 512].astype(F32)
        cc = proj_ref[:, CC:CC + 512].astype(F32)
        u = cc * cx
        row = lax.broadcasted_iota(jnp.int32, (TM, D_CONV), 0)
        h7 = uhalo[7:8, :]
        h6 = uhalo[6:7, :]
        u1 = jnp.where(row == 0, h7, pltpu.roll(u, 1, 0))
        u2 = jnp.where(row == 0, h6, jnp.where(row == 1, h7, pltpu.roll(u, 2, 0)))
        conv = cw_ref[2:3, :] * u + cw_ref[1:2, :] * u1 + cw_ref[0:1, :] * u2
        uhalo[...] = u[TM - 8:TM, :]
        cb = proj_ref[:, CB:CB + 512].astype(F32)
        cg = proj_ref[:, CG:CG + 512].astype(F32)
        mixed_ref[:, 0:D_CONV] = (cb * conv * (cg * _sigmoid(cg))).astype(BF16)
        conv_ref[...] = conv.astype(BF16)

        scale = HEAD_DIM ** -0.5
        for c in range(NCH):
            r0 = c * CHUNK
            rc = rc_ref[r0:r0 + CHUNK, :]
            rs = rs_ref[r0:r0 + CHUNK, :]
            for h in range(RET_HEADS):
                lo = h * HEAD_DIM
                q = proj_ref[r0:r0 + CHUNK, CQ + lo:CQ + lo + HEAD_DIM].astype(F32)
                k = proj_ref[r0:r0 + CHUNK, CK + lo:CK + lo + HEAD_DIM].astype(F32)
                v = proj_ref[r0:r0 + CHUNK, CV + lo:CV + lo + HEAD_DIM]
                rg = proj_ref[r0:r0 + CHUNK, CR + lo:CR + lo + HEAD_DIM].astype(F32)
                qf = _rot(q, rc, rs) * scale
                kf = _rot(k, rc, rs)
                st = state[h]
                stb = st.astype(BF16)
                states_ref[c, h] = stb
                a = _dot_tb(qf.astype(BF16), kf.astype(BF16)) * dec_ref[h]
                o = _dot(a.astype(BF16), v) + _dot((qf * xi_ref[h]).astype(BF16), stb)
                upd = _dot_ta((kf * zeta_ref[h]).astype(BF16), v)
                state[h] = cd_ref[h, 0:1, :] * st + upd
                mu = jnp.mean(o, axis=-1, keepdims=True)
                d = o - mu
                var = jnp.mean(d * d, axis=-1, keepdims=True)
                yh = d * lax.rsqrt(var + EPS)
                ret = yh * g_ref[:, lo:lo + HEAD_DIM]
                mixed_ref[r0:r0 + CHUNK, D_CONV + lo:D_CONV + lo + HEAD_DIM] = (
                    ret * (rg * _sigmoid(rg))).astype(BF16)

    tile = lambda w: pl.BlockSpec((TM, w), lambda s: (phys(s), 0))
    return pl.pallas_call(
        body, name="f2_mixer_fwd",
        grid=(nt + 1,),
        in_specs=[tile(N_PROJ_COLS), _resident((8, D_CONV)), _resident((1, D_RET)),
                  tile(HEAD_DIM), tile(HEAD_DIM),
                  _resident((RET_HEADS, CHUNK, CHUNK)), _resident((RET_HEADS, CHUNK, HEAD_DIM)),
                  _resident((RET_HEADS, CHUNK, HEAD_DIM)), _resident((RET_HEADS, 8, HEAD_DIM))],
        out_specs=[tile(D_MODEL), tile(D_CONV),
                   pl.BlockSpec((NCH, RET_HEADS, HEAD_DIM, HEAD_DIM), lambda s: (phys(s), 0, 0, 0))],
        out_shape=[jax.ShapeDtypeStruct((rows, D_MODEL), BF16),
                   jax.ShapeDtypeStruct((rows, D_CONV), BF16),
                   jax.ShapeDtypeStruct(((nt + 1) * NCH, RET_HEADS, HEAD_DIM, HEAD_DIM), BF16)],
        scratch_shapes=[pltpu.VMEM((RET_HEADS, HEAD_DIM, HEAD_DIM), F32), pltpu.VMEM((8, D_CONV), F32)],
        compiler_params=_cparams(("arbitrary",)),
    )(proj, conv_w8, gret, tb["rc"], tb["rs"], tb["decay"], tb["xi"], tb["zeta"], tb["cd"])


def _f3_call(x, mixed, w_out, fg, target, nt):
    seq = nt * TM

    def body(x_ref, mx_ref, w_ref, g_ref, t_ref, dh2_ref, dmx_ref, gwo_ref, gfg_ref, loss_ref, lacc):
        i = pl.program_id(0)

        @pl.when(i == 0)
        def _():
            gwo_ref[...] = jnp.zeros_like(gwo_ref)
            gfg_ref[...] = jnp.zeros_like(gfg_ref)
            lacc[...] = jnp.zeros_like(lacc)

        mx = mx_ref[...]
        h2 = x_ref[...] + _dot(mx, w_ref[...])
        ms = jnp.mean(h2 * h2, axis=-1, keepdims=True)
        rstd = lax.rsqrt(ms + EPS)
        yh = h2 * rstd
        g = g_ref[...]
        e = yh * g - t_ref[...]
        lacc[...] += jnp.sum(e * e, axis=0, keepdims=True)
        dy = e * (1.0 / D_MODEL)
        gfg_ref[...] += jnp.sum(dy * yh, axis=0, keepdims=True)
        dyh = dy * g
        dh2 = rstd * (dyh - yh * jnp.mean(dyh * yh, axis=-1, keepdims=True))
        dh2_ref[...] = dh2
        db = dh2.astype(BF16)
        dmx_ref[...] = _dot_tb(db, w_ref[...]).astype(BF16)
        gw = _dot_ta(mx, db)
        for j in range(N_CHIPS):
            for hf in range(2):
                r0 = j * 256 + hf * 128
                gwo_ref[hf, j] += gw[r0:r0 + 128, :]

        @pl.when(i == nt - 1)
        def _():
            tot = jnp.sum(lacc[...], axis=1, keepdims=True) * (0.5 / D_MODEL)
            loss_ref[...] = jnp.broadcast_to(tot, (1, 128))

    tile = lambda w: pl.BlockSpec((TM, w), lambda i: (i, 0))
    return pl.pallas_call(
        body, name="f3_outproj_loss",
        grid=(nt,),
        in_specs=[tile(D_MODEL), tile(D_MODEL), _resident((D_MODEL, D_MODEL)), _resident((1, D_MODEL)),
                  tile(D_MODEL)],
        out_specs=[tile(D_MODEL), tile(D_MODEL), _resident((2, N_CHIPS, 128, D_MODEL)), _resident((1, D_MODEL)),
                   _resident((1, 128))],
        out_shape=[jax.ShapeDtypeStruct((seq, D_MODEL), F32),
                   jax.ShapeDtypeStruct((seq, D_MODEL), BF16),
                   jax.ShapeDtypeStruct((2, N_CHIPS, 128, D_MODEL), F32),
                   jax.ShapeDtypeStruct((1, D_MODEL), F32),
                   jax.ShapeDtypeStruct((1, 128), F32)],
        scratch_shapes=[pltpu.VMEM((1, D_MODEL), F32)],
        compiler_params=_cparams(("arbitrary",)),
    )(x, mixed, w_out, fg, target)


def _b2_call(proj, dmixed, conv_s, states, conv_w8, gret, tb):
    nt, rows = tb["nt"], tb["rows"]

    def phys(r):
        return jnp.where(r == nt, nt, nt - 1 - r)

    def body(proj_ref, dmx_ref, conv_ref, states_ref, cw_ref, g_ref, rc_ref, rs_ref, dec_ref, xi_ref,
             zeta_ref, cd_ref, dproj_ref, gcw_ref, gg_ref, gstate, dchalo):
        r = pl.program_id(0)
        live = jnp.where(r == nt, 0.0, 1.0)

        @pl.when(r == 0)
        def _():
            gstate[...] = jnp.zeros_like(gstate)
            dchalo[...] = jnp.zeros_like(dchalo)
            gcw_ref[...] = jnp.zeros_like(gcw_ref)
            gg_ref[...] = jnp.zeros_like(gg_ref)

        cx = proj_ref[:, CX:CX + 512].astype(F32)
        cb = proj_ref[:, CB:CB + 512].astype(F32)
        cc = proj_ref[:, CC:CC + 512].astype(F32)
        cg = proj_ref[:, CG:CG + 512].astype(F32)
        dco = dmx_ref[:, 0:D_CONV].astype(F32) * live
        conv = conv_ref[...].astype(F32)
        sg = _sigmoid(cg)
        sil = cg * sg
        t = dco * conv
        dproj_ref[:, CB:CB + 512] = (t * sil).astype(BF16)
        dproj_ref[:, CG:CG + 512] = (t * cb * (sg * (1.0 + cg * (1.0 - sg)))).astype(BF16)
        dconv = dco * cb * sil
        row = lax.broadcasted_iota(jnp.int32, (TM, D_CONV), 0)
        n0 = dchalo[0:1, :]
        n1 = dchalo[1:2, :]
        dc1 = jnp.where(row == TM - 1, n0, pltpu.roll(dconv, TM - 1, 0))
        dc2 = jnp.where(row == TM - 2, n0, jnp.where(row == TM - 1, n1, pltpu.roll(dconv, TM - 2, 0)))
        dchalo[...] = dconv[0:8, :]
        du = cw_ref[2:3, :] * dconv + cw_ref[1:2, :] * dc1 + cw_ref[0:1, :] * dc2
        u = cc * cx
        gcw_ref[2:3, :] += jnp.sum(u * dconv, axis=0, keepdims=True)
        gcw_ref[1:2, :] += jnp.sum(u * dc1, axis=0, keepdims=True)
        gcw_ref[0:1, :] += jnp.sum(u * dc2, axis=0, keepdims=True)
        dproj_ref[:, CC:CC + 512] = (du * cx).astype(BF16)
        dproj_ref[:, CX:CX + 512] = (du * cc).astype(BF16)

        scale = HEAD_DIM ** -0.5
        for c in range(NCH - 1, -1, -1):
            r0 = c * CHUNK
            rc = rc_ref[r0:r0 + CHUNK, :]
            rs = rs_ref[r0:r0 + CHUNK, :]
            for h in range(RET_HEADS):
                lo = h * HEAD_DIM
                q = proj_ref[r0:r0 + CHUNK, CQ + lo:CQ + lo + HEAD_DIM].astype(F32)
                k = proj_ref[r0:r0 + CHUNK, CK + lo:CK + lo + HEAD_DIM].astype(F32)
                v = proj_ref[r0:r0 + CHUNK, CV + lo:CV + lo + HEAD_DIM]
                rg = proj_ref[r0:r0 + CHUNK, CR + lo:CR + lo + HEAD_DIM].astype(F32)
                dro = dmx_ref[r0:r0 + CHUNK, D_CONV + lo:D_CONV + lo + HEAD_DIM].astype(F32) * live
                dec = dec_ref[h]
                xi = xi_ref[h]
                zeta = zeta_ref[h]
                stb = states_ref[c, h]
                gs = gstate[h]
                gsb = gs.astype(BF16)
                qf = _rot(q, rc, rs) * scale
                kf = _rot(k, rc, rs)
                qb = qf.astype(BF16)
                kb = kf.astype(BF16)
                qxb = (qf * xi).astype(BF16)
                kzb = (kf * zeta).astype(BF16)
                ab = (_dot_tb(qb, kb) * dec).astype(BF16)
                o = _dot(ab, v) + _dot(qxb, stb)
                mu = jnp.mean(o, axis=-1, keepdims=True)
                d = o - mu
                var = jnp.mean(d * d, axis=-1, keepdims=True)
                rstd = lax.rsqrt(var + EPS)
                yh = d * rstd
                g = g_ref[:, lo:lo + HEAD_DIM]
                sg = _sigmoid(rg)
                sil = rg * sg
                dproj_ref[r0:r0 + CHUNK, CR + lo:CR + lo + HEAD_DIM] = (
                    dro * (yh * g) * (sg * (1.0 + rg * (1.0 - sg)))).astype(BF16)
                dret = dro * sil
                gg_ref[:, lo:lo + HEAD_DIM] += jnp.sum(dret * yh, axis=0, keepdims=True)
                dyh = dret * g
                do = rstd * (dyh - jnp.mean(dyh, axis=-1, keepdims=True)
                             - yh * jnp.mean(dyh * yh, axis=-1, keepdims=True))
                dob = do.astype(BF16)
                dv = _dot_ta(ab, dob) + _dot(kzb, gsb)
                ds = (_dot_tb(dob, v) * dec).astype(BF16)
                dqf = _dot(ds, kb) + _dot_tb(dob, stb) * xi
                dkf = _dot_ta(ds, qb) + _dot_tb(v, gsb) * zeta
                gstate[h] = cd_ref[h, 0:1, :] * gs + _dot_ta(qxb, dob)
                dproj_ref[r0:r0 + CHUNK, CQ + lo:CQ + lo + HEAD_DIM] = (
                    _rot_t(dqf, rc, rs) * scale).astype(BF16)
                dproj_ref[r0:r0 + CHUNK, CK + lo:CK + lo + HEAD_DIM] = _rot_t(dkf, rc, rs).astype(BF16)
                dproj_ref[r0:r0 + CHUNK, CV + lo:CV + lo + HEAD_DIM] = dv.astype(BF16)

    tile = lambda w: pl.BlockSpec((TM, w), lambda r: (phys(r), 0))
    return pl.pallas_call(
        body, name="b2_mixer_bwd",
        grid=(nt + 1,),
        in_specs=[tile(N_PROJ_COLS),
                  pl.BlockSpec((TM, D_MODEL), lambda r: (jnp.minimum(phys(r), nt - 1), 0)),
                  tile(D_CONV),
                  pl.BlockSpec((NCH, RET_HEADS, HEAD_DIM, HEAD_DIM), lambda r: (phys(r), 0, 0, 0)),
                  _resident((8, D_CONV)), _resident((1, D_RET)),
                  tile(HEAD_DIM), tile(HEAD_DIM),
                  _resident((RET_HEADS, CHUNK, CHUNK)), _resident((RET_HEADS, CHUNK, HEAD_DIM)),
                  _resident((RET_HEADS, CHUNK, HEAD_DIM)), _resident((RET_HEADS, 8, HEAD_DIM))],
        out_specs=[tile(N_PROJ_COLS), _resident((8, D_CONV)), _resident((1, D_RET))],
        out_shape=[jax.ShapeDtypeStruct((rows, N_PROJ_COLS), BF16),
                   jax.ShapeDtypeStruct((8, D_CONV), F32),
                   jax.ShapeDtypeStruct((1, D_RET), F32)],
        scratch_shapes=[pltpu.VMEM((RET_HEADS, HEAD_DIM, HEAD_DIM), F32), pltpu.VMEM((8, D_CONV), F32)],
        compiler_params=_cparams(("arbitrary",)),
    )(proj, dmixed, conv_s, states, conv_w8, gret, tb["rc"], tb["rs"], tb["decay"], tb["xi"], tb["zeta"],
      tb["cd"])


def _b1a_call(dproj, w_in_g, x, meta_tile, g1, dh2, nt):
    seq = nt * TM

    def dphys(i):
        return jnp.where(i == 0, nt, i - 1)

    def xblk(i):
        return jnp.maximum(i - 1, 0)

    def body(dp_ref, w_ref, x_ref, mt_ref, g_ref, dh2_ref, gx_ref, dmeta_ref, gn_ref):
        i = pl.program_id(0)
        is_meta = i == 0

        @pl.when(is_meta)
        def _():
            gn_ref[...] = jnp.zeros_like(gn_ref)

        dhn = _dot_tb(dp_ref[:, 0:1024], w_ref[0])
        for j in range(1, N_CHIPS):
            dhn += _dot_tb(dp_ref[:, j * 1024:(j + 1) * 1024], w_ref[j])
        h = jnp.where(is_meta, mt_ref[...], x_ref[...])
        ms = jnp.mean(h * h, axis=-1, keepdims=True)
        rstd = lax.rsqrt(ms + EPS)
        xh = h * rstd
        gn_ref[...] += jnp.sum(dhn * xh, axis=0, keepdims=True)
        dxh = dhn * g_ref[...]
        dh = rstd * (dxh - xh * jnp.mean(dxh * xh, axis=-1, keepdims=True))
        gx_ref[...] = dh + jnp.where(is_meta, 0.0, 1.0) * dh2_ref[...]

        @pl.when(is_meta)
        def _():
            dmeta_ref[...] = dh[TM - N_META:TM, :]

    return pl.pallas_call(
        body, name="b1a_inproj_bwd_x",
        grid=(nt + 1,),
        in_specs=[pl.BlockSpec((TM, N_PROJ_COLS), lambda i: (dphys(i), 0)),
                  _resident((N_CHIPS, D_MODEL, 1024)),
                  pl.BlockSpec((TM, D_MODEL), lambda i: (xblk(i), 0)),
                  _resident((TM, D_MODEL)),
                  _resident((1, D_MODEL)),
                  pl.BlockSpec((TM, D_MODEL), lambda i: (xblk(i), 0))],
        out_specs=[pl.BlockSpec((TM, D_MODEL), lambda i: (xblk(i), 0)),
                   _resident((N_META, D_MODEL)), _resident((1, D_MODEL))],
        out_shape=[jax.ShapeDtypeStruct((seq, D_MODEL), F32),
                   jax.ShapeDtypeStruct((N_META, D_MODEL), F32),
                   jax.ShapeDtypeStruct((1, D_MODEL), F32)],
        compiler_params=_cparams(("arbitrary",)),
    )(dproj, w_in_g, x, meta_tile, g1, dh2)


def _b1b_call(hnt, dproj, nt):
    def body(a_ref, b_ref, o_ref):
        k = pl.program_id(1)

        @pl.when(k == 0)
        def _():
            o_ref[...] = jnp.zeros_like(o_ref)

        o_ref[0] += _dot(a_ref[0:512, :], b_ref[...])
        o_ref[1] += _dot(a_ref[512:1024, :], b_ref[...])

    return pl.pallas_call(
        body, name="b1b_inproj_bwd_w",
        grid=(N_CHIPS, nt + 1),
        in_specs=[pl.BlockSpec((D_MODEL, TM), lambda j, k: (0, k)),
                  pl.BlockSpec((TM, 1024), lambda j, k: (k, j))],
        out_specs=pl.BlockSpec((2, None, 512, 1024), lambda j, k: (0, j, 0, 0)),
        out_shape=jax.ShapeDtypeStruct((2, N_CHIPS, 512, 1024), F32),
        compiler_params=_cparams(("arbitrary", "arbitrary")),
    )(hnt, dproj)


def _local_step(x, target, meta, g1, w_in_g, conv_w, gret, w_out, fg):
    seq = x.shape[0]
    tb = _tables(seq)
    nt = tb["nt"]
    meta_tile = jnp.concatenate([jnp.zeros((TM - N_META, D_MODEL), F32), meta], axis=0)
    conv_w8 = jnp.concatenate([conv_w, jnp.zeros((5, D_CONV), F32)], axis=0)
    g1r, gretr, fgr = g1.reshape(1, -1), gret.reshape(1, -1), fg.reshape(1, -1)

    proj, hnt = _f1_call(x, meta_tile, g1r, w_in_g, nt)
    mixed, conv_s, states = _f2_call(proj, conv_w8, gretr, tb)
    dh2, dmixed, g_wout, g_fg, loss = _f3_call(x, mixed, w_out, fgr, target, nt)
    dproj, g_cw8, g_gret = _b2_call(proj, dmixed, conv_s, states, conv_w8, gretr, tb)
    grad_x, g_meta, g_g1 = _b1a_call(dproj, w_in_g, x, meta_tile, g1r, dh2, nt)
    g_win = _b1b_call(hnt, dproj, nt)
    return loss, grad_x, dict(w_in=g_win, w_out=g_wout, meta=g_meta, conv_w=g_cw8[0:3], norm1_g=g_g1,
                              ret_norm_g=g_gret, final_g=g_fg)


VMEM_SPEC = pl.BlockSpec(memory_space=pltpu.VMEM)
SMALL_ROWS = 24


def _place():
    x, y, c = lax.axis_index("x"), lax.axis_index("y"), lax.axis_index("c")
    chips = [(1 - x, y), (x, 1 - y), (1 - x, 1 - y)]
    return x, y, c, 2 * x + y, (x, y, 1 - c), chips


def _rcopy(src, dst, send_sems, recv_sems, k, to):
    return pltpu.make_async_remote_copy(src_ref=src, dst_ref=dst, send_sem=send_sems.at[k],
                                        recv_sem=recv_sems.at[k], device_id=to, device_id_type=MESH_ID)


def _gather_weights_call(win_sh, wout_sh, small_sh):
    def body(win_ref, wout_ref, sm_ref, wg_ref, wog_ref, smg_ref, send_sems, recv_sems):
        x, y, c, me, sib, chips = _place()
        rc = functools.partial(_rcopy, send_sems=send_sems, recv_sems=recv_sems)
        hw, ho = pl.ds(c * 512, 512), pl.ds(c * 128, 128)
        hw2, ho2 = pl.ds((1 - c) * 512, 512), pl.ds((1 - c) * 128, 128)
        first = []
        for j, (kx, ky) in enumerate(chips):
            to = (kx, ky, c)
            first.append(rc(win_ref.at[hw], wg_ref.at[me, hw], k=j, to=to))
            first.append(rc(wout_ref.at[ho], wog_ref.at[me, ho], k=3 + j, to=to))
            first.append(rc(sm_ref, smg_ref.at[me], k=6 + j, to=to))
        for cp in first:
            cp.start()
        wg_ref[me] = win_ref[...]
        wog_ref[me] = wout_ref[...]
        smg_ref[me] = sm_ref[...]
        passed = []
        for j, (kx, ky) in enumerate(chips):
            kid = 2 * kx + ky
            rc(win_ref.at[hw], wg_ref.at[kid, hw], k=j, to=sib).wait_recv()
            f = rc(wg_ref.at[kid, hw], wg_ref.at[kid, hw], k=9 + j, to=sib)
            f.start()
            passed.append(f)
            rc(wout_ref.at[ho], wog_ref.at[kid, ho], k=3 + j, to=sib).wait_recv()
            f = rc(wog_ref.at[kid, ho], wog_ref.at[kid, ho], k=12 + j, to=sib)
            f.start()
            passed.append(f)
        for j, (kx, ky) in enumerate(chips):
            kid = 2 * kx + ky
            rc(sm_ref, smg_ref.at[kid], k=6 + j, to=sib).wait_recv()
            rc(wg_ref.at[kid, hw2], wg_ref.at[kid, hw2], k=9 + j, to=sib).wait_recv()
            rc(wog_ref.at[kid, ho2], wog_ref.at[kid, ho2], k=12 + j, to=sib).wait_recv()
        for cp in first + passed:
            cp.wait_send()

    return pl.pallas_call(
        body, name="gather_weights",
        in_specs=[VMEM_SPEC] * 3, out_specs=[VMEM_SPEC] * 3,
        out_shape=[jax.ShapeDtypeStruct((N_CHIPS, D_MODEL, 1024), BF16),
                   jax.ShapeDtypeStruct((N_CHIPS, 256, D_MODEL), BF16),
                   jax.ShapeDtypeStruct((N_CHIPS, SMALL_ROWS, 256), F32)],
        scratch_shapes=[pltpu.SemaphoreType.DMA((15,)), pltpu.SemaphoreType.DMA((15,))],
        compiler_params=_cparams(),
    )(win_sh, wout_sh, small_sh)


def _reduce_scatter_call(g, name):
    _, _, rows, cols = g.shape

    def body(g_ref, out_ref, a_ref, b_ref, send_sems, recv_sems):
        x, y, c, me, sib, chips = _place()
        rc = functools.partial(_rcopy, send_sems=send_sems, recv_sems=recv_sems)
        d1 = rc(g_ref.at[1 - c], a_ref, k=0, to=sib)
        d1.start()
        d1.wait_recv()
        for j in range(N_CHIPS):
            a_ref[j] = a_ref[j] + g_ref[c, j]
        sends = []
        for j, (kx, ky) in enumerate(chips):
            s = rc(a_ref.at[2 * kx + ky], b_ref.at[me], k=1 + j, to=(kx, ky, c))
            s.start()
            sends.append(s)
        b_ref[me] = a_ref[me]
        for j, (kx, ky) in enumerate(chips):
            kid = 2 * kx + ky
            rc(a_ref.at[kid], b_ref.at[kid], k=1 + j, to=sib).wait_recv()
        out_ref[c] = ((b_ref[0] + b_ref[1]) + b_ref[2]) + b_ref[3]
        d5 = rc(out_ref.at[c], out_ref.at[c], k=4, to=sib)
        d5.start()
        rc(out_ref.at[1 - c], out_ref.at[1 - c], k=4, to=sib).wait_recv()
        for cp in [d1] + sends + [d5]:
            cp.wait_send()

    return pl.pallas_call(
        body, name=name,
        in_specs=[VMEM_SPEC], out_specs=VMEM_SPEC,
        out_shape=jax.ShapeDtypeStruct((2, rows, cols), F32),
        scratch_shapes=[pltpu.VMEM((N_CHIPS, rows, cols), F32), pltpu.VMEM((N_CHIPS, rows, cols), F32),
                        pltpu.SemaphoreType.DMA((5,)), pltpu.SemaphoreType.DMA((5,))],
        compiler_params=_cparams(),
    )(g)


def _allreduce_small_call(pack):
    def body(p_ref, out_ref, slot_ref, send_sems, recv_sems):
        x, y, c = lax.axis_index("x"), lax.axis_index("y"), lax.axis_index("c")
        myid = 4 * x + 2 * y + c
        rc = functools.partial(_rcopy, send_sems=send_sems, recv_sems=recv_sems)
        copies = []
        for r in range(1, N_DEV):
            peer = ((1 - x) if r & 4 else x, (1 - y) if r & 2 else y, (1 - c) if r & 1 else c)
            cp = rc(p_ref, slot_ref.at[r], k=r, to=peer)
            cp.start()
            copies.append(cp)
        slot_ref[0] = p_ref[...]
        for cp in copies:
            cp.wait_recv()
        acc = slot_ref[myid]
        for a in range(1, N_DEV):
            acc = acc + slot_ref[jnp.bitwise_xor(myid, a)]
        out_ref[...] = acc
        for cp in copies:
            cp.wait_send()

    return pl.pallas_call(
        body, name="allreduce_small",
        in_specs=[VMEM_SPEC], out_specs=VMEM_SPEC,
        out_shape=jax.ShapeDtypeStruct(pack.shape, F32),
        scratch_shapes=[pltpu.VMEM((N_DEV,) + pack.shape, F32),
                        pltpu.SemaphoreType.DMA((N_DEV,)), pltpu.SemaphoreType.DMA((N_DEV,))],
        compiler_params=_cparams(),
    )(pack)


def _adamw_call(w, g, m, v, name):
    shape = w.shape
    w2, g2, m2, v2 = (a.reshape(-1, shape[-1]) for a in (w, g, m, v))
    rows, cols = w2.shape
    br = 256 if rows % 256 == 0 else rows

    def body(w_ref, g_ref, m_ref, v_ref, d_ref, nm_ref, nv_ref):
        gg = g_ref[...]
        nm = ADAM_B1 * m_ref[...] + (1.0 - ADAM_B1) * gg
        nv = ADAM_B2 * v_ref[...] + (1.0 - ADAM_B2) * (gg * gg)
        m_hat = nm / (1.0 - ADAM_B1 ** ADAM_STEP)
        v_hat = nv / (1.0 - ADAM_B2 ** ADAM_STEP)
        d_ref[...] = -ADAM_LR * (m_hat / (jnp.sqrt(v_hat) + ADAM_EPS) + ADAM_WD * w_ref[...])
        nm_ref[...] = nm
        nv_ref[...] = nv

    spec = pl.BlockSpec((br, cols), lambda i: (i, 0))
    outs = pl.pallas_call(
        body, name=name, grid=(rows // br,),
        in_specs=[spec] * 4, out_specs=[spec] * 3,
        out_shape=[jax.ShapeDtypeStruct((rows, cols), F32)] * 3,
        compiler_params=_cparams(("arbitrary",)),
    )(w2, g2, m2, v2)
    return tuple(o.reshape(shape) for o in outs)


def _pad_to(a, rows, cols):
    return jnp.pad(a, ((0, rows - a.shape[0]), (0, cols - a.shape[1])))


def kernel(x, meta, norm1_g, w_in, conv_w, ret_norm_g, w_out, final_g, loss_target, m_meta, m_norm1_g, m_w_in, m_conv_w, m_ret_norm_g, m_w_out, m_final_g, v_meta, v_norm1_g, v_w_in, v_conv_w, v_ret_norm_g, v_w_out, v_final_g):
    me = 2 * lax.axis_index("x") + lax.axis_index("y")

    small_sh = jnp.concatenate([meta, _pad_to(conv_w, 8, 256)], axis=0)
    w_in_g, w_out_g, small_g = _gather_weights_call(w_in.astype(BF16), w_out.astype(BF16), small_sh)
    meta_full = jnp.concatenate([small_g[j, 0:N_META, :] for j in range(N_CHIPS)], axis=1)
    conv_full = jnp.concatenate([small_g[j, N_META:N_META + 3, 0:128] for j in range(N_CHIPS)], axis=1)

    loss, grad_x, g = _local_step(x[0], loss_target[0], meta_full, norm1_g, w_in_g, conv_full, ret_norm_g,
                                  w_out_g.reshape(D_MODEL, D_MODEL), final_g)

    g_win = _reduce_scatter_call(g["w_in"], "reduce_scatter_w_in").reshape(D_MODEL, 1024)
    g_wout = _reduce_scatter_call(g["w_out"], "reduce_scatter_w_out").reshape(256, D_MODEL)
    vec = jnp.concatenate([g["norm1_g"], g["final_g"], _pad_to(g["ret_norm_g"], 1, D_MODEL),
                           _pad_to(g["conv_w"], 5, D_MODEL)], axis=0)
    tot = _allreduce_small_call(jnp.concatenate([g["meta"], vec], axis=0))
    g_meta = lax.dynamic_slice(tot, (0, me * 256), (N_META, 256))
    g_conv = lax.dynamic_slice(tot, (N_META + 3, me * 128), (3, 128))
    g_n1, g_fg, g_rn = tot[N_META], tot[N_META + 1], tot[N_META + 2, 0:D_RET]

    loss_tot = lax.psum(loss[0, 0], ("x", "y", "c"))

    grads = [g_meta, g_n1, g_win, g_conv, g_rn, g_wout, g_fg]
    ws = [meta, norm1_g, w_in, conv_w, ret_norm_g, w_out, final_g]
    ms = [m_meta, m_norm1_g, m_w_in, m_conv_w, m_ret_norm_g, m_w_out, m_final_g]
    vs = [v_meta, v_norm1_g, v_w_in, v_conv_w, v_ret_norm_g, v_w_out, v_final_g]
    names = ["meta", "norm1_g", "w_in", "conv_w", "ret_norm_g", "w_out", "final_g"]
    deltas, new_ms, new_vs = [], [], []
    for w_, g_, m_, v_, n_ in zip(ws, grads, ms, vs, names):
        as2d = (lambda a: a.reshape(1, -1)) if w_.ndim == 1 else (lambda a: a)
        d_, nm_, nv_ = _adamw_call(as2d(w_), as2d(g_), as2d(m_), as2d(v_), "adamw_" + n_)
        deltas.append(d_.reshape(w_.shape))
        new_ms.append(nm_.reshape(w_.shape))
        new_vs.append(nv_.reshape(w_.shape))
    return (loss_tot, grad_x[None], *grads, *deltas, *new_ms, *new_vs)
```

```python
import functools

import jax
import jax.numpy as jnp
from jax import lax
from jax.experimental import pallas as pl
from jax.experimental.pallas import tpu as pltpu

F32 = jnp.float32
BF16 = jnp.bfloat16

D_MODEL = 1024
N_META = 16
D_CONV = 512
D_RET = 512
RET_HEADS = 4
HEAD_DIM = 128
CHUNK = 128
N_PROJ_COLS = 4096
ROPE_BASE = 10000.0
EPS = 1e-6
N_CHIPS = 4
N_DEV = 8

ADAM_LR = 0.001
ADAM_B1 = 0.9
ADAM_B2 = 0.999
ADAM_EPS = 1e-08
ADAM_WD = 0.01
ADAM_STEP = 10

TM = 512
NCH = TM // CHUNK
VMEM_LIMIT = 56 * 1024 * 1024

CX, CB, CC, CG, CQ, CK, CV, CR = (i * 512 for i in range(8))

MESH_ID = pl.DeviceIdType.MESH


def _cparams(sem=None, **kw):
    return pltpu.CompilerParams(dimension_semantics=sem, vmem_limit_bytes=VMEM_LIMIT, **kw)


def _sigmoid(x):
    return 1.0 / (1.0 + jnp.exp(-x))


def _dot(a, b):
    return jnp.dot(a, b, preferred_element_type=F32)


def _dot_tb(a, b):
    return lax.dot_general(a, b, (((1,), (1,)), ((), ())), preferred_element_type=F32)


def _dot_ta(a, b):
    return lax.dot_general(a, b, (((0,), (0,)), ((), ())), preferred_element_type=F32)


def _resident(shape):
    nd = len(shape)
    return pl.BlockSpec(shape, lambda *_: (0,) * nd)


def _tables(seq):
    nt = seq // TM
    rows = seq + TM
    r = jnp.arange(rows, dtype=jnp.int32)
    pos = jnp.where(r < seq, r + N_META, jnp.maximum(r - (seq + TM - N_META), 0))
    half = HEAD_DIM // 2
    freqs = 1.0 / (ROPE_BASE ** (jnp.arange(half, dtype=F32) / half))
    ang = pos.astype(F32)[:, None] * freqs[None, :]
    cos, sin = jnp.cos(ang), jnp.sin(ang)
    rc = jnp.concatenate([cos, cos], axis=-1)
    rs = jnp.concatenate([-sin, sin], axis=-1)
    log_g = jnp.log(1.0 - 2.0 ** (-5.0 - jnp.arange(RET_HEADS, dtype=F32)))
    idx = jnp.arange(CHUNK, dtype=F32)
    diff = idx[:, None] - idx[None, :]
    decay = jnp.where(diff[None] >= 0, jnp.exp(diff[None] * log_g[:, None, None]), 0.0)
    zeta = jnp.exp((CHUNK - 1 - idx)[None, :] * log_g[:, None])
    xi = jnp.exp((idx + 1.0)[None, :] * log_g[:, None])
    cd = jnp.exp(CHUNK * log_g)
    zeta_b = jnp.broadcast_to(zeta[:, :, None], (RET_HEADS, CHUNK, HEAD_DIM))
    xi_b = jnp.broadcast_to(xi[:, :, None], (RET_HEADS, CHUNK, HEAD_DIM))
    cd_b = jnp.broadcast_to(cd[:, None, None], (RET_HEADS, 8, HEAD_DIM))
    return dict(nt=nt, rows=rows, rc=rc, rs=rs, decay=decay, decay_t=jnp.swapaxes(decay, 1, 2), zeta=zeta_b,
                xi=xi_b, cd=cd_b)


def _rot(t, rc, rs):
    return t * rc + pltpu.roll(t, HEAD_DIM // 2, 1) * rs


def _rot_t(dt, rc, rs):
    return dt * rc + pltpu.roll(dt * rs, HEAD_DIM // 2, 1)


def _f1_call(x, meta_tile, g1, w_in_g, nt):
    rows = (nt + 1) * TM

    def body(x_ref, mt_ref, g_ref, w_ref, proj_ref, hnt_ref):
        i = pl.program_id(0)
        h = jnp.where(i == nt, mt_ref[...], x_ref[...])
        ms = jnp.mean(h * h, axis=-1, keepdims=True)
        hn = (h * lax.rsqrt(ms + EPS)) * g_ref[...]
        hb = hn.astype(BF16)
        for j in range(N_CHIPS):
            proj_ref[:, j * 1024:(j + 1) * 1024] = _dot(hb, w_ref[j]).astype(BF16)
        hnt_ref[...] = hn.T.astype(BF16)

    return pl.pallas_call(
        body, name="f1_norm_inproj",
        grid=(nt + 1,),
        in_specs=[pl.BlockSpec((TM, D_MODEL), lambda i: (jnp.minimum(i, nt - 1), 0)),
                  _resident((TM, D_MODEL)),
                  _resident((1, D_MODEL)),
                  _resident((N_CHIPS, D_MODEL, 1024))],
        out_specs=[pl.BlockSpec((TM, N_PROJ_COLS), lambda i: (i, 0)),
                   pl.BlockSpec((D_MODEL, TM), lambda i: (0, i))],
        out_shape=[jax.ShapeDtypeStruct((rows, N_PROJ_COLS), BF16),
                   jax.ShapeDtypeStruct((D_MODEL, rows), BF16)],
        compiler_params=_cparams(("arbitrary",)),
    )(x, meta_tile, g1, w_in_g)


def _f2_call(proj, conv_w8, gret, tb):
    nt, rows = tb["nt"], tb["rows"]

    def phys(s):
        return jnp.where(s == 0, nt, s - 1)

    def body(proj_ref, cw_ref, g_ref, rc_ref, rs_ref, dec_ref, xi_ref, zeta_ref, cd_ref,
             mixed_ref, conv_ref, states_ref, state, uhalo):
        s = pl.program_id(0)

        @pl.when(s == 0)
        def _():
            state[...] = jnp.zeros_like(state)
            uhalo[...] = jnp.zeros_like(uhalo)

        cx = proj_ref[:, CX:CX + 512].astype(F32)
        cc = proj_ref[:, CC:CC + 512].astype(F32)
        u = cc * cx
        row = lax.broadcasted_iota(jnp.int32, (TM, D_CONV), 0)
        h7 = uhalo[7:8, :]
        h6 = uhalo[6:7, :]
        u1 = jnp.where(row == 0, h7, pltpu.roll(u, 1, 0))
        u2 = jnp.where(row == 0, h6, jnp.where(row == 1, h7, pltpu.roll(u, 2, 0)))
        conv = cw_ref[2:3, :] * u + cw_ref[1:2, :] * u1 + cw_ref[0:1, :] * u2
        uhalo[...] = u[TM - 8:TM, :]
        cb = proj_ref[:, CB:CB + 512].astype(F32)
        cg = proj_ref[:, CG:CG + 512].astype(F32)
        mixed_ref[:, 0:D_CONV] = (cb * conv * (cg * _sigmoid(cg))).astype(BF16)
        conv_ref[...] = conv.astype(BF16)

        scale = HEAD_DIM ** -0.5
        H = range(RET_HEADS)
        st = [state[h] for h in H]
        for c in range(NCH):
            r0 = c * CHUNK
            rc = rc_ref[r0:r0 + CHUNK, :]
            rs = rs_ref[r0:r0 + CHUNK, :]
            col = lambda base, h: slice(base + h * HEAD_DIM, base + (h + 1) * HEAD_DIM)
            rws = slice(r0, r0 + CHUNK)
            v = [proj_ref[rws, col(CV, h)] for h in H]
            qf = [_rot(proj_ref[rws, col(CQ, h)].astype(F32), rc, rs) * scale for h in H]
            kf = [_rot(proj_ref[rws, col(CK, h)].astype(F32), rc, rs) for h in H]
            stb = [t.astype(BF16) for t in st]
            for h in H:
                states_ref[c, h] = stb[h]
            a = [(_dot_tb(qf[h].astype(BF16), kf[h].astype(BF16)) * dec_ref[h]).astype(BF16) for h in H]
            o = [_dot(a[h], v[h]) + _dot((qf[h] * xi_ref[h]).astype(BF16), stb[h]) for h in H]
            st = [cd_ref[h, 0:1, :] * st[h] + _dot_ta((kf[h] * zeta_ref[h]).astype(BF16), v[h]) for h in H]
            for h in H:
                mu = jnp.mean(o[h], axis=-1, keepdims=True)
                d = o[h] - mu
                var = jnp.mean(d * d, axis=-1, keepdims=True)
                yh = d * lax.rsqrt(var + EPS)
                rg = proj_ref[rws, col(CR, h)].astype(F32)
                mixed_ref[rws, col(D_CONV, h)] = (yh * g_ref[:, col(0, h)] * (rg * _sigmoid(rg))).astype(BF16)
        for h in H:
            state[h] = st[h]

    tile = lambda w: pl.BlockSpec((TM, w), lambda s: (phys(s), 0))
    return pl.pallas_call(
        body, name="f2_mixer_fwd",
        grid=(nt + 1,),
        in_specs=[tile(N_PROJ_COLS), _resident((8, D_CONV)), _resident((1, D_RET)),
                  tile(HEAD_DIM), tile(HEAD_DIM),
                  _resident((RET_HEADS, CHUNK, CHUNK)), _resident((RET_HEADS, CHUNK, HEAD_DIM)),
                  _resident((RET_HEADS, CHUNK, HEAD_DIM)), _resident((RET_HEADS, 8, HEAD_DIM))],
        out_specs=[tile(D_MODEL), tile(D_CONV),
                   pl.BlockSpec((NCH, RET_HEADS, HEAD_DIM, HEAD_DIM), lambda s: (phys(s), 0, 0, 0))],
        out_shape=[jax.ShapeDtypeStruct((rows, D_MODEL), BF16),
                   jax.ShapeDtypeStruct((rows, D_CONV), BF16),
                   jax.ShapeDtypeStruct(((nt + 1) * NCH, RET_HEADS, HEAD_DIM, HEAD_DIM), BF16)],
        scratch_shapes=[pltpu.VMEM((RET_HEADS, HEAD_DIM, HEAD_DIM), F32), pltpu.VMEM((8, D_CONV), F32)],
        compiler_params=_cparams(("arbitrary",)),
    )(proj, conv_w8, gret, tb["rc"], tb["rs"], tb["decay"], tb["xi"], tb["zeta"], tb["cd"])


def _f3_call(x, mixed, w_out, fg, target, nt):
    seq = nt * TM

    def body(x_ref, mx_ref, w_ref, g_ref, t_ref, dh2_ref, dmx_ref, gwo_ref, gfg_ref, loss_ref, lacc):
        i = pl.program_id(0)

        @pl.when(i == 0)
        def _():
            gwo_ref[...] = jnp.zeros_like(gwo_ref)
            gfg_ref[...] = jnp.zeros_like(gfg_ref)
            lacc[...] = jnp.zeros_like(lacc)

        mx = mx_ref[...]
        h2 = x_ref[...] + _dot(mx, w_ref[...])
        ms = jnp.mean(h2 * h2, axis=-1, keepdims=True)
        rstd = lax.rsqrt(ms + EPS)
        yh = h2 * rstd
        g = g_ref[...]
        e = yh * g - t_ref[...]
        lacc[...] += jnp.sum(e * e, axis=0, keepdims=True)
        dy = e * (1.0 / D_MODEL)
        gfg_ref[...] += jnp.sum(dy * yh, axis=0, keepdims=True)
        dyh = dy * g
        dh2 = rstd * (dyh - yh * jnp.mean(dyh * yh, axis=-1, keepdims=True))
        dh2_ref[...] = dh2
        db = dh2.astype(BF16)
        dmx_ref[...] = _dot_tb(db, w_ref[...]).astype(BF16)
        gw = _dot_ta(mx, db)
        for j in range(N_CHIPS):
            for hf in range(2):
                r0 = j * 256 + hf * 128
                gwo_ref[hf, j] += gw[r0:r0 + 128, :]

        @pl.when(i == nt - 1)
        def _():
            tot = jnp.sum(lacc[...], axis=1, keepdims=True) * (0.5 / D_MODEL)
            loss_ref[...] = jnp.broadcast_to(tot, (1, 128))

    tile = lambda w: pl.BlockSpec((TM, w), lambda i: (i, 0))
    return pl.pallas_call(
        body, name="f3_outproj_loss",
        grid=(nt,),
        in_specs=[tile(D_MODEL), tile(D_MODEL), _resident((D_MODEL, D_MODEL)), _resident((1, D_MODEL)),
                  tile(D_MODEL)],
        out_specs=[tile(D_MODEL), tile(D_MODEL), _resident((2, N_CHIPS, 128, D_MODEL)), _resident((1, D_MODEL)),
                   _resident((1, 128))],
        out_shape=[jax.ShapeDtypeStruct((seq, D_MODEL), F32),
                   jax.ShapeDtypeStruct((seq, D_MODEL), BF16),
                   jax.ShapeDtypeStruct((2, N_CHIPS, 128, D_MODEL), F32),
                   jax.ShapeDtypeStruct((1, D_MODEL), F32),
                   jax.ShapeDtypeStruct((1, 128), F32)],
        scratch_shapes=[pltpu.VMEM((1, D_MODEL), F32)],
        compiler_params=_cparams(("arbitrary",)),
    )(x, mixed, w_out, fg, target)


def _b2_call(proj, dmixed, conv_s, states, conv_w8, gret, tb):
    nt, rows = tb["nt"], tb["rows"]

    def phys(r):
        return jnp.where(r == nt, nt, nt - 1 - r)

    def body(proj_ref, dmx_ref, conv_ref, states_ref, cw_ref, g_ref, rc_ref, rs_ref, dec_ref, dect_ref, xi_ref,
             zeta_ref, cd_ref, dproj_ref, gcw_ref, gg_ref, gstate, dchalo):
        r = pl.program_id(0)
        live = jnp.where(r == nt, 0.0, 1.0)

        @pl.when(r == 0)
        def _():
            gstate[...] = jnp.zeros_like(gstate)
            dchalo[...] = jnp.zeros_like(dchalo)
            gcw_ref[...] = jnp.zeros_like(gcw_ref)
            gg_ref[...] = jnp.zeros_like(gg_ref)

        cx = proj_ref[:, CX:CX + 512].astype(F32)
        cb = proj_ref[:, CB:CB + 512].astype(F32)
        cc = proj_ref[:, CC:CC + 512].astype(F32)
        cg = proj_ref[:, CG:CG + 512].astype(F32)
        dco = dmx_ref[:, 0:D_CONV].astype(F32) * live
        conv = conv_ref[...].astype(F32)
        sg = _sigmoid(cg)
        sil = cg * sg
        t = dco * conv
        dproj_ref[:, CB:CB + 512] = (t * sil).astype(BF16)
        dproj_ref[:, CG:CG + 512] = (t * cb * (sg * (1.0 + cg * (1.0 - sg)))).astype(BF16)
        dconv = dco * cb * sil
        row = lax.broadcasted_iota(jnp.int32, (TM, D_CONV), 0)
        n0 = dchalo[0:1, :]
        n1 = dchalo[1:2, :]
        dc1 = jnp.where(row == TM - 1, n0, pltpu.roll(dconv, TM - 1, 0))
        dc2 = jnp.where(row == TM - 2, n0, jnp.where(row == TM - 1, n1, pltpu.roll(dconv, TM - 2, 0)))
        dchalo[...] = dconv[0:8, :]
        du = cw_ref[2:3, :] * dconv + cw_ref[1:2, :] * dc1 + cw_ref[0:1, :] * dc2
        u = cc * cx
        gcw_ref[2:3, :] += jnp.sum(u * dconv, axis=0, keepdims=True)
        gcw_ref[1:2, :] += jnp.sum(u * dc1, axis=0, keepdims=True)
        gcw_ref[0:1, :] += jnp.sum(u * dc2, axis=0, keepdims=True)
        dproj_ref[:, CC:CC + 512] = (du * cx).astype(BF16)
        dproj_ref[:, CX:CX + 512] = (du * cc).astype(BF16)

        scale = HEAD_DIM ** -0.5
        H = range(RET_HEADS)
        gs = [gstate[h] for h in H]
        gg = [jnp.zeros((1, HEAD_DIM), F32) for _ in H]
        for c in range(NCH - 1, -1, -1):
            r0 = c * CHUNK
            rc = rc_ref[r0:r0 + CHUNK, :]
            rs = rs_ref[r0:r0 + CHUNK, :]
            col = lambda base, h: slice(base + h * HEAD_DIM, base + (h + 1) * HEAD_DIM)
            rws = slice(r0, r0 + CHUNK)
            v = [proj_ref[rws, col(CV, h)] for h in H]
            stb = [states_ref[c, h] for h in H]
            qf = [_rot(proj_ref[rws, col(CQ, h)].astype(F32), rc, rs) * scale for h in H]
            kf = [_rot(proj_ref[rws, col(CK, h)].astype(F32), rc, rs) for h in H]
            qb = [t.astype(BF16) for t in qf]
            kb = [t.astype(BF16) for t in kf]
            qxb = [(qf[h] * xi_ref[h]).astype(BF16) for h in H]
            kzb = [(kf[h] * zeta_ref[h]).astype(BF16) for h in H]
            gsb = [t.astype(BF16) for t in gs]
            ab = [(_dot_tb(qb[h], kb[h]) * dec_ref[h]).astype(BF16) for h in H]
            atb = [(_dot_tb(kb[h], qb[h]) * dect_ref[h]).astype(BF16) for h in H]
            o = [_dot(ab[h], v[h]) + _dot(qxb[h], stb[h]) for h in H]
            dob = []
            for h in H:
                mu = jnp.mean(o[h], axis=-1, keepdims=True)
                d = o[h] - mu
                var = jnp.mean(d * d, axis=-1, keepdims=True)
                rstd = lax.rsqrt(var + EPS)
                yh = d * rstd
                g = g_ref[:, col(0, h)]
                rg = proj_ref[rws, col(CR, h)].astype(F32)
                dro = dmx_ref[rws, col(D_CONV, h)].astype(F32) * live
                sg = _sigmoid(rg)
                dproj_ref[rws, col(CR, h)] = (dro * (yh * g) * (sg * (1.0 + rg * (1.0 - sg)))).astype(BF16)
                dret = dro * (rg * sg)
                gg[h] = gg[h] + jnp.sum(dret * yh, axis=0, keepdims=True)
                dyh = dret * g
                do = rstd * (dyh - jnp.mean(dyh, axis=-1, keepdims=True)
                             - yh * jnp.mean(dyh * yh, axis=-1, keepdims=True))
                dob.append(do.astype(BF16))
            dv = [_dot(atb[h], dob[h]) + _dot(kzb[h], gsb[h]) for h in H]
            ds = [(_dot_tb(dob[h], v[h]) * dec_ref[h]).astype(BF16) for h in H]
            dst = [(_dot_tb(v[h], dob[h]) * dect_ref[h]).astype(BF16) for h in H]
            dqf = [_dot(ds[h], kb[h]) + _dot_tb(dob[h], stb[h]) * xi_ref[h] for h in H]
            dkf = [_dot(dst[h], qb[h]) + _dot_tb(v[h], gsb[h]) * zeta_ref[h] for h in H]
            gs = [cd_ref[h, 0:1, :] * gs[h] + _dot_ta(qxb[h], dob[h]) for h in H]
            for h in H:
                dproj_ref[rws, col(CQ, h)] = (_rot_t(dqf[h], rc, rs) * scale).astype(BF16)
                dproj_ref[rws, col(CK, h)] = _rot_t(dkf[h], rc, rs).astype(BF16)
                dproj_ref[rws, col(CV, h)] = dv[h].astype(BF16)
        for h in H:
            gstate[h] = gs[h]
            gg_ref[:, h * HEAD_DIM:(h + 1) * HEAD_DIM] += gg[h]

    tile = lambda w: pl.BlockSpec((TM, w), lambda r: (phys(r), 0))
    return pl.pallas_call(
        body, name="b2_mixer_bwd",
        grid=(nt + 1,),
        in_specs=[tile(N_PROJ_COLS),
                  pl.BlockSpec((TM, D_MODEL), lambda r: (jnp.minimum(phys(r), nt - 1), 0)),
                  tile(D_CONV),
                  pl.BlockSpec((NCH, RET_HEADS, HEAD_DIM, HEAD_DIM), lambda r: (phys(r), 0, 0, 0)),
                  _resident((8, D_CONV)), _resident((1, D_RET)),
                  tile(HEAD_DIM), tile(HEAD_DIM),
                  _resident((RET_HEADS, CHUNK, CHUNK)), _resident((RET_HEADS, CHUNK, CHUNK)),
                  _resident((RET_HEADS, CHUNK, HEAD_DIM)),
                  _resident((RET_HEADS, CHUNK, HEAD_DIM)), _resident((RET_HEADS, 8, HEAD_DIM))],
        out_specs=[tile(N_PROJ_COLS), _resident((8, D_CONV)), _resident((1, D_RET))],
        out_shape=[jax.ShapeDtypeStruct((rows, N_PROJ_COLS), BF16),
                   jax.ShapeDtypeStruct((8, D_CONV), F32),
                   jax.ShapeDtypeStruct((1, D_RET), F32)],
        scratch_shapes=[pltpu.VMEM((RET_HEADS, HEAD_DIM, HEAD_DIM), F32), pltpu.VMEM((8, D_CONV), F32)],
        compiler_params=_cparams(("arbitrary",)),
    )(proj, dmixed, conv_s, states, conv_w8, gret, tb["rc"], tb["rs"], tb["decay"], tb["decay_t"], tb["xi"],
      tb["zeta"], tb["cd"])


def _b1a_call(dproj, w_in_g, x, meta_tile, g1, dh2, nt):
    seq = nt * TM

    def dphys(i):
        return jnp.where(i == 0, nt, i - 1)

    def xblk(i):
        return jnp.maximum(i - 1, 0)

    def body(dp_ref, w_ref, x_ref, mt_ref, g_ref, dh2_ref, gx_ref, dmeta_ref, gn_ref):
        i = pl.program_id(0)
        is_meta = i == 0

        @pl.when(is_meta)
        def _():
            gn_ref[...] = jnp.zeros_like(gn_ref)

        dhn = _dot_tb(dp_ref[:, 0:1024], w_ref[0])
        for j in range(1, N_CHIPS):
            dhn += _dot_tb(dp_ref[:, j * 1024:(j + 1) * 1024], w_ref[j])
        h = jnp.where(is_meta, mt_ref[...], x_ref[...])
        ms = jnp.mean(h * h, axis=-1, keepdims=True)
        rstd = lax.rsqrt(ms + EPS)
        xh = h * rstd
        gn_ref[...] += jnp.sum(dhn * xh, axis=0, keepdims=True)
        dxh = dhn * g_ref[...]
        dh = rstd * (dxh - xh * jnp.mean(dxh * xh, axis=-1, keepdims=True))
        gx_ref[...] = dh + jnp.where(is_meta, 0.0, 1.0) * dh2_ref[...]

        @pl.when(is_meta)
        def _():
            dmeta_ref[...] = dh[TM - N_META:TM, :]

    return pl.pallas_call(
        body, name="b1a_inproj_bwd_x",
        grid=(nt + 1,),
        in_specs=[pl.BlockSpec((TM, N_PROJ_COLS), lambda i: (dphys(i), 0)),
                  _resident((N_CHIPS, D_MODEL, 1024)),
                  pl.BlockSpec((TM, D_MODEL), lambda i: (xblk(i), 0)),
                  _resident((TM, D_MODEL)),
                  _resident((1, D_MODEL)),
                  pl.BlockSpec((TM, D_MODEL), lambda i: (xblk(i), 0))],
        out_specs=[pl.BlockSpec((TM, D_MODEL), lambda i: (xblk(i), 0)),
                   _resident((N_META, D_MODEL)), _resident((1, D_MODEL))],
        out_shape=[jax.ShapeDtypeStruct((seq, D_MODEL), F32),
                   jax.ShapeDtypeStruct((N_META, D_MODEL), F32),
                   jax.ShapeDtypeStruct((1, D_MODEL), F32)],
        compiler_params=_cparams(("arbitrary",)),
    )(dproj, w_in_g, x, meta_tile, g1, dh2)


def _b1b_call(hnt, dproj, nt):
    def body(a_ref, b_ref, o_ref):
        k = pl.program_id(1)

        @pl.when(k == 0)
        def _():
            o_ref[...] = jnp.zeros_like(o_ref)

        o_ref[0] += _dot(a_ref[0:512, :], b_ref[...])
        o_ref[1] += _dot(a_ref[512:1024, :], b_ref[...])

    return pl.pallas_call(
        body, name="b1b_inproj_bwd_w",
        grid=(N_CHIPS, nt + 1),
        in_specs=[pl.BlockSpec((D_MODEL, TM), lambda j, k: (0, k)),
                  pl.BlockSpec((TM, 1024), lambda j, k: (k, j))],
        out_specs=pl.BlockSpec((2, None, 512, 1024), lambda j, k: (0, j, 0, 0)),
        out_shape=jax.ShapeDtypeStruct((2, N_CHIPS, 512, 1024), F32),
        compiler_params=_cparams(("arbitrary", "arbitrary")),
    )(hnt, dproj)


def _local_step(x, target, meta, g1, w_in_g, conv_w, gret, w_out, fg):
    seq = x.shape[0]
    tb = _tables(seq)
    nt = tb["nt"]
    meta_tile = jnp.concatenate([jnp.zeros((TM - N_META, D_MODEL), F32), meta], axis=0)
    conv_w8 = jnp.concatenate([conv_w, jnp.zeros((5, D_CONV), F32)], axis=0)
    g1r, gretr, fgr = g1.reshape(1, -1), gret.reshape(1, -1), fg.reshape(1, -1)

    proj, hnt = _f1_call(x, meta_tile, g1r, w_in_g, nt)
    mixed, conv_s, states = _f2_call(proj, conv_w8, gretr, tb)
    dh2, dmixed, g_wout, g_fg, loss = _f3_call(x, mixed, w_out, fgr, target, nt)
    dproj, g_cw8, g_gret = _b2_call(proj, dmixed, conv_s, states, conv_w8, gretr, tb)
    grad_x, g_meta, g_g1 = _b1a_call(dproj, w_in_g, x, meta_tile, g1r, dh2, nt)
    g_win = _b1b_call(hnt, dproj, nt)
    return loss, grad_x, dict(w_in=g_win, w_out=g_wout, meta=g_meta, conv_w=g_cw8[0:3], norm1_g=g_g1,
                              ret_norm_g=g_gret, final_g=g_fg)


VMEM_SPEC = pl.BlockSpec(memory_space=pltpu.VMEM)
SMALL_ROWS = 24


def _place():
    x, y, c = lax.axis_index("x"), lax.axis_index("y"), lax.axis_index("c")
    chips = [(1 - x, y), (x, 1 - y), (1 - x, 1 - y)]
    return x, y, c, 2 * x + y, (x, y, 1 - c), chips


def _rcopy(src, dst, send_sems, recv_sems, k, to):
    return pltpu.make_async_remote_copy(src_ref=src, dst_ref=dst, send_sem=send_sems.at[k],
                                        recv_sem=recv_sems.at[k], device_id=to, device_id_type=MESH_ID)


def _gather_weights_call(win_sh, wout_sh, small_sh):
    def body(win_ref, wout_ref, sm_ref, wg_ref, wog_ref, smg_ref, send_sems, recv_sems):
        x, y, c, me, sib, chips = _place()
        rc = functools.partial(_rcopy, send_sems=send_sems, recv_sems=recv_sems)
        hw, ho = pl.ds(c * 512, 512), pl.ds(c * 128, 128)
        hw2, ho2 = pl.ds((1 - c) * 512, 512), pl.ds((1 - c) * 128, 128)
        first = []
        for j, (kx, ky) in enumerate(chips):
            to = (kx, ky, c)
            first.append(rc(win_ref.at[hw], wg_ref.at[me, hw], k=j, to=to))
            first.append(rc(wout_ref.at[ho], wog_ref.at[me, ho], k=3 + j, to=to))
            first.append(rc(sm_ref, smg_ref.at[me], k=6 + j, to=to))
        for cp in first:
            cp.start()
        wg_ref[me] = win_ref[...]
        wog_ref[me] = wout_ref[...]
        smg_ref[me] = sm_ref[...]
        passed = []
        for j, (kx, ky) in enumerate(chips):
            kid = 2 * kx + ky
            rc(win_ref.at[hw], wg_ref.at[kid, hw], k=j, to=sib).wait_recv()
            f = rc(wg_ref.at[kid, hw], wg_ref.at[kid, hw], k=9 + j, to=sib)
            f.start()
            passed.append(f)
            rc(wout_ref.at[ho], wog_ref.at[kid, ho], k=3 + j, to=sib).wait_recv()
            f = rc(wog_ref.at[kid, ho], wog_ref.at[kid, ho], k=12 + j, to=sib)
            f.start()
            passed.append(f)
        for j, (kx, ky) in enumerate(chips):
            kid = 2 * kx + ky
            rc(sm_ref, smg_ref.at[kid], k=6 + j, to=sib).wait_recv()
            rc(wg_ref.at[kid, hw2], wg_ref.at[kid, hw2], k=9 + j, to=sib).wait_recv()
            rc(wog_ref.at[kid, ho2], wog_ref.at[kid, ho2], k=12 + j, to=sib).wait_recv()
        for cp in first + passed:
            cp.wait_send()

    return pl.pallas_call(
        body, name="gather_weights",
        in_specs=[VMEM_SPEC] * 3, out_specs=[VMEM_SPEC] * 3,
        out_shape=[jax.ShapeDtypeStruct((N_CHIPS, D_MODEL, 1024), BF16),
                   jax.ShapeDtypeStruct((N_CHIPS, 256, D_MODEL), BF16),
                   jax.ShapeDtypeStruct((N_CHIPS, SMALL_ROWS, 256), F32)],
        scratch_shapes=[pltpu.SemaphoreType.DMA((15,)), pltpu.SemaphoreType.DMA((15,))],
        compiler_params=_cparams(),
    )(win_sh, wout_sh, small_sh)


def _reduce_scatter_call(g, name):
    _, _, rows, cols = g.shape

    def body(g_ref, out_ref, a_ref, b_ref, send_sems, recv_sems):
        x, y, c, me, sib, chips = _place()
        rc = functools.partial(_rcopy, send_sems=send_sems, recv_sems=recv_sems)
        d1 = rc(g_ref.at[1 - c], a_ref, k=0, to=sib)
        d1.start()
        d1.wait_recv()
        for j in range(N_CHIPS):
            a_ref[j] = a_ref[j] + g_ref[c, j]
        sends = []
        for j, (kx, ky) in enumerate(chips):
            s = rc(a_ref.at[2 * kx + ky], b_ref.at[me], k=1 + j, to=(kx, ky, c))
            s.start()
            sends.append(s)
        b_ref[me] = a_ref[me]
        for j, (kx, ky) in enumerate(chips):
            kid = 2 * kx + ky
            rc(a_ref.at[kid], b_ref.at[kid], k=1 + j, to=sib).wait_recv()
        out_ref[c] = ((b_ref[0] + b_ref[1]) + b_ref[2]) + b_ref[3]
        d5 = rc(out_ref.at[c], out_ref.at[c], k=4, to=sib)
        d5.start()
        rc(out_ref.at[1 - c], out_ref.at[1 - c], k=4, to=sib).wait_recv()
        for cp in [d1] + sends + [d5]:
            cp.wait_send()

    return pl.pallas_call(
        body, name=name,
        in_specs=[VMEM_SPEC], out_specs=VMEM_SPEC,
        out_shape=jax.ShapeDtypeStruct((2, rows, cols), F32),
        scratch_shapes=[pltpu.VMEM((N_CHIPS, rows, cols), F32), pltpu.VMEM((N_CHIPS, rows, cols), F32),
                        pltpu.SemaphoreType.DMA((5,)), pltpu.SemaphoreType.DMA((5,))],
        compiler_params=_cparams(),
    )(g)


def _allreduce_small_call(pack):
    def body(p_ref, out_ref, slot_ref, send_sems, recv_sems):
        x, y, c = lax.axis_index("x"), lax.axis_index("y"), lax.axis_index("c")
        myid = 4 * x + 2 * y + c
        rc = functools.partial(_rcopy, send_sems=send_sems, recv_sems=recv_sems)
        copies = []
        for r in range(1, N_DEV):
            peer = ((1 - x) if r & 4 else x, (1 - y) if r & 2 else y, (1 - c) if r & 1 else c)
            cp = rc(p_ref, slot_ref.at[r], k=r, to=peer)
            cp.start()
            copies.append(cp)
        slot_ref[0] = p_ref[...]
        for cp in copies:
            cp.wait_recv()
        acc = slot_ref[myid]
        for a in range(1, N_DEV):
            acc = acc + slot_ref[jnp.bitwise_xor(myid, a)]
        out_ref[...] = acc
        for cp in copies:
            cp.wait_send()

    return pl.pallas_call(
        body, name="allreduce_small",
        in_specs=[VMEM_SPEC], out_specs=VMEM_SPEC,
        out_shape=jax.ShapeDtypeStruct(pack.shape, F32),
        scratch_shapes=[pltpu.VMEM((N_DEV,) + pack.shape, F32),
                        pltpu.SemaphoreType.DMA((N_DEV,)), pltpu.SemaphoreType.DMA((N_DEV,))],
        compiler_params=_cparams(),
    )(pack)


def _adamw_call(w, g, m, v, name):
    shape = w.shape
    w2, g2, m2, v2 = (a.reshape(-1, shape[-1]) for a in (w, g, m, v))
    rows, cols = w2.shape
    br = 256 if rows % 256 == 0 else rows

    def body(w_ref, g_ref, m_ref, v_ref, d_ref, nm_ref, nv_ref):
        gg = g_ref[...]
        nm = ADAM_B1 * m_ref[...] + (1.0 - ADAM_B1) * gg
        nv = ADAM_B2 * v_ref[...] + (1.0 - ADAM_B2) * (gg * gg)
        m_hat = nm / (1.0 - ADAM_B1 ** ADAM_STEP)
        v_hat = nv / (1.0 - ADAM_B2 ** ADAM_STEP)
        d_ref[...] = -ADAM_LR * (m_hat / (jnp.sqrt(v_hat) + ADAM_EPS) + ADAM_WD * w_ref[...])
        nm_ref[...] = nm
        nv_ref[...] = nv

    spec = pl.BlockSpec((br, cols), lambda i: (i, 0))
    outs = pl.pallas_call(
        body, name=name, grid=(rows // br,),
        in_specs=[spec] * 4, out_specs=[spec] * 3,
        out_shape=[jax.ShapeDtypeStruct((rows, cols), F32)] * 3,
        compiler_params=_cparams(("arbitrary",)),
    )(w2, g2, m2, v2)
    return tuple(o.reshape(shape) for o in outs)


def _pad_to(a, rows, cols):
    return jnp.pad(a, ((0, rows - a.shape[0]), (0, cols - a.shape[1])))


def kernel(x, meta, norm1_g, w_in, conv_w, ret_norm_g, w_out, final_g, loss_target, m_meta, m_norm1_g, m_w_in, m_conv_w, m_ret_norm_g, m_w_out, m_final_g, v_meta, v_norm1_g, v_w_in, v_conv_w, v_ret_norm_g, v_w_out, v_final_g):
    me = 2 * lax.axis_index("x") + lax.axis_index("y")

    small_sh = jnp.concatenate([meta, _pad_to(conv_w, 8, 256)], axis=0)
    w_in_g, w_out_g, small_g = _gather_weights_call(w_in.astype(BF16), w_out.astype(BF16), small_sh)
    meta_full = jnp.concatenate([small_g[j, 0:N_META, :] for j in range(N_CHIPS)], axis=1)
    conv_full = jnp.concatenate([small_g[j, N_META:N_META + 3, 0:128] for j in range(N_CHIPS)], axis=1)

    loss, grad_x, g = _local_step(x[0], loss_target[0], meta_full, norm1_g, w_in_g, conv_full, ret_norm_g,
                                  w_out_g.reshape(D_MODEL, D_MODEL), final_g)

    g_win = _reduce_scatter_call(g["w_in"], "reduce_scatter_w_in").reshape(D_MODEL, 1024)
    g_wout = _reduce_scatter_call(g["w_out"], "reduce_scatter_w_out").reshape(256, D_MODEL)
    vec = jnp.concatenate([g["norm1_g"], g["final_g"], _pad_to(g["ret_norm_g"], 1, D_MODEL),
                           _pad_to(g["conv_w"], 5, D_MODEL)], axis=0)
    tot = _allreduce_small_call(jnp.concatenate([g["meta"], vec], axis=0))
    g_meta = lax.dynamic_slice(tot, (0, me * 256), (N_META, 256))
    g_conv = lax.dynamic_slice(tot, (N_META + 3, me * 128), (3, 128))
    g_n1, g_fg, g_rn = tot[N_META], tot[N_META + 1], tot[N_META + 2, 0:D_RET]

    loss_tot = lax.psum(loss[0, 0], ("x", "y", "c"))

    grads = [g_meta, g_n1, g_win, g_conv, g_rn, g_wout, g_fg]
    ws = [meta, norm1_g, w_in, conv_w, ret_norm_g, w_out, final_g]
    ms = [m_meta, m_norm1_g, m_w_in, m_conv_w, m_ret_norm_g, m_w_out, m_final_g]
    vs = [v_meta, v_norm1_g, v_w_in, v_conv_w, v_ret_norm_g, v_w_out, v_final_g]
    names = ["meta", "norm1_g", "w_in", "conv_w", "ret_norm_g", "w_out", "final_g"]
    deltas, new_ms, new_vs = [], [], []
    for w_, g_, m_, v_, n_ in zip(ws, grads, ms, vs, names):
        as2d = (lambda a: a.reshape(1, -1)) if w_.ndim == 1 else (lambda a: a)
        d_, nm_, nv_ = _adamw_call(as2d(w_), as2d(g_), as2d(m_), as2d(v_), "adamw_" + n_)
        deltas.append(d_.reshape(w_.shape))
        new_ms.append(nm_.reshape(w_.shape))
        new_vs.append(nv_.reshape(w_.shape))
    return (loss_tot, grad_x[None], *grads, *deltas, *new_ms, *new_vs)
```

```python
import functools

import jax
import jax.numpy as jnp
from jax import lax
from jax.experimental import pallas as pl
from jax.experimental.pallas import tpu as pltpu

F32 = jnp.float32
BF16 = jnp.bfloat16

D_MODEL = 1024
N_META = 16
D_CONV = 512
D_RET = 512
RET_HEADS = 4
HEAD_DIM = 128
CHUNK = 128
N_PROJ_COLS = 4096
ROPE_BASE = 10000.0
EPS = 1e-6
N_CHIPS = 4
N_DEV = 8

ADAM_LR = 0.001
ADAM_B1 = 0.9
ADAM_B2 = 0.999
ADAM_EPS = 1e-08
ADAM_WD = 0.01
ADAM_STEP = 10

TM = 512
NCH = TM // CHUNK
VMEM_LIMIT = 56 * 1024 * 1024

CX, CB, CC, CG, CQ, CK, CV, CR = (i * 512 for i in range(8))

MESH_ID = pl.DeviceIdType.MESH


def _cparams(sem=None, **kw):
    return pltpu.CompilerParams(dimension_semantics=sem, vmem_limit_bytes=VMEM_LIMIT, **kw)


def _sigmoid(x):
    return 1.0 / (1.0 + jnp.exp(-x))


def _dot(a, b):
    return jnp.dot(a, b, preferred_element_type=F32)


def _dot_tb(a, b):
    return lax.dot_general(a, b, (((1,), (1,)), ((), ())), preferred_element_type=F32)


def _dot_ta(a, b):
    return lax.dot_general(a, b, (((0,), (0,)), ((), ())), preferred_element_type=F32)


def _resident(shape):
    nd = len(shape)
    return pl.BlockSpec(shape, lambda *_: (0,) * nd)


def _tables(seq):
    nt = seq // TM
    rows = seq + TM
    r = jnp.arange(rows, dtype=jnp.int32)
    pos = jnp.where(r < seq, r + N_META, jnp.maximum(r - (seq + TM - N_META), 0))
    half = HEAD_DIM // 2
    freqs = 1.0 / (ROPE_BASE ** (jnp.arange(half, dtype=F32) / half))
    ang = pos.astype(F32)[:, None] * freqs[None, :]
    cos, sin = jnp.cos(ang), jnp.sin(ang)
    rc = jnp.concatenate([cos, cos], axis=-1)
    rs = jnp.concatenate([-sin, sin], axis=-1)
    log_g = jnp.log(1.0 - 2.0 ** (-5.0 - jnp.arange(RET_HEADS, dtype=F32)))
    idx = jnp.arange(CHUNK, dtype=F32)
    diff = idx[:, None] - idx[None, :]
    decay = jnp.where(diff[None] >= 0, jnp.exp(diff[None] * log_g[:, None, None]), 0.0)
    zeta = jnp.exp((CHUNK - 1 - idx)[None, :] * log_g[:, None])
    xi = jnp.exp((idx + 1.0)[None, :] * log_g[:, None])
    cd = jnp.exp(CHUNK * log_g)
    zeta_b = jnp.broadcast_to(zeta[:, :, None], (RET_HEADS, CHUNK, HEAD_DIM))
    xi_b = jnp.broadcast_to(xi[:, :, None], (RET_HEADS, CHUNK, HEAD_DIM))
    cd_b = jnp.broadcast_to(cd[:, None, None], (RET_HEADS, 8, HEAD_DIM))
    return dict(nt=nt, rows=rows, rc=rc, rs=rs, decay=decay, decay_t=jnp.swapaxes(decay, 1, 2), zeta=zeta_b,
                xi=xi_b, cd=cd_b)


def _rot(t, rc, rs):
    return t * rc + pltpu.roll(t, HEAD_DIM // 2, 1) * rs


def _rot_t(dt, rc, rs):
    return dt * rc + pltpu.roll(dt * rs, HEAD_DIM // 2, 1)


def _f1_call(x, meta_tile, g1, w_in_g, nt):
    rows = (nt + 1) * TM

    def body(x_ref, mt_ref, g_ref, w_ref, proj_ref, hnt_ref):
        i = pl.program_id(0)
        h = jnp.where(i == nt, mt_ref[...], x_ref[...])
        ms = jnp.mean(h * h, axis=-1, keepdims=True)
        hn = (h * lax.rsqrt(ms + EPS)) * g_ref[...]
        hb = hn.astype(BF16)
        for j in range(N_CHIPS):
            proj_ref[:, j * 1024:(j + 1) * 1024] = _dot(hb, w_ref[j]).astype(BF16)
        hnt_ref[...] = hn.T.astype(BF16)

    return pl.pallas_call(
        body, name="f1_norm_inproj",
        grid=(nt + 1,),
        in_specs=[pl.BlockSpec((TM, D_MODEL), lambda i: (jnp.minimum(i, nt - 1), 0)),
                  _resident((TM, D_MODEL)),
                  _resident((1, D_MODEL)),
                  _resident((N_CHIPS, D_MODEL, 1024))],
        out_specs=[pl.BlockSpec((TM, N_PROJ_COLS), lambda i: (i, 0)),
                   pl.BlockSpec((D_MODEL, TM), lambda i: (0, i))],
        out_shape=[jax.ShapeDtypeStruct((rows, N_PROJ_COLS), BF16),
                   jax.ShapeDtypeStruct((D_MODEL, rows), BF16)],
        compiler_params=_cparams(("arbitrary",)),
    )(x, meta_tile, g1, w_in_g)


def _f2_call(proj, conv_w8, gret, tb):
    nt, rows = tb["nt"], tb["rows"]

    def phys(s):
        return jnp.where(s == 0, nt, s - 1)

    def body(proj_ref, cw_ref, g_ref, rc_ref, rs_ref, dec_ref, xi_ref, zeta_ref, cd_ref,
             mixed_ref, conv_ref, states_ref, state, uhalo):
        s = pl.program_id(0)

        @pl.when(s == 0)
        def _():
            state[...] = jnp.zeros_like(state)
            uhalo[...] = jnp.zeros_like(uhalo)

        cx = proj_ref[:, CX:CX + 512].astype(F32)
        cc = proj_ref[:, CC:CC + 512].astype(F32)
        u = cc * cx
        row = lax.broadcasted_iota(jnp.int32, (TM, D_CONV), 0)
        h7 = uhalo[7:8, :]
        h6 = uhalo[6:7, :]
        u1 = jnp.where(row == 0, h7, pltpu.roll(u, 1, 0))
        u2 = jnp.where(row == 0, h6, jnp.where(row == 1, h7, pltpu.roll(u, 2, 0)))
        conv = cw_ref[2:3, :] * u + cw_ref[1:2, :] * u1 + cw_ref[0:1, :] * u2
        uhalo[...] = u[TM - 8:TM, :]
        cb = proj_ref[:, CB:CB + 512].astype(F32)
        cg = proj_ref[:, CG:CG + 512].astype(F32)
        mixed_ref[:, 0:D_CONV] = (cb * conv * (cg * _sigmoid(cg))).astype(BF16)
        conv_ref[...] = conv.astype(BF16)

        scale = HEAD_DIM ** -0.5
        H = range(RET_HEADS)
        st = [state[h] for h in H]
        for c in range(NCH):
            r0 = c * CHUNK
            rc = rc_ref[r0:r0 + CHUNK, :]
            rs = rs_ref[r0:r0 + CHUNK, :]
            col = lambda base, h: slice(base + h * HEAD_DIM, base + (h + 1) * HEAD_DIM)
            rws = slice(r0, r0 + CHUNK)
            v = [proj_ref[rws, col(CV, h)] for h in H]
            qf = [_rot(proj_ref[rws, col(CQ, h)].astype(F32), rc, rs) * scale for h in H]
            kf = [_rot(proj_ref[rws, col(CK, h)].astype(F32), rc, rs) for h in H]
            stb = [t.astype(BF16) for t in st]
            for h in H:
                states_ref[c, h] = stb[h]
            a = [(_dot_tb(qf[h].astype(BF16), kf[h].astype(BF16)) * dec_ref[h]).astype(BF16) for h in H]
            o = [_dot(a[h], v[h]) + _dot((qf[h] * xi_ref[h]).astype(BF16), stb[h]) for h in H]
            st = [cd_ref[h, 0:1, :] * st[h] + _dot_ta((kf[h] * zeta_ref[h]).astype(BF16), v[h]) for h in H]
            for h in H:
                mu = jnp.mean(o[h], axis=-1, keepdims=True)
                d = o[h] - mu
                var = jnp.mean(d * d, axis=-1, keepdims=True)
                yh = d * lax.rsqrt(var + EPS)
                rg = proj_ref[rws, col(CR, h)].astype(F32)
                mixed_ref[rws, col(D_CONV, h)] = (yh * g_ref[:, col(0, h)] * (rg * _sigmoid(rg))).astype(BF16)
        for h in H:
            state[h] = st[h]

    tile = lambda w: pl.BlockSpec((TM, w), lambda s: (phys(s), 0))
    return pl.pallas_call(
        body, name="f2_mixer_fwd",
        grid=(nt + 1,),
        in_specs=[tile(N_PROJ_COLS), _resident((8, D_CONV)), _resident((1, D_RET)),
                  tile(HEAD_DIM), tile(HEAD_DIM),
                  _resident((RET_HEADS, CHUNK, CHUNK)), _resident((RET_HEADS, CHUNK, HEAD_DIM)),
                  _resident((RET_HEADS, CHUNK, HEAD_DIM)), _resident((RET_HEADS, 8, HEAD_DIM))],
        out_specs=[tile(D_MODEL), tile(D_CONV),
                   pl.BlockSpec((NCH, RET_HEADS, HEAD_DIM, HEAD_DIM), lambda s: (phys(s), 0, 0, 0))],
        out_shape=[jax.ShapeDtypeStruct((rows, D_MODEL), BF16),
                   jax.ShapeDtypeStruct((rows, D_CONV), BF16),
                   jax.ShapeDtypeStruct(((nt + 1) * NCH, RET_HEADS, HEAD_DIM, HEAD_DIM), BF16)],
        scratch_shapes=[pltpu.VMEM((RET_HEADS, HEAD_DIM, HEAD_DIM), F32), pltpu.VMEM((8, D_CONV), F32)],
        compiler_params=_cparams(("arbitrary",)),
    )(proj, conv_w8, gret, tb["rc"], tb["rs"], tb["decay"], tb["xi"], tb["zeta"], tb["cd"])


def _f3_call(x, mixed, w_out, fg, target, nt):
    seq = nt * TM

    def body(x_ref, mx_ref, w_ref, g_ref, t_ref, dh2_ref, dmx_ref, gwo_ref, gfg_ref, loss_ref, lacc):
        i = pl.program_id(0)

        @pl.when(i == 0)
        def _():
            gwo_ref[...] = jnp.zeros_like(gwo_ref)
            gfg_ref[...] = jnp.zeros_like(gfg_ref)
            lacc[...] = jnp.zeros_like(lacc)

        mx = mx_ref[...]
        h2 = x_ref[...] + _dot(mx, w_ref[...])
        ms = jnp.mean(h2 * h2, axis=-1, keepdims=True)
        rstd = lax.rsqrt(ms + EPS)
        yh = h2 * rstd
        g = g_ref[...]
        e = yh * g - t_ref[...]
        lacc[...] += jnp.sum(e * e, axis=0, keepdims=True)
        dy = e * (1.0 / D_MODEL)
        gfg_ref[...] += jnp.sum(dy * yh, axis=0, keepdims=True)
        dyh = dy * g
        dh2 = rstd * (dyh - yh * jnp.mean(dyh * yh, axis=-1, keepdims=True))
        dh2_ref[...] = dh2
        db = dh2.astype(BF16)
        dmx_ref[...] = _dot_tb(db, w_ref[...]).astype(BF16)
        gw = _dot_ta(mx, db)
        for j in range(N_CHIPS):
            for hf in range(2):
                r0 = j * 256 + hf * 128
                gwo_ref[hf, j] += gw[r0:r0 + 128, :]

        @pl.when(i == nt - 1)
        def _():
            tot = jnp.sum(lacc[...], axis=1, keepdims=True) * (0.5 / D_MODEL)
            loss_ref[...] = jnp.broadcast_to(tot, (1, 128))

    tile = lambda w: pl.BlockSpec((TM, w), lambda i: (i, 0))
    return pl.pallas_call(
        body, name="f3_outproj_loss",
        grid=(nt,),
        in_specs=[tile(D_MODEL), tile(D_MODEL), _resident((D_MODEL, D_MODEL)), _resident((1, D_MODEL)),
                  tile(D_MODEL)],
        out_specs=[tile(D_MODEL), tile(D_MODEL), _resident((2, N_CHIPS, 128, D_MODEL)), _resident((1, D_MODEL)),
                   _resident((1, 128))],
        out_shape=[jax.ShapeDtypeStruct((seq, D_MODEL), F32),
                   jax.ShapeDtypeStruct((seq, D_MODEL), BF16),
                   jax.ShapeDtypeStruct((2, N_CHIPS, 128, D_MODEL), F32),
                   jax.ShapeDtypeStruct((1, D_MODEL), F32),
                   jax.ShapeDtypeStruct((1, 128), F32)],
        scratch_shapes=[pltpu.VMEM((1, D_MODEL), F32)],
        compiler_params=_cparams(("arbitrary",)),
    )(x, mixed, w_out, fg, target)


def _b2_call(proj, dmixed, conv_s, states, conv_w8, gret, tb):
    nt, rows = tb["nt"], tb["rows"]

    def phys(r):
        return jnp.where(r == nt, nt, nt - 1 - r)

    def body(proj_ref, dmx_ref, conv_ref, states_ref, cw_ref, g_ref, rc_ref, rs_ref, dec_ref, dect_ref, xi_ref,
             zeta_ref, cd_ref, dproj_ref, gcw_ref, gg_ref, gstate, dchalo):
        r = pl.program_id(0)
        live = jnp.where(r == nt, 0.0, 1.0)

        @pl.when(r == 0)
        def _():
            gstate[...] = jnp.zeros_like(gstate)
            dchalo[...] = jnp.zeros_like(dchalo)
            gcw_ref[...] = jnp.zeros_like(gcw_ref)
            gg_ref[...] = jnp.zeros_like(gg_ref)

        cx = proj_ref[:, CX:CX + 512].astype(F32)
        cb = proj_ref[:, CB:CB + 512].astype(F32)
        cc = proj_ref[:, CC:CC + 512].astype(F32)
        cg = proj_ref[:, CG:CG + 512].astype(F32)
        dco = dmx_ref[:, 0:D_CONV].astype(F32) * live
        conv = conv_ref[...].astype(F32)
        sg = _sigmoid(cg)
        sil = cg * sg
        t = dco * conv
        dproj_ref[:, CB:CB + 512] = (t * sil).astype(BF16)
        dproj_ref[:, CG:CG + 512] = (t * cb * (sg * (1.0 + cg * (1.0 - sg)))).astype(BF16)
        dconv = dco * cb * sil
        row = lax.broadcasted_iota(jnp.int32, (TM, D_CONV), 0)
        n0 = dchalo[0:1, :]
        n1 = dchalo[1:2, :]
        dc1 = jnp.where(row == TM - 1, n0, pltpu.roll(dconv, TM - 1, 0))
        dc2 = jnp.where(row == TM - 2, n0, jnp.where(row == TM - 1, n1, pltpu.roll(dconv, TM - 2, 0)))
        dchalo[...] = dconv[0:8, :]
        du = cw_ref[2:3, :] * dconv + cw_ref[1:2, :] * dc1 + cw_ref[0:1, :] * dc2
        u = cc * cx
        gcw_ref[2:3, :] += jnp.sum(u * dconv, axis=0, keepdims=True)
        gcw_ref[1:2, :] += jnp.sum(u * dc1, axis=0, keepdims=True)
        gcw_ref[0:1, :] += jnp.sum(u * dc2, axis=0, keepdims=True)
        dproj_ref[:, CC:CC + 512] = (du * cx).astype(BF16)
        dproj_ref[:, CX:CX + 512] = (du * cc).astype(BF16)

        scale = HEAD_DIM ** -0.5
        H = range(RET_HEADS)
        gs = [gstate[h] for h in H]
        gg = [jnp.zeros((1, HEAD_DIM), F32) for _ in H]
        for c in range(NCH - 1, -1, -1):
            r0 = c * CHUNK
            rc = rc_ref[r0:r0 + CHUNK, :]
            rs = rs_ref[r0:r0 + CHUNK, :]
            col = lambda base, h: slice(base + h * HEAD_DIM, base + (h + 1) * HEAD_DIM)
            rws = slice(r0, r0 + CHUNK)
            v = [proj_ref[rws, col(CV, h)] for h in H]
            stb = [states_ref[c, h] for h in H]
            qf = [_rot(proj_ref[rws, col(CQ, h)].astype(F32), rc, rs) * scale for h in H]
            kf = [_rot(proj_ref[rws, col(CK, h)].astype(F32), rc, rs) for h in H]
            qb = [t.astype(BF16) for t in qf]
            kb = [t.astype(BF16) for t in kf]
            qxb = [(qf[h] * xi_ref[h]).astype(BF16) for h in H]
            kzb = [(kf[h] * zeta_ref[h]).astype(BF16) for h in H]
            gsb = [t.astype(BF16) for t in gs]
            ab = [(_dot_tb(qb[h], kb[h]) * dec_ref[h]).astype(BF16) for h in H]
            atb = [(_dot_tb(kb[h], qb[h]) * dect_ref[h]).astype(BF16) for h in H]
            o = [_dot(ab[h], v[h]) + _dot(qxb[h], stb[h]) for h in H]
            dob = []
            for h in H:
                mu = jnp.mean(o[h], axis=-1, keepdims=True)
                d = o[h] - mu
                var = jnp.mean(d * d, axis=-1, keepdims=True)
                rstd = lax.rsqrt(var + EPS)
                yh = d * rstd
                g = g_ref[:, col(0, h)]
                rg = proj_ref[rws, col(CR, h)].astype(F32)
                dro = dmx_ref[rws, col(D_CONV, h)].astype(F32) * live
                sg = _sigmoid(rg)
                dproj_ref[rws, col(CR, h)] = (dro * (yh * g) * (sg * (1.0 + rg * (1.0 - sg)))).astype(BF16)
                dret = dro * (rg * sg)
                gg[h] = gg[h] + jnp.sum(dret * yh, axis=0, keepdims=True)
                dyh = dret * g
                do = rstd * (dyh - jnp.mean(dyh, axis=-1, keepdims=True)
                             - yh * jnp.mean(dyh * yh, axis=-1, keepdims=True))
                dob.append(do.astype(BF16))
            dv = [_dot(atb[h], dob[h]) + _dot(kzb[h], gsb[h]) for h in H]
            ds = [(_dot_tb(dob[h], v[h]) * dec_ref[h]).astype(BF16) for h in H]
            dst = [(_dot_tb(v[h], dob[h]) * dect_ref[h]).astype(BF16) for h in H]
            dqf = [_dot(ds[h], kb[h]) + _dot_tb(dob[h], stb[h]) * xi_ref[h] for h in H]
            dkf = [_dot(dst[h], qb[h]) + _dot_tb(v[h], gsb[h]) * zeta_ref[h] for h in H]
            gs = [cd_ref[h, 0:1, :] * gs[h] + _dot_ta(qxb[h], dob[h]) for h in H]
            for h in H:
                dproj_ref[rws, col(CQ, h)] = (_rot_t(dqf[h], rc, rs) * scale).astype(BF16)
                dproj_ref[rws, col(CK, h)] = _rot_t(dkf[h], rc, rs).astype(BF16)
                dproj_ref[rws, col(CV, h)] = dv[h].astype(BF16)
        for h in H:
            gstate[h] = gs[h]
            gg_ref[:, h * HEAD_DIM:(h + 1) * HEAD_DIM] += gg[h]

    tile = lambda w: pl.BlockSpec((TM, w), lambda r: (phys(r), 0))
    return pl.pallas_call(
        body, name="b2_mixer_bwd",
        grid=(nt + 1,),
        in_specs=[tile(N_PROJ_COLS),
                  pl.BlockSpec((TM, D_MODEL), lambda r: (jnp.minimum(phys(r), nt - 1), 0)),
                  tile(D_CONV),
                  pl.BlockSpec((NCH, RET_HEADS, HEAD_DIM, HEAD_DIM), lambda r: (phys(r), 0, 0, 0)),
                  _resident((8, D_CONV)), _resident((1, D_RET)),
                  tile(HEAD_DIM), tile(HEAD_DIM),
                  _resident((RET_HEADS, CHUNK, CHUNK)), _resident((RET_HEADS, CHUNK, CHUNK)),
                  _resident((RET_HEADS, CHUNK, HEAD_DIM)),
                  _resident((RET_HEADS, CHUNK, HEAD_DIM)), _resident((RET_HEADS, 8, HEAD_DIM))],
        out_specs=[tile(N_PROJ_COLS), _resident((8, D_CONV)), _resident((1, D_RET))],
        out_shape=[jax.ShapeDtypeStruct((rows, N_PROJ_COLS), BF16),
                   jax.ShapeDtypeStruct((8, D_CONV), F32),
                   jax.ShapeDtypeStruct((1, D_RET), F32)],
        scratch_shapes=[pltpu.VMEM((RET_HEADS, HEAD_DIM, HEAD_DIM), F32), pltpu.VMEM((8, D_CONV), F32)],
        compiler_params=_cparams(("arbitrary",)),
    )(proj, dmixed, conv_s, states, conv_w8, gret, tb["rc"], tb["rs"], tb["decay"], tb["decay_t"], tb["xi"],
      tb["zeta"], tb["cd"])


def _b1a_call(dproj, w_in_g, x, meta_tile, g1, dh2, nt):
    seq = nt * TM

    def dphys(i):
        return jnp.where(i == 0, nt, i - 1)

    def xblk(i):
        return jnp.maximum(i - 1, 0)

    def body(dp_ref, w_ref, x_ref, mt_ref, g_ref, dh2_ref, gx_ref, dmeta_ref, gn_ref):
        i = pl.program_id(0)
        is_meta = i == 0

        @pl.when(is_meta)
        def _():
            gn_ref[...] = jnp.zeros_like(gn_ref)

        dhn = _dot_tb(dp_ref[:, 0:1024], w_ref[0])
        for j in range(1, N_CHIPS):
            dhn += _dot_tb(dp_ref[:, j * 1024:(j + 1) * 1024], w_ref[j])
        h = jnp.where(is_meta, mt_ref[...], x_ref[...])
        ms = jnp.mean(h * h, axis=-1, keepdims=True)
        rstd = lax.rsqrt(ms + EPS)
        xh = h * rstd
        gn_ref[...] += jnp.sum(dhn * xh, axis=0, keepdims=True)
        dxh = dhn * g_ref[...]
        dh = rstd * (dxh - xh * jnp.mean(dxh * xh, axis=-1, keepdims=True))
        gx_ref[...] = dh + jnp.where(is_meta, 0.0, 1.0) * dh2_ref[...]

        @pl.when(is_meta)
        def _():
            dmeta_ref[...] = dh[TM - N_META:TM, :]

    return pl.pallas_call(
        body, name="b1a_inproj_bwd_x",
        grid=(nt + 1,),
        in_specs=[pl.BlockSpec((TM, N_PROJ_COLS), lambda i: (dphys(i), 0)),
                  _resident((N_CHIPS, D_MODEL, 1024)),
                  pl.BlockSpec((TM, D_MODEL), lambda i: (xblk(i), 0)),
                  _resident((TM, D_MODEL)),
                  _resident((1, D_MODEL)),
                  pl.BlockSpec((TM, D_MODEL), lambda i: (xblk(i), 0))],
        out_specs=[pl.BlockSpec((TM, D_MODEL), lambda i: (xblk(i), 0)),
                   _resident((N_META, D_MODEL)), _resident((1, D_MODEL))],
        out_shape=[jax.ShapeDtypeStruct((seq, D_MODEL), F32),
                   jax.ShapeDtypeStruct((N_META, D_MODEL), F32),
                   jax.ShapeDtypeStruct((1, D_MODEL), F32)],
        compiler_params=_cparams(("arbitrary",)),
    )(dproj, w_in_g, x, meta_tile, g1, dh2)


REL = (2, 1, 3)
SMALL_ROWS = 24
HALF_STEP = 2
WOUT_STEP = 4


def _rcopy(src, dst, send_sems, recv_sems, k, to):
    return pltpu.make_async_remote_copy(src_ref=src, dst_ref=dst, send_sem=send_sems.at[k],
                                        recv_sem=recv_sems.at[k], device_id=to, device_id_type=MESH_ID)


def _b1b_reduce_call(order, hnt, dproj, gwo, pack, nt):
    nk = nt + 1
    last = nk - 1
    any_spec = pl.BlockSpec(memory_space=pl.ANY)

    def body(order_ref, a_ref, b_ref, gwo_hbm, pack_hbm, gwin_hbm, gwout_hbm, tot_hbm,
             acc, sb, abuf, pb, bbuf, fin, go, ao, pbo, bo, fino, slots, totv, send_sems, recv_sems, loc_sems):
        jj, k = pl.program_id(0), pl.program_id(1)
        x, y, c = lax.axis_index("x"), lax.axis_index("y"), lax.axis_index("c")
        me, myid, sib = 2 * x + y, 4 * x + 2 * y + c, (x, y, 1 - c)
        rc = functools.partial(_rcopy, send_sems=send_sems, recv_sems=recv_sems)
        peers = [((1 - x) if r & 2 else x, (1 - y) if r & 1 else y, c) for r in REL]
        kids = [jnp.bitwise_xor(me, r) for r in REL]

        def dev_peer(r):
            return ((1 - x) if r & 4 else x, (1 - y) if r & 2 else y, (1 - c) if r & 1 else c)

        own_go = pltpu.make_async_copy(gwo_hbm.at[c], go, loc_sems.at[0])
        own_pack = pltpu.make_async_copy(pack_hbm, slots.at[0], loc_sems.at[1])
        wo_half = rc(gwo_hbm.at[1 - c], ao, k=8, to=sib)
        wo_part = [rc(pbo.at[kids[p]], bo.at[p], k=9 + p, to=peers[p]) for p in range(3)]
        sm = [rc(pack_hbm, slots.at[r], k=12 + r, to=dev_peer(r)) for r in range(1, N_DEV)]
        half = [rc(sb.at[j % 2, 1 - c], abuf.at[j], k=j, to=sib) for j in range(N_CHIPS)]
        part = [rc(pb.at[p], bbuf.at[p], k=4 + p, to=peers[p]) for p in range(3)]

        @pl.when(jnp.logical_and(jj == 0, k == 0))
        def _():
            own_go.start()
            own_pack.start()
            wo_half.start()
            for cp in sm:
                cp.start()

        @pl.when(k == 0)
        def _():
            acc[...] = jnp.zeros_like(acc)

        acc[0] += _dot(a_ref[0:512, :], b_ref[...])
        acc[1] += _dot(a_ref[512:1024, :], b_ref[...])

        @pl.when(jnp.logical_and(jj == 0, k == WOUT_STEP))
        def _():
            own_go.wait()
            wo_half.wait_recv()
            for j in range(N_CHIPS):
                go[j] = go[j] + ao[j]
            pbo[...] = go[...].astype(BF16)
            for cp in wo_part:
                cp.start()

        for j in range(N_CHIPS):
            @pl.when(jnp.logical_and(jj == j, k == last))
            def _(j=j):
                sb[j % 2] = acc[...]
                half[j].start()

        for p in range(3):
            @pl.when(jnp.logical_and(jj == p + 1, k == HALF_STEP))
            def _(p=p):
                half[p].wait_recv()
                half[p].wait_send()
                pb[p] = (sb[p % 2, c] + abuf[p]).astype(BF16)
                part[p].start()

        @pl.when(jnp.logical_and(jj == N_CHIPS - 1, k == last))
        def _():
            half[3].wait_recv()
            own = sb[1, c] + abuf[3]
            for cp in part:
                cp.wait_recv()
            fin[c] = ((own + bbuf[0].astype(F32)) + bbuf[1].astype(F32)) + bbuf[2].astype(F32)
            done = rc(fin.at[c], fin.at[c], k=7, to=sib)
            done.start()
            for cp in wo_part:
                cp.wait_recv()
            fino[c] = ((go[me] + bo[0].astype(F32)) + bo[1].astype(F32)) + bo[2].astype(F32)
            done_o = rc(fino.at[c], fino.at[c], k=12, to=sib)
            done_o.start()
            own_pack.wait()
            for cp in sm:
                cp.wait_recv()
            tot = slots[myid]
            for a in range(1, N_DEV):
                tot = tot + slots[jnp.bitwise_xor(myid, a)]
            totv[...] = tot
            out_t = pltpu.make_async_copy(totv, tot_hbm, loc_sems.at[1])
            out_t.start()
            rc(fin.at[1 - c], fin.at[1 - c], k=7, to=sib).wait_recv()
            out_w = pltpu.make_async_copy(fin, gwin_hbm, loc_sems.at[0])
            out_w.start()
            rc(fino.at[1 - c], fino.at[1 - c], k=12, to=sib).wait_recv()
            out_o = pltpu.make_async_copy(fino, gwout_hbm, loc_sems.at[2])
            out_o.start()
            for cp in [half[3]] + part + [done, wo_half] + wo_part + [done_o] + sm:
                cp.wait_send()
            out_t.wait()
            out_w.wait()
            out_o.wait()

    grid_spec = pltpu.PrefetchScalarGridSpec(
        num_scalar_prefetch=1,
        grid=(N_CHIPS, nk),
        in_specs=[pl.BlockSpec((D_MODEL, TM), lambda j, k, o: (0, k)),
                  pl.BlockSpec((TM, 1024), lambda j, k, o: (k, o[j])),
                  any_spec, any_spec],
        out_specs=[any_spec, any_spec, any_spec],
        scratch_shapes=[
            pltpu.VMEM((2, 512, 1024), F32),
            pltpu.VMEM((2, 2, 512, 1024), F32),
            pltpu.VMEM((N_CHIPS, 512, 1024), F32),
            pltpu.VMEM((3, 512, 1024), BF16),
            pltpu.VMEM((3, 512, 1024), BF16),
            pltpu.VMEM((2, 512, 1024), F32),
            pltpu.VMEM((N_CHIPS, 128, D_MODEL), F32),
            pltpu.VMEM((N_CHIPS, 128, D_MODEL), F32),
            pltpu.VMEM((N_CHIPS, 128, D_MODEL), BF16),
            pltpu.VMEM((3, 128, D_MODEL), BF16),
            pltpu.VMEM((2, 128, D_MODEL), F32),
            pltpu.VMEM((N_DEV, SMALL_ROWS, D_MODEL), F32),
            pltpu.VMEM((SMALL_ROWS, D_MODEL), F32),
            pltpu.SemaphoreType.DMA((20,)), pltpu.SemaphoreType.DMA((20,)), pltpu.SemaphoreType.DMA((3,))])
    return pl.pallas_call(
        body, name="b1b_inproj_bwd_w_reduce",
        grid_spec=grid_spec,
        out_shape=[jax.ShapeDtypeStruct((2, 512, 1024), F32),
                   jax.ShapeDtypeStruct((2, 128, D_MODEL), F32),
                   jax.ShapeDtypeStruct((SMALL_ROWS, D_MODEL), F32)],
        compiler_params=_cparams(("arbitrary", "arbitrary")),
    )(order, hnt, dproj, gwo, pack)


def _local_step(x, target, meta, g1, w_in_g, conv_w, gret, w_out, fg):
    seq = x.shape[0]
    tb = _tables(seq)
    nt = tb["nt"]
    meta_tile = jnp.concatenate([jnp.zeros((TM - N_META, D_MODEL), F32), meta], axis=0)
    conv_w8 = jnp.concatenate([conv_w, jnp.zeros((5, D_CONV), F32)], axis=0)
    g1r, gretr, fgr = g1.reshape(1, -1), gret.reshape(1, -1), fg.reshape(1, -1)

    proj, hnt = _f1_call(x, meta_tile, g1r, w_in_g, nt)
    mixed, conv_s, states = _f2_call(proj, conv_w8, gretr, tb)
    dh2, dmixed, g_wout, g_fg, loss = _f3_call(x, mixed, w_out, fgr, target, nt)
    dproj, g_cw8, g_gret = _b2_call(proj, dmixed, conv_s, states, conv_w8, gretr, tb)
    grad_x, g_meta, g_g1 = _b1a_call(dproj, w_in_g, x, meta_tile, g1r, dh2, nt)
    return loss, grad_x, dict(w_out=g_wout, meta=g_meta, conv_w=g_cw8[0:3], norm1_g=g_g1,
                              ret_norm_g=g_gret, final_g=g_fg), hnt, dproj


VMEM_SPEC = pl.BlockSpec(memory_space=pltpu.VMEM)


def _place():
    x, y, c = lax.axis_index("x"), lax.axis_index("y"), lax.axis_index("c")
    chips = [(1 - x, y), (x, 1 - y), (1 - x, 1 - y)]
    return x, y, c, 2 * x + y, (x, y, 1 - c), chips


def _gather_weights_call(win_sh, wout_sh, small_sh):
    def body(win_ref, wout_ref, sm_ref, wg_ref, wog_ref, smg_ref, send_sems, recv_sems):
        x, y, c, me, sib, chips = _place()
        rc = functools.partial(_rcopy, send_sems=send_sems, recv_sems=recv_sems)
        hw, ho = pl.ds(c * 512, 512), pl.ds(c * 128, 128)
        hw2, ho2 = pl.ds((1 - c) * 512, 512), pl.ds((1 - c) * 128, 128)
        first = []
        for j, (kx, ky) in enumerate(chips):
            to = (kx, ky, c)
            first.append(rc(win_ref.at[hw], wg_ref.at[me, hw], k=j, to=to))
            first.append(rc(wout_ref.at[ho], wog_ref.at[me, ho], k=3 + j, to=to))
            first.append(rc(sm_ref, smg_ref.at[me], k=6 + j, to=to))
        for cp in first:
            cp.start()
        wg_ref[me] = win_ref[...]
        wog_ref[me] = wout_ref[...]
        smg_ref[me] = sm_ref[...]
        passed = []
        for j, (kx, ky) in enumerate(chips):
            kid = 2 * kx + ky
            rc(win_ref.at[hw], wg_ref.at[kid, hw], k=j, to=sib).wait_recv()
            f = rc(wg_ref.at[kid, hw], wg_ref.at[kid, hw], k=9 + j, to=sib)
            f.start()
            passed.append(f)
            rc(wout_ref.at[ho], wog_ref.at[kid, ho], k=3 + j, to=sib).wait_recv()
            f = rc(wog_ref.at[kid, ho], wog_ref.at[kid, ho], k=12 + j, to=sib)
            f.start()
            passed.append(f)
        for j, (kx, ky) in enumerate(chips):
            kid = 2 * kx + ky
            rc(sm_ref, smg_ref.at[kid], k=6 + j, to=sib).wait_recv()
            rc(wg_ref.at[kid, hw2], wg_ref.at[kid, hw2], k=9 + j, to=sib).wait_recv()
            rc(wog_ref.at[kid, ho2], wog_ref.at[kid, ho2], k=12 + j, to=sib).wait_recv()
        for cp in first + passed:
            cp.wait_send()

    return pl.pallas_call(
        body, name="gather_weights",
        in_specs=[VMEM_SPEC] * 3, out_specs=[VMEM_SPEC] * 3,
        out_shape=[jax.ShapeDtypeStruct((N_CHIPS, D_MODEL, 1024), BF16),
                   jax.ShapeDtypeStruct((N_CHIPS, 256, D_MODEL), BF16),
                   jax.ShapeDtypeStruct((N_CHIPS, SMALL_ROWS, 256), F32)],
        scratch_shapes=[pltpu.SemaphoreType.DMA((15,)), pltpu.SemaphoreType.DMA((15,))],
        compiler_params=_cparams(),
    )(win_sh, wout_sh, small_sh)


def _adamw_call(w, g, m, v, name):
    shape = w.shape
    w2, g2, m2, v2 = (a.reshape(-1, shape[-1]) for a in (w, g, m, v))
    rows, cols = w2.shape
    br = 256 if rows % 256 == 0 else rows

    def body(w_ref, g_ref, m_ref, v_ref, d_ref, nm_ref, nv_ref):
        gg = g_ref[...]
        nm = ADAM_B1 * m_ref[...] + (1.0 - ADAM_B1) * gg
        nv = ADAM_B2 * v_ref[...] + (1.0 - ADAM_B2) * (gg * gg)
        m_hat = nm / (1.0 - ADAM_B1 ** ADAM_STEP)
        v_hat = nv / (1.0 - ADAM_B2 ** ADAM_STEP)
        d_ref[...] = -ADAM_LR * (m_hat / (jnp.sqrt(v_hat) + ADAM_EPS) + ADAM_WD * w_ref[...])
        nm_ref[...] = nm
        nv_ref[...] = nv

    spec = pl.BlockSpec((br, cols), lambda i: (i, 0))
    outs = pl.pallas_call(
        body, name=name, grid=(rows // br,),
        in_specs=[spec] * 4, out_specs=[spec] * 3,
        out_shape=[jax.ShapeDtypeStruct((rows, cols), F32)] * 3,
        compiler_params=_cparams(("arbitrary",)),
    )(w2, g2, m2, v2)
    return tuple(o.reshape(shape) for o in outs)


def _pad_to(a, rows, cols):
    return jnp.pad(a, ((0, rows - a.shape[0]), (0, cols - a.shape[1])))


def kernel(x, meta, norm1_g, w_in, conv_w, ret_norm_g, w_out, final_g, loss_target, m_meta, m_norm1_g, m_w_in, m_conv_w, m_ret_norm_g, m_w_out, m_final_g, v_meta, v_norm1_g, v_w_in, v_conv_w, v_ret_norm_g, v_w_out, v_final_g):
    me = 2 * lax.axis_index("x") + lax.axis_index("y")

    small_sh = jnp.concatenate([meta, _pad_to(conv_w, 8, 256)], axis=0)
    w_in_g, w_out_g, small_g = _gather_weights_call(w_in.astype(BF16), w_out.astype(BF16), small_sh)
    meta_full = jnp.concatenate([small_g[j, 0:N_META, :] for j in range(N_CHIPS)], axis=1)
    conv_full = jnp.concatenate([small_g[j, N_META:N_META + 3, 0:128] for j in range(N_CHIPS)], axis=1)

    loss, grad_x, g, hnt, dproj = _local_step(x[0], loss_target[0], meta_full, norm1_g, w_in_g, conv_full,
                                              ret_norm_g, w_out_g.reshape(D_MODEL, D_MODEL), final_g)

    vec = jnp.concatenate([g["norm1_g"], g["final_g"], _pad_to(g["ret_norm_g"], 1, D_MODEL),
                           _pad_to(g["conv_w"], 5, D_MODEL)], axis=0)
    pack = jnp.concatenate([g["meta"], vec], axis=0)
    order = jnp.stack([me ^ REL[0], me ^ REL[1], me ^ REL[2], me]).astype(jnp.int32)
    g_win, g_wout, tot = _b1b_reduce_call(order, hnt, dproj, g["w_out"], pack, x.shape[1] // TM)
    g_win, g_wout = g_win.reshape(D_MODEL, 1024), g_wout.reshape(256, D_MODEL)
    g_meta = lax.dynamic_slice(tot, (0, me * 256), (N_META, 256))
    g_conv = lax.dynamic_slice(tot, (N_META + 3, me * 128), (3, 128))
    g_n1, g_fg, g_rn = tot[N_META], tot[N_META + 1], tot[N_META + 2, 0:D_RET]

    loss_tot = lax.psum(loss[0, 0], ("x", "y", "c"))

    grads = [g_meta, g_n1, g_win, g_conv, g_rn, g_wout, g_fg]
    ws = [meta, norm1_g, w_in, conv_w, ret_norm_g, w_out, final_g]
    ms = [m_meta, m_norm1_g, m_w_in, m_conv_w, m_ret_norm_g, m_w_out, m_final_g]
    vs = [v_meta, v_norm1_g, v_w_in, v_conv_w, v_ret_norm_g, v_w_out, v_final_g]
    names = ["meta", "norm1_g", "w_in", "conv_w", "ret_norm_g", "w_out", "final_g"]
    deltas, new_ms, new_vs = [], [], []
    for w_, g_, m_, v_, n_ in zip(ws, grads, ms, vs, names):
        as2d = (lambda a: a.reshape(1, -1)) if w_.ndim == 1 else (lambda a: a)
        d_, nm_, nv_ = _adamw_call(as2d(w_), as2d(g_), as2d(m_), as2d(v_), "adamw_" + n_)
        deltas.append(d_.reshape(w_.shape))
        new_ms.append(nm_.reshape(w_.shape))
        new_vs.append(nv_.reshape(w_.shape))
    return (loss_tot, grad_x[None], *grads, *deltas, *new_ms, *new_vs)
```

```python
import functools

import jax
import jax.numpy as jnp
from jax import lax
from jax.experimental import pallas as pl
from jax.experimental.pallas import tpu as pltpu

F32 = jnp.float32
BF16 = jnp.bfloat16

D_MODEL = 1024
N_META = 16
D_CONV = 512
D_RET = 512
RET_HEADS = 4
HEAD_DIM = 128
CHUNK = 128
N_PROJ_COLS = 4096
ROPE_BASE = 10000.0
EPS = 1e-6
N_CHIPS = 4
N_DEV = 8

ADAM_LR = 0.001
ADAM_B1 = 0.9
ADAM_B2 = 0.999
ADAM_EPS = 1e-08
ADAM_WD = 0.01
ADAM_STEP = 10

TM = 512
NCH = TM // CHUNK
VMEM_LIMIT = 56 * 1024 * 1024

CX, CB, CC, CG, CQ, CK, CV, CR = (i * 512 for i in range(8))

MESH_ID = pl.DeviceIdType.MESH


def _cparams(sem=None, **kw):
    return pltpu.CompilerParams(dimension_semantics=sem, vmem_limit_bytes=VMEM_LIMIT, **kw)


def _sigmoid(x):
    return 1.0 / (1.0 + jnp.exp(-x))


def _dot(a, b):
    return jnp.dot(a, b, preferred_element_type=F32)


def _dot_tb(a, b):
    return lax.dot_general(a, b, (((1,), (1,)), ((), ())), preferred_element_type=F32)


def _dot_ta(a, b):
    return lax.dot_general(a, b, (((0,), (0,)), ((), ())), preferred_element_type=F32)


def _resident(shape):
    nd = len(shape)
    return pl.BlockSpec(shape, lambda *_: (0,) * nd)


def _tables(seq):
    nt = seq // TM
    rows = seq + TM
    r = jnp.arange(rows, dtype=jnp.int32)
    pos = jnp.where(r < seq, r + N_META, jnp.maximum(r - (seq + TM - N_META), 0))
    half = HEAD_DIM // 2
    freqs = 1.0 / (ROPE_BASE ** (jnp.arange(half, dtype=F32) / half))
    ang = pos.astype(F32)[:, None] * freqs[None, :]
    cos, sin = jnp.cos(ang), jnp.sin(ang)
    rc = jnp.concatenate([cos, cos], axis=-1)
    rs = jnp.concatenate([-sin, sin], axis=-1)
    log_g = jnp.log(1.0 - 2.0 ** (-5.0 - jnp.arange(RET_HEADS, dtype=F32)))
    idx = jnp.arange(CHUNK, dtype=F32)
    diff = idx[:, None] - idx[None, :]
    decay = jnp.where(diff[None] >= 0, jnp.exp(diff[None] * log_g[:, None, None]), 0.0)
    zeta = jnp.exp((CHUNK - 1 - idx)[None, :] * log_g[:, None])
    xi = jnp.exp((idx + 1.0)[None, :] * log_g[:, None])
    cd = jnp.exp(CHUNK * log_g)
    zeta_b = jnp.broadcast_to(zeta[:, :, None], (RET_HEADS, CHUNK, HEAD_DIM))
    xi_b = jnp.broadcast_to(xi[:, :, None], (RET_HEADS, CHUNK, HEAD_DIM))
    cd_b = jnp.broadcast_to(cd[:, None, None], (RET_HEADS, 8, HEAD_DIM))
    return dict(nt=nt, rows=rows, rc=rc, rs=rs, decay=decay, decay_t=jnp.swapaxes(decay, 1, 2), zeta=zeta_b,
                xi=xi_b, cd=cd_b)


def _rot(t, rc, rs):
    return t * rc + pltpu.roll(t, HEAD_DIM // 2, 1) * rs


def _rot_t(dt, rc, rs):
    return dt * rc + pltpu.roll(dt * rs, HEAD_DIM // 2, 1)


def _f1_gather_call(order, x, g1, win_sh, wout_sh, small_sh, nt):
    nk = nt + 1
    rows = nk * TM
    any_spec = pl.BlockSpec(memory_space=pl.ANY)

    def body(order_ref, x_ref, g_ref, win_hbm, wout_hbm, sm_hbm,
             proj_ref, hnt_ref, wg_hbm, wog_hbm, mt_hbm, cw_hbm,
             wg, wog, smg, mt, cw, send_sems, recv_sems, loc_sems):
        jj, k = pl.program_id(0), pl.program_id(1)
        x, y, c = lax.axis_index("x"), lax.axis_index("y"), lax.axis_index("c")
        me, sib = 2 * x + y, (x, y, 1 - c)
        rc = functools.partial(_rcopy, send_sems=send_sems, recv_sems=recv_sems)
        peers = [((1 - x) if r & 2 else x, (1 - y) if r & 1 else y, c) for r in REL]
        kids = [jnp.bitwise_xor(me, r) for r in REL]
        hw, ho = pl.ds(c * 512, 512), pl.ds(c * 128, 128)
        hw2, ho2 = pl.ds((1 - c) * 512, 512), pl.ds((1 - c) * 128, 128)
        at = lambda j_, k_: jnp.logical_and(jj == j_, k == k_)

        sm_cp = [rc(sm_hbm, smg.at[me], k=p, to=peers[p]) for p in range(3)]
        win_cp = [rc(win_hbm.at[hw], wg.at[me, hw], k=3 + p, to=peers[p]) for p in range(3)]
        wout_cp = [rc(wout_hbm.at[ho], wog.at[me, ho], k=6 + p, to=peers[p]) for p in range(3)]
        sm_in = [rc(sm_hbm, smg.at[kids[p]], k=p, to=sib) for p in range(3)]
        win_in = [rc(win_hbm.at[hw], wg.at[kids[p], hw], k=3 + p, to=sib) for p in range(3)]
        wout_in = [rc(wout_hbm.at[ho], wog.at[kids[p], ho], k=6 + p, to=sib) for p in range(3)]
        win_fw = [rc(wg.at[kids[p], hw], wg.at[kids[p], hw], k=9 + p, to=sib) for p in range(3)]
        wout_fw = [rc(wog.at[kids[p], ho], wog.at[kids[p], ho], k=12 + p, to=sib) for p in range(3)]
        win_fw_in = [rc(wg.at[kids[p], hw2], wg.at[kids[p], hw2], k=9 + p, to=sib) for p in range(3)]
        wout_fw_in = [rc(wog.at[kids[p], ho2], wog.at[kids[p], ho2], k=12 + p, to=sib) for p in range(3)]
        own_w = pltpu.make_async_copy(win_hbm, wg.at[me], loc_sems.at[0])
        own_o = pltpu.make_async_copy(wout_hbm, wog.at[me], loc_sems.at[1])
        own_s = pltpu.make_async_copy(sm_hbm, smg.at[me], loc_sems.at[2])
        out_wg = pltpu.make_async_copy(wg, wg_hbm, loc_sems.at[3])
        out_wog = pltpu.make_async_copy(wog, wog_hbm, loc_sems.at[4])
        out_mt = pltpu.make_async_copy(mt, mt_hbm, loc_sems.at[5])
        out_cw = pltpu.make_async_copy(cw, cw_hbm, loc_sems.at[6])

        @pl.when(at(0, 0))
        def _():
            own_w.start()
            own_s.start()
            own_o.start()
            for cp in sm_cp + win_cp + wout_cp:
                cp.start()
            own_w.wait()

        @pl.when(at(0, nk - 2))
        def _():
            own_s.wait()
            for cp in sm_in:
                cp.wait_recv()
            mt[...] = jnp.zeros_like(mt)
            cw[...] = jnp.zeros_like(cw)
            for j in range(N_CHIPS):
                mt[TM - N_META:TM, j * 256:(j + 1) * 256] = smg[j, 0:N_META, :]
                cw[0:3, j * 128:(j + 1) * 128] = smg[j, N_META:N_META + 3, 0:128]
            out_mt.start()
            out_cw.start()

        for p, (j_, k_) in enumerate([(0, nk - 1), (1, 1), (2, nk // 2)]):
            @pl.when(at(j_, k_))
            def _(p=p):
                win_in[p].wait_recv()
                win_fw[p].start()

            @pl.when(at(p + 1, 0))
            def _(p=p):
                win_fw_in[p].wait_recv()

        @pl.when(at(3, 0))
        def _():
            out_wg.start()

        @pl.when(at(3, 1))
        def _():
            for p in range(3):
                wout_in[p].wait_recv()
                wout_fw[p].start()

        h = jnp.where(k == nt, mt[...], x_ref[...])
        ms = jnp.mean(h * h, axis=-1, keepdims=True)
        hn = (h * lax.rsqrt(ms + EPS)) * g_ref[...]
        proj_ref[...] = _dot(hn.astype(BF16), wg[order_ref[jj]]).astype(BF16)

        @pl.when(jj == 0)
        def _():
            hnt_ref[...] = hn.T.astype(BF16)

        @pl.when(at(3, nk - 1))
        def _():
            own_o.wait()
            for cp in wout_fw_in:
                cp.wait_recv()
            out_wog.start()
            for cp in sm_cp + win_cp + wout_cp + win_fw + wout_fw:
                cp.wait_send()
            for cp in (out_wg, out_wog, out_mt, out_cw):
                cp.wait()

    grid_spec = pltpu.PrefetchScalarGridSpec(
        num_scalar_prefetch=1,
        grid=(N_CHIPS, nk),
        in_specs=[pl.BlockSpec((TM, D_MODEL), lambda j, k, o: (jnp.minimum(k, nt - 1), 0)),
                  pl.BlockSpec((1, D_MODEL), lambda j, k, o: (0, 0)),
                  any_spec, any_spec, any_spec],
        out_specs=[pl.BlockSpec((TM, 1024), lambda j, k, o: (k, o[j])),
                   pl.BlockSpec((D_MODEL, TM), lambda j, k, o: (0, jnp.where(j == 0, k, nk - 1))),
                   any_spec, any_spec, any_spec, any_spec],
        scratch_shapes=[
            pltpu.VMEM((N_CHIPS, D_MODEL, 1024), BF16),
            pltpu.VMEM((N_CHIPS, 256, D_MODEL), BF16),
            pltpu.VMEM((N_CHIPS, SMALL_ROWS, 256), F32),
            pltpu.VMEM((TM, D_MODEL), F32),
            pltpu.VMEM((8, D_CONV), F32),
            pltpu.SemaphoreType.DMA((15,)), pltpu.SemaphoreType.DMA((15,)), pltpu.SemaphoreType.DMA((7,))])
    return pl.pallas_call(
        body, name="f1_norm_inproj_gather",
        grid_spec=grid_spec,
        out_shape=[jax.ShapeDtypeStruct((rows, N_PROJ_COLS), BF16),
                   jax.ShapeDtypeStruct((D_MODEL, rows), BF16),
                   jax.ShapeDtypeStruct((N_CHIPS, D_MODEL, 1024), BF16),
                   jax.ShapeDtypeStruct((N_CHIPS, 256, D_MODEL), BF16),
                   jax.ShapeDtypeStruct((TM, D_MODEL), F32),
                   jax.ShapeDtypeStruct((8, D_CONV), F32)],
        compiler_params=_cparams(("arbitrary", "arbitrary")),
    )(order, x, g1, win_sh, wout_sh, small_sh)


def _f2_call(proj, conv_w8, gret, tb):
    nt, rows = tb["nt"], tb["rows"]

    def phys(s):
        return jnp.where(s == 0, nt, s - 1)

    def body(proj_ref, cw_ref, g_ref, rc_ref, rs_ref, dec_ref, xi_ref, zeta_ref, cd_ref,
             mixed_ref, conv_ref, states_ref, state, uhalo):
        s = pl.program_id(0)

        @pl.when(s == 0)
        def _():
            state[...] = jnp.zeros_like(state)
            uhalo[...] = jnp.zeros_like(uhalo)

        cx = proj_ref[:, CX:CX + 512].astype(F32)
        cc = proj_ref[:, CC:CC + 512].astype(F32)
        u = cc * cx
        row = lax.broadcasted_iota(jnp.int32, (TM, D_CONV), 0)
        h7 = uhalo[7:8, :]
        h6 = uhalo[6:7, :]
        u1 = jnp.where(row == 0, h7, pltpu.roll(u, 1, 0))
        u2 = jnp.where(row == 0, h6, jnp.where(row == 1, h7, pltpu.roll(u, 2, 0)))
        conv = cw_ref[2:3, :] * u + cw_ref[1:2, :] * u1 + cw_ref[0:1, :] * u2
        uhalo[...] = u[TM - 8:TM, :]
        cb = proj_ref[:, CB:CB + 512].astype(F32)
        cg = proj_ref[:, CG:CG + 512].astype(F32)
        mixed_ref[:, 0:D_CONV] = (cb * conv * (cg * _sigmoid(cg))).astype(BF16)
        conv_ref[...] = conv.astype(BF16)

        scale = HEAD_DIM ** -0.5
        H = range(RET_HEADS)
        st = [state[h] for h in H]
        for c in range(NCH):
            r0 = c * CHUNK
            rc = rc_ref[r0:r0 + CHUNK, :]
            rs = rs_ref[r0:r0 + CHUNK, :]
            col = lambda base, h: slice(base + h * HEAD_DIM, base + (h + 1) * HEAD_DIM)
            rws = slice(r0, r0 + CHUNK)
            v = [proj_ref[rws, col(CV, h)] for h in H]
            qf = [_rot(proj_ref[rws, col(CQ, h)].astype(F32), rc, rs) * scale for h in H]
            kf = [_rot(proj_ref[rws, col(CK, h)].astype(F32), rc, rs) for h in H]
            stb = [t.astype(BF16) for t in st]
            for h in H:
                states_ref[c, h] = stb[h]
            a = [(_dot_tb(qf[h].astype(BF16), kf[h].astype(BF16)) * dec_ref[h]).astype(BF16) for h in H]
            o = [_dot(a[h], v[h]) + _dot((qf[h] * xi_ref[h]).astype(BF16), stb[h]) for h in H]
            st = [cd_ref[h, 0:1, :] * st[h] + _dot_ta((kf[h] * zeta_ref[h]).astype(BF16), v[h]) for h in H]
            for h in H:
                mu = jnp.mean(o[h], axis=-1, keepdims=True)
                d = o[h] - mu
                var = jnp.mean(d * d, axis=-1, keepdims=True)
                yh = d * lax.rsqrt(var + EPS)
                rg = proj_ref[rws, col(CR, h)].astype(F32)
                mixed_ref[rws, col(D_CONV, h)] = (yh * g_ref[:, col(0, h)] * (rg * _sigmoid(rg))).astype(BF16)
        for h in H:
            state[h] = st[h]

    tile = lambda w: pl.BlockSpec((TM, w), lambda s: (phys(s), 0))
    return pl.pallas_call(
        body, name="f2_mixer_fwd",
        grid=(nt + 1,),
        in_specs=[tile(N_PROJ_COLS), _resident((8, D_CONV)), _resident((1, D_RET)),
                  tile(HEAD_DIM), tile(HEAD_DIM),
                  _resident((RET_HEADS, CHUNK, CHUNK)), _resident((RET_HEADS, CHUNK, HEAD_DIM)),
                  _resident((RET_HEADS, CHUNK, HEAD_DIM)), _resident((RET_HEADS, 8, HEAD_DIM))],
        out_specs=[tile(D_MODEL), tile(D_CONV),
                   pl.BlockSpec((NCH, RET_HEADS, HEAD_DIM, HEAD_DIM), lambda s: (phys(s), 0, 0, 0))],
        out_shape=[jax.ShapeDtypeStruct((rows, D_MODEL), BF16),
                   jax.ShapeDtypeStruct((rows, D_CONV), BF16),
                   jax.ShapeDtypeStruct(((nt + 1) * NCH, RET_HEADS, HEAD_DIM, HEAD_DIM), BF16)],
        scratch_shapes=[pltpu.VMEM((RET_HEADS, HEAD_DIM, HEAD_DIM), F32), pltpu.VMEM((8, D_CONV), F32)],
        compiler_params=_cparams(("arbitrary",)),
    )(proj, conv_w8, gret, tb["rc"], tb["rs"], tb["decay"], tb["xi"], tb["zeta"], tb["cd"])


def _f3_call(x, mixed, w_out, fg, target, nt):
    seq = nt * TM

    def body(x_ref, mx_ref, w_ref, g_ref, t_ref, dh2_ref, dmx_ref, gwo_ref, gfg_ref, loss_ref, lacc):
        i = pl.program_id(0)

        @pl.when(i == 0)
        def _():
            gwo_ref[...] = jnp.zeros_like(gwo_ref)
            gfg_ref[...] = jnp.zeros_like(gfg_ref)
            lacc[...] = jnp.zeros_like(lacc)

        mx = mx_ref[...]
        h2 = x_ref[...] + _dot(mx, w_ref[...])
        ms = jnp.mean(h2 * h2, axis=-1, keepdims=True)
        rstd = lax.rsqrt(ms + EPS)
        yh = h2 * rstd
        g = g_ref[...]
        e = yh * g - t_ref[...]
        lacc[...] += jnp.sum(e * e, axis=0, keepdims=True)
        dy = e * (1.0 / D_MODEL)
        gfg_ref[...] += jnp.sum(dy * yh, axis=0, keepdims=True)
        dyh = dy * g
        dh2 = rstd * (dyh - yh * jnp.mean(dyh * yh, axis=-1, keepdims=True))
        dh2_ref[...] = dh2
        db = dh2.astype(BF16)
        dmx_ref[...] = _dot_tb(db, w_ref[...]).astype(BF16)
        gw = _dot_ta(mx, db)
        for j in range(N_CHIPS):
            for hf in range(2):
                r0 = j * 256 + hf * 128
                gwo_ref[hf, j] += gw[r0:r0 + 128, :]

        @pl.when(i == nt - 1)
        def _():
            tot = jnp.sum(lacc[...], axis=1, keepdims=True) * (0.5 / D_MODEL)
            loss_ref[...] = jnp.broadcast_to(tot, (1, 128))

    tile = lambda w: pl.BlockSpec((TM, w), lambda i: (i, 0))
    return pl.pallas_call(
        body, name="f3_outproj_loss",
        grid=(nt,),
        in_specs=[tile(D_MODEL), tile(D_MODEL), _resident((D_MODEL, D_MODEL)), _resident((1, D_MODEL)),
                  tile(D_MODEL)],
        out_specs=[tile(D_MODEL), tile(D_MODEL), _resident((2, N_CHIPS, 128, D_MODEL)), _resident((1, D_MODEL)),
                   _resident((1, 128))],
        out_shape=[jax.ShapeDtypeStruct((seq, D_MODEL), F32),
                   jax.ShapeDtypeStruct((seq, D_MODEL), BF16),
                   jax.ShapeDtypeStruct((2, N_CHIPS, 128, D_MODEL), F32),
                   jax.ShapeDtypeStruct((1, D_MODEL), F32),
                   jax.ShapeDtypeStruct((1, 128), F32)],
        scratch_shapes=[pltpu.VMEM((1, D_MODEL), F32)],
        compiler_params=_cparams(("arbitrary",)),
    )(x, mixed, w_out, fg, target)


def _b2_call(proj, dmixed, conv_s, states, conv_w8, gret, tb):
    nt, rows = tb["nt"], tb["rows"]

    def phys(r):
        return jnp.where(r == nt, nt, nt - 1 - r)

    def body(proj_ref, dmx_ref, conv_ref, states_ref, cw_ref, g_ref, rc_ref, rs_ref, dec_ref, dect_ref, xi_ref,
             zeta_ref, cd_ref, dproj_ref, gcw_ref, gg_ref, gstate, dchalo):
        r = pl.program_id(0)
        live = jnp.where(r == nt, 0.0, 1.0)

        @pl.when(r == 0)
        def _():
            gstate[...] = jnp.zeros_like(gstate)
            dchalo[...] = jnp.zeros_like(dchalo)
            gcw_ref[...] = jnp.zeros_like(gcw_ref)
            gg_ref[...] = jnp.zeros_like(gg_ref)

        cx = proj_ref[:, CX:CX + 512].astype(F32)
        cb = proj_ref[:, CB:CB + 512].astype(F32)
        cc = proj_ref[:, CC:CC + 512].astype(F32)
        cg = proj_ref[:, CG:CG + 512].astype(F32)
        dco = dmx_ref[:, 0:D_CONV].astype(F32) * live
        conv = conv_ref[...].astype(F32)
        sg = _sigmoid(cg)
        sil = cg * sg
        t = dco * conv
        dproj_ref[:, CB:CB + 512] = (t * sil).astype(BF16)
        dproj_ref[:, CG:CG + 512] = (t * cb * (sg * (1.0 + cg * (1.0 - sg)))).astype(BF16)
        dconv = dco * cb * sil
        row = lax.broadcasted_iota(jnp.int32, (TM, D_CONV), 0)
        n0 = dchalo[0:1, :]
        n1 = dchalo[1:2, :]
        dc1 = jnp.where(row == TM - 1, n0, pltpu.roll(dconv, TM - 1, 0))
        dc2 = jnp.where(row == TM - 2, n0, jnp.where(row == TM - 1, n1, pltpu.roll(dconv, TM - 2, 0)))
        dchalo[...] = dconv[0:8, :]
        du = cw_ref[2:3, :] * dconv + cw_ref[1:2, :] * dc1 + cw_ref[0:1, :] * dc2
        u = cc * cx
        gcw_ref[2:3, :] += jnp.sum(u * dconv, axis=0, keepdims=True)
        gcw_ref[1:2, :] += jnp.sum(u * dc1, axis=0, keepdims=True)
        gcw_ref[0:1, :] += jnp.sum(u * dc2, axis=0, keepdims=True)
        dproj_ref[:, CC:CC + 512] = (du * cx).astype(BF16)
        dproj_ref[:, CX:CX + 512] = (du * cc).astype(BF16)

        scale = HEAD_DIM ** -0.5
        H = range(RET_HEADS)
        gs = [gstate[h] for h in H]
        gg = [jnp.zeros((1, HEAD_DIM), F32) for _ in H]
        for c in range(NCH - 1, -1, -1):
            r0 = c * CHUNK
            rc = rc_ref[r0:r0 + CHUNK, :]
            rs = rs_ref[r0:r0 + CHUNK, :]
            col = lambda base, h: slice(base + h * HEAD_DIM, base + (h + 1) * HEAD_DIM)
            rws = slice(r0, r0 + CHUNK)
            v = [proj_ref[rws, col(CV, h)] for h in H]
            stb = [states_ref[c, h] for h in H]
            qf = [_rot(proj_ref[rws, col(CQ, h)].astype(F32), rc, rs) * scale for h in H]
            kf = [_rot(proj_ref[rws, col(CK, h)].astype(F32), rc, rs) for h in H]
            qb = [t.astype(BF16) for t in qf]
            kb = [t.astype(BF16) for t in kf]
            qxb = [(qf[h] * xi_ref[h]).astype(BF16) for h in H]
            kzb = [(kf[h] * zeta_ref[h]).astype(BF16) for h in H]
            gsb = [t.astype(BF16) for t in gs]
            ab = [(_dot_tb(qb[h], kb[h]) * dec_ref[h]).astype(BF16) for h in H]
            atb = [(_dot_tb(kb[h], qb[h]) * dect_ref[h]).astype(BF16) for h in H]
            o = [_dot(ab[h], v[h]) + _dot(qxb[h], stb[h]) for h in H]
            dob = []
            for h in H:
                mu = jnp.mean(o[h], axis=-1, keepdims=True)
                d = o[h] - mu
                var = jnp.mean(d * d, axis=-1, keepdims=True)
                rstd = lax.rsqrt(var + EPS)
                yh = d * rstd
                g = g_ref[:, col(0, h)]
                rg = proj_ref[rws, col(CR, h)].astype(F32)
                dro = dmx_ref[rws, col(D_CONV, h)].astype(F32) * live
                sg = _sigmoid(rg)
                dproj_ref[rws, col(CR, h)] = (dro * (yh * g) * (sg * (1.0 + rg * (1.0 - sg)))).astype(BF16)
                dret = dro * (rg * sg)
                gg[h] = gg[h] + jnp.sum(dret * yh, axis=0, keepdims=True)
                dyh = dret * g
                do = rstd * (dyh - jnp.mean(dyh, axis=-1, keepdims=True)
                             - yh * jnp.mean(dyh * yh, axis=-1, keepdims=True))
                dob.append(do.astype(BF16))
            dv = [_dot(atb[h], dob[h]) + _dot(kzb[h], gsb[h]) for h in H]
            ds = [(_dot_tb(dob[h], v[h]) * dec_ref[h]).astype(BF16) for h in H]
            dst = [(_dot_tb(v[h], dob[h]) * dect_ref[h]).astype(BF16) for h in H]
            dqf = [_dot(ds[h], kb[h]) + _dot_tb(dob[h], stb[h]) * xi_ref[h] for h in H]
            dkf = [_dot(dst[h], qb[h]) + _dot_tb(v[h], gsb[h]) * zeta_ref[h] for h in H]
            gs = [cd_ref[h, 0:1, :] * gs[h] + _dot_ta(qxb[h], dob[h]) for h in H]
            for h in H:
                dproj_ref[rws, col(CQ, h)] = (_rot_t(dqf[h], rc, rs) * scale).astype(BF16)
                dproj_ref[rws, col(CK, h)] = _rot_t(dkf[h], rc, rs).astype(BF16)
                dproj_ref[rws, col(CV, h)] = dv[h].astype(BF16)
        for h in H:
            gstate[h] = gs[h]
            gg_ref[:, h * HEAD_DIM:(h + 1) * HEAD_DIM] += gg[h]

    tile = lambda w: pl.BlockSpec((TM, w), lambda r: (phys(r), 0))
    return pl.pallas_call(
        body, name="b2_mixer_bwd",
        grid=(nt + 1,),
        in_specs=[tile(N_PROJ_COLS),
                  pl.BlockSpec((TM, D_MODEL), lambda r: (jnp.minimum(phys(r), nt - 1), 0)),
                  tile(D_CONV),
                  pl.BlockSpec((NCH, RET_HEADS, HEAD_DIM, HEAD_DIM), lambda r: (phys(r), 0, 0, 0)),
                  _resident((8, D_CONV)), _resident((1, D_RET)),
                  tile(HEAD_DIM), tile(HEAD_DIM),
                  _resident((RET_HEADS, CHUNK, CHUNK)), _resident((RET_HEADS, CHUNK, CHUNK)),
                  _resident((RET_HEADS, CHUNK, HEAD_DIM)),
                  _resident((RET_HEADS, CHUNK, HEAD_DIM)), _resident((RET_HEADS, 8, HEAD_DIM))],
        out_specs=[tile(N_PROJ_COLS), _resident((8, D_CONV)), _resident((1, D_RET))],
        out_shape=[jax.ShapeDtypeStruct((rows, N_PROJ_COLS), BF16),
                   jax.ShapeDtypeStruct((8, D_CONV), F32),
                   jax.ShapeDtypeStruct((1, D_RET), F32)],
        scratch_shapes=[pltpu.VMEM((RET_HEADS, HEAD_DIM, HEAD_DIM), F32), pltpu.VMEM((8, D_CONV), F32)],
        compiler_params=_cparams(("arbitrary",)),
    )(proj, dmixed, conv_s, states, conv_w8, gret, tb["rc"], tb["rs"], tb["decay"], tb["decay_t"], tb["xi"],
      tb["zeta"], tb["cd"])


def _b1a_call(dproj, w_in_g, x, meta_tile, g1, dh2, nt):
    seq = nt * TM

    def dphys(i):
        return jnp.where(i == 0, nt, i - 1)

    def xblk(i):
        return jnp.maximum(i - 1, 0)

    def body(dp_ref, w_ref, x_ref, mt_ref, g_ref, dh2_ref, gx_ref, dmeta_ref, gn_ref):
        i = pl.program_id(0)
        is_meta = i == 0

        @pl.when(is_meta)
        def _():
            gn_ref[...] = jnp.zeros_like(gn_ref)

        dhn = _dot_tb(dp_ref[:, 0:1024], w_ref[0])
        for j in range(1, N_CHIPS):
            dhn += _dot_tb(dp_ref[:, j * 1024:(j + 1) * 1024], w_ref[j])
        h = jnp.where(is_meta, mt_ref[...], x_ref[...])
        ms = jnp.mean(h * h, axis=-1, keepdims=True)
        rstd = lax.rsqrt(ms + EPS)
        xh = h * rstd
        gn_ref[...] += jnp.sum(dhn * xh, axis=0, keepdims=True)
        dxh = dhn * g_ref[...]
        dh = rstd * (dxh - xh * jnp.mean(dxh * xh, axis=-1, keepdims=True))
        gx_ref[...] = dh + jnp.where(is_meta, 0.0, 1.0) * dh2_ref[...]

        @pl.when(is_meta)
        def _():
            dmeta_ref[...] = dh[TM - N_META:TM, :]

    return pl.pallas_call(
        body, name="b1a_inproj_bwd_x",
        grid=(nt + 1,),
        in_specs=[pl.BlockSpec((TM, N_PROJ_COLS), lambda i: (dphys(i), 0)),
                  _resident((N_CHIPS, D_MODEL, 1024)),
                  pl.BlockSpec((TM, D_MODEL), lambda i: (xblk(i), 0)),
                  _resident((TM, D_MODEL)),
                  _resident((1, D_MODEL)),
                  pl.BlockSpec((TM, D_MODEL), lambda i: (xblk(i), 0))],
        out_specs=[pl.BlockSpec((TM, D_MODEL), lambda i: (xblk(i), 0)),
                   _resident((N_META, D_MODEL)), _resident((1, D_MODEL))],
        out_shape=[jax.ShapeDtypeStruct((seq, D_MODEL), F32),
                   jax.ShapeDtypeStruct((N_META, D_MODEL), F32),
                   jax.ShapeDtypeStruct((1, D_MODEL), F32)],
        compiler_params=_cparams(("arbitrary",)),
    )(dproj, w_in_g, x, meta_tile, g1, dh2)


REL = (2, 1, 3)
SMALL_ROWS = 24
HALF_STEP = 2
WOUT_STEP = 4


def _rcopy(src, dst, send_sems, recv_sems, k, to):
    return pltpu.make_async_remote_copy(src_ref=src, dst_ref=dst, send_sem=send_sems.at[k],
                                        recv_sem=recv_sems.at[k], device_id=to, device_id_type=MESH_ID)


def _b1b_reduce_call(order, hnt, dproj, gwo, pack, nt):
    nk = nt + 1
    last = nk - 1
    any_spec = pl.BlockSpec(memory_space=pl.ANY)

    def body(order_ref, a_ref, b_ref, gwo_hbm, pack_hbm, gwin_hbm, gwout_hbm, tot_hbm,
             acc, sb, abuf, pb, bbuf, fin, go, ao, pbo, bo, fino, slots, totv, send_sems, recv_sems, loc_sems):
        jj, k = pl.program_id(0), pl.program_id(1)
        x, y, c = lax.axis_index("x"), lax.axis_index("y"), lax.axis_index("c")
        me, myid, sib = 2 * x + y, 4 * x + 2 * y + c, (x, y, 1 - c)
        rc = functools.partial(_rcopy, send_sems=send_sems, recv_sems=recv_sems)
        peers = [((1 - x) if r & 2 else x, (1 - y) if r & 1 else y, c) for r in REL]
        kids = [jnp.bitwise_xor(me, r) for r in REL]

        def dev_peer(r):
            return ((1 - x) if r & 4 else x, (1 - y) if r & 2 else y, (1 - c) if r & 1 else c)

        own_go = pltpu.make_async_copy(gwo_hbm.at[c], go, loc_sems.at[0])
        own_pack = pltpu.make_async_copy(pack_hbm, slots.at[0], loc_sems.at[1])
        wo_half = rc(gwo_hbm.at[1 - c], ao, k=8, to=sib)
        wo_part = [rc(pbo.at[kids[p]], bo.at[p], k=9 + p, to=peers[p]) for p in range(3)]
        sm = [rc(pack_hbm, slots.at[r], k=12 + r, to=dev_peer(r)) for r in range(1, N_DEV)]
        half = [rc(sb.at[j % 2, 1 - c], abuf.at[j], k=j, to=sib) for j in range(N_CHIPS)]
        part = [rc(pb.at[p], bbuf.at[p], k=4 + p, to=peers[p]) for p in range(3)]

        @pl.when(jnp.logical_and(jj == 0, k == 0))
        def _():
            own_go.start()
            own_pack.start()
            wo_half.start()
            for cp in sm:
                cp.start()

        @pl.when(k == 0)
        def _():
            acc[...] = jnp.zeros_like(acc)

        acc[0] += _dot(a_ref[0:512, :], b_ref[...])
        acc[1] += _dot(a_ref[512:1024, :], b_ref[...])

        @pl.when(jnp.logical_and(jj == 0, k == WOUT_STEP))
        def _():
            own_go.wait()
            wo_half.wait_recv()
            for j in range(N_CHIPS):
                go[j] = go[j] + ao[j]
            pbo[...] = go[...].astype(BF16)
            for cp in wo_part:
                cp.start()

        for j in range(N_CHIPS):
            @pl.when(jnp.logical_and(jj == j, k == last))
            def _(j=j):
                sb[j % 2] = acc[...]
                half[j].start()

        for p in range(3):
            @pl.when(jnp.logical_and(jj == p + 1, k == HALF_STEP))
            def _(p=p):
                half[p].wait_recv()
                half[p].wait_send()
                pb[p] = (sb[p % 2, c] + abuf[p]).astype(BF16)
                part[p].start()

        @pl.when(jnp.logical_and(jj == N_CHIPS - 1, k == last))
        def _():
            half[3].wait_recv()
            own = sb[1, c] + abuf[3]
            for cp in part:
                cp.wait_recv()
            fin[c] = ((own + bbuf[0].astype(F32)) + bbuf[1].astype(F32)) + bbuf[2].astype(F32)
            done = rc(fin.at[c], fin.at[c], k=7, to=sib)
            done.start()
            for cp in wo_part:
                cp.wait_recv()
            fino[c] = ((go[me] + bo[0].astype(F32)) + bo[1].astype(F32)) + bo[2].astype(F32)
            done_o = rc(fino.at[c], fino.at[c], k=12, to=sib)
            done_o.start()
            own_pack.wait()
            for cp in sm:
                cp.wait_recv()
            tot = slots[myid]
            for a in range(1, N_DEV):
                tot = tot + slots[jnp.bitwise_xor(myid, a)]
            totv[...] = tot
            out_t = pltpu.make_async_copy(totv, tot_hbm, loc_sems.at[1])
            out_t.start()
            rc(fin.at[1 - c], fin.at[1 - c], k=7, to=sib).wait_recv()
            out_w = pltpu.make_async_copy(fin, gwin_hbm, loc_sems.at[0])
            out_w.start()
            rc(fino.at[1 - c], fino.at[1 - c], k=12, to=sib).wait_recv()
            out_o = pltpu.make_async_copy(fino, gwout_hbm, loc_sems.at[2])
            out_o.start()
            for cp in [half[3]] + part + [done, wo_half] + wo_part + [done_o] + sm:
                cp.wait_send()
            out_t.wait()
            out_w.wait()
            out_o.wait()

    grid_spec = pltpu.PrefetchScalarGridSpec(
        num_scalar_prefetch=1,
        grid=(N_CHIPS, nk),
        in_specs=[pl.BlockSpec((D_MODEL, TM), lambda j, k, o: (0, k)),
                  pl.BlockSpec((TM, 1024), lambda j, k, o: (k, o[j])),
                  any_spec, any_spec],
        out_specs=[any_spec, any_spec, any_spec],
        scratch_shapes=[
            pltpu.VMEM((2, 512, 1024), F32),
            pltpu.VMEM((2, 2, 512, 1024), F32),
            pltpu.VMEM((N_CHIPS, 512, 1024), F32),
            pltpu.VMEM((3, 512, 1024), BF16),
            pltpu.VMEM((3, 512, 1024), BF16),
            pltpu.VMEM((2, 512, 1024), F32),
            pltpu.VMEM((N_CHIPS, 128, D_MODEL), F32),
            pltpu.VMEM((N_CHIPS, 128, D_MODEL), F32),
            pltpu.VMEM((N_CHIPS, 128, D_MODEL), BF16),
            pltpu.VMEM((3, 128, D_MODEL), BF16),
            pltpu.VMEM((2, 128, D_MODEL), F32),
            pltpu.VMEM((N_DEV, SMALL_ROWS, D_MODEL), F32),
            pltpu.VMEM((SMALL_ROWS, D_MODEL), F32),
            pltpu.SemaphoreType.DMA((20,)), pltpu.SemaphoreType.DMA((20,)), pltpu.SemaphoreType.DMA((3,))])
    return pl.pallas_call(
        body, name="b1b_inproj_bwd_w_reduce",
        grid_spec=grid_spec,
        out_shape=[jax.ShapeDtypeStruct((2, 512, 1024), F32),
                   jax.ShapeDtypeStruct((2, 128, D_MODEL), F32),
                   jax.ShapeDtypeStruct((SMALL_ROWS, D_MODEL), F32)],
        compiler_params=_cparams(("arbitrary", "arbitrary")),
    )(order, hnt, dproj, gwo, pack)


def _local_step(me, x, target, g1, gret, fg, win_sh, wout_sh, small_sh):
    seq = x.shape[0]
    tb = _tables(seq)
    nt = tb["nt"]
    g1r, gretr, fgr = g1.reshape(1, -1), gret.reshape(1, -1), fg.reshape(1, -1)
    order = jnp.stack([me, me ^ REL[0], me ^ REL[1], me ^ REL[2]]).astype(jnp.int32)

    proj, hnt, w_in_g, w_out_g, meta_tile, conv_w8 = _f1_gather_call(order, x, g1r, win_sh, wout_sh, small_sh, nt)
    w_out = w_out_g.reshape(D_MODEL, D_MODEL)
    mixed, conv_s, states = _f2_call(proj, conv_w8, gretr, tb)
    dh2, dmixed, g_wout, g_fg, loss = _f3_call(x, mixed, w_out, fgr, target, nt)
    dproj, g_cw8, g_gret = _b2_call(proj, dmixed, conv_s, states, conv_w8, gretr, tb)
    grad_x, g_meta, g_g1 = _b1a_call(dproj, w_in_g, x, meta_tile, g1r, dh2, nt)
    return loss, grad_x, dict(w_out=g_wout, meta=g_meta, conv_w=g_cw8[0:3], norm1_g=g_g1,
                              ret_norm_g=g_gret, final_g=g_fg), hnt, dproj


def _adamw_call(w, g, m, v, name):
    shape = w.shape
    w2, g2, m2, v2 = (a.reshape(-1, shape[-1]) for a in (w, g, m, v))
    rows, cols = w2.shape
    br = 256 if rows % 256 == 0 else rows

    def body(w_ref, g_ref, m_ref, v_ref, d_ref, nm_ref, nv_ref):
        gg = g_ref[...]
        nm = ADAM_B1 * m_ref[...] + (1.0 - ADAM_B1) * gg
        nv = ADAM_B2 * v_ref[...] + (1.0 - ADAM_B2) * (gg * gg)
        m_hat = nm / (1.0 - ADAM_B1 ** ADAM_STEP)
        v_hat = nv / (1.0 - ADAM_B2 ** ADAM_STEP)
        d_ref[...] = -ADAM_LR * (m_hat / (jnp.sqrt(v_hat) + ADAM_EPS) + ADAM_WD * w_ref[...])
        nm_ref[...] = nm
        nv_ref[...] = nv

    spec = pl.BlockSpec((br, cols), lambda i: (i, 0))
    outs = pl.pallas_call(
        body, name=name, grid=(rows // br,),
        in_specs=[spec] * 4, out_specs=[spec] * 3,
        out_shape=[jax.ShapeDtypeStruct((rows, cols), F32)] * 3,
        compiler_params=_cparams(("arbitrary",)),
    )(w2, g2, m2, v2)
    return tuple(o.reshape(shape) for o in outs)


def _pad_to(a, rows, cols):
    return jnp.pad(a, ((0, rows - a.shape[0]), (0, cols - a.shape[1])))


def kernel(x, meta, norm1_g, w_in, conv_w, ret_norm_g, w_out, final_g, loss_target, m_meta, m_norm1_g, m_w_in, m_conv_w, m_ret_norm_g, m_w_out, m_final_g, v_meta, v_norm1_g, v_w_in, v_conv_w, v_ret_norm_g, v_w_out, v_final_g):
    me = 2 * lax.axis_index("x") + lax.axis_index("y")

    small_sh = jnp.concatenate([meta, _pad_to(conv_w, 8, 256)], axis=0)
    loss, grad_x, g, hnt, dproj = _local_step(me, x[0], loss_target[0], norm1_g, ret_norm_g, final_g,
                                              w_in.astype(BF16), w_out.astype(BF16), small_sh)

    vec = jnp.concatenate([g["norm1_g"], g["final_g"], _pad_to(g["ret_norm_g"], 1, D_MODEL),
                           _pad_to(g["conv_w"], 3, D_MODEL), _pad_to(loss, 2, D_MODEL)], axis=0)
    pack = jnp.concatenate([g["meta"], vec], axis=0)
    order = jnp.stack([me ^ REL[0], me ^ REL[1], me ^ REL[2], me]).astype(jnp.int32)
    g_win, g_wout, tot = _b1b_reduce_call(order, hnt, dproj, g["w_out"], pack, x.shape[1] // TM)
    g_win, g_wout = g_win.reshape(D_MODEL, 1024), g_wout.reshape(256, D_MODEL)
    g_meta = lax.dynamic_slice(tot, (0, me * 256), (N_META, 256))
    g_conv = lax.dynamic_slice(tot, (N_META + 3, me * 128), (3, 128))
    g_n1, g_fg, g_rn = tot[N_META], tot[N_META + 1], tot[N_META + 2, 0:D_RET]

    loss_tot = tot[N_META + 6, 0]

    grads = [g_meta, g_n1, g_win, g_conv, g_rn, g_wout, g_fg]
    ws = [meta, norm1_g, w_in, conv_w, ret_norm_g, w_out, final_g]
    ms = [m_meta, m_norm1_g, m_w_in, m_conv_w, m_ret_norm_g, m_w_out, m_final_g]
    vs = [v_meta, v_norm1_g, v_w_in, v_conv_w, v_ret_norm_g, v_w_out, v_final_g]
    names = ["meta", "norm1_g", "w_in", "conv_w", "ret_norm_g", "w_out", "final_g"]
    deltas, new_ms, new_vs = [], [], []
    for w_, g_, m_, v_, n_ in zip(ws, grads, ms, vs, names):
        as2d = (lambda a: a.reshape(1, -1)) if w_.ndim == 1 else (lambda a: a)
        d_, nm_, nv_ = _adamw_call(as2d(w_), as2d(g_), as2d(m_), as2d(v_), "adamw_" + n_)
        deltas.append(d_.reshape(w_.shape))
        new_ms.append(nm_.reshape(w_.shape))
        new_vs.append(nv_.reshape(w_.shape))
    return (loss_tot, grad_x[None], *grads, *deltas, *new_ms, *new_vs)
```

```python
import functools

import jax
import jax.numpy as jnp
from jax import lax
from jax.experimental import pallas as pl
from jax.experimental.pallas import tpu as pltpu

F32 = jnp.float32
BF16 = jnp.bfloat16

D_MODEL = 1024
N_META = 16
D_CONV = 512
D_RET = 512
RET_HEADS = 4
HEAD_DIM = 128
CHUNK = 128
N_PROJ_COLS = 4096
ROPE_BASE = 10000.0
EPS = 1e-6
N_CHIPS = 4
N_DEV = 8

ADAM_LR = 0.001
ADAM_B1 = 0.9
ADAM_B2 = 0.999
ADAM_EPS = 1e-08
ADAM_WD = 0.01
ADAM_STEP = 10

TM = 512
NCH = TM // CHUNK
VMEM_LIMIT = 56 * 1024 * 1024

CX, CB, CC, CG, CQ, CK, CV, CR = (i * 512 for i in range(8))

MESH_ID = pl.DeviceIdType.MESH


def _cparams(sem=None, **kw):
    return pltpu.CompilerParams(dimension_semantics=sem, vmem_limit_bytes=VMEM_LIMIT, **kw)


def _sigmoid(x):
    return 1.0 / (1.0 + jnp.exp(-x))


def _dot(a, b):
    return jnp.dot(a, b, preferred_element_type=F32)


def _dot_tb(a, b):
    return lax.dot_general(a, b, (((1,), (1,)), ((), ())), preferred_element_type=F32)


def _dot_ta(a, b):
    return lax.dot_general(a, b, (((0,), (0,)), ((), ())), preferred_element_type=F32)


def _resident(shape):
    nd = len(shape)
    return pl.BlockSpec(shape, lambda *_: (0,) * nd)


def _tables(seq):
    nt = seq // TM
    rows = seq + TM
    half = HEAD_DIM // 2
    freqs = 1.0 / (ROPE_BASE ** (jnp.arange(half, dtype=F32) / half))
    ang_t = (jnp.arange(nt, dtype=F32) * TM)[:, None] * freqs[None, :]
    ang_r = (jnp.arange(TM, dtype=F32) + N_META)[:, None] * freqs[None, :]
    ct, st, cr, sr = jnp.cos(ang_t), jnp.sin(ang_t), jnp.cos(ang_r), jnp.sin(ang_r)
    cos_x = (ct[:, None, :] * cr[None] - st[:, None, :] * sr[None]).reshape(seq, half)
    sin_x = (st[:, None, :] * cr[None] + ct[:, None, :] * sr[None]).reshape(seq, half)
    ang_m = jnp.maximum(jnp.arange(TM, dtype=F32) - (TM - N_META), 0.0)[:, None] * freqs[None, :]
    cos = jnp.concatenate([cos_x, jnp.cos(ang_m)], axis=0)
    sin = jnp.concatenate([sin_x, jnp.sin(ang_m)], axis=0)
    rc = jnp.concatenate([cos, cos], axis=-1)
    rs = jnp.concatenate([-sin, sin], axis=-1)
    log_g = jnp.log(1.0 - 2.0 ** (-5.0 - jnp.arange(RET_HEADS, dtype=F32)))
    idx = jnp.arange(CHUNK, dtype=F32)
    diff = idx[:, None] - idx[None, :]
    decay = jnp.where(diff[None] >= 0, jnp.exp(diff[None] * log_g[:, None, None]), 0.0)
    zeta = jnp.exp((CHUNK - 1 - idx)[None, :] * log_g[:, None])
    xi = jnp.exp((idx + 1.0)[None, :] * log_g[:, None])
    cd = jnp.exp(CHUNK * log_g)
    zeta_b = jnp.broadcast_to(zeta[:, :, None], (RET_HEADS, CHUNK, HEAD_DIM))
    xi_b = jnp.broadcast_to(xi[:, :, None], (RET_HEADS, CHUNK, HEAD_DIM))
    cd_b = jnp.broadcast_to(cd[:, None, None], (RET_HEADS, 8, HEAD_DIM))
    return dict(nt=nt, rows=rows, rc=rc, rs=rs, decay=decay, decay_t=jnp.swapaxes(decay, 1, 2), zeta=zeta_b,
                xi=xi_b, cd=cd_b)


def _rot(t, rc, rs):
    return t * rc + pltpu.roll(t, HEAD_DIM // 2, 1) * rs


def _rot_t(dt, rc, rs):
    return dt * rc + pltpu.roll(dt * rs, HEAD_DIM // 2, 1)


def _f1_gather_call(order, x, g1, win_sh, wout_sh, small_sh, nt):
    nk = nt + 1
    rows = nk * TM
    any_spec = pl.BlockSpec(memory_space=pl.ANY)

    def body(order_ref, x_ref, g_ref, win_hbm, wout_hbm, sm_hbm,
             proj_ref, hnt_ref, wg_hbm, wog_hbm, mt_hbm, cw_hbm,
             wg, wog, smg, mt, cw, hbs, send_sems, recv_sems, loc_sems):
        jj, k = pl.program_id(0), pl.program_id(1)
        x, y, c = lax.axis_index("x"), lax.axis_index("y"), lax.axis_index("c")
        me, sib = 2 * x + y, (x, y, 1 - c)
        rc = functools.partial(_rcopy, send_sems=send_sems, recv_sems=recv_sems)
        peers = [((1 - x) if r & 2 else x, (1 - y) if r & 1 else y, c) for r in REL]
        kids = [jnp.bitwise_xor(me, r) for r in REL]
        hw, ho = pl.ds(c * 512, 512), pl.ds(c * 128, 128)
        hw2, ho2 = pl.ds((1 - c) * 512, 512), pl.ds((1 - c) * 128, 128)
        at = lambda j_, k_: jnp.logical_and(jj == j_, k == k_)

        sm_cp = [rc(sm_hbm, smg.at[me], k=p, to=peers[p]) for p in range(3)]
        win_cp = [rc(win_hbm.at[hw], wg.at[me, hw], k=3 + p, to=peers[p]) for p in range(3)]
        wout_cp = [rc(wout_hbm.at[ho], wog.at[me, ho], k=6 + p, to=peers[p]) for p in range(3)]
        sm_in = [rc(sm_hbm, smg.at[kids[p]], k=p, to=sib) for p in range(3)]
        win_in = [rc(win_hbm.at[hw], wg.at[kids[p], hw], k=3 + p, to=sib) for p in range(3)]
        wout_in = [rc(wout_hbm.at[ho], wog.at[kids[p], ho], k=6 + p, to=sib) for p in range(3)]
        win_fw = [rc(wg.at[kids[p], hw], wg.at[kids[p], hw], k=9 + p, to=sib) for p in range(3)]
        wout_fw = [rc(wog.at[kids[p], ho], wog.at[kids[p], ho], k=12 + p, to=sib) for p in range(3)]
        win_fw_in = [rc(wg.at[kids[p], hw2], wg.at[kids[p], hw2], k=9 + p, to=sib) for p in range(3)]
        wout_fw_in = [rc(wog.at[kids[p], ho2], wog.at[kids[p], ho2], k=12 + p, to=sib) for p in range(3)]
        own_w = pltpu.make_async_copy(win_hbm, wg.at[me], loc_sems.at[0])
        own_o = pltpu.make_async_copy(wout_hbm, wog.at[me], loc_sems.at[1])
        own_s = pltpu.make_async_copy(sm_hbm, smg.at[me], loc_sems.at[2])
        out_wg = pltpu.make_async_copy(wg, wg_hbm, loc_sems.at[3])
        out_wog = pltpu.make_async_copy(wog, wog_hbm, loc_sems.at[4])
        out_mt = pltpu.make_async_copy(mt, mt_hbm, loc_sems.at[5])
        out_cw = pltpu.make_async_copy(cw, cw_hbm, loc_sems.at[6])

        @pl.when(at(0, 0))
        def _():
            own_w.start()
            own_s.start()
            own_o.start()
            for cp in sm_cp + win_cp + wout_cp:
                cp.start()
            own_w.wait()

        @pl.when(at(0, nk - 2))
        def _():
            own_s.wait()
            for cp in sm_in:
                cp.wait_recv()
            mt[...] = jnp.zeros_like(mt)
            cw[...] = jnp.zeros_like(cw)
            for j in range(N_CHIPS):
                mt[TM - N_META:TM, j * 256:(j + 1) * 256] = smg[j, 0:N_META, :]
                cw[0:3, j * 128:(j + 1) * 128] = smg[j, N_META:N_META + 3, 0:128]
            out_mt.start()
            out_cw.start()

        for p, (j_, k_) in enumerate([(0, nk - 1), (1, 1), (2, nk // 2)]):
            @pl.when(at(j_, k_))
            def _(p=p):
                win_in[p].wait_recv()
                win_fw[p].start()

            @pl.when(at(p + 1, 0))
            def _(p=p):
                win_fw_in[p].wait_recv()

        @pl.when(at(3, 0))
        def _():
            out_wg.start()

        @pl.when(at(3, 1))
        def _():
            for p in range(3):
                wout_in[p].wait_recv()
                wout_fw[p].start()

        tile_rows = pl.ds(pl.multiple_of(k * TM, TM), TM)

        @pl.when(jj == 0)
        def _():
            h = jnp.where(k == nt, mt[...], x_ref[...])
            ms = jnp.mean(h * h, axis=-1, keepdims=True)
            hn = (h * lax.rsqrt(ms + EPS)) * g_ref[...]
            hb = hn.astype(BF16)
            hbs[tile_rows, :] = hb
            proj_ref[...] = _dot(hb, wg[order_ref[0]]).astype(BF16)
            hnt_ref[...] = hn.T.astype(BF16)

        @pl.when(jj > 0)
        def _():
            proj_ref[...] = _dot(hbs[tile_rows, :], wg[order_ref[jj]]).astype(BF16)

        @pl.when(at(3, nk - 1))
        def _():
            own_o.wait()
            for cp in wout_fw_in:
                cp.wait_recv()
            out_wog.start()
            for cp in sm_cp + win_cp + wout_cp + win_fw + wout_fw:
                cp.wait_send()
            for cp in (out_wg, out_wog, out_mt, out_cw):
                cp.wait()

    grid_spec = pltpu.PrefetchScalarGridSpec(
        num_scalar_prefetch=1,
        grid=(N_CHIPS, nk),
        in_specs=[pl.BlockSpec((TM, D_MODEL), lambda j, k, o: (jnp.where(j == 0, jnp.minimum(k, nt - 1), nt - 1), 0)),
                  pl.BlockSpec((1, D_MODEL), lambda j, k, o: (0, 0)),
                  any_spec, any_spec, any_spec],
        out_specs=[pl.BlockSpec((TM, 1024), lambda j, k, o: (k, o[j])),
                   pl.BlockSpec((D_MODEL, TM), lambda j, k, o: (0, jnp.where(j == 0, k, nk - 1))),
                   any_spec, any_spec, any_spec, any_spec],
        scratch_shapes=[
            pltpu.VMEM((N_CHIPS, D_MODEL, 1024), BF16),
            pltpu.VMEM((N_CHIPS, 256, D_MODEL), BF16),
            pltpu.VMEM((N_CHIPS, SMALL_ROWS, 256), F32),
            pltpu.VMEM((TM, D_MODEL), F32),
            pltpu.VMEM((8, D_CONV), F32),
            pltpu.VMEM((rows, D_MODEL), BF16),
            pltpu.SemaphoreType.DMA((15,)), pltpu.SemaphoreType.DMA((15,)), pltpu.SemaphoreType.DMA((7,))])
    return pl.pallas_call(
        body, name="f1_norm_inproj_gather",
        grid_spec=grid_spec,
        out_shape=[jax.ShapeDtypeStruct((rows, N_PROJ_COLS), BF16),
                   jax.ShapeDtypeStruct((D_MODEL, rows), BF16),
                   jax.ShapeDtypeStruct((N_CHIPS, D_MODEL, 1024), BF16),
                   jax.ShapeDtypeStruct((N_CHIPS, 256, D_MODEL), BF16),
                   jax.ShapeDtypeStruct((TM, D_MODEL), F32),
                   jax.ShapeDtypeStruct((8, D_CONV), F32)],
        compiler_params=_cparams(("arbitrary", "arbitrary")),
    )(order, x, g1, win_sh, wout_sh, small_sh)


def _f2_call(proj, conv_w8, gret, tb):
    nt, rows = tb["nt"], tb["rows"]

    def phys(s):
        return jnp.where(s == 0, nt, s - 1)

    def body(proj_ref, cw_ref, g_ref, rc_ref, rs_ref, dec_ref, xi_ref, zeta_ref, cd_ref,
             mixed_ref, conv_ref, states_ref, state, uhalo):
        s = pl.program_id(0)

        @pl.when(s == 0)
        def _():
            state[...] = jnp.zeros_like(state)
            uhalo[...] = jnp.zeros_like(uhalo)

        cx = proj_ref[:, CX:CX + 512].astype(F32)
        cc = proj_ref[:, CC:CC + 512].astype(F32)
        u = cc * cx
        row = lax.broadcasted_iota(jnp.int32, (TM, D_CONV), 0)
        h7 = uhalo[7:8, :]
        h6 = uhalo[6:7, :]
        u1 = jnp.where(row == 0, h7, pltpu.roll(u, 1, 0))
        u2 = jnp.where(row == 0, h6, jnp.where(row == 1, h7, pltpu.roll(u, 2, 0)))
        conv = cw_ref[2:3, :] * u + cw_ref[1:2, :] * u1 + cw_ref[0:1, :] * u2
        uhalo[...] = u[TM - 8:TM, :]
        cb = proj_ref[:, CB:CB + 512].astype(F32)
        cg = proj_ref[:, CG:CG + 512].astype(F32)
        mixed_ref[:, 0:D_CONV] = (cb * conv * (cg * _sigmoid(cg))).astype(BF16)
        conv_ref[...] = conv.astype(BF16)

        scale = HEAD_DIM ** -0.5
        H = range(RET_HEADS)
        st = [state[h] for h in H]
        for c in range(NCH):
            r0 = c * CHUNK
            rc = rc_ref[r0:r0 + CHUNK, :]
            rs = rs_ref[r0:r0 + CHUNK, :]
            col = lambda base, h: slice(base + h * HEAD_DIM, base + (h + 1) * HEAD_DIM)
            rws = slice(r0, r0 + CHUNK)
            v = [proj_ref[rws, col(CV, h)] for h in H]
            qf = [_rot(proj_ref[rws, col(CQ, h)].astype(F32), rc, rs) * scale for h in H]
            kf = [_rot(proj_ref[rws, col(CK, h)].astype(F32), rc, rs) for h in H]
            stb = [t.astype(BF16) for t in st]
            for h in H:
                states_ref[c, h] = stb[h]
            a = [(_dot_tb(qf[h].astype(BF16), kf[h].astype(BF16)) * dec_ref[h]).astype(BF16) for h in H]
            o = [_dot(a[h], v[h]) + _dot((qf[h] * xi_ref[h]).astype(BF16), stb[h]) for h in H]
            st = [cd_ref[h, 0:1, :] * st[h] + _dot_ta((kf[h] * zeta_ref[h]).astype(BF16), v[h]) for h in H]
            for h in H:
                mu = jnp.mean(o[h], axis=-1, keepdims=True)
                d = o[h] - mu
                var = jnp.mean(d * d, axis=-1, keepdims=True)
                yh = d * lax.rsqrt(var + EPS)
                rg = proj_ref[rws, col(CR, h)].astype(F32)
                mixed_ref[rws, col(D_CONV, h)] = (yh * g_ref[:, col(0, h)] * (rg * _sigmoid(rg))).astype(BF16)
        for h in H:
            state[h] = st[h]

    tile = lambda w: pl.BlockSpec((TM, w), lambda s: (phys(s), 0))
    return pl.pallas_call(
        body, name="f2_mixer_fwd",
        grid=(nt + 1,),
        in_specs=[tile(N_PROJ_COLS), _resident((8, D_CONV)), _resident((1, D_RET)),
                  tile(HEAD_DIM), tile(HEAD_DIM),
                  _resident((RET_HEADS, CHUNK, CHUNK)), _resident((RET_HEADS, CHUNK, HEAD_DIM)),
                  _resident((RET_HEADS, CHUNK, HEAD_DIM)), _resident((RET_HEADS, 8, HEAD_DIM))],
        out_specs=[tile(D_MODEL), tile(D_CONV),
                   pl.BlockSpec((NCH, RET_HEADS, HEAD_DIM, HEAD_DIM), lambda s: (phys(s), 0, 0, 0))],
        out_shape=[jax.ShapeDtypeStruct((rows, D_MODEL), BF16),
                   jax.ShapeDtypeStruct((rows, D_CONV), BF16),
                   jax.ShapeDtypeStruct(((nt + 1) * NCH, RET_HEADS, HEAD_DIM, HEAD_DIM), BF16)],
        scratch_shapes=[pltpu.VMEM((RET_HEADS, HEAD_DIM, HEAD_DIM), F32), pltpu.VMEM((8, D_CONV), F32)],
        compiler_params=_cparams(("arbitrary",)),
    )(proj, conv_w8, gret, tb["rc"], tb["rs"], tb["decay"], tb["xi"], tb["zeta"], tb["cd"])


def _f3_call(x, mixed, w_out, fg, target, nt):
    seq = nt * TM

    def body(x_ref, mx_ref, w_ref, g_ref, t_ref, dh2_ref, dmx_ref, gwo_ref, gfg_ref, loss_ref, lacc):
        i = pl.program_id(0)

        @pl.when(i == 0)
        def _():
            gwo_ref[...] = jnp.zeros_like(gwo_ref)
            gfg_ref[...] = jnp.zeros_like(gfg_ref)
            lacc[...] = jnp.zeros_like(lacc)

        mx = mx_ref[...]
        h2 = x_ref[...] + _dot(mx, w_ref[...])
        ms = jnp.mean(h2 * h2, axis=-1, keepdims=True)
        rstd = lax.rsqrt(ms + EPS)
        yh = h2 * rstd
        g = g_ref[...]
        e = yh * g - t_ref[...]
        lacc[...] += jnp.sum(e * e, axis=0, keepdims=True)
        dy = e * (1.0 / D_MODEL)
        gfg_ref[...] += jnp.sum(dy * yh, axis=0, keepdims=True)
        dyh = dy * g
        dh2 = rstd * (dyh - yh * jnp.mean(dyh * yh, axis=-1, keepdims=True))
        dh2_ref[...] = dh2
        db = dh2.astype(BF16)
        dmx_ref[...] = _dot_tb(db, w_ref[...]).astype(BF16)
        gw = _dot_ta(mx, db)
        for j in range(N_CHIPS):
            for hf in range(2):
                r0 = j * 256 + hf * 128
                gwo_ref[hf, j] += gw[r0:r0 + 128, :]

        @pl.when(i == nt - 1)
        def _():
            tot = jnp.sum(lacc[...], axis=1, keepdims=True) * (0.5 / D_MODEL)
            loss_ref[...] = jnp.broadcast_to(tot, (1, 128))

    tile = lambda w: pl.BlockSpec((TM, w), lambda i: (i, 0))
    return pl.pallas_call(
        body, name="f3_outproj_loss",
        grid=(nt,),
        in_specs=[tile(D_MODEL), tile(D_MODEL), _resident((D_MODEL, D_MODEL)), _resident((1, D_MODEL)),
                  tile(D_MODEL)],
        out_specs=[tile(D_MODEL), tile(D_MODEL), _resident((2, N_CHIPS, 128, D_MODEL)), _resident((1, D_MODEL)),
                   _resident((1, 128))],
        out_shape=[jax.ShapeDtypeStruct((seq, D_MODEL), F32),
                   jax.ShapeDtypeStruct((seq, D_MODEL), BF16),
                   jax.ShapeDtypeStruct((2, N_CHIPS, 128, D_MODEL), F32),
                   jax.ShapeDtypeStruct((1, D_MODEL), F32),
                   jax.ShapeDtypeStruct((1, 128), F32)],
        scratch_shapes=[pltpu.VMEM((1, D_MODEL), F32)],
        compiler_params=_cparams(("arbitrary",)),
    )(x, mixed, w_out, fg, target)


def _b2_call(proj, dmixed, conv_s, states, conv_w8, gret, tb):
    nt, rows = tb["nt"], tb["rows"]

    def phys(r):
        return jnp.where(r == nt, nt, nt - 1 - r)

    def body(proj_ref, dmx_ref, conv_ref, states_ref, cw_ref, g_ref, rc_ref, rs_ref, dec_ref, dect_ref, xi_ref,
             zeta_ref, cd_ref, dproj_ref, gcw_ref, gg_ref, gstate, dchalo):
        r = pl.program_id(0)
        live = jnp.where(r == nt, 0.0, 1.0)

        @pl.when(r == 0)
        def _():
            gstate[...] = jnp.zeros_like(gstate)
            dchalo[...] = jnp.zeros_like(dchalo)
            gcw_ref[...] = jnp.zeros_like(gcw_ref)
            gg_ref[...] = jnp.zeros_like(gg_ref)

        cx = proj_ref[:, CX:CX + 512].astype(F32)
        cb = proj_ref[:, CB:CB + 512].astype(F32)
        cc = proj_ref[:, CC:CC + 512].astype(F32)
        cg = proj_ref[:, CG:CG + 512].astype(F32)
        dco = dmx_ref[:, 0:D_CONV].astype(F32) * live
        conv = conv_ref[...].astype(F32)
        sg = _sigmoid(cg)
        sil = cg * sg
        t = dco * conv
        dproj_ref[:, CB:CB + 512] = (t * sil).astype(BF16)
        dproj_ref[:, CG:CG + 512] = (t * cb * (sg * (1.0 + cg * (1.0 - sg)))).astype(BF16)
        dconv = dco * cb * sil
        row = lax.broadcasted_iota(jnp.int32, (TM, D_CONV), 0)
        n0 = dchalo[0:1, :]
        n1 = dchalo[1:2, :]
        dc1 = jnp.where(row == TM - 1, n0, pltpu.roll(dconv, TM - 1, 0))
        dc2 = jnp.where(row == TM - 2, n0, jnp.where(row == TM - 1, n1, pltpu.roll(dconv, TM - 2, 0)))
        dchalo[...] = dconv[0:8, :]
        du = cw_ref[2:3, :] * dconv + cw_ref[1:2, :] * dc1 + cw_ref[0:1, :] * dc2
        u = cc * cx
        gcw_ref[2:3, :] += jnp.sum(u * dconv, axis=0, keepdims=True)
        gcw_ref[1:2, :] += jnp.sum(u * dc1, axis=0, keepdims=True)
        gcw_ref[0:1, :] += jnp.sum(u * dc2, axis=0, keepdims=True)
        dproj_ref[:, CC:CC + 512] = (du * cx).astype(BF16)
        dproj_ref[:, CX:CX + 512] = (du * cc).astype(BF16)

        scale = HEAD_DIM ** -0.5
        H = range(RET_HEADS)
        gs = [gstate[h] for h in H]
        gg = [jnp.zeros((1, HEAD_DIM), F32) for _ in H]
        for c in range(NCH - 1, -1, -1):
            r0 = c * CHUNK
            rc = rc_ref[r0:r0 + CHUNK, :]
            rs = rs_ref[r0:r0 + CHUNK, :]
            col = lambda base, h: slice(base + h * HEAD_DIM, base + (h + 1) * HEAD_DIM)
            rws = slice(r0, r0 + CHUNK)
            v = [proj_ref[rws, col(CV, h)] for h in H]
            stb = [states_ref[c, h] for h in H]
            qf = [_rot(proj_ref[rws, col(CQ, h)].astype(F32), rc, rs) * scale for h in H]
            kf = [_rot(proj_ref[rws, col(CK, h)].astype(F32), rc, rs) for h in H]
            qb = [t.astype(BF16) for t in qf]
            kb = [t.astype(BF16) for t in kf]
            qxb = [(qf[h] * xi_ref[h]).astype(BF16) for h in H]
            kzb = [(kf[h] * zeta_ref[h]).astype(BF16) for h in H]
            gsb = [t.astype(BF16) for t in gs]
            ab = [(_dot_tb(qb[h], kb[h]) * dec_ref[h]).astype(BF16) for h in H]
            atb = [(_dot_tb(kb[h], qb[h]) * dect_ref[h]).astype(BF16) for h in H]
            o = [_dot(ab[h], v[h]) + _dot(qxb[h], stb[h]) for h in H]
            dob = []
            for h in H:
                mu = jnp.mean(o[h], axis=-1, keepdims=True)
                d = o[h] - mu
                var = jnp.mean(d * d, axis=-1, keepdims=True)
                rstd = lax.rsqrt(var + EPS)
                yh = d * rstd
                g = g_ref[:, col(0, h)]
                rg = proj_ref[rws, col(CR, h)].astype(F32)
                dro = dmx_ref[rws, col(D_CONV, h)].astype(F32) * live
                sg = _sigmoid(rg)
                dproj_ref[rws, col(CR, h)] = (dro * (yh * g) * (sg * (1.0 + rg * (1.0 - sg)))).astype(BF16)
                dret = dro * (rg * sg)
                gg[h] = gg[h] + jnp.sum(dret * yh, axis=0, keepdims=True)
                dyh = dret * g
                do = rstd * (dyh - jnp.mean(dyh, axis=-1, keepdims=True)
                             - yh * jnp.mean(dyh * yh, axis=-1, keepdims=True))
                dob.append(do.astype(BF16))
            dv = [_dot(atb[h], dob[h]) + _dot(kzb[h], gsb[h]) for h in H]
            ds = [(_dot_tb(dob[h], v[h]) * dec_ref[h]).astype(BF16) for h in H]
            dst = [(_dot_tb(v[h], dob[h]) * dect_ref[h]).astype(BF16) for h in H]
            dqf = [_dot(ds[h], kb[h]) + _dot_tb(dob[h], stb[h]) * xi_ref[h] for h in H]
            dkf = [_dot(dst[h], qb[h]) + _dot_tb(v[h], gsb[h]) * zeta_ref[h] for h in H]
            gs = [cd_ref[h, 0:1, :] * gs[h] + _dot_ta(qxb[h], dob[h]) for h in H]
            for h in H:
                dproj_ref[rws, col(CQ, h)] = (_rot_t(dqf[h], rc, rs) * scale).astype(BF16)
                dproj_ref[rws, col(CK, h)] = _rot_t(dkf[h], rc, rs).astype(BF16)
                dproj_ref[rws, col(CV, h)] = dv[h].astype(BF16)
        for h in H:
            gstate[h] = gs[h]
            gg_ref[:, h * HEAD_DIM:(h + 1) * HEAD_DIM] += gg[h]

    tile = lambda w: pl.BlockSpec((TM, w), lambda r: (phys(r), 0))
    return pl.pallas_call(
        body, name="b2_mixer_bwd",
        grid=(nt + 1,),
        in_specs=[tile(N_PROJ_COLS),
                  pl.BlockSpec((TM, D_MODEL), lambda r: (jnp.minimum(phys(r), nt - 1), 0)),
                  tile(D_CONV),
                  pl.BlockSpec((NCH, RET_HEADS, HEAD_DIM, HEAD_DIM), lambda r: (phys(r), 0, 0, 0)),
                  _resident((8, D_CONV)), _resident((1, D_RET)),
                  tile(HEAD_DIM), tile(HEAD_DIM),
                  _resident((RET_HEADS, CHUNK, CHUNK)), _resident((RET_HEADS, CHUNK, CHUNK)),
                  _resident((RET_HEADS, CHUNK, HEAD_DIM)),
                  _resident((RET_HEADS, CHUNK, HEAD_DIM)), _resident((RET_HEADS, 8, HEAD_DIM))],
        out_specs=[tile(N_PROJ_COLS), _resident((8, D_CONV)), _resident((1, D_RET))],
        out_shape=[jax.ShapeDtypeStruct((rows, N_PROJ_COLS), BF16),
                   jax.ShapeDtypeStruct((8, D_CONV), F32),
                   jax.ShapeDtypeStruct((1, D_RET), F32)],
        scratch_shapes=[pltpu.VMEM((RET_HEADS, HEAD_DIM, HEAD_DIM), F32), pltpu.VMEM((8, D_CONV), F32)],
        compiler_params=_cparams(("arbitrary",)),
    )(proj, dmixed, conv_s, states, conv_w8, gret, tb["rc"], tb["rs"], tb["decay"], tb["decay_t"], tb["xi"],
      tb["zeta"], tb["cd"])


def _b1a_call(dproj, w_in_g, x, meta_tile, g1, dh2, nt):
    seq = nt * TM

    def dphys(i):
        return jnp.where(i == 0, nt, i - 1)

    def xblk(i):
        return jnp.maximum(i - 1, 0)

    def body(dp_ref, w_ref, x_ref, mt_ref, g_ref, dh2_ref, gx_ref, dmeta_ref, gn_ref):
        i = pl.program_id(0)
        is_meta = i == 0

        @pl.when(is_meta)
        def _():
            gn_ref[...] = jnp.zeros_like(gn_ref)

        dhn = _dot_tb(dp_ref[:, 0:1024], w_ref[0])
        for j in range(1, N_CHIPS):
            dhn += _dot_tb(dp_ref[:, j * 1024:(j + 1) * 1024], w_ref[j])
        h = jnp.where(is_meta, mt_ref[...], x_ref[...])
        ms = jnp.mean(h * h, axis=-1, keepdims=True)
        rstd = lax.rsqrt(ms + EPS)
        xh = h * rstd
        gn_ref[...] += jnp.sum(dhn * xh, axis=0, keepdims=True)
        dxh = dhn * g_ref[...]
        dh = rstd * (dxh - xh * jnp.mean(dxh * xh, axis=-1, keepdims=True))
        gx_ref[...] = dh + jnp.where(is_meta, 0.0, 1.0) * dh2_ref[...]

        @pl.when(is_meta)
        def _():
            dmeta_ref[...] = dh[TM - N_META:TM, :]

    return pl.pallas_call(
        body, name="b1a_inproj_bwd_x",
        grid=(nt + 1,),
        in_specs=[pl.BlockSpec((TM, N_PROJ_COLS), lambda i: (dphys(i), 0)),
                  _resident((N_CHIPS, D_MODEL, 1024)),
                  pl.BlockSpec((TM, D_MODEL), lambda i: (xblk(i), 0)),
                  _resident((TM, D_MODEL)),
                  _resident((1, D_MODEL)),
                  pl.BlockSpec((TM, D_MODEL), lambda i: (xblk(i), 0))],
        out_specs=[pl.BlockSpec((TM, D_MODEL), lambda i: (xblk(i), 0)),
                   _resident((N_META, D_MODEL)), _resident((1, D_MODEL))],
        out_shape=[jax.ShapeDtypeStruct((seq, D_MODEL), F32),
                   jax.ShapeDtypeStruct((N_META, D_MODEL), F32),
                   jax.ShapeDtypeStruct((1, D_MODEL), F32)],
        compiler_params=_cparams(("arbitrary",)),
    )(dproj, w_in_g, x, meta_tile, g1, dh2)


REL = (2, 1, 3)
SMALL_ROWS = 24
HALF_STEP = 2
WOUT_STEP = 4


def _rcopy(src, dst, send_sems, recv_sems, k, to):
    return pltpu.make_async_remote_copy(src_ref=src, dst_ref=dst, send_sem=send_sems.at[k],
                                        recv_sem=recv_sems.at[k], device_id=to, device_id_type=MESH_ID)


def _b1b_reduce_call(order, hnt, dproj, gwo, pack, nt):
    nk = nt + 1
    last = nk - 1
    any_spec = pl.BlockSpec(memory_space=pl.ANY)

    def body(order_ref, a_ref, b_ref, gwo_hbm, pack_hbm, gwin_hbm, gwout_hbm, tot_hbm,
             acc, sb, abuf, pb, bbuf, fin, go, ao, pbo, bo, fino, slots, totv, send_sems, recv_sems, loc_sems):
        jj, k = pl.program_id(0), pl.program_id(1)
        x, y, c = lax.axis_index("x"), lax.axis_index("y"), lax.axis_index("c")
        me, myid, sib = 2 * x + y, 4 * x + 2 * y + c, (x, y, 1 - c)
        rc = functools.partial(_rcopy, send_sems=send_sems, recv_sems=recv_sems)
        peers = [((1 - x) if r & 2 else x, (1 - y) if r & 1 else y, c) for r in REL]
        kids = [jnp.bitwise_xor(me, r) for r in REL]

        def dev_peer(r):
            return ((1 - x) if r & 4 else x, (1 - y) if r & 2 else y, (1 - c) if r & 1 else c)

        own_go = pltpu.make_async_copy(gwo_hbm.at[c], go, loc_sems.at[0])
        own_pack = pltpu.make_async_copy(pack_hbm, slots.at[0], loc_sems.at[1])
        wo_half = rc(gwo_hbm.at[1 - c], ao, k=8, to=sib)
        wo_part = [rc(pbo.at[kids[p]], bo.at[p], k=9 + p, to=peers[p]) for p in range(3)]
        sm = [rc(pack_hbm, slots.at[r], k=12 + r, to=dev_peer(r)) for r in range(1, N_DEV)]
        half = [rc(sb.at[j % 2, 1 - c], abuf.at[j], k=j, to=sib) for j in range(N_CHIPS)]
        part = [rc(pb.at[p], bbuf.at[p], k=4 + p, to=peers[p]) for p in range(3)]

        @pl.when(jnp.logical_and(jj == 0, k == 0))
        def _():
            own_go.start()
            own_pack.start()
            wo_half.start()
            for cp in sm:
                cp.start()

        @pl.when(k == 0)
        def _():
            acc[...] = jnp.zeros_like(acc)

        acc[0] += _dot(a_ref[0:512, :], b_ref[...])
        acc[1] += _dot(a_ref[512:1024, :], b_ref[...])

        @pl.when(jnp.logical_and(jj == 0, k == WOUT_STEP))
        def _():
            own_go.wait()
            wo_half.wait_recv()
            for j in range(N_CHIPS):
                go[j] = go[j] + ao[j]
            pbo[...] = go[...].astype(BF16)
            for cp in wo_part:
                cp.start()

        for j in range(N_CHIPS):
            @pl.when(jnp.logical_and(jj == j, k == last))
            def _(j=j):
                sb[j % 2] = acc[...]
                half[j].start()

        for p in range(3):
            @pl.when(jnp.logical_and(jj == p + 1, k == HALF_STEP))
            def _(p=p):
                half[p].wait_recv()
                half[p].wait_send()
                pb[p] = (sb[p % 2, c] + abuf[p]).astype(BF16)
                part[p].start()

        @pl.when(jnp.logical_and(jj == N_CHIPS - 1, k == last))
        def _():
            half[3].wait_recv()
            own = sb[1, c] + abuf[3]
            for cp in part:
                cp.wait_recv()
            fin[c] = ((own + bbuf[0].astype(F32)) + bbuf[1].astype(F32)) + bbuf[2].astype(F32)
            done = rc(fin.at[c], fin.at[c], k=7, to=sib)
            done.start()
            for cp in wo_part:
                cp.wait_recv()
            fino[c] = ((go[me] + bo[0].astype(F32)) + bo[1].astype(F32)) + bo[2].astype(F32)
            done_o = rc(fino.at[c], fino.at[c], k=12, to=sib)
            done_o.start()
            own_pack.wait()
            for cp in sm:
                cp.wait_recv()
            tot = slots[myid]
            for a in range(1, N_DEV):
                tot = tot + slots[jnp.bitwise_xor(myid, a)]
            totv[...] = tot
            out_t = pltpu.make_async_copy(totv, tot_hbm, loc_sems.at[1])
            out_t.start()
            rc(fin.at[1 - c], fin.at[1 - c], k=7, to=sib).wait_recv()
            out_w = pltpu.make_async_copy(fin, gwin_hbm, loc_sems.at[0])
            out_w.start()
            rc(fino.at[1 - c], fino.at[1 - c], k=12, to=sib).wait_recv()
            out_o = pltpu.make_async_copy(fino, gwout_hbm, loc_sems.at[2])
            out_o.start()
            for cp in [half[3]] + part + [done, wo_half] + wo_part + [done_o] + sm:
                cp.wait_send()
            out_t.wait()
            out_w.wait()
            out_o.wait()

    grid_spec = pltpu.PrefetchScalarGridSpec(
        num_scalar_prefetch=1,
        grid=(N_CHIPS, nk),
        in_specs=[pl.BlockSpec((D_MODEL, TM), lambda j, k, o: (0, k)),
                  pl.BlockSpec((TM, 1024), lambda j, k, o: (k, o[j])),
                  any_spec, any_spec],
        out_specs=[any_spec, any_spec, any_spec],
        scratch_shapes=[
            pltpu.VMEM((2, 512, 1024), F32),
            pltpu.VMEM((2, 2, 512, 1024), F32),
            pltpu.VMEM((N_CHIPS, 512, 1024), F32),
            pltpu.VMEM((3, 512, 1024), BF16),
            pltpu.VMEM((3, 512, 1024), BF16),
            pltpu.VMEM((2, 512, 1024), F32),
            pltpu.VMEM((N_CHIPS, 128, D_MODEL), F32),
            pltpu.VMEM((N_CHIPS, 128, D_MODEL), F32),
            pltpu.VMEM((N_CHIPS, 128, D_MODEL), BF16),
            pltpu.VMEM((3, 128, D_MODEL), BF16),
            pltpu.VMEM((2, 128, D_MODEL), F32),
            pltpu.VMEM((N_DEV, SMALL_ROWS, D_MODEL), F32),
            pltpu.VMEM((SMALL_ROWS, D_MODEL), F32),
            pltpu.SemaphoreType.DMA((20,)), pltpu.SemaphoreType.DMA((20,)), pltpu.SemaphoreType.DMA((3,))])
    return pl.pallas_call(
        body, name="b1b_inproj_bwd_w_reduce",
        grid_spec=grid_spec,
        out_shape=[jax.ShapeDtypeStruct((2, 512, 1024), F32),
                   jax.ShapeDtypeStruct((2, 128, D_MODEL), F32),
                   jax.ShapeDtypeStruct((SMALL_ROWS, D_MODEL), F32)],
        compiler_params=_cparams(("arbitrary", "arbitrary")),
    )(order, hnt, dproj, gwo, pack)


def _local_step(me, x, target, g1, gret, fg, win_sh, wout_sh, small_sh):
    seq = x.shape[0]
    tb = _tables(seq)
    nt = tb["nt"]
    g1r, gretr, fgr = g1.reshape(1, -1), gret.reshape(1, -1), fg.reshape(1, -1)
    order = jnp.stack([me, me ^ REL[0], me ^ REL[1], me ^ REL[2]]).astype(jnp.int32)

    proj, hnt, w_in_g, w_out_g, meta_tile, conv_w8 = _f1_gather_call(order, x, g1r, win_sh, wout_sh, small_sh, nt)
    w_out = w_out_g.reshape(D_MODEL, D_MODEL)
    mixed, conv_s, states = _f2_call(proj, conv_w8, gretr, tb)
    dh2, dmixed, g_wout, g_fg, loss = _f3_call(x, mixed, w_out, fgr, target, nt)
    dproj, g_cw8, g_gret = _b2_call(proj, dmixed, conv_s, states, conv_w8, gretr, tb)
    grad_x, g_meta, g_g1 = _b1a_call(dproj, w_in_g, x, meta_tile, g1r, dh2, nt)
    return loss, grad_x, dict(w_out=g_wout, meta=g_meta, conv_w=g_cw8[0:3], norm1_g=g_g1,
                              ret_norm_g=g_gret, final_g=g_fg), hnt, dproj


def _adamw_update(w_ref, g_ref, m_ref, v_ref, d_ref, nm_ref, nv_ref):
    gg = g_ref[...]
    nm = ADAM_B1 * m_ref[...] + (1.0 - ADAM_B1) * gg
    nv = ADAM_B2 * v_ref[...] + (1.0 - ADAM_B2) * (gg * gg)
    m_hat = nm / (1.0 - ADAM_B1 ** ADAM_STEP)
    v_hat = nv / (1.0 - ADAM_B2 ** ADAM_STEP)
    d_ref[...] = -ADAM_LR * (m_hat / (jnp.sqrt(v_hat) + ADAM_EPS) + ADAM_WD * w_ref[...])
    nm_ref[...] = nm
    nv_ref[...] = nv


def _adamw_small_call(ws, gs, ms, vs):
    n = len(ws)

    def body(*refs):
        ins, outs = refs[:4 * n], refs[4 * n:]
        for i in range(n):
            _adamw_update(ins[i], ins[n + i], ins[2 * n + i], ins[3 * n + i],
                          outs[i], outs[n + i], outs[2 * n + i])

    shapes = [jax.ShapeDtypeStruct(w.shape, F32) for w in ws]
    outs = pl.pallas_call(body, name="adamw_small", out_shape=shapes * 3,
                          compiler_params=_cparams())(*ws, *gs, *ms, *vs)
    return outs[:n], outs[n:2 * n], outs[2 * n:]


def _adamw_call(w, g, m, v, name):
    shape = w.shape
    w2, g2, m2, v2 = (a.reshape(-1, shape[-1]) for a in (w, g, m, v))
    rows, cols = w2.shape
    br = 256 if rows % 256 == 0 else rows
    body = functools.partial(_adamw_update)
    spec = pl.BlockSpec((br, cols), lambda i: (i, 0))
    outs = pl.pallas_call(
        body, name=name, grid=(rows // br,),
        in_specs=[spec] * 4, out_specs=[spec] * 3,
        out_shape=[jax.ShapeDtypeStruct((rows, cols), F32)] * 3,
        compiler_params=_cparams(("arbitrary",)),
    )(w2, g2, m2, v2)
    return tuple(o.reshape(shape) for o in outs)


def _pad_to(a, rows, cols):
    return jnp.pad(a, ((0, rows - a.shape[0]), (0, cols - a.shape[1])))


def kernel(x, meta, norm1_g, w_in, conv_w, ret_norm_g, w_out, final_g, loss_target, m_meta, m_norm1_g, m_w_in, m_conv_w, m_ret_norm_g, m_w_out, m_final_g, v_meta, v_norm1_g, v_w_in, v_conv_w, v_ret_norm_g, v_w_out, v_final_g):
    me = 2 * lax.axis_index("x") + lax.axis_index("y")

    small_sh = jnp.concatenate([meta, _pad_to(conv_w, 8, 256)], axis=0)
    loss, grad_x, g, hnt, dproj = _local_step(me, x[0], loss_target[0], norm1_g, ret_norm_g, final_g,
                                              w_in.astype(BF16), w_out.astype(BF16), small_sh)

    vec = jnp.concatenate([g["norm1_g"], g["final_g"], _pad_to(g["ret_norm_g"], 1, D_MODEL),
                           _pad_to(g["conv_w"], 3, D_MODEL), _pad_to(loss, 2, D_MODEL)], axis=0)
    pack = jnp.concatenate([g["meta"], vec], axis=0)
    order = jnp.stack([me ^ REL[0], me ^ REL[1], me ^ REL[2], me]).astype(jnp.int32)
    g_win, g_wout, tot = _b1b_reduce_call(order, hnt, dproj, g["w_out"], pack, x.shape[1] // TM)
    g_win, g_wout = g_win.reshape(D_MODEL, 1024), g_wout.reshape(256, D_MODEL)
    g_meta = lax.dynamic_slice(tot, (0, me * 256), (N_META, 256))
    g_conv = lax.dynamic_slice(tot, (N_META + 3, me * 128), (3, 128))
    g_n1, g_fg, g_rn = tot[N_META], tot[N_META + 1], tot[N_META + 2, 0:D_RET]

    loss_tot = tot[N_META + 6, 0]

    grads = [g_meta, g_n1, g_win, g_conv, g_rn, g_wout, g_fg]
    ws = [meta, norm1_g, w_in, conv_w, ret_norm_g, w_out, final_g]
    ms = [m_meta, m_norm1_g, m_w_in, m_conv_w, m_ret_norm_g, m_w_out, m_final_g]
    vs = [v_meta, v_norm1_g, v_w_in, v_conv_w, v_ret_norm_g, v_w_out, v_final_g]
    names = ["meta", "norm1_g", "w_in", "conv_w", "ret_norm_g", "w_out", "final_g"]
    as2d = lambda a: a.reshape(1, -1) if a.ndim == 1 else a
    big = [i for i, n_ in enumerate(names) if n_ in ("w_in", "w_out")]
    small = [i for i in range(len(names)) if i not in big]
    deltas, new_ms, new_vs = [None] * 7, [None] * 7, [None] * 7
    for i in big:
        deltas[i], new_ms[i], new_vs[i] = _adamw_call(ws[i], grads[i], ms[i], vs[i], "adamw_" + names[i])
    sd, sm_, sv = _adamw_small_call(*[[as2d(t[i]) for i in small] for t in (ws, grads, ms, vs)])
    for j, i in enumerate(small):
        deltas[i], new_ms[i], new_vs[i] = (o[j].reshape(ws[i].shape) for o in (sd, sm_, sv))
    return (loss_tot, grad_x[None], *grads, *deltas, *new_ms, *new_vs)
```

```python
import functools

import jax
import jax.numpy as jnp
from jax import lax
from jax.experimental import pallas as pl
from jax.experimental.pallas import tpu as pltpu

F32 = jnp.float32
BF16 = jnp.bfloat16

D_MODEL = 1024
N_META = 16
D_CONV = 512
D_RET = 512
RET_HEADS = 4
HEAD_DIM = 128
CHUNK = 128
N_PROJ_COLS = 4096
ROPE_BASE = 10000.0
EPS = 1e-6
N_CHIPS = 4
N_DEV = 8

ADAM_LR = 0.001
ADAM_B1 = 0.9
ADAM_B2 = 0.999
ADAM_EPS = 1e-08
ADAM_WD = 0.01
ADAM_STEP = 10

TM = 512
NCH = TM // CHUNK
CHUNK_GROUP = 2
VMEM_LIMIT = 56 * 1024 * 1024
VMEM_LIMIT_MAX = 63 * 1024 * 1024

CX, CB, CC, CG, CQ, CK, CV, CR = (i * 512 for i in range(8))

MESH_ID = pl.DeviceIdType.MESH


def _cparams(sem=None, vmem=VMEM_LIMIT, **kw):
    return pltpu.CompilerParams(dimension_semantics=sem, vmem_limit_bytes=vmem, **kw)


def _sigmoid(x):
    return 1.0 / (1.0 + jnp.exp(-x))


def _dot(a, b):
    return jnp.dot(a, b, preferred_element_type=F32)


def _dot_tb(a, b):
    return lax.dot_general(a, b, (((1,), (1,)), ((), ())), preferred_element_type=F32)


def _dot_ta(a, b):
    return lax.dot_general(a, b, (((0,), (0,)), ((), ())), preferred_element_type=F32)


def _resident(shape):
    nd = len(shape)
    return pl.BlockSpec(shape, lambda *_: (0,) * nd)


def _resident1(shape):
    nd = len(shape)
    return pl.BlockSpec(shape, lambda *_: (0,) * nd, pipeline_mode=pl.Buffered(1))


def _tables(seq):
    nt = seq // TM
    rows = seq + TM
    half = HEAD_DIM // 2
    freqs = 1.0 / (ROPE_BASE ** (jnp.arange(half, dtype=F32) / half))
    ang_t = (jnp.arange(nt, dtype=F32) * TM)[:, None] * freqs[None, :]
    ang_r = (jnp.arange(TM, dtype=F32) + N_META)[:, None] * freqs[None, :]
    ct, st, cr, sr = jnp.cos(ang_t), jnp.sin(ang_t), jnp.cos(ang_r), jnp.sin(ang_r)
    cos_x = (ct[:, None, :] * cr[None] - st[:, None, :] * sr[None]).reshape(seq, half)
    sin_x = (st[:, None, :] * cr[None] + ct[:, None, :] * sr[None]).reshape(seq, half)
    ang_m = jnp.maximum(jnp.arange(TM, dtype=F32) - (TM - N_META), 0.0)[:, None] * freqs[None, :]
    cos = jnp.concatenate([cos_x, jnp.cos(ang_m)], axis=0)
    sin = jnp.concatenate([sin_x, jnp.sin(ang_m)], axis=0)
    rc = jnp.concatenate([cos, cos], axis=-1)
    rs = jnp.concatenate([-sin, sin], axis=-1)
    log_g = jnp.log(1.0 - 2.0 ** (-5.0 - jnp.arange(RET_HEADS, dtype=F32)))
    idx = jnp.arange(CHUNK, dtype=F32)
    diff = idx[:, None] - idx[None, :]
    decay = jnp.where(diff[None] >= 0, jnp.exp(diff[None] * log_g[:, None, None]), 0.0)
    zeta = jnp.exp((CHUNK - 1 - idx)[None, :] * log_g[:, None])
    xi = jnp.exp((idx + 1.0)[None, :] * log_g[:, None])
    cd = jnp.exp(CHUNK * log_g)
    zeta_b = jnp.broadcast_to(zeta[:, :, None], (RET_HEADS, CHUNK, HEAD_DIM))
    xi_b = jnp.broadcast_to(xi[:, :, None], (RET_HEADS, CHUNK, HEAD_DIM))
    cd_b = jnp.broadcast_to(cd[:, None, None], (RET_HEADS, 8, HEAD_DIM))
    return dict(nt=nt, rows=rows, rc=rc, rs=rs, decay=decay, decay_t=jnp.swapaxes(decay, 1, 2), zeta=zeta_b,
                xi=xi_b, cd=cd_b)


def _rot(t, rc, rs):
    return t * rc + pltpu.roll(t, HEAD_DIM // 2, 1) * rs


def _rot_t(dt, rc, rs):
    return dt * rc + pltpu.roll(dt * rs, HEAD_DIM // 2, 1)


def _f1_gather_call(order, x, g1, win_sh, wout_sh, small_sh, nt):
    nk = nt + 1
    rows = nk * TM
    any_spec = pl.BlockSpec(memory_space=pl.ANY)

    def body(order_ref, x_ref, g_ref, win_hbm, wout_hbm, sm_hbm,
             proj_ref, hnt_ref, wg_hbm, wog_hbm, mt_hbm, cw_hbm,
             wg, wog, smg, mt, cw, hbs, send_sems, recv_sems, loc_sems):
        jj, k = pl.program_id(0), pl.program_id(1)
        x, y, c = lax.axis_index("x"), lax.axis_index("y"), lax.axis_index("c")
        me, sib = 2 * x + y, (x, y, 1 - c)
        rc = functools.partial(_rcopy, send_sems=send_sems, recv_sems=recv_sems)
        peers = [((1 - x) if r & 2 else x, (1 - y) if r & 1 else y, c) for r in REL]
        kids = [jnp.bitwise_xor(me, r) for r in REL]
        hw, ho = pl.ds(c * 512, 512), pl.ds(c * 128, 128)
        hw2, ho2 = pl.ds((1 - c) * 512, 512), pl.ds((1 - c) * 128, 128)
        at = lambda j_, k_: jnp.logical_and(jj == j_, k == k_)

        sm_cp = [rc(sm_hbm, smg.at[me], k=p, to=peers[p]) for p in range(3)]
        win_cp = [rc(win_hbm.at[hw], wg.at[me, hw], k=3 + p, to=peers[p]) for p in range(3)]
        wout_cp = [rc(wout_hbm.at[ho], wog.at[me, ho], k=6 + p, to=peers[p]) for p in range(3)]
        sm_in = [rc(sm_hbm, smg.at[kids[p]], k=p, to=sib) for p in range(3)]
        win_in = [rc(win_hbm.at[hw], wg.at[kids[p], hw], k=3 + p, to=sib) for p in range(3)]
        wout_in = [rc(wout_hbm.at[ho], wog.at[kids[p], ho], k=6 + p, to=sib) for p in range(3)]
        win_fw = [rc(wg.at[kids[p], hw], wg.at[kids[p], hw], k=9 + p, to=sib) for p in range(3)]
        wout_fw = [rc(wog.at[kids[p], ho], wog.at[kids[p], ho], k=12 + p, to=sib) for p in range(3)]
        win_fw_in = [rc(wg.at[kids[p], hw2], wg.at[kids[p], hw2], k=9 + p, to=sib) for p in range(3)]
        wout_fw_in = [rc(wog.at[kids[p], ho2], wog.at[kids[p], ho2], k=12 + p, to=sib) for p in range(3)]
        own_w = pltpu.make_async_copy(win_hbm, wg.at[me], loc_sems.at[0])
        own_o = pltpu.make_async_copy(wout_hbm, wog.at[me], loc_sems.at[1])
        own_s = pltpu.make_async_copy(sm_hbm, smg.at[me], loc_sems.at[2])
        out_wg = pltpu.make_async_copy(wg, wg_hbm, loc_sems.at[3])
        out_wog = pltpu.make_async_copy(wog, wog_hbm, loc_sems.at[4])
        out_mt = pltpu.make_async_copy(mt, mt_hbm, loc_sems.at[5])
        out_cw = pltpu.make_async_copy(cw, cw_hbm, loc_sems.at[6])

        @pl.when(at(0, 0))
        def _():
            own_w.start()
            own_s.start()
            own_o.start()
            for cp in sm_cp + win_cp + wout_cp:
                cp.start()
            own_w.wait()

        @pl.when(at(0, nk - 2))
        def _():
            own_s.wait()
            for cp in sm_in:
                cp.wait_recv()
            mt[...] = jnp.zeros_like(mt)
            cw[...] = jnp.zeros_like(cw)
            for j in range(N_CHIPS):
                mt[TM - N_META:TM, j * 256:(j + 1) * 256] = smg[j, 0:N_META, :]
                cw[0:3, j * 128:(j + 1) * 128] = smg[j, N_META:N_META + 3, 0:128]
            out_mt.start()
            out_cw.start()

        for p, (j_, k_) in enumerate([(0, nk - 1), (1, 1), (2, nk // 2)]):
            @pl.when(at(j_, k_))
            def _(p=p):
                win_in[p].wait_recv()
                win_fw[p].start()

            @pl.when(at(p + 1, 0))
            def _(p=p):
                win_fw_in[p].wait_recv()

        @pl.when(at(3, 0))
        def _():
            out_wg.start()

        @pl.when(at(3, 1))
        def _():
            for p in range(3):
                wout_in[p].wait_recv()
                wout_fw[p].start()

        tile_rows = pl.ds(pl.multiple_of(k * TM, TM), TM)

        @pl.when(jj == 0)
        def _():
            h = jnp.where(k == nt, mt[...], x_ref[...])
            ms = jnp.mean(h * h, axis=-1, keepdims=True)
            hn = (h * lax.rsqrt(ms + EPS)) * g_ref[...]
            hb = hn.astype(BF16)
            hbs[tile_rows, :] = hb
            proj_ref[...] = _dot(hb, wg[order_ref[0]]).astype(BF16)
            hnt_ref[...] = hn.T.astype(BF16)

        @pl.when(jj > 0)
        def _():
            proj_ref[...] = _dot(hbs[tile_rows, :], wg[order_ref[jj]]).astype(BF16)

        @pl.when(at(3, nk - 1))
        def _():
            own_o.wait()
            for cp in wout_fw_in:
                cp.wait_recv()
            out_wog.start()
            for cp in sm_cp + win_cp + wout_cp + win_fw + wout_fw:
                cp.wait_send()
            for cp in (out_wg, out_wog, out_mt, out_cw):
                cp.wait()

    grid_spec = pltpu.PrefetchScalarGridSpec(
        num_scalar_prefetch=1,
        grid=(N_CHIPS, nk),
        in_specs=[pl.BlockSpec((TM, D_MODEL), lambda j, k, o: (jnp.where(j == 0, jnp.minimum(k, nt - 1), nt - 1), 0)),
                  pl.BlockSpec((1, D_MODEL), lambda j, k, o: (0, 0)),
                  any_spec, any_spec, any_spec],
        out_specs=[pl.BlockSpec((TM, 1024), lambda j, k, o: (k, o[j])),
                   pl.BlockSpec((D_MODEL, TM), lambda j, k, o: (0, jnp.where(j == 0, k, nk - 1))),
                   any_spec, any_spec, any_spec, any_spec],
        scratch_shapes=[
            pltpu.VMEM((N_CHIPS, D_MODEL, 1024), BF16),
            pltpu.VMEM((N_CHIPS, 256, D_MODEL), BF16),
            pltpu.VMEM((N_CHIPS, SMALL_ROWS, 256), F32),
            pltpu.VMEM((TM, D_MODEL), F32),
            pltpu.VMEM((8, D_CONV), F32),
            pltpu.VMEM((rows, D_MODEL), BF16),
            pltpu.SemaphoreType.DMA((15,)), pltpu.SemaphoreType.DMA((15,)), pltpu.SemaphoreType.DMA((7,))])
    return pl.pallas_call(
        body, name="f1_norm_inproj_gather",
        grid_spec=grid_spec,
        out_shape=[jax.ShapeDtypeStruct((rows, N_PROJ_COLS), BF16),
                   jax.ShapeDtypeStruct((D_MODEL, rows), BF16),
                   jax.ShapeDtypeStruct((N_CHIPS, D_MODEL, 1024), BF16),
                   jax.ShapeDtypeStruct((N_CHIPS, 256, D_MODEL), BF16),
                   jax.ShapeDtypeStruct((TM, D_MODEL), F32),
                   jax.ShapeDtypeStruct((8, D_CONV), F32)],
        compiler_params=_cparams(("arbitrary", "arbitrary")),
    )(order, x, g1, win_sh, wout_sh, small_sh)


def _f2_call(proj, conv_w8, gret, tb):
    nt, rows = tb["nt"], tb["rows"]

    def phys(s):
        return jnp.where(s == 0, nt, s - 1)

    def body(proj_ref, cw_ref, g_ref, rc_ref, rs_ref, dec_ref, xi_ref, zeta_ref, cd_ref,
             mixed_ref, conv_ref, states_ref, state, uhalo):
        s = pl.program_id(0)

        @pl.when(s == 0)
        def _():
            state[...] = jnp.zeros_like(state)
            uhalo[...] = jnp.zeros_like(uhalo)

        cx = proj_ref[:, CX:CX + 512].astype(F32)
        cc = proj_ref[:, CC:CC + 512].astype(F32)
        u = cc * cx
        row = lax.broadcasted_iota(jnp.int32, (TM, D_CONV), 0)
        h7 = uhalo[7:8, :]
        h6 = uhalo[6:7, :]
        u1 = jnp.where(row == 0, h7, pltpu.roll(u, 1, 0))
        u2 = jnp.where(row == 0, h6, jnp.where(row == 1, h7, pltpu.roll(u, 2, 0)))
        conv = cw_ref[2:3, :] * u + cw_ref[1:2, :] * u1 + cw_ref[0:1, :] * u2
        uhalo[...] = u[TM - 8:TM, :]
        cb = proj_ref[:, CB:CB + 512].astype(F32)
        cg = proj_ref[:, CG:CG + 512].astype(F32)
        mixed_ref[:, 0:D_CONV] = (cb * conv * (cg * _sigmoid(cg))).astype(BF16)
        conv_ref[...] = conv.astype(BF16)

        scale = HEAD_DIM ** -0.5
        H = range(RET_HEADS)
        st = [state[h] for h in H]
        for c in range(NCH):
            r0 = c * CHUNK
            rc = rc_ref[r0:r0 + CHUNK, :]
            rs = rs_ref[r0:r0 + CHUNK, :]
            col = lambda base, h: slice(base + h * HEAD_DIM, base + (h + 1) * HEAD_DIM)
            rws = slice(r0, r0 + CHUNK)
            v = [proj_ref[rws, col(CV, h)] for h in H]
            qf = [_rot(proj_ref[rws, col(CQ, h)].astype(F32), rc, rs) * scale for h in H]
            kf = [_rot(proj_ref[rws, col(CK, h)].astype(F32), rc, rs) for h in H]
            stb = [t.astype(BF16) for t in st]
            for h in H:
                states_ref[c, h] = stb[h]
            a = [(_dot_tb(qf[h].astype(BF16), kf[h].astype(BF16)) * dec_ref[h]).astype(BF16) for h in H]
            o = [_dot(a[h], v[h]) + _dot((qf[h] * xi_ref[h]).astype(BF16), stb[h]) for h in H]
            st = [cd_ref[h, 0:1, :] * st[h] + _dot_ta((kf[h] * zeta_ref[h]).astype(BF16), v[h]) for h in H]
            for h in H:
                mu = jnp.mean(o[h], axis=-1, keepdims=True)
                d = o[h] - mu
                var = jnp.mean(d * d, axis=-1, keepdims=True)
                yh = d * lax.rsqrt(var + EPS)
                rg = proj_ref[rws, col(CR, h)].astype(F32)
                mixed_ref[rws, col(D_CONV, h)] = (yh * g_ref[:, col(0, h)] * (rg * _sigmoid(rg))).astype(BF16)
        for h in H:
            state[h] = st[h]

    tile = lambda w: pl.BlockSpec((TM, w), lambda s: (phys(s), 0))
    return pl.pallas_call(
        body, name="f2_mixer_fwd",
        grid=(nt + 1,),
        in_specs=[tile(N_PROJ_COLS), _resident((8, D_CONV)), _resident((1, D_RET)),
                  tile(HEAD_DIM), tile(HEAD_DIM),
                  _resident((RET_HEADS, CHUNK, CHUNK)), _resident((RET_HEADS, CHUNK, HEAD_DIM)),
                  _resident((RET_HEADS, CHUNK, HEAD_DIM)), _resident((RET_HEADS, 8, HEAD_DIM))],
        out_specs=[tile(D_MODEL), tile(D_CONV),
                   pl.BlockSpec((NCH, RET_HEADS, HEAD_DIM, HEAD_DIM), lambda s: (phys(s), 0, 0, 0))],
        out_shape=[jax.ShapeDtypeStruct((rows, D_MODEL), BF16),
                   jax.ShapeDtypeStruct((rows, D_CONV), BF16),
                   jax.ShapeDtypeStruct(((nt + 1) * NCH, RET_HEADS, HEAD_DIM, HEAD_DIM), BF16)],
        scratch_shapes=[pltpu.VMEM((RET_HEADS, HEAD_DIM, HEAD_DIM), F32), pltpu.VMEM((8, D_CONV), F32)],
        compiler_params=_cparams(("arbitrary",)),
    )(proj, conv_w8, gret, tb["rc"], tb["rs"], tb["decay"], tb["xi"], tb["zeta"], tb["cd"])


def _f3_call(x, mixed, w_out, fg, target, nt):
    seq = nt * TM

    def body(x_ref, mx_ref, w_ref, g_ref, t_ref, dh2_ref, dmx_ref, gwo_ref, gfg_ref, loss_ref, lacc):
        i = pl.program_id(0)

        @pl.when(i == 0)
        def _():
            gwo_ref[...] = jnp.zeros_like(gwo_ref)
            gfg_ref[...] = jnp.zeros_like(gfg_ref)
            lacc[...] = jnp.zeros_like(lacc)

        mx = mx_ref[...]
        h2 = x_ref[...] + _dot(mx, w_ref[...])
        ms = jnp.mean(h2 * h2, axis=-1, keepdims=True)
        rstd = lax.rsqrt(ms + EPS)
        yh = h2 * rstd
        g = g_ref[...]
        e = yh * g - t_ref[...]
        lacc[...] += jnp.sum(e * e, axis=0, keepdims=True)
        dy = e * (1.0 / D_MODEL)
        gfg_ref[...] += jnp.sum(dy * yh, axis=0, keepdims=True)
        dyh = dy * g
        dh2 = rstd * (dyh - yh * jnp.mean(dyh * yh, axis=-1, keepdims=True))
        dh2_ref[...] = dh2
        db = dh2.astype(BF16)
        dmx_ref[...] = _dot_tb(db, w_ref[...]).astype(BF16)
        gw = _dot_ta(mx, db)
        for j in range(N_CHIPS):
            for hf in range(2):
                r0 = j * 256 + hf * 128
                gwo_ref[hf, j] += gw[r0:r0 + 128, :]

        @pl.when(i == nt - 1)
        def _():
            tot = jnp.sum(lacc[...], axis=1, keepdims=True) * (0.5 / D_MODEL)
            loss_ref[...] = jnp.broadcast_to(tot, (1, 128))

    tile = lambda w: pl.BlockSpec((TM, w), lambda i: (i, 0))
    return pl.pallas_call(
        body, name="f3_outproj_loss",
        grid=(nt,),
        in_specs=[tile(D_MODEL), tile(D_MODEL), _resident((D_MODEL, D_MODEL)), _resident((1, D_MODEL)),
                  tile(D_MODEL)],
        out_specs=[tile(D_MODEL), tile(D_MODEL), _resident((2, N_CHIPS, 128, D_MODEL)), _resident((1, D_MODEL)),
                   _resident((1, 128))],
        out_shape=[jax.ShapeDtypeStruct((seq, D_MODEL), F32),
                   jax.ShapeDtypeStruct((seq, D_MODEL), BF16),
                   jax.ShapeDtypeStruct((2, N_CHIPS, 128, D_MODEL), F32),
                   jax.ShapeDtypeStruct((1, D_MODEL), F32),
                   jax.ShapeDtypeStruct((1, 128), F32)],
        scratch_shapes=[pltpu.VMEM((1, D_MODEL), F32)],
        compiler_params=_cparams(("arbitrary",)),
    )(x, mixed, w_out, fg, target)


def _b2_b1a_call(proj, dmixed, conv_s, states, conv_w8, gret, tb, w_in_g, x, meta_tile, g1, dh2):
    nt, rows = tb["nt"], tb["rows"]
    seq = nt * TM

    def pb(r):
        return jnp.where(r >= nt, nt, nt - 1 - r)

    def xprev(r):
        return jnp.clip(nt - r, 0, nt - 1)

    def body(proj_ref, dmx_ref, conv_ref, states_ref, cw_ref, g_ref, rc_ref, rs_ref, dec_ref, dect_ref, xi_ref,
             zeta_ref, cd_ref, w_ref, x_ref, mt_ref, g1_ref, dh2_ref,
             dproj_hbm, gcw_ref, gg_ref, gx_ref, dmeta_ref, gn_ref,
             gstate, dchalo, dps, gx_keep, out_sems):
        r = pl.program_id(0)
        live = jnp.where(r >= nt, 0.0, 1.0)
        slot = lax.rem(r, 2)
        dproj_ref = dps.at[slot]

        def to_hbm(s, tile):
            return pltpu.make_async_copy(dps.at[s], dproj_hbm.at[pl.ds(pl.multiple_of(tile * TM, TM), TM), :],
                                         out_sems.at[s])

        @pl.when(r == 0)
        def _():
            gstate[...] = jnp.zeros_like(gstate)
            dchalo[...] = jnp.zeros_like(dchalo)
            gcw_ref[...] = jnp.zeros_like(gcw_ref)
            gg_ref[...] = jnp.zeros_like(gg_ref)
            gn_ref[...] = jnp.zeros_like(gn_ref)
            dmeta_ref[...] = jnp.zeros_like(dmeta_ref)
            gx_keep[...] = jnp.zeros_like(gx_keep)
            dps[...] = jnp.zeros_like(dps)

        @pl.when(r >= 2)
        def _():
            to_hbm(slot, pb(r - 2)).wait()

        dprev = dps.at[1 - slot]
        is_meta = r == nt + 1
        pieces = []

        def emit_piece():
            j = len(pieces)
            if j < N_CHIPS:
                p = _dot_tb(dprev[:, j * 1024:(j + 1) * 1024], w_ref[j])
                pieces.append(p if j == 0 else pieces[-1] + p)

        cx = proj_ref[:, CX:CX + 512].astype(F32)
        cb = proj_ref[:, CB:CB + 512].astype(F32)
        cc = proj_ref[:, CC:CC + 512].astype(F32)
        cg = proj_ref[:, CG:CG + 512].astype(F32)
        dco = dmx_ref[:, 0:D_CONV].astype(F32) * live
        conv = conv_ref[...].astype(F32)
        sg = _sigmoid(cg)
        sil = cg * sg
        t = dco * conv
        dproj_ref[:, CB:CB + 512] = (t * sil).astype(BF16)
        dproj_ref[:, CG:CG + 512] = (t * cb * (sg * (1.0 + cg * (1.0 - sg)))).astype(BF16)
        dconv = dco * cb * sil
        row = lax.broadcasted_iota(jnp.int32, (TM, D_CONV), 0)
        n0 = dchalo[0:1, :]
        n1 = dchalo[1:2, :]
        dc1 = jnp.where(row == TM - 1, n0, pltpu.roll(dconv, TM - 1, 0))
        dc2 = jnp.where(row == TM - 2, n0, jnp.where(row == TM - 1, n1, pltpu.roll(dconv, TM - 2, 0)))
        dchalo[...] = dconv[0:8, :]
        du = cw_ref[2:3, :] * dconv + cw_ref[1:2, :] * dc1 + cw_ref[0:1, :] * dc2
        u = cc * cx
        gcw_ref[2:3, :] += jnp.sum(u * dconv, axis=0, keepdims=True)
        gcw_ref[1:2, :] += jnp.sum(u * dc1, axis=0, keepdims=True)
        gcw_ref[0:1, :] += jnp.sum(u * dc2, axis=0, keepdims=True)
        dproj_ref[:, CC:CC + 512] = (du * cx).astype(BF16)
        dproj_ref[:, CX:CX + 512] = (du * cc).astype(BF16)
        emit_piece()

        scale = HEAD_DIM ** -0.5
        gs = {h: gstate[h] for h in range(RET_HEADS)}
        gg = {h: jnp.zeros((1, HEAD_DIM), F32) for h in range(RET_HEADS)}
        col = lambda base, h: slice(base + h * HEAD_DIM, base + (h + 1) * HEAD_DIM)
        rw = lambda c: slice(c * CHUNK, (c + 1) * CHUNK)
        for c0 in range(NCH - CHUNK_GROUP, -1, -CHUNK_GROUP):
            cs = range(c0 + CHUNK_GROUP - 1, c0 - 1, -1)
            U = [(c, h) for c in cs for h in range(RET_HEADS)]
            rc = {c: rc_ref[rw(c), :] for c in cs}
            rs = {c: rs_ref[rw(c), :] for c in cs}
            v = {(c, h): proj_ref[rw(c), col(CV, h)] for c, h in U}
            stb = {(c, h): states_ref[c, h] for c, h in U}
            qf = {(c, h): _rot(proj_ref[rw(c), col(CQ, h)].astype(F32), rc[c], rs[c]) * scale for c, h in U}
            kf = {(c, h): _rot(proj_ref[rw(c), col(CK, h)].astype(F32), rc[c], rs[c]) for c, h in U}
            qb = {u: qf[u].astype(BF16) for u in U}
            kb = {u: kf[u].astype(BF16) for u in U}
            qxb = {(c, h): (qf[c, h] * xi_ref[h]).astype(BF16) for c, h in U}
            kzb = {(c, h): (kf[c, h] * zeta_ref[h]).astype(BF16) for c, h in U}
            ab = {(c, h): (_dot_tb(qb[c, h], kb[c, h]) * dec_ref[h]).astype(BF16) for c, h in U}
            atb = {(c, h): (_dot_tb(kb[c, h], qb[c, h]) * dect_ref[h]).astype(BF16) for c, h in U}
            o = {u: _dot(ab[u], v[u]) + _dot(qxb[u], stb[u]) for u in U}
            emit_piece()
            dob = {}
            for c, h in U:
                mu = jnp.mean(o[c, h], axis=-1, keepdims=True)
                d = o[c, h] - mu
                var = jnp.mean(d * d, axis=-1, keepdims=True)
                rstd = lax.rsqrt(var + EPS)
                yh = d * rstd
                g = g_ref[:, col(0, h)]
                rg = proj_ref[rw(c), col(CR, h)].astype(F32)
                dro = dmx_ref[rw(c), col(D_CONV, h)].astype(F32) * live
                sg = _sigmoid(rg)
                dproj_ref[rw(c), col(CR, h)] = (dro * (yh * g) * (sg * (1.0 + rg * (1.0 - sg)))).astype(BF16)
                dret = dro * (rg * sg)
                gg[h] = gg[h] + jnp.sum(dret * yh, axis=0, keepdims=True)
                dyh = dret * g
                do = rstd * (dyh - jnp.mean(dyh, axis=-1, keepdims=True)
                             - yh * jnp.mean(dyh * yh, axis=-1, keepdims=True))
                dob[c, h] = do.astype(BF16)
            dv1 = {u: _dot(atb[u], dob[u]) for u in U}
            ds = {(c, h): (_dot_tb(dob[c, h], v[c, h]) * dec_ref[h]).astype(BF16) for c, h in U}
            dst = {(c, h): (_dot_tb(v[c, h], dob[c, h]) * dect_ref[h]).astype(BF16) for c, h in U}
            gup = {u: _dot_ta(qxb[u], dob[u]) for u in U}
            dq = {(c, h): _dot(ds[c, h], kb[c, h]) + _dot_tb(dob[c, h], stb[c, h]) * xi_ref[h] for c, h in U}
            dk1 = {u: _dot(dst[u], qb[u]) for u in U}
            emit_piece()
            for c, h in U:
                gsb = gs[h].astype(BF16)
                dv = dv1[c, h] + _dot(kzb[c, h], gsb)
                dk = dk1[c, h] + _dot_tb(v[c, h], gsb) * zeta_ref[h]
                gs[h] = cd_ref[h, 0:1, :] * gs[h] + gup[c, h]
                dproj_ref[rw(c), col(CQ, h)] = (_rot_t(dq[c, h], rc[c], rs[c]) * scale).astype(BF16)
                dproj_ref[rw(c), col(CK, h)] = _rot_t(dk, rc[c], rs[c]).astype(BF16)
                dproj_ref[rw(c), col(CV, h)] = dv.astype(BF16)
        for h in range(RET_HEADS):
            gstate[h] = gs[h]
            gg_ref[:, col(0, h)] += gg[h]

        while len(pieces) < N_CHIPS:
            emit_piece()
        dhn = pieces[-1]
        hx = jnp.where(is_meta, mt_ref[...], x_ref[...])
        ms = jnp.mean(hx * hx, axis=-1, keepdims=True)
        rstd1 = lax.rsqrt(ms + EPS)
        xh = hx * rstd1
        gn_ref[...] += jnp.sum(dhn * xh, axis=0, keepdims=True)
        dxh = dhn * g1_ref[...]
        dh = rstd1 * (dxh - xh * jnp.mean(dxh * xh, axis=-1, keepdims=True))
        gx = jnp.where(is_meta, gx_keep[...], dh + dh2_ref[...])
        gx_ref[...] = gx
        gx_keep[...] = gx
        dmeta_ref[...] = jnp.where(is_meta, dh[TM - N_META:TM, :], dmeta_ref[...])

        @pl.when(r <= nt)
        def _():
            to_hbm(slot, pb(r)).start()

        @pl.when(r == nt + 1)
        def _():
            to_hbm(1 - slot, pb(r - 1)).wait()

    tile = lambda w: pl.BlockSpec((TM, w), lambda r: (pb(r), 0))
    xtile = pl.BlockSpec((TM, D_MODEL), lambda r: (xprev(r), 0))
    return pl.pallas_call(
        body, name="b2_mixer_bwd_b1a_inproj_bwd_x",
        grid=(nt + 2,),
        in_specs=[tile(N_PROJ_COLS),
                  pl.BlockSpec((TM, D_MODEL), lambda r: (jnp.minimum(pb(r), nt - 1), 0)),
                  tile(D_CONV),
                  pl.BlockSpec((NCH, RET_HEADS, HEAD_DIM, HEAD_DIM), lambda r: (pb(r), 0, 0, 0)),
                  _resident((8, D_CONV)), _resident((1, D_RET)),
                  tile(HEAD_DIM), tile(HEAD_DIM),
                  _resident((RET_HEADS, CHUNK, CHUNK)), _resident((RET_HEADS, CHUNK, CHUNK)),
                  _resident((RET_HEADS, CHUNK, HEAD_DIM)),
                  _resident((RET_HEADS, CHUNK, HEAD_DIM)), _resident((RET_HEADS, 8, HEAD_DIM)),
                  _resident1((N_CHIPS, D_MODEL, 1024)), xtile, _resident1((TM, D_MODEL)), _resident((1, D_MODEL)),
                  xtile],
        out_specs=[pl.BlockSpec(memory_space=pl.ANY), _resident((8, D_CONV)), _resident((1, D_RET)),
                   xtile, _resident((N_META, D_MODEL)), _resident((1, D_MODEL))],
        out_shape=[jax.ShapeDtypeStruct((rows, N_PROJ_COLS), BF16),
                   jax.ShapeDtypeStruct((8, D_CONV), F32),
                   jax.ShapeDtypeStruct((1, D_RET), F32),
                   jax.ShapeDtypeStruct((seq, D_MODEL), F32),
                   jax.ShapeDtypeStruct((N_META, D_MODEL), F32),
                   jax.ShapeDtypeStruct((1, D_MODEL), F32)],
        scratch_shapes=[pltpu.VMEM((RET_HEADS, HEAD_DIM, HEAD_DIM), F32), pltpu.VMEM((8, D_CONV), F32),
                        pltpu.VMEM((2, TM, N_PROJ_COLS), BF16), pltpu.VMEM((TM, D_MODEL), F32),
                        pltpu.SemaphoreType.DMA((2,))],
        compiler_params=_cparams(("arbitrary",), vmem=VMEM_LIMIT_MAX),
    )(proj, dmixed, conv_s, states, conv_w8, gret, tb["rc"], tb["rs"], tb["decay"], tb["decay_t"], tb["xi"],
      tb["zeta"], tb["cd"], w_in_g, x, meta_tile, g1, dh2)


REL = (2, 1, 3)
SMALL_ROWS = 24
HALF_STEP = 2
WOUT_STEP = 4


def _rcopy(src, dst, send_sems, recv_sems, k, to):
    return pltpu.make_async_remote_copy(src_ref=src, dst_ref=dst, send_sem=send_sems.at[k],
                                        recv_sem=recv_sems.at[k], device_id=to, device_id_type=MESH_ID)


def _b1b_reduce_call(order, hnt, dproj, gwo, pack, nt):
    nk = nt + 1
    last = nk - 1
    any_spec = pl.BlockSpec(memory_space=pl.ANY)

    def body(order_ref, a_ref, b_ref, gwo_hbm, pack_hbm, gwin_hbm, gwout_hbm, tot_hbm,
             acc, sb, abuf, pb, bbuf, fin, go, ao, pbo, bo, fino, slots, totv, send_sems, recv_sems, loc_sems):
        jj, k = pl.program_id(0), pl.program_id(1)
        x, y, c = lax.axis_index("x"), lax.axis_index("y"), lax.axis_index("c")
        me, myid, sib = 2 * x + y, 4 * x + 2 * y + c, (x, y, 1 - c)
        rc = functools.partial(_rcopy, send_sems=send_sems, recv_sems=recv_sems)
        peers = [((1 - x) if r & 2 else x, (1 - y) if r & 1 else y, c) for r in REL]
        kids = [jnp.bitwise_xor(me, r) for r in REL]

        def dev_peer(r):
            return ((1 - x) if r & 4 else x, (1 - y) if r & 2 else y, (1 - c) if r & 1 else c)

        own_go = pltpu.make_async_copy(gwo_hbm.at[c], go, loc_sems.at[0])
        own_pack = pltpu.make_async_copy(pack_hbm, slots.at[0], loc_sems.at[1])
        wo_half = rc(gwo_hbm.at[1 - c], ao, k=8, to=sib)
        wo_part = [rc(pbo.at[kids[p]], bo.at[p], k=9 + p, to=peers[p]) for p in range(3)]
        sm = [rc(pack_hbm, slots.at[r], k=12 + r, to=dev_peer(r)) for r in range(1, N_DEV)]
        half = [rc(sb.at[j % 2, 1 - c], abuf.at[j], k=j, to=sib) for j in range(N_CHIPS)]
        part = [rc(pb.at[p], bbuf.at[p], k=4 + p, to=peers[p]) for p in range(3)]

        @pl.when(jnp.logical_and(jj == 0, k == 0))
        def _():
            own_go.start()
            own_pack.start()
            wo_half.start()
            for cp in sm:
                cp.start()

        @pl.when(k == 0)
        def _():
            acc[...] = jnp.zeros_like(acc)

        acc[0] += _dot(a_ref[0:512, :], b_ref[...])
        acc[1] += _dot(a_ref[512:1024, :], b_ref[...])

        @pl.when(jnp.logical_and(jj == 0, k == WOUT_STEP))
        def _():
            own_go.wait()
            wo_half.wait_recv()
            for j in range(N_CHIPS):
                go[j] = go[j] + ao[j]
            pbo[...] = go[...].astype(BF16)
            for cp in wo_part:
                cp.start()

        for j in range(N_CHIPS):
            @pl.when(jnp.logical_and(jj == j, k == last))
            def _(j=j):
                sb[j % 2] = acc[...]
                half[j].start()

        for p in range(3):
            @pl.when(jnp.logical_and(jj == p + 1, k == HALF_STEP))
            def _(p=p):
                half[p].wait_recv()
                half[p].wait_send()
                pb[p] = (sb[p % 2, c] + abuf[p]).astype(BF16)
                part[p].start()

        @pl.when(jnp.logical_and(jj == N_CHIPS - 1, k == last))
        def _():
            half[3].wait_recv()
            own = sb[1, c] + abuf[3]
            for cp in part:
                cp.wait_recv()
            fin[c] = ((own + bbuf[0].astype(F32)) + bbuf[1].astype(F32)) + bbuf[2].astype(F32)
            done = rc(fin.at[c], fin.at[c], k=7, to=sib)
            done.start()
            for cp in wo_part:
                cp.wait_recv()
            fino[c] = ((go[me] + bo[0].astype(F32)) + bo[1].astype(F32)) + bo[2].astype(F32)
            done_o = rc(fino.at[c], fino.at[c], k=12, to=sib)
            done_o.start()
            own_pack.wait()
            for cp in sm:
                cp.wait_recv()
            tot = slots[myid]
            for a in range(1, N_DEV):
                tot = tot + slots[jnp.bitwise_xor(myid, a)]
            totv[...] = tot
            out_t = pltpu.make_async_copy(totv, tot_hbm, loc_sems.at[1])
            out_t.start()
            rc(fin.at[1 - c], fin.at[1 - c], k=7, to=sib).wait_recv()
            out_w = pltpu.make_async_copy(fin, gwin_hbm, loc_sems.at[0])
            out_w.start()
            rc(fino.at[1 - c], fino.at[1 - c], k=12, to=sib).wait_recv()
            out_o = pltpu.make_async_copy(fino, gwout_hbm, loc_sems.at[2])
            out_o.start()
            for cp in [half[3]] + part + [done, wo_half] + wo_part + [done_o] + sm:
                cp.wait_send()
            out_t.wait()
            out_w.wait()
            out_o.wait()

    grid_spec = pltpu.PrefetchScalarGridSpec(
        num_scalar_prefetch=1,
        grid=(N_CHIPS, nk),
        in_specs=[pl.BlockSpec((D_MODEL, TM), lambda j, k, o: (0, k)),
                  pl.BlockSpec((TM, 1024), lambda j, k, o: (k, o[j])),
                  any_spec, any_spec],
        out_specs=[any_spec, any_spec, any_spec],
        scratch_shapes=[
            pltpu.VMEM((2, 512, 1024), F32),
            pltpu.VMEM((2, 2, 512, 1024), F32),
            pltpu.VMEM((N_CHIPS, 512, 1024), F32),
            pltpu.VMEM((3, 512, 1024), BF16),
            pltpu.VMEM((3, 512, 1024), BF16),
            pltpu.VMEM((2, 512, 1024), F32),
            pltpu.VMEM((N_CHIPS, 128, D_MODEL), F32),
            pltpu.VMEM((N_CHIPS, 128, D_MODEL), F32),
            pltpu.VMEM((N_CHIPS, 128, D_MODEL), BF16),
            pltpu.VMEM((3, 128, D_MODEL), BF16),
            pltpu.VMEM((2, 128, D_MODEL), F32),
            pltpu.VMEM((N_DEV, SMALL_ROWS, D_MODEL), F32),
            pltpu.VMEM((SMALL_ROWS, D_MODEL), F32),
            pltpu.SemaphoreType.DMA((20,)), pltpu.SemaphoreType.DMA((20,)), pltpu.SemaphoreType.DMA((3,))])
    return pl.pallas_call(
        body, name="b1b_inproj_bwd_w_reduce",
        grid_spec=grid_spec,
        out_shape=[jax.ShapeDtypeStruct((2, 512, 1024), F32),
                   jax.ShapeDtypeStruct((2, 128, D_MODEL), F32),
                   jax.ShapeDtypeStruct((SMALL_ROWS, D_MODEL), F32)],
        compiler_params=_cparams(("arbitrary", "arbitrary")),
    )(order, hnt, dproj, gwo, pack)


def _local_step(me, x, target, g1, gret, fg, win_sh, wout_sh, small_sh):
    seq = x.shape[0]
    tb = _tables(seq)
    nt = tb["nt"]
    g1r, gretr, fgr = g1.reshape(1, -1), gret.reshape(1, -1), fg.reshape(1, -1)
    order = jnp.stack([me, me ^ REL[0], me ^ REL[1], me ^ REL[2]]).astype(jnp.int32)

    proj, hnt, w_in_g, w_out_g, meta_tile, conv_w8 = _f1_gather_call(order, x, g1r, win_sh, wout_sh, small_sh, nt)
    w_out = w_out_g.reshape(D_MODEL, D_MODEL)
    mixed, conv_s, states = _f2_call(proj, conv_w8, gretr, tb)
    dh2, dmixed, g_wout, g_fg, loss = _f3_call(x, mixed, w_out, fgr, target, nt)
    dproj, g_cw8, g_gret, grad_x, g_meta, g_g1 = _b2_b1a_call(proj, dmixed, conv_s, states, conv_w8, gretr, tb,
                                                              w_in_g, x, meta_tile, g1r, dh2)
    return loss, grad_x, dict(w_out=g_wout, meta=g_meta, conv_w=g_cw8[0:3], norm1_g=g_g1,
                              ret_norm_g=g_gret, final_g=g_fg), hnt, dproj


def _adamw_update(w_ref, g_ref, m_ref, v_ref, d_ref, nm_ref, nv_ref):
    gg = g_ref[...]
    nm = ADAM_B1 * m_ref[...] + (1.0 - ADAM_B1) * gg
    nv = ADAM_B2 * v_ref[...] + (1.0 - ADAM_B2) * (gg * gg)
    m_hat = nm / (1.0 - ADAM_B1 ** ADAM_STEP)
    v_hat = nv / (1.0 - ADAM_B2 ** ADAM_STEP)
    d_ref[...] = -ADAM_LR * (m_hat / (jnp.sqrt(v_hat) + ADAM_EPS) + ADAM_WD * w_ref[...])
    nm_ref[...] = nm
    nv_ref[...] = nv


def _adamw_small_call(ws, gs, ms, vs):
    n = len(ws)

    def body(*refs):
        ins, outs = refs[:4 * n], refs[4 * n:]
        for i in range(n):
            _adamw_update(ins[i], ins[n + i], ins[2 * n + i], ins[3 * n + i],
                          outs[i], outs[n + i], outs[2 * n + i])

    shapes = [jax.ShapeDtypeStruct(w.shape, F32) for w in ws]
    outs = pl.pallas_call(body, name="adamw_small", out_shape=shapes * 3,
                          compiler_params=_cparams())(*ws, *gs, *ms, *vs)
    return outs[:n], outs[n:2 * n], outs[2 * n:]


def _adamw_call(w, g, m, v, name):
    shape = w.shape
    w2, g2, m2, v2 = (a.reshape(-1, shape[-1]) for a in (w, g, m, v))
    rows, cols = w2.shape
    br = 256 if rows % 256 == 0 else rows
    body = functools.partial(_adamw_update)
    spec = pl.BlockSpec((br, cols), lambda i: (i, 0))
    outs = pl.pallas_call(
        body, name=name, grid=(rows // br,),
        in_specs=[spec] * 4, out_specs=[spec] * 3,
        out_shape=[jax.ShapeDtypeStruct((rows, cols), F32)] * 3,
        compiler_params=_cparams(("arbitrary",)),
    )(w2, g2, m2, v2)
    return tuple(o.reshape(shape) for o in outs)


def _pad_to(a, rows, cols):
    return jnp.pad(a, ((0, rows - a.shape[0]), (0, cols - a.shape[1])))


def kernel(x, meta, norm1_g, w_in, conv_w, ret_norm_g, w_out, final_g, loss_target, m_meta, m_norm1_g, m_w_in, m_conv_w, m_ret_norm_g, m_w_out, m_final_g, v_meta, v_norm1_g, v_w_in, v_conv_w, v_ret_norm_g, v_w_out, v_final_g):
    me = 2 * lax.axis_index("x") + lax.axis_index("y")

    small_sh = jnp.concatenate([meta, _pad_to(conv_w, 8, 256)], axis=0)
    loss, grad_x, g, hnt, dproj = _local_step(me, x[0], loss_target[0], norm1_g, ret_norm_g, final_g,
                                              w_in.astype(BF16), w_out.astype(BF16), small_sh)

    vec = jnp.concatenate([g["norm1_g"], g["final_g"], _pad_to(g["ret_norm_g"], 1, D_MODEL),
                           _pad_to(g["conv_w"], 3, D_MODEL), _pad_to(loss, 2, D_MODEL)], axis=0)
    pack = jnp.concatenate([g["meta"], vec], axis=0)
    order = jnp.stack([me ^ REL[0], me ^ REL[1], me ^ REL[2], me]).astype(jnp.int32)
    g_win, g_wout, tot = _b1b_reduce_call(order, hnt, dproj, g["w_out"], pack, x.shape[1] // TM)
    g_win, g_wout = g_win.reshape(D_MODEL, 1024), g_wout.reshape(256, D_MODEL)
    g_meta = lax.dynamic_slice(tot, (0, me * 256), (N_META, 256))
    g_conv = lax.dynamic_slice(tot, (N_META + 3, me * 128), (3, 128))
    g_n1, g_fg, g_rn = tot[N_META], tot[N_META + 1], tot[N_META + 2, 0:D_RET]

    loss_tot = tot[N_META + 6, 0]

    grads = [g_meta, g_n1, g_win, g_conv, g_rn, g_wout, g_fg]
    ws = [meta, norm1_g, w_in, conv_w, ret_norm_g, w_out, final_g]
    ms = [m_meta, m_norm1_g, m_w_in, m_conv_w, m_ret_norm_g, m_w_out, m_final_g]
    vs = [v_meta, v_norm1_g, v_w_in, v_conv_w, v_ret_norm_g, v_w_out, v_final_g]
    names = ["meta", "norm1_g", "w_in", "conv_w", "ret_norm_g", "w_out", "final_g"]
    as2d = lambda a: a.reshape(1, -1) if a.ndim == 1 else a
    big = [i for i, n_ in enumerate(names) if n_ in ("w_in", "w_out")]
    small = [i for i in range(len(names)) if i not in big]
    deltas, new_ms, new_vs = [None] * 7, [None] * 7, [None] * 7
    for i in big:
        deltas[i], new_ms[i], new_vs[i] = _adamw_call(ws[i], grads[i], ms[i], vs[i], "adamw_" + names[i])
    sd, sm_, sv = _adamw_small_call(*[[as2d(t[i]) for i in small] for t in (ws, grads, ms, vs)])
    for j, i in enumerate(small):
        deltas[i], new_ms[i], new_vs[i] = (o[j].reshape(ws[i].shape) for o in (sd, sm_, sv))
    return (loss_tot, grad_x[None], *grads, *deltas, *new_ms, *new_vs)
```

```python
import functools

import jax
import jax.numpy as jnp
from jax import lax
from jax.experimental import pallas as pl
from jax.experimental.pallas import tpu as pltpu

F32 = jnp.float32
BF16 = jnp.bfloat16

D_MODEL = 1024
N_META = 16
D_CONV = 512
D_RET = 512
RET_HEADS = 4
HEAD_DIM = 128
CHUNK = 128
N_PROJ_COLS = 4096
ROPE_BASE = 10000.0
EPS = 1e-6
N_CHIPS = 4
N_DEV = 8

ADAM_LR = 0.001
ADAM_B1 = 0.9
ADAM_B2 = 0.999
ADAM_EPS = 1e-08
ADAM_WD = 0.01
ADAM_STEP = 10

TM = 512
NCH = TM // CHUNK
CHUNK_GROUP = 2
VMEM_LIMIT = 56 * 1024 * 1024
VMEM_LIMIT_MAX = 63 * 1024 * 1024

CX, CB, CC, CG, CQ, CK, CV, CR = (i * 512 for i in range(8))

MESH_ID = pl.DeviceIdType.MESH


def _cparams(sem=None, vmem=VMEM_LIMIT, **kw):
    return pltpu.CompilerParams(dimension_semantics=sem, vmem_limit_bytes=vmem, **kw)


def _sigmoid(x):
    return 1.0 / (1.0 + jnp.exp(-x))


def _dot(a, b):
    return jnp.dot(a, b, preferred_element_type=F32)


def _dot_tb(a, b):
    return lax.dot_general(a, b, (((1,), (1,)), ((), ())), preferred_element_type=F32)


def _dot_ta(a, b):
    return lax.dot_general(a, b, (((0,), (0,)), ((), ())), preferred_element_type=F32)


def _resident(shape):
    nd = len(shape)
    return pl.BlockSpec(shape, lambda *_: (0,) * nd)


def _resident1(shape):
    nd = len(shape)
    return pl.BlockSpec(shape, lambda *_: (0,) * nd, pipeline_mode=pl.Buffered(1))


def _tables(seq):
    nt = seq // TM
    rows = seq + TM
    half = HEAD_DIM // 2
    freqs = 1.0 / (ROPE_BASE ** (jnp.arange(half, dtype=F32) / half))
    ang_t = (jnp.arange(nt, dtype=F32) * TM)[:, None] * freqs[None, :]
    ang_r = (jnp.arange(TM, dtype=F32) + N_META)[:, None] * freqs[None, :]
    ct, st, cr, sr = jnp.cos(ang_t), jnp.sin(ang_t), jnp.cos(ang_r), jnp.sin(ang_r)
    cos_x = (ct[:, None, :] * cr[None] - st[:, None, :] * sr[None]).reshape(seq, half)
    sin_x = (st[:, None, :] * cr[None] + ct[:, None, :] * sr[None]).reshape(seq, half)
    ang_m = jnp.maximum(jnp.arange(TM, dtype=F32) - (TM - N_META), 0.0)[:, None] * freqs[None, :]
    cos = jnp.concatenate([cos_x, jnp.cos(ang_m)], axis=0)
    sin = jnp.concatenate([sin_x, jnp.sin(ang_m)], axis=0)
    rc = jnp.concatenate([cos, cos], axis=-1)
    rs = jnp.concatenate([-sin, sin], axis=-1)
    log_g = jnp.log(1.0 - 2.0 ** (-5.0 - jnp.arange(RET_HEADS, dtype=F32)))
    idx = jnp.arange(CHUNK, dtype=F32)
    diff = idx[:, None] - idx[None, :]
    decay = jnp.where(diff[None] >= 0, jnp.exp(diff[None] * log_g[:, None, None]), 0.0)
    zeta = jnp.exp((CHUNK - 1 - idx)[None, :] * log_g[:, None])
    xi = jnp.exp((idx + 1.0)[None, :] * log_g[:, None])
    cd = jnp.exp(CHUNK * log_g)
    zeta_b = jnp.broadcast_to(zeta[:, :, None], (RET_HEADS, CHUNK, HEAD_DIM))
    xi_b = jnp.broadcast_to(xi[:, :, None], (RET_HEADS, CHUNK, HEAD_DIM))
    cd_b = jnp.broadcast_to(cd[:, None, None], (RET_HEADS, 8, HEAD_DIM))
    return dict(nt=nt, rows=rows, rc=rc, rs=rs, decay=decay, decay_t=jnp.swapaxes(decay, 1, 2), zeta=zeta_b,
                xi=xi_b, cd=cd_b)


def _rot(t, rc, rs):
    return t * rc + pltpu.roll(t, HEAD_DIM // 2, 1) * rs


def _rot_t(dt, rc, rs):
    return dt * rc + pltpu.roll(dt * rs, HEAD_DIM // 2, 1)


def _f1_gather_call(order, x, g1, win_sh, wout_sh, small_sh, nt):
    nk = nt + 1
    rows = nk * TM
    any_spec = pl.BlockSpec(memory_space=pl.ANY)

    def body(order_ref, x_ref, g_ref, win_hbm, wout_hbm, sm_hbm,
             proj_ref, hnt_ref, wg_hbm, wog_hbm, mt_hbm, cw_hbm,
             wg, wog, smg, mt, cw, hbs, send_sems, recv_sems, loc_sems):
        jj, k = pl.program_id(0), pl.program_id(1)
        x, y, c = lax.axis_index("x"), lax.axis_index("y"), lax.axis_index("c")
        me, sib = 2 * x + y, (x, y, 1 - c)
        rc = functools.partial(_rcopy, send_sems=send_sems, recv_sems=recv_sems)
        peers = [((1 - x) if r & 2 else x, (1 - y) if r & 1 else y, c) for r in REL]
        kids = [jnp.bitwise_xor(me, r) for r in REL]
        hw, ho = pl.ds(c * 512, 512), pl.ds(c * 128, 128)
        hw2, ho2 = pl.ds((1 - c) * 512, 512), pl.ds((1 - c) * 128, 128)
        at = lambda j_, k_: jnp.logical_and(jj == j_, k == k_)

        sm_cp = [rc(sm_hbm, smg.at[me], k=p, to=peers[p]) for p in range(3)]
        win_cp = [rc(win_hbm.at[hw], wg.at[me, hw], k=3 + p, to=peers[p]) for p in range(3)]
        wout_cp = [rc(wout_hbm.at[ho], wog.at[me, ho], k=6 + p, to=peers[p]) for p in range(3)]
        sm_in = [rc(sm_hbm, smg.at[kids[p]], k=p, to=sib) for p in range(3)]
        win_in = [rc(win_hbm.at[hw], wg.at[kids[p], hw], k=3 + p, to=sib) for p in range(3)]
        wout_in = [rc(wout_hbm.at[ho], wog.at[kids[p], ho], k=6 + p, to=sib) for p in range(3)]
        win_fw = [rc(wg.at[kids[p], hw], wg.at[kids[p], hw], k=9 + p, to=sib) for p in range(3)]
        wout_fw = [rc(wog.at[kids[p], ho], wog.at[kids[p], ho], k=12 + p, to=sib) for p in range(3)]
        win_fw_in = [rc(wg.at[kids[p], hw2], wg.at[kids[p], hw2], k=9 + p, to=sib) for p in range(3)]
        wout_fw_in = [rc(wog.at[kids[p], ho2], wog.at[kids[p], ho2], k=12 + p, to=sib) for p in range(3)]
        own_w = pltpu.make_async_copy(win_hbm, wg.at[me], loc_sems.at[0])
        own_o = pltpu.make_async_copy(wout_hbm, wog.at[me], loc_sems.at[1])
        own_s = pltpu.make_async_copy(sm_hbm, smg.at[me], loc_sems.at[2])
        out_wg = pltpu.make_async_copy(wg, wg_hbm, loc_sems.at[3])
        out_wog = pltpu.make_async_copy(wog, wog_hbm, loc_sems.at[4])
        out_mt = pltpu.make_async_copy(mt, mt_hbm, loc_sems.at[5])
        out_cw = pltpu.make_async_copy(cw, cw_hbm, loc_sems.at[6])

        @pl.when(at(0, 0))
        def _():
            own_w.start()
            own_s.start()
            own_o.start()
            for cp in sm_cp + win_cp + wout_cp:
                cp.start()
            own_w.wait()

        @pl.when(at(0, nk - 2))
        def _():
            own_s.wait()
            for cp in sm_in:
                cp.wait_recv()
            mt[...] = jnp.zeros_like(mt)
            cw[...] = jnp.zeros_like(cw)
            for j in range(N_CHIPS):
                mt[TM - N_META:TM, j * 256:(j + 1) * 256] = smg[j, 0:N_META, :]
                cw[0:3, j * 128:(j + 1) * 128] = smg[j, N_META:N_META + 3, 0:128]
            out_mt.start()
            out_cw.start()

        for p, (j_, k_) in enumerate([(0, nk - 1), (1, 1), (2, nk // 2)]):
            @pl.when(at(j_, k_))
            def _(p=p):
                win_in[p].wait_recv()
                win_fw[p].start()

            @pl.when(at(p + 1, 0))
            def _(p=p):
                win_fw_in[p].wait_recv()

        @pl.when(at(3, 0))
        def _():
            out_wg.start()

        @pl.when(at(3, 1))
        def _():
            for p in range(3):
                wout_in[p].wait_recv()
                wout_fw[p].start()

        tile_rows = pl.ds(pl.multiple_of(k * TM, TM), TM)

        @pl.when(jj == 0)
        def _():
            h = jnp.where(k == nt, mt[...], x_ref[...])
            ms = jnp.mean(h * h, axis=-1, keepdims=True)
            hn = (h * lax.rsqrt(ms + EPS)) * g_ref[...]
            hb = hn.astype(BF16)
            hbs[tile_rows, :] = hb
            proj_ref[...] = _dot(hb, wg[order_ref[0]]).astype(BF16)
            hnt_ref[...] = hn.T.astype(BF16)

        @pl.when(jj > 0)
        def _():
            proj_ref[...] = _dot(hbs[tile_rows, :], wg[order_ref[jj]]).astype(BF16)

        @pl.when(at(3, nk - 1))
        def _():
            own_o.wait()
            for cp in wout_fw_in:
                cp.wait_recv()
            out_wog.start()
            for cp in sm_cp + win_cp + wout_cp + win_fw + wout_fw:
                cp.wait_send()
            for cp in (out_wg, out_wog, out_mt, out_cw):
                cp.wait()

    grid_spec = pltpu.PrefetchScalarGridSpec(
        num_scalar_prefetch=1,
        grid=(N_CHIPS, nk),
        in_specs=[pl.BlockSpec((TM, D_MODEL), lambda j, k, o: (jnp.where(j == 0, jnp.minimum(k, nt - 1), nt - 1), 0)),
                  pl.BlockSpec((1, D_MODEL), lambda j, k, o: (0, 0)),
                  any_spec, any_spec, any_spec],
        out_specs=[pl.BlockSpec((TM, 1024), lambda j, k, o: (k, o[j])),
                   pl.BlockSpec((D_MODEL, TM), lambda j, k, o: (0, jnp.where(j == 0, k, nk - 1))),
                   any_spec, any_spec, any_spec, any_spec],
        scratch_shapes=[
            pltpu.VMEM((N_CHIPS, D_MODEL, 1024), BF16),
            pltpu.VMEM((N_CHIPS, 256, D_MODEL), BF16),
            pltpu.VMEM((N_CHIPS, SMALL_ROWS, 256), F32),
            pltpu.VMEM((TM, D_MODEL), F32),
            pltpu.VMEM((8, D_CONV), F32),
            pltpu.VMEM((rows, D_MODEL), BF16),
            pltpu.SemaphoreType.DMA((15,)), pltpu.SemaphoreType.DMA((15,)), pltpu.SemaphoreType.DMA((7,))])
    return pl.pallas_call(
        body, name="f1_norm_inproj_gather",
        grid_spec=grid_spec,
        out_shape=[jax.ShapeDtypeStruct((rows, N_PROJ_COLS), BF16),
                   jax.ShapeDtypeStruct((D_MODEL, rows), BF16),
                   jax.ShapeDtypeStruct((N_CHIPS, D_MODEL, 1024), BF16),
                   jax.ShapeDtypeStruct((N_CHIPS, 256, D_MODEL), BF16),
                   jax.ShapeDtypeStruct((TM, D_MODEL), F32),
                   jax.ShapeDtypeStruct((8, D_CONV), F32)],
        compiler_params=_cparams(("arbitrary", "arbitrary")),
    )(order, x, g1, win_sh, wout_sh, small_sh)


def _f2_f3_call(proj, conv_w8, gret, tb, x, w_out, fg, target):
    nt, rows = tb["nt"], tb["rows"]
    seq = nt * TM

    def pf(s):
        return jnp.where(s == 0, nt, jnp.minimum(s - 1, nt - 1))

    def xt(s):
        return jnp.clip(s - 2, 0, nt - 1)

    def body(proj_ref, cw_ref, g_ref, rc_ref, rs_ref, dec_ref, xi_ref, zeta_ref, cd_ref,
             x_ref, w_ref, fg_ref, t_ref,
             conv_hbm, states_hbm, dh2_ref, dmx_ref, gwo_ref, gfg_ref, loss_ref,
             state, uhalo, mxs, convs, sts, lacc, out_sems):
        s = pl.program_id(0)
        slot = lax.rem(s, 2)
        mixed_ref = mxs.at[slot]
        conv_ref = convs.at[slot]
        states_ref = sts.at[slot]

        def conv_out(sl, tile):
            return pltpu.make_async_copy(convs.at[sl], conv_hbm.at[pl.ds(pl.multiple_of(tile * TM, TM), TM), :],
                                         out_sems.at[sl])

        def states_out(sl, tile):
            return pltpu.make_async_copy(sts.at[sl], states_hbm.at[pl.ds(pl.multiple_of(tile * NCH, NCH), NCH)],
                                         out_sems.at[2 + sl])

        @pl.when(s == 0)
        def _():
            state[...] = jnp.zeros_like(state)
            uhalo[...] = jnp.zeros_like(uhalo)
            mxs[...] = jnp.zeros_like(mxs)
            gwo_ref[...] = jnp.zeros_like(gwo_ref)
            gfg_ref[...] = jnp.zeros_like(gfg_ref)
            lacc[...] = jnp.zeros_like(lacc)

        @pl.when(s >= 2)
        def _():
            conv_out(slot, pf(s - 2)).wait()
            states_out(slot, pf(s - 2)).wait()

        valid = jnp.where(s >= 2, 1.0, 0.0)
        mx_prev = mxs.at[1 - slot]
        f3 = {}

        def f3_fwd():
            f3["h2"] = x_ref[...] + _dot(mx_prev[...], w_ref[...])

        def f3_loss():
            h2 = f3.pop("h2")
            ms = jnp.mean(h2 * h2, axis=-1, keepdims=True)
            rstd = lax.rsqrt(ms + EPS)
            yh = h2 * rstd
            g = fg_ref[...]
            e = (yh * g - t_ref[...]) * valid
            lacc[...] += jnp.sum(e * e, axis=0, keepdims=True)
            dy = e * (1.0 / D_MODEL)
            gfg_ref[...] += jnp.sum(dy * yh, axis=0, keepdims=True)
            dyh = dy * g
            dh2 = rstd * (dyh - yh * jnp.mean(dyh * yh, axis=-1, keepdims=True))
            dh2_ref[...] = dh2
            f3["db"] = dh2.astype(BF16)

        def f3_dmx():
            dmx_ref[...] = _dot_tb(f3["db"], w_ref[...]).astype(BF16)

        def f3_gw():
            gw = _dot_ta(mx_prev[...], f3["db"])
            for j in range(N_CHIPS):
                for hf in range(2):
                    r0 = j * 256 + hf * 128
                    gwo_ref[hf, j] += gw[r0:r0 + 128, :]

        cx = proj_ref[:, CX:CX + 512].astype(F32)
        cc = proj_ref[:, CC:CC + 512].astype(F32)
        u = cc * cx
        row = lax.broadcasted_iota(jnp.int32, (TM, D_CONV), 0)
        h7 = uhalo[7:8, :]
        h6 = uhalo[6:7, :]
        u1 = jnp.where(row == 0, h7, pltpu.roll(u, 1, 0))
        u2 = jnp.where(row == 0, h6, jnp.where(row == 1, h7, pltpu.roll(u, 2, 0)))
        conv = cw_ref[2:3, :] * u + cw_ref[1:2, :] * u1 + cw_ref[0:1, :] * u2
        uhalo[...] = u[TM - 8:TM, :]
        cb = proj_ref[:, CB:CB + 512].astype(F32)
        cg = proj_ref[:, CG:CG + 512].astype(F32)
        mixed_ref[:, 0:D_CONV] = (cb * conv * (cg * _sigmoid(cg))).astype(BF16)
        conv_ref[...] = conv.astype(BF16)
        f3_fwd()

        scale = HEAD_DIM ** -0.5
        H = range(RET_HEADS)
        st = [state[h] for h in H]
        between = [f3_loss, f3_dmx, f3_gw, None]
        for c in range(NCH):
            r0 = c * CHUNK
            rc = rc_ref[r0:r0 + CHUNK, :]
            rs = rs_ref[r0:r0 + CHUNK, :]
            col = lambda base, h: slice(base + h * HEAD_DIM, base + (h + 1) * HEAD_DIM)
            rws = slice(r0, r0 + CHUNK)
            v = [proj_ref[rws, col(CV, h)] for h in H]
            qf = [_rot(proj_ref[rws, col(CQ, h)].astype(F32), rc, rs) * scale for h in H]
            kf = [_rot(proj_ref[rws, col(CK, h)].astype(F32), rc, rs) for h in H]
            stb = [t.astype(BF16) for t in st]
            for h in H:
                states_ref[c, h] = stb[h]
            a = [(_dot_tb(qf[h].astype(BF16), kf[h].astype(BF16)) * dec_ref[h]).astype(BF16) for h in H]
            o = [_dot(a[h], v[h]) + _dot((qf[h] * xi_ref[h]).astype(BF16), stb[h]) for h in H]
            st = [cd_ref[h, 0:1, :] * st[h] + _dot_ta((kf[h] * zeta_ref[h]).astype(BF16), v[h]) for h in H]
            for h in H:
                mu = jnp.mean(o[h], axis=-1, keepdims=True)
                d = o[h] - mu
                var = jnp.mean(d * d, axis=-1, keepdims=True)
                yh = d * lax.rsqrt(var + EPS)
                rg = proj_ref[rws, col(CR, h)].astype(F32)
                mixed_ref[rws, col(D_CONV, h)] = (yh * g_ref[:, col(0, h)] * (rg * _sigmoid(rg))).astype(BF16)
            if between[c] is not None:
                between[c]()
        for h in H:
            state[h] = st[h]

        @pl.when(s <= nt)
        def _():
            conv_out(slot, pf(s)).start()
            states_out(slot, pf(s)).start()

        @pl.when(s == nt + 1)
        def _():
            conv_out(1 - slot, pf(s - 1)).wait()
            states_out(1 - slot, pf(s - 1)).wait()
            tot = jnp.sum(lacc[...], axis=1, keepdims=True) * (0.5 / D_MODEL)
            loss_ref[...] = jnp.broadcast_to(tot, (1, 128))

    tile = lambda w: pl.BlockSpec((TM, w), lambda s: (pf(s), 0))
    xtile = lambda w: pl.BlockSpec((TM, w), lambda s: (xt(s), 0))
    any_spec = pl.BlockSpec(memory_space=pl.ANY)
    return pl.pallas_call(
        body, name="f2_mixer_fwd_f3_outproj_loss",
        grid=(nt + 2,),
        in_specs=[tile(N_PROJ_COLS), _resident((8, D_CONV)), _resident((1, D_RET)),
                  tile(HEAD_DIM), tile(HEAD_DIM),
                  _resident((RET_HEADS, CHUNK, CHUNK)), _resident((RET_HEADS, CHUNK, HEAD_DIM)),
                  _resident((RET_HEADS, CHUNK, HEAD_DIM)), _resident((RET_HEADS, 8, HEAD_DIM)),
                  xtile(D_MODEL), _resident1((D_MODEL, D_MODEL)), _resident((1, D_MODEL)), xtile(D_MODEL)],
        out_specs=[any_spec, any_spec, xtile(D_MODEL), xtile(D_MODEL),
                   _resident((2, N_CHIPS, 128, D_MODEL)), _resident((1, D_MODEL)), _resident((1, 128))],
        out_shape=[jax.ShapeDtypeStruct((rows, D_CONV), BF16),
                   jax.ShapeDtypeStruct(((nt + 1) * NCH, RET_HEADS, HEAD_DIM, HEAD_DIM), BF16),
                   jax.ShapeDtypeStruct((seq, D_MODEL), F32),
                   jax.ShapeDtypeStruct((seq, D_MODEL), BF16),
                   jax.ShapeDtypeStruct((2, N_CHIPS, 128, D_MODEL), F32),
                   jax.ShapeDtypeStruct((1, D_MODEL), F32),
                   jax.ShapeDtypeStruct((1, 128), F32)],
        scratch_shapes=[pltpu.VMEM((RET_HEADS, HEAD_DIM, HEAD_DIM), F32), pltpu.VMEM((8, D_CONV), F32),
                        pltpu.VMEM((2, TM, D_MODEL), BF16), pltpu.VMEM((2, TM, D_CONV), BF16),
                        pltpu.VMEM((2, NCH, RET_HEADS, HEAD_DIM, HEAD_DIM), BF16),
                        pltpu.VMEM((1, D_MODEL), F32), pltpu.SemaphoreType.DMA((4,))],
        compiler_params=_cparams(("arbitrary",)),
    )(proj, conv_w8, gret, tb["rc"], tb["rs"], tb["decay"], tb["xi"], tb["zeta"], tb["cd"], x, w_out, fg, target)


def _b2_b1a_call(proj, dmixed, conv_s, states, conv_w8, gret, tb, w_in_g, x, meta_tile, g1, dh2):
    nt, rows = tb["nt"], tb["rows"]
    seq = nt * TM

    def pb(r):
        return jnp.where(r >= nt, nt, nt - 1 - r)

    def xprev(r):
        return jnp.clip(nt - r, 0, nt - 1)

    def body(proj_ref, dmx_ref, conv_ref, states_ref, cw_ref, g_ref, rc_ref, rs_ref, dec_ref, dect_ref, xi_ref,
             zeta_ref, cd_ref, w_ref, x_ref, mt_ref, g1_ref, dh2_ref,
             dproj_hbm, gcw_ref, gg_ref, gx_ref, dmeta_ref, gn_ref,
             gstate, dchalo, dps, gx_keep, out_sems):
        r = pl.program_id(0)
        live = jnp.where(r >= nt, 0.0, 1.0)
        slot = lax.rem(r, 2)
        dproj_ref = dps.at[slot]

        def to_hbm(s, tile):
            return pltpu.make_async_copy(dps.at[s], dproj_hbm.at[pl.ds(pl.multiple_of(tile * TM, TM), TM), :],
                                         out_sems.at[s])

        @pl.when(r == 0)
        def _():
            gstate[...] = jnp.zeros_like(gstate)
            dchalo[...] = jnp.zeros_like(dchalo)
            gcw_ref[...] = jnp.zeros_like(gcw_ref)
            gg_ref[...] = jnp.zeros_like(gg_ref)
            gn_ref[...] = jnp.zeros_like(gn_ref)
            dmeta_ref[...] = jnp.zeros_like(dmeta_ref)
            gx_keep[...] = jnp.zeros_like(gx_keep)
            dps[...] = jnp.zeros_like(dps)

        @pl.when(r >= 2)
        def _():
            to_hbm(slot, pb(r - 2)).wait()

        dprev = dps.at[1 - slot]
        is_meta = r == nt + 1
        pieces = []

        def emit_piece():
            j = len(pieces)
            if j < N_CHIPS:
                p = _dot_tb(dprev[:, j * 1024:(j + 1) * 1024], w_ref[j])
                pieces.append(p if j == 0 else pieces[-1] + p)

        cx = proj_ref[:, CX:CX + 512].astype(F32)
        cb = proj_ref[:, CB:CB + 512].astype(F32)
        cc = proj_ref[:, CC:CC + 512].astype(F32)
        cg = proj_ref[:, CG:CG + 512].astype(F32)
        dco = dmx_ref[:, 0:D_CONV].astype(F32) * live
        conv = conv_ref[...].astype(F32)
        sg = _sigmoid(cg)
        sil = cg * sg
        t = dco * conv
        dproj_ref[:, CB:CB + 512] = (t * sil).astype(BF16)
        dproj_ref[:, CG:CG + 512] = (t * cb * (sg * (1.0 + cg * (1.0 - sg)))).astype(BF16)
        dconv = dco * cb * sil
        row = lax.broadcasted_iota(jnp.int32, (TM, D_CONV), 0)
        n0 = dchalo[0:1, :]
        n1 = dchalo[1:2, :]
        dc1 = jnp.where(row == TM - 1, n0, pltpu.roll(dconv, TM - 1, 0))
        dc2 = jnp.where(row == TM - 2, n0, jnp.where(row == TM - 1, n1, pltpu.roll(dconv, TM - 2, 0)))
        dchalo[...] = dconv[0:8, :]
        du = cw_ref[2:3, :] * dconv + cw_ref[1:2, :] * dc1 + cw_ref[0:1, :] * dc2
        u = cc * cx
        gcw_ref[2:3, :] += jnp.sum(u * dconv, axis=0, keepdims=True)
        gcw_ref[1:2, :] += jnp.sum(u * dc1, axis=0, keepdims=True)
        gcw_ref[0:1, :] += jnp.sum(u * dc2, axis=0, keepdims=True)
        dproj_ref[:, CC:CC + 512] = (du * cx).astype(BF16)
        dproj_ref[:, CX:CX + 512] = (du * cc).astype(BF16)
        emit_piece()

        scale = HEAD_DIM ** -0.5
        gs = {h: gstate[h] for h in range(RET_HEADS)}
        gg = {h: jnp.zeros((1, HEAD_DIM), F32) for h in range(RET_HEADS)}
        col = lambda base, h: slice(base + h * HEAD_DIM, base + (h + 1) * HEAD_DIM)
        rw = lambda c: slice(c * CHUNK, (c + 1) * CHUNK)
        for c0 in range(NCH - CHUNK_GROUP, -1, -CHUNK_GROUP):
            cs = range(c0 + CHUNK_GROUP - 1, c0 - 1, -1)
            U = [(c, h) for c in cs for h in range(RET_HEADS)]
            rc = {c: rc_ref[rw(c), :] for c in cs}
            rs = {c: rs_ref[rw(c), :] for c in cs}
            v = {(c, h): proj_ref[rw(c), col(CV, h)] for c, h in U}
            stb = {(c, h): states_ref[c, h] for c, h in U}
            qf = {(c, h): _rot(proj_ref[rw(c), col(CQ, h)].astype(F32), rc[c], rs[c]) * scale for c, h in U}
            kf = {(c, h): _rot(proj_ref[rw(c), col(CK, h)].astype(F32), rc[c], rs[c]) for c, h in U}
            qb = {u: qf[u].astype(BF16) for u in U}
            kb = {u: kf[u].astype(BF16) for u in U}
            qxb = {(c, h): (qf[c, h] * xi_ref[h]).astype(BF16) for c, h in U}
            kzb = {(c, h): (kf[c, h] * zeta_ref[h]).astype(BF16) for c, h in U}
            ab = {(c, h): (_dot_tb(qb[c, h], kb[c, h]) * dec_ref[h]).astype(BF16) for c, h in U}
            atb = {(c, h): (_dot_tb(kb[c, h], qb[c, h]) * dect_ref[h]).astype(BF16) for c, h in U}
            o = {u: _dot(ab[u], v[u]) + _dot(qxb[u], stb[u]) for u in U}
            emit_piece()
            dob = {}
            for c, h in U:
                mu = jnp.mean(o[c, h], axis=-1, keepdims=True)
                d = o[c, h] - mu
                var = jnp.mean(d * d, axis=-1, keepdims=True)
                rstd = lax.rsqrt(var + EPS)
                yh = d * rstd
                g = g_ref[:, col(0, h)]
                rg = proj_ref[rw(c), col(CR, h)].astype(F32)
                dro = dmx_ref[rw(c), col(D_CONV, h)].astype(F32) * live
                sg = _sigmoid(rg)
                dproj_ref[rw(c), col(CR, h)] = (dro * (yh * g) * (sg * (1.0 + rg * (1.0 - sg)))).astype(BF16)
                dret = dro * (rg * sg)
                gg[h] = gg[h] + jnp.sum(dret * yh, axis=0, keepdims=True)
                dyh = dret * g
                do = rstd * (dyh - jnp.mean(dyh, axis=-1, keepdims=True)
                             - yh * jnp.mean(dyh * yh, axis=-1, keepdims=True))
                dob[c, h] = do.astype(BF16)
            dv1 = {u: _dot(atb[u], dob[u]) for u in U}
            ds = {(c, h): (_dot_tb(dob[c, h], v[c, h]) * dec_ref[h]).astype(BF16) for c, h in U}
            dst = {(c, h): (_dot_tb(v[c, h], dob[c, h]) * dect_ref[h]).astype(BF16) for c, h in U}
            gup = {u: _dot_ta(qxb[u], dob[u]) for u in U}
            dq = {(c, h): _dot(ds[c, h], kb[c, h]) + _dot_tb(dob[c, h], stb[c, h]) * xi_ref[h] for c, h in U}
            dk1 = {u: _dot(dst[u], qb[u]) for u in U}
            emit_piece()
            for c, h in U:
                gsb = gs[h].astype(BF16)
                dv = dv1[c, h] + _dot(kzb[c, h], gsb)
                dk = dk1[c, h] + _dot_tb(v[c, h], gsb) * zeta_ref[h]
                gs[h] = cd_ref[h, 0:1, :] * gs[h] + gup[c, h]
                dproj_ref[rw(c), col(CQ, h)] = (_rot_t(dq[c, h], rc[c], rs[c]) * scale).astype(BF16)
                dproj_ref[rw(c), col(CK, h)] = _rot_t(dk, rc[c], rs[c]).astype(BF16)
                dproj_ref[rw(c), col(CV, h)] = dv.astype(BF16)
        for h in range(RET_HEADS):
            gstate[h] = gs[h]
            gg_ref[:, col(0, h)] += gg[h]

        while len(pieces) < N_CHIPS:
            emit_piece()
        dhn = pieces[-1]
        hx = jnp.where(is_meta, mt_ref[...], x_ref[...])
        ms = jnp.mean(hx * hx, axis=-1, keepdims=True)
        rstd1 = lax.rsqrt(ms + EPS)
        xh = hx * rstd1
        gn_ref[...] += jnp.sum(dhn * xh, axis=0, keepdims=True)
        dxh = dhn * g1_ref[...]
        dh = rstd1 * (dxh - xh * jnp.mean(dxh * xh, axis=-1, keepdims=True))
        gx = jnp.where(is_meta, gx_keep[...], dh + dh2_ref[...])
        gx_ref[...] = gx
        gx_keep[...] = gx
        dmeta_ref[...] = jnp.where(is_meta, dh[TM - N_META:TM, :], dmeta_ref[...])

        @pl.when(r <= nt)
        def _():
            to_hbm(slot, pb(r)).start()

        @pl.when(r == nt + 1)
        def _():
            to_hbm(1 - slot, pb(r - 1)).wait()

    tile = lambda w: pl.BlockSpec((TM, w), lambda r: (pb(r), 0))
    xtile = pl.BlockSpec((TM, D_MODEL), lambda r: (xprev(r), 0))
    return pl.pallas_call(
        body, name="b2_mixer_bwd_b1a_inproj_bwd_x",
        grid=(nt + 2,),
        in_specs=[tile(N_PROJ_COLS),
                  pl.BlockSpec((TM, D_MODEL), lambda r: (jnp.minimum(pb(r), nt - 1), 0)),
                  tile(D_CONV),
                  pl.BlockSpec((NCH, RET_HEADS, HEAD_DIM, HEAD_DIM), lambda r: (pb(r), 0, 0, 0)),
                  _resident((8, D_CONV)), _resident((1, D_RET)),
                  tile(HEAD_DIM), tile(HEAD_DIM),
                  _resident((RET_HEADS, CHUNK, CHUNK)), _resident((RET_HEADS, CHUNK, CHUNK)),
                  _resident((RET_HEADS, CHUNK, HEAD_DIM)),
                  _resident((RET_HEADS, CHUNK, HEAD_DIM)), _resident((RET_HEADS, 8, HEAD_DIM)),
                  _resident1((N_CHIPS, D_MODEL, 1024)), xtile, _resident1((TM, D_MODEL)), _resident((1, D_MODEL)),
                  xtile],
        out_specs=[pl.BlockSpec(memory_space=pl.ANY), _resident((8, D_CONV)), _resident((1, D_RET)),
                   xtile, _resident((N_META, D_MODEL)), _resident((1, D_MODEL))],
        out_shape=[jax.ShapeDtypeStruct((rows, N_PROJ_COLS), BF16),
                   jax.ShapeDtypeStruct((8, D_CONV), F32),
                   jax.ShapeDtypeStruct((1, D_RET), F32),
                   jax.ShapeDtypeStruct((seq, D_MODEL), F32),
                   jax.ShapeDtypeStruct((N_META, D_MODEL), F32),
                   jax.ShapeDtypeStruct((1, D_MODEL), F32)],
        scratch_shapes=[pltpu.VMEM((RET_HEADS, HEAD_DIM, HEAD_DIM), F32), pltpu.VMEM((8, D_CONV), F32),
                        pltpu.VMEM((2, TM, N_PROJ_COLS), BF16), pltpu.VMEM((TM, D_MODEL), F32),
                        pltpu.SemaphoreType.DMA((2,))],
        compiler_params=_cparams(("arbitrary",), vmem=VMEM_LIMIT_MAX),
    )(proj, dmixed, conv_s, states, conv_w8, gret, tb["rc"], tb["rs"], tb["decay"], tb["decay_t"], tb["xi"],
      tb["zeta"], tb["cd"], w_in_g, x, meta_tile, g1, dh2)


REL = (2, 1, 3)
SMALL_ROWS = 24
HALF_STEP = 2
WOUT_STEP = 4


def _rcopy(src, dst, send_sems, recv_sems, k, to):
    return pltpu.make_async_remote_copy(src_ref=src, dst_ref=dst, send_sem=send_sems.at[k],
                                        recv_sem=recv_sems.at[k], device_id=to, device_id_type=MESH_ID)


def _b1b_reduce_call(order, hnt, dproj, gwo, pack, nt):
    nk = nt + 1
    last = nk - 1
    any_spec = pl.BlockSpec(memory_space=pl.ANY)

    def body(order_ref, a_ref, b_ref, gwo_hbm, pack_hbm, gwin_hbm, gwout_hbm, tot_hbm,
             acc, sb, abuf, pb, bbuf, fin, go, ao, pbo, bo, fino, slots, totv, send_sems, recv_sems, loc_sems):
        jj, k = pl.program_id(0), pl.program_id(1)
        x, y, c = lax.axis_index("x"), lax.axis_index("y"), lax.axis_index("c")
        me, myid, sib = 2 * x + y, 4 * x + 2 * y + c, (x, y, 1 - c)
        rc = functools.partial(_rcopy, send_sems=send_sems, recv_sems=recv_sems)
        peers = [((1 - x) if r & 2 else x, (1 - y) if r & 1 else y, c) for r in REL]
        kids = [jnp.bitwise_xor(me, r) for r in REL]

        def dev_peer(r):
            return ((1 - x) if r & 4 else x, (1 - y) if r & 2 else y, (1 - c) if r & 1 else c)

        own_go = pltpu.make_async_copy(gwo_hbm.at[c], go, loc_sems.at[0])
        own_pack = pltpu.make_async_copy(pack_hbm, slots.at[0], loc_sems.at[1])
        wo_half = rc(gwo_hbm.at[1 - c], ao, k=8, to=sib)
        wo_part = [rc(pbo.at[kids[p]], bo.at[p], k=9 + p, to=peers[p]) for p in range(3)]
        sm = [rc(pack_hbm, slots.at[r], k=12 + r, to=dev_peer(r)) for r in range(1, N_DEV)]
        half = [rc(sb.at[j % 2, 1 - c], abuf.at[j], k=j, to=sib) for j in range(N_CHIPS)]
        part = [rc(pb.at[p], bbuf.at[p], k=4 + p, to=peers[p]) for p in range(3)]

        @pl.when(jnp.logical_and(jj == 0, k == 0))
        def _():
            own_go.start()
            own_pack.start()
            wo_half.start()
            for cp in sm:
                cp.start()

        @pl.when(k == 0)
        def _():
            acc[...] = jnp.zeros_like(acc)

        acc[0] += _dot(a_ref[0:512, :], b_ref[...])
        acc[1] += _dot(a_ref[512:1024, :], b_ref[...])

        @pl.when(jnp.logical_and(jj == 0, k == WOUT_STEP))
        def _():
            own_go.wait()
            wo_half.wait_recv()
            for j in range(N_CHIPS):
                go[j] = go[j] + ao[j]
            pbo[...] = go[...].astype(BF16)
            for cp in wo_part:
                cp.start()

        for j in range(N_CHIPS):
            @pl.when(jnp.logical_and(jj == j, k == last))
            def _(j=j):
                sb[j % 2] = acc[...]
                half[j].start()

        for p in range(3):
            @pl.when(jnp.logical_and(jj == p + 1, k == HALF_STEP))
            def _(p=p):
                half[p].wait_recv()
                half[p].wait_send()
                pb[p] = (sb[p % 2, c] + abuf[p]).astype(BF16)
                part[p].start()

        @pl.when(jnp.logical_and(jj == N_CHIPS - 1, k == last))
        def _():
            half[3].wait_recv()
            own = sb[1, c] + abuf[3]
            for cp in part:
                cp.wait_recv()
            fin[c] = ((own + bbuf[0].astype(F32)) + bbuf[1].astype(F32)) + bbuf[2].astype(F32)
            done = rc(fin.at[c], fin.at[c], k=7, to=sib)
            done.start()
            for cp in wo_part:
                cp.wait_recv()
            fino[c] = ((go[me] + bo[0].astype(F32)) + bo[1].astype(F32)) + bo[2].astype(F32)
            done_o = rc(fino.at[c], fino.at[c], k=12, to=sib)
            done_o.start()
            own_pack.wait()
            for cp in sm:
                cp.wait_recv()
            tot = slots[myid]
            for a in range(1, N_DEV):
                tot = tot + slots[jnp.bitwise_xor(myid, a)]
            totv[...] = tot
            out_t = pltpu.make_async_copy(totv, tot_hbm, loc_sems.at[1])
            out_t.start()
            rc(fin.at[1 - c], fin.at[1 - c], k=7, to=sib).wait_recv()
            out_w = pltpu.make_async_copy(fin, gwin_hbm, loc_sems.at[0])
            out_w.start()
            rc(fino.at[1 - c], fino.at[1 - c], k=12, to=sib).wait_recv()
            out_o = pltpu.make_async_copy(fino, gwout_hbm, loc_sems.at[2])
            out_o.start()
            for cp in [half[3]] + part + [done, wo_half] + wo_part + [done_o] + sm:
                cp.wait_send()
            out_t.wait()
            out_w.wait()
            out_o.wait()

    grid_spec = pltpu.PrefetchScalarGridSpec(
        num_scalar_prefetch=1,
        grid=(N_CHIPS, nk),
        in_specs=[pl.BlockSpec((D_MODEL, TM), lambda j, k, o: (0, k)),
                  pl.BlockSpec((TM, 1024), lambda j, k, o: (k, o[j])),
                  any_spec, any_spec],
        out_specs=[any_spec, any_spec, any_spec],
        scratch_shapes=[
            pltpu.VMEM((2, 512, 1024), F32),
            pltpu.VMEM((2, 2, 512, 1024), F32),
            pltpu.VMEM((N_CHIPS, 512, 1024), F32),
            pltpu.VMEM((3, 512, 1024), BF16),
            pltpu.VMEM((3, 512, 1024), BF16),
            pltpu.VMEM((2, 512, 1024), F32),
            pltpu.VMEM((N_CHIPS, 128, D_MODEL), F32),
            pltpu.VMEM((N_CHIPS, 128, D_MODEL), F32),
            pltpu.VMEM((N_CHIPS, 128, D_MODEL), BF16),
            pltpu.VMEM((3, 128, D_MODEL), BF16),
            pltpu.VMEM((2, 128, D_MODEL), F32),
            pltpu.VMEM((N_DEV, SMALL_ROWS, D_MODEL), F32),
            pltpu.VMEM((SMALL_ROWS, D_MODEL), F32),
            pltpu.SemaphoreType.DMA((20,)), pltpu.SemaphoreType.DMA((20,)), pltpu.SemaphoreType.DMA((3,))])
    return pl.pallas_call(
        body, name="b1b_inproj_bwd_w_reduce",
        grid_spec=grid_spec,
        out_shape=[jax.ShapeDtypeStruct((2, 512, 1024), F32),
                   jax.ShapeDtypeStruct((2, 128, D_MODEL), F32),
                   jax.ShapeDtypeStruct((SMALL_ROWS, D_MODEL), F32)],
        compiler_params=_cparams(("arbitrary", "arbitrary")),
    )(order, hnt, dproj, gwo, pack)


def _local_step(me, x, target, g1, gret, fg, win_sh, wout_sh, small_sh):
    seq = x.shape[0]
    tb = _tables(seq)
    nt = tb["nt"]
    g1r, gretr, fgr = g1.reshape(1, -1), gret.reshape(1, -1), fg.reshape(1, -1)
    order = jnp.stack([me, me ^ REL[0], me ^ REL[1], me ^ REL[2]]).astype(jnp.int32)

    proj, hnt, w_in_g, w_out_g, meta_tile, conv_w8 = _f1_gather_call(order, x, g1r, win_sh, wout_sh, small_sh, nt)
    w_out = w_out_g.reshape(D_MODEL, D_MODEL)
    conv_s, states, dh2, dmixed, g_wout, g_fg, loss = _f2_f3_call(proj, conv_w8, gretr, tb, x, w_out, fgr, target)
    dproj, g_cw8, g_gret, grad_x, g_meta, g_g1 = _b2_b1a_call(proj, dmixed, conv_s, states, conv_w8, gretr, tb,
                                                              w_in_g, x, meta_tile, g1r, dh2)
    return loss, grad_x, dict(w_out=g_wout, meta=g_meta, conv_w=g_cw8[0:3], norm1_g=g_g1,
                              ret_norm_g=g_gret, final_g=g_fg), hnt, dproj


def _adamw_update(w_ref, g_ref, m_ref, v_ref, d_ref, nm_ref, nv_ref):
    gg = g_ref[...]
    nm = ADAM_B1 * m_ref[...] + (1.0 - ADAM_B1) * gg
    nv = ADAM_B2 * v_ref[...] + (1.0 - ADAM_B2) * (gg * gg)
    m_hat = nm / (1.0 - ADAM_B1 ** ADAM_STEP)
    v_hat = nv / (1.0 - ADAM_B2 ** ADAM_STEP)
    d_ref[...] = -ADAM_LR * (m_hat / (jnp.sqrt(v_hat) + ADAM_EPS) + ADAM_WD * w_ref[...])
    nm_ref[...] = nm
    nv_ref[...] = nv


def _adamw_small_call(ws, gs, ms, vs):
    n = len(ws)

    def body(*refs):
        ins, outs = refs[:4 * n], refs[4 * n:]
        for i in range(n):
            _adamw_update(ins[i], ins[n + i], ins[2 * n + i], ins[3 * n + i],
                          outs[i], outs[n + i], outs[2 * n + i])

    shapes = [jax.ShapeDtypeStruct(w.shape, F32) for w in ws]
    outs = pl.pallas_call(body, name="adamw_small", out_shape=shapes * 3,
                          compiler_params=_cparams())(*ws, *gs, *ms, *vs)
    return outs[:n], outs[n:2 * n], outs[2 * n:]


def _adamw_call(w, g, m, v, name):
    shape = w.shape
    w2, g2, m2, v2 = (a.reshape(-1, shape[-1]) for a in (w, g, m, v))
    rows, cols = w2.shape
    br = 256 if rows % 256 == 0 else rows
    body = functools.partial(_adamw_update)
    spec = pl.BlockSpec((br, cols), lambda i: (i, 0))
    outs = pl.pallas_call(
        body, name=name, grid=(rows // br,),
        in_specs=[spec] * 4, out_specs=[spec] * 3,
        out_shape=[jax.ShapeDtypeStruct((rows, cols), F32)] * 3,
        compiler_params=_cparams(("arbitrary",)),
    )(w2, g2, m2, v2)
    return tuple(o.reshape(shape) for o in outs)


def _pad_to(a, rows, cols):
    return jnp.pad(a, ((0, rows - a.shape[0]), (0, cols - a.shape[1])))


def kernel(x, meta, norm1_g, w_in, conv_w, ret_norm_g, w_out, final_g, loss_target, m_meta, m_norm1_g, m_w_in, m_conv_w, m_ret_norm_g, m_w_out, m_final_g, v_meta, v_norm1_g, v_w_in, v_conv_w, v_ret_norm_g, v_w_out, v_final_g):
    me = 2 * lax.axis_index("x") + lax.axis_index("y")

    small_sh = jnp.concatenate([meta, _pad_to(conv_w, 8, 256)], axis=0)
    loss, grad_x, g, hnt, dproj = _local_step(me, x[0], loss_target[0], norm1_g, ret_norm_g, final_g,
                                              w_in.astype(BF16), w_out.astype(BF16), small_sh)

    vec = jnp.concatenate([g["norm1_g"], g["final_g"], _pad_to(g["ret_norm_g"], 1, D_MODEL),
                           _pad_to(g["conv_w"], 3, D_MODEL), _pad_to(loss, 2, D_MODEL)], axis=0)
    pack = jnp.concatenate([g["meta"], vec], axis=0)
    order = jnp.stack([me ^ REL[0], me ^ REL[1], me ^ REL[2], me]).astype(jnp.int32)
    g_win, g_wout, tot = _b1b_reduce_call(order, hnt, dproj, g["w_out"], pack, x.shape[1] // TM)
    g_win, g_wout = g_win.reshape(D_MODEL, 1024), g_wout.reshape(256, D_MODEL)
    g_meta = lax.dynamic_slice(tot, (0, me * 256), (N_META, 256))
    g_conv = lax.dynamic_slice(tot, (N_META + 3, me * 128), (3, 128))
    g_n1, g_fg, g_rn = tot[N_META], tot[N_META + 1], tot[N_META + 2, 0:D_RET]

    loss_tot = tot[N_META + 6, 0]

    grads = [g_meta, g_n1, g_win, g_conv, g_rn, g_wout, g_fg]
    ws = [meta, norm1_g, w_in, conv_w, ret_norm_g, w_out, final_g]
    ms = [m_meta, m_norm1_g, m_w_in, m_conv_w, m_ret_norm_g, m_w_out, m_final_g]
    vs = [v_meta, v_norm1_g, v_w_in, v_conv_w, v_ret_norm_g, v_w_out, v_final_g]
    names = ["meta", "norm1_g", "w_in", "conv_w", "ret_norm_g", "w_out", "final_g"]
    as2d = lambda a: a.reshape(1, -1) if a.ndim == 1 else a
    big = [i for i, n_ in enumerate(names) if n_ in ("w_in", "w_out")]
    small = [i for i in range(len(names)) if i not in big]
    deltas, new_ms, new_vs = [None] * 7, [None] * 7, [None] * 7
    for i in big:
        deltas[i], new_ms[i], new_vs[i] = _adamw_call(ws[i], grads[i], ms[i], vs[i], "adamw_" + names[i])
    sd, sm_, sv = _adamw_small_call(*[[as2d(t[i]) for i in small] for t in (ws, grads, ms, vs)])
    for j, i in enumerate(small):
        deltas[i], new_ms[i], new_vs[i] = (o[j].reshape(ws[i].shape) for o in (sd, sm_, sv))
    return (loss_tot, grad_x[None], *grads, *deltas, *new_ms, *new_vs)
```

```python
import functools

import jax
import jax.numpy as jnp
from jax import lax
from jax.experimental import pallas as pl
from jax.experimental.pallas import tpu as pltpu

F32 = jnp.float32
BF16 = jnp.bfloat16

D_MODEL = 1024
N_META = 16
D_CONV = 512
D_RET = 512
RET_HEADS = 4
HEAD_DIM = 128
CHUNK = 128
N_PROJ_COLS = 4096
ROPE_BASE = 10000.0
EPS = 1e-6
N_CHIPS = 4
N_DEV = 8

ADAM_LR = 0.001
ADAM_B1 = 0.9
ADAM_B2 = 0.999
ADAM_EPS = 1e-08
ADAM_WD = 0.01
ADAM_STEP = 10

TM = 512
NCH = TM // CHUNK
CHUNK_GROUP = 2
VMEM_LIMIT = 56 * 1024 * 1024
VMEM_LIMIT_MAX = 63 * 1024 * 1024

CX, CB, CC, CG, CQ, CK, CV, CR = (i * 512 for i in range(8))

MESH_ID = pl.DeviceIdType.MESH


def _cparams(sem=None, vmem=VMEM_LIMIT, **kw):
    return pltpu.CompilerParams(dimension_semantics=sem, vmem_limit_bytes=vmem, **kw)


def _sigmoid(x):
    return 1.0 / (1.0 + jnp.exp(-x))


def _dot(a, b):
    return jnp.dot(a, b, preferred_element_type=F32)


def _dot_tb(a, b):
    return lax.dot_general(a, b, (((1,), (1,)), ((), ())), preferred_element_type=F32)


def _dot_ta(a, b):
    return lax.dot_general(a, b, (((0,), (0,)), ((), ())), preferred_element_type=F32)


def _resident(shape):
    nd = len(shape)
    return pl.BlockSpec(shape, lambda *_: (0,) * nd)


def _resident1(shape):
    nd = len(shape)
    return pl.BlockSpec(shape, lambda *_: (0,) * nd, pipeline_mode=pl.Buffered(1))


def _tables(seq):
    nt = seq // TM
    rows = seq + TM
    half = HEAD_DIM // 2
    freqs = 1.0 / (ROPE_BASE ** (jnp.arange(half, dtype=F32) / half))
    tile_start = jnp.concatenate([jnp.arange(nt, dtype=F32), -jnp.ones((1,), F32)]) * TM
    ang_t = tile_start[:, None] * freqs[None, :]
    ang_r = (jnp.arange(TM, dtype=F32) + N_META)[:, None] * freqs[None, :]
    dup = lambda a: jnp.concatenate([a, a], axis=-1)
    tt = jnp.stack([dup(jnp.cos(ang_t)), dup(jnp.sin(ang_t))], axis=1)
    tt = jnp.pad(tt, ((0, 0), (0, 6), (0, 0)))
    cr2, sr2 = dup(jnp.cos(ang_r)), dup(jnp.sin(ang_r))
    sgn = jnp.concatenate([-jnp.ones((8, half), F32), jnp.ones((8, half), F32)], axis=-1)
    log_g = jnp.log(1.0 - 2.0 ** (-5.0 - jnp.arange(RET_HEADS, dtype=F32)))
    idx = jnp.arange(CHUNK, dtype=F32)
    diff = idx[:, None] - idx[None, :]
    decay = jnp.where(diff[None] >= 0, jnp.exp(diff[None] * log_g[:, None, None]), 0.0)
    zeta = jnp.exp((CHUNK - 1 - idx)[None, :] * log_g[:, None])
    xi = jnp.exp((idx + 1.0)[None, :] * log_g[:, None])
    cd = jnp.exp(CHUNK * log_g)
    zeta_b = jnp.broadcast_to(zeta[:, :, None], (RET_HEADS, CHUNK, HEAD_DIM))
    xi_b = jnp.broadcast_to(xi[:, :, None], (RET_HEADS, CHUNK, HEAD_DIM))
    cd_b = jnp.broadcast_to(cd[:, None, None], (RET_HEADS, 8, HEAD_DIM))
    return dict(nt=nt, rows=rows, tt=tt, cr2=cr2, sr2=sr2, sgn=sgn, decay=decay,
                decay_t=jnp.swapaxes(decay, 1, 2), zeta=zeta_b, xi=xi_b, cd=cd_b)


def _tile_rotary(tt_ref, cr_ref, sr_ref, sgn_ref):
    ct, st = tt_ref[0:1, :], tt_ref[1:2, :]
    cr, sr = cr_ref[...], sr_ref[...]
    return ct * cr - st * sr, (st * cr + ct * sr) * sgn_ref[0:1, :]


def _rot(t, rc, rs):
    return t * rc + pltpu.roll(t, HEAD_DIM // 2, 1) * rs


def _rot_t(dt, rc, rs):
    return dt * rc + pltpu.roll(dt * rs, HEAD_DIM // 2, 1)


def _f1_gather_call(order, x, g1, win_sh, wout_sh, small_sh, nt):
    nk = nt + 1
    rows = nk * TM
    any_spec = pl.BlockSpec(memory_space=pl.ANY)

    def body(order_ref, x_ref, g_ref, win_hbm, wout_hbm, sm_hbm,
             proj_ref, hnt_ref, wg_hbm, wog_hbm, mt_hbm, cw_hbm,
             wg, wog, smg, mt, cw, hbs, send_sems, recv_sems, loc_sems):
        jj, k = pl.program_id(0), pl.program_id(1)
        x, y, c = lax.axis_index("x"), lax.axis_index("y"), lax.axis_index("c")
        me, sib = 2 * x + y, (x, y, 1 - c)
        rc = functools.partial(_rcopy, send_sems=send_sems, recv_sems=recv_sems)
        peers = [((1 - x) if r & 2 else x, (1 - y) if r & 1 else y, c) for r in REL]
        kids = [jnp.bitwise_xor(me, r) for r in REL]
        hw, ho = pl.ds(c * 512, 512), pl.ds(c * 128, 128)
        hw2, ho2 = pl.ds((1 - c) * 512, 512), pl.ds((1 - c) * 128, 128)
        at = lambda j_, k_: jnp.logical_and(jj == j_, k == k_)

        sm_cp = [rc(sm_hbm, smg.at[me], k=p, to=peers[p]) for p in range(3)]
        win_cp = [rc(win_hbm.at[hw], wg.at[me, hw], k=3 + p, to=peers[p]) for p in range(3)]
        wout_cp = [rc(wout_hbm.at[ho], wog.at[me, ho], k=6 + p, to=peers[p]) for p in range(3)]
        sm_in = [rc(sm_hbm, smg.at[kids[p]], k=p, to=sib) for p in range(3)]
        win_in = [rc(win_hbm.at[hw], wg.at[kids[p], hw], k=3 + p, to=sib) for p in range(3)]
        wout_in = [rc(wout_hbm.at[ho], wog.at[kids[p], ho], k=6 + p, to=sib) for p in range(3)]
        win_fw = [rc(wg.at[kids[p], hw], wg.at[kids[p], hw], k=9 + p, to=sib) for p in range(3)]
        wout_fw = [rc(wog.at[kids[p], ho], wog.at[kids[p], ho], k=12 + p, to=sib) for p in range(3)]
        win_fw_in = [rc(wg.at[kids[p], hw2], wg.at[kids[p], hw2], k=9 + p, to=sib) for p in range(3)]
        wout_fw_in = [rc(wog.at[kids[p], ho2], wog.at[kids[p], ho2], k=12 + p, to=sib) for p in range(3)]
        own_w = pltpu.make_async_copy(win_hbm, wg.at[me], loc_sems.at[0])
        own_o = pltpu.make_async_copy(wout_hbm, wog.at[me], loc_sems.at[1])
        own_s = pltpu.make_async_copy(sm_hbm, smg.at[me], loc_sems.at[2])
        out_wg = pltpu.make_async_copy(wg, wg_hbm, loc_sems.at[3])
        out_wog = pltpu.make_async_copy(wog, wog_hbm, loc_sems.at[4])
        out_mt = pltpu.make_async_copy(mt, mt_hbm, loc_sems.at[5])
        out_cw = pltpu.make_async_copy(cw, cw_hbm, loc_sems.at[6])

        def pass_on(p):
            win_in[p].wait_recv()
            win_fw[p].start()

        @pl.when(k <= 1)
        def _():
            @pl.when(at(0, 0))
            def _():
                own_w.start()
                own_s.start()
                own_o.start()
                for cp in sm_cp + win_cp + wout_cp:
                    cp.start()
                own_w.wait()

            for p in range(3):
                @pl.when(at(p + 1, 0))
                def _(p=p):
                    win_fw_in[p].wait_recv()

            @pl.when(at(1, 1))
            def _():
                pass_on(1)

            @pl.when(at(3, 0))
            def _():
                out_wg.start()

            @pl.when(at(3, 1))
            def _():
                for p in range(3):
                    wout_in[p].wait_recv()
                    wout_fw[p].start()

        @pl.when(jnp.logical_and(jj == 0, k >= nk - 2))
        def _():
            @pl.when(k == nk - 2)
            def _():
                own_s.wait()
                for cp in sm_in:
                    cp.wait_recv()
                mt[...] = jnp.zeros_like(mt)
                cw[...] = jnp.zeros_like(cw)
                for j in range(N_CHIPS):
                    mt[TM - N_META:TM, j * 256:(j + 1) * 256] = smg[j, 0:N_META, :]
                    cw[0:3, j * 128:(j + 1) * 128] = smg[j, N_META:N_META + 3, 0:128]
                out_mt.start()
                out_cw.start()

            @pl.when(k == nk - 1)
            def _():
                pass_on(0)

        @pl.when(at(2, nk // 2))
        def _():
            pass_on(2)

        tile_rows = pl.ds(pl.multiple_of(k * TM, TM), TM)

        @pl.when(jj == 0)
        def _():
            h = jnp.where(k == nt, mt[...], x_ref[...])
            ms = jnp.mean(h * h, axis=-1, keepdims=True)
            hn = (h * lax.rsqrt(ms + EPS)) * g_ref[...]
            hb = hn.astype(BF16)
            hbs[tile_rows, :] = hb
            proj_ref[...] = _dot(hb, wg[order_ref[0]]).astype(BF16)
            hnt_ref[...] = hn.T.astype(BF16)

        @pl.when(jj > 0)
        def _():
            proj_ref[...] = _dot(hbs[tile_rows, :], wg[order_ref[jj]]).astype(BF16)

        @pl.when(at(3, nk - 1))
        def _():
            own_o.wait()
            for cp in wout_fw_in:
                cp.wait_recv()
            out_wog.start()
            for cp in sm_cp + win_cp + wout_cp + win_fw + wout_fw:
                cp.wait_send()
            for cp in (out_wg, out_wog, out_mt, out_cw):
                cp.wait()

    grid_spec = pltpu.PrefetchScalarGridSpec(
        num_scalar_prefetch=1,
        grid=(N_CHIPS, nk),
        in_specs=[pl.BlockSpec((TM, D_MODEL), lambda j, k, o: (jnp.where(j == 0, jnp.minimum(k, nt - 1), nt - 1), 0)),
                  pl.BlockSpec((1, D_MODEL), lambda j, k, o: (0, 0)),
                  any_spec, any_spec, any_spec],
        out_specs=[pl.BlockSpec((TM, 1024), lambda j, k, o: (k, o[j])),
                   pl.BlockSpec((D_MODEL, TM), lambda j, k, o: (0, jnp.where(j == 0, k, nk - 1))),
                   any_spec, any_spec, any_spec, any_spec],
        scratch_shapes=[
            pltpu.VMEM((N_CHIPS, D_MODEL, 1024), BF16),
            pltpu.VMEM((N_CHIPS, 256, D_MODEL), BF16),
            pltpu.VMEM((N_CHIPS, SMALL_ROWS, 256), F32),
            pltpu.VMEM((TM, D_MODEL), F32),
            pltpu.VMEM((8, D_CONV), F32),
            pltpu.VMEM((rows, D_MODEL), BF16),
            pltpu.SemaphoreType.DMA((15,)), pltpu.SemaphoreType.DMA((15,)), pltpu.SemaphoreType.DMA((7,))])
    return pl.pallas_call(
        body, name="f1_norm_inproj_gather",
        grid_spec=grid_spec,
        out_shape=[jax.ShapeDtypeStruct((rows, N_PROJ_COLS), BF16),
                   jax.ShapeDtypeStruct((D_MODEL, rows), BF16),
                   jax.ShapeDtypeStruct((N_CHIPS, D_MODEL, 1024), BF16),
                   jax.ShapeDtypeStruct((N_CHIPS, 256, D_MODEL), BF16),
                   jax.ShapeDtypeStruct((TM, D_MODEL), F32),
                   jax.ShapeDtypeStruct((8, D_CONV), F32)],
        compiler_params=_cparams(("arbitrary", "arbitrary")),
    )(order, x, g1, win_sh, wout_sh, small_sh)


def _f2_f3_call(proj, conv_w8, gret, tb, x, w_out, fg, target):
    nt, rows = tb["nt"], tb["rows"]
    seq = nt * TM

    def pf(s):
        return jnp.where(s == 0, nt, jnp.minimum(s - 1, nt - 1))

    def xt(s):
        return jnp.clip(s - 2, 0, nt - 1)

    def body(proj_ref, cw_ref, g_ref, tt_ref, cr_ref, sr_ref, sgn_ref, dec_ref, xi_ref, zeta_ref, cd_ref,
             x_ref, w_ref, fg_ref, t_ref,
             conv_hbm, states_hbm, dh2_ref, dmx_ref, gwo_ref, gfg_ref, loss_ref,
             state, uhalo, mxs, convs, sts, lacc, out_sems):
        s = pl.program_id(0)
        slot = lax.rem(s, 2)
        mixed_ref = mxs.at[slot]
        conv_ref = convs.at[slot]
        states_ref = sts.at[slot]

        def conv_out(sl, tile):
            return pltpu.make_async_copy(convs.at[sl], conv_hbm.at[pl.ds(pl.multiple_of(tile * TM, TM), TM), :],
                                         out_sems.at[sl])

        def states_out(sl, tile):
            return pltpu.make_async_copy(sts.at[sl], states_hbm.at[pl.ds(pl.multiple_of(tile * NCH, NCH), NCH)],
                                         out_sems.at[2 + sl])

        @pl.when(s == 0)
        def _():
            state[...] = jnp.zeros_like(state)
            uhalo[...] = jnp.zeros_like(uhalo)
            mxs[...] = jnp.zeros_like(mxs)
            gwo_ref[...] = jnp.zeros_like(gwo_ref)
            gfg_ref[...] = jnp.zeros_like(gfg_ref)
            lacc[...] = jnp.zeros_like(lacc)

        @pl.when(s >= 2)
        def _():
            conv_out(slot, pf(s - 2)).wait()
            states_out(slot, pf(s - 2)).wait()

        valid = jnp.where(s >= 2, 1.0, 0.0)
        mx_prev = mxs.at[1 - slot]
        f3 = {}

        def f3_fwd():
            f3["h2"] = x_ref[...] + _dot(mx_prev[...], w_ref[...])

        def f3_loss():
            h2 = f3.pop("h2")
            ms = jnp.mean(h2 * h2, axis=-1, keepdims=True)
            rstd = lax.rsqrt(ms + EPS)
            yh = h2 * rstd
            g = fg_ref[...]
            e = (yh * g - t_ref[...]) * valid
            lacc[...] += jnp.sum(e * e, axis=0, keepdims=True)
            dy = e * (1.0 / D_MODEL)
            gfg_ref[...] += jnp.sum(dy * yh, axis=0, keepdims=True)
            dyh = dy * g
            dh2 = rstd * (dyh - yh * jnp.mean(dyh * yh, axis=-1, keepdims=True))
            dh2_ref[...] = dh2
            f3["db"] = dh2.astype(BF16)

        def f3_dmx():
            dmx_ref[...] = _dot_tb(f3["db"], w_ref[...]).astype(BF16)

        def f3_gw():
            gw = _dot_ta(mx_prev[...], f3["db"])
            for j in range(N_CHIPS):
                for hf in range(2):
                    r0 = j * 256 + hf * 128
                    gwo_ref[hf, j] += gw[r0:r0 + 128, :]

        cx = proj_ref[:, CX:CX + 512].astype(F32)
        cc = proj_ref[:, CC:CC + 512].astype(F32)
        u = cc * cx
        row = lax.broadcasted_iota(jnp.int32, (TM, D_CONV), 0)
        h7 = uhalo[7:8, :]
        h6 = uhalo[6:7, :]
        u1 = jnp.where(row == 0, h7, pltpu.roll(u, 1, 0))
        u2 = jnp.where(row == 0, h6, jnp.where(row == 1, h7, pltpu.roll(u, 2, 0)))
        conv = cw_ref[2:3, :] * u + cw_ref[1:2, :] * u1 + cw_ref[0:1, :] * u2
        uhalo[...] = u[TM - 8:TM, :]
        cb = proj_ref[:, CB:CB + 512].astype(F32)
        cg = proj_ref[:, CG:CG + 512].astype(F32)
        mixed_ref[:, 0:D_CONV] = (cb * conv * (cg * _sigmoid(cg))).astype(BF16)
        conv_ref[...] = conv.astype(BF16)
        f3_fwd()

        scale = HEAD_DIM ** -0.5
        H = range(RET_HEADS)
        st = [state[h] for h in H]
        between = [f3_loss, f3_dmx, f3_gw, None]
        rc_t, rs_t = _tile_rotary(tt_ref, cr_ref, sr_ref, sgn_ref)
        for c in range(NCH):
            r0 = c * CHUNK
            rc = rc_t[r0:r0 + CHUNK, :]
            rs = rs_t[r0:r0 + CHUNK, :]
            col = lambda base, h: slice(base + h * HEAD_DIM, base + (h + 1) * HEAD_DIM)
            rws = slice(r0, r0 + CHUNK)
            v = [proj_ref[rws, col(CV, h)] for h in H]
            qf = [_rot(proj_ref[rws, col(CQ, h)].astype(F32), rc, rs) * scale for h in H]
            kf = [_rot(proj_ref[rws, col(CK, h)].astype(F32), rc, rs) for h in H]
            stb = [t.astype(BF16) for t in st]
            for h in H:
                states_ref[c, h] = stb[h]
            a = [(_dot_tb(qf[h].astype(BF16), kf[h].astype(BF16)) * dec_ref[h]).astype(BF16) for h in H]
            o = [_dot(a[h], v[h]) + _dot((qf[h] * xi_ref[h]).astype(BF16), stb[h]) for h in H]
            st = [cd_ref[h, 0:1, :] * st[h] + _dot_ta((kf[h] * zeta_ref[h]).astype(BF16), v[h]) for h in H]
            for h in H:
                mu = jnp.mean(o[h], axis=-1, keepdims=True)
                d = o[h] - mu
                var = jnp.mean(d * d, axis=-1, keepdims=True)
                yh = d * lax.rsqrt(var + EPS)
                rg = proj_ref[rws, col(CR, h)].astype(F32)
                mixed_ref[rws, col(D_CONV, h)] = (yh * g_ref[:, col(0, h)] * (rg * _sigmoid(rg))).astype(BF16)
            if between[c] is not None:
                between[c]()
        for h in H:
            state[h] = st[h]

        @pl.when(s <= nt)
        def _():
            conv_out(slot, pf(s)).start()
            states_out(slot, pf(s)).start()

        @pl.when(s == nt + 1)
        def _():
            conv_out(1 - slot, pf(s - 1)).wait()
            states_out(1 - slot, pf(s - 1)).wait()
            tot = jnp.sum(lacc[...], axis=1, keepdims=True) * (0.5 / D_MODEL)
            loss_ref[...] = jnp.broadcast_to(tot, (1, 128))

    tile = lambda w: pl.BlockSpec((TM, w), lambda s: (pf(s), 0))
    xtile = lambda w: pl.BlockSpec((TM, w), lambda s: (xt(s), 0))
    any_spec = pl.BlockSpec(memory_space=pl.ANY)
    return pl.pallas_call(
        body, name="f2_mixer_fwd_f3_outproj_loss",
        grid=(nt + 2,),
        in_specs=[tile(N_PROJ_COLS), _resident((8, D_CONV)), _resident((1, D_RET)),
                  pl.BlockSpec((None, 8, HEAD_DIM), lambda s: (pf(s), 0, 0)),
                  _resident((TM, HEAD_DIM)), _resident((TM, HEAD_DIM)), _resident((8, HEAD_DIM)),
                  _resident((RET_HEADS, CHUNK, CHUNK)), _resident((RET_HEADS, CHUNK, HEAD_DIM)),
                  _resident((RET_HEADS, CHUNK, HEAD_DIM)), _resident((RET_HEADS, 8, HEAD_DIM)),
                  xtile(D_MODEL), _resident1((D_MODEL, D_MODEL)), _resident((1, D_MODEL)), xtile(D_MODEL)],
        out_specs=[any_spec, any_spec, xtile(D_MODEL), xtile(D_MODEL),
                   _resident((2, N_CHIPS, 128, D_MODEL)), _resident((1, D_MODEL)), _resident((1, 128))],
        out_shape=[jax.ShapeDtypeStruct((rows, D_CONV), BF16),
                   jax.ShapeDtypeStruct(((nt + 1) * NCH, RET_HEADS, HEAD_DIM, HEAD_DIM), BF16),
                   jax.ShapeDtypeStruct((seq, D_MODEL), F32),
                   jax.ShapeDtypeStruct((seq, D_MODEL), BF16),
                   jax.ShapeDtypeStruct((2, N_CHIPS, 128, D_MODEL), F32),
                   jax.ShapeDtypeStruct((1, D_MODEL), F32),
                   jax.ShapeDtypeStruct((1, 128), F32)],
        scratch_shapes=[pltpu.VMEM((RET_HEADS, HEAD_DIM, HEAD_DIM), F32), pltpu.VMEM((8, D_CONV), F32),
                        pltpu.VMEM((2, TM, D_MODEL), BF16), pltpu.VMEM((2, TM, D_CONV), BF16),
                        pltpu.VMEM((2, NCH, RET_HEADS, HEAD_DIM, HEAD_DIM), BF16),
                        pltpu.VMEM((1, D_MODEL), F32), pltpu.SemaphoreType.DMA((4,))],
        compiler_params=_cparams(("arbitrary",)),
    )(proj, conv_w8, gret, tb["tt"], tb["cr2"], tb["sr2"], tb["sgn"], tb["decay"], tb["xi"], tb["zeta"], tb["cd"],
      x, w_out, fg, target)


def _b2_b1a_call(proj, dmixed, conv_s, states, conv_w8, gret, tb, w_in_g, x, meta_tile, g1, dh2):
    nt, rows = tb["nt"], tb["rows"]
    seq = nt * TM

    def pb(r):
        return jnp.where(r >= nt, nt, nt - 1 - r)

    def xprev(r):
        return jnp.clip(nt - r, 0, nt - 1)

    def body(proj_ref, dmx_ref, conv_ref, states_ref, cw_ref, g_ref, tt_ref, cr_ref, sr_ref, sgn_ref, dec_ref,
             dect_ref, xi_ref, zeta_ref, cd_ref, w_ref, x_ref, mt_ref, g1_ref, dh2_ref,
             dproj_hbm, gcw_ref, gg_ref, gx_ref, dmeta_ref, gn_ref,
             gstate, dchalo, dps, gx_keep, out_sems):
        r = pl.program_id(0)
        live = jnp.where(r >= nt, 0.0, 1.0)
        slot = lax.rem(r, 2)
        dproj_ref = dps.at[slot]

        def to_hbm(s, tile):
            return pltpu.make_async_copy(dps.at[s], dproj_hbm.at[pl.ds(pl.multiple_of(tile * TM, TM), TM), :],
                                         out_sems.at[s])

        @pl.when(r == 0)
        def _():
            gstate[...] = jnp.zeros_like(gstate)
            dchalo[...] = jnp.zeros_like(dchalo)
            gcw_ref[...] = jnp.zeros_like(gcw_ref)
            gg_ref[...] = jnp.zeros_like(gg_ref)
            gn_ref[...] = jnp.zeros_like(gn_ref)
            dmeta_ref[...] = jnp.zeros_like(dmeta_ref)
            gx_keep[...] = jnp.zeros_like(gx_keep)
            dps[...] = jnp.zeros_like(dps)

        @pl.when(r >= 2)
        def _():
            to_hbm(slot, pb(r - 2)).wait()

        dprev = dps.at[1 - slot]
        is_meta = r == nt + 1
        pieces = []

        def emit_piece():
            j = len(pieces)
            if j < N_CHIPS:
                p = _dot_tb(dprev[:, j * 1024:(j + 1) * 1024], w_ref[j])
                pieces.append(p if j == 0 else pieces[-1] + p)

        cx = proj_ref[:, CX:CX + 512].astype(F32)
        cb = proj_ref[:, CB:CB + 512].astype(F32)
        cc = proj_ref[:, CC:CC + 512].astype(F32)
        cg = proj_ref[:, CG:CG + 512].astype(F32)
        dco = dmx_ref[:, 0:D_CONV].astype(F32) * live
        conv = conv_ref[...].astype(F32)
        sg = _sigmoid(cg)
        sil = cg * sg
        t = dco * conv
        dproj_ref[:, CB:CB + 512] = (t * sil).astype(BF16)
        dproj_ref[:, CG:CG + 512] = (t * cb * (sg * (1.0 + cg * (1.0 - sg)))).astype(BF16)
        dconv = dco * cb * sil
        row = lax.broadcasted_iota(jnp.int32, (TM, D_CONV), 0)
        n0 = dchalo[0:1, :]
        n1 = dchalo[1:2, :]
        dc1 = jnp.where(row == TM - 1, n0, pltpu.roll(dconv, TM - 1, 0))
        dc2 = jnp.where(row == TM - 2, n0, jnp.where(row == TM - 1, n1, pltpu.roll(dconv, TM - 2, 0)))
        dchalo[...] = dconv[0:8, :]
        du = cw_ref[2:3, :] * dconv + cw_ref[1:2, :] * dc1 + cw_ref[0:1, :] * dc2
        u = cc * cx
        gcw_ref[2:3, :] += jnp.sum(u * dconv, axis=0, keepdims=True)
        gcw_ref[1:2, :] += jnp.sum(u * dc1, axis=0, keepdims=True)
        gcw_ref[0:1, :] += jnp.sum(u * dc2, axis=0, keepdims=True)
        dproj_ref[:, CC:CC + 512] = (du * cx).astype(BF16)
        dproj_ref[:, CX:CX + 512] = (du * cc).astype(BF16)
        emit_piece()

        scale = HEAD_DIM ** -0.5
        gs = {h: gstate[h] for h in range(RET_HEADS)}
        gg = {h: jnp.zeros((1, HEAD_DIM), F32) for h in range(RET_HEADS)}
        col = lambda base, h: slice(base + h * HEAD_DIM, base + (h + 1) * HEAD_DIM)
        rw = lambda c: slice(c * CHUNK, (c + 1) * CHUNK)
        rc_t, rs_t = _tile_rotary(tt_ref, cr_ref, sr_ref, sgn_ref)
        for c0 in range(NCH - CHUNK_GROUP, -1, -CHUNK_GROUP):
            cs = range(c0 + CHUNK_GROUP - 1, c0 - 1, -1)
            U = [(c, h) for c in cs for h in range(RET_HEADS)]
            rc = {c: rc_t[rw(c), :] for c in cs}
            rs = {c: rs_t[rw(c), :] for c in cs}
            v = {(c, h): proj_ref[rw(c), col(CV, h)] for c, h in U}
            stb = {(c, h): states_ref[c, h] for c, h in U}
            qf = {(c, h): _rot(proj_ref[rw(c), col(CQ, h)].astype(F32), rc[c], rs[c]) * scale for c, h in U}
            kf = {(c, h): _rot(proj_ref[rw(c), col(CK, h)].astype(F32), rc[c], rs[c]) for c, h in U}
            qb = {u: qf[u].astype(BF16) for u in U}
            kb = {u: kf[u].astype(BF16) for u in U}
            qxb = {(c, h): (qf[c, h] * xi_ref[h]).astype(BF16) for c, h in U}
            kzb = {(c, h): (kf[c, h] * zeta_ref[h]).astype(BF16) for c, h in U}
            ab = {(c, h): (_dot_tb(qb[c, h], kb[c, h]) * dec_ref[h]).astype(BF16) for c, h in U}
            atb = {(c, h): (_dot_tb(kb[c, h], qb[c, h]) * dect_ref[h]).astype(BF16) for c, h in U}
            o = {u: _dot(ab[u], v[u]) + _dot(qxb[u], stb[u]) for u in U}
            emit_piece()
            dob = {}
            for c, h in U:
                mu = jnp.mean(o[c, h], axis=-1, keepdims=True)
                d = o[c, h] - mu
                var = jnp.mean(d * d, axis=-1, keepdims=True)
                rstd = lax.rsqrt(var + EPS)
                yh = d * rstd
                g = g_ref[:, col(0, h)]
                rg = proj_ref[rw(c), col(CR, h)].astype(F32)
                dro = dmx_ref[rw(c), col(D_CONV, h)].astype(F32) * live
                sg = _sigmoid(rg)
                dproj_ref[rw(c), col(CR, h)] = (dro * (yh * g) * (sg * (1.0 + rg * (1.0 - sg)))).astype(BF16)
                dret = dro * (rg * sg)
                gg[h] = gg[h] + jnp.sum(dret * yh, axis=0, keepdims=True)
                dyh = dret * g
                do = rstd * (dyh - jnp.mean(dyh, axis=-1, keepdims=True)
                             - yh * jnp.mean(dyh * yh, axis=-1, keepdims=True))
                dob[c, h] = do.astype(BF16)
            dv1 = {u: _dot(atb[u], dob[u]) for u in U}
            ds = {(c, h): (_dot_tb(dob[c, h], v[c, h]) * dec_ref[h]).astype(BF16) for c, h in U}
            dst = {(c, h): (_dot_tb(v[c, h], dob[c, h]) * dect_ref[h]).astype(BF16) for c, h in U}
            gup = {u: _dot_ta(qxb[u], dob[u]) for u in U}
            dq = {(c, h): _dot(ds[c, h], kb[c, h]) + _dot_tb(dob[c, h], stb[c, h]) * xi_ref[h] for c, h in U}
            dk1 = {u: _dot(dst[u], qb[u]) for u in U}
            emit_piece()
            for c, h in U:
                gsb = gs[h].astype(BF16)
                dv = dv1[c, h] + _dot(kzb[c, h], gsb)
                dk = dk1[c, h] + _dot_tb(v[c, h], gsb) * zeta_ref[h]
                gs[h] = cd_ref[h, 0:1, :] * gs[h] + gup[c, h]
                dproj_ref[rw(c), col(CQ, h)] = (_rot_t(dq[c, h], rc[c], rs[c]) * scale).astype(BF16)
                dproj_ref[rw(c), col(CK, h)] = _rot_t(dk, rc[c], rs[c]).astype(BF16)
                dproj_ref[rw(c), col(CV, h)] = dv.astype(BF16)
        for h in range(RET_HEADS):
            gstate[h] = gs[h]
            gg_ref[:, col(0, h)] += gg[h]

        while len(pieces) < N_CHIPS:
            emit_piece()
        dhn = pieces[-1]
        hx = jnp.where(is_meta, mt_ref[...], x_ref[...])
        ms = jnp.mean(hx * hx, axis=-1, keepdims=True)
        rstd1 = lax.rsqrt(ms + EPS)
        xh = hx * rstd1
        gn_ref[...] += jnp.sum(dhn * xh, axis=0, keepdims=True)
        dxh = dhn * g1_ref[...]
        dh = rstd1 * (dxh - xh * jnp.mean(dxh * xh, axis=-1, keepdims=True))
        gx = jnp.where(is_meta, gx_keep[...], dh + dh2_ref[...])
        gx_ref[...] = gx
        gx_keep[...] = gx
        dmeta_ref[...] = jnp.where(is_meta, dh[TM - N_META:TM, :], dmeta_ref[...])

        @pl.when(r <= nt)
        def _():
            to_hbm(slot, pb(r)).start()

        @pl.when(r == nt + 1)
        def _():
            to_hbm(1 - slot, pb(r - 1)).wait()

    tile = lambda w: pl.BlockSpec((TM, w), lambda r: (pb(r), 0))
    xtile = pl.BlockSpec((TM, D_MODEL), lambda r: (xprev(r), 0))
    return pl.pallas_call(
        body, name="b2_mixer_bwd_b1a_inproj_bwd_x",
        grid=(nt + 2,),
        in_specs=[tile(N_PROJ_COLS),
                  pl.BlockSpec((TM, D_MODEL), lambda r: (jnp.minimum(pb(r), nt - 1), 0)),
                  tile(D_CONV),
                  pl.BlockSpec((NCH, RET_HEADS, HEAD_DIM, HEAD_DIM), lambda r: (pb(r), 0, 0, 0)),
                  _resident((8, D_CONV)), _resident((1, D_RET)),
                  pl.BlockSpec((None, 8, HEAD_DIM), lambda r: (pb(r), 0, 0)),
                  _resident((TM, HEAD_DIM)), _resident((TM, HEAD_DIM)), _resident((8, HEAD_DIM)),
                  _resident((RET_HEADS, CHUNK, CHUNK)), _resident((RET_HEADS, CHUNK, CHUNK)),
                  _resident((RET_HEADS, CHUNK, HEAD_DIM)),
                  _resident((RET_HEADS, CHUNK, HEAD_DIM)), _resident((RET_HEADS, 8, HEAD_DIM)),
                  _resident1((N_CHIPS, D_MODEL, 1024)), xtile, _resident1((TM, D_MODEL)), _resident((1, D_MODEL)),
                  xtile],
        out_specs=[pl.BlockSpec(memory_space=pl.ANY), _resident((8, D_CONV)), _resident((1, D_RET)),
                   xtile, _resident((N_META, D_MODEL)), _resident((1, D_MODEL))],
        out_shape=[jax.ShapeDtypeStruct((rows, N_PROJ_COLS), BF16),
                   jax.ShapeDtypeStruct((8, D_CONV), F32),
                   jax.ShapeDtypeStruct((1, D_RET), F32),
                   jax.ShapeDtypeStruct((seq, D_MODEL), F32),
                   jax.ShapeDtypeStruct((N_META, D_MODEL), F32),
                   jax.ShapeDtypeStruct((1, D_MODEL), F32)],
        scratch_shapes=[pltpu.VMEM((RET_HEADS, HEAD_DIM, HEAD_DIM), F32), pltpu.VMEM((8, D_CONV), F32),
                        pltpu.VMEM((2, TM, N_PROJ_COLS), BF16), pltpu.VMEM((TM, D_MODEL), F32),
                        pltpu.SemaphoreType.DMA((2,))],
        compiler_params=_cparams(("arbitrary",), vmem=VMEM_LIMIT_MAX),
    )(proj, dmixed, conv_s, states, conv_w8, gret, tb["tt"], tb["cr2"], tb["sr2"], tb["sgn"], tb["decay"],
      tb["decay_t"], tb["xi"], tb["zeta"], tb["cd"], w_in_g, x, meta_tile, g1, dh2)


REL = (2, 1, 3)
SMALL_ROWS = 24
HALF_STEP = 4


def _rcopy(src, dst, send_sems, recv_sems, k, to):
    return pltpu.make_async_remote_copy(src_ref=src, dst_ref=dst, send_sem=send_sems.at[k],
                                        recv_sem=recv_sems.at[k], device_id=to, device_id_type=MESH_ID)


def _b1b_reduce_call(order, hnt, dproj, gwo, pack, nt):
    nk = nt + 1
    last = nk - 1
    any_spec = pl.BlockSpec(memory_space=pl.ANY)

    def body(order_ref, a_ref, b_ref, gwo_hbm, pack_hbm, gwin_hbm, gwout_hbm, tot_hbm,
             acc, sb, abuf, pb, bbuf, fin, go, ao, pbo, bo, fino, slots, totv, send_sems, recv_sems, loc_sems):
        jj, k = pl.program_id(0), pl.program_id(1)
        x, y, c = lax.axis_index("x"), lax.axis_index("y"), lax.axis_index("c")
        me, myid, sib = 2 * x + y, 4 * x + 2 * y + c, (x, y, 1 - c)
        rc = functools.partial(_rcopy, send_sems=send_sems, recv_sems=recv_sems)
        peers = [((1 - x) if r & 2 else x, (1 - y) if r & 1 else y, c) for r in REL]
        kids = [jnp.bitwise_xor(me, r) for r in REL]

        def dev_peer(r):
            return ((1 - x) if r & 4 else x, (1 - y) if r & 2 else y, (1 - c) if r & 1 else c)

        own_go = pltpu.make_async_copy(gwo_hbm.at[c], go, loc_sems.at[0])
        own_pack = pltpu.make_async_copy(pack_hbm, slots.at[0], loc_sems.at[1])
        wo_half = rc(gwo_hbm.at[1 - c], ao, k=8, to=sib)
        wo_part = [rc(pbo.at[kids[p]], bo.at[p], k=9 + p, to=peers[p]) for p in range(3)]
        sm = [rc(pack_hbm, slots.at[r], k=12 + r, to=dev_peer(r)) for r in range(1, N_DEV)]
        half = [rc(sb.at[j % 2, 1 - c], abuf.at[j], k=j, to=sib) for j in range(N_CHIPS)]
        part = [rc(pb.at[p], bbuf.at[p], k=4 + p, to=peers[p]) for p in range(3)]

        @pl.when(jnp.logical_and(jj == 0, k == 0))
        def _():
            own_go.start()
            own_pack.start()
            wo_half.start()
            for cp in sm:
                cp.start()

        @pl.when(k == 0)
        def _():
            acc[...] = jnp.zeros_like(acc)

        acc[0] += _dot(a_ref[0:512, :], b_ref[...])
        acc[1] += _dot(a_ref[512:1024, :], b_ref[...])

        @pl.when(k == HALF_STEP)
        def _():
            @pl.when(jj == 0)
            def _():
                own_go.wait()
                wo_half.wait_recv()
                for j in range(N_CHIPS):
                    go[j] = go[j] + ao[j]
                pbo[...] = go[...].astype(BF16)
                for cp in wo_part:
                    cp.start()

            for p in range(3):
                @pl.when(jj == p + 1)
                def _(p=p):
                    half[p].wait_recv()
                    half[p].wait_send()
                    pb[p] = (sb[p % 2, c] + abuf[p]).astype(BF16)
                    part[p].start()

        @pl.when(k == last)
        def _():
            for j in range(N_CHIPS):
                @pl.when(jj == j)
                def _(j=j):
                    sb[j % 2] = acc[...]
                    half[j].start()

        @pl.when(jnp.logical_and(jj == N_CHIPS - 1, k == last))
        def _():
            half[3].wait_recv()
            own = sb[1, c] + abuf[3]
            for cp in part:
                cp.wait_recv()
            fin[c] = ((own + bbuf[0].astype(F32)) + bbuf[1].astype(F32)) + bbuf[2].astype(F32)
            done = rc(fin.at[c], fin.at[c], k=7, to=sib)
            done.start()
            for cp in wo_part:
                cp.wait_recv()
            fino[c] = ((go[me] + bo[0].astype(F32)) + bo[1].astype(F32)) + bo[2].astype(F32)
            done_o = rc(fino.at[c], fino.at[c], k=12, to=sib)
            done_o.start()
            own_pack.wait()
            for cp in sm:
                cp.wait_recv()
            tot = slots[myid]
            for a in range(1, N_DEV):
                tot = tot + slots[jnp.bitwise_xor(myid, a)]
            totv[...] = tot
            out_t = pltpu.make_async_copy(totv, tot_hbm, loc_sems.at[1])
            out_t.start()
            rc(fin.at[1 - c], fin.at[1 - c], k=7, to=sib).wait_recv()
            out_w = pltpu.make_async_copy(fin, gwin_hbm, loc_sems.at[0])
            out_w.start()
            rc(fino.at[1 - c], fino.at[1 - c], k=12, to=sib).wait_recv()
            out_o = pltpu.make_async_copy(fino, gwout_hbm, loc_sems.at[2])
            out_o.start()
            for cp in [half[3]] + part + [done, wo_half] + wo_part + [done_o] + sm:
                cp.wait_send()
            out_t.wait()
            out_w.wait()
            out_o.wait()

    grid_spec = pltpu.PrefetchScalarGridSpec(
        num_scalar_prefetch=1,
        grid=(N_CHIPS, nk),
        in_specs=[pl.BlockSpec((D_MODEL, TM), lambda j, k, o: (0, k)),
                  pl.BlockSpec((TM, 1024), lambda j, k, o: (k, o[j])),
                  any_spec, any_spec],
        out_specs=[any_spec, any_spec, any_spec],
        scratch_shapes=[
            pltpu.VMEM((2, 512, 1024), F32),
            pltpu.VMEM((2, 2, 512, 1024), F32),
            pltpu.VMEM((N_CHIPS, 512, 1024), F32),
            pltpu.VMEM((3, 512, 1024), BF16),
            pltpu.VMEM((3, 512, 1024), BF16),
            pltpu.VMEM((2, 512, 1024), F32),
            pltpu.VMEM((N_CHIPS, 128, D_MODEL), F32),
            pltpu.VMEM((N_CHIPS, 128, D_MODEL), F32),
            pltpu.VMEM((N_CHIPS, 128, D_MODEL), BF16),
            pltpu.VMEM((3, 128, D_MODEL), BF16),
            pltpu.VMEM((2, 128, D_MODEL), F32),
            pltpu.VMEM((N_DEV, SMALL_ROWS, D_MODEL), F32),
            pltpu.VMEM((SMALL_ROWS, D_MODEL), F32),
            pltpu.SemaphoreType.DMA((20,)), pltpu.SemaphoreType.DMA((20,)), pltpu.SemaphoreType.DMA((3,))])
    return pl.pallas_call(
        body, name="b1b_inproj_bwd_w_reduce",
        grid_spec=grid_spec,
        out_shape=[jax.ShapeDtypeStruct((2, 512, 1024), F32),
                   jax.ShapeDtypeStruct((2, 128, D_MODEL), F32),
                   jax.ShapeDtypeStruct((SMALL_ROWS, D_MODEL), F32)],
        compiler_params=_cparams(("arbitrary", "arbitrary")),
    )(order, hnt, dproj, gwo, pack)


def _local_step(me, x, target, g1, gret, fg, win_sh, wout_sh, small_sh):
    seq = x.shape[0]
    tb = _tables(seq)
    nt = tb["nt"]
    g1r, gretr, fgr = g1.reshape(1, -1), gret.reshape(1, -1), fg.reshape(1, -1)
    order = jnp.stack([me, me ^ REL[0], me ^ REL[1], me ^ REL[2]]).astype(jnp.int32)

    proj, hnt, w_in_g, w_out_g, meta_tile, conv_w8 = _f1_gather_call(order, x, g1r, win_sh, wout_sh, small_sh, nt)
    w_out = w_out_g.reshape(D_MODEL, D_MODEL)
    conv_s, states, dh2, dmixed, g_wout, g_fg, loss = _f2_f3_call(proj, conv_w8, gretr, tb, x, w_out, fgr, target)
    dproj, g_cw8, g_gret, grad_x, g_meta, g_g1 = _b2_b1a_call(proj, dmixed, conv_s, states, conv_w8, gretr, tb,
                                                              w_in_g, x, meta_tile, g1r, dh2)
    return loss, grad_x, dict(w_out=g_wout, meta=g_meta, conv_w=g_cw8[0:3], norm1_g=g_g1,
                              ret_norm_g=g_gret, final_g=g_fg), hnt, dproj


def _adamw_update(w_ref, g_ref, m_ref, v_ref, d_ref, nm_ref, nv_ref):
    gg = g_ref[...]
    nm = ADAM_B1 * m_ref[...] + (1.0 - ADAM_B1) * gg
    nv = ADAM_B2 * v_ref[...] + (1.0 - ADAM_B2) * (gg * gg)
    m_hat = nm / (1.0 - ADAM_B1 ** ADAM_STEP)
    v_hat = nv / (1.0 - ADAM_B2 ** ADAM_STEP)
    d_ref[...] = -ADAM_LR * (m_hat / (jnp.sqrt(v_hat) + ADAM_EPS) + ADAM_WD * w_ref[...])
    nm_ref[...] = nm
    nv_ref[...] = nv


def _adamw_small_call(ws, gs, ms, vs):
    n = len(ws)

    def body(*refs):
        ins, outs = refs[:4 * n], refs[4 * n:]
        for i in range(n):
            _adamw_update(ins[i], ins[n + i], ins[2 * n + i], ins[3 * n + i],
                          outs[i], outs[n + i], outs[2 * n + i])

    shapes = [jax.ShapeDtypeStruct(w.shape, F32) for w in ws]
    outs = pl.pallas_call(body, name="adamw_small", out_shape=shapes * 3,
                          compiler_params=_cparams())(*ws, *gs, *ms, *vs)
    return outs[:n], outs[n:2 * n], outs[2 * n:]


def _adamw_call(w, g, m, v, name):
    shape = w.shape
    w2, g2, m2, v2 = (a.reshape(-1, shape[-1]) for a in (w, g, m, v))
    rows, cols = w2.shape
    br = 256 if rows % 256 == 0 else rows
    body = functools.partial(_adamw_update)
    spec = pl.BlockSpec((br, cols), lambda i: (i, 0))
    outs = pl.pallas_call(
        body, name=name, grid=(rows // br,),
        in_specs=[spec] * 4, out_specs=[spec] * 3,
        out_shape=[jax.ShapeDtypeStruct((rows, cols), F32)] * 3,
        compiler_params=_cparams(("arbitrary",)),
    )(w2, g2, m2, v2)
    return tuple(o.reshape(shape) for o in outs)


def _pad_to(a, rows, cols):
    return jnp.pad(a, ((0, rows - a.shape[0]), (0, cols - a.shape[1])))


def kernel(x, meta, norm1_g, w_in, conv_w, ret_norm_g, w_out, final_g, loss_target, m_meta, m_norm1_g, m_w_in, m_conv_w, m_ret_norm_g, m_w_out, m_final_g, v_meta, v_norm1_g, v_w_in, v_conv_w, v_ret_norm_g, v_w_out, v_final_g):
    me = 2 * lax.axis_index("x") + lax.axis_index("y")

    small_sh = jnp.concatenate([meta, _pad_to(conv_w, 8, 256)], axis=0)
    loss, grad_x, g, hnt, dproj = _local_step(me, x[0], loss_target[0], norm1_g, ret_norm_g, final_g,
                                              w_in.astype(BF16), w_out.astype(BF16), small_sh)

    vec = jnp.concatenate([g["norm1_g"], g["final_g"], _pad_to(g["ret_norm_g"], 1, D_MODEL),
                           _pad_to(g["conv_w"], 3, D_MODEL), _pad_to(loss, 2, D_MODEL)], axis=0)
    pack = jnp.concatenate([g["meta"], vec], axis=0)
    order = jnp.stack([me ^ REL[0], me ^ REL[1], me ^ REL[2], me]).astype(jnp.int32)
    g_win, g_wout, tot = _b1b_reduce_call(order, hnt, dproj, g["w_out"], pack, x.shape[1] // TM)
    g_win, g_wout = g_win.reshape(D_MODEL, 1024), g_wout.reshape(256, D_MODEL)
    g_meta = lax.dynamic_slice(tot, (0, me * 256), (N_META, 256))
    g_conv = lax.dynamic_slice(tot, (N_META + 3, me * 128), (3, 128))
    g_n1, g_fg, g_rn = tot[N_META], tot[N_META + 1], tot[N_META + 2, 0:D_RET]

    loss_tot = tot[N_META + 6, 0]

    grads = [g_meta, g_n1, g_win, g_conv, g_rn, g_wout, g_fg]
    ws = [meta, norm1_g, w_in, conv_w, ret_norm_g, w_out, final_g]
    ms = [m_meta, m_norm1_g, m_w_in, m_conv_w, m_ret_norm_g, m_w_out, m_final_g]
    vs = [v_meta, v_norm1_g, v_w_in, v_conv_w, v_ret_norm_g, v_w_out, v_final_g]
    names = ["meta", "norm1_g", "w_in", "conv_w", "ret_norm_g", "w_out", "final_g"]
    as2d = lambda a: a.reshape(1, -1) if a.ndim == 1 else a
    big = [i for i, n_ in enumerate(names) if n_ in ("w_in", "w_out")]
    small = [i for i in range(len(names)) if i not in big]
    deltas, new_ms, new_vs = [None] * 7, [None] * 7, [None] * 7
    for i in big:
        deltas[i], new_ms[i], new_vs[i] = _adamw_call(ws[i], grads[i], ms[i], vs[i], "adamw_" + names[i])
    sd, sm_, sv = _adamw_small_call(*[[as2d(t[i]) for i in small] for t in (ws, grads, ms, vs)])
    for j, i in enumerate(small):
        deltas[i], new_ms[i], new_vs[i] = (o[j].reshape(ws[i].shape) for o in (sd, sm_, sv))
    return (loss_tot, grad_x[None], *grads, *deltas, *new_ms, *new_vs)
```

```python
import functools

import jax
import jax.numpy as jnp
from jax import lax
from jax.experimental import pallas as pl
from jax.experimental.pallas import tpu as pltpu

F32 = jnp.float32
BF16 = jnp.bfloat16

D_MODEL = 1024
N_META = 16
D_CONV = 512
D_RET = 512
RET_HEADS = 4
HEAD_DIM = 128
CHUNK = 128
N_PROJ_COLS = 4096
ROPE_BASE = 10000.0
EPS = 1e-6
N_CHIPS = 4
N_DEV = 8

ADAM_LR = 0.001
ADAM_B1 = 0.9
ADAM_B2 = 0.999
ADAM_EPS = 1e-08
ADAM_WD = 0.01
ADAM_STEP = 10

TM = 512
NCH = TM // CHUNK
CHUNK_GROUP = 2
VMEM_LIMIT = 56 * 1024 * 1024
VMEM_LIMIT_MAX = 63 * 1024 * 1024

CX, CB, CC, CG, CQ, CK, CV, CR = (i * 512 for i in range(8))

MESH_ID = pl.DeviceIdType.MESH


def _cparams(sem=None, vmem=VMEM_LIMIT, **kw):
    return pltpu.CompilerParams(dimension_semantics=sem, vmem_limit_bytes=vmem, **kw)


def _sigmoid(x):
    return 1.0 / (1.0 + jnp.exp(-x))


def _dot(a, b):
    return jnp.dot(a, b, preferred_element_type=F32)


def _dot_tb(a, b):
    return lax.dot_general(a, b, (((1,), (1,)), ((), ())), preferred_element_type=F32)


def _dot_ta(a, b):
    return lax.dot_general(a, b, (((0,), (0,)), ((), ())), preferred_element_type=F32)


def _resident(shape):
    nd = len(shape)
    return pl.BlockSpec(shape, lambda *_: (0,) * nd)


def _resident1(shape):
    nd = len(shape)
    return pl.BlockSpec(shape, lambda *_: (0,) * nd, pipeline_mode=pl.Buffered(1))


def _tables(seq):
    nt = seq // TM
    rows = seq + TM
    half = HEAD_DIM // 2
    freqs = 1.0 / (ROPE_BASE ** (jnp.arange(half, dtype=F32) / half))
    tile_start = jnp.concatenate([jnp.arange(nt, dtype=F32), -jnp.ones((1,), F32)]) * TM
    ang_t = tile_start[:, None] * freqs[None, :]
    ang_r = (jnp.arange(TM, dtype=F32) + N_META)[:, None] * freqs[None, :]
    dup = lambda a: jnp.concatenate([a, a], axis=-1)
    tt = jnp.stack([dup(jnp.cos(ang_t)), dup(jnp.sin(ang_t))], axis=1)
    tt = jnp.pad(tt, ((0, 0), (0, 6), (0, 0)))
    cr2, sr2 = dup(jnp.cos(ang_r)), dup(jnp.sin(ang_r))
    sgn = jnp.concatenate([-jnp.ones((8, half), F32), jnp.ones((8, half), F32)], axis=-1)
    log_g = jnp.log(1.0 - 2.0 ** (-5.0 - jnp.arange(RET_HEADS, dtype=F32)))
    idx = jnp.arange(CHUNK, dtype=F32)
    diff = idx[:, None] - idx[None, :]
    decay = jnp.where(diff[None] >= 0, jnp.exp(diff[None] * log_g[:, None, None]), 0.0)
    zeta = jnp.exp((CHUNK - 1 - idx)[None, :] * log_g[:, None])
    xi = jnp.exp((idx + 1.0)[None, :] * log_g[:, None])
    cd = jnp.exp(CHUNK * log_g)
    zeta_b = jnp.broadcast_to(zeta[:, :, None], (RET_HEADS, CHUNK, HEAD_DIM))
    xi_b = jnp.broadcast_to(xi[:, :, None], (RET_HEADS, CHUNK, HEAD_DIM))
    cd_b = jnp.broadcast_to(cd[:, None, None], (RET_HEADS, 8, HEAD_DIM))
    return dict(nt=nt, rows=rows, tt=tt, cr2=cr2, sr2=sr2, sgn=sgn, decay=decay,
                decay_t=jnp.swapaxes(decay, 1, 2), zeta=zeta_b, xi=xi_b, cd=cd_b)


def _tile_rotary(tt_ref, cr_ref, sr_ref, sgn_ref):
    ct, st = tt_ref[0:1, :], tt_ref[1:2, :]
    cr, sr = cr_ref[...], sr_ref[...]
    return ct * cr - st * sr, (st * cr + ct * sr) * sgn_ref[0:1, :]


def _rot(t, rc, rs):
    return t * rc + pltpu.roll(t, HEAD_DIM // 2, 1) * rs


def _rot_t(dt, rc, rs):
    return dt * rc + pltpu.roll(dt * rs, HEAD_DIM // 2, 1)


def _f1_gather_call(order, x, g1, win_sh, wout_sh, small_sh, nt):
    nk = nt + 1
    rows = nk * TM
    any_spec = pl.BlockSpec(memory_space=pl.ANY)

    def body(order_ref, x_ref, g_ref, win_hbm, wout_hbm, sm_hbm,
             proj_ref, hnt_ref, wg_hbm, wog_hbm, mt_hbm, cw_hbm,
             wg, wog, smg, mt, cw, hbs, send_sems, recv_sems, loc_sems):
        jj, k = pl.program_id(0), pl.program_id(1)
        x, y, c = lax.axis_index("x"), lax.axis_index("y"), lax.axis_index("c")
        me, sib = 2 * x + y, (x, y, 1 - c)
        rc = functools.partial(_rcopy, send_sems=send_sems, recv_sems=recv_sems)
        peers = [((1 - x) if r & 2 else x, (1 - y) if r & 1 else y, c) for r in REL]
        kids = [jnp.bitwise_xor(me, r) for r in REL]
        hw, ho = pl.ds(c * 512, 512), pl.ds(c * 128, 128)
        hw2, ho2 = pl.ds((1 - c) * 512, 512), pl.ds((1 - c) * 128, 128)
        at = lambda j_, k_: jnp.logical_and(jj == j_, k == k_)

        sm_cp = [rc(sm_hbm, smg.at[me], k=p, to=peers[p]) for p in range(3)]
        win_cp = [rc(win_hbm.at[hw], wg.at[me, hw], k=3 + p, to=peers[p]) for p in range(3)]
        wout_cp = [rc(wout_hbm.at[ho], wog.at[me, ho], k=6 + p, to=peers[p]) for p in range(3)]
        sm_in = [rc(sm_hbm, smg.at[kids[p]], k=p, to=sib) for p in range(3)]
        win_in = [rc(win_hbm.at[hw], wg.at[kids[p], hw], k=3 + p, to=sib) for p in range(3)]
        wout_in = [rc(wout_hbm.at[ho], wog.at[kids[p], ho], k=6 + p, to=sib) for p in range(3)]
        win_fw = [rc(wg.at[kids[p], hw], wg.at[kids[p], hw], k=9 + p, to=sib) for p in range(3)]
        wout_fw = [rc(wog.at[kids[p], ho], wog.at[kids[p], ho], k=12 + p, to=sib) for p in range(3)]
        win_fw_in = [rc(wg.at[kids[p], hw2], wg.at[kids[p], hw2], k=9 + p, to=sib) for p in range(3)]
        wout_fw_in = [rc(wog.at[kids[p], ho2], wog.at[kids[p], ho2], k=12 + p, to=sib) for p in range(3)]
        own_w = pltpu.make_async_copy(win_hbm, wg.at[me], loc_sems.at[0])
        own_o = pltpu.make_async_copy(wout_hbm, wog.at[me], loc_sems.at[1])
        own_s = pltpu.make_async_copy(sm_hbm, smg.at[me], loc_sems.at[2])
        out_wg = pltpu.make_async_copy(wg, wg_hbm, loc_sems.at[3])
        out_wog = pltpu.make_async_copy(wog, wog_hbm, loc_sems.at[4])
        out_mt = pltpu.make_async_copy(mt, mt_hbm, loc_sems.at[5])
        out_cw = pltpu.make_async_copy(cw, cw_hbm, loc_sems.at[6])

        def pass_on(p):
            win_in[p].wait_recv()
            win_fw[p].start()

        @pl.when(k <= 1)
        def _():
            @pl.when(at(0, 0))
            def _():
                own_w.start()
                own_s.start()
                own_o.start()
                for cp in sm_cp + win_cp + wout_cp:
                    cp.start()
                own_w.wait()

            for p in range(3):
                @pl.when(at(p + 1, 0))
                def _(p=p):
                    win_fw_in[p].wait_recv()

            @pl.when(at(1, 1))
            def _():
                pass_on(1)

            @pl.when(at(3, 0))
            def _():
                out_wg.start()

            @pl.when(at(3, 1))
            def _():
                for p in range(3):
                    wout_in[p].wait_recv()
                    wout_fw[p].start()

        @pl.when(jnp.logical_and(jj == 0, k >= nk - 2))
        def _():
            @pl.when(k == nk - 2)
            def _():
                own_s.wait()
                for cp in sm_in:
                    cp.wait_recv()
                mt[...] = jnp.zeros_like(mt)
                cw[...] = jnp.zeros_like(cw)
                for j in range(N_CHIPS):
                    mt[TM - N_META:TM, j * 256:(j + 1) * 256] = smg[j, 0:N_META, :]
                    cw[0:3, j * 128:(j + 1) * 128] = smg[j, N_META:N_META + 3, 0:128]
                out_mt.start()
                out_cw.start()

            @pl.when(k == nk - 1)
            def _():
                pass_on(0)

        @pl.when(at(2, nk // 2))
        def _():
            pass_on(2)

        tile_rows = pl.ds(pl.multiple_of(k * TM, TM), TM)

        @pl.when(jj == 0)
        def _():
            h = jnp.where(k == nt, mt[...], x_ref[...])
            ms = jnp.mean(h * h, axis=-1, keepdims=True)
            hn = (h * lax.rsqrt(ms + EPS)) * g_ref[...]
            hb = hn.astype(BF16)
            hbs[tile_rows, :] = hb
            proj_ref[...] = _dot(hb, wg[order_ref[0]]).astype(BF16)
            hnt_ref[...] = hn.T.astype(BF16)

        @pl.when(jj > 0)
        def _():
            proj_ref[...] = _dot(hbs[tile_rows, :], wg[order_ref[jj]]).astype(BF16)

        @pl.when(at(3, nk - 1))
        def _():
            own_o.wait()
            for cp in wout_fw_in:
                cp.wait_recv()
            out_wog.start()
            for cp in sm_cp + win_cp + wout_cp + win_fw + wout_fw:
                cp.wait_send()
            for cp in (out_wg, out_wog, out_mt, out_cw):
                cp.wait()

    grid_spec = pltpu.PrefetchScalarGridSpec(
        num_scalar_prefetch=1,
        grid=(N_CHIPS, nk),
        in_specs=[pl.BlockSpec((TM, D_MODEL), lambda j, k, o: (jnp.where(j == 0, jnp.minimum(k, nt - 1), nt - 1), 0)),
                  pl.BlockSpec((1, D_MODEL), lambda j, k, o: (0, 0)),
                  any_spec, any_spec, any_spec],
        out_specs=[pl.BlockSpec((TM, 1024), lambda j, k, o: (k, o[j])),
                   pl.BlockSpec((D_MODEL, TM), lambda j, k, o: (0, jnp.where(j == 0, k, nk - 1))),
                   any_spec, any_spec, any_spec, any_spec],
        scratch_shapes=[
            pltpu.VMEM((N_CHIPS, D_MODEL, 1024), BF16),
            pltpu.VMEM((N_CHIPS, 256, D_MODEL), BF16),
            pltpu.VMEM((N_CHIPS, SMALL_ROWS, 256), F32),
            pltpu.VMEM((TM, D_MODEL), F32),
            pltpu.VMEM((8, D_CONV), F32),
            pltpu.VMEM((rows, D_MODEL), BF16),
            pltpu.SemaphoreType.DMA((15,)), pltpu.SemaphoreType.DMA((15,)), pltpu.SemaphoreType.DMA((7,))])
    return pl.pallas_call(
        body, name="f1_norm_inproj_gather",
        grid_spec=grid_spec,
        out_shape=[jax.ShapeDtypeStruct((rows, N_PROJ_COLS), BF16),
                   jax.ShapeDtypeStruct((D_MODEL, rows), BF16),
                   jax.ShapeDtypeStruct((N_CHIPS, D_MODEL, 1024), BF16),
                   jax.ShapeDtypeStruct((N_CHIPS, 256, D_MODEL), BF16),
                   jax.ShapeDtypeStruct((TM, D_MODEL), F32),
                   jax.ShapeDtypeStruct((8, D_CONV), F32)],
        compiler_params=_cparams(("arbitrary", "arbitrary")),
    )(order, x, g1, win_sh, wout_sh, small_sh)


def _f2_f3_call(proj, conv_w8, gret, tb, x, w_out, fg, target):
    nt, rows = tb["nt"], tb["rows"]
    seq = nt * TM

    def pf(s):
        return jnp.where(s == 0, nt, jnp.minimum(s - 1, nt - 1))

    def xt(s):
        return jnp.clip(s - 2, 0, nt - 1)

    def body(proj_ref, cw_ref, g_ref, tt_ref, cr_ref, sr_ref, sgn_ref, dec_ref, xi_ref, zeta_ref, cd_ref,
             x_ref, w_ref, fg_ref, t_ref,
             conv_hbm, states_hbm, dh2_ref, dmx_ref, gwo_ref, gfg_ref, loss_ref,
             state, uhalo, mxs, convs, sts, lacc, out_sems):
        s = pl.program_id(0)
        slot = lax.rem(s, 2)
        mixed_ref = mxs.at[slot]
        conv_ref = convs.at[slot]
        states_ref = sts.at[slot]

        def conv_out(sl, tile):
            return pltpu.make_async_copy(convs.at[sl], conv_hbm.at[pl.ds(pl.multiple_of(tile * TM, TM), TM), :],
                                         out_sems.at[sl])

        def states_out(sl, tile):
            return pltpu.make_async_copy(sts.at[sl], states_hbm.at[pl.ds(pl.multiple_of(tile * NCH, NCH), NCH)],
                                         out_sems.at[2 + sl])

        @pl.when(s == 0)
        def _():
            state[...] = jnp.zeros_like(state)
            uhalo[...] = jnp.zeros_like(uhalo)
            mxs[...] = jnp.zeros_like(mxs)
            gwo_ref[...] = jnp.zeros_like(gwo_ref)
            gfg_ref[...] = jnp.zeros_like(gfg_ref)
            lacc[...] = jnp.zeros_like(lacc)

        @pl.when(s >= 2)
        def _():
            conv_out(slot, pf(s - 2)).wait()
            states_out(slot, pf(s - 2)).wait()

        valid = jnp.where(s >= 2, 1.0, 0.0)
        mx_prev = mxs.at[1 - slot]
        f3 = {}

        def f3_fwd():
            f3["h2"] = x_ref[...] + _dot(mx_prev[...], w_ref[...])

        def f3_loss():
            h2 = f3.pop("h2")
            ms = jnp.mean(h2 * h2, axis=-1, keepdims=True)
            rstd = lax.rsqrt(ms + EPS)
            yh = h2 * rstd
            g = fg_ref[...]
            e = (yh * g - t_ref[...]) * valid
            lacc[...] += jnp.sum(e * e, axis=0, keepdims=True)
            dy = e * (1.0 / D_MODEL)
            gfg_ref[...] += jnp.sum(dy * yh, axis=0, keepdims=True)
            dyh = dy * g
            dh2 = rstd * (dyh - yh * jnp.mean(dyh * yh, axis=-1, keepdims=True))
            dh2_ref[...] = dh2
            f3["db"] = dh2.astype(BF16)

        def f3_dmx():
            dmx_ref[...] = _dot_tb(f3["db"], w_ref[...]).astype(BF16)

        def f3_gw():
            gw = _dot_ta(mx_prev[...], f3["db"])
            for j in range(N_CHIPS):
                for hf in range(2):
                    r0 = j * 256 + hf * 128
                    gwo_ref[hf, j] += gw[r0:r0 + 128, :]

        cx = proj_ref[:, CX:CX + 512].astype(F32)
        cc = proj_ref[:, CC:CC + 512].astype(F32)
        u = cc * cx
        row = lax.broadcasted_iota(jnp.int32, (TM, D_CONV), 0)
        h7 = uhalo[7:8, :]
        h6 = uhalo[6:7, :]
        u1 = jnp.where(row == 0, h7, pltpu.roll(u, 1, 0))
        u2 = jnp.where(row == 0, h6, jnp.where(row == 1, h7, pltpu.roll(u, 2, 0)))
        conv = cw_ref[2:3, :] * u + cw_ref[1:2, :] * u1 + cw_ref[0:1, :] * u2
        uhalo[...] = u[TM - 8:TM, :]
        cb = proj_ref[:, CB:CB + 512].astype(F32)
        cg = proj_ref[:, CG:CG + 512].astype(F32)
        mixed_ref[:, 0:D_CONV] = (cb * conv * (cg * _sigmoid(cg))).astype(BF16)
        conv_ref[...] = conv.astype(BF16)
        f3_fwd()

        scale = HEAD_DIM ** -0.5
        H = range(RET_HEADS)
        st = [state[h] for h in H]
        between = [f3_loss, f3_dmx, f3_gw, None]
        rc_t, rs_t = _tile_rotary(tt_ref, cr_ref, sr_ref, sgn_ref)
        for c in range(NCH):
            r0 = c * CHUNK
            rc = rc_t[r0:r0 + CHUNK, :]
            rs = rs_t[r0:r0 + CHUNK, :]
            col = lambda base, h: slice(base + h * HEAD_DIM, base + (h + 1) * HEAD_DIM)
            rws = slice(r0, r0 + CHUNK)
            v = [proj_ref[rws, col(CV, h)] for h in H]
            qf = [_rot(proj_ref[rws, col(CQ, h)].astype(F32), rc, rs) * scale for h in H]
            kf = [_rot(proj_ref[rws, col(CK, h)].astype(F32), rc, rs) for h in H]
            stb = [t.astype(BF16) for t in st]
            for h in H:
                states_ref[c, h] = stb[h]
            a = [(_dot_tb(qf[h].astype(BF16), kf[h].astype(BF16)) * dec_ref[h]).astype(BF16) for h in H]
            o = [_dot(a[h], v[h]) + _dot((qf[h] * xi_ref[h]).astype(BF16), stb[h]) for h in H]
            st = [cd_ref[h, 0:1, :] * st[h] + _dot_ta((kf[h] * zeta_ref[h]).astype(BF16), v[h]) for h in H]
            for h in H:
                mu = jnp.mean(o[h], axis=-1, keepdims=True)
                d = o[h] - mu
                var = jnp.mean(d * d, axis=-1, keepdims=True)
                yh = d * lax.rsqrt(var + EPS)
                rg = proj_ref[rws, col(CR, h)].astype(F32)
                mixed_ref[rws, col(D_CONV, h)] = (yh * g_ref[:, col(0, h)] * (rg * _sigmoid(rg))).astype(BF16)
            if between[c] is not None:
                between[c]()
        for h in H:
            state[h] = st[h]

        @pl.when(s <= nt)
        def _():
            conv_out(slot, pf(s)).start()
            states_out(slot, pf(s)).start()

        @pl.when(s == nt + 1)
        def _():
            conv_out(1 - slot, pf(s - 1)).wait()
            states_out(1 - slot, pf(s - 1)).wait()
            tot = jnp.sum(lacc[...], axis=1, keepdims=True) * (0.5 / D_MODEL)
            loss_ref[...] = jnp.broadcast_to(tot, (1, 128))

    tile = lambda w: pl.BlockSpec((TM, w), lambda s: (pf(s), 0))
    xtile = lambda w: pl.BlockSpec((TM, w), lambda s: (xt(s), 0))
    any_spec = pl.BlockSpec(memory_space=pl.ANY)
    return pl.pallas_call(
        body, name="f2_mixer_fwd_f3_outproj_loss",
        grid=(nt + 2,),
        in_specs=[tile(N_PROJ_COLS), _resident((8, D_CONV)), _resident((1, D_RET)),
                  pl.BlockSpec((None, 8, HEAD_DIM), lambda s: (pf(s), 0, 0)),
                  _resident((TM, HEAD_DIM)), _resident((TM, HEAD_DIM)), _resident((8, HEAD_DIM)),
                  _resident((RET_HEADS, CHUNK, CHUNK)), _resident((RET_HEADS, CHUNK, HEAD_DIM)),
                  _resident((RET_HEADS, CHUNK, HEAD_DIM)), _resident((RET_HEADS, 8, HEAD_DIM)),
                  xtile(D_MODEL), _resident1((D_MODEL, D_MODEL)), _resident((1, D_MODEL)), xtile(D_MODEL)],
        out_specs=[any_spec, any_spec, xtile(D_MODEL), xtile(D_MODEL),
                   _resident((2, N_CHIPS, 128, D_MODEL)), _resident((1, D_MODEL)), _resident((1, 128))],
        out_shape=[jax.ShapeDtypeStruct((rows, D_CONV), BF16),
                   jax.ShapeDtypeStruct(((nt + 1) * NCH, RET_HEADS, HEAD_DIM, HEAD_DIM), BF16),
                   jax.ShapeDtypeStruct((seq, D_MODEL), F32),
                   jax.ShapeDtypeStruct((seq, D_MODEL), BF16),
                   jax.ShapeDtypeStruct((2, N_CHIPS, 128, D_MODEL), F32),
                   jax.ShapeDtypeStruct((1, D_MODEL), F32),
                   jax.ShapeDtypeStruct((1, 128), F32)],
        scratch_shapes=[pltpu.VMEM((RET_HEADS, HEAD_DIM, HEAD_DIM), F32), pltpu.VMEM((8, D_CONV), F32),
                        pltpu.VMEM((2, TM, D_MODEL), BF16), pltpu.VMEM((2, TM, D_CONV), BF16),
                        pltpu.VMEM((2, NCH, RET_HEADS, HEAD_DIM, HEAD_DIM), BF16),
                        pltpu.VMEM((1, D_MODEL), F32), pltpu.SemaphoreType.DMA((4,))],
        compiler_params=_cparams(("arbitrary",)),
    )(proj, conv_w8, gret, tb["tt"], tb["cr2"], tb["sr2"], tb["sgn"], tb["decay"], tb["xi"], tb["zeta"], tb["cd"],
      x, w_out, fg, target)


def _b2_b1a_call(proj, dmixed, conv_s, states, conv_w8, gret, tb, w_in_g, x, meta_tile, g1, dh2):
    nt, rows = tb["nt"], tb["rows"]
    seq = nt * TM

    def pb(r):
        return jnp.where(r == nt, nt, nt - 1 - r)

    def xprev(r):
        return jnp.clip(nt - r, 0, nt - 1)

    def body(proj_ref, dmx_ref, conv_ref, states_ref, cw_ref, g_ref, tt_ref, cr_ref, sr_ref, sgn_ref, dec_ref,
             dect_ref, xi_ref, zeta_ref, cd_ref, w_ref, x_ref, mt_ref, g1_ref, dh2_ref,
             dproj_hbm, gcw_ref, gg_ref, gx_ref, dmeta_ref, gn_ref,
             gstate, dchalo, dps, out_sems):
        r = pl.program_id(0)
        live = jnp.where(r == nt, 0.0, 1.0)
        slot = lax.rem(r, 2)
        dproj_ref = dps.at[slot]

        def to_hbm(s, tile):
            return pltpu.make_async_copy(dps.at[s], dproj_hbm.at[pl.ds(pl.multiple_of(tile * TM, TM), TM), :],
                                         out_sems.at[s])

        @pl.when(r == 0)
        def _():
            gstate[...] = jnp.zeros_like(gstate)
            dchalo[...] = jnp.zeros_like(dchalo)
            gcw_ref[...] = jnp.zeros_like(gcw_ref)
            gg_ref[...] = jnp.zeros_like(gg_ref)
            gn_ref[...] = jnp.zeros_like(gn_ref)
            dps[...] = jnp.zeros_like(dps)

        @pl.when(r >= 2)
        def _():
            to_hbm(slot, pb(r - 2)).wait()

        dprev = dps.at[1 - slot]
        pieces = []

        def emit_piece():
            j = len(pieces)
            if j < N_CHIPS:
                p = _dot_tb(dprev[:, j * 1024:(j + 1) * 1024], w_ref[j])
                pieces.append(p if j == 0 else pieces[-1] + p)

        cx = proj_ref[:, CX:CX + 512].astype(F32)
        cb = proj_ref[:, CB:CB + 512].astype(F32)
        cc = proj_ref[:, CC:CC + 512].astype(F32)
        cg = proj_ref[:, CG:CG + 512].astype(F32)
        dco = dmx_ref[:, 0:D_CONV].astype(F32) * live
        conv = conv_ref[...].astype(F32)
        sg = _sigmoid(cg)
        sil = cg * sg
        t = dco * conv
        dproj_ref[:, CB:CB + 512] = (t * sil).astype(BF16)
        dproj_ref[:, CG:CG + 512] = (t * cb * (sg * (1.0 + cg * (1.0 - sg)))).astype(BF16)
        dconv = dco * cb * sil
        row = lax.broadcasted_iota(jnp.int32, (TM, D_CONV), 0)
        n0 = dchalo[0:1, :]
        n1 = dchalo[1:2, :]
        dc1 = jnp.where(row == TM - 1, n0, pltpu.roll(dconv, TM - 1, 0))
        dc2 = jnp.where(row == TM - 2, n0, jnp.where(row == TM - 1, n1, pltpu.roll(dconv, TM - 2, 0)))
        dchalo[...] = dconv[0:8, :]
        du = cw_ref[2:3, :] * dconv + cw_ref[1:2, :] * dc1 + cw_ref[0:1, :] * dc2
        u = cc * cx
        gcw_ref[2:3, :] += jnp.sum(u * dconv, axis=0, keepdims=True)
        gcw_ref[1:2, :] += jnp.sum(u * dc1, axis=0, keepdims=True)
        gcw_ref[0:1, :] += jnp.sum(u * dc2, axis=0, keepdims=True)
        dproj_ref[:, CC:CC + 512] = (du * cx).astype(BF16)
        dproj_ref[:, CX:CX + 512] = (du * cc).astype(BF16)
        emit_piece()

        scale = HEAD_DIM ** -0.5
        gs = {h: gstate[h] for h in range(RET_HEADS)}
        gg = {h: jnp.zeros((1, HEAD_DIM), F32) for h in range(RET_HEADS)}
        col = lambda base, h: slice(base + h * HEAD_DIM, base + (h + 1) * HEAD_DIM)
        rw = lambda c: slice(c * CHUNK, (c + 1) * CHUNK)
        rc_t, rs_t = _tile_rotary(tt_ref, cr_ref, sr_ref, sgn_ref)
        for c0 in range(NCH - CHUNK_GROUP, -1, -CHUNK_GROUP):
            cs = range(c0 + CHUNK_GROUP - 1, c0 - 1, -1)
            U = [(c, h) for c in cs for h in range(RET_HEADS)]
            rc = {c: rc_t[rw(c), :] for c in cs}
            rs = {c: rs_t[rw(c), :] for c in cs}
            v = {(c, h): proj_ref[rw(c), col(CV, h)] for c, h in U}
            stb = {(c, h): states_ref[c, h] for c, h in U}
            qf = {(c, h): _rot(proj_ref[rw(c), col(CQ, h)].astype(F32), rc[c], rs[c]) * scale for c, h in U}
            kf = {(c, h): _rot(proj_ref[rw(c), col(CK, h)].astype(F32), rc[c], rs[c]) for c, h in U}
            qb = {u: qf[u].astype(BF16) for u in U}
            kb = {u: kf[u].astype(BF16) for u in U}
            qxb = {(c, h): (qf[c, h] * xi_ref[h]).astype(BF16) for c, h in U}
            kzb = {(c, h): (kf[c, h] * zeta_ref[h]).astype(BF16) for c, h in U}
            ab = {(c, h): (_dot_tb(qb[c, h], kb[c, h]) * dec_ref[h]).astype(BF16) for c, h in U}
            atb = {(c, h): (_dot_tb(kb[c, h], qb[c, h]) * dect_ref[h]).astype(BF16) for c, h in U}
            o = {u: _dot(ab[u], v[u]) + _dot(qxb[u], stb[u]) for u in U}
            emit_piece()
            dob = {}
            for c, h in U:
                mu = jnp.mean(o[c, h], axis=-1, keepdims=True)
                d = o[c, h] - mu
                var = jnp.mean(d * d, axis=-1, keepdims=True)
                rstd = lax.rsqrt(var + EPS)
                yh = d * rstd
                g = g_ref[:, col(0, h)]
                rg = proj_ref[rw(c), col(CR, h)].astype(F32)
                dro = dmx_ref[rw(c), col(D_CONV, h)].astype(F32) * live
                sg = _sigmoid(rg)
                dproj_ref[rw(c), col(CR, h)] = (dro * (yh * g) * (sg * (1.0 + rg * (1.0 - sg)))).astype(BF16)
                dret = dro * (rg * sg)
                gg[h] = gg[h] + jnp.sum(dret * yh, axis=0, keepdims=True)
                dyh = dret * g
                do = rstd * (dyh - jnp.mean(dyh, axis=-1, keepdims=True)
                             - yh * jnp.mean(dyh * yh, axis=-1, keepdims=True))
                dob[c, h] = do.astype(BF16)
            dv1 = {u: _dot(atb[u], dob[u]) for u in U}
            ds = {(c, h): (_dot_tb(dob[c, h], v[c, h]) * dec_ref[h]).astype(BF16) for c, h in U}
            dst = {(c, h): (_dot_tb(v[c, h], dob[c, h]) * dect_ref[h]).astype(BF16) for c, h in U}
            gup = {u: _dot_ta(qxb[u], dob[u]) for u in U}
            dq = {(c, h): _dot(ds[c, h], kb[c, h]) + _dot_tb(dob[c, h], stb[c, h]) * xi_ref[h] for c, h in U}
            dk1 = {u: _dot(dst[u], qb[u]) for u in U}
            emit_piece()
            for c, h in U:
                gsb = gs[h].astype(BF16)
                dv = dv1[c, h] + _dot(kzb[c, h], gsb)
                dk = dk1[c, h] + _dot_tb(v[c, h], gsb) * zeta_ref[h]
                gs[h] = cd_ref[h, 0:1, :] * gs[h] + gup[c, h]
                dproj_ref[rw(c), col(CQ, h)] = (_rot_t(dq[c, h], rc[c], rs[c]) * scale).astype(BF16)
                dproj_ref[rw(c), col(CK, h)] = _rot_t(dk, rc[c], rs[c]).astype(BF16)
                dproj_ref[rw(c), col(CV, h)] = dv.astype(BF16)
        for h in range(RET_HEADS):
            gstate[h] = gs[h]
            gg_ref[:, col(0, h)] += gg[h]

        while len(pieces) < N_CHIPS:
            emit_piece()

        def norm_bwd(dhn, hx):
            ms = jnp.mean(hx * hx, axis=-1, keepdims=True)
            rstd1 = lax.rsqrt(ms + EPS)
            xh = hx * rstd1
            gn_ref[...] += jnp.sum(dhn * xh, axis=0, keepdims=True)
            dxh = dhn * g1_ref[...]
            return rstd1 * (dxh - xh * jnp.mean(dxh * xh, axis=-1, keepdims=True))

        gx_ref[...] = norm_bwd(pieces[-1], x_ref[...]) + dh2_ref[...]

        @pl.when(r < nt)
        def _():
            to_hbm(slot, pb(r)).start()

        @pl.when(r == nt)
        def _():
            to_hbm(slot, pb(r)).start()
            mrows = slice(TM - N_META, TM)
            d16 = dproj_ref[mrows, :]
            dhn16 = _dot_tb(d16[:, 0:1024], w_ref[0])
            for j in range(1, N_CHIPS):
                dhn16 += _dot_tb(d16[:, j * 1024:(j + 1) * 1024], w_ref[j])
            dmeta_ref[...] = norm_bwd(dhn16, mt_ref[mrows, :])
            to_hbm(1 - slot, pb(r - 1)).wait()
            to_hbm(slot, pb(r)).wait()

    tile = lambda w: pl.BlockSpec((TM, w), lambda r: (pb(r), 0))
    xtile = pl.BlockSpec((TM, D_MODEL), lambda r: (xprev(r), 0))
    return pl.pallas_call(
        body, name="b2_mixer_bwd_b1a_inproj_bwd_x",
        grid=(nt + 1,),
        in_specs=[tile(N_PROJ_COLS),
                  pl.BlockSpec((TM, D_MODEL), lambda r: (jnp.minimum(pb(r), nt - 1), 0)),
                  tile(D_CONV),
                  pl.BlockSpec((NCH, RET_HEADS, HEAD_DIM, HEAD_DIM), lambda r: (pb(r), 0, 0, 0)),
                  _resident((8, D_CONV)), _resident((1, D_RET)),
                  pl.BlockSpec((None, 8, HEAD_DIM), lambda r: (pb(r), 0, 0)),
                  _resident((TM, HEAD_DIM)), _resident((TM, HEAD_DIM)), _resident((8, HEAD_DIM)),
                  _resident((RET_HEADS, CHUNK, CHUNK)), _resident((RET_HEADS, CHUNK, CHUNK)),
                  _resident((RET_HEADS, CHUNK, HEAD_DIM)),
                  _resident((RET_HEADS, CHUNK, HEAD_DIM)), _resident((RET_HEADS, 8, HEAD_DIM)),
                  _resident1((N_CHIPS, D_MODEL, 1024)), xtile, _resident1((TM, D_MODEL)), _resident((1, D_MODEL)),
                  xtile],
        out_specs=[pl.BlockSpec(memory_space=pl.ANY), _resident((8, D_CONV)), _resident((1, D_RET)),
                   xtile, _resident((N_META, D_MODEL)), _resident((1, D_MODEL))],
        out_shape=[jax.ShapeDtypeStruct((rows, N_PROJ_COLS), BF16),
                   jax.ShapeDtypeStruct((8, D_CONV), F32),
                   jax.ShapeDtypeStruct((1, D_RET), F32),
                   jax.ShapeDtypeStruct((seq, D_MODEL), F32),
                   jax.ShapeDtypeStruct((N_META, D_MODEL), F32),
                   jax.ShapeDtypeStruct((1, D_MODEL), F32)],
        scratch_shapes=[pltpu.VMEM((RET_HEADS, HEAD_DIM, HEAD_DIM), F32), pltpu.VMEM((8, D_CONV), F32),
                        pltpu.VMEM((2, TM, N_PROJ_COLS), BF16), pltpu.SemaphoreType.DMA((2,))],
        compiler_params=_cparams(("arbitrary",), vmem=VMEM_LIMIT_MAX),
    )(proj, dmixed, conv_s, states, conv_w8, gret, tb["tt"], tb["cr2"], tb["sr2"], tb["sgn"], tb["decay"],
      tb["decay_t"], tb["xi"], tb["zeta"], tb["cd"], w_in_g, x, meta_tile, g1, dh2)


REL = (2, 1, 3)
SMALL_ROWS = 24
HALF_STEP = 4


def _rcopy(src, dst, send_sems, recv_sems, k, to):
    return pltpu.make_async_remote_copy(src_ref=src, dst_ref=dst, send_sem=send_sems.at[k],
                                        recv_sem=recv_sems.at[k], device_id=to, device_id_type=MESH_ID)


def _b1b_reduce_call(order, hnt, dproj, gwo, pack, nt):
    nk = nt + 1
    last = nk - 1
    any_spec = pl.BlockSpec(memory_space=pl.ANY)

    def body(order_ref, a_ref, b_ref, gwo_hbm, pack_hbm, gwin_hbm, gwout_hbm, tot_hbm,
             acc, sb, abuf, pb, bbuf, fin, go, ao, pbo, bo, fino, slots, totv, send_sems, recv_sems, loc_sems):
        jj, k = pl.program_id(0), pl.program_id(1)
        x, y, c = lax.axis_index("x"), lax.axis_index("y"), lax.axis_index("c")
        me, myid, sib = 2 * x + y, 4 * x + 2 * y + c, (x, y, 1 - c)
        rc = functools.partial(_rcopy, send_sems=send_sems, recv_sems=recv_sems)
        peers = [((1 - x) if r & 2 else x, (1 - y) if r & 1 else y, c) for r in REL]
        kids = [jnp.bitwise_xor(me, r) for r in REL]

        def dev_peer(r):
            return ((1 - x) if r & 4 else x, (1 - y) if r & 2 else y, (1 - c) if r & 1 else c)

        own_go = pltpu.make_async_copy(gwo_hbm.at[c], go, loc_sems.at[0])
        own_pack = pltpu.make_async_copy(pack_hbm, slots.at[0], loc_sems.at[1])
        wo_half = rc(gwo_hbm.at[1 - c], ao, k=8, to=sib)
        wo_part = [rc(pbo.at[kids[p]], bo.at[p], k=9 + p, to=peers[p]) for p in range(3)]
        sm = [rc(pack_hbm, slots.at[r], k=12 + r, to=dev_peer(r)) for r in range(1, N_DEV)]
        half = [rc(sb.at[j % 2, 1 - c], abuf.at[j], k=j, to=sib) for j in range(N_CHIPS)]
        part = [rc(pb.at[p], bbuf.at[p], k=4 + p, to=peers[p]) for p in range(3)]

        @pl.when(jnp.logical_and(jj == 0, k == 0))
        def _():
            own_go.start()
            own_pack.start()
            wo_half.start()
            for cp in sm:
                cp.start()

        @pl.when(k == 0)
        def _():
            acc[...] = jnp.zeros_like(acc)

        acc[0] += _dot(a_ref[0:512, :], b_ref[...])
        acc[1] += _dot(a_ref[512:1024, :], b_ref[...])

        @pl.when(k == HALF_STEP)
        def _():
            @pl.when(jj == 0)
            def _():
                own_go.wait()
                wo_half.wait_recv()
                for j in range(N_CHIPS):
                    go[j] = go[j] + ao[j]
                pbo[...] = go[...].astype(BF16)
                for cp in wo_part:
                    cp.start()

            for p in range(3):
                @pl.when(jj == p + 1)
                def _(p=p):
                    half[p].wait_recv()
                    half[p].wait_send()
                    pb[p] = (sb[p % 2, c] + abuf[p]).astype(BF16)
                    part[p].start()

        @pl.when(k == last)
        def _():
            for j in range(N_CHIPS):
                @pl.when(jj == j)
                def _(j=j):
                    sb[j % 2] = acc[...]
                    half[j].start()

        @pl.when(jnp.logical_and(jj == N_CHIPS - 1, k == last))
        def _():
            half[3].wait_recv()
            own = sb[1, c] + abuf[3]
            for cp in part:
                cp.wait_recv()
            fin[c] = ((own + bbuf[0].astype(F32)) + bbuf[1].astype(F32)) + bbuf[2].astype(F32)
            done = rc(fin.at[c], fin.at[c], k=7, to=sib)
            done.start()
            for cp in wo_part:
                cp.wait_recv()
            fino[c] = ((go[me] + bo[0].astype(F32)) + bo[1].astype(F32)) + bo[2].astype(F32)
            done_o = rc(fino.at[c], fino.at[c], k=12, to=sib)
            done_o.start()
            own_pack.wait()
            for cp in sm:
                cp.wait_recv()
            tot = slots[myid]
            for a in range(1, N_DEV):
                tot = tot + slots[jnp.bitwise_xor(myid, a)]
            totv[...] = tot
            out_t = pltpu.make_async_copy(totv, tot_hbm, loc_sems.at[1])
            out_t.start()
            rc(fin.at[1 - c], fin.at[1 - c], k=7, to=sib).wait_recv()
            out_w = pltpu.make_async_copy(fin, gwin_hbm, loc_sems.at[0])
            out_w.start()
            rc(fino.at[1 - c], fino.at[1 - c], k=12, to=sib).wait_recv()
            out_o = pltpu.make_async_copy(fino, gwout_hbm, loc_sems.at[2])
            out_o.start()
            for cp in [half[3]] + part + [done, wo_half] + wo_part + [done_o] + sm:
                cp.wait_send()
            out_t.wait()
            out_w.wait()
            out_o.wait()

    grid_spec = pltpu.PrefetchScalarGridSpec(
        num_scalar_prefetch=1,
        grid=(N_CHIPS, nk),
        in_specs=[pl.BlockSpec((D_MODEL, TM), lambda j, k, o: (0, k)),
                  pl.BlockSpec((TM, 1024), lambda j, k, o: (k, o[j])),
                  any_spec, any_spec],
        out_specs=[any_spec, any_spec, any_spec],
        scratch_shapes=[
            pltpu.VMEM((2, 512, 1024), F32),
            pltpu.VMEM((2, 2, 512, 1024), F32),
            pltpu.VMEM((N_CHIPS, 512, 1024), F32),
            pltpu.VMEM((3, 512, 1024), BF16),
            pltpu.VMEM((3, 512, 1024), BF16),
            pltpu.VMEM((2, 512, 1024), F32),
            pltpu.VMEM((N_CHIPS, 128, D_MODEL), F32),
            pltpu.VMEM((N_CHIPS, 128, D_MODEL), F32),
            pltpu.VMEM((N_CHIPS, 128, D_MODEL), BF16),
            pltpu.VMEM((3, 128, D_MODEL), BF16),
            pltpu.VMEM((2, 128, D_MODEL), F32),
            pltpu.VMEM((N_DEV, SMALL_ROWS, D_MODEL), F32),
            pltpu.VMEM((SMALL_ROWS, D_MODEL), F32),
            pltpu.SemaphoreType.DMA((20,)), pltpu.SemaphoreType.DMA((20,)), pltpu.SemaphoreType.DMA((3,))])
    return pl.pallas_call(
        body, name="b1b_inproj_bwd_w_reduce",
        grid_spec=grid_spec,
        out_shape=[jax.ShapeDtypeStruct((2, 512, 1024), F32),
                   jax.ShapeDtypeStruct((2, 128, D_MODEL), F32),
                   jax.ShapeDtypeStruct((SMALL_ROWS, D_MODEL), F32)],
        compiler_params=_cparams(("arbitrary", "arbitrary")),
    )(order, hnt, dproj, gwo, pack)


def _local_step(me, x, target, g1, gret, fg, win_sh, wout_sh, small_sh):
    seq = x.shape[0]
    tb = _tables(seq)
    nt = tb["nt"]
    g1r, gretr, fgr = g1.reshape(1, -1), gret.reshape(1, -1), fg.reshape(1, -1)
    order = jnp.stack([me, me ^ REL[0], me ^ REL[1], me ^ REL[2]]).astype(jnp.int32)

    proj, hnt, w_in_g, w_out_g, meta_tile, conv_w8 = _f1_gather_call(order, x, g1r, win_sh, wout_sh, small_sh, nt)
    w_out = w_out_g.reshape(D_MODEL, D_MODEL)
    conv_s, states, dh2, dmixed, g_wout, g_fg, loss = _f2_f3_call(proj, conv_w8, gretr, tb, x, w_out, fgr, target)
    dproj, g_cw8, g_gret, grad_x, g_meta, g_g1 = _b2_b1a_call(proj, dmixed, conv_s, states, conv_w8, gretr, tb,
                                                              w_in_g, x, meta_tile, g1r, dh2)
    return loss, grad_x, dict(w_out=g_wout, meta=g_meta, conv_w=g_cw8[0:3], norm1_g=g_g1,
                              ret_norm_g=g_gret, final_g=g_fg), hnt, dproj


def _adamw_update(w_ref, g_ref, m_ref, v_ref, d_ref, nm_ref, nv_ref):
    gg = g_ref[...]
    nm = ADAM_B1 * m_ref[...] + (1.0 - ADAM_B1) * gg
    nv = ADAM_B2 * v_ref[...] + (1.0 - ADAM_B2) * (gg * gg)
    m_hat = nm / (1.0 - ADAM_B1 ** ADAM_STEP)
    v_hat = nv / (1.0 - ADAM_B2 ** ADAM_STEP)
    d_ref[...] = -ADAM_LR * (m_hat / (jnp.sqrt(v_hat) + ADAM_EPS) + ADAM_WD * w_ref[...])
    nm_ref[...] = nm
    nv_ref[...] = nv


def _adamw_small_call(ws, gs, ms, vs):
    n = len(ws)

    def body(*refs):
        ins, outs = refs[:4 * n], refs[4 * n:]
        for i in range(n):
            _adamw_update(ins[i], ins[n + i], ins[2 * n + i], ins[3 * n + i],
                          outs[i], outs[n + i], outs[2 * n + i])

    shapes = [jax.ShapeDtypeStruct(w.shape, F32) for w in ws]
    outs = pl.pallas_call(body, name="adamw_small", out_shape=shapes * 3,
                          compiler_params=_cparams())(*ws, *gs, *ms, *vs)
    return outs[:n], outs[n:2 * n], outs[2 * n:]


def _adamw_call(w, g, m, v, name):
    shape = w.shape
    w2, g2, m2, v2 = (a.reshape(-1, shape[-1]) for a in (w, g, m, v))
    rows, cols = w2.shape
    br = 256 if rows % 256 == 0 else rows
    body = functools.partial(_adamw_update)
    spec = pl.BlockSpec((br, cols), lambda i: (i, 0))
    outs = pl.pallas_call(
        body, name=name, grid=(rows // br,),
        in_specs=[spec] * 4, out_specs=[spec] * 3,
        out_shape=[jax.ShapeDtypeStruct((rows, cols), F32)] * 3,
        compiler_params=_cparams(("arbitrary",)),
    )(w2, g2, m2, v2)
    return tuple(o.reshape(shape) for o in outs)


def _pad_to(a, rows, cols):
    return jnp.pad(a, ((0, rows - a.shape[0]), (0, cols - a.shape[1])))


def kernel(x, meta, norm1_g, w_in, conv_w, ret_norm_g, w_out, final_g, loss_target, m_meta, m_norm1_g, m_w_in, m_conv_w, m_ret_norm_g, m_w_out, m_final_g, v_meta, v_norm1_g, v_w_in, v_conv_w, v_ret_norm_g, v_w_out, v_final_g):
    me = 2 * lax.axis_index("x") + lax.axis_index("y")

    small_sh = jnp.concatenate([meta, _pad_to(conv_w, 8, 256)], axis=0)
    loss, grad_x, g, hnt, dproj = _local_step(me, x[0], loss_target[0], norm1_g, ret_norm_g, final_g,
                                              w_in.astype(BF16), w_out.astype(BF16), small_sh)

    vec = jnp.concatenate([g["norm1_g"], g["final_g"], _pad_to(g["ret_norm_g"], 1, D_MODEL),
                           _pad_to(g["conv_w"], 3, D_MODEL), _pad_to(loss, 2, D_MODEL)], axis=0)
    pack = jnp.concatenate([g["meta"], vec], axis=0)
    order = jnp.stack([me ^ REL[0], me ^ REL[1], me ^ REL[2], me]).astype(jnp.int32)
    g_win, g_wout, tot = _b1b_reduce_call(order, hnt, dproj, g["w_out"], pack, x.shape[1] // TM)
    g_win, g_wout = g_win.reshape(D_MODEL, 1024), g_wout.reshape(256, D_MODEL)
    g_meta = lax.dynamic_slice(tot, (0, me * 256), (N_META, 256))
    g_conv = lax.dynamic_slice(tot, (N_META + 3, me * 128), (3, 128))
    g_n1, g_fg, g_rn = tot[N_META], tot[N_META + 1], tot[N_META + 2, 0:D_RET]

    loss_tot = tot[N_META + 6, 0]

    grads = [g_meta, g_n1, g_win, g_conv, g_rn, g_wout, g_fg]
    ws = [meta, norm1_g, w_in, conv_w, ret_norm_g, w_out, final_g]
    ms = [m_meta, m_norm1_g, m_w_in, m_conv_w, m_ret_norm_g, m_w_out, m_final_g]
    vs = [v_meta, v_norm1_g, v_w_in, v_conv_w, v_ret_norm_g, v_w_out, v_final_g]
    names = ["meta", "norm1_g", "w_in", "conv_w", "ret_norm_g", "w_out", "final_g"]
    as2d = lambda a: a.reshape(1, -1) if a.ndim == 1 else a
    big = [i for i, n_ in enumerate(names) if n_ in ("w_in", "w_out")]
    small = [i for i in range(len(names)) if i not in big]
    deltas, new_ms, new_vs = [None] * 7, [None] * 7, [None] * 7
    for i in big:
        deltas[i], new_ms[i], new_vs[i] = _adamw_call(ws[i], grads[i], ms[i], vs[i], "adamw_" + names[i])
    sd, sm_, sv = _adamw_small_call(*[[as2d(t[i]) for i in small] for t in (ws, grads, ms, vs)])
    for j, i in enumerate(small):
        deltas[i], new_ms[i], new_vs[i] = (o[j].reshape(ws[i].shape) for o in (sd, sm_, sv))
    return (loss_tot, grad_x[None], *grads, *deltas, *new_ms, *new_vs)
```

```python
import functools

import jax
import jax.numpy as jnp
from jax import lax
from jax.experimental import pallas as pl
from jax.experimental.pallas import tpu as pltpu

F32 = jnp.float32
BF16 = jnp.bfloat16

D_MODEL = 1024
N_META = 16
D_CONV = 512
D_RET = 512
RET_HEADS = 4
HEAD_DIM = 128
CHUNK = 128
N_PROJ_COLS = 4096
ROPE_BASE = 10000.0
EPS = 1e-6
N_CHIPS = 4
N_DEV = 8

ADAM_LR = 0.001
ADAM_B1 = 0.9
ADAM_B2 = 0.999
ADAM_EPS = 1e-08
ADAM_WD = 0.01
ADAM_STEP = 10

TM = 512
NCH = TM // CHUNK
CHUNK_GROUP = 2
VMEM_LIMIT = 56 * 1024 * 1024
VMEM_LIMIT_MAX = 63 * 1024 * 1024

CX, CB, CC, CG, CQ, CK, CV, CR = (i * 512 for i in range(8))

MESH_ID = pl.DeviceIdType.MESH


def _cparams(sem=None, vmem=VMEM_LIMIT, **kw):
    return pltpu.CompilerParams(dimension_semantics=sem, vmem_limit_bytes=vmem, **kw)


def _sigmoid(x):
    return 1.0 / (1.0 + jnp.exp(-x))


def _dot(a, b):
    return jnp.dot(a, b, preferred_element_type=F32)


def _dot_tb(a, b):
    return lax.dot_general(a, b, (((1,), (1,)), ((), ())), preferred_element_type=F32)


def _dot_ta(a, b):
    return lax.dot_general(a, b, (((0,), (0,)), ((), ())), preferred_element_type=F32)


def _resident(shape):
    nd = len(shape)
    return pl.BlockSpec(shape, lambda *_: (0,) * nd)


def _resident1(shape):
    nd = len(shape)
    return pl.BlockSpec(shape, lambda *_: (0,) * nd, pipeline_mode=pl.Buffered(1))


def _tables(seq):
    nt = seq // TM
    rows = seq + TM
    half = HEAD_DIM // 2
    freqs = 1.0 / (ROPE_BASE ** (jnp.arange(half, dtype=F32) / half))
    tile_start = jnp.concatenate([jnp.arange(nt, dtype=F32), -jnp.ones((1,), F32)]) * TM
    ang_t = tile_start[:, None] * freqs[None, :]
    ang_r = (jnp.arange(TM, dtype=F32) + N_META)[:, None] * freqs[None, :]
    dup = lambda a: jnp.concatenate([a, a], axis=-1)
    tt = jnp.stack([dup(jnp.cos(ang_t)), dup(jnp.sin(ang_t))], axis=1)
    tt = jnp.pad(tt, ((0, 0), (0, 6), (0, 0)))
    cr2, sr2 = dup(jnp.cos(ang_r)), dup(jnp.sin(ang_r))
    sgn = jnp.concatenate([-jnp.ones((8, half), F32), jnp.ones((8, half), F32)], axis=-1)
    log_g = jnp.log(1.0 - 2.0 ** (-5.0 - jnp.arange(RET_HEADS, dtype=F32)))
    idx = jnp.arange(CHUNK, dtype=F32)
    diff = idx[:, None] - idx[None, :]
    decay = jnp.where(diff[None] >= 0, jnp.exp(diff[None] * log_g[:, None, None]), 0.0)
    zeta = jnp.exp((CHUNK - 1 - idx)[None, :] * log_g[:, None])
    xi = jnp.exp((idx + 1.0)[None, :] * log_g[:, None])
    cd = jnp.exp(CHUNK * log_g)
    zeta_b = jnp.broadcast_to(zeta[:, :, None], (RET_HEADS, CHUNK, HEAD_DIM))
    xi_b = jnp.broadcast_to(xi[:, :, None], (RET_HEADS, CHUNK, HEAD_DIM))
    cd_b = jnp.broadcast_to(cd[:, None, None], (RET_HEADS, 8, HEAD_DIM))
    return dict(nt=nt, rows=rows, tt=tt, cr2=cr2, sr2=sr2, sgn=sgn, decay=decay,
                decay_t=jnp.swapaxes(decay, 1, 2), zeta=zeta_b, xi=xi_b, cd=cd_b)


def _tile_rotary(tt_ref, cr_ref, sr_ref, sgn_ref):
    ct, st = tt_ref[0:1, :], tt_ref[1:2, :]
    cr, sr = cr_ref[...], sr_ref[...]
    return ct * cr - st * sr, (st * cr + ct * sr) * sgn_ref[0:1, :]


def _rot(t, rc, rs):
    return t * rc + pltpu.roll(t, HEAD_DIM // 2, 1) * rs


def _rot_t(dt, rc, rs):
    return dt * rc + pltpu.roll(dt * rs, HEAD_DIM // 2, 1)


def _f1_gather_call(order, x, g1, win_sh, wout_sh, small_sh, nt):
    nk = nt + 1
    rows = nk * TM
    any_spec = pl.BlockSpec(memory_space=pl.ANY)

    def body(order_ref, x_ref, g_ref, win_hbm, wout_hbm, sm_hbm,
             proj_ref, hnt_ref, wg_hbm, wog_hbm, mt_hbm, cw_hbm,
             wg, wog, smg, mt, cw, hbs, send_sems, recv_sems, loc_sems):
        jj, k = pl.program_id(0), pl.program_id(1)
        x, y, c = lax.axis_index("x"), lax.axis_index("y"), lax.axis_index("c")
        me, sib = 2 * x + y, (x, y, 1 - c)
        rc = functools.partial(_rcopy, send_sems=send_sems, recv_sems=recv_sems)
        peers = [((1 - x) if r & 2 else x, (1 - y) if r & 1 else y, c) for r in REL]
        kids = [jnp.bitwise_xor(me, r) for r in REL]
        hw, ho = pl.ds(c * 512, 512), pl.ds(c * 128, 128)
        hw2, ho2 = pl.ds((1 - c) * 512, 512), pl.ds((1 - c) * 128, 128)
        at = lambda j_, k_: jnp.logical_and(jj == j_, k == k_)

        sm_cp = [rc(sm_hbm, smg.at[me], k=p, to=peers[p]) for p in range(3)]
        win_cp = [rc(win_hbm.at[hw], wg.at[me, hw], k=3 + p, to=peers[p]) for p in range(3)]
        wout_cp = [rc(wout_hbm.at[ho], wog.at[me, ho], k=6 + p, to=peers[p]) for p in range(3)]
        sm_in = [rc(sm_hbm, smg.at[kids[p]], k=p, to=sib) for p in range(3)]
        win_in = [rc(win_hbm.at[hw], wg.at[kids[p], hw], k=3 + p, to=sib) for p in range(3)]
        wout_in = [rc(wout_hbm.at[ho], wog.at[kids[p], ho], k=6 + p, to=sib) for p in range(3)]
        win_fw = [rc(wg.at[kids[p], hw], wg.at[kids[p], hw], k=9 + p, to=sib) for p in range(3)]
        wout_fw = [rc(wog.at[kids[p], ho], wog.at[kids[p], ho], k=12 + p, to=sib) for p in range(3)]
        win_fw_in = [rc(wg.at[kids[p], hw2], wg.at[kids[p], hw2], k=9 + p, to=sib) for p in range(3)]
        wout_fw_in = [rc(wog.at[kids[p], ho2], wog.at[kids[p], ho2], k=12 + p, to=sib) for p in range(3)]
        own_w = pltpu.make_async_copy(win_hbm, wg.at[me], loc_sems.at[0])
        own_o = pltpu.make_async_copy(wout_hbm, wog.at[me], loc_sems.at[1])
        own_s = pltpu.make_async_copy(sm_hbm, smg.at[me], loc_sems.at[2])
        out_wg = pltpu.make_async_copy(wg, wg_hbm, loc_sems.at[3])
        out_wog = pltpu.make_async_copy(wog, wog_hbm, loc_sems.at[4])
        out_mt = pltpu.make_async_copy(mt, mt_hbm, loc_sems.at[5])
        out_cw = pltpu.make_async_copy(cw, cw_hbm, loc_sems.at[6])

        def pass_on(p):
            win_in[p].wait_recv()
            win_fw[p].start()

        @pl.when(k <= 1)
        def _():
            @pl.when(at(0, 0))
            def _():
                own_w.start()
                own_s.start()
                own_o.start()
                for cp in sm_cp + win_cp + wout_cp:
                    cp.start()
                own_w.wait()

            for p in range(3):
                @pl.when(at(p + 1, 0))
                def _(p=p):
                    win_fw_in[p].wait_recv()

            @pl.when(at(1, 1))
            def _():
                pass_on(1)

            @pl.when(at(3, 0))
            def _():
                out_wg.start()

            @pl.when(at(3, 1))
            def _():
                for p in range(3):
                    wout_in[p].wait_recv()
                    wout_fw[p].start()

        @pl.when(jnp.logical_and(jj == 0, k >= nk - 2))
        def _():
            @pl.when(k == nk - 2)
            def _():
                own_s.wait()
                for cp in sm_in:
                    cp.wait_recv()
                mt[...] = jnp.zeros_like(mt)
                cw[...] = jnp.zeros_like(cw)
                for j in range(N_CHIPS):
                    mt[TM - N_META:TM, j * 256:(j + 1) * 256] = smg[j, 0:N_META, :]
                    cw[0:3, j * 128:(j + 1) * 128] = smg[j, N_META:N_META + 3, 0:128]
                out_mt.start()
                out_cw.start()

            @pl.when(k == nk - 1)
            def _():
                pass_on(0)

        @pl.when(at(2, nk // 2))
        def _():
            pass_on(2)

        tile_rows = pl.ds(pl.multiple_of(k * TM, TM), TM)

        @pl.when(jj == 0)
        def _():
            h = jnp.where(k == nt, mt[...], x_ref[...])
            ms = jnp.mean(h * h, axis=-1, keepdims=True)
            hn = (h * lax.rsqrt(ms + EPS)) * g_ref[...]
            hb = hn.astype(BF16)
            hbs[tile_rows, :] = hb
            proj_ref[...] = _dot(hb, wg[order_ref[0]]).astype(BF16)
            hnt_ref[...] = hn.T.astype(BF16)

        @pl.when(jj > 0)
        def _():
            proj_ref[...] = _dot(hbs[tile_rows, :], wg[order_ref[jj]]).astype(BF16)

        @pl.when(at(3, nk - 1))
        def _():
            own_o.wait()
            for cp in wout_fw_in:
                cp.wait_recv()
            out_wog.start()
            for cp in sm_cp + win_cp + wout_cp + win_fw + wout_fw:
                cp.wait_send()
            for cp in (out_wg, out_wog, out_mt, out_cw):
                cp.wait()

    grid_spec = pltpu.PrefetchScalarGridSpec(
        num_scalar_prefetch=1,
        grid=(N_CHIPS, nk),
        in_specs=[pl.BlockSpec((TM, D_MODEL), lambda j, k, o: (jnp.where(j == 0, jnp.minimum(k, nt - 1), nt - 1), 0)),
                  pl.BlockSpec((1, D_MODEL), lambda j, k, o: (0, 0)),
                  any_spec, any_spec, any_spec],
        out_specs=[pl.BlockSpec((TM, 1024), lambda j, k, o: (k, o[j])),
                   pl.BlockSpec((D_MODEL, TM), lambda j, k, o: (0, jnp.where(j == 0, k, nk - 1))),
                   any_spec, any_spec, any_spec, any_spec],
        scratch_shapes=[
            pltpu.VMEM((N_CHIPS, D_MODEL, 1024), BF16),
            pltpu.VMEM((N_CHIPS, 256, D_MODEL), BF16),
            pltpu.VMEM((N_CHIPS, SMALL_ROWS, 256), F32),
            pltpu.VMEM((TM, D_MODEL), F32),
            pltpu.VMEM((8, D_CONV), F32),
            pltpu.VMEM((rows, D_MODEL), BF16),
            pltpu.SemaphoreType.DMA((15,)), pltpu.SemaphoreType.DMA((15,)), pltpu.SemaphoreType.DMA((7,))])
    return pl.pallas_call(
        body, name="f1_norm_inproj_gather",
        grid_spec=grid_spec,
        out_shape=[jax.ShapeDtypeStruct((rows, N_PROJ_COLS), BF16),
                   jax.ShapeDtypeStruct((D_MODEL, rows), BF16),
                   jax.ShapeDtypeStruct((N_CHIPS, D_MODEL, 1024), BF16),
                   jax.ShapeDtypeStruct((N_CHIPS, 256, D_MODEL), BF16),
                   jax.ShapeDtypeStruct((TM, D_MODEL), F32),
                   jax.ShapeDtypeStruct((8, D_CONV), F32)],
        compiler_params=_cparams(("arbitrary", "arbitrary")),
    )(order, x, g1, win_sh, wout_sh, small_sh)


def _f2_f3_call(proj, conv_w8, gret, tb, x, w_out, fg, target):
    nt, rows = tb["nt"], tb["rows"]
    seq = nt * TM

    def pf(s):
        return jnp.where(s == 0, nt, jnp.minimum(s - 1, nt - 1))

    def xt(s):
        return jnp.clip(s - 2, 0, nt - 1)

    def body(proj_ref, cw_ref, g_ref, tt_ref, cr_ref, sr_ref, sgn_ref, dec_ref, xi_ref, zeta_ref, cd_ref,
             x_ref, w_ref, fg_ref, t_ref,
             conv_hbm, states_hbm, dh2_ref, dmx_ref, gwo_ref, gfg_ref, loss_ref,
             state, uhalo, mxs, convs, sts, lacc, out_sems):
        s = pl.program_id(0)
        slot = lax.rem(s, 2)
        mixed_ref = mxs.at[slot]
        conv_ref = convs.at[slot]
        states_ref = sts.at[slot]

        def conv_out(sl, tile):
            return pltpu.make_async_copy(convs.at[sl], conv_hbm.at[pl.ds(pl.multiple_of(tile * TM, TM), TM), :],
                                         out_sems.at[sl])

        def states_out(sl, tile):
            return pltpu.make_async_copy(sts.at[sl], states_hbm.at[pl.ds(pl.multiple_of(tile * NCH, NCH), NCH)],
                                         out_sems.at[2 + sl])

        @pl.when(s == 0)
        def _():
            state[...] = jnp.zeros_like(state)
            uhalo[...] = jnp.zeros_like(uhalo)
            mxs[...] = jnp.zeros_like(mxs)
            gwo_ref[...] = jnp.zeros_like(gwo_ref)
            gfg_ref[...] = jnp.zeros_like(gfg_ref)
            lacc[...] = jnp.zeros_like(lacc)

        @pl.when(s >= 2)
        def _():
            conv_out(slot, pf(s - 2)).wait()
            states_out(slot, pf(s - 2)).wait()

        valid = jnp.where(s >= 2, 1.0, 0.0)
        mx_prev = mxs.at[1 - slot]
        f3 = {}

        def f3_fwd():
            f3["h2"] = x_ref[...] + _dot(mx_prev[...], w_ref[...])

        def f3_loss():
            h2 = f3.pop("h2")
            ms = jnp.mean(h2 * h2, axis=-1, keepdims=True)
            rstd = lax.rsqrt(ms + EPS)
            yh = h2 * rstd
            g = fg_ref[...]
            e = (yh * g - t_ref[...]) * valid
            lacc[...] += jnp.sum(e * e, axis=0, keepdims=True)
            dy = e * (1.0 / D_MODEL)
            gfg_ref[...] += jnp.sum(dy * yh, axis=0, keepdims=True)
            dyh = dy * g
            dh2 = rstd * (dyh - yh * jnp.mean(dyh * yh, axis=-1, keepdims=True))
            dh2_ref[...] = dh2
            f3["db"] = dh2.astype(BF16)

        def f3_dmx():
            dmx_ref[...] = _dot_tb(f3["db"], w_ref[...]).astype(BF16)

        def f3_gw():
            gw = _dot_ta(mx_prev[...], f3["db"])
            for j in range(N_CHIPS):
                for hf in range(2):
                    r0 = j * 256 + hf * 128
                    gwo_ref[hf, j] += gw[r0:r0 + 128, :]

        cx = proj_ref[:, CX:CX + 512].astype(F32)
        cc = proj_ref[:, CC:CC + 512].astype(F32)
        u = cc * cx
        row = lax.broadcasted_iota(jnp.int32, (TM, D_CONV), 0)
        h7 = uhalo[7:8, :]
        h6 = uhalo[6:7, :]
        u1 = jnp.where(row == 0, h7, pltpu.roll(u, 1, 0))
        u2 = jnp.where(row == 0, h6, jnp.where(row == 1, h7, pltpu.roll(u, 2, 0)))
        conv = cw_ref[2:3, :] * u + cw_ref[1:2, :] * u1 + cw_ref[0:1, :] * u2
        uhalo[...] = u[TM - 8:TM, :]
        cb = proj_ref[:, CB:CB + 512].astype(F32)
        cg = proj_ref[:, CG:CG + 512].astype(F32)
        mixed_ref[:, 0:D_CONV] = (cb * conv * (cg * _sigmoid(cg))).astype(BF16)
        conv_ref[...] = conv.astype(BF16)
        f3_fwd()

        scale = HEAD_DIM ** -0.5
        H = range(RET_HEADS)
        st = [state[h] for h in H]
        between = [f3_loss, f3_dmx, f3_gw, None]
        rc_t, rs_t = _tile_rotary(tt_ref, cr_ref, sr_ref, sgn_ref)
        for c in range(NCH):
            r0 = c * CHUNK
            rc = rc_t[r0:r0 + CHUNK, :]
            rs = rs_t[r0:r0 + CHUNK, :]
            col = lambda base, h: slice(base + h * HEAD_DIM, base + (h + 1) * HEAD_DIM)
            rws = slice(r0, r0 + CHUNK)
            v = [proj_ref[rws, col(CV, h)] for h in H]
            qf = [_rot(proj_ref[rws, col(CQ, h)].astype(F32), rc, rs) * scale for h in H]
            kf = [_rot(proj_ref[rws, col(CK, h)].astype(F32), rc, rs) for h in H]
            stb = [t.astype(BF16) for t in st]
            for h in H:
                states_ref[c, h] = stb[h]
            a = [(_dot_tb(qf[h].astype(BF16), kf[h].astype(BF16)) * dec_ref[h]).astype(BF16) for h in H]
            o = [_dot(a[h], v[h]) + _dot((qf[h] * xi_ref[h]).astype(BF16), stb[h]) for h in H]
            st = [cd_ref[h, 0:1, :] * st[h] + _dot_ta((kf[h] * zeta_ref[h]).astype(BF16), v[h]) for h in H]
            for h in H:
                mu = jnp.mean(o[h], axis=-1, keepdims=True)
                d = o[h] - mu
                var = jnp.mean(d * d, axis=-1, keepdims=True)
                yh = d * lax.rsqrt(var + EPS)
                rg = proj_ref[rws, col(CR, h)].astype(F32)
                mixed_ref[rws, col(D_CONV, h)] = (yh * g_ref[:, col(0, h)] * (rg * _sigmoid(rg))).astype(BF16)
            if between[c] is not None:
                between[c]()
        for h in H:
            state[h] = st[h]

        @pl.when(s <= nt)
        def _():
            conv_out(slot, pf(s)).start()
            states_out(slot, pf(s)).start()

        @pl.when(s == nt + 1)
        def _():
            conv_out(1 - slot, pf(s - 1)).wait()
            states_out(1 - slot, pf(s - 1)).wait()
            tot = jnp.sum(lacc[...], axis=1, keepdims=True) * (0.5 / D_MODEL)
            loss_ref[...] = jnp.broadcast_to(tot, (1, 128))

    tile = lambda w: pl.BlockSpec((TM, w), lambda s: (pf(s), 0))
    xtile = lambda w: pl.BlockSpec((TM, w), lambda s: (xt(s), 0))
    any_spec = pl.BlockSpec(memory_space=pl.ANY)
    return pl.pallas_call(
        body, name="f2_mixer_fwd_f3_outproj_loss",
        grid=(nt + 2,),
        in_specs=[tile(N_PROJ_COLS), _resident((8, D_CONV)), _resident((1, D_RET)),
                  pl.BlockSpec((None, 8, HEAD_DIM), lambda s: (pf(s), 0, 0)),
                  _resident((TM, HEAD_DIM)), _resident((TM, HEAD_DIM)), _resident((8, HEAD_DIM)),
                  _resident((RET_HEADS, CHUNK, CHUNK)), _resident((RET_HEADS, CHUNK, HEAD_DIM)),
                  _resident((RET_HEADS, CHUNK, HEAD_DIM)), _resident((RET_HEADS, 8, HEAD_DIM)),
                  xtile(D_MODEL), _resident1((D_MODEL, D_MODEL)), _resident((1, D_MODEL)), xtile(D_MODEL)],
        out_specs=[any_spec, any_spec, xtile(D_MODEL), xtile(D_MODEL),
                   _resident((2, N_CHIPS, 128, D_MODEL)), _resident((1, D_MODEL)), _resident((1, 128))],
        out_shape=[jax.ShapeDtypeStruct((rows, D_CONV), BF16),
                   jax.ShapeDtypeStruct(((nt + 1) * NCH, RET_HEADS, HEAD_DIM, HEAD_DIM), BF16),
                   jax.ShapeDtypeStruct((seq, D_MODEL), F32),
                   jax.ShapeDtypeStruct((seq, D_MODEL), BF16),
                   jax.ShapeDtypeStruct((2, N_CHIPS, 128, D_MODEL), F32),
                   jax.ShapeDtypeStruct((1, D_MODEL), F32),
                   jax.ShapeDtypeStruct((1, 128), F32)],
        scratch_shapes=[pltpu.VMEM((RET_HEADS, HEAD_DIM, HEAD_DIM), F32), pltpu.VMEM((8, D_CONV), F32),
                        pltpu.VMEM((2, TM, D_MODEL), BF16), pltpu.VMEM((2, TM, D_CONV), BF16),
                        pltpu.VMEM((2, NCH, RET_HEADS, HEAD_DIM, HEAD_DIM), BF16),
                        pltpu.VMEM((1, D_MODEL), F32), pltpu.SemaphoreType.DMA((4,))],
        compiler_params=_cparams(("arbitrary",)),
    )(proj, conv_w8, gret, tb["tt"], tb["cr2"], tb["sr2"], tb["sgn"], tb["decay"], tb["xi"], tb["zeta"], tb["cd"],
      x, w_out, fg, target)


def _b2_b1a_call(proj, dmixed, conv_s, states, conv_w8, gret, tb, w_in_g, x, meta_tile, g1, dh2):
    nt, rows = tb["nt"], tb["rows"]
    seq = nt * TM

    def pb(r):
        return jnp.where(r == nt, nt, nt - 1 - r)

    def xprev(r):
        return jnp.clip(nt - r, 0, nt - 1)

    def body(proj_ref, dmx_ref, conv_ref, states_ref, cw_ref, g_ref, tt_ref, cr_ref, sr_ref, sgn_ref, dec_ref,
             dect_ref, xi_ref, zeta_ref, cd_ref, w_ref, x_ref, mt_ref, g1_ref, dh2_ref,
             dproj_hbm, gcw_ref, gg_ref, gx_ref, dmeta_ref, gn_ref,
             gstate, dchalo, dps, out_sems):
        r = pl.program_id(0)
        live = jnp.where(r == nt, 0.0, 1.0)
        slot = lax.rem(r, 2)
        dproj_ref = dps.at[slot]

        def to_hbm(s, tile):
            return pltpu.make_async_copy(dps.at[s], dproj_hbm.at[pl.ds(pl.multiple_of(tile * TM, TM), TM), :],
                                         out_sems.at[s])

        @pl.when(r == 0)
        def _():
            gstate[...] = jnp.zeros_like(gstate)
            dchalo[...] = jnp.zeros_like(dchalo)
            gcw_ref[...] = jnp.zeros_like(gcw_ref)
            gg_ref[...] = jnp.zeros_like(gg_ref)
            gn_ref[...] = jnp.zeros_like(gn_ref)
            dps[...] = jnp.zeros_like(dps)

        @pl.when(r >= 2)
        def _():
            to_hbm(slot, pb(r - 2)).wait()

        dprev = dps.at[1 - slot]
        pieces = []

        def emit_piece():
            j = len(pieces)
            if j < N_CHIPS:
                p = _dot_tb(dprev[:, j * 1024:(j + 1) * 1024], w_ref[j])
                pieces.append(p if j == 0 else pieces[-1] + p)

        cx = proj_ref[:, CX:CX + 512].astype(F32)
        cb = proj_ref[:, CB:CB + 512].astype(F32)
        cc = proj_ref[:, CC:CC + 512].astype(F32)
        cg = proj_ref[:, CG:CG + 512].astype(F32)
        dco = dmx_ref[:, 0:D_CONV].astype(F32) * live
        conv = conv_ref[...].astype(F32)
        sg = _sigmoid(cg)
        sil = cg * sg
        t = dco * conv
        dproj_ref[:, CB:CB + 512] = (t * sil).astype(BF16)
        dproj_ref[:, CG:CG + 512] = (t * cb * (sg * (1.0 + cg * (1.0 - sg)))).astype(BF16)
        dconv = dco * cb * sil
        row = lax.broadcasted_iota(jnp.int32, (TM, D_CONV), 0)
        n0 = dchalo[0:1, :]
        n1 = dchalo[1:2, :]
        dc1 = jnp.where(row == TM - 1, n0, pltpu.roll(dconv, TM - 1, 0))
        dc2 = jnp.where(row == TM - 2, n0, jnp.where(row == TM - 1, n1, pltpu.roll(dconv, TM - 2, 0)))
        dchalo[...] = dconv[0:8, :]
        du = cw_ref[2:3, :] * dconv + cw_ref[1:2, :] * dc1 + cw_ref[0:1, :] * dc2
        u = cc * cx
        gcw_ref[2:3, :] += jnp.sum(u * dconv, axis=0, keepdims=True)
        gcw_ref[1:2, :] += jnp.sum(u * dc1, axis=0, keepdims=True)
        gcw_ref[0:1, :] += jnp.sum(u * dc2, axis=0, keepdims=True)
        dproj_ref[:, CC:CC + 512] = (du * cx).astype(BF16)
        dproj_ref[:, CX:CX + 512] = (du * cc).astype(BF16)
        emit_piece()

        scale = HEAD_DIM ** -0.5
        gs = {h: gstate[h] for h in range(RET_HEADS)}
        gg = {h: jnp.zeros((1, HEAD_DIM), F32) for h in range(RET_HEADS)}
        col = lambda base, h: slice(base + h * HEAD_DIM, base + (h + 1) * HEAD_DIM)
        rw = lambda c: slice(c * CHUNK, (c + 1) * CHUNK)
        rc_t, rs_t = _tile_rotary(tt_ref, cr_ref, sr_ref, sgn_ref)
        for c0 in range(NCH - CHUNK_GROUP, -1, -CHUNK_GROUP):
            cs = range(c0 + CHUNK_GROUP - 1, c0 - 1, -1)
            U = [(c, h) for c in cs for h in range(RET_HEADS)]
            rc = {c: rc_t[rw(c), :] for c in cs}
            rs = {c: rs_t[rw(c), :] for c in cs}
            v = {(c, h): proj_ref[rw(c), col(CV, h)] for c, h in U}
            stb = {(c, h): states_ref[c, h] for c, h in U}
            qf = {(c, h): _rot(proj_ref[rw(c), col(CQ, h)].astype(F32), rc[c], rs[c]) * scale for c, h in U}
            kf = {(c, h): _rot(proj_ref[rw(c), col(CK, h)].astype(F32), rc[c], rs[c]) for c, h in U}
            qb = {u: qf[u].astype(BF16) for u in U}
            kb = {u: kf[u].astype(BF16) for u in U}
            qxb = {(c, h): (qf[c, h] * xi_ref[h]).astype(BF16) for c, h in U}
            kzb = {(c, h): (kf[c, h] * zeta_ref[h]).astype(BF16) for c, h in U}
            ab = {(c, h): (_dot_tb(qb[c, h], kb[c, h]) * dec_ref[h]).astype(BF16) for c, h in U}
            atb = {(c, h): (_dot_tb(kb[c, h], qb[c, h]) * dect_ref[h]).astype(BF16) for c, h in U}
            o = {u: _dot(ab[u], v[u]) + _dot(qxb[u], stb[u]) for u in U}
            emit_piece()
            dob = {}
            for c, h in U:
                mu = jnp.mean(o[c, h], axis=-1, keepdims=True)
                d = o[c, h] - mu
                var = jnp.mean(d * d, axis=-1, keepdims=True)
                rstd = lax.rsqrt(var + EPS)
                yh = d * rstd
                g = g_ref[:, col(0, h)]
                rg = proj_ref[rw(c), col(CR, h)].astype(F32)
                dro = dmx_ref[rw(c), col(D_CONV, h)].astype(F32) * live
                sg = _sigmoid(rg)
                dproj_ref[rw(c), col(CR, h)] = (dro * (yh * g) * (sg * (1.0 + rg * (1.0 - sg)))).astype(BF16)
                dret = dro * (rg * sg)
                gg[h] = gg[h] + jnp.sum(dret * yh, axis=0, keepdims=True)
                dyh = dret * g
                do = rstd * (dyh - jnp.mean(dyh, axis=-1, keepdims=True)
                             - yh * jnp.mean(dyh * yh, axis=-1, keepdims=True))
                dob[c, h] = do.astype(BF16)
            dv1 = {u: _dot(atb[u], dob[u]) for u in U}
            ds = {(c, h): (_dot_tb(dob[c, h], v[c, h]) * dec_ref[h]).astype(BF16) for c, h in U}
            dst = {(c, h): (_dot_tb(v[c, h], dob[c, h]) * dect_ref[h]).astype(BF16) for c, h in U}
            gup = {u: _dot_ta(qxb[u], dob[u]) for u in U}
            dq = {(c, h): _dot(ds[c, h], kb[c, h]) + _dot_tb(dob[c, h], stb[c, h]) * xi_ref[h] for c, h in U}
            dk1 = {u: _dot(dst[u], qb[u]) for u in U}
            emit_piece()
            for c, h in U:
                gsb = gs[h].astype(BF16)
                dv = dv1[c, h] + _dot(kzb[c, h], gsb)
                dk = dk1[c, h] + _dot_tb(v[c, h], gsb) * zeta_ref[h]
                gs[h] = cd_ref[h, 0:1, :] * gs[h] + gup[c, h]
                dproj_ref[rw(c), col(CQ, h)] = (_rot_t(dq[c, h], rc[c], rs[c]) * scale).astype(BF16)
                dproj_ref[rw(c), col(CK, h)] = _rot_t(dk, rc[c], rs[c]).astype(BF16)
                dproj_ref[rw(c), col(CV, h)] = dv.astype(BF16)
        for h in range(RET_HEADS):
            gstate[h] = gs[h]
            gg_ref[:, col(0, h)] += gg[h]

        while len(pieces) < N_CHIPS:
            emit_piece()

        def norm_bwd(dhn, hx):
            ms = jnp.mean(hx * hx, axis=-1, keepdims=True)
            rstd1 = lax.rsqrt(ms + EPS)
            xh = hx * rstd1
            gn_ref[...] += jnp.sum(dhn * xh, axis=0, keepdims=True)
            dxh = dhn * g1_ref[...]
            return rstd1 * (dxh - xh * jnp.mean(dxh * xh, axis=-1, keepdims=True))

        gx_ref[...] = norm_bwd(pieces[-1], x_ref[...]) + dh2_ref[...]

        @pl.when(r < nt)
        def _():
            to_hbm(slot, pb(r)).start()

        @pl.when(r == nt)
        def _():
            to_hbm(slot, pb(r)).start()
            mrows = slice(TM - N_META, TM)
            d16 = dproj_ref[mrows, :]
            dhn16 = _dot_tb(d16[:, 0:1024], w_ref[0])
            for j in range(1, N_CHIPS):
                dhn16 += _dot_tb(d16[:, j * 1024:(j + 1) * 1024], w_ref[j])
            dmeta_ref[...] = norm_bwd(dhn16, mt_ref[mrows, :])
            to_hbm(1 - slot, pb(r - 1)).wait()
            to_hbm(slot, pb(r)).wait()

    tile = lambda w: pl.BlockSpec((TM, w), lambda r: (pb(r), 0))
    xtile = pl.BlockSpec((TM, D_MODEL), lambda r: (xprev(r), 0))
    return pl.pallas_call(
        body, name="b2_mixer_bwd_b1a_inproj_bwd_x",
        grid=(nt + 1,),
        in_specs=[tile(N_PROJ_COLS),
                  pl.BlockSpec((TM, D_MODEL), lambda r: (jnp.minimum(pb(r), nt - 1), 0)),
                  tile(D_CONV),
                  pl.BlockSpec((NCH, RET_HEADS, HEAD_DIM, HEAD_DIM), lambda r: (pb(r), 0, 0, 0)),
                  _resident((8, D_CONV)), _resident((1, D_RET)),
                  pl.BlockSpec((None, 8, HEAD_DIM), lambda r: (pb(r), 0, 0)),
                  _resident((TM, HEAD_DIM)), _resident((TM, HEAD_DIM)), _resident((8, HEAD_DIM)),
                  _resident((RET_HEADS, CHUNK, CHUNK)), _resident((RET_HEADS, CHUNK, CHUNK)),
                  _resident((RET_HEADS, CHUNK, HEAD_DIM)),
                  _resident((RET_HEADS, CHUNK, HEAD_DIM)), _resident((RET_HEADS, 8, HEAD_DIM)),
                  _resident1((N_CHIPS, D_MODEL, 1024)), xtile, _resident1((TM, D_MODEL)), _resident((1, D_MODEL)),
                  xtile],
        out_specs=[pl.BlockSpec(memory_space=pl.ANY), _resident((8, D_CONV)), _resident((1, D_RET)),
                   xtile, _resident((N_META, D_MODEL)), _resident((1, D_MODEL))],
        out_shape=[jax.ShapeDtypeStruct((rows, N_PROJ_COLS), BF16),
                   jax.ShapeDtypeStruct((8, D_CONV), F32),
                   jax.ShapeDtypeStruct((1, D_RET), F32),
                   jax.ShapeDtypeStruct((seq, D_MODEL), F32),
                   jax.ShapeDtypeStruct((N_META, D_MODEL), F32),
                   jax.ShapeDtypeStruct((1, D_MODEL), F32)],
        scratch_shapes=[pltpu.VMEM((RET_HEADS, HEAD_DIM, HEAD_DIM), F32), pltpu.VMEM((8, D_CONV), F32),
                        pltpu.VMEM((2, TM, N_PROJ_COLS), BF16), pltpu.SemaphoreType.DMA((2,))],
        compiler_params=_cparams(("arbitrary",), vmem=VMEM_LIMIT_MAX),
    )(proj, dmixed, conv_s, states, conv_w8, gret, tb["tt"], tb["cr2"], tb["sr2"], tb["sgn"], tb["decay"],
      tb["decay_t"], tb["xi"], tb["zeta"], tb["cd"], w_in_g, x, meta_tile, g1, dh2)


REL = (2, 1, 3)
SMALL_ROWS = 24
HALF_STEP = 4


def _rcopy(src, dst, send_sems, recv_sems, k, to):
    return pltpu.make_async_remote_copy(src_ref=src, dst_ref=dst, send_sem=send_sems.at[k],
                                        recv_sem=recv_sems.at[k], device_id=to, device_id_type=MESH_ID)


def _b1b_reduce_call(order, hnt, dproj, gwo, pack, adam_win, adam_wout, nt):
    nk = nt + 1
    last = nk - 1
    any_spec = pl.BlockSpec(memory_space=pl.ANY)

    def body(order_ref, a_ref, b_ref, gwo_hbm, pack_hbm, ww, wm, wv, ow, om, ov,
             gwin_hbm, gwout_hbm, tot_hbm, d_ww, n_wm, n_wv, d_ow, n_om, n_ov,
             acc, sb, abuf, pb, bbuf, fin, go, ao, pbo, bo, fino, slots, totv, vbuf, awo,
             send_sems, recv_sems, loc_sems):
        jj, k = pl.program_id(0), pl.program_id(1)
        x, y, c = lax.axis_index("x"), lax.axis_index("y"), lax.axis_index("c")
        me, myid, sib = 2 * x + y, 4 * x + 2 * y + c, (x, y, 1 - c)
        rc = functools.partial(_rcopy, send_sems=send_sems, recv_sems=recv_sems)
        peers = [((1 - x) if r & 2 else x, (1 - y) if r & 1 else y, c) for r in REL]
        kids = [jnp.bitwise_xor(me, r) for r in REL]

        def dev_peer(r):
            return ((1 - x) if r & 4 else x, (1 - y) if r & 2 else y, (1 - c) if r & 1 else c)

        own_go = pltpu.make_async_copy(gwo_hbm.at[c], go, loc_sems.at[0])
        own_pack = pltpu.make_async_copy(pack_hbm, slots.at[0], loc_sems.at[1])
        wo_half = rc(gwo_hbm.at[1 - c], ao, k=8, to=sib)
        wo_part = [rc(pbo.at[kids[p]], bo.at[p], k=9 + p, to=peers[p]) for p in range(3)]
        sm = [rc(pack_hbm, slots.at[r], k=12 + r, to=dev_peer(r)) for r in range(1, N_DEV)]
        half = [rc(sb.at[j % 2, 1 - c], abuf.at[j], k=j, to=sib) for j in range(N_CHIPS)]
        part = [rc(pb.at[p], bbuf.at[p], k=4 + p, to=peers[p]) for p in range(3)]
        adam_buf = [abuf.at[pl.ds(0, 2)], sb.at[0], vbuf, awo.at[0], awo.at[1], awo.at[2]]
        adam_in = [pltpu.make_async_copy(src, adam_buf[i], loc_sems.at[3 + i])
                   for i, src in enumerate((ww, wm, wv, ow, om, ov))]
        adam_out = [d_ww, n_wm, n_wv, d_ow, n_om, n_ov]

        @pl.when(jnp.logical_and(jj == 0, k == 0))
        def _():
            own_go.start()
            own_pack.start()
            wo_half.start()
            for cp in sm:
                cp.start()

        @pl.when(k == 0)
        def _():
            acc[...] = jnp.zeros_like(acc)

        acc[0] += _dot(a_ref[0:512, :], b_ref[...])
        acc[1] += _dot(a_ref[512:1024, :], b_ref[...])

        @pl.when(k == HALF_STEP)
        def _():
            @pl.when(jj == 0)
            def _():
                own_go.wait()
                wo_half.wait_recv()
                for j in range(N_CHIPS):
                    go[j] = go[j] + ao[j]
                pbo[...] = go[...].astype(BF16)
                for cp in wo_part:
                    cp.start()

            for p in range(3):
                @pl.when(jj == p + 1)
                def _(p=p):
                    half[p].wait_recv()
                    half[p].wait_send()
                    pb[p] = (sb[p % 2, c] + abuf[p]).astype(BF16)
                    part[p].start()

        @pl.when(jnp.logical_and(jj == N_CHIPS - 1, k == HALF_STEP + 1))
        def _():
            for cp in adam_in:
                cp.start()

        @pl.when(k == last)
        def _():
            for j in range(N_CHIPS):
                @pl.when(jj == j)
                def _(j=j):
                    sb[j % 2] = acc[...]
                    half[j].start()

        @pl.when(jnp.logical_and(jj == N_CHIPS - 1, k == last))
        def _():
            half[3].wait_recv()
            own = sb[1, c] + abuf[3]
            for cp in part:
                cp.wait_recv()
            fin[c] = ((own + bbuf[0].astype(F32)) + bbuf[1].astype(F32)) + bbuf[2].astype(F32)
            done = rc(fin.at[c], fin.at[c], k=7, to=sib)
            done.start()
            for cp in wo_part:
                cp.wait_recv()
            fino[c] = ((go[me] + bo[0].astype(F32)) + bo[1].astype(F32)) + bo[2].astype(F32)
            done_o = rc(fino.at[c], fino.at[c], k=12, to=sib)
            done_o.start()
            own_pack.wait()
            for cp in sm:
                cp.wait_recv()
            tot = slots[myid]
            for a in range(1, N_DEV):
                tot = tot + slots[jnp.bitwise_xor(myid, a)]
            totv[...] = tot
            out_t = pltpu.make_async_copy(totv, tot_hbm, loc_sems.at[1])
            out_t.start()
            rc(fin.at[1 - c], fin.at[1 - c], k=7, to=sib).wait_recv()
            out_w = pltpu.make_async_copy(fin, gwin_hbm, loc_sems.at[0])
            out_w.start()
            for cp in adam_in[0:3]:
                cp.wait()
            _adamw_in_place(adam_buf[0], fin, adam_buf[1], adam_buf[2], 512)
            outs = [pltpu.make_async_copy(adam_buf[i], adam_out[i], loc_sems.at[9 + i]) for i in range(3)]
            rc(fino.at[1 - c], fino.at[1 - c], k=12, to=sib).wait_recv()
            out_o = pltpu.make_async_copy(fino, gwout_hbm, loc_sems.at[2])
            out_o.start()
            for cp in adam_in[3:6]:
                cp.wait()
            _adamw_in_place(adam_buf[3], fino, adam_buf[4], adam_buf[5], 128)
            outs += [pltpu.make_async_copy(adam_buf[i], adam_out[i], loc_sems.at[9 + i]) for i in range(3, 6)]
            for cp in outs:
                cp.start()
            for cp in [half[3]] + part + [done, wo_half] + wo_part + [done_o] + sm:
                cp.wait_send()
            for cp in [out_t, out_w, out_o] + outs:
                cp.wait()

    grid_spec = pltpu.PrefetchScalarGridSpec(
        num_scalar_prefetch=1,
        grid=(N_CHIPS, nk),
        in_specs=[pl.BlockSpec((D_MODEL, TM), lambda j, k, o: (0, k)),
                  pl.BlockSpec((TM, 1024), lambda j, k, o: (k, o[j]))] + [any_spec] * 8,
        out_specs=[any_spec] * 9,
        scratch_shapes=[
            pltpu.VMEM((2, 512, 1024), F32),
            pltpu.VMEM((2, 2, 512, 1024), F32),
            pltpu.VMEM((N_CHIPS, 512, 1024), F32),
            pltpu.VMEM((3, 512, 1024), BF16),
            pltpu.VMEM((3, 512, 1024), BF16),
            pltpu.VMEM((2, 512, 1024), F32),
            pltpu.VMEM((N_CHIPS, 128, D_MODEL), F32),
            pltpu.VMEM((N_CHIPS, 128, D_MODEL), F32),
            pltpu.VMEM((N_CHIPS, 128, D_MODEL), BF16),
            pltpu.VMEM((3, 128, D_MODEL), BF16),
            pltpu.VMEM((2, 128, D_MODEL), F32),
            pltpu.VMEM((N_DEV, SMALL_ROWS, D_MODEL), F32),
            pltpu.VMEM((SMALL_ROWS, D_MODEL), F32),
            pltpu.VMEM((2, 512, 1024), F32),
            pltpu.VMEM((3, 2, 128, D_MODEL), F32),
            pltpu.SemaphoreType.DMA((20,)), pltpu.SemaphoreType.DMA((20,)), pltpu.SemaphoreType.DMA((15,))])
    big, small = jax.ShapeDtypeStruct((2, 512, 1024), F32), jax.ShapeDtypeStruct((2, 128, D_MODEL), F32)
    return pl.pallas_call(
        body, name="b1b_inproj_bwd_w_reduce_adamw",
        grid_spec=grid_spec,
        out_shape=[big, small, jax.ShapeDtypeStruct((SMALL_ROWS, D_MODEL), F32), big, big, big, small, small, small],
        compiler_params=_cparams(("arbitrary", "arbitrary"), vmem=VMEM_LIMIT_MAX),
    )(order, hnt, dproj, gwo, pack, *[a.reshape(2, 512, 1024) for a in adam_win],
      *[a.reshape(2, 128, D_MODEL) for a in adam_wout])


def _local_step(me, x, target, g1, gret, fg, win_sh, wout_sh, small_sh):
    seq = x.shape[0]
    tb = _tables(seq)
    nt = tb["nt"]
    g1r, gretr, fgr = g1.reshape(1, -1), gret.reshape(1, -1), fg.reshape(1, -1)
    order = jnp.stack([me, me ^ REL[0], me ^ REL[1], me ^ REL[2]]).astype(jnp.int32)

    proj, hnt, w_in_g, w_out_g, meta_tile, conv_w8 = _f1_gather_call(order, x, g1r, win_sh, wout_sh, small_sh, nt)
    w_out = w_out_g.reshape(D_MODEL, D_MODEL)
    conv_s, states, dh2, dmixed, g_wout, g_fg, loss = _f2_f3_call(proj, conv_w8, gretr, tb, x, w_out, fgr, target)
    dproj, g_cw8, g_gret, grad_x, g_meta, g_g1 = _b2_b1a_call(proj, dmixed, conv_s, states, conv_w8, gretr, tb,
                                                              w_in_g, x, meta_tile, g1r, dh2)
    return loss, grad_x, dict(w_out=g_wout, meta=g_meta, conv_w=g_cw8[0:3], norm1_g=g_g1,
                              ret_norm_g=g_gret, final_g=g_fg), hnt, dproj


def _adamw_update(w_ref, g_ref, m_ref, v_ref, d_ref, nm_ref, nv_ref):
    gg = g_ref[...]
    nm = ADAM_B1 * m_ref[...] + (1.0 - ADAM_B1) * gg
    nv = ADAM_B2 * v_ref[...] + (1.0 - ADAM_B2) * (gg * gg)
    m_hat = nm / (1.0 - ADAM_B1 ** ADAM_STEP)
    v_hat = nv / (1.0 - ADAM_B2 ** ADAM_STEP)
    d_ref[...] = -ADAM_LR * (m_hat / (jnp.sqrt(v_hat) + ADAM_EPS) + ADAM_WD * w_ref[...])
    nm_ref[...] = nm
    nv_ref[...] = nv


def _adamw_in_place(w_ref, g_ref, m_ref, v_ref, rows):
    for hf in range(2):
        for r0 in range(0, rows, 128):
            sl = (hf, slice(r0, r0 + 128))
            _adamw_update(w_ref.at[sl], g_ref.at[sl], m_ref.at[sl], v_ref.at[sl],
                          w_ref.at[sl], m_ref.at[sl], v_ref.at[sl])


def _adamw_small_call(ws, gs, ms, vs):
    n = len(ws)

    def body(*refs):
        ins, outs = refs[:4 * n], refs[4 * n:]
        for i in range(n):
            _adamw_update(ins[i], ins[n + i], ins[2 * n + i], ins[3 * n + i],
                          outs[i], outs[n + i], outs[2 * n + i])

    shapes = [jax.ShapeDtypeStruct(w.shape, F32) for w in ws]
    outs = pl.pallas_call(body, name="adamw_small", out_shape=shapes * 3,
                          compiler_params=_cparams())(*ws, *gs, *ms, *vs)
    return outs[:n], outs[n:2 * n], outs[2 * n:]


def _pad_to(a, rows, cols):
    return jnp.pad(a, ((0, rows - a.shape[0]), (0, cols - a.shape[1])))


def kernel(x, meta, norm1_g, w_in, conv_w, ret_norm_g, w_out, final_g, loss_target, m_meta, m_norm1_g, m_w_in, m_conv_w, m_ret_norm_g, m_w_out, m_final_g, v_meta, v_norm1_g, v_w_in, v_conv_w, v_ret_norm_g, v_w_out, v_final_g):
    me = 2 * lax.axis_index("x") + lax.axis_index("y")

    small_sh = jnp.concatenate([meta, _pad_to(conv_w, 8, 256)], axis=0)
    loss, grad_x, g, hnt, dproj = _local_step(me, x[0], loss_target[0], norm1_g, ret_norm_g, final_g,
                                              w_in.astype(BF16), w_out.astype(BF16), small_sh)

    vec = jnp.concatenate([g["norm1_g"], g["final_g"], _pad_to(g["ret_norm_g"], 1, D_MODEL),
                           _pad_to(g["conv_w"], 3, D_MODEL), _pad_to(loss, 2, D_MODEL)], axis=0)
    pack = jnp.concatenate([g["meta"], vec], axis=0)
    order = jnp.stack([me ^ REL[0], me ^ REL[1], me ^ REL[2], me]).astype(jnp.int32)
    g_win, g_wout, tot, *adam_big = _b1b_reduce_call(order, hnt, dproj, g["w_out"], pack, (w_in, m_w_in, v_w_in),
                                                     (w_out, m_w_out, v_w_out), x.shape[1] // TM)
    g_win, g_wout = g_win.reshape(D_MODEL, 1024), g_wout.reshape(256, D_MODEL)
    adam_big = [a.reshape(D_MODEL, 1024) for a in adam_big[0:3]] + [a.reshape(256, D_MODEL) for a in adam_big[3:6]]
    g_meta = lax.dynamic_slice(tot, (0, me * 256), (N_META, 256))
    g_conv = lax.dynamic_slice(tot, (N_META + 3, me * 128), (3, 128))
    g_n1, g_fg, g_rn = tot[N_META], tot[N_META + 1], tot[N_META + 2, 0:D_RET]

    loss_tot = tot[N_META + 6, 0]

    grads = [g_meta, g_n1, g_win, g_conv, g_rn, g_wout, g_fg]
    ws = [meta, norm1_g, w_in, conv_w, ret_norm_g, w_out, final_g]
    ms = [m_meta, m_norm1_g, m_w_in, m_conv_w, m_ret_norm_g, m_w_out, m_final_g]
    vs = [v_meta, v_norm1_g, v_w_in, v_conv_w, v_ret_norm_g, v_w_out, v_final_g]
    names = ["meta", "norm1_g", "w_in", "conv_w", "ret_norm_g", "w_out", "final_g"]
    as2d = lambda a: a.reshape(1, -1) if a.ndim == 1 else a
    big = [i for i, n_ in enumerate(names) if n_ in ("w_in", "w_out")]
    small = [i for i in range(len(names)) if i not in big]
    deltas, new_ms, new_vs = [None] * 7, [None] * 7, [None] * 7
    for n, i in enumerate(big):
        deltas[i], new_ms[i], new_vs[i] = adam_big[3 * n:3 * n + 3]
    sd, sm_, sv = _adamw_small_call(*[[as2d(t[i]) for i in small] for t in (ws, grads, ms, vs)])
    for j, i in enumerate(small):
        deltas[i], new_ms[i], new_vs[i] = (o[j].reshape(ws[i].shape) for o in (sd, sm_, sv))
    return (loss_tot, grad_x[None], *grads, *deltas, *new_ms, *new_vs)
```

```python
import functools

import jax
import jax.numpy as jnp
from jax import lax
from jax.experimental import pallas as pl
from jax.experimental.pallas import tpu as pltpu

F32 = jnp.float32
BF16 = jnp.bfloat16

D_MODEL = 1024
N_META = 16
D_CONV = 512
D_RET = 512
RET_HEADS = 4
HEAD_DIM = 128
CHUNK = 128
N_PROJ_COLS = 4096
ROPE_BASE = 10000.0
EPS = 1e-6
N_CHIPS = 4
N_DEV = 8

ADAM_LR = 0.001
ADAM_B1 = 0.9
ADAM_B2 = 0.999
ADAM_EPS = 1e-08
ADAM_WD = 0.01
ADAM_STEP = 10

TM = 512
NCH = TM // CHUNK
CHUNK_GROUP = 2
VMEM_LIMIT = 56 * 1024 * 1024
VMEM_LIMIT_MAX = 63 * 1024 * 1024

CX, CB, CC, CG, CQ, CK, CV, CR = (i * 512 for i in range(8))

MESH_ID = pl.DeviceIdType.MESH


def _cparams(sem=None, vmem=VMEM_LIMIT, **kw):
    return pltpu.CompilerParams(dimension_semantics=sem, vmem_limit_bytes=vmem, **kw)


def _sigmoid(x):
    return 1.0 / (1.0 + jnp.exp(-x))


def _dot(a, b):
    return jnp.dot(a, b, preferred_element_type=F32)


def _dot_tb(a, b):
    return lax.dot_general(a, b, (((1,), (1,)), ((), ())), preferred_element_type=F32)


def _dot_ta(a, b):
    return lax.dot_general(a, b, (((0,), (0,)), ((), ())), preferred_element_type=F32)


def _resident(shape):
    nd = len(shape)
    return pl.BlockSpec(shape, lambda *_: (0,) * nd)


def _resident1(shape):
    nd = len(shape)
    return pl.BlockSpec(shape, lambda *_: (0,) * nd, pipeline_mode=pl.Buffered(1))


def _tables(seq):
    nt = seq // TM
    rows = seq + TM
    half = HEAD_DIM // 2
    freqs = 1.0 / (ROPE_BASE ** (jnp.arange(half, dtype=F32) / half))
    tile_start = jnp.concatenate([jnp.arange(nt, dtype=F32), -jnp.ones((1,), F32)]) * TM
    ang_t = tile_start[:, None] * freqs[None, :]
    ang_r = (jnp.arange(TM, dtype=F32) + N_META)[:, None] * freqs[None, :]
    dup = lambda a: jnp.concatenate([a, a], axis=-1)
    tt = jnp.stack([dup(jnp.cos(ang_t)), dup(jnp.sin(ang_t))], axis=1)
    tt = jnp.pad(tt, ((0, 0), (0, 6), (0, 0)))
    cr2, sr2 = dup(jnp.cos(ang_r)), dup(jnp.sin(ang_r))
    sgn = jnp.concatenate([-jnp.ones((8, half), F32), jnp.ones((8, half), F32)], axis=-1)
    log_g = jnp.log(1.0 - 2.0 ** (-5.0 - jnp.arange(RET_HEADS, dtype=F32)))
    idx = jnp.arange(CHUNK, dtype=F32)
    diff = idx[:, None] - idx[None, :]
    decay = jnp.where(diff[None] >= 0, jnp.exp(diff[None] * log_g[:, None, None]), 0.0)
    zeta = jnp.exp((CHUNK - 1 - idx)[None, :] * log_g[:, None])
    xi = jnp.exp((idx + 1.0)[None, :] * log_g[:, None])
    cd = jnp.exp(CHUNK * log_g)
    zeta_b = jnp.broadcast_to(zeta[:, :, None], (RET_HEADS, CHUNK, HEAD_DIM))
    xi_b = jnp.broadcast_to(xi[:, :, None], (RET_HEADS, CHUNK, HEAD_DIM))
    cd_b = jnp.broadcast_to(cd[:, None, None], (RET_HEADS, 8, HEAD_DIM))
    return dict(nt=nt, rows=rows, tt=tt, cr2=cr2, sr2=sr2, sgn=sgn, decay=decay,
                decay_t=jnp.swapaxes(decay, 1, 2), zeta=zeta_b, xi=xi_b, cd=cd_b)


def _tile_rotary(tt_ref, cr_ref, sr_ref, sgn_ref):
    ct, st = tt_ref[0:1, :], tt_ref[1:2, :]
    cr, sr = cr_ref[...], sr_ref[...]
    return ct * cr - st * sr, (st * cr + ct * sr) * sgn_ref[0:1, :]


def _rot(t, rc, rs):
    return t * rc + pltpu.roll(t, HEAD_DIM // 2, 1) * rs


def _rot_t(dt, rc, rs):
    return dt * rc + pltpu.roll(dt * rs, HEAD_DIM // 2, 1)


def _f1_gather_call(order, x, g1, win_sh, wout_sh, small_sh, nt):
    nk = nt + 1
    rows = nk * TM
    any_spec = pl.BlockSpec(memory_space=pl.ANY)

    def body(order_ref, x_ref, g_ref, win_hbm, wout_hbm, sm_hbm,
             proj_ref, hnt_ref, wg_hbm, wog_hbm, mt_hbm, cw_hbm,
             wg, wog, smg, mt, cw, hbs, send_sems, recv_sems, loc_sems):
        jj, k = pl.program_id(0), pl.program_id(1)
        x, y, c = lax.axis_index("x"), lax.axis_index("y"), lax.axis_index("c")
        me, sib = 2 * x + y, (x, y, 1 - c)
        rc = functools.partial(_rcopy, send_sems=send_sems, recv_sems=recv_sems)
        peers = [((1 - x) if r & 2 else x, (1 - y) if r & 1 else y, c) for r in REL]
        kids = [jnp.bitwise_xor(me, r) for r in REL]
        hw, ho = pl.ds(c * 512, 512), pl.ds(c * 128, 128)
        hw2, ho2 = pl.ds((1 - c) * 512, 512), pl.ds((1 - c) * 128, 128)
        at = lambda j_, k_: jnp.logical_and(jj == j_, k == k_)

        sm_cp = [rc(sm_hbm, smg.at[me], k=p, to=peers[p]) for p in range(3)]
        win_cp = [rc(win_hbm.at[hw], wg.at[me, hw], k=3 + p, to=peers[p]) for p in range(3)]
        wout_cp = [rc(wout_hbm.at[ho], wog.at[me, ho], k=6 + p, to=peers[p]) for p in range(3)]
        sm_in = [rc(sm_hbm, smg.at[kids[p]], k=p, to=sib) for p in range(3)]
        win_in = [rc(win_hbm.at[hw], wg.at[kids[p], hw], k=3 + p, to=sib) for p in range(3)]
        wout_in = [rc(wout_hbm.at[ho], wog.at[kids[p], ho], k=6 + p, to=sib) for p in range(3)]
        win_fw = [rc(wg.at[kids[p], hw], wg.at[kids[p], hw], k=9 + p, to=sib) for p in range(3)]
        wout_fw = [rc(wog.at[kids[p], ho], wog.at[kids[p], ho], k=12 + p, to=sib) for p in range(3)]
        win_fw_in = [rc(wg.at[kids[p], hw2], wg.at[kids[p], hw2], k=9 + p, to=sib) for p in range(3)]
        wout_fw_in = [rc(wog.at[kids[p], ho2], wog.at[kids[p], ho2], k=12 + p, to=sib) for p in range(3)]
        own_w = pltpu.make_async_copy(win_hbm, wg.at[me], loc_sems.at[0])
        own_o = pltpu.make_async_copy(wout_hbm, wog.at[me], loc_sems.at[1])
        own_s = pltpu.make_async_copy(sm_hbm, smg.at[me], loc_sems.at[2])
        out_wg = pltpu.make_async_copy(wg, wg_hbm, loc_sems.at[3])
        out_wog = pltpu.make_async_copy(wog, wog_hbm, loc_sems.at[4])
        out_mt = pltpu.make_async_copy(mt, mt_hbm, loc_sems.at[5])
        out_cw = pltpu.make_async_copy(cw, cw_hbm, loc_sems.at[6])

        def pass_on(p):
            win_in[p].wait_recv()
            win_fw[p].start()

        @pl.when(k <= 1)
        def _():
            @pl.when(at(0, 0))
            def _():
                own_w.start()
                own_s.start()
                own_o.start()
                for cp in sm_cp + win_cp + wout_cp:
                    cp.start()
                own_w.wait()

            for p in range(3):
                @pl.when(at(p + 1, 0))
                def _(p=p):
                    win_fw_in[p].wait_recv()

            @pl.when(at(1, 1))
            def _():
                pass_on(1)

            @pl.when(at(3, 0))
            def _():
                out_wg.start()

            @pl.when(at(3, 1))
            def _():
                for p in range(3):
                    wout_in[p].wait_recv()
                    wout_fw[p].start()

        @pl.when(jnp.logical_and(jj == 0, k >= nk - 2))
        def _():
            @pl.when(k == nk - 2)
            def _():
                own_s.wait()
                for cp in sm_in:
                    cp.wait_recv()
                mt[...] = jnp.zeros_like(mt)
                cw[...] = jnp.zeros_like(cw)
                for j in range(N_CHIPS):
                    mt[TM - N_META:TM, j * 256:(j + 1) * 256] = smg[j, 0:N_META, :]
                    cw[0:3, j * 128:(j + 1) * 128] = smg[j, N_META:N_META + 3, 0:128]
                out_mt.start()
                out_cw.start()

            @pl.when(k == nk - 1)
            def _():
                pass_on(0)

        @pl.when(at(2, nk // 2))
        def _():
            pass_on(2)

        tile_rows = pl.ds(pl.multiple_of(k * TM, TM), TM)

        @pl.when(jj == 0)
        def _():
            h = jnp.where(k == nt, mt[...], x_ref[...])
            ms = jnp.mean(h * h, axis=-1, keepdims=True)
            hn = (h * lax.rsqrt(ms + EPS)) * g_ref[...]
            hb = hn.astype(BF16)
            hbs[tile_rows, :] = hb
            proj_ref[...] = _dot(hb, wg[order_ref[0]]).astype(BF16)
            hnt_ref[...] = hn.T.astype(BF16)

        @pl.when(jj > 0)
        def _():
            proj_ref[...] = _dot(hbs[tile_rows, :], wg[order_ref[jj]]).astype(BF16)

        @pl.when(at(3, nk - 1))
        def _():
            own_o.wait()
            for cp in wout_fw_in:
                cp.wait_recv()
            out_wog.start()
            for cp in sm_cp + win_cp + wout_cp + win_fw + wout_fw:
                cp.wait_send()
            for cp in (out_wg, out_wog, out_mt, out_cw):
                cp.wait()

    grid_spec = pltpu.PrefetchScalarGridSpec(
        num_scalar_prefetch=1,
        grid=(N_CHIPS, nk),
        in_specs=[pl.BlockSpec((TM, D_MODEL), lambda j, k, o: (jnp.where(j == 0, jnp.minimum(k, nt - 1), nt - 1), 0)),
                  pl.BlockSpec((1, D_MODEL), lambda j, k, o: (0, 0)),
                  any_spec, any_spec, any_spec],
        out_specs=[pl.BlockSpec((TM, 1024), lambda j, k, o: (k, o[j])),
                   pl.BlockSpec((None, D_MODEL, TM), lambda j, k, o: (jnp.where(j == 0, k, nk - 1), 0, 0)),
                   any_spec, any_spec, any_spec, any_spec],
        scratch_shapes=[
            pltpu.VMEM((N_CHIPS, D_MODEL, 1024), BF16),
            pltpu.VMEM((N_CHIPS, 256, D_MODEL), BF16),
            pltpu.VMEM((N_CHIPS, SMALL_ROWS, 256), F32),
            pltpu.VMEM((TM, D_MODEL), F32),
            pltpu.VMEM((8, D_CONV), F32),
            pltpu.VMEM((rows, D_MODEL), BF16),
            pltpu.SemaphoreType.DMA((15,)), pltpu.SemaphoreType.DMA((15,)), pltpu.SemaphoreType.DMA((7,))])
    return pl.pallas_call(
        body, name="f1_norm_inproj_gather",
        grid_spec=grid_spec,
        out_shape=[jax.ShapeDtypeStruct((rows, N_PROJ_COLS), BF16),
                   jax.ShapeDtypeStruct((nk, D_MODEL, TM), BF16),
                   jax.ShapeDtypeStruct((N_CHIPS, D_MODEL, 1024), BF16),
                   jax.ShapeDtypeStruct((N_CHIPS, 256, D_MODEL), BF16),
                   jax.ShapeDtypeStruct((TM, D_MODEL), F32),
                   jax.ShapeDtypeStruct((8, D_CONV), F32)],
        compiler_params=_cparams(("arbitrary", "arbitrary")),
    )(order, x, g1, win_sh, wout_sh, small_sh)


def _f2_f3_call(proj, conv_w8, gret, tb, x, w_out, fg, target):
    nt, rows = tb["nt"], tb["rows"]
    seq = nt * TM

    def pf(s):
        return jnp.where(s == 0, nt, jnp.minimum(s - 1, nt - 1))

    def xt(s):
        return jnp.clip(s - 2, 0, nt - 1)

    def body(proj_ref, cw_ref, g_ref, tt_ref, cr_ref, sr_ref, sgn_ref, dec_ref, xi_ref, zeta_ref, cd_ref,
             x_ref, w_ref, fg_ref, t_ref,
             conv_hbm, states_hbm, dh2_ref, dmx_ref, gwo_ref, gfg_ref, loss_ref,
             state, uhalo, mxs, convs, sts, lacc, out_sems):
        s = pl.program_id(0)
        slot = lax.rem(s, 2)
        mixed_ref = mxs.at[slot]
        conv_ref = convs.at[slot]
        states_ref = sts.at[slot]

        def conv_out(sl, tile):
            return pltpu.make_async_copy(convs.at[sl], conv_hbm.at[pl.ds(pl.multiple_of(tile * TM, TM), TM), :],
                                         out_sems.at[sl])

        def states_out(sl, tile):
            return pltpu.make_async_copy(sts.at[sl], states_hbm.at[pl.ds(pl.multiple_of(tile * NCH, NCH), NCH)],
                                         out_sems.at[2 + sl])

        @pl.when(s == 0)
        def _():
            state[...] = jnp.zeros_like(state)
            uhalo[...] = jnp.zeros_like(uhalo)
            mxs[...] = jnp.zeros_like(mxs)
            gwo_ref[...] = jnp.zeros_like(gwo_ref)
            gfg_ref[...] = jnp.zeros_like(gfg_ref)
            lacc[...] = jnp.zeros_like(lacc)

        @pl.when(s >= 2)
        def _():
            conv_out(slot, pf(s - 2)).wait()
            states_out(slot, pf(s - 2)).wait()

        valid = jnp.where(s >= 2, 1.0, 0.0)
        mx_prev = mxs.at[1 - slot]
        f3 = {}

        def f3_fwd():
            f3["h2"] = x_ref[...] + _dot(mx_prev[...], w_ref[...])

        def f3_loss():
            h2 = f3.pop("h2")
            ms = jnp.mean(h2 * h2, axis=-1, keepdims=True)
            rstd = lax.rsqrt(ms + EPS)
            yh = h2 * rstd
            g = fg_ref[...]
            e = (yh * g - t_ref[...]) * valid
            lacc[...] += jnp.sum(e * e, axis=0, keepdims=True)
            dy = e * (1.0 / D_MODEL)
            gfg_ref[...] += jnp.sum(dy * yh, axis=0, keepdims=True)
            dyh = dy * g
            dh2 = rstd * (dyh - yh * jnp.mean(dyh * yh, axis=-1, keepdims=True))
            dh2_ref[...] = dh2
            f3["db"] = dh2.astype(BF16)

        def f3_dmx():
            dmx_ref[...] = _dot_tb(f3["db"], w_ref[...]).astype(BF16)

        def f3_gw():
            gw = _dot_ta(mx_prev[...], f3["db"])
            for j in range(N_CHIPS):
                for hf in range(2):
                    r0 = j * 256 + hf * 128
                    gwo_ref[hf, j] += gw[r0:r0 + 128, :]

        cx = proj_ref[:, CX:CX + 512].astype(F32)
        cc = proj_ref[:, CC:CC + 512].astype(F32)
        u = cc * cx
        row = lax.broadcasted_iota(jnp.int32, (TM, D_CONV), 0)
        h7 = uhalo[7:8, :]
        h6 = uhalo[6:7, :]
        u1 = jnp.where(row == 0, h7, pltpu.roll(u, 1, 0))
        u2 = jnp.where(row == 0, h6, jnp.where(row == 1, h7, pltpu.roll(u, 2, 0)))
        conv = cw_ref[2:3, :] * u + cw_ref[1:2, :] * u1 + cw_ref[0:1, :] * u2
        uhalo[...] = u[TM - 8:TM, :]
        cb = proj_ref[:, CB:CB + 512].astype(F32)
        cg = proj_ref[:, CG:CG + 512].astype(F32)
        mixed_ref[:, 0:D_CONV] = (cb * conv * (cg * _sigmoid(cg))).astype(BF16)
        conv_ref[...] = conv.astype(BF16)
        f3_fwd()

        scale = HEAD_DIM ** -0.5
        H = range(RET_HEADS)
        st = [state[h] for h in H]
        between = [f3_loss, f3_dmx, f3_gw, None]
        rc_t, rs_t = _tile_rotary(tt_ref, cr_ref, sr_ref, sgn_ref)
        for c in range(NCH):
            r0 = c * CHUNK
            rc = rc_t[r0:r0 + CHUNK, :]
            rs = rs_t[r0:r0 + CHUNK, :]
            col = lambda base, h: slice(base + h * HEAD_DIM, base + (h + 1) * HEAD_DIM)
            rws = slice(r0, r0 + CHUNK)
            v = [proj_ref[rws, col(CV, h)] for h in H]
            qf = [_rot(proj_ref[rws, col(CQ, h)].astype(F32), rc, rs) * scale for h in H]
            kf = [_rot(proj_ref[rws, col(CK, h)].astype(F32), rc, rs) for h in H]
            stb = [t.astype(BF16) for t in st]
            for h in H:
                states_ref[c, h] = stb[h]
            a = [(_dot_tb(qf[h].astype(BF16), kf[h].astype(BF16)) * dec_ref[h]).astype(BF16) for h in H]
            o = [_dot(a[h], v[h]) + _dot((qf[h] * xi_ref[h]).astype(BF16), stb[h]) for h in H]
            st = [cd_ref[h, 0:1, :] * st[h] + _dot_ta((kf[h] * zeta_ref[h]).astype(BF16), v[h]) for h in H]
            for h in H:
                mu = jnp.mean(o[h], axis=-1, keepdims=True)
                d = o[h] - mu
                var = jnp.mean(d * d, axis=-1, keepdims=True)
                yh = d * lax.rsqrt(var + EPS)
                rg = proj_ref[rws, col(CR, h)].astype(F32)
                mixed_ref[rws, col(D_CONV, h)] = (yh * g_ref[:, col(0, h)] * (rg * _sigmoid(rg))).astype(BF16)
            if between[c] is not None:
                between[c]()
        for h in H:
            state[h] = st[h]

        @pl.when(s <= nt)
        def _():
            conv_out(slot, pf(s)).start()
            states_out(slot, pf(s)).start()

        @pl.when(s == nt + 1)
        def _():
            conv_out(1 - slot, pf(s - 1)).wait()
            states_out(1 - slot, pf(s - 1)).wait()
            tot = jnp.sum(lacc[...], axis=1, keepdims=True) * (0.5 / D_MODEL)
            loss_ref[...] = jnp.broadcast_to(tot, (1, 128))

    tile = lambda w: pl.BlockSpec((TM, w), lambda s: (pf(s), 0))
    xtile = lambda w: pl.BlockSpec((TM, w), lambda s: (xt(s), 0))
    any_spec = pl.BlockSpec(memory_space=pl.ANY)
    return pl.pallas_call(
        body, name="f2_mixer_fwd_f3_outproj_loss",
        grid=(nt + 2,),
        in_specs=[tile(N_PROJ_COLS), _resident((8, D_CONV)), _resident((1, D_RET)),
                  pl.BlockSpec((None, 8, HEAD_DIM), lambda s: (pf(s), 0, 0)),
                  _resident((TM, HEAD_DIM)), _resident((TM, HEAD_DIM)), _resident((8, HEAD_DIM)),
                  _resident((RET_HEADS, CHUNK, CHUNK)), _resident((RET_HEADS, CHUNK, HEAD_DIM)),
                  _resident((RET_HEADS, CHUNK, HEAD_DIM)), _resident((RET_HEADS, 8, HEAD_DIM)),
                  xtile(D_MODEL), _resident1((D_MODEL, D_MODEL)), _resident((1, D_MODEL)), xtile(D_MODEL)],
        out_specs=[any_spec, any_spec, xtile(D_MODEL), xtile(D_MODEL),
                   _resident((2, N_CHIPS, 128, D_MODEL)), _resident((1, D_MODEL)), _resident((1, 128))],
        out_shape=[jax.ShapeDtypeStruct((rows, D_CONV), BF16),
                   jax.ShapeDtypeStruct(((nt + 1) * NCH, RET_HEADS, HEAD_DIM, HEAD_DIM), BF16),
                   jax.ShapeDtypeStruct((seq, D_MODEL), F32),
                   jax.ShapeDtypeStruct((seq, D_MODEL), BF16),
                   jax.ShapeDtypeStruct((2, N_CHIPS, 128, D_MODEL), F32),
                   jax.ShapeDtypeStruct((1, D_MODEL), F32),
                   jax.ShapeDtypeStruct((1, 128), F32)],
        scratch_shapes=[pltpu.VMEM((RET_HEADS, HEAD_DIM, HEAD_DIM), F32), pltpu.VMEM((8, D_CONV), F32),
                        pltpu.VMEM((2, TM, D_MODEL), BF16), pltpu.VMEM((2, TM, D_CONV), BF16),
                        pltpu.VMEM((2, NCH, RET_HEADS, HEAD_DIM, HEAD_DIM), BF16),
                        pltpu.VMEM((1, D_MODEL), F32), pltpu.SemaphoreType.DMA((4,))],
        compiler_params=_cparams(("arbitrary",)),
    )(proj, conv_w8, gret, tb["tt"], tb["cr2"], tb["sr2"], tb["sgn"], tb["decay"], tb["xi"], tb["zeta"], tb["cd"],
      x, w_out, fg, target)


def _b2_b1a_call(proj, dmixed, conv_s, states, conv_w8, gret, tb, w_in_g, x, meta_tile, g1, dh2):
    nt, rows = tb["nt"], tb["rows"]
    seq = nt * TM

    def pb(r):
        return jnp.where(r == nt, nt, nt - 1 - r)

    def xprev(r):
        return jnp.clip(nt - r, 0, nt - 1)

    def body(proj_ref, dmx_ref, conv_ref, states_ref, cw_ref, g_ref, tt_ref, cr_ref, sr_ref, sgn_ref, dec_ref,
             dect_ref, xi_ref, zeta_ref, cd_ref, w_ref, x_ref, mt_ref, g1_ref, dh2_ref,
             dproj_hbm, gcw_ref, gg_ref, gx_ref, dmeta_ref, gn_ref,
             gstate, dchalo, dps, out_sems):
        r = pl.program_id(0)
        live = jnp.where(r == nt, 0.0, 1.0)
        slot = lax.rem(r, 2)
        dproj_ref = dps.at[slot]

        class to_hbm:
            def __init__(self, s, tile):
                self.copies = [pltpu.make_async_copy(dps.at[s, :, j * 1024:(j + 1) * 1024], dproj_hbm.at[tile, j],
                                                     out_sems.at[N_CHIPS * s + j]) for j in range(N_CHIPS)]

            def start(self):
                for cp in self.copies:
                    cp.start()

            def wait(self):
                for cp in self.copies:
                    cp.wait()

        @pl.when(r == 0)
        def _():
            gstate[...] = jnp.zeros_like(gstate)
            dchalo[...] = jnp.zeros_like(dchalo)
            gcw_ref[...] = jnp.zeros_like(gcw_ref)
            gg_ref[...] = jnp.zeros_like(gg_ref)
            gn_ref[...] = jnp.zeros_like(gn_ref)
            dps[...] = jnp.zeros_like(dps)

        @pl.when(r >= 2)
        def _():
            to_hbm(slot, pb(r - 2)).wait()

        dprev = dps.at[1 - slot]
        pieces = []

        def emit_piece():
            j = len(pieces)
            if j < N_CHIPS:
                p = _dot_tb(dprev[:, j * 1024:(j + 1) * 1024], w_ref[j])
                pieces.append(p if j == 0 else pieces[-1] + p)

        cx = proj_ref[:, CX:CX + 512].astype(F32)
        cb = proj_ref[:, CB:CB + 512].astype(F32)
        cc = proj_ref[:, CC:CC + 512].astype(F32)
        cg = proj_ref[:, CG:CG + 512].astype(F32)
        dco = dmx_ref[:, 0:D_CONV].astype(F32) * live
        conv = conv_ref[...].astype(F32)
        sg = _sigmoid(cg)
        sil = cg * sg
        t = dco * conv
        dproj_ref[:, CB:CB + 512] = (t * sil).astype(BF16)
        dproj_ref[:, CG:CG + 512] = (t * cb * (sg * (1.0 + cg * (1.0 - sg)))).astype(BF16)
        dconv = dco * cb * sil
        row = lax.broadcasted_iota(jnp.int32, (TM, D_CONV), 0)
        n0 = dchalo[0:1, :]
        n1 = dchalo[1:2, :]
        dc1 = jnp.where(row == TM - 1, n0, pltpu.roll(dconv, TM - 1, 0))
        dc2 = jnp.where(row == TM - 2, n0, jnp.where(row == TM - 1, n1, pltpu.roll(dconv, TM - 2, 0)))
        dchalo[...] = dconv[0:8, :]
        du = cw_ref[2:3, :] * dconv + cw_ref[1:2, :] * dc1 + cw_ref[0:1, :] * dc2
        u = cc * cx
        gcw_ref[2:3, :] += jnp.sum(u * dconv, axis=0, keepdims=True)
        gcw_ref[1:2, :] += jnp.sum(u * dc1, axis=0, keepdims=True)
        gcw_ref[0:1, :] += jnp.sum(u * dc2, axis=0, keepdims=True)
        dproj_ref[:, CC:CC + 512] = (du * cx).astype(BF16)
        dproj_ref[:, CX:CX + 512] = (du * cc).astype(BF16)
        emit_piece()

        scale = HEAD_DIM ** -0.5
        gs = {h: gstate[h] for h in range(RET_HEADS)}
        gg = {h: jnp.zeros((1, HEAD_DIM), F32) for h in range(RET_HEADS)}
        col = lambda base, h: slice(base + h * HEAD_DIM, base + (h + 1) * HEAD_DIM)
        rw = lambda c: slice(c * CHUNK, (c + 1) * CHUNK)
        rc_t, rs_t = _tile_rotary(tt_ref, cr_ref, sr_ref, sgn_ref)
        for c0 in range(NCH - CHUNK_GROUP, -1, -CHUNK_GROUP):
            cs = range(c0 + CHUNK_GROUP - 1, c0 - 1, -1)
            U = [(c, h) for c in cs for h in range(RET_HEADS)]
            rc = {c: rc_t[rw(c), :] for c in cs}
            rs = {c: rs_t[rw(c), :] for c in cs}
            v = {(c, h): proj_ref[rw(c), col(CV, h)] for c, h in U}
            stb = {(c, h): states_ref[c, h] for c, h in U}
            qf = {(c, h): _rot(proj_ref[rw(c), col(CQ, h)].astype(F32), rc[c], rs[c]) * scale for c, h in U}
            kf = {(c, h): _rot(proj_ref[rw(c), col(CK, h)].astype(F32), rc[c], rs[c]) for c, h in U}
            qb = {u: qf[u].astype(BF16) for u in U}
            kb = {u: kf[u].astype(BF16) for u in U}
            qxb = {(c, h): (qf[c, h] * xi_ref[h]).astype(BF16) for c, h in U}
            kzb = {(c, h): (kf[c, h] * zeta_ref[h]).astype(BF16) for c, h in U}
            ab = {(c, h): (_dot_tb(qb[c, h], kb[c, h]) * dec_ref[h]).astype(BF16) for c, h in U}
            atb = {(c, h): (_dot_tb(kb[c, h], qb[c, h]) * dect_ref[h]).astype(BF16) for c, h in U}
            o = {u: _dot(ab[u], v[u]) + _dot(qxb[u], stb[u]) for u in U}
            emit_piece()
            dob = {}
            for c, h in U:
                mu = jnp.mean(o[c, h], axis=-1, keepdims=True)
                d = o[c, h] - mu
                var = jnp.mean(d * d, axis=-1, keepdims=True)
                rstd = lax.rsqrt(var + EPS)
                yh = d * rstd
                g = g_ref[:, col(0, h)]
                rg = proj_ref[rw(c), col(CR, h)].astype(F32)
                dro = dmx_ref[rw(c), col(D_CONV, h)].astype(F32) * live
                sg = _sigmoid(rg)
                dproj_ref[rw(c), col(CR, h)] = (dro * (yh * g) * (sg * (1.0 + rg * (1.0 - sg)))).astype(BF16)
                dret = dro * (rg * sg)
                gg[h] = gg[h] + jnp.sum(dret * yh, axis=0, keepdims=True)
                dyh = dret * g
                do = rstd * (dyh - jnp.mean(dyh, axis=-1, keepdims=True)
                             - yh * jnp.mean(dyh * yh, axis=-1, keepdims=True))
                dob[c, h] = do.astype(BF16)
            dv1 = {u: _dot(atb[u], dob[u]) for u in U}
            ds = {(c, h): (_dot_tb(dob[c, h], v[c, h]) * dec_ref[h]).astype(BF16) for c, h in U}
            dst = {(c, h): (_dot_tb(v[c, h], dob[c, h]) * dect_ref[h]).astype(BF16) for c, h in U}
            gup = {u: _dot_ta(qxb[u], dob[u]) for u in U}
            dq = {(c, h): _dot(ds[c, h], kb[c, h]) + _dot_tb(dob[c, h], stb[c, h]) * xi_ref[h] for c, h in U}
            dk1 = {u: _dot(dst[u], qb[u]) for u in U}
            emit_piece()
            for c, h in U:
                gsb = gs[h].astype(BF16)
                dv = dv1[c, h] + _dot(kzb[c, h], gsb)
                dk = dk1[c, h] + _dot_tb(v[c, h], gsb) * zeta_ref[h]
                gs[h] = cd_ref[h, 0:1, :] * gs[h] + gup[c, h]
                dproj_ref[rw(c), col(CQ, h)] = (_rot_t(dq[c, h], rc[c], rs[c]) * scale).astype(BF16)
                dproj_ref[rw(c), col(CK, h)] = _rot_t(dk, rc[c], rs[c]).astype(BF16)
                dproj_ref[rw(c), col(CV, h)] = dv.astype(BF16)
        for h in range(RET_HEADS):
            gstate[h] = gs[h]
            gg_ref[:, col(0, h)] += gg[h]

        while len(pieces) < N_CHIPS:
            emit_piece()

        def norm_bwd(dhn, hx):
            ms = jnp.mean(hx * hx, axis=-1, keepdims=True)
            rstd1 = lax.rsqrt(ms + EPS)
            xh = hx * rstd1
            gn_ref[...] += jnp.sum(dhn * xh, axis=0, keepdims=True)
            dxh = dhn * g1_ref[...]
            return rstd1 * (dxh - xh * jnp.mean(dxh * xh, axis=-1, keepdims=True))

        gx_ref[...] = norm_bwd(pieces[-1], x_ref[...]) + dh2_ref[...]

        @pl.when(r < nt)
        def _():
            to_hbm(slot, pb(r)).start()

        @pl.when(r == nt)
        def _():
            to_hbm(slot, pb(r)).start()
            mrows = slice(TM - N_META, TM)
            d16 = dproj_ref[mrows, :]
            dhn16 = _dot_tb(d16[:, 0:1024], w_ref[0])
            for j in range(1, N_CHIPS):
                dhn16 += _dot_tb(d16[:, j * 1024:(j + 1) * 1024], w_ref[j])
            dmeta_ref[...] = norm_bwd(dhn16, mt_ref[mrows, :])
            to_hbm(1 - slot, pb(r - 1)).wait()
            to_hbm(slot, pb(r)).wait()

    tile = lambda w: pl.BlockSpec((TM, w), lambda r: (pb(r), 0))
    xtile = pl.BlockSpec((TM, D_MODEL), lambda r: (xprev(r), 0))
    return pl.pallas_call(
        body, name="b2_mixer_bwd_b1a_inproj_bwd_x",
        grid=(nt + 1,),
        in_specs=[tile(N_PROJ_COLS),
                  pl.BlockSpec((TM, D_MODEL), lambda r: (jnp.minimum(pb(r), nt - 1), 0)),
                  tile(D_CONV),
                  pl.BlockSpec((NCH, RET_HEADS, HEAD_DIM, HEAD_DIM), lambda r: (pb(r), 0, 0, 0)),
                  _resident((8, D_CONV)), _resident((1, D_RET)),
                  pl.BlockSpec((None, 8, HEAD_DIM), lambda r: (pb(r), 0, 0)),
                  _resident((TM, HEAD_DIM)), _resident((TM, HEAD_DIM)), _resident((8, HEAD_DIM)),
                  _resident((RET_HEADS, CHUNK, CHUNK)), _resident((RET_HEADS, CHUNK, CHUNK)),
                  _resident((RET_HEADS, CHUNK, HEAD_DIM)),
                  _resident((RET_HEADS, CHUNK, HEAD_DIM)), _resident((RET_HEADS, 8, HEAD_DIM)),
                  _resident1((N_CHIPS, D_MODEL, 1024)), xtile, _resident1((TM, D_MODEL)), _resident((1, D_MODEL)),
                  xtile],
        out_specs=[pl.BlockSpec(memory_space=pl.ANY), _resident((8, D_CONV)), _resident((1, D_RET)),
                   xtile, _resident((N_META, D_MODEL)), _resident((1, D_MODEL))],
        out_shape=[jax.ShapeDtypeStruct((nt + 1, N_CHIPS, TM, 1024), BF16),
                   jax.ShapeDtypeStruct((8, D_CONV), F32),
                   jax.ShapeDtypeStruct((1, D_RET), F32),
                   jax.ShapeDtypeStruct((seq, D_MODEL), F32),
                   jax.ShapeDtypeStruct((N_META, D_MODEL), F32),
                   jax.ShapeDtypeStruct((1, D_MODEL), F32)],
        scratch_shapes=[pltpu.VMEM((RET_HEADS, HEAD_DIM, HEAD_DIM), F32), pltpu.VMEM((8, D_CONV), F32),
                        pltpu.VMEM((2, TM, N_PROJ_COLS), BF16), pltpu.SemaphoreType.DMA((2 * N_CHIPS,))],
        compiler_params=_cparams(("arbitrary",), vmem=VMEM_LIMIT_MAX),
    )(proj, dmixed, conv_s, states, conv_w8, gret, tb["tt"], tb["cr2"], tb["sr2"], tb["sgn"], tb["decay"],
      tb["decay_t"], tb["xi"], tb["zeta"], tb["cd"], w_in_g, x, meta_tile, g1, dh2)


REL = (2, 1, 3)
SMALL_ROWS = 24
HALF_STEP = 4


def _rcopy(src, dst, send_sems, recv_sems, k, to):
    return pltpu.make_async_remote_copy(src_ref=src, dst_ref=dst, send_sem=send_sems.at[k],
                                        recv_sem=recv_sems.at[k], device_id=to, device_id_type=MESH_ID)


def _b1b_reduce_call(order, hnt, dproj, gwo, pack, nt):
    nk = nt + 1
    last = nk - 1
    any_spec = pl.BlockSpec(memory_space=pl.ANY)

    def body(order_ref, a_ref, b_ref, gwo_hbm, pack_hbm, gwin_hbm, gwout_hbm, tot_hbm,
             acc, sb, abuf, pb, bbuf, fin, go, ao, pbo, bo, fino, slots, totv, send_sems, recv_sems, loc_sems):
        jj, k = pl.program_id(0), pl.program_id(1)
        x, y, c = lax.axis_index("x"), lax.axis_index("y"), lax.axis_index("c")
        me, myid, sib = 2 * x + y, 4 * x + 2 * y + c, (x, y, 1 - c)
        rc = functools.partial(_rcopy, send_sems=send_sems, recv_sems=recv_sems)
        peers = [((1 - x) if r & 2 else x, (1 - y) if r & 1 else y, c) for r in REL]
        kids = [jnp.bitwise_xor(me, r) for r in REL]

        def dev_peer(r):
            return ((1 - x) if r & 4 else x, (1 - y) if r & 2 else y, (1 - c) if r & 1 else c)

        own_go = pltpu.make_async_copy(gwo_hbm.at[c], go, loc_sems.at[0])
        own_pack = pltpu.make_async_copy(pack_hbm, slots.at[0], loc_sems.at[1])
        wo_half = rc(gwo_hbm.at[1 - c], ao, k=8, to=sib)
        wo_part = [rc(pbo.at[kids[p]], bo.at[p], k=9 + p, to=peers[p]) for p in range(3)]
        sm = [rc(pack_hbm, slots.at[r], k=12 + r, to=dev_peer(r)) for r in range(1, N_DEV)]
        half = [rc(sb.at[j % 2, 1 - c], abuf.at[j], k=j, to=sib) for j in range(N_CHIPS)]
        part = [rc(pb.at[p], bbuf.at[p], k=4 + p, to=peers[p]) for p in range(3)]

        @pl.when(jnp.logical_and(jj == 0, k == 0))
        def _():
            own_go.start()
            own_pack.start()
            wo_half.start()
            for cp in sm:
                cp.start()

        @pl.when(k == 0)
        def _():
            acc[...] = jnp.zeros_like(acc)

        acc[0] += _dot(a_ref[0:512, :], b_ref[...])
        acc[1] += _dot(a_ref[512:1024, :], b_ref[...])

        @pl.when(k == HALF_STEP)
        def _():
            @pl.when(jj == 0)
            def _():
                own_go.wait()
                wo_half.wait_recv()
                for j in range(N_CHIPS):
                    go[j] = go[j] + ao[j]
                pbo[...] = go[...].astype(BF16)
                for cp in wo_part:
                    cp.start()

            for p in range(3):
                @pl.when(jj == p + 1)
                def _(p=p):
                    half[p].wait_recv()
                    half[p].wait_send()
                    pb[p] = (sb[p % 2, c] + abuf[p]).astype(BF16)
                    part[p].start()

        @pl.when(k == last)
        def _():
            for j in range(N_CHIPS):
                @pl.when(jj == j)
                def _(j=j):
                    sb[j % 2] = acc[...]
                    half[j].start()

        @pl.when(jnp.logical_and(jj == N_CHIPS - 1, k == last))
        def _():
            half[3].wait_recv()
            own = sb[1, c] + abuf[3]
            for cp in part:
                cp.wait_recv()
            fin[c] = ((own + bbuf[0].astype(F32)) + bbuf[1].astype(F32)) + bbuf[2].astype(F32)
            done = rc(fin.at[c], fin.at[c], k=7, to=sib)
            done.start()
            for cp in wo_part:
                cp.wait_recv()
            fino[c] = ((go[me] + bo[0].astype(F32)) + bo[1].astype(F32)) + bo[2].astype(F32)
            done_o = rc(fino.at[c], fino.at[c], k=12, to=sib)
            done_o.start()
            own_pack.wait()
            for cp in sm:
                cp.wait_recv()
            tot = slots[myid]
            for a in range(1, N_DEV):
                tot = tot + slots[jnp.bitwise_xor(myid, a)]
            totv[...] = tot
            out_t = pltpu.make_async_copy(totv, tot_hbm, loc_sems.at[1])
            out_t.start()
            rc(fin.at[1 - c], fin.at[1 - c], k=7, to=sib).wait_recv()
            out_w = pltpu.make_async_copy(fin, gwin_hbm, loc_sems.at[0])
            out_w.start()
            rc(fino.at[1 - c], fino.at[1 - c], k=12, to=sib).wait_recv()
            out_o = pltpu.make_async_copy(fino, gwout_hbm, loc_sems.at[2])
            out_o.start()
            for cp in [half[3]] + part + [done, wo_half] + wo_part + [done_o] + sm:
                cp.wait_send()
            out_t.wait()
            out_w.wait()
            out_o.wait()

    grid_spec = pltpu.PrefetchScalarGridSpec(
        num_scalar_prefetch=1,
        grid=(N_CHIPS, nk),
        in_specs=[pl.BlockSpec((None, D_MODEL, TM), lambda j, k, o: (k, 0, 0)),
                  pl.BlockSpec((None, None, TM, 1024), lambda j, k, o: (k, o[j], 0, 0)),
                  any_spec, any_spec],
        out_specs=[any_spec, any_spec, any_spec],
        scratch_shapes=[
            pltpu.VMEM((2, 512, 1024), F32),
            pltpu.VMEM((2, 2, 512, 1024), F32),
            pltpu.VMEM((N_CHIPS, 512, 1024), F32),
            pltpu.VMEM((3, 512, 1024), BF16),
            pltpu.VMEM((3, 512, 1024), BF16),
            pltpu.VMEM((2, 512, 1024), F32),
            pltpu.VMEM((N_CHIPS, 128, D_MODEL), F32),
            pltpu.VMEM((N_CHIPS, 128, D_MODEL), F32),
            pltpu.VMEM((N_CHIPS, 128, D_MODEL), BF16),
            pltpu.VMEM((3, 128, D_MODEL), BF16),
            pltpu.VMEM((2, 128, D_MODEL), F32),
            pltpu.VMEM((N_DEV, SMALL_ROWS, D_MODEL), F32),
            pltpu.VMEM((SMALL_ROWS, D_MODEL), F32),
            pltpu.SemaphoreType.DMA((20,)), pltpu.SemaphoreType.DMA((20,)), pltpu.SemaphoreType.DMA((3,))])
    return pl.pallas_call(
        body, name="b1b_inproj_bwd_w_reduce",
        grid_spec=grid_spec,
        out_shape=[jax.ShapeDtypeStruct((2, 512, 1024), F32),
                   jax.ShapeDtypeStruct((2, 128, D_MODEL), F32),
                   jax.ShapeDtypeStruct((SMALL_ROWS, D_MODEL), F32)],
        compiler_params=_cparams(("arbitrary", "arbitrary")),
    )(order, hnt, dproj, gwo, pack)


def _local_step(me, x, target, g1, gret, fg, win_sh, wout_sh, small_sh):
    seq = x.shape[0]
    tb = _tables(seq)
    nt = tb["nt"]
    g1r, gretr, fgr = g1.reshape(1, -1), gret.reshape(1, -1), fg.reshape(1, -1)
    order = jnp.stack([me, me ^ REL[0], me ^ REL[1], me ^ REL[2]]).astype(jnp.int32)

    proj, hnt, w_in_g, w_out_g, meta_tile, conv_w8 = _f1_gather_call(order, x, g1r, win_sh, wout_sh, small_sh, nt)
    w_out = w_out_g.reshape(D_MODEL, D_MODEL)
    conv_s, states, dh2, dmixed, g_wout, g_fg, loss = _f2_f3_call(proj, conv_w8, gretr, tb, x, w_out, fgr, target)
    dproj, g_cw8, g_gret, grad_x, g_meta, g_g1 = _b2_b1a_call(proj, dmixed, conv_s, states, conv_w8, gretr, tb,
                                                              w_in_g, x, meta_tile, g1r, dh2)
    return loss, grad_x, dict(w_out=g_wout, meta=g_meta, conv_w=g_cw8[0:3], norm1_g=g_g1,
                              ret_norm_g=g_gret, final_g=g_fg), hnt, dproj


def _adamw_update(w_ref, g_ref, m_ref, v_ref, d_ref, nm_ref, nv_ref):
    gg = g_ref[...]
    nm = ADAM_B1 * m_ref[...] + (1.0 - ADAM_B1) * gg
    nv = ADAM_B2 * v_ref[...] + (1.0 - ADAM_B2) * (gg * gg)
    m_hat = nm / (1.0 - ADAM_B1 ** ADAM_STEP)
    v_hat = nv / (1.0 - ADAM_B2 ** ADAM_STEP)
    d_ref[...] = -ADAM_LR * (m_hat / (jnp.sqrt(v_hat) + ADAM_EPS) + ADAM_WD * w_ref[...])
    nm_ref[...] = nm
    nv_ref[...] = nv


def _adamw_small_call(ws, gs, ms, vs):
    n = len(ws)

    def body(*refs):
        ins, outs = refs[:4 * n], refs[4 * n:]
        for i in range(n):
            _adamw_update(ins[i], ins[n + i], ins[2 * n + i], ins[3 * n + i],
                          outs[i], outs[n + i], outs[2 * n + i])

    shapes = [jax.ShapeDtypeStruct(w.shape, F32) for w in ws]
    outs = pl.pallas_call(body, name="adamw_small", out_shape=shapes * 3,
                          compiler_params=_cparams())(*ws, *gs, *ms, *vs)
    return outs[:n], outs[n:2 * n], outs[2 * n:]


def _adamw_call(w, g, m, v, name):
    shape = w.shape
    w2, g2, m2, v2 = (a.reshape(-1, shape[-1]) for a in (w, g, m, v))
    rows, cols = w2.shape
    br = 256 if rows % 256 == 0 else rows
    body = functools.partial(_adamw_update)
    spec = pl.BlockSpec((br, cols), lambda i: (i, 0))
    outs = pl.pallas_call(
        body, name=name, grid=(rows // br,),
        in_specs=[spec] * 4, out_specs=[spec] * 3,
        out_shape=[jax.ShapeDtypeStruct((rows, cols), F32)] * 3,
        compiler_params=_cparams(("arbitrary",)),
    )(w2, g2, m2, v2)
    return tuple(o.reshape(shape) for o in outs)


def _pad_to(a, rows, cols):
    return jnp.pad(a, ((0, rows - a.shape[0]), (0, cols - a.shape[1])))


def kernel(x, meta, norm1_g, w_in, conv_w, ret_norm_g, w_out, final_g, loss_target, m_meta, m_norm1_g, m_w_in, m_conv_w, m_ret_norm_g, m_w_out, m_final_g, v_meta, v_norm1_g, v_w_in, v_conv_w, v_ret_norm_g, v_w_out, v_final_g):
    me = 2 * lax.axis_index("x") + lax.axis_index("y")

    small_sh = jnp.concatenate([meta, _pad_to(conv_w, 8, 256)], axis=0)
    loss, grad_x, g, hnt, dproj = _local_step(me, x[0], loss_target[0], norm1_g, ret_norm_g, final_g,
                                              w_in.astype(BF16), w_out.astype(BF16), small_sh)

    vec = jnp.concatenate([g["norm1_g"], g["final_g"], _pad_to(g["ret_norm_g"], 1, D_MODEL),
                           _pad_to(g["conv_w"], 3, D_MODEL), _pad_to(loss, 2, D_MODEL)], axis=0)
    pack = jnp.concatenate([g["meta"], vec], axis=0)
    order = jnp.stack([me ^ REL[0], me ^ REL[1], me ^ REL[2], me]).astype(jnp.int32)
    g_win, g_wout, tot = _b1b_reduce_call(order, hnt, dproj, g["w_out"], pack, x.shape[1] // TM)
    g_win, g_wout = g_win.reshape(D_MODEL, 1024), g_wout.reshape(256, D_MODEL)
    g_meta = lax.dynamic_slice(tot, (0, me * 256), (N_META, 256))
    g_conv = lax.dynamic_slice(tot, (N_META + 3, me * 128), (3, 128))
    g_n1, g_fg, g_rn = tot[N_META], tot[N_META + 1], tot[N_META + 2, 0:D_RET]

    loss_tot = tot[N_META + 6, 0]

    grads = [g_meta, g_n1, g_win, g_conv, g_rn, g_wout, g_fg]
    ws = [meta, norm1_g, w_in, conv_w, ret_norm_g, w_out, final_g]
    ms = [m_meta, m_norm1_g, m_w_in, m_conv_w, m_ret_norm_g, m_w_out, m_final_g]
    vs = [v_meta, v_norm1_g, v_w_in, v_conv_w, v_ret_norm_g, v_w_out, v_final_g]
    names = ["meta", "norm1_g", "w_in", "conv_w", "ret_norm_g", "w_out", "final_g"]
    as2d = lambda a: a.reshape(1, -1) if a.ndim == 1 else a
    big = [i for i, n_ in enumerate(names) if n_ in ("w_in", "w_out")]
    small = [i for i in range(len(names)) if i not in big]
    deltas, new_ms, new_vs = [None] * 7, [None] * 7, [None] * 7
    for i in big:
        deltas[i], new_ms[i], new_vs[i] = _adamw_call(ws[i], grads[i], ms[i], vs[i], "adamw_" + names[i])
    sd, sm_, sv = _adamw_small_call(*[[as2d(t[i]) for i in small] for t in (ws, grads, ms, vs)])
    for j, i in enumerate(small):
        deltas[i], new_ms[i], new_vs[i] = (o[j].reshape(ws[i].shape) for o in (sd, sm_, sv))
    return (loss_tot, grad_x[None], *grads, *deltas, *new_ms, *new_vs)
```

```python
import functools

import jax
import jax.numpy as jnp
from jax import lax
from jax.experimental import pallas as pl
from jax.experimental.pallas import tpu as pltpu

F32 = jnp.float32
BF16 = jnp.bfloat16

D_MODEL = 1024
N_META = 16
D_CONV = 512
D_RET = 512
RET_HEADS = 4
HEAD_DIM = 128
CHUNK = 128
N_PROJ_COLS = 4096
ROPE_BASE = 10000.0
EPS = 1e-6
N_CHIPS = 4
N_DEV = 8

ADAM_LR = 0.001
ADAM_B1 = 0.9
ADAM_B2 = 0.999
ADAM_EPS = 1e-08
ADAM_WD = 0.01
ADAM_STEP = 10

TM = 512
NCH = TM // CHUNK
CHUNK_GROUP = 2
VMEM_LIMIT = 56 * 1024 * 1024
VMEM_LIMIT_MAX = 63 * 1024 * 1024

CX, CB, CC, CG, CQ, CK, CV, CR = (i * 512 for i in range(8))

MESH_ID = pl.DeviceIdType.MESH


def _cparams(sem=None, vmem=VMEM_LIMIT, **kw):
    return pltpu.CompilerParams(dimension_semantics=sem, vmem_limit_bytes=vmem, **kw)


def _sigmoid(x):
    return 1.0 / (1.0 + jnp.exp(-x))


def _dot(a, b):
    return jnp.dot(a, b, preferred_element_type=F32)


def _dot_tb(a, b):
    return lax.dot_general(a, b, (((1,), (1,)), ((), ())), preferred_element_type=F32)


def _dot_ta(a, b):
    return lax.dot_general(a, b, (((0,), (0,)), ((), ())), preferred_element_type=F32)


def _resident(shape):
    nd = len(shape)
    return pl.BlockSpec(shape, lambda *_: (0,) * nd)


def _resident1(shape):
    nd = len(shape)
    return pl.BlockSpec(shape, lambda *_: (0,) * nd, pipeline_mode=pl.Buffered(1))


def _tables(seq):
    nt = seq // TM
    rows = seq + TM
    half = HEAD_DIM // 2
    freqs = 1.0 / (ROPE_BASE ** (jnp.arange(half, dtype=F32) / half))
    tile_start = jnp.concatenate([jnp.arange(nt, dtype=F32), -jnp.ones((1,), F32)]) * TM
    ang_t = tile_start[:, None] * freqs[None, :]
    ang_r = (jnp.arange(TM, dtype=F32) + N_META)[:, None] * freqs[None, :]
    dup = lambda a: jnp.concatenate([a, a], axis=-1)
    tt = jnp.stack([dup(jnp.cos(ang_t)), dup(jnp.sin(ang_t))], axis=1)
    tt = jnp.pad(tt, ((0, 0), (0, 6), (0, 0)))
    cr2, sr2 = dup(jnp.cos(ang_r)), dup(jnp.sin(ang_r))
    sgn = jnp.concatenate([-jnp.ones((8, half), F32), jnp.ones((8, half), F32)], axis=-1)
    log_g = jnp.log(1.0 - 2.0 ** (-5.0 - jnp.arange(RET_HEADS, dtype=F32)))
    idx = jnp.arange(CHUNK, dtype=F32)
    diff = idx[:, None] - idx[None, :]
    decay = jnp.where(diff[None] >= 0, jnp.exp(diff[None] * log_g[:, None, None]), 0.0)
    zeta = jnp.exp((CHUNK - 1 - idx)[None, :] * log_g[:, None])
    xi = jnp.exp((idx + 1.0)[None, :] * log_g[:, None])
    cd = jnp.exp(CHUNK * log_g)
    zeta_b = jnp.broadcast_to(zeta[:, :, None], (RET_HEADS, CHUNK, HEAD_DIM))
    xi_b = jnp.broadcast_to(xi[:, :, None], (RET_HEADS, CHUNK, HEAD_DIM))
    cd_b = jnp.broadcast_to(cd[:, None, None], (RET_HEADS, 8, HEAD_DIM))
    return dict(nt=nt, rows=rows, tt=tt, cr2=cr2, sr2=sr2, sgn=sgn, decay=decay, zeta=zeta_b, xi=xi_b, cd=cd_b)


def _tile_rotary(tt_ref, cr_ref, sr_ref, sgn_ref):
    ct, st = tt_ref[0:1, :], tt_ref[1:2, :]
    cr, sr = cr_ref[...], sr_ref[...]
    return ct * cr - st * sr, (st * cr + ct * sr) * sgn_ref[0:1, :]


def _rot(t, rc, rs):
    return t * rc + pltpu.roll(t, HEAD_DIM // 2, 1) * rs


def _rot_t(dt, rc, rs):
    return dt * rc + pltpu.roll(dt * rs, HEAD_DIM // 2, 1)


def _f1_gather_call(order, x, g1, win_sh, wout_sh, small_sh, nt):
    nk = nt + 1
    rows = nk * TM
    any_spec = pl.BlockSpec(memory_space=pl.ANY)

    def body(order_ref, x_ref, g_ref, win_hbm, wout_hbm, sm_hbm,
             proj_ref, hnt_ref, wg_hbm, wog_hbm, mt_hbm, cw_hbm,
             wg, wog, smg, mt, cw, hbs, send_sems, recv_sems, loc_sems):
        jj, k = pl.program_id(0), pl.program_id(1)
        x, y, c = lax.axis_index("x"), lax.axis_index("y"), lax.axis_index("c")
        me, sib = 2 * x + y, (x, y, 1 - c)
        rc = functools.partial(_rcopy, send_sems=send_sems, recv_sems=recv_sems)
        peers = [((1 - x) if r & 2 else x, (1 - y) if r & 1 else y, c) for r in REL]
        kids = [jnp.bitwise_xor(me, r) for r in REL]
        hw, ho = pl.ds(c * 512, 512), pl.ds(c * 128, 128)
        hw2, ho2 = pl.ds((1 - c) * 512, 512), pl.ds((1 - c) * 128, 128)
        at = lambda j_, k_: jnp.logical_and(jj == j_, k == k_)

        sm_cp = [rc(sm_hbm, smg.at[me], k=p, to=peers[p]) for p in range(3)]
        win_cp = [rc(win_hbm.at[hw], wg.at[me, hw], k=3 + p, to=peers[p]) for p in range(3)]
        wout_cp = [rc(wout_hbm.at[ho], wog.at[me, ho], k=6 + p, to=peers[p]) for p in range(3)]
        sm_in = [rc(sm_hbm, smg.at[kids[p]], k=p, to=sib) for p in range(3)]
        win_in = [rc(win_hbm.at[hw], wg.at[kids[p], hw], k=3 + p, to=sib) for p in range(3)]
        wout_in = [rc(wout_hbm.at[ho], wog.at[kids[p], ho], k=6 + p, to=sib) for p in range(3)]
        win_fw = [rc(wg.at[kids[p], hw], wg.at[kids[p], hw], k=9 + p, to=sib) for p in range(3)]
        wout_fw = [rc(wog.at[kids[p], ho], wog.at[kids[p], ho], k=12 + p, to=sib) for p in range(3)]
        win_fw_in = [rc(wg.at[kids[p], hw2], wg.at[kids[p], hw2], k=9 + p, to=sib) for p in range(3)]
        wout_fw_in = [rc(wog.at[kids[p], ho2], wog.at[kids[p], ho2], k=12 + p, to=sib) for p in range(3)]
        own_w = pltpu.make_async_copy(win_hbm, wg.at[me], loc_sems.at[0])
        own_o = pltpu.make_async_copy(wout_hbm, wog.at[me], loc_sems.at[1])
        own_s = pltpu.make_async_copy(sm_hbm, smg.at[me], loc_sems.at[2])
        out_wg = pltpu.make_async_copy(wg, wg_hbm, loc_sems.at[3])
        out_wog = pltpu.make_async_copy(wog, wog_hbm, loc_sems.at[4])
        out_mt = pltpu.make_async_copy(mt, mt_hbm, loc_sems.at[5])
        out_cw = pltpu.make_async_copy(cw, cw_hbm, loc_sems.at[6])

        def pass_on(p):
            win_in[p].wait_recv()
            win_fw[p].start()

        @pl.when(k <= 1)
        def _():
            @pl.when(at(0, 0))
            def _():
                own_w.start()
                own_s.start()
                own_o.start()
                for cp in sm_cp + win_cp + wout_cp:
                    cp.start()
                own_w.wait()

            for p in range(3):
                @pl.when(at(p + 1, 0))
                def _(p=p):
                    win_fw_in[p].wait_recv()

            @pl.when(at(1, 1))
            def _():
                pass_on(1)

            @pl.when(at(3, 0))
            def _():
                out_wg.start()

            @pl.when(at(3, 1))
            def _():
                for p in range(3):
                    wout_in[p].wait_recv()
                    wout_fw[p].start()

        @pl.when(jnp.logical_and(jj == 0, k >= nk - 2))
        def _():
            @pl.when(k == nk - 2)
            def _():
                own_s.wait()
                for cp in sm_in:
                    cp.wait_recv()
                mt[...] = jnp.zeros_like(mt)
                cw[...] = jnp.zeros_like(cw)
                for j in range(N_CHIPS):
                    mt[TM - N_META:TM, j * 256:(j + 1) * 256] = smg[j, 0:N_META, :]
                    cw[0:3, j * 128:(j + 1) * 128] = smg[j, N_META:N_META + 3, 0:128]
                out_mt.start()
                out_cw.start()

            @pl.when(k == nk - 1)
            def _():
                pass_on(0)

        @pl.when(at(2, nk // 2))
        def _():
            pass_on(2)

        tile_rows = pl.ds(pl.multiple_of(k * TM, TM), TM)

        @pl.when(jj == 0)
        def _():
            h = jnp.where(k == nt, mt[...], x_ref[...])
            ms = jnp.mean(h * h, axis=-1, keepdims=True)
            hn = (h * lax.rsqrt(ms + EPS)) * g_ref[...]
            hb = hn.astype(BF16)
            hbs[tile_rows, :] = hb
            proj_ref[...] = _dot(hb, wg[order_ref[0]]).astype(BF16)
            hnt_ref[...] = hn.T.astype(BF16)

        @pl.when(jj > 0)
        def _():
            proj_ref[...] = _dot(hbs[tile_rows, :], wg[order_ref[jj]]).astype(BF16)

        @pl.when(at(3, nk - 1))
        def _():
            own_o.wait()
            for cp in wout_fw_in:
                cp.wait_recv()
            out_wog.start()
            for cp in sm_cp + win_cp + wout_cp + win_fw + wout_fw:
                cp.wait_send()
            for cp in (out_wg, out_wog, out_mt, out_cw):
                cp.wait()

    grid_spec = pltpu.PrefetchScalarGridSpec(
        num_scalar_prefetch=1,
        grid=(N_CHIPS, nk),
        in_specs=[pl.BlockSpec((TM, D_MODEL), lambda j, k, o: (jnp.where(j == 0, jnp.minimum(k, nt - 1), nt - 1), 0)),
                  pl.BlockSpec((1, D_MODEL), lambda j, k, o: (0, 0)),
                  any_spec, any_spec, any_spec],
        out_specs=[pl.BlockSpec((TM, 1024), lambda j, k, o: (k, o[j])),
                   pl.BlockSpec((None, D_MODEL, TM), lambda j, k, o: (jnp.where(j == 0, k, nk - 1), 0, 0)),
                   any_spec, any_spec, any_spec, any_spec],
        scratch_shapes=[
            pltpu.VMEM((N_CHIPS, D_MODEL, 1024), BF16),
            pltpu.VMEM((N_CHIPS, 256, D_MODEL), BF16),
            pltpu.VMEM((N_CHIPS, SMALL_ROWS, 256), F32),
            pltpu.VMEM((TM, D_MODEL), F32),
            pltpu.VMEM((8, D_CONV), F32),
            pltpu.VMEM((rows, D_MODEL), BF16),
            pltpu.SemaphoreType.DMA((15,)), pltpu.SemaphoreType.DMA((15,)), pltpu.SemaphoreType.DMA((7,))])
    return pl.pallas_call(
        body, name="f1_norm_inproj_gather",
        grid_spec=grid_spec,
        out_shape=[jax.ShapeDtypeStruct((rows, N_PROJ_COLS), BF16),
                   jax.ShapeDtypeStruct((nk, D_MODEL, TM), BF16),
                   jax.ShapeDtypeStruct((N_CHIPS, D_MODEL, 1024), BF16),
                   jax.ShapeDtypeStruct((N_CHIPS, 256, D_MODEL), BF16),
                   jax.ShapeDtypeStruct((TM, D_MODEL), F32),
                   jax.ShapeDtypeStruct((8, D_CONV), F32)],
        compiler_params=_cparams(("arbitrary", "arbitrary")),
    )(order, x, g1, win_sh, wout_sh, small_sh)


def _f2_f3_call(proj, conv_w8, gret, tb, x, w_out, fg, target):
    nt, rows = tb["nt"], tb["rows"]
    seq = nt * TM

    def pf(s):
        return jnp.where(s == 0, nt, jnp.minimum(s - 1, nt - 1))

    def xt(s):
        return jnp.clip(s - 2, 0, nt - 1)

    def body(proj_ref, cw_ref, g_ref, tt_ref, cr_ref, sr_ref, sgn_ref, dec_ref, xi_ref, zeta_ref, cd_ref,
             x_ref, w_ref, fg_ref, t_ref,
             conv_hbm, states_hbm, dh2_ref, dmx_ref, gwo_ref, gfg_ref, loss_ref,
             state, uhalo, mxs, convs, sts, lacc, out_sems):
        s = pl.program_id(0)
        slot = lax.rem(s, 2)
        mixed_ref = mxs.at[slot]
        conv_ref = convs.at[slot]
        states_ref = sts.at[slot]

        def conv_out(sl, tile):
            return pltpu.make_async_copy(convs.at[sl], conv_hbm.at[pl.ds(pl.multiple_of(tile * TM, TM), TM), :],
                                         out_sems.at[sl])

        def states_out(sl, tile):
            return pltpu.make_async_copy(sts.at[sl], states_hbm.at[pl.ds(pl.multiple_of(tile * NCH, NCH), NCH)],
                                         out_sems.at[2 + sl])

        @pl.when(s == 0)
        def _():
            state[...] = jnp.zeros_like(state)
            uhalo[...] = jnp.zeros_like(uhalo)
            mxs[...] = jnp.zeros_like(mxs)
            gwo_ref[...] = jnp.zeros_like(gwo_ref)
            gfg_ref[...] = jnp.zeros_like(gfg_ref)
            lacc[...] = jnp.zeros_like(lacc)

        @pl.when(s >= 2)
        def _():
            conv_out(slot, pf(s - 2)).wait()
            states_out(slot, pf(s - 2)).wait()

        valid = jnp.where(s >= 2, 1.0, 0.0)
        mx_prev = mxs.at[1 - slot]
        f3 = {}

        def f3_fwd():
            f3["h2"] = x_ref[...] + _dot(mx_prev[...], w_ref[...])

        def f3_loss():
            h2 = f3.pop("h2")
            ms = jnp.mean(h2 * h2, axis=-1, keepdims=True)
            rstd = lax.rsqrt(ms + EPS)
            yh = h2 * rstd
            g = fg_ref[...]
            e = (yh * g - t_ref[...]) * valid
            lacc[...] += jnp.sum(e * e, axis=0, keepdims=True)
            dy = e * (1.0 / D_MODEL)
            gfg_ref[...] += jnp.sum(dy * yh, axis=0, keepdims=True)
            dyh = dy * g
            dh2 = rstd * (dyh - yh * jnp.mean(dyh * yh, axis=-1, keepdims=True))
            dh2_ref[...] = dh2
            f3["db"] = dh2.astype(BF16)

        def f3_dmx():
            dmx_ref[...] = _dot_tb(f3["db"], w_ref[...]).astype(BF16)

        def f3_gw():
            gw = _dot_ta(mx_prev[...], f3["db"])
            for j in range(N_CHIPS):
                for hf in range(2):
                    r0 = j * 256 + hf * 128
                    gwo_ref[hf, j] += gw[r0:r0 + 128, :]

        cx = proj_ref[:, CX:CX + 512].astype(F32)
        cc = proj_ref[:, CC:CC + 512].astype(F32)
        u = cc * cx
        row = lax.broadcasted_iota(jnp.int32, (TM, D_CONV), 0)
        h7 = uhalo[7:8, :]
        h6 = uhalo[6:7, :]
        u1 = jnp.where(row == 0, h7, pltpu.roll(u, 1, 0))
        u2 = jnp.where(row == 0, h6, jnp.where(row == 1, h7, pltpu.roll(u, 2, 0)))
        conv = cw_ref[2:3, :] * u + cw_ref[1:2, :] * u1 + cw_ref[0:1, :] * u2
        uhalo[...] = u[TM - 8:TM, :]
        cb = proj_ref[:, CB:CB + 512].astype(F32)
        cg = proj_ref[:, CG:CG + 512].astype(F32)
        mixed_ref[:, 0:D_CONV] = (cb * conv * (cg * _sigmoid(cg))).astype(BF16)
        conv_ref[...] = conv.astype(BF16)
        f3_fwd()

        scale = HEAD_DIM ** -0.5
        H = range(RET_HEADS)
        st = [state[h] for h in H]
        between = [f3_loss, f3_dmx, f3_gw, None]
        rc_t, rs_t = _tile_rotary(tt_ref, cr_ref, sr_ref, sgn_ref)
        for c in range(NCH):
            r0 = c * CHUNK
            rc = rc_t[r0:r0 + CHUNK, :]
            rs = rs_t[r0:r0 + CHUNK, :]
            col = lambda base, h: slice(base + h * HEAD_DIM, base + (h + 1) * HEAD_DIM)
            rws = slice(r0, r0 + CHUNK)
            v = [proj_ref[rws, col(CV, h)] for h in H]
            qf = [_rot(proj_ref[rws, col(CQ, h)].astype(F32), rc, rs) * scale for h in H]
            kf = [_rot(proj_ref[rws, col(CK, h)].astype(F32), rc, rs) for h in H]
            stb = [t.astype(BF16) for t in st]
            for h in H:
                states_ref[c, h] = stb[h]
            a = [(_dot_tb(qf[h].astype(BF16), kf[h].astype(BF16)) * dec_ref[h]).astype(BF16) for h in H]
            o = [_dot(a[h], v[h]) + _dot((qf[h] * xi_ref[h]).astype(BF16), stb[h]) for h in H]
            st = [cd_ref[h, 0:1, :] * st[h] + _dot_ta((kf[h] * zeta_ref[h]).astype(BF16), v[h]) for h in H]
            for h in H:
                mu = jnp.mean(o[h], axis=-1, keepdims=True)
                d = o[h] - mu
                var = jnp.mean(d * d, axis=-1, keepdims=True)
                yh = d * lax.rsqrt(var + EPS)
                rg = proj_ref[rws, col(CR, h)].astype(F32)
                mixed_ref[rws, col(D_CONV, h)] = (yh * g_ref[:, col(0, h)] * (rg * _sigmoid(rg))).astype(BF16)
            if between[c] is not None:
                between[c]()
        for h in H:
            state[h] = st[h]

        @pl.when(s <= nt)
        def _():
            conv_out(slot, pf(s)).start()
            states_out(slot, pf(s)).start()

        @pl.when(s == nt + 1)
        def _():
            conv_out(1 - slot, pf(s - 1)).wait()
            states_out(1 - slot, pf(s - 1)).wait()
            tot = jnp.sum(lacc[...], axis=1, keepdims=True) * (0.5 / D_MODEL)
            loss_ref[...] = jnp.broadcast_to(tot, (1, 128))

    tile = lambda w: pl.BlockSpec((TM, w), lambda s: (pf(s), 0))
    xtile = lambda w: pl.BlockSpec((TM, w), lambda s: (xt(s), 0))
    any_spec = pl.BlockSpec(memory_space=pl.ANY)
    return pl.pallas_call(
        body, name="f2_mixer_fwd_f3_outproj_loss",
        grid=(nt + 2,),
        in_specs=[tile(N_PROJ_COLS), _resident((8, D_CONV)), _resident((1, D_RET)),
                  pl.BlockSpec((None, 8, HEAD_DIM), lambda s: (pf(s), 0, 0)),
                  _resident((TM, HEAD_DIM)), _resident((TM, HEAD_DIM)), _resident((8, HEAD_DIM)),
                  _resident((RET_HEADS, CHUNK, CHUNK)), _resident((RET_HEADS, CHUNK, HEAD_DIM)),
                  _resident((RET_HEADS, CHUNK, HEAD_DIM)), _resident((RET_HEADS, 8, HEAD_DIM)),
                  xtile(D_MODEL), _resident1((D_MODEL, D_MODEL)), _resident((1, D_MODEL)), xtile(D_MODEL)],
        out_specs=[any_spec, any_spec, xtile(D_MODEL), xtile(D_MODEL),
                   _resident((2, N_CHIPS, 128, D_MODEL)), _resident((1, D_MODEL)), _resident((1, 128))],
        out_shape=[jax.ShapeDtypeStruct((rows, D_CONV), BF16),
                   jax.ShapeDtypeStruct(((nt + 1) * NCH, RET_HEADS, HEAD_DIM, HEAD_DIM), BF16),
                   jax.ShapeDtypeStruct((seq, D_MODEL), F32),
                   jax.ShapeDtypeStruct((seq, D_MODEL), BF16),
                   jax.ShapeDtypeStruct((2, N_CHIPS, 128, D_MODEL), F32),
                   jax.ShapeDtypeStruct((1, D_MODEL), F32),
                   jax.ShapeDtypeStruct((1, 128), F32)],
        scratch_shapes=[pltpu.VMEM((RET_HEADS, HEAD_DIM, HEAD_DIM), F32), pltpu.VMEM((8, D_CONV), F32),
                        pltpu.VMEM((2, TM, D_MODEL), BF16), pltpu.VMEM((2, TM, D_CONV), BF16),
                        pltpu.VMEM((2, NCH, RET_HEADS, HEAD_DIM, HEAD_DIM), BF16),
                        pltpu.VMEM((1, D_MODEL), F32), pltpu.SemaphoreType.DMA((4,))],
        compiler_params=_cparams(("arbitrary",)),
    )(proj, conv_w8, gret, tb["tt"], tb["cr2"], tb["sr2"], tb["sgn"], tb["decay"], tb["xi"], tb["zeta"], tb["cd"],
      x, w_out, fg, target)


def _b2_b1a_call(proj, dmixed, conv_s, states, conv_w8, gret, tb, w_in_g, x, meta_tile, g1, dh2):
    nt, rows = tb["nt"], tb["rows"]
    seq = nt * TM

    def pb(r):
        return jnp.where(r == nt, nt, nt - 1 - r)

    def xprev(r):
        return jnp.clip(nt - r, 0, nt - 1)

    def body(proj_ref, dmx_ref, conv_ref, states_ref, cw_ref, g_ref, tt_ref, cr_ref, sr_ref, sgn_ref, dec_ref,
             xi_ref, zeta_ref, cd_ref, w_ref, x_ref, mt_ref, g1_ref, dh2_ref,
             dproj_hbm, gcw_ref, gg_ref, gx_ref, dmeta_ref, gn_ref,
             gstate, dchalo, dps, out_sems):
        r = pl.program_id(0)
        live = jnp.where(r == nt, 0.0, 1.0)
        slot = lax.rem(r, 2)
        dproj_ref = dps.at[slot]

        class to_hbm:
            def __init__(self, s, tile):
                self.copies = [pltpu.make_async_copy(dps.at[s, :, j * 1024:(j + 1) * 1024], dproj_hbm.at[tile, j],
                                                     out_sems.at[N_CHIPS * s + j]) for j in range(N_CHIPS)]

            def start(self):
                for cp in self.copies:
                    cp.start()

            def wait(self):
                for cp in self.copies:
                    cp.wait()

        @pl.when(r == 0)
        def _():
            gstate[...] = jnp.zeros_like(gstate)
            dchalo[...] = jnp.zeros_like(dchalo)
            gcw_ref[...] = jnp.zeros_like(gcw_ref)
            gg_ref[...] = jnp.zeros_like(gg_ref)
            gn_ref[...] = jnp.zeros_like(gn_ref)
            dps[...] = jnp.zeros_like(dps)

        @pl.when(r >= 2)
        def _():
            to_hbm(slot, pb(r - 2)).wait()

        dprev = dps.at[1 - slot]
        pieces = []

        def emit_piece():
            j = len(pieces)
            if j < N_CHIPS:
                p = _dot_tb(dprev[:, j * 1024:(j + 1) * 1024], w_ref[j])
                pieces.append(p if j == 0 else pieces[-1] + p)

        cx = proj_ref[:, CX:CX + 512].astype(F32)
        cb = proj_ref[:, CB:CB + 512].astype(F32)
        cc = proj_ref[:, CC:CC + 512].astype(F32)
        cg = proj_ref[:, CG:CG + 512].astype(F32)
        dco = dmx_ref[:, 0:D_CONV].astype(F32) * live
        conv = conv_ref[...].astype(F32)
        sg = _sigmoid(cg)
        sil = cg * sg
        t = dco * conv
        dproj_ref[:, CB:CB + 512] = (t * sil).astype(BF16)
        dproj_ref[:, CG:CG + 512] = (t * cb * (sg * (1.0 + cg * (1.0 - sg)))).astype(BF16)
        dconv = dco * cb * sil
        row = lax.broadcasted_iota(jnp.int32, (TM, D_CONV), 0)
        n0 = dchalo[0:1, :]
        n1 = dchalo[1:2, :]
        dc1 = jnp.where(row == TM - 1, n0, pltpu.roll(dconv, TM - 1, 0))
        dc2 = jnp.where(row == TM - 2, n0, jnp.where(row == TM - 1, n1, pltpu.roll(dconv, TM - 2, 0)))
        dchalo[...] = dconv[0:8, :]
        du = cw_ref[2:3, :] * dconv + cw_ref[1:2, :] * dc1 + cw_ref[0:1, :] * dc2
        u = cc * cx
        gcw_ref[2:3, :] += jnp.sum(u * dconv, axis=0, keepdims=True)
        gcw_ref[1:2, :] += jnp.sum(u * dc1, axis=0, keepdims=True)
        gcw_ref[0:1, :] += jnp.sum(u * dc2, axis=0, keepdims=True)
        dproj_ref[:, CC:CC + 512] = (du * cx).astype(BF16)
        dproj_ref[:, CX:CX + 512] = (du * cc).astype(BF16)
        emit_piece()

        scale = HEAD_DIM ** -0.5
        gs = {h: gstate[h] for h in range(RET_HEADS)}
        gg = {h: jnp.zeros((1, HEAD_DIM), F32) for h in range(RET_HEADS)}
        col = lambda base, h: slice(base + h * HEAD_DIM, base + (h + 1) * HEAD_DIM)
        rw = lambda c: slice(c * CHUNK, (c + 1) * CHUNK)
        rc_t, rs_t = _tile_rotary(tt_ref, cr_ref, sr_ref, sgn_ref)
        for c0 in range(NCH - CHUNK_GROUP, -1, -CHUNK_GROUP):
            cs = range(c0 + CHUNK_GROUP - 1, c0 - 1, -1)
            U = [(c, h) for c in cs for h in range(RET_HEADS)]
            rc = {c: rc_t[rw(c), :] for c in cs}
            rs = {c: rs_t[rw(c), :] for c in cs}
            v = {(c, h): proj_ref[rw(c), col(CV, h)] for c, h in U}
            stb = {(c, h): states_ref[c, h] for c, h in U}
            qf = {(c, h): _rot(proj_ref[rw(c), col(CQ, h)].astype(F32), rc[c], rs[c]) * scale for c, h in U}
            kf = {(c, h): _rot(proj_ref[rw(c), col(CK, h)].astype(F32), rc[c], rs[c]) for c, h in U}
            qb = {u: qf[u].astype(BF16) for u in U}
            kb = {u: kf[u].astype(BF16) for u in U}
            qxb = {(c, h): (qf[c, h] * xi_ref[h]).astype(BF16) for c, h in U}
            kzb = {(c, h): (kf[c, h] * zeta_ref[h]).astype(BF16) for c, h in U}
            ab = {(c, h): (_dot_tb(qb[c, h], kb[c, h]) * dec_ref[h]).astype(BF16) for c, h in U}
            o = {u: _dot(ab[u], v[u]) + _dot(qxb[u], stb[u]) for u in U}
            emit_piece()
            dob = {}
            for c, h in U:
                mu = jnp.mean(o[c, h], axis=-1, keepdims=True)
                d = o[c, h] - mu
                var = jnp.mean(d * d, axis=-1, keepdims=True)
                rstd = lax.rsqrt(var + EPS)
                yh = d * rstd
                g = g_ref[:, col(0, h)]
                rg = proj_ref[rw(c), col(CR, h)].astype(F32)
                dro = dmx_ref[rw(c), col(D_CONV, h)].astype(F32) * live
                sg = _sigmoid(rg)
                dproj_ref[rw(c), col(CR, h)] = (dro * (yh * g) * (sg * (1.0 + rg * (1.0 - sg)))).astype(BF16)
                dret = dro * (rg * sg)
                gg[h] = gg[h] + jnp.sum(dret * yh, axis=0, keepdims=True)
                dyh = dret * g
                do = rstd * (dyh - jnp.mean(dyh, axis=-1, keepdims=True)
                             - yh * jnp.mean(dyh * yh, axis=-1, keepdims=True))
                dob[c, h] = do.astype(BF16)
            dv1 = {u: _dot_ta(ab[u], dob[u]) for u in U}
            ds = {(c, h): (_dot_tb(dob[c, h], v[c, h]) * dec_ref[h]).astype(BF16) for c, h in U}
            gup = {u: _dot_ta(qxb[u], dob[u]) for u in U}
            dq = {(c, h): _dot(ds[c, h], kb[c, h]) + _dot_tb(dob[c, h], stb[c, h]) * xi_ref[h] for c, h in U}
            dk1 = {u: _dot_ta(ds[u], qb[u]) for u in U}
            emit_piece()
            for c, h in U:
                gsb = gs[h].astype(BF16)
                dv = dv1[c, h] + _dot(kzb[c, h], gsb)
                dk = dk1[c, h] + _dot_tb(v[c, h], gsb) * zeta_ref[h]
                gs[h] = cd_ref[h, 0:1, :] * gs[h] + gup[c, h]
                dproj_ref[rw(c), col(CQ, h)] = (_rot_t(dq[c, h], rc[c], rs[c]) * scale).astype(BF16)
                dproj_ref[rw(c), col(CK, h)] = _rot_t(dk, rc[c], rs[c]).astype(BF16)
                dproj_ref[rw(c), col(CV, h)] = dv.astype(BF16)
        for h in range(RET_HEADS):
            gstate[h] = gs[h]
            gg_ref[:, col(0, h)] += gg[h]

        while len(pieces) < N_CHIPS:
            emit_piece()

        def norm_bwd(dhn, hx):
            ms = jnp.mean(hx * hx, axis=-1, keepdims=True)
            rstd1 = lax.rsqrt(ms + EPS)
            xh = hx * rstd1
            gn_ref[...] += jnp.sum(dhn * xh, axis=0, keepdims=True)
            dxh = dhn * g1_ref[...]
            return rstd1 * (dxh - xh * jnp.mean(dxh * xh, axis=-1, keepdims=True))

        gx_ref[...] = norm_bwd(pieces[-1], x_ref[...]) + dh2_ref[...]

        @pl.when(r < nt)
        def _():
            to_hbm(slot, pb(r)).start()

        @pl.when(r == nt)
        def _():
            to_hbm(slot, pb(r)).start()
            mrows = slice(TM - N_META, TM)
            d16 = dproj_ref[mrows, :]
            dhn16 = _dot_tb(d16[:, 0:1024], w_ref[0])
            for j in range(1, N_CHIPS):
                dhn16 += _dot_tb(d16[:, j * 1024:(j + 1) * 1024], w_ref[j])
            dmeta_ref[...] = norm_bwd(dhn16, mt_ref[mrows, :])
            to_hbm(1 - slot, pb(r - 1)).wait()
            to_hbm(slot, pb(r)).wait()

    tile = lambda w: pl.BlockSpec((TM, w), lambda r: (pb(r), 0))
    xtile = pl.BlockSpec((TM, D_MODEL), lambda r: (xprev(r), 0))
    return pl.pallas_call(
        body, name="b2_mixer_bwd_b1a_inproj_bwd_x",
        grid=(nt + 1,),
        in_specs=[tile(N_PROJ_COLS),
                  pl.BlockSpec((TM, D_MODEL), lambda r: (jnp.minimum(pb(r), nt - 1), 0)),
                  tile(D_CONV),
                  pl.BlockSpec((NCH, RET_HEADS, HEAD_DIM, HEAD_DIM), lambda r: (pb(r), 0, 0, 0)),
                  _resident((8, D_CONV)), _resident((1, D_RET)),
                  pl.BlockSpec((None, 8, HEAD_DIM), lambda r: (pb(r), 0, 0)),
                  _resident((TM, HEAD_DIM)), _resident((TM, HEAD_DIM)), _resident((8, HEAD_DIM)),
                  _resident((RET_HEADS, CHUNK, CHUNK)),
                  _resident((RET_HEADS, CHUNK, HEAD_DIM)),
                  _resident((RET_HEADS, CHUNK, HEAD_DIM)), _resident((RET_HEADS, 8, HEAD_DIM)),
                  _resident1((N_CHIPS, D_MODEL, 1024)), xtile, _resident1((TM, D_MODEL)), _resident((1, D_MODEL)),
                  xtile],
        out_specs=[pl.BlockSpec(memory_space=pl.ANY), _resident((8, D_CONV)), _resident((1, D_RET)),
                   xtile, _resident((N_META, D_MODEL)), _resident((1, D_MODEL))],
        out_shape=[jax.ShapeDtypeStruct((nt + 1, N_CHIPS, TM, 1024), BF16),
                   jax.ShapeDtypeStruct((8, D_CONV), F32),
                   jax.ShapeDtypeStruct((1, D_RET), F32),
                   jax.ShapeDtypeStruct((seq, D_MODEL), F32),
                   jax.ShapeDtypeStruct((N_META, D_MODEL), F32),
                   jax.ShapeDtypeStruct((1, D_MODEL), F32)],
        scratch_shapes=[pltpu.VMEM((RET_HEADS, HEAD_DIM, HEAD_DIM), F32), pltpu.VMEM((8, D_CONV), F32),
                        pltpu.VMEM((2, TM, N_PROJ_COLS), BF16), pltpu.SemaphoreType.DMA((2 * N_CHIPS,))],
        compiler_params=_cparams(("arbitrary",), vmem=VMEM_LIMIT_MAX),
    )(proj, dmixed, conv_s, states, conv_w8, gret, tb["tt"], tb["cr2"], tb["sr2"], tb["sgn"], tb["decay"],
      tb["xi"], tb["zeta"], tb["cd"], w_in_g, x, meta_tile, g1, dh2)


REL = (2, 1, 3)
SMALL_ROWS = 24
HALF_STEP = 4


def _rcopy(src, dst, send_sems, recv_sems, k, to):
    return pltpu.make_async_remote_copy(src_ref=src, dst_ref=dst, send_sem=send_sems.at[k],
                                        recv_sem=recv_sems.at[k], device_id=to, device_id_type=MESH_ID)


def _b1b_reduce_call(order, hnt, dproj, gwo, pack, nt):
    nk = nt + 1
    last = nk - 1
    any_spec = pl.BlockSpec(memory_space=pl.ANY)

    def body(order_ref, a_ref, b_ref, gwo_hbm, pack_hbm, gwin_hbm, gwout_hbm, tot_hbm,
             acc, sb, abuf, pb, bbuf, fin, go, ao, pbo, bo, fino, slots, totv, send_sems, recv_sems, loc_sems):
        jj, k = pl.program_id(0), pl.program_id(1)
        x, y, c = lax.axis_index("x"), lax.axis_index("y"), lax.axis_index("c")
        me, myid, sib = 2 * x + y, 4 * x + 2 * y + c, (x, y, 1 - c)
        rc = functools.partial(_rcopy, send_sems=send_sems, recv_sems=recv_sems)
        peers = [((1 - x) if r & 2 else x, (1 - y) if r & 1 else y, c) for r in REL]
        kids = [jnp.bitwise_xor(me, r) for r in REL]

        def dev_peer(r):
            return ((1 - x) if r & 4 else x, (1 - y) if r & 2 else y, (1 - c) if r & 1 else c)

        own_go = pltpu.make_async_copy(gwo_hbm.at[c], go, loc_sems.at[0])
        own_pack = pltpu.make_async_copy(pack_hbm, slots.at[0], loc_sems.at[1])
        wo_half = rc(gwo_hbm.at[1 - c], ao, k=8, to=sib)
        wo_part = [rc(pbo.at[kids[p]], bo.at[p], k=9 + p, to=peers[p]) for p in range(3)]
        sm = [rc(pack_hbm, slots.at[r], k=12 + r, to=dev_peer(r)) for r in range(1, N_DEV)]
        half = [rc(sb.at[j % 2, 1 - c], abuf.at[j], k=j, to=sib) for j in range(N_CHIPS)]
        part = [rc(pb.at[p], bbuf.at[p], k=4 + p, to=peers[p]) for p in range(3)]

        @pl.when(jnp.logical_and(jj == 0, k == 0))
        def _():
            own_go.start()
            own_pack.start()
            wo_half.start()
            for cp in sm:
                cp.start()

        @pl.when(k == 0)
        def _():
            acc[...] = jnp.zeros_like(acc)

        acc[0] += _dot(a_ref[0:512, :], b_ref[...])
        acc[1] += _dot(a_ref[512:1024, :], b_ref[...])

        @pl.when(k == HALF_STEP)
        def _():
            @pl.when(jj == 0)
            def _():
                own_go.wait()
                wo_half.wait_recv()
                for j in range(N_CHIPS):
                    go[j] = go[j] + ao[j]
                pbo[...] = go[...].astype(BF16)
                for cp in wo_part:
                    cp.start()

            for p in range(3):
                @pl.when(jj == p + 1)
                def _(p=p):
                    half[p].wait_recv()
                    half[p].wait_send()
                    pb[p] = (sb[p % 2, c] + abuf[p]).astype(BF16)
                    part[p].start()

        @pl.when(k == last)
        def _():
            for j in range(N_CHIPS):
                @pl.when(jj == j)
                def _(j=j):
                    sb[j % 2] = acc[...]
                    half[j].start()

        @pl.when(jnp.logical_and(jj == N_CHIPS - 1, k == last))
        def _():
            half[3].wait_recv()
            own = sb[1, c] + abuf[3]
            for cp in part:
                cp.wait_recv()
            fin[c] = ((own + bbuf[0].astype(F32)) + bbuf[1].astype(F32)) + bbuf[2].astype(F32)
            done = rc(fin.at[c], fin.at[c], k=7, to=sib)
            done.start()
            for cp in wo_part:
                cp.wait_recv()
            fino[c] = ((go[me] + bo[0].astype(F32)) + bo[1].astype(F32)) + bo[2].astype(F32)
            done_o = rc(fino.at[c], fino.at[c], k=12, to=sib)
            done_o.start()
            own_pack.wait()
            for cp in sm:
                cp.wait_recv()
            tot = slots[myid]
            for a in range(1, N_DEV):
                tot = tot + slots[jnp.bitwise_xor(myid, a)]
            totv[...] = tot
            out_t = pltpu.make_async_copy(totv, tot_hbm, loc_sems.at[1])
            out_t.start()
            rc(fin.at[1 - c], fin.at[1 - c], k=7, to=sib).wait_recv()
            out_w = pltpu.make_async_copy(fin, gwin_hbm, loc_sems.at[0])
            out_w.start()
            rc(fino.at[1 - c], fino.at[1 - c], k=12, to=sib).wait_recv()
            out_o = pltpu.make_async_copy(fino, gwout_hbm, loc_sems.at[2])
            out_o.start()
            for cp in [half[3]] + part + [done, wo_half] + wo_part + [done_o] + sm:
                cp.wait_send()
            out_t.wait()
            out_w.wait()
            out_o.wait()

    grid_spec = pltpu.PrefetchScalarGridSpec(
        num_scalar_prefetch=1,
        grid=(N_CHIPS, nk),
        in_specs=[pl.BlockSpec((None, D_MODEL, TM), lambda j, k, o: (k, 0, 0)),
                  pl.BlockSpec((None, None, TM, 1024), lambda j, k, o: (k, o[j], 0, 0)),
                  any_spec, any_spec],
        out_specs=[any_spec, any_spec, any_spec],
        scratch_shapes=[
            pltpu.VMEM((2, 512, 1024), F32),
            pltpu.VMEM((2, 2, 512, 1024), F32),
            pltpu.VMEM((N_CHIPS, 512, 1024), F32),
            pltpu.VMEM((3, 512, 1024), BF16),
            pltpu.VMEM((3, 512, 1024), BF16),
            pltpu.VMEM((2, 512, 1024), F32),
            pltpu.VMEM((N_CHIPS, 128, D_MODEL), F32),
            pltpu.VMEM((N_CHIPS, 128, D_MODEL), F32),
            pltpu.VMEM((N_CHIPS, 128, D_MODEL), BF16),
            pltpu.VMEM((3, 128, D_MODEL), BF16),
            pltpu.VMEM((2, 128, D_MODEL), F32),
            pltpu.VMEM((N_DEV, SMALL_ROWS, D_MODEL), F32),
            pltpu.VMEM((SMALL_ROWS, D_MODEL), F32),
            pltpu.SemaphoreType.DMA((20,)), pltpu.SemaphoreType.DMA((20,)), pltpu.SemaphoreType.DMA((3,))])
    return pl.pallas_call(
        body, name="b1b_inproj_bwd_w_reduce",
        grid_spec=grid_spec,
        out_shape=[jax.ShapeDtypeStruct((2, 512, 1024), F32),
                   jax.ShapeDtypeStruct((2, 128, D_MODEL), F32),
                   jax.ShapeDtypeStruct((SMALL_ROWS, D_MODEL), F32)],
        compiler_params=_cparams(("arbitrary", "arbitrary")),
    )(order, hnt, dproj, gwo, pack)


def _local_step(me, x, target, g1, gret, fg, win_sh, wout_sh, small_sh):
    seq = x.shape[0]
    tb = _tables(seq)
    nt = tb["nt"]
    g1r, gretr, fgr = g1.reshape(1, -1), gret.reshape(1, -1), fg.reshape(1, -1)
    order = jnp.stack([me, me ^ REL[0], me ^ REL[1], me ^ REL[2]]).astype(jnp.int32)

    proj, hnt, w_in_g, w_out_g, meta_tile, conv_w8 = _f1_gather_call(order, x, g1r, win_sh, wout_sh, small_sh, nt)
    w_out = w_out_g.reshape(D_MODEL, D_MODEL)
    conv_s, states, dh2, dmixed, g_wout, g_fg, loss = _f2_f3_call(proj, conv_w8, gretr, tb, x, w_out, fgr, target)
    dproj, g_cw8, g_gret, grad_x, g_meta, g_g1 = _b2_b1a_call(proj, dmixed, conv_s, states, conv_w8, gretr, tb,
                                                              w_in_g, x, meta_tile, g1r, dh2)
    return loss, grad_x, dict(w_out=g_wout, meta=g_meta, conv_w=g_cw8[0:3], norm1_g=g_g1,
                              ret_norm_g=g_gret, final_g=g_fg), hnt, dproj


def _adamw_update(w_ref, g_ref, m_ref, v_ref, d_ref, nm_ref, nv_ref):
    gg = g_ref[...]
    nm = ADAM_B1 * m_ref[...] + (1.0 - ADAM_B1) * gg
    nv = ADAM_B2 * v_ref[...] + (1.0 - ADAM_B2) * (gg * gg)
    m_hat = nm / (1.0 - ADAM_B1 ** ADAM_STEP)
    v_hat = nv / (1.0 - ADAM_B2 ** ADAM_STEP)
    d_ref[...] = -ADAM_LR * (m_hat / (jnp.sqrt(v_hat) + ADAM_EPS) + ADAM_WD * w_ref[...])
    nm_ref[...] = nm
    nv_ref[...] = nv


def _adamw_small_call(ws, gs, ms, vs):
    n = len(ws)

    def body(*refs):
        ins, outs = refs[:4 * n], refs[4 * n:]
        for i in range(n):
            _adamw_update(ins[i], ins[n + i], ins[2 * n + i], ins[3 * n + i],
                          outs[i], outs[n + i], outs[2 * n + i])

    shapes = [jax.ShapeDtypeStruct(w.shape, F32) for w in ws]
    outs = pl.pallas_call(body, name="adamw_small", out_shape=shapes * 3,
                          compiler_params=_cparams())(*ws, *gs, *ms, *vs)
    return outs[:n], outs[n:2 * n], outs[2 * n:]


def _adamw_call(w, g, m, v, name):
    shape = w.shape
    w2, g2, m2, v2 = (a.reshape(-1, shape[-1]) for a in (w, g, m, v))
    rows, cols = w2.shape
    br = 256 if rows % 256 == 0 else rows
    body = functools.partial(_adamw_update)
    spec = pl.BlockSpec((br, cols), lambda i: (i, 0))
    outs = pl.pallas_call(
        body, name=name, grid=(rows // br,),
        in_specs=[spec] * 4, out_specs=[spec] * 3,
        out_shape=[jax.ShapeDtypeStruct((rows, cols), F32)] * 3,
        compiler_params=_cparams(("arbitrary",)),
    )(w2, g2, m2, v2)
    return tuple(o.reshape(shape) for o in outs)


def _pad_to(a, rows, cols):
    return jnp.pad(a, ((0, rows - a.shape[0]), (0, cols - a.shape[1])))


def kernel(x, meta, norm1_g, w_in, conv_w, ret_norm_g, w_out, final_g, loss_target, m_meta, m_norm1_g, m_w_in, m_conv_w, m_ret_norm_g, m_w_out, m_final_g, v_meta, v_norm1_g, v_w_in, v_conv_w, v_ret_norm_g, v_w_out, v_final_g):
    me = 2 * lax.axis_index("x") + lax.axis_index("y")

    small_sh = jnp.concatenate([meta, _pad_to(conv_w, 8, 256)], axis=0)
    loss, grad_x, g, hnt, dproj = _local_step(me, x[0], loss_target[0], norm1_g, ret_norm_g, final_g,
                                              w_in.astype(BF16), w_out.astype(BF16), small_sh)

    vec = jnp.concatenate([g["norm1_g"], g["final_g"], _pad_to(g["ret_norm_g"], 1, D_MODEL),
                           _pad_to(g["conv_w"], 3, D_MODEL), _pad_to(loss, 2, D_MODEL)], axis=0)
    pack = jnp.concatenate([g["meta"], vec], axis=0)
    order = jnp.stack([me ^ REL[0], me ^ REL[1], me ^ REL[2], me]).astype(jnp.int32)
    g_win, g_wout, tot = _b1b_reduce_call(order, hnt, dproj, g["w_out"], pack, x.shape[1] // TM)
    g_win, g_wout = g_win.reshape(D_MODEL, 1024), g_wout.reshape(256, D_MODEL)
    g_meta = lax.dynamic_slice(tot, (0, me * 256), (N_META, 256))
    g_conv = lax.dynamic_slice(tot, (N_META + 3, me * 128), (3, 128))
    g_n1, g_fg, g_rn = tot[N_META], tot[N_META + 1], tot[N_META + 2, 0:D_RET]

    loss_tot = tot[N_META + 6, 0]

    grads = [g_meta, g_n1, g_win, g_conv, g_rn, g_wout, g_fg]
    ws = [meta, norm1_g, w_in, conv_w, ret_norm_g, w_out, final_g]
    ms = [m_meta, m_norm1_g, m_w_in, m_conv_w, m_ret_norm_g, m_w_out, m_final_g]
    vs = [v_meta, v_norm1_g, v_w_in, v_conv_w, v_ret_norm_g, v_w_out, v_final_g]
    names = ["meta", "norm1_g", "w_in", "conv_w", "ret_norm_g", "w_out", "final_g"]
    as2d = lambda a: a.reshape(1, -1) if a.ndim == 1 else a
    big = [i for i, n_ in enumerate(names) if n_ in ("w_in", "w_out")]
    small = [i for i in range(len(names)) if i not in big]
    deltas, new_ms, new_vs = [None] * 7, [None] * 7, [None] * 7
    for i in big:
        deltas[i], new_ms[i], new_vs[i] = _adamw_call(ws[i], grads[i], ms[i], vs[i], "adamw_" + names[i])
    sd, sm_, sv = _adamw_small_call(*[[as2d(t[i]) for i in small] for t in (ws, grads, ms, vs)])
    for j, i in enumerate(small):
        deltas[i], new_ms[i], new_vs[i] = (o[j].reshape(ws[i].shape) for o in (sd, sm_, sv))
    return (loss_tot, grad_x[None], *grads, *deltas, *new_ms, *new_vs)
```

```python
import functools

import jax
import jax.numpy as jnp
from jax import lax
from jax.experimental import pallas as pl
from jax.experimental.pallas import tpu as pltpu

F32 = jnp.float32
BF16 = jnp.bfloat16

D_MODEL = 1024
N_META = 16
D_CONV = 512
D_RET = 512
RET_HEADS = 4
HEAD_DIM = 128
CHUNK = 128
N_PROJ_COLS = 4096
ROPE_BASE = 10000.0
EPS = 1e-6
N_CHIPS = 4
N_DEV = 8

ADAM_LR = 0.001
ADAM_B1 = 0.9
ADAM_B2 = 0.999
ADAM_EPS = 1e-08
ADAM_WD = 0.01
ADAM_STEP = 10

TM = 512
NCH = TM // CHUNK
CHUNK_GROUP = 2
VMEM_LIMIT = 56 * 1024 * 1024
VMEM_LIMIT_MAX = 63 * 1024 * 1024

CX, CB, CC, CG, CQ, CK, CV, CR = (i * 512 for i in range(8))

MESH_ID = pl.DeviceIdType.MESH


def _cparams(sem=None, vmem=VMEM_LIMIT, **kw):
    return pltpu.CompilerParams(dimension_semantics=sem, vmem_limit_bytes=vmem, **kw)


def _sigmoid(x):
    return 1.0 / (1.0 + jnp.exp(-x))


def _dot(a, b):
    return jnp.dot(a, b, preferred_element_type=F32)


def _dot_tb(a, b):
    return lax.dot_general(a, b, (((1,), (1,)), ((), ())), preferred_element_type=F32)


def _dot_ta(a, b):
    return lax.dot_general(a, b, (((0,), (0,)), ((), ())), preferred_element_type=F32)


def _resident(shape):
    nd = len(shape)
    return pl.BlockSpec(shape, lambda *_: (0,) * nd)


def _resident1(shape):
    nd = len(shape)
    return pl.BlockSpec(shape, lambda *_: (0,) * nd, pipeline_mode=pl.Buffered(1))


def _tables(seq):
    nt = seq // TM
    rows = seq + TM
    half = HEAD_DIM // 2
    freqs = 1.0 / (ROPE_BASE ** (jnp.arange(half, dtype=F32) / half))
    tile_start = jnp.concatenate([jnp.arange(nt, dtype=F32), -jnp.ones((1,), F32)]) * TM
    ang_t = tile_start[:, None] * freqs[None, :]
    ang_r = (jnp.arange(TM, dtype=F32) + N_META)[:, None] * freqs[None, :]
    dup = lambda a: jnp.concatenate([a, a], axis=-1)
    tt = jnp.stack([dup(jnp.cos(ang_t)), dup(jnp.sin(ang_t))], axis=1)
    tt = jnp.pad(tt, ((0, 0), (0, 6), (0, 0)))
    cr2, sr2 = dup(jnp.cos(ang_r)), dup(jnp.sin(ang_r))
    sgn = jnp.concatenate([-jnp.ones((8, half), F32), jnp.ones((8, half), F32)], axis=-1)
    log_g = jnp.log(1.0 - 2.0 ** (-5.0 - jnp.arange(RET_HEADS, dtype=F32)))
    idx = jnp.arange(CHUNK, dtype=F32)
    diff = idx[:, None] - idx[None, :]
    decay = jnp.where(diff[None] >= 0, jnp.exp(diff[None] * log_g[:, None, None]), 0.0)
    zeta = jnp.exp((CHUNK - 1 - idx)[None, :] * log_g[:, None])
    xi = jnp.exp((idx + 1.0)[None, :] * log_g[:, None])
    cd = jnp.exp(CHUNK * log_g)
    zeta_b = jnp.broadcast_to(zeta[:, :, None], (RET_HEADS, CHUNK, HEAD_DIM))
    xi_b = jnp.broadcast_to(xi[:, :, None], (RET_HEADS, CHUNK, HEAD_DIM))
    cd_b = jnp.broadcast_to(cd[:, None, None], (RET_HEADS, 8, HEAD_DIM))
    return dict(nt=nt, rows=rows, tt=tt, cr2=cr2, sr2=sr2, sgn=sgn, decay=decay, zeta=zeta_b, xi=xi_b, cd=cd_b)


def _tile_rotary(tt_ref, cr_ref, sr_ref, sgn_ref):
    ct, st = tt_ref[0:1, :], tt_ref[1:2, :]
    cr, sr = cr_ref[...], sr_ref[...]
    return ct * cr - st * sr, (st * cr + ct * sr) * sgn_ref[0:1, :]


def _rot(t, rc, rs):
    return t * rc + pltpu.roll(t, HEAD_DIM // 2, 1) * rs


def _rot_t(dt, rc, rs):
    return dt * rc + pltpu.roll(dt * rs, HEAD_DIM // 2, 1)


def _f1_gather_call(order, x, g1, win_sh, wout_sh, small_sh, nt):
    nk = nt + 1
    rows = nk * TM
    any_spec = pl.BlockSpec(memory_space=pl.ANY)

    def body(order_ref, x_ref, g_ref, win_hbm, wout_hbm, sm_hbm,
             proj_ref, hnt_ref, wg_hbm, wog_hbm, mt_hbm, cw_hbm,
             wg, wog, smg, mt, cw, hbs, send_sems, recv_sems, loc_sems):
        jj, k = pl.program_id(0), pl.program_id(1)
        x, y, c = lax.axis_index("x"), lax.axis_index("y"), lax.axis_index("c")
        me, sib = 2 * x + y, (x, y, 1 - c)
        rc = functools.partial(_rcopy, send_sems=send_sems, recv_sems=recv_sems)
        peers = [((1 - x) if r & 2 else x, (1 - y) if r & 1 else y, c) for r in REL]
        kids = [jnp.bitwise_xor(me, r) for r in REL]
        hw, ho = pl.ds(c * 512, 512), pl.ds(c * 128, 128)
        hw2, ho2 = pl.ds((1 - c) * 512, 512), pl.ds((1 - c) * 128, 128)
        at = lambda j_, k_: jnp.logical_and(jj == j_, k == k_)

        sm_cp = [rc(sm_hbm, smg.at[me], k=p, to=peers[p]) for p in range(3)]
        win_cp = [rc(win_hbm.at[hw], wg.at[me, hw], k=3 + p, to=peers[p]) for p in range(3)]
        wout_cp = [rc(wout_hbm.at[ho], wog.at[me, ho], k=6 + p, to=peers[p]) for p in range(3)]
        sm_in = [rc(sm_hbm, smg.at[kids[p]], k=p, to=sib) for p in range(3)]
        win_in = [rc(win_hbm.at[hw], wg.at[kids[p], hw], k=3 + p, to=sib) for p in range(3)]
        wout_in = [rc(wout_hbm.at[ho], wog.at[kids[p], ho], k=6 + p, to=sib) for p in range(3)]
        win_fw = [rc(wg.at[kids[p], hw], wg.at[kids[p], hw], k=9 + p, to=sib) for p in range(3)]
        wout_fw = [rc(wog.at[kids[p], ho], wog.at[kids[p], ho], k=12 + p, to=sib) for p in range(3)]
        win_fw_in = [rc(wg.at[kids[p], hw2], wg.at[kids[p], hw2], k=9 + p, to=sib) for p in range(3)]
        wout_fw_in = [rc(wog.at[kids[p], ho2], wog.at[kids[p], ho2], k=12 + p, to=sib) for p in range(3)]
        own_w = pltpu.make_async_copy(win_hbm, wg.at[me], loc_sems.at[0])
        own_o = pltpu.make_async_copy(wout_hbm, wog.at[me], loc_sems.at[1])
        own_s = pltpu.make_async_copy(sm_hbm, smg.at[me], loc_sems.at[2])
        out_wg = pltpu.make_async_copy(wg, wg_hbm, loc_sems.at[3])
        out_wog = pltpu.make_async_copy(wog, wog_hbm, loc_sems.at[4])
        out_mt = pltpu.make_async_copy(mt, mt_hbm, loc_sems.at[5])
        out_cw = pltpu.make_async_copy(cw, cw_hbm, loc_sems.at[6])

        def pass_on(p):
            win_in[p].wait_recv()
            win_fw[p].start()

        @pl.when(k <= 1)
        def _():
            @pl.when(at(0, 0))
            def _():
                own_w.start()
                own_s.start()
                own_o.start()
                for cp in sm_cp + win_cp + wout_cp:
                    cp.start()
                own_w.wait()

            for p in range(3):
                @pl.when(at(p + 1, 0))
                def _(p=p):
                    win_fw_in[p].wait_recv()

            @pl.when(at(1, 1))
            def _():
                pass_on(1)

            @pl.when(at(3, 0))
            def _():
                out_wg.start()

            @pl.when(at(3, 1))
            def _():
                for p in range(3):
                    wout_in[p].wait_recv()
                    wout_fw[p].start()

        @pl.when(jnp.logical_and(jj == 0, k >= nk - 2))
        def _():
            @pl.when(k == nk - 2)
            def _():
                own_s.wait()
                for cp in sm_in:
                    cp.wait_recv()
                mt[...] = jnp.zeros_like(mt)
                cw[...] = jnp.zeros_like(cw)
                for j in range(N_CHIPS):
                    mt[TM - N_META:TM, j * 256:(j + 1) * 256] = smg[j, 0:N_META, :]
                    cw[0:3, j * 128:(j + 1) * 128] = smg[j, N_META:N_META + 3, 0:128]
                out_mt.start()
                out_cw.start()

            @pl.when(k == nk - 1)
            def _():
                pass_on(0)

        @pl.when(at(2, nk // 2))
        def _():
            pass_on(2)

        tile_rows = pl.ds(pl.multiple_of(k * TM, TM), TM)

        @pl.when(jj == 0)
        def _():
            h = jnp.where(k == nt, mt[...], x_ref[...])
            ms = jnp.mean(h * h, axis=-1, keepdims=True)
            hn = (h * lax.rsqrt(ms + EPS)) * g_ref[...]
            hb = hn.astype(BF16)
            hbs[tile_rows, :] = hb
            proj_ref[...] = _dot(hb, wg[order_ref[0]]).astype(BF16)
            hnt_ref[...] = hn.T.astype(BF16)

        @pl.when(jj > 0)
        def _():
            proj_ref[...] = _dot(hbs[tile_rows, :], wg[order_ref[jj]]).astype(BF16)

        @pl.when(at(3, nk - 1))
        def _():
            own_o.wait()
            for cp in wout_fw_in:
                cp.wait_recv()
            out_wog.start()
            for cp in sm_cp + win_cp + wout_cp + win_fw + wout_fw:
                cp.wait_send()
            for cp in (out_wg, out_wog, out_mt, out_cw):
                cp.wait()

    grid_spec = pltpu.PrefetchScalarGridSpec(
        num_scalar_prefetch=1,
        grid=(N_CHIPS, nk),
        in_specs=[pl.BlockSpec((TM, D_MODEL), lambda j, k, o: (jnp.where(j == 0, jnp.minimum(k, nt - 1), nt - 1), 0)),
                  pl.BlockSpec((1, D_MODEL), lambda j, k, o: (0, 0)),
                  any_spec, any_spec, any_spec],
        out_specs=[pl.BlockSpec((TM, 1024), lambda j, k, o: (k, o[j])),
                   pl.BlockSpec((None, D_MODEL, TM), lambda j, k, o: (jnp.where(j == 0, k, nk - 1), 0, 0)),
                   any_spec, any_spec, any_spec, any_spec],
        scratch_shapes=[
            pltpu.VMEM((N_CHIPS, D_MODEL, 1024), BF16),
            pltpu.VMEM((N_CHIPS, 256, D_MODEL), BF16),
            pltpu.VMEM((N_CHIPS, SMALL_ROWS, 256), F32),
            pltpu.VMEM((TM, D_MODEL), F32),
            pltpu.VMEM((8, D_CONV), F32),
            pltpu.VMEM((rows, D_MODEL), BF16),
            pltpu.SemaphoreType.DMA((15,)), pltpu.SemaphoreType.DMA((15,)), pltpu.SemaphoreType.DMA((7,))])
    return pl.pallas_call(
        body, name="f1_norm_inproj_gather",
        grid_spec=grid_spec,
        out_shape=[jax.ShapeDtypeStruct((rows, N_PROJ_COLS), BF16),
                   jax.ShapeDtypeStruct((nk, D_MODEL, TM), BF16),
                   jax.ShapeDtypeStruct((N_CHIPS, D_MODEL, 1024), BF16),
                   jax.ShapeDtypeStruct((N_CHIPS, 256, D_MODEL), BF16),
                   jax.ShapeDtypeStruct((TM, D_MODEL), F32),
                   jax.ShapeDtypeStruct((8, D_CONV), F32)],
        compiler_params=_cparams(("arbitrary", "arbitrary")),
    )(order, x, g1, win_sh, wout_sh, small_sh)


def _f2_f3_call(proj, conv_w8, gret, tb, x, w_out, fg, target):
    nt, rows = tb["nt"], tb["rows"]
    seq = nt * TM

    def pf(s):
        return jnp.where(s == 0, nt, jnp.minimum(s - 1, nt - 1))

    def xt(s):
        return jnp.clip(s - 2, 0, nt - 1)

    def body(proj_ref, cw_ref, g_ref, tt_ref, cr_ref, sr_ref, sgn_ref, dec_ref, xi_ref, zeta_ref, cd_ref,
             x_ref, w_ref, fg_ref, t_ref,
             conv_hbm, states_hbm, dh2_ref, dmx_ref, gwo_ref, gfg_ref, loss_ref,
             state, uhalo, mxs, convs, sts, lacc, out_sems):
        s = pl.program_id(0)
        slot = lax.rem(s, 2)
        mixed_ref = mxs.at[slot]
        conv_ref = convs.at[slot]
        states_ref = sts.at[slot]

        def conv_out(sl, tile):
            return pltpu.make_async_copy(convs.at[sl], conv_hbm.at[pl.ds(pl.multiple_of(tile * TM, TM), TM), :],
                                         out_sems.at[sl])

        def states_out(sl, tile):
            return pltpu.make_async_copy(sts.at[sl], states_hbm.at[pl.ds(pl.multiple_of(tile * NCH, NCH), NCH)],
                                         out_sems.at[2 + sl])

        @pl.when(s == 0)
        def _():
            state[...] = jnp.zeros_like(state)
            uhalo[...] = jnp.zeros_like(uhalo)
            mxs[...] = jnp.zeros_like(mxs)
            gwo_ref[...] = jnp.zeros_like(gwo_ref)
            gfg_ref[...] = jnp.zeros_like(gfg_ref)
            lacc[...] = jnp.zeros_like(lacc)

        @pl.when(s >= 2)
        def _():
            conv_out(slot, pf(s - 2)).wait()
            states_out(slot, pf(s - 2)).wait()

        valid = jnp.where(s >= 2, 1.0, 0.0)
        mx_prev = mxs.at[1 - slot]
        f3 = {}

        def f3_fwd():
            f3["h2"] = x_ref[...] + _dot(mx_prev[...], w_ref[...])

        def f3_loss():
            h2 = f3.pop("h2")
            ms = jnp.mean(h2 * h2, axis=-1, keepdims=True)
            rstd = lax.rsqrt(ms + EPS)
            yh = h2 * rstd
            g = fg_ref[...]
            e = (yh * g - t_ref[...]) * valid
            lacc[...] += jnp.sum(e * e, axis=0, keepdims=True)
            dy = e * (1.0 / D_MODEL)
            gfg_ref[...] += jnp.sum(dy * yh, axis=0, keepdims=True)
            dyh = dy * g
            dh2 = rstd * (dyh - yh * jnp.mean(dyh * yh, axis=-1, keepdims=True))
            dh2_ref[...] = dh2
            f3["db"] = dh2.astype(BF16)

        def f3_dmx():
            dmx_ref[...] = _dot_tb(f3["db"], w_ref[...]).astype(BF16)

        def f3_gw():
            gw = _dot_ta(mx_prev[...], f3["db"])
            for j in range(N_CHIPS):
                for hf in range(2):
                    r0 = j * 256 + hf * 128
                    gwo_ref[hf, j] += gw[r0:r0 + 128, :]

        cx = proj_ref[:, CX:CX + 512].astype(F32)
        cc = proj_ref[:, CC:CC + 512].astype(F32)
        u = cc * cx
        row = lax.broadcasted_iota(jnp.int32, (TM, D_CONV), 0)
        h7 = uhalo[7:8, :]
        h6 = uhalo[6:7, :]
        u1 = jnp.where(row == 0, h7, pltpu.roll(u, 1, 0))
        u2 = jnp.where(row == 0, h6, jnp.where(row == 1, h7, pltpu.roll(u, 2, 0)))
        conv = cw_ref[2:3, :] * u + cw_ref[1:2, :] * u1 + cw_ref[0:1, :] * u2
        uhalo[...] = u[TM - 8:TM, :]
        cb = proj_ref[:, CB:CB + 512].astype(F32)
        cg = proj_ref[:, CG:CG + 512].astype(F32)
        mixed_ref[:, 0:D_CONV] = (cb * conv * (cg * _sigmoid(cg))).astype(BF16)
        conv_ref[...] = conv.astype(BF16)
        f3_fwd()

        scale = HEAD_DIM ** -0.5
        H = range(RET_HEADS)
        st = [state[h] for h in H]
        between = [f3_loss, f3_dmx, f3_gw, None]
        rc_t, rs_t = _tile_rotary(tt_ref, cr_ref, sr_ref, sgn_ref)
        for c in range(NCH):
            r0 = c * CHUNK
            rc = rc_t[r0:r0 + CHUNK, :]
            rs = rs_t[r0:r0 + CHUNK, :]
            col = lambda base, h: slice(base + h * HEAD_DIM, base + (h + 1) * HEAD_DIM)
            rws = slice(r0, r0 + CHUNK)
            v = [proj_ref[rws, col(CV, h)] for h in H]
            qf = [_rot(proj_ref[rws, col(CQ, h)].astype(F32), rc, rs) * scale for h in H]
            kf = [_rot(proj_ref[rws, col(CK, h)].astype(F32), rc, rs) for h in H]
            stb = [t.astype(BF16) for t in st]
            for h in H:
                states_ref[c, h] = stb[h]
            a = [(_dot_tb(qf[h].astype(BF16), kf[h].astype(BF16)) * dec_ref[h]).astype(BF16) for h in H]
            o = [_dot(a[h], v[h]) + _dot((qf[h] * xi_ref[h]).astype(BF16), stb[h]) for h in H]
            st = [cd_ref[h, 0:1, :] * st[h] + _dot_ta((kf[h] * zeta_ref[h]).astype(BF16), v[h]) for h in H]
            for h in H:
                mu = jnp.mean(o[h], axis=-1, keepdims=True)
                d = o[h] - mu
                var = jnp.mean(d * d, axis=-1, keepdims=True)
                yh = d * lax.rsqrt(var + EPS)
                rg = proj_ref[rws, col(CR, h)].astype(F32)
                mixed_ref[rws, col(D_CONV, h)] = (yh * g_ref[:, col(0, h)] * (rg * _sigmoid(rg))).astype(BF16)
            if between[c] is not None:
                between[c]()
        for h in H:
            state[h] = st[h]

        @pl.when(s <= nt)
        def _():
            conv_out(slot, pf(s)).start()
            states_out(slot, pf(s)).start()

        @pl.when(s == nt + 1)
        def _():
            conv_out(1 - slot, pf(s - 1)).wait()
            states_out(1 - slot, pf(s - 1)).wait()
            tot = jnp.sum(lacc[...], axis=1, keepdims=True) * (0.5 / D_MODEL)
            loss_ref[...] = jnp.broadcast_to(tot, (1, 128))

    tile = lambda w: pl.BlockSpec((TM, w), lambda s: (pf(s), 0))
    xtile = lambda w: pl.BlockSpec((TM, w), lambda s: (xt(s), 0))
    any_spec = pl.BlockSpec(memory_space=pl.ANY)
    return pl.pallas_call(
        body, name="f2_mixer_fwd_f3_outproj_loss",
        grid=(nt + 2,),
        in_specs=[tile(N_PROJ_COLS), _resident((8, D_CONV)), _resident((1, D_RET)),
                  pl.BlockSpec((None, 8, HEAD_DIM), lambda s: (pf(s), 0, 0)),
                  _resident((TM, HEAD_DIM)), _resident((TM, HEAD_DIM)), _resident((8, HEAD_DIM)),
                  _resident((RET_HEADS, CHUNK, CHUNK)), _resident((RET_HEADS, CHUNK, HEAD_DIM)),
                  _resident((RET_HEADS, CHUNK, HEAD_DIM)), _resident((RET_HEADS, 8, HEAD_DIM)),
                  xtile(D_MODEL), _resident1((D_MODEL, D_MODEL)), _resident((1, D_MODEL)), xtile(D_MODEL)],
        out_specs=[any_spec, any_spec, xtile(D_MODEL), xtile(D_MODEL),
                   _resident((2, N_CHIPS, 128, D_MODEL)), _resident((1, D_MODEL)), _resident((1, 128))],
        out_shape=[jax.ShapeDtypeStruct((rows, D_CONV), BF16),
                   jax.ShapeDtypeStruct(((nt + 1) * NCH, RET_HEADS, HEAD_DIM, HEAD_DIM), BF16),
                   jax.ShapeDtypeStruct((seq, D_MODEL), F32),
                   jax.ShapeDtypeStruct((seq, D_MODEL), BF16),
                   jax.ShapeDtypeStruct((2, N_CHIPS, 128, D_MODEL), F32),
                   jax.ShapeDtypeStruct((1, D_MODEL), F32),
                   jax.ShapeDtypeStruct((1, 128), F32)],
        scratch_shapes=[pltpu.VMEM((RET_HEADS, HEAD_DIM, HEAD_DIM), F32), pltpu.VMEM((8, D_CONV), F32),
                        pltpu.VMEM((2, TM, D_MODEL), BF16), pltpu.VMEM((2, TM, D_CONV), BF16),
                        pltpu.VMEM((2, NCH, RET_HEADS, HEAD_DIM, HEAD_DIM), BF16),
                        pltpu.VMEM((1, D_MODEL), F32), pltpu.SemaphoreType.DMA((4,))],
        compiler_params=_cparams(("arbitrary",)),
    )(proj, conv_w8, gret, tb["tt"], tb["cr2"], tb["sr2"], tb["sgn"], tb["decay"], tb["xi"], tb["zeta"], tb["cd"],
      x, w_out, fg, target)


def _b2_b1a_call(proj, dmixed, conv_s, states, conv_w8, gret, tb, w_in_g, x, meta_tile, g1, dh2):
    nt, rows = tb["nt"], tb["rows"]
    seq = nt * TM

    def pb(r):
        return jnp.where(r == nt, nt, nt - 1 - r)

    def xprev(r):
        return jnp.clip(nt - r, 0, nt - 1)

    def body(proj_ref, dmx_ref, conv_ref, states_ref, cw_ref, g_ref, tt_ref, cr_ref, sr_ref, sgn_ref, dec_ref,
             xi_ref, zeta_ref, cd_ref, w_ref, x_ref, mt_ref, g1_ref, dh2_ref,
             dproj_hbm, gcw_ref, gg_ref, gx_ref, dmeta_ref, gn_ref,
             gstate, dchalo, dps, out_sems):
        r = pl.program_id(0)
        live = jnp.where(r == nt, 0.0, 1.0)
        slot = lax.rem(r, 2)
        dproj_ref = dps.at[slot]

        class to_hbm:
            def __init__(self, s, tile):
                self.copies = [pltpu.make_async_copy(dps.at[s, :, j * 1024:(j + 1) * 1024], dproj_hbm.at[tile, j],
                                                     out_sems.at[N_CHIPS * s + j]) for j in range(N_CHIPS)]

            def start(self):
                for cp in self.copies:
                    cp.start()

            def wait(self):
                for cp in self.copies:
                    cp.wait()

        @pl.when(r == 0)
        def _():
            gstate[...] = jnp.zeros_like(gstate)
            dchalo[...] = jnp.zeros_like(dchalo)
            gcw_ref[...] = jnp.zeros_like(gcw_ref)
            gg_ref[...] = jnp.zeros_like(gg_ref)
            gn_ref[...] = jnp.zeros_like(gn_ref)
            dps[...] = jnp.zeros_like(dps)

        @pl.when(r >= 2)
        def _():
            to_hbm(slot, pb(r - 2)).wait()

        dprev = dps.at[1 - slot]
        pieces = []

        def emit_piece():
            j = len(pieces)
            if j < N_CHIPS:
                p = _dot_tb(dprev[:, j * 1024:(j + 1) * 1024], w_ref[j])
                pieces.append(p if j == 0 else pieces[-1] + p)

        cx = proj_ref[:, CX:CX + 512].astype(F32)
        cb = proj_ref[:, CB:CB + 512].astype(F32)
        cc = proj_ref[:, CC:CC + 512].astype(F32)
        cg = proj_ref[:, CG:CG + 512].astype(F32)
        dco = dmx_ref[:, 0:D_CONV].astype(F32) * live
        conv = conv_ref[...].astype(F32)
        sg = _sigmoid(cg)
        sil = cg * sg
        t = dco * conv
        dproj_ref[:, CB:CB + 512] = (t * sil).astype(BF16)
        dproj_ref[:, CG:CG + 512] = (t * cb * (sg * (1.0 + cg * (1.0 - sg)))).astype(BF16)
        dconv = dco * cb * sil
        row = lax.broadcasted_iota(jnp.int32, (TM, D_CONV), 0)
        n0 = dchalo[0:1, :]
        n1 = dchalo[1:2, :]
        dc1 = jnp.where(row == TM - 1, n0, pltpu.roll(dconv, TM - 1, 0))
        dc2 = jnp.where(row == TM - 2, n0, jnp.where(row == TM - 1, n1, pltpu.roll(dconv, TM - 2, 0)))
        dchalo[...] = dconv[0:8, :]
        du = cw_ref[2:3, :] * dconv + cw_ref[1:2, :] * dc1 + cw_ref[0:1, :] * dc2
        u = cc * cx
        gcw_ref[2:3, :] += jnp.sum(u * dconv, axis=0, keepdims=True)
        gcw_ref[1:2, :] += jnp.sum(u * dc1, axis=0, keepdims=True)
        gcw_ref[0:1, :] += jnp.sum(u * dc2, axis=0, keepdims=True)
        dproj_ref[:, CC:CC + 512] = (du * cx).astype(BF16)
        dproj_ref[:, CX:CX + 512] = (du * cc).astype(BF16)
        emit_piece()

        scale = HEAD_DIM ** -0.5
        gs = {h: gstate[h] for h in range(RET_HEADS)}
        gg = {h: jnp.zeros((1, HEAD_DIM), F32) for h in range(RET_HEADS)}
        col = lambda base, h: slice(base + h * HEAD_DIM, base + (h + 1) * HEAD_DIM)
        rw = lambda c: slice(c * CHUNK, (c + 1) * CHUNK)
        rc_t, rs_t = _tile_rotary(tt_ref, cr_ref, sr_ref, sgn_ref)
        for c0 in range(NCH - CHUNK_GROUP, -1, -CHUNK_GROUP):
            cs = range(c0 + CHUNK_GROUP - 1, c0 - 1, -1)
            U = [(c, h) for c in cs for h in range(RET_HEADS)]
            rc = {c: rc_t[rw(c), :] for c in cs}
            rs = {c: rs_t[rw(c), :] for c in cs}
            v = {(c, h): proj_ref[rw(c), col(CV, h)] for c, h in U}
            stb = {(c, h): states_ref[c, h] for c, h in U}
            qf = {(c, h): _rot(proj_ref[rw(c), col(CQ, h)].astype(F32), rc[c], rs[c]) * scale for c, h in U}
            kf = {(c, h): _rot(proj_ref[rw(c), col(CK, h)].astype(F32), rc[c], rs[c]) for c, h in U}
            qb = {u: qf[u].astype(BF16) for u in U}
            kb = {u: kf[u].astype(BF16) for u in U}
            qxb = {(c, h): (qf[c, h] * xi_ref[h]).astype(BF16) for c, h in U}
            kzb = {(c, h): (kf[c, h] * zeta_ref[h]).astype(BF16) for c, h in U}
            ab = {(c, h): (_dot_tb(qb[c, h], kb[c, h]) * dec_ref[h]).astype(BF16) for c, h in U}
            o = {u: _dot(ab[u], v[u]) + _dot(qxb[u], stb[u]) for u in U}
            emit_piece()
            dob = {}
            for c, h in U:
                mu = jnp.mean(o[c, h], axis=-1, keepdims=True)
                d = o[c, h] - mu
                var = jnp.mean(d * d, axis=-1, keepdims=True)
                rstd = lax.rsqrt(var + EPS)
                yh = d * rstd
                g = g_ref[:, col(0, h)]
                rg = proj_ref[rw(c), col(CR, h)].astype(F32)
                dro = dmx_ref[rw(c), col(D_CONV, h)].astype(F32) * live
                sg = _sigmoid(rg)
                dproj_ref[rw(c), col(CR, h)] = (dro * (yh * g) * (sg * (1.0 + rg * (1.0 - sg)))).astype(BF16)
                dret = dro * (rg * sg)
                gg[h] = gg[h] + jnp.sum(dret * yh, axis=0, keepdims=True)
                dyh = dret * g
                do = rstd * (dyh - jnp.mean(dyh, axis=-1, keepdims=True)
                             - yh * jnp.mean(dyh * yh, axis=-1, keepdims=True))
                dob[c, h] = do.astype(BF16)
            dv1 = {u: _dot_ta(ab[u], dob[u]) for u in U}
            ds = {(c, h): (_dot_tb(dob[c, h], v[c, h]) * dec_ref[h]).astype(BF16) for c, h in U}
            gup = {u: _dot_ta(qxb[u], dob[u]) for u in U}
            dq = {(c, h): _dot(ds[c, h], kb[c, h]) + _dot_tb(dob[c, h], stb[c, h]) * xi_ref[h] for c, h in U}
            dk1 = {u: _dot_ta(ds[u], qb[u]) for u in U}
            emit_piece()
            for c, h in U:
                gsb = gs[h].astype(BF16)
                dv = dv1[c, h] + _dot(kzb[c, h], gsb)
                dk = dk1[c, h] + _dot_tb(v[c, h], gsb) * zeta_ref[h]
                gs[h] = cd_ref[h, 0:1, :] * gs[h] + gup[c, h]
                dproj_ref[rw(c), col(CQ, h)] = (_rot_t(dq[c, h], rc[c], rs[c]) * scale).astype(BF16)
                dproj_ref[rw(c), col(CK, h)] = _rot_t(dk, rc[c], rs[c]).astype(BF16)
                dproj_ref[rw(c), col(CV, h)] = dv.astype(BF16)
        for h in range(RET_HEADS):
            gstate[h] = gs[h]
            gg_ref[:, col(0, h)] += gg[h]

        while len(pieces) < N_CHIPS:
            emit_piece()

        def norm_bwd(dhn, hx):
            ms = jnp.mean(hx * hx, axis=-1, keepdims=True)
            rstd1 = lax.rsqrt(ms + EPS)
            xh = hx * rstd1
            gn_ref[...] += jnp.sum(dhn * xh, axis=0, keepdims=True)
            dxh = dhn * g1_ref[...]
            return rstd1 * (dxh - xh * jnp.mean(dxh * xh, axis=-1, keepdims=True))

        gx_ref[...] = norm_bwd(pieces[-1], x_ref[...]) + dh2_ref[...]

        @pl.when(r < nt)
        def _():
            to_hbm(slot, pb(r)).start()

        @pl.when(r == nt)
        def _():
            to_hbm(slot, pb(r)).start()
            mrows = slice(TM - N_META, TM)
            d16 = dproj_ref[mrows, :]
            dhn16 = _dot_tb(d16[:, 0:1024], w_ref[0])
            for j in range(1, N_CHIPS):
                dhn16 += _dot_tb(d16[:, j * 1024:(j + 1) * 1024], w_ref[j])
            dmeta_ref[...] = norm_bwd(dhn16, mt_ref[mrows, :])
            to_hbm(1 - slot, pb(r - 1)).wait()
            to_hbm(slot, pb(r)).wait()

    tile = lambda w: pl.BlockSpec((TM, w), lambda r: (pb(r), 0))
    xtile = pl.BlockSpec((TM, D_MODEL), lambda r: (xprev(r), 0))
    return pl.pallas_call(
        body, name="b2_mixer_bwd_b1a_inproj_bwd_x",
        grid=(nt + 1,),
        in_specs=[tile(N_PROJ_COLS),
                  pl.BlockSpec((TM, D_MODEL), lambda r: (jnp.minimum(pb(r), nt - 1), 0)),
                  tile(D_CONV),
                  pl.BlockSpec((NCH, RET_HEADS, HEAD_DIM, HEAD_DIM), lambda r: (pb(r), 0, 0, 0)),
                  _resident((8, D_CONV)), _resident((1, D_RET)),
                  pl.BlockSpec((None, 8, HEAD_DIM), lambda r: (pb(r), 0, 0)),
                  _resident((TM, HEAD_DIM)), _resident((TM, HEAD_DIM)), _resident((8, HEAD_DIM)),
                  _resident((RET_HEADS, CHUNK, CHUNK)),
                  _resident((RET_HEADS, CHUNK, HEAD_DIM)),
                  _resident((RET_HEADS, CHUNK, HEAD_DIM)), _resident((RET_HEADS, 8, HEAD_DIM)),
                  _resident1((N_CHIPS, D_MODEL, 1024)), xtile, _resident1((TM, D_MODEL)), _resident((1, D_MODEL)),
                  xtile],
        out_specs=[pl.BlockSpec(memory_space=pl.ANY), _resident((8, D_CONV)), _resident((1, D_RET)),
                   xtile, _resident((N_META, D_MODEL)), _resident((1, D_MODEL))],
        out_shape=[jax.ShapeDtypeStruct((nt + 1, N_CHIPS, TM, 1024), BF16),
                   jax.ShapeDtypeStruct((8, D_CONV), F32),
                   jax.ShapeDtypeStruct((1, D_RET), F32),
                   jax.ShapeDtypeStruct((seq, D_MODEL), F32),
                   jax.ShapeDtypeStruct((N_META, D_MODEL), F32),
                   jax.ShapeDtypeStruct((1, D_MODEL), F32)],
        scratch_shapes=[pltpu.VMEM((RET_HEADS, HEAD_DIM, HEAD_DIM), F32), pltpu.VMEM((8, D_CONV), F32),
                        pltpu.VMEM((2, TM, N_PROJ_COLS), BF16), pltpu.SemaphoreType.DMA((2 * N_CHIPS,))],
        compiler_params=_cparams(("arbitrary",), vmem=VMEM_LIMIT_MAX),
    )(proj, dmixed, conv_s, states, conv_w8, gret, tb["tt"], tb["cr2"], tb["sr2"], tb["sgn"], tb["decay"],
      tb["xi"], tb["zeta"], tb["cd"], w_in_g, x, meta_tile, g1, dh2)


REL = (2, 1, 3)
SMALL_ROWS = 24
HALF_STEP = 4


def _rcopy(src, dst, send_sems, recv_sems, k, to):
    return pltpu.make_async_remote_copy(src_ref=src, dst_ref=dst, send_sem=send_sems.at[k],
                                        recv_sem=recv_sems.at[k], device_id=to, device_id_type=MESH_ID)


def _b1b_reduce_call(order, hnt, dproj, gwo, pack, nt):
    nk = nt + 1
    last = nk - 1
    any_spec = pl.BlockSpec(memory_space=pl.ANY)

    def body(order_ref, a_ref, b_ref, gwo_hbm, pack_hbm, gwin_hbm, gwout_hbm, tot_hbm,
             acc, sb, abuf, pb, bbuf, fin, go, ao, pbo, bo, fino, slots, totv, send_sems, recv_sems, loc_sems):
        jj, k = pl.program_id(0), pl.program_id(1)
        x, y, c = lax.axis_index("x"), lax.axis_index("y"), lax.axis_index("c")
        me, myid, sib = 2 * x + y, 4 * x + 2 * y + c, (x, y, 1 - c)
        rc = functools.partial(_rcopy, send_sems=send_sems, recv_sems=recv_sems)
        peers = [((1 - x) if r & 2 else x, (1 - y) if r & 1 else y, c) for r in REL]
        kids = [jnp.bitwise_xor(me, r) for r in REL]

        def dev_peer(r):
            return ((1 - x) if r & 4 else x, (1 - y) if r & 2 else y, (1 - c) if r & 1 else c)

        own_go = pltpu.make_async_copy(gwo_hbm.at[c], go, loc_sems.at[0])
        own_pack = pltpu.make_async_copy(pack_hbm, slots.at[0], loc_sems.at[1])
        wo_half = rc(gwo_hbm.at[1 - c], ao, k=8, to=sib)
        wo_part = [rc(pbo.at[kids[p]], bo.at[p], k=9 + p, to=peers[p]) for p in range(3)]
        sm = [rc(pack_hbm, slots.at[r], k=12 + r, to=dev_peer(r)) for r in range(1, N_DEV)]
        half = [rc(sb.at[j % 2, 1 - c], abuf.at[j], k=j, to=sib) for j in range(N_CHIPS)]
        part = [rc(pb.at[p], bbuf.at[p], k=4 + p, to=peers[p]) for p in range(3)]

        @pl.when(jnp.logical_and(jj == 0, k == 0))
        def _():
            own_go.start()
            own_pack.start()
            wo_half.start()
            for cp in sm:
                cp.start()

        @pl.when(k == 0)
        def _():
            acc[...] = jnp.zeros_like(acc)

        acc[0] += _dot(a_ref[0:512, :], b_ref[...])
        acc[1] += _dot(a_ref[512:1024, :], b_ref[...])

        @pl.when(k == HALF_STEP)
        def _():
            @pl.when(jj == 0)
            def _():
                own_go.wait()
                wo_half.wait_recv()
                for j in range(N_CHIPS):
                    go[j] = go[j] + ao[j]
                pbo[...] = go[...].astype(BF16)
                for cp in wo_part:
                    cp.start()

            for p in range(3):
                @pl.when(jj == p + 1)
                def _(p=p):
                    half[p].wait_recv()
                    half[p].wait_send()
                    pb[p] = (sb[p % 2, c] + abuf[p]).astype(BF16)
                    part[p].start()

        @pl.when(k == last)
        def _():
            for j in range(N_CHIPS):
                @pl.when(jj == j)
                def _(j=j):
                    sb[j % 2] = acc[...]
                    half[j].start()

        @pl.when(jnp.logical_and(jj == N_CHIPS - 1, k == last))
        def _():
            half[3].wait_recv()
            own = sb[1, c] + abuf[3]
            for cp in part:
                cp.wait_recv()
            fin[c] = ((own + bbuf[0].astype(F32)) + bbuf[1].astype(F32)) + bbuf[2].astype(F32)
            done = rc(fin.at[c], fin.at[c], k=7, to=sib)
            done.start()
            for cp in wo_part:
                cp.wait_recv()
            fino[c] = ((go[me] + bo[0].astype(F32)) + bo[1].astype(F32)) + bo[2].astype(F32)
            done_o = rc(fino.at[c], fino.at[c], k=12, to=sib)
            done_o.start()
            own_pack.wait()
            for cp in sm:
                cp.wait_recv()
            tot = slots[myid]
            for a in range(1, N_DEV):
                tot = tot + slots[jnp.bitwise_xor(myid, a)]
            totv[...] = tot
            out_t = pltpu.make_async_copy(totv, tot_hbm, loc_sems.at[1])
            out_t.start()
            rc(fin.at[1 - c], fin.at[1 - c], k=7, to=sib).wait_recv()
            out_w = pltpu.make_async_copy(fin, gwin_hbm, loc_sems.at[0])
            out_w.start()
            rc(fino.at[1 - c], fino.at[1 - c], k=12, to=sib).wait_recv()
            out_o = pltpu.make_async_copy(fino, gwout_hbm, loc_sems.at[2])
            out_o.start()
            for cp in [half[3]] + part + [done, wo_half] + wo_part + [done_o] + sm:
                cp.wait_send()
            out_t.wait()
            out_w.wait()
            out_o.wait()

    grid_spec = pltpu.PrefetchScalarGridSpec(
        num_scalar_prefetch=1,
        grid=(N_CHIPS, nk),
        in_specs=[pl.BlockSpec((None, D_MODEL, TM), lambda j, k, o: (k, 0, 0)),
                  pl.BlockSpec((None, None, TM, 1024), lambda j, k, o: (k, o[j], 0, 0)),
                  any_spec, any_spec],
        out_specs=[any_spec, any_spec, any_spec],
        scratch_shapes=[
            pltpu.VMEM((2, 512, 1024), F32),
            pltpu.VMEM((2, 2, 512, 1024), F32),
            pltpu.VMEM((N_CHIPS, 512, 1024), F32),
            pltpu.VMEM((3, 512, 1024), BF16),
            pltpu.VMEM((3, 512, 1024), BF16),
            pltpu.VMEM((2, 512, 1024), F32),
            pltpu.VMEM((N_CHIPS, 128, D_MODEL), F32),
            pltpu.VMEM((N_CHIPS, 128, D_MODEL), F32),
            pltpu.VMEM((N_CHIPS, 128, D_MODEL), BF16),
            pltpu.VMEM((3, 128, D_MODEL), BF16),
            pltpu.VMEM((2, 128, D_MODEL), F32),
            pltpu.VMEM((N_DEV, SMALL_ROWS, D_MODEL), F32),
            pltpu.VMEM((SMALL_ROWS, D_MODEL), F32),
            pltpu.SemaphoreType.DMA((20,)), pltpu.SemaphoreType.DMA((20,)), pltpu.SemaphoreType.DMA((3,))])
    return pl.pallas_call(
        body, name="b1b_inproj_bwd_w_reduce",
        grid_spec=grid_spec,
        out_shape=[jax.ShapeDtypeStruct((2, 512, 1024), F32),
                   jax.ShapeDtypeStruct((2, 128, D_MODEL), F32),
                   jax.ShapeDtypeStruct((SMALL_ROWS, D_MODEL), F32)],
        compiler_params=_cparams(("arbitrary", "arbitrary")),
    )(order, hnt, dproj, gwo, pack)


def _local_step(me, x, target, g1, gret, fg, win_sh, wout_sh, small_sh):
    seq = x.shape[0]
    tb = _tables(seq)
    nt = tb["nt"]
    g1r, gretr, fgr = g1.reshape(1, -1), gret.reshape(1, -1), fg.reshape(1, -1)
    order = jnp.stack([me, me ^ REL[0], me ^ REL[1], me ^ REL[2]]).astype(jnp.int32)

    proj, hnt, w_in_g, w_out_g, meta_tile, conv_w8 = _f1_gather_call(order, x, g1r, win_sh, wout_sh, small_sh, nt)
    w_out = w_out_g.reshape(D_MODEL, D_MODEL)
    conv_s, states, dh2, dmixed, g_wout, g_fg, loss = _f2_f3_call(proj, conv_w8, gretr, tb, x, w_out, fgr, target)
    dproj, g_cw8, g_gret, grad_x, g_meta, g_g1 = _b2_b1a_call(proj, dmixed, conv_s, states, conv_w8, gretr, tb,
                                                              w_in_g, x, meta_tile, g1r, dh2)
    return loss, grad_x, dict(w_out=g_wout, meta=g_meta, conv_w=g_cw8[0:3], norm1_g=g_g1,
                              ret_norm_g=g_gret, final_g=g_fg), hnt, dproj


def _adamw_update(w_ref, g_ref, m_ref, v_ref, d_ref, nm_ref, nv_ref):
    gg = g_ref[...]
    nm = ADAM_B1 * m_ref[...] + (1.0 - ADAM_B1) * gg
    nv = ADAM_B2 * v_ref[...] + (1.0 - ADAM_B2) * (gg * gg)
    m_hat = nm / (1.0 - ADAM_B1 ** ADAM_STEP)
    v_hat = nv / (1.0 - ADAM_B2 ** ADAM_STEP)
    d_ref[...] = -ADAM_LR * (m_hat / (jnp.sqrt(v_hat) + ADAM_EPS) + ADAM_WD * w_ref[...])
    nm_ref[...] = nm
    nv_ref[...] = nv


def _adamw_small_call(me, tot, ws, ms, vs):
    n = len(ws)

    def body(me_ref, tmeta_ref, tvec_ref, tconv_ref, *refs):
        ins, outs = refs[:3 * n], refs[3 * n:]
        g_refs, loss_ref, upd = outs[0:n], outs[n], outs[n + 1:]
        g_refs[0][...] = tmeta_ref[...]
        g_refs[1][...] = tvec_ref[0:1, :]
        g_refs[2][...] = tconv_ref[3:6, :]
        g_refs[3][...] = tvec_ref[2:3, 0:D_RET]
        g_refs[4][...] = tvec_ref[1:2, :]
        loss_ref[...] = tvec_ref[6:7, 0:1]
        for i in range(n):
            _adamw_update(ins[i], g_refs[i], ins[n + i], ins[2 * n + i], upd[i], upd[n + i], upd[2 * n + i])

    whole = lambda a: pl.BlockSpec(a.shape, lambda i, m: (0,) * a.ndim)
    shapes = [jax.ShapeDtypeStruct(w.shape, F32) for w in ws]
    out_shape = shapes + [jax.ShapeDtypeStruct((1, 1), F32)] + shapes * 3
    grid_spec = pltpu.PrefetchScalarGridSpec(
        num_scalar_prefetch=1, grid=(1,),
        in_specs=[pl.BlockSpec((N_META, 256), lambda i, m: (0, m[0])),
                  pl.BlockSpec((8, D_MODEL), lambda i, m: (N_META // 8, 0)),
                  pl.BlockSpec((8, 128), lambda i, m: (N_META // 8, m[0]))] + [whole(a) for a in ws + ms + vs],
        out_specs=[whole(s) for s in out_shape])
    outs = pl.pallas_call(body, name="adamw_small", grid_spec=grid_spec, out_shape=out_shape,
                          compiler_params=_cparams(("arbitrary",)))(me.reshape(1), tot, tot, tot, *ws, *ms, *vs)
    return outs[:n], outs[n], outs[n + 1:2 * n + 1], outs[2 * n + 1:3 * n + 1], outs[3 * n + 1:]


def _adamw_call(w, g, m, v, name):
    shape = w.shape
    w2, g2, m2, v2 = (a.reshape(-1, shape[-1]) for a in (w, g, m, v))
    rows, cols = w2.shape
    br = 256 if rows % 256 == 0 else rows
    body = functools.partial(_adamw_update)
    spec = pl.BlockSpec((br, cols), lambda i: (i, 0))
    outs = pl.pallas_call(
        body, name=name, grid=(rows // br,),
        in_specs=[spec] * 4, out_specs=[spec] * 3,
        out_shape=[jax.ShapeDtypeStruct((rows, cols), F32)] * 3,
        compiler_params=_cparams(("arbitrary",)),
    )(w2, g2, m2, v2)
    return tuple(o.reshape(shape) for o in outs)


def _pad_to(a, rows, cols):
    return jnp.pad(a, ((0, rows - a.shape[0]), (0, cols - a.shape[1])))


def kernel(x, meta, norm1_g, w_in, conv_w, ret_norm_g, w_out, final_g, loss_target, m_meta, m_norm1_g, m_w_in, m_conv_w, m_ret_norm_g, m_w_out, m_final_g, v_meta, v_norm1_g, v_w_in, v_conv_w, v_ret_norm_g, v_w_out, v_final_g):
    me = 2 * lax.axis_index("x") + lax.axis_index("y")

    small_sh = jnp.concatenate([meta, _pad_to(conv_w, 8, 256)], axis=0)
    loss, grad_x, g, hnt, dproj = _local_step(me, x[0], loss_target[0], norm1_g, ret_norm_g, final_g,
                                              w_in.astype(BF16), w_out.astype(BF16), small_sh)

    vec = jnp.concatenate([g["norm1_g"], g["final_g"], _pad_to(g["ret_norm_g"], 1, D_MODEL),
                           _pad_to(g["conv_w"], 3, D_MODEL), _pad_to(loss, 2, D_MODEL)], axis=0)
    pack = jnp.concatenate([g["meta"], vec], axis=0)
    order = jnp.stack([me ^ REL[0], me ^ REL[1], me ^ REL[2], me]).astype(jnp.int32)
    g_win, g_wout, tot = _b1b_reduce_call(order, hnt, dproj, g["w_out"], pack, x.shape[1] // TM)
    g_win, g_wout = g_win.reshape(D_MODEL, 1024), g_wout.reshape(256, D_MODEL)

    ws = [meta, norm1_g, w_in, conv_w, ret_norm_g, w_out, final_g]
    ms = [m_meta, m_norm1_g, m_w_in, m_conv_w, m_ret_norm_g, m_w_out, m_final_g]
    vs = [v_meta, v_norm1_g, v_w_in, v_conv_w, v_ret_norm_g, v_w_out, v_final_g]
    names = ["meta", "norm1_g", "w_in", "conv_w", "ret_norm_g", "w_out", "final_g"]
    as2d = lambda a: a.reshape(1, -1) if a.ndim == 1 else a
    big = {names.index("w_in"): g_win, names.index("w_out"): g_wout}
    small = [i for i in range(len(names)) if i not in big]
    grads, deltas, new_ms, new_vs = [None] * 7, [None] * 7, [None] * 7, [None] * 7
    for i, g_ in big.items():
        grads[i] = g_
        deltas[i], new_ms[i], new_vs[i] = _adamw_call(ws[i], g_, ms[i], vs[i], "adamw_" + names[i])
    sg, loss_tot, sd, sm_, sv = _adamw_small_call(me.astype(jnp.int32), tot,
                                                  *[[as2d(t[i]) for i in small] for t in (ws, ms, vs)])
    for j, i in enumerate(small):
        grads[i], deltas[i], new_ms[i], new_vs[i] = (o[j].reshape(ws[i].shape) for o in (sg, sd, sm_, sv))
    return (loss_tot.reshape(()), grad_x[None], *grads, *deltas, *new_ms, *new_vs)
```

```python
import functools

import jax
import jax.numpy as jnp
import numpy as np
from jax import lax
from jax.experimental import pallas as pl
from jax.experimental.pallas import tpu as pltpu

F32 = jnp.float32
BF16 = jnp.bfloat16

D_MODEL = 1024
N_META = 16
D_CONV = 512
D_RET = 512
RET_HEADS = 4
HEAD_DIM = 128
CHUNK = 128
N_PROJ_COLS = 4096
ROPE_BASE = 10000.0
EPS = 1e-6
N_CHIPS = 4
N_DEV = 8

ADAM_LR = 0.001
ADAM_B1 = 0.9
ADAM_B2 = 0.999
ADAM_EPS = 1e-08
ADAM_WD = 0.01
ADAM_STEP = 10

TM = 512
NCH = TM // CHUNK
CHUNK_GROUP = 2
VMEM_LIMIT = 56 * 1024 * 1024
VMEM_LIMIT_MAX = 63 * 1024 * 1024

CX, CB, CC, CG, CQ, CK, CV, CR = (i * 512 for i in range(8))

MESH_ID = pl.DeviceIdType.MESH


def _cparams(sem=None, vmem=VMEM_LIMIT, **kw):
    return pltpu.CompilerParams(dimension_semantics=sem, vmem_limit_bytes=vmem, **kw)


def _sigmoid(x):
    return 1.0 / (1.0 + jnp.exp(-x))


def _dot(a, b):
    return jnp.dot(a, b, preferred_element_type=F32)


def _dot_tb(a, b):
    return lax.dot_general(a, b, (((1,), (1,)), ((), ())), preferred_element_type=F32)


def _dot_ta(a, b):
    return lax.dot_general(a, b, (((0,), (0,)), ((), ())), preferred_element_type=F32)


def _resident(shape):
    nd = len(shape)
    return pl.BlockSpec(shape, lambda *_: (0,) * nd)


def _resident1(shape):
    nd = len(shape)
    return pl.BlockSpec(shape, lambda *_: (0,) * nd, pipeline_mode=pl.Buffered(1))


def _tables(seq):
    f32 = np.float32
    nt = seq // TM
    rows = seq + TM
    half = HEAD_DIM // 2
    freqs = (f32(1.0) / (f32(ROPE_BASE) ** (np.arange(half, dtype=f32) / f32(half)))).astype(f32)
    tile_start = np.concatenate([np.arange(nt, dtype=f32), -np.ones((1,), f32)]) * f32(TM)
    ang_t = tile_start[:, None] * freqs[None, :]
    ang_r = (np.arange(TM, dtype=f32) + f32(N_META))[:, None] * freqs[None, :]
    dup = lambda a: np.concatenate([a, a], axis=-1).astype(f32)
    tt = np.stack([dup(np.cos(ang_t)), dup(np.sin(ang_t))], axis=1)
    tt = np.pad(tt, ((0, 0), (0, 6), (0, 0)))
    cr2, sr2 = dup(np.cos(ang_r)), dup(np.sin(ang_r))
    sgn = np.concatenate([-np.ones((8, half), f32), np.ones((8, half), f32)], axis=-1)
    log_g = np.log(f32(1.0) - f32(2.0) ** (f32(-5.0) - np.arange(RET_HEADS, dtype=f32))).astype(f32)
    idx = np.arange(CHUNK, dtype=f32)
    diff = idx[:, None] - idx[None, :]
    decay = np.where(diff[None] >= 0, np.exp(diff[None] * log_g[:, None, None]), f32(0.0)).astype(f32)
    zeta = np.exp((f32(CHUNK - 1) - idx)[None, :] * log_g[:, None]).astype(f32)
    xi = np.exp((idx + f32(1.0))[None, :] * log_g[:, None]).astype(f32)
    cd = np.exp(f32(CHUNK) * log_g).astype(f32)
    zeta_b = np.broadcast_to(zeta[:, :, None], (RET_HEADS, CHUNK, HEAD_DIM))
    xi_b = np.broadcast_to(xi[:, :, None], (RET_HEADS, CHUNK, HEAD_DIM))
    cd_b = np.broadcast_to(cd[:, None, None], (RET_HEADS, 8, HEAD_DIM))
    tables = dict(tt=tt, cr2=cr2, sr2=sr2, sgn=sgn, decay=decay, zeta=zeta_b, xi=xi_b, cd=cd_b)
    return dict(nt=nt, rows=rows, **{k: jnp.asarray(np.ascontiguousarray(v, dtype=f32)) for k, v in tables.items()})


def _tile_rotary(tt_ref, cr_ref, sr_ref, sgn_ref):
    ct, st = tt_ref[0:1, :], tt_ref[1:2, :]
    cr, sr = cr_ref[...], sr_ref[...]
    return ct * cr - st * sr, (st * cr + ct * sr) * sgn_ref[0:1, :]


def _rot(t, rc, rs):
    return t * rc + pltpu.roll(t, HEAD_DIM // 2, 1) * rs


def _rot_t(dt, rc, rs):
    return dt * rc + pltpu.roll(dt * rs, HEAD_DIM // 2, 1)


def _f1_gather_call(order, x, g1, win_sh, wout_sh, small_sh, nt):
    nk = nt + 1
    rows = nk * TM
    any_spec = pl.BlockSpec(memory_space=pl.ANY)

    def body(order_ref, x_ref, g_ref, win_hbm, wout_hbm, sm_hbm,
             proj_ref, hnt_ref, wg_hbm, wog_hbm, mt_hbm, cw_hbm,
             wg, wog, smg, mt, cw, hbs, send_sems, recv_sems, loc_sems):
        jj, k = pl.program_id(0), pl.program_id(1)
        x, y, c = lax.axis_index("x"), lax.axis_index("y"), lax.axis_index("c")
        me, sib = 2 * x + y, (x, y, 1 - c)
        rc = functools.partial(_rcopy, send_sems=send_sems, recv_sems=recv_sems)
        peers = [((1 - x) if r & 2 else x, (1 - y) if r & 1 else y, c) for r in REL]
        kids = [jnp.bitwise_xor(me, r) for r in REL]
        hw, ho = pl.ds(c * 512, 512), pl.ds(c * 128, 128)
        hw2, ho2 = pl.ds((1 - c) * 512, 512), pl.ds((1 - c) * 128, 128)
        at = lambda j_, k_: jnp.logical_and(jj == j_, k == k_)

        sm_cp = [rc(sm_hbm, smg.at[me], k=p, to=peers[p]) for p in range(3)]
        win_cp = [rc(win_hbm.at[hw], wg.at[me, hw], k=3 + p, to=peers[p]) for p in range(3)]
        wout_cp = [rc(wout_hbm.at[ho], wog.at[me, ho], k=6 + p, to=peers[p]) for p in range(3)]
        sm_in = [rc(sm_hbm, smg.at[kids[p]], k=p, to=sib) for p in range(3)]
        win_in = [rc(win_hbm.at[hw], wg.at[kids[p], hw], k=3 + p, to=sib) for p in range(3)]
        wout_in = [rc(wout_hbm.at[ho], wog.at[kids[p], ho], k=6 + p, to=sib) for p in range(3)]
        win_fw = [rc(wg.at[kids[p], hw], wg.at[kids[p], hw], k=9 + p, to=sib) for p in range(3)]
        wout_fw = [rc(wog.at[kids[p], ho], wog.at[kids[p], ho], k=12 + p, to=sib) for p in range(3)]
        win_fw_in = [rc(wg.at[kids[p], hw2], wg.at[kids[p], hw2], k=9 + p, to=sib) for p in range(3)]
        wout_fw_in = [rc(wog.at[kids[p], ho2], wog.at[kids[p], ho2], k=12 + p, to=sib) for p in range(3)]
        own_w = pltpu.make_async_copy(win_hbm, wg.at[me], loc_sems.at[0])
        own_o = pltpu.make_async_copy(wout_hbm, wog.at[me], loc_sems.at[1])
        own_s = pltpu.make_async_copy(sm_hbm, smg.at[me], loc_sems.at[2])
        out_wg = pltpu.make_async_copy(wg, wg_hbm, loc_sems.at[3])
        out_wog = pltpu.make_async_copy(wog, wog_hbm, loc_sems.at[4])
        out_mt = pltpu.make_async_copy(mt, mt_hbm, loc_sems.at[5])
        out_cw = pltpu.make_async_copy(cw, cw_hbm, loc_sems.at[6])

        def pass_on(p):
            win_in[p].wait_recv()
            win_fw[p].start()

        @pl.when(k <= 1)
        def _():
            @pl.when(at(0, 0))
            def _():
                own_w.start()
                own_s.start()
                own_o.start()
                for cp in sm_cp + win_cp + wout_cp:
                    cp.start()
                own_w.wait()

            for p in range(3):
                @pl.when(at(p + 1, 0))
                def _(p=p):
                    win_fw_in[p].wait_recv()

            @pl.when(at(1, 1))
            def _():
                pass_on(1)

            @pl.when(at(3, 0))
            def _():
                out_wg.start()

            @pl.when(at(3, 1))
            def _():
                for p in range(3):
                    wout_in[p].wait_recv()
                    wout_fw[p].start()

        @pl.when(jnp.logical_and(jj == 0, k >= nk - 2))
        def _():
            @pl.when(k == nk - 2)
            def _():
                own_s.wait()
                for cp in sm_in:
                    cp.wait_recv()
                mt[...] = jnp.zeros_like(mt)
                cw[...] = jnp.zeros_like(cw)
                for j in range(N_CHIPS):
                    mt[TM - N_META:TM, j * 256:(j + 1) * 256] = smg[j, 0:N_META, :]
                    cw[0:3, j * 128:(j + 1) * 128] = smg[j, N_META:N_META + 3, 0:128]
                out_mt.start()
                out_cw.start()

            @pl.when(k == nk - 1)
            def _():
                pass_on(0)

        @pl.when(at(2, nk // 2))
        def _():
            pass_on(2)

        tile_rows = pl.ds(pl.multiple_of(k * TM, TM), TM)

        @pl.when(jj == 0)
        def _():
            h = jnp.where(k == nt, mt[...], x_ref[...])
            ms = jnp.mean(h * h, axis=-1, keepdims=True)
            hn = (h * lax.rsqrt(ms + EPS)) * g_ref[...]
            hb = hn.astype(BF16)
            hbs[tile_rows, :] = hb
            proj_ref[...] = _dot(hb, wg[order_ref[0]]).astype(BF16)
            hnt_ref[...] = hn.T.astype(BF16)

        @pl.when(jj > 0)
        def _():
            proj_ref[...] = _dot(hbs[tile_rows, :], wg[order_ref[jj]]).astype(BF16)

        @pl.when(at(3, nk - 1))
        def _():
            own_o.wait()
            for cp in wout_fw_in:
                cp.wait_recv()
            out_wog.start()
            for cp in sm_cp + win_cp + wout_cp + win_fw + wout_fw:
                cp.wait_send()
            for cp in (out_wg, out_wog, out_mt, out_cw):
                cp.wait()

    grid_spec = pltpu.PrefetchScalarGridSpec(
        num_scalar_prefetch=1,
        grid=(N_CHIPS, nk),
        in_specs=[pl.BlockSpec((TM, D_MODEL), lambda j, k, o: (jnp.where(j == 0, jnp.minimum(k, nt - 1), nt - 1), 0)),
                  pl.BlockSpec((1, D_MODEL), lambda j, k, o: (0, 0)),
                  any_spec, any_spec, any_spec],
        out_specs=[pl.BlockSpec((TM, 1024), lambda j, k, o: (k, o[j])),
                   pl.BlockSpec((None, D_MODEL, TM), lambda j, k, o: (jnp.where(j == 0, k, nk - 1), 0, 0)),
                   any_spec, any_spec, any_spec, any_spec],
        scratch_shapes=[
            pltpu.VMEM((N_CHIPS, D_MODEL, 1024), BF16),
            pltpu.VMEM((N_CHIPS, 256, D_MODEL), BF16),
            pltpu.VMEM((N_CHIPS, SMALL_ROWS, 256), F32),
            pltpu.VMEM((TM, D_MODEL), F32),
            pltpu.VMEM((8, D_CONV), F32),
            pltpu.VMEM((rows, D_MODEL), BF16),
            pltpu.SemaphoreType.DMA((15,)), pltpu.SemaphoreType.DMA((15,)), pltpu.SemaphoreType.DMA((7,))])
    return pl.pallas_call(
        body, name="f1_norm_inproj_gather",
        grid_spec=grid_spec,
        out_shape=[jax.ShapeDtypeStruct((rows, N_PROJ_COLS), BF16),
                   jax.ShapeDtypeStruct((nk, D_MODEL, TM), BF16),
                   jax.ShapeDtypeStruct((N_CHIPS, D_MODEL, 1024), BF16),
                   jax.ShapeDtypeStruct((N_CHIPS, 256, D_MODEL), BF16),
                   jax.ShapeDtypeStruct((TM, D_MODEL), F32),
                   jax.ShapeDtypeStruct((8, D_CONV), F32)],
        compiler_params=_cparams(("arbitrary", "arbitrary")),
    )(order, x, g1, win_sh, wout_sh, small_sh)


def _f2_f3_call(proj, conv_w8, gret, tb, x, w_out, fg, target):
    nt, rows = tb["nt"], tb["rows"]
    seq = nt * TM

    def pf(s):
        return jnp.where(s == 0, nt, jnp.minimum(s - 1, nt - 1))

    def xt(s):
        return jnp.clip(s - 2, 0, nt - 1)

    def body(proj_ref, cw_ref, g_ref, tt_ref, cr_ref, sr_ref, sgn_ref, dec_ref, xi_ref, zeta_ref, cd_ref,
             x_ref, w_ref, fg_ref, t_ref,
             conv_hbm, states_hbm, dh2_ref, dmx_ref, gwo_ref, gfg_ref, loss_ref,
             state, uhalo, mxs, convs, sts, lacc, out_sems):
        s = pl.program_id(0)
        slot = lax.rem(s, 2)
        mixed_ref = mxs.at[slot]
        conv_ref = convs.at[slot]
        states_ref = sts.at[slot]

        def conv_out(sl, tile):
            return pltpu.make_async_copy(convs.at[sl], conv_hbm.at[pl.ds(pl.multiple_of(tile * TM, TM), TM), :],
                                         out_sems.at[sl])

        def states_out(sl, tile):
            return pltpu.make_async_copy(sts.at[sl], states_hbm.at[pl.ds(pl.multiple_of(tile * NCH, NCH), NCH)],
                                         out_sems.at[2 + sl])

        @pl.when(s == 0)
        def _():
            state[...] = jnp.zeros_like(state)
            uhalo[...] = jnp.zeros_like(uhalo)
            mxs[...] = jnp.zeros_like(mxs)
            gwo_ref[...] = jnp.zeros_like(gwo_ref)
            gfg_ref[...] = jnp.zeros_like(gfg_ref)
            lacc[...] = jnp.zeros_like(lacc)

        @pl.when(s >= 2)
        def _():
            conv_out(slot, pf(s - 2)).wait()
            states_out(slot, pf(s - 2)).wait()

        valid = jnp.where(s >= 2, 1.0, 0.0)
        mx_prev = mxs.at[1 - slot]
        f3 = {}

        def f3_fwd():
            f3["h2"] = x_ref[...] + _dot(mx_prev[...], w_ref[...])

        def f3_loss():
            h2 = f3.pop("h2")
            ms = jnp.mean(h2 * h2, axis=-1, keepdims=True)
            rstd = lax.rsqrt(ms + EPS)
            yh = h2 * rstd
            g = fg_ref[...]
            e = (yh * g - t_ref[...]) * valid
            lacc[...] += jnp.sum(e * e, axis=0, keepdims=True)
            dy = e * (1.0 / D_MODEL)
            gfg_ref[...] += jnp.sum(dy * yh, axis=0, keepdims=True)
            dyh = dy * g
            dh2 = rstd * (dyh - yh * jnp.mean(dyh * yh, axis=-1, keepdims=True))
            dh2_ref[...] = dh2
            f3["db"] = dh2.astype(BF16)

        def f3_dmx():
            dmx_ref[...] = _dot_tb(f3["db"], w_ref[...]).astype(BF16)

        def f3_gw():
            gw = _dot_ta(mx_prev[...], f3["db"])
            for j in range(N_CHIPS):
                for hf in range(2):
                    r0 = j * 256 + hf * 128
                    gwo_ref[hf, j] += gw[r0:r0 + 128, :]

        cx = proj_ref[:, CX:CX + 512].astype(F32)
        cc = proj_ref[:, CC:CC + 512].astype(F32)
        u = cc * cx
        row = lax.broadcasted_iota(jnp.int32, (TM, D_CONV), 0)
        h7 = uhalo[7:8, :]
        h6 = uhalo[6:7, :]
        u1 = jnp.where(row == 0, h7, pltpu.roll(u, 1, 0))
        u2 = jnp.where(row == 0, h6, jnp.where(row == 1, h7, pltpu.roll(u, 2, 0)))
        conv = cw_ref[2:3, :] * u + cw_ref[1:2, :] * u1 + cw_ref[0:1, :] * u2
        uhalo[...] = u[TM - 8:TM, :]
        cb = proj_ref[:, CB:CB + 512].astype(F32)
        cg = proj_ref[:, CG:CG + 512].astype(F32)
        mixed_ref[:, 0:D_CONV] = (cb * conv * (cg * _sigmoid(cg))).astype(BF16)
        conv_ref[...] = conv.astype(BF16)
        f3_fwd()

        scale = HEAD_DIM ** -0.5
        H = range(RET_HEADS)
        st = [state[h] for h in H]
        between = [f3_loss, f3_dmx, f3_gw, None]
        rc_t, rs_t = _tile_rotary(tt_ref, cr_ref, sr_ref, sgn_ref)
        for c in range(NCH):
            r0 = c * CHUNK
            rc = rc_t[r0:r0 + CHUNK, :]
            rs = rs_t[r0:r0 + CHUNK, :]
            col = lambda base, h: slice(base + h * HEAD_DIM, base + (h + 1) * HEAD_DIM)
            rws = slice(r0, r0 + CHUNK)
            v = [proj_ref[rws, col(CV, h)] for h in H]
            qf = [_rot(proj_ref[rws, col(CQ, h)].astype(F32), rc, rs) * scale for h in H]
            kf = [_rot(proj_ref[rws, col(CK, h)].astype(F32), rc, rs) for h in H]
            stb = [t.astype(BF16) for t in st]
            for h in H:
                states_ref[c, h] = stb[h]
            a = [(_dot_tb(qf[h].astype(BF16), kf[h].astype(BF16)) * dec_ref[h]).astype(BF16) for h in H]
            o = [_dot(a[h], v[h]) + _dot((qf[h] * xi_ref[h]).astype(BF16), stb[h]) for h in H]
            st = [cd_ref[h, 0:1, :] * st[h] + _dot_ta((kf[h] * zeta_ref[h]).astype(BF16), v[h]) for h in H]
            for h in H:
                mu = jnp.mean(o[h], axis=-1, keepdims=True)
                d = o[h] - mu
                var = jnp.mean(d * d, axis=-1, keepdims=True)
                yh = d * lax.rsqrt(var + EPS)
                rg = proj_ref[rws, col(CR, h)].astype(F32)
                mixed_ref[rws, col(D_CONV, h)] = (yh * g_ref[:, col(0, h)] * (rg * _sigmoid(rg))).astype(BF16)
            if between[c] is not None:
                between[c]()
        for h in H:
            state[h] = st[h]

        @pl.when(s <= nt)
        def _():
            conv_out(slot, pf(s)).start()
            states_out(slot, pf(s)).start()

        @pl.when(s == nt + 1)
        def _():
            conv_out(1 - slot, pf(s - 1)).wait()
            states_out(1 - slot, pf(s - 1)).wait()
            tot = jnp.sum(lacc[...], axis=1, keepdims=True) * (0.5 / D_MODEL)
            loss_ref[...] = jnp.broadcast_to(tot, (1, 128))

    tile = lambda w: pl.BlockSpec((TM, w), lambda s: (pf(s), 0))
    xtile = lambda w: pl.BlockSpec((TM, w), lambda s: (xt(s), 0))
    any_spec = pl.BlockSpec(memory_space=pl.ANY)
    return pl.pallas_call(
        body, name="f2_mixer_fwd_f3_outproj_loss",
        grid=(nt + 2,),
        in_specs=[tile(N_PROJ_COLS), _resident((8, D_CONV)), _resident((1, D_RET)),
                  pl.BlockSpec((None, 8, HEAD_DIM), lambda s: (pf(s), 0, 0)),
                  _resident((TM, HEAD_DIM)), _resident((TM, HEAD_DIM)), _resident((8, HEAD_DIM)),
                  _resident((RET_HEADS, CHUNK, CHUNK)), _resident((RET_HEADS, CHUNK, HEAD_DIM)),
                  _resident((RET_HEADS, CHUNK, HEAD_DIM)), _resident((RET_HEADS, 8, HEAD_DIM)),
                  xtile(D_MODEL), _resident1((D_MODEL, D_MODEL)), _resident((1, D_MODEL)), xtile(D_MODEL)],
        out_specs=[any_spec, any_spec, xtile(D_MODEL), xtile(D_MODEL),
                   _resident((2, N_CHIPS, 128, D_MODEL)), _resident((1, D_MODEL)), _resident((1, 128))],
        out_shape=[jax.ShapeDtypeStruct((rows, D_CONV), BF16),
                   jax.ShapeDtypeStruct(((nt + 1) * NCH, RET_HEADS, HEAD_DIM, HEAD_DIM), BF16),
                   jax.ShapeDtypeStruct((seq, D_MODEL), F32),
                   jax.ShapeDtypeStruct((seq, D_MODEL), BF16),
                   jax.ShapeDtypeStruct((2, N_CHIPS, 128, D_MODEL), F32),
                   jax.ShapeDtypeStruct((1, D_MODEL), F32),
                   jax.ShapeDtypeStruct((1, 128), F32)],
        scratch_shapes=[pltpu.VMEM((RET_HEADS, HEAD_DIM, HEAD_DIM), F32), pltpu.VMEM((8, D_CONV), F32),
                        pltpu.VMEM((2, TM, D_MODEL), BF16), pltpu.VMEM((2, TM, D_CONV), BF16),
                        pltpu.VMEM((2, NCH, RET_HEADS, HEAD_DIM, HEAD_DIM), BF16),
                        pltpu.VMEM((1, D_MODEL), F32), pltpu.SemaphoreType.DMA((4,))],
        compiler_params=_cparams(("arbitrary",)),
    )(proj, conv_w8, gret, tb["tt"], tb["cr2"], tb["sr2"], tb["sgn"], tb["decay"], tb["xi"], tb["zeta"], tb["cd"],
      x, w_out, fg, target)


def _b2_b1a_call(proj, dmixed, conv_s, states, conv_w8, gret, tb, w_in_g, x, meta_tile, g1, dh2):
    nt, rows = tb["nt"], tb["rows"]
    seq = nt * TM

    def pb(r):
        return jnp.where(r == nt, nt, nt - 1 - r)

    def xprev(r):
        return jnp.clip(nt - r, 0, nt - 1)

    def body(proj_ref, dmx_ref, conv_ref, states_ref, cw_ref, g_ref, tt_ref, cr_ref, sr_ref, sgn_ref, dec_ref,
             xi_ref, zeta_ref, cd_ref, w_ref, x_ref, mt_ref, g1_ref, dh2_ref,
             dproj_hbm, gcw_ref, gg_ref, gx_ref, dmeta_ref, gn_ref,
             gstate, dchalo, dps, out_sems):
        r = pl.program_id(0)
        live = jnp.where(r == nt, 0.0, 1.0)
        slot = lax.rem(r, 2)
        dproj_ref = dps.at[slot]

        class to_hbm:
            def __init__(self, s, tile):
                self.copies = [pltpu.make_async_copy(dps.at[s, :, j * 1024:(j + 1) * 1024], dproj_hbm.at[tile, j],
                                                     out_sems.at[N_CHIPS * s + j]) for j in range(N_CHIPS)]

            def start(self):
                for cp in self.copies:
                    cp.start()

            def wait(self):
                for cp in self.copies:
                    cp.wait()

        @pl.when(r == 0)
        def _():
            gstate[...] = jnp.zeros_like(gstate)
            dchalo[...] = jnp.zeros_like(dchalo)
            gcw_ref[...] = jnp.zeros_like(gcw_ref)
            gg_ref[...] = jnp.zeros_like(gg_ref)
            gn_ref[...] = jnp.zeros_like(gn_ref)
            dps[...] = jnp.zeros_like(dps)

        @pl.when(r >= 2)
        def _():
            to_hbm(slot, pb(r - 2)).wait()

        dprev = dps.at[1 - slot]
        pieces = []

        def emit_piece():
            j = len(pieces)
            if j < N_CHIPS:
                p = _dot_tb(dprev[:, j * 1024:(j + 1) * 1024], w_ref[j])
                pieces.append(p if j == 0 else pieces[-1] + p)

        cx = proj_ref[:, CX:CX + 512].astype(F32)
        cb = proj_ref[:, CB:CB + 512].astype(F32)
        cc = proj_ref[:, CC:CC + 512].astype(F32)
        cg = proj_ref[:, CG:CG + 512].astype(F32)
        dco = dmx_ref[:, 0:D_CONV].astype(F32) * live
        conv = conv_ref[...].astype(F32)
        sg = _sigmoid(cg)
        sil = cg * sg
        t = dco * conv
        dproj_ref[:, CB:CB + 512] = (t * sil).astype(BF16)
        dproj_ref[:, CG:CG + 512] = (t * cb * (sg * (1.0 + cg * (1.0 - sg)))).astype(BF16)
        dconv = dco * cb * sil
        row = lax.broadcasted_iota(jnp.int32, (TM, D_CONV), 0)
        n0 = dchalo[0:1, :]
        n1 = dchalo[1:2, :]
        dc1 = jnp.where(row == TM - 1, n0, pltpu.roll(dconv, TM - 1, 0))
        dc2 = jnp.where(row == TM - 2, n0, jnp.where(row == TM - 1, n1, pltpu.roll(dconv, TM - 2, 0)))
        dchalo[...] = dconv[0:8, :]
        du = cw_ref[2:3, :] * dconv + cw_ref[1:2, :] * dc1 + cw_ref[0:1, :] * dc2
        u = cc * cx
        gcw_ref[2:3, :] += jnp.sum(u * dconv, axis=0, keepdims=True)
        gcw_ref[1:2, :] += jnp.sum(u * dc1, axis=0, keepdims=True)
        gcw_ref[0:1, :] += jnp.sum(u * dc2, axis=0, keepdims=True)
        dproj_ref[:, CC:CC + 512] = (du * cx).astype(BF16)
        dproj_ref[:, CX:CX + 512] = (du * cc).astype(BF16)
        emit_piece()

        scale = HEAD_DIM ** -0.5
        gs = {h: gstate[h] for h in range(RET_HEADS)}
        gg = {h: jnp.zeros((1, HEAD_DIM), F32) for h in range(RET_HEADS)}
        col = lambda base, h: slice(base + h * HEAD_DIM, base + (h + 1) * HEAD_DIM)
        rw = lambda c: slice(c * CHUNK, (c + 1) * CHUNK)
        rc_t, rs_t = _tile_rotary(tt_ref, cr_ref, sr_ref, sgn_ref)
        for c0 in range(NCH - CHUNK_GROUP, -1, -CHUNK_GROUP):
            cs = range(c0 + CHUNK_GROUP - 1, c0 - 1, -1)
            U = [(c, h) for c in cs for h in range(RET_HEADS)]
            rc = {c: rc_t[rw(c), :] for c in cs}
            rs = {c: rs_t[rw(c), :] for c in cs}
            v = {(c, h): proj_ref[rw(c), col(CV, h)] for c, h in U}
            stb = {(c, h): states_ref[c, h] for c, h in U}
            qf = {(c, h): _rot(proj_ref[rw(c), col(CQ, h)].astype(F32), rc[c], rs[c]) * scale for c, h in U}
            kf = {(c, h): _rot(proj_ref[rw(c), col(CK, h)].astype(F32), rc[c], rs[c]) for c, h in U}
            qb = {u: qf[u].astype(BF16) for u in U}
            kb = {u: kf[u].astype(BF16) for u in U}
            qxb = {(c, h): (qf[c, h] * xi_ref[h]).astype(BF16) for c, h in U}
            kzb = {(c, h): (kf[c, h] * zeta_ref[h]).astype(BF16) for c, h in U}
            ab = {(c, h): (_dot_tb(qb[c, h], kb[c, h]) * dec_ref[h]).astype(BF16) for c, h in U}
            o = {u: _dot(ab[u], v[u]) + _dot(qxb[u], stb[u]) for u in U}
            emit_piece()
            dob = {}
            for c, h in U:
                mu = jnp.mean(o[c, h], axis=-1, keepdims=True)
                d = o[c, h] - mu
                var = jnp.mean(d * d, axis=-1, keepdims=True)
                rstd = lax.rsqrt(var + EPS)
                yh = d * rstd
                g = g_ref[:, col(0, h)]
                rg = proj_ref[rw(c), col(CR, h)].astype(F32)
                dro = dmx_ref[rw(c), col(D_CONV, h)].astype(F32) * live
                sg = _sigmoid(rg)
                dproj_ref[rw(c), col(CR, h)] = (dro * (yh * g) * (sg * (1.0 + rg * (1.0 - sg)))).astype(BF16)
                dret = dro * (rg * sg)
                gg[h] = gg[h] + jnp.sum(dret * yh, axis=0, keepdims=True)
                dyh = dret * g
                do = rstd * (dyh - jnp.mean(dyh, axis=-1, keepdims=True)
                             - yh * jnp.mean(dyh * yh, axis=-1, keepdims=True))
                dob[c, h] = do.astype(BF16)
            dv1 = {u: _dot_ta(ab[u], dob[u]) for u in U}
            ds = {(c, h): (_dot_tb(dob[c, h], v[c, h]) * dec_ref[h]).astype(BF16) for c, h in U}
            gup = {u: _dot_ta(qxb[u], dob[u]) for u in U}
            dq = {(c, h): _dot(ds[c, h], kb[c, h]) + _dot_tb(dob[c, h], stb[c, h]) * xi_ref[h] for c, h in U}
            dk1 = {u: _dot_ta(ds[u], qb[u]) for u in U}
            emit_piece()
            for c, h in U:
                gsb = gs[h].astype(BF16)
                dv = dv1[c, h] + _dot(kzb[c, h], gsb)
                dk = dk1[c, h] + _dot_tb(v[c, h], gsb) * zeta_ref[h]
                gs[h] = cd_ref[h, 0:1, :] * gs[h] + gup[c, h]
                dproj_ref[rw(c), col(CQ, h)] = (_rot_t(dq[c, h], rc[c], rs[c]) * scale).astype(BF16)
                dproj_ref[rw(c), col(CK, h)] = _rot_t(dk, rc[c], rs[c]).astype(BF16)
                dproj_ref[rw(c), col(CV, h)] = dv.astype(BF16)
        for h in range(RET_HEADS):
            gstate[h] = gs[h]
            gg_ref[:, col(0, h)] += gg[h]

        while len(pieces) < N_CHIPS:
            emit_piece()

        def norm_bwd(dhn, hx):
            ms = jnp.mean(hx * hx, axis=-1, keepdims=True)
            rstd1 = lax.rsqrt(ms + EPS)
            xh = hx * rstd1
            gn_ref[...] += jnp.sum(dhn * xh, axis=0, keepdims=True)
            dxh = dhn * g1_ref[...]
            return rstd1 * (dxh - xh * jnp.mean(dxh * xh, axis=-1, keepdims=True))

        gx_ref[...] = norm_bwd(pieces[-1], x_ref[...]) + dh2_ref[...]

        @pl.when(r < nt)
        def _():
            to_hbm(slot, pb(r)).start()

        @pl.when(r == nt)
        def _():
            to_hbm(slot, pb(r)).start()
            mrows = slice(TM - N_META, TM)
            d16 = dproj_ref[mrows, :]
            dhn16 = _dot_tb(d16[:, 0:1024], w_ref[0])
            for j in range(1, N_CHIPS):
                dhn16 += _dot_tb(d16[:, j * 1024:(j + 1) * 1024], w_ref[j])
            dmeta_ref[...] = norm_bwd(dhn16, mt_ref[mrows, :])
            to_hbm(1 - slot, pb(r - 1)).wait()
            to_hbm(slot, pb(r)).wait()

    tile = lambda w: pl.BlockSpec((TM, w), lambda r: (pb(r), 0))
    xtile = pl.BlockSpec((TM, D_MODEL), lambda r: (xprev(r), 0))
    return pl.pallas_call(
        body, name="b2_mixer_bwd_b1a_inproj_bwd_x",
        grid=(nt + 1,),
        in_specs=[tile(N_PROJ_COLS),
                  pl.BlockSpec((TM, D_MODEL), lambda r: (jnp.minimum(pb(r), nt - 1), 0)),
                  tile(D_CONV),
                  pl.BlockSpec((NCH, RET_HEADS, HEAD_DIM, HEAD_DIM), lambda r: (pb(r), 0, 0, 0)),
                  _resident((8, D_CONV)), _resident((1, D_RET)),
                  pl.BlockSpec((None, 8, HEAD_DIM), lambda r: (pb(r), 0, 0)),
                  _resident((TM, HEAD_DIM)), _resident((TM, HEAD_DIM)), _resident((8, HEAD_DIM)),
                  _resident((RET_HEADS, CHUNK, CHUNK)),
                  _resident((RET_HEADS, CHUNK, HEAD_DIM)),
                  _resident((RET_HEADS, CHUNK, HEAD_DIM)), _resident((RET_HEADS, 8, HEAD_DIM)),
                  _resident1((N_CHIPS, D_MODEL, 1024)), xtile, _resident1((TM, D_MODEL)), _resident((1, D_MODEL)),
                  xtile],
        out_specs=[pl.BlockSpec(memory_space=pl.ANY), _resident((8, D_CONV)), _resident((1, D_RET)),
                   xtile, _resident((N_META, D_MODEL)), _resident((1, D_MODEL))],
        out_shape=[jax.ShapeDtypeStruct((nt + 1, N_CHIPS, TM, 1024), BF16),
                   jax.ShapeDtypeStruct((8, D_CONV), F32),
                   jax.ShapeDtypeStruct((1, D_RET), F32),
                   jax.ShapeDtypeStruct((seq, D_MODEL), F32),
                   jax.ShapeDtypeStruct((N_META, D_MODEL), F32),
                   jax.ShapeDtypeStruct((1, D_MODEL), F32)],
        scratch_shapes=[pltpu.VMEM((RET_HEADS, HEAD_DIM, HEAD_DIM), F32), pltpu.VMEM((8, D_CONV), F32),
                        pltpu.VMEM((2, TM, N_PROJ_COLS), BF16), pltpu.SemaphoreType.DMA((2 * N_CHIPS,))],
        compiler_params=_cparams(("arbitrary",), vmem=VMEM_LIMIT_MAX),
    )(proj, dmixed, conv_s, states, conv_w8, gret, tb["tt"], tb["cr2"], tb["sr2"], tb["sgn"], tb["decay"],
      tb["xi"], tb["zeta"], tb["cd"], w_in_g, x, meta_tile, g1, dh2)


REL = (2, 1, 3)
SMALL_ROWS = 24
HALF_STEP = 4


def _rcopy(src, dst, send_sems, recv_sems, k, to):
    return pltpu.make_async_remote_copy(src_ref=src, dst_ref=dst, send_sem=send_sems.at[k],
                                        recv_sem=recv_sems.at[k], device_id=to, device_id_type=MESH_ID)


def _b1b_reduce_call(order, hnt, dproj, gwo, pack, nt):
    nk = nt + 1
    last = nk - 1
    any_spec = pl.BlockSpec(memory_space=pl.ANY)

    def body(order_ref, a_ref, b_ref, gwo_hbm, pack_hbm, gwin_hbm, gwout_hbm, tot_hbm,
             acc, sb, abuf, pb, bbuf, fin, go, ao, pbo, bo, fino, slots, totv, send_sems, recv_sems, loc_sems):
        jj, k = pl.program_id(0), pl.program_id(1)
        x, y, c = lax.axis_index("x"), lax.axis_index("y"), lax.axis_index("c")
        me, myid, sib = 2 * x + y, 4 * x + 2 * y + c, (x, y, 1 - c)
        rc = functools.partial(_rcopy, send_sems=send_sems, recv_sems=recv_sems)
        peers = [((1 - x) if r & 2 else x, (1 - y) if r & 1 else y, c) for r in REL]
        kids = [jnp.bitwise_xor(me, r) for r in REL]

        def dev_peer(r):
            return ((1 - x) if r & 4 else x, (1 - y) if r & 2 else y, (1 - c) if r & 1 else c)

        own_go = pltpu.make_async_copy(gwo_hbm.at[c], go, loc_sems.at[0])
        own_pack = pltpu.make_async_copy(pack_hbm, slots.at[0], loc_sems.at[1])
        wo_half = rc(gwo_hbm.at[1 - c], ao, k=8, to=sib)
        wo_part = [rc(pbo.at[kids[p]], bo.at[p], k=9 + p, to=peers[p]) for p in range(3)]
        sm = [rc(pack_hbm, slots.at[r], k=12 + r, to=dev_peer(r)) for r in range(1, N_DEV)]
        half = [rc(sb.at[j % 2, 1 - c], abuf.at[j], k=j, to=sib) for j in range(N_CHIPS)]
        part = [rc(pb.at[p], bbuf.at[p], k=4 + p, to=peers[p]) for p in range(3)]

        @pl.when(jnp.logical_and(jj == 0, k == 0))
        def _():
            own_go.start()
            own_pack.start()
            wo_half.start()
            for cp in sm:
                cp.start()

        @pl.when(k == 0)
        def _():
            acc[...] = jnp.zeros_like(acc)

        acc[0] += _dot(a_ref[0:512, :], b_ref[...])
        acc[1] += _dot(a_ref[512:1024, :], b_ref[...])

        @pl.when(k == HALF_STEP)
        def _():
            @pl.when(jj == 0)
            def _():
                own_go.wait()
                wo_half.wait_recv()
                for j in range(N_CHIPS):
                    go[j] = go[j] + ao[j]
                pbo[...] = go[...].astype(BF16)
                for cp in wo_part:
                    cp.start()

            for p in range(3):
                @pl.when(jj == p + 1)
                def _(p=p):
                    half[p].wait_recv()
                    half[p].wait_send()
                    pb[p] = (sb[p % 2, c] + abuf[p]).astype(BF16)
                    part[p].start()

        @pl.when(k == last)
        def _():
            for j in range(N_CHIPS):
                @pl.when(jj == j)
                def _(j=j):
                    sb[j % 2] = acc[...]
                    half[j].start()

        @pl.when(jnp.logical_and(jj == N_CHIPS - 1, k == last))
        def _():
            half[3].wait_recv()
            own = sb[1, c] + abuf[3]
            for cp in part:
                cp.wait_recv()
            fin[c] = ((own + bbuf[0].astype(F32)) + bbuf[1].astype(F32)) + bbuf[2].astype(F32)
            done = rc(fin.at[c], fin.at[c], k=7, to=sib)
            done.start()
            for cp in wo_part:
                cp.wait_recv()
            fino[c] = ((go[me] + bo[0].astype(F32)) + bo[1].astype(F32)) + bo[2].astype(F32)
            done_o = rc(fino.at[c], fino.at[c], k=12, to=sib)
            done_o.start()
            own_pack.wait()
            for cp in sm:
                cp.wait_recv()
            tot = slots[myid]
            for a in range(1, N_DEV):
                tot = tot + slots[jnp.bitwise_xor(myid, a)]
            totv[...] = tot
            out_t = pltpu.make_async_copy(totv, tot_hbm, loc_sems.at[1])
            out_t.start()
            rc(fin.at[1 - c], fin.at[1 - c], k=7, to=sib).wait_recv()
            out_w = pltpu.make_async_copy(fin, gwin_hbm, loc_sems.at[0])
            out_w.start()
            rc(fino.at[1 - c], fino.at[1 - c], k=12, to=sib).wait_recv()
            out_o = pltpu.make_async_copy(fino, gwout_hbm, loc_sems.at[2])
            out_o.start()
            for cp in [half[3]] + part + [done, wo_half] + wo_part + [done_o] + sm:
                cp.wait_send()
            out_t.wait()
            out_w.wait()
            out_o.wait()

    grid_spec = pltpu.PrefetchScalarGridSpec(
        num_scalar_prefetch=1,
        grid=(N_CHIPS, nk),
        in_specs=[pl.BlockSpec((None, D_MODEL, TM), lambda j, k, o: (k, 0, 0)),
                  pl.BlockSpec((None, None, TM, 1024), lambda j, k, o: (k, o[j], 0, 0)),
                  any_spec, any_spec],
        out_specs=[any_spec, any_spec, any_spec],
        scratch_shapes=[
            pltpu.VMEM((2, 512, 1024), F32),
            pltpu.VMEM((2, 2, 512, 1024), F32),
            pltpu.VMEM((N_CHIPS, 512, 1024), F32),
            pltpu.VMEM((3, 512, 1024), BF16),
            pltpu.VMEM((3, 512, 1024), BF16),
            pltpu.VMEM((2, 512, 1024), F32),
            pltpu.VMEM((N_CHIPS, 128, D_MODEL), F32),
            pltpu.VMEM((N_CHIPS, 128, D_MODEL), F32),
            pltpu.VMEM((N_CHIPS, 128, D_MODEL), BF16),
            pltpu.VMEM((3, 128, D_MODEL), BF16),
            pltpu.VMEM((2, 128, D_MODEL), F32),
            pltpu.VMEM((N_DEV, SMALL_ROWS, D_MODEL), F32),
            pltpu.VMEM((SMALL_ROWS, D_MODEL), F32),
            pltpu.SemaphoreType.DMA((20,)), pltpu.SemaphoreType.DMA((20,)), pltpu.SemaphoreType.DMA((3,))])
    return pl.pallas_call(
        body, name="b1b_inproj_bwd_w_reduce",
        grid_spec=grid_spec,
        out_shape=[jax.ShapeDtypeStruct((2, 512, 1024), F32),
                   jax.ShapeDtypeStruct((2, 128, D_MODEL), F32),
                   jax.ShapeDtypeStruct((SMALL_ROWS, D_MODEL), F32)],
        compiler_params=_cparams(("arbitrary", "arbitrary")),
    )(order, hnt, dproj, gwo, pack)


def _local_step(me, x, target, g1, gret, fg, win_sh, wout_sh, small_sh):
    seq = x.shape[0]
    tb = _tables(seq)
    nt = tb["nt"]
    g1r, gretr, fgr = g1.reshape(1, -1), gret.reshape(1, -1), fg.reshape(1, -1)
    order = jnp.stack([me, me ^ REL[0], me ^ REL[1], me ^ REL[2]]).astype(jnp.int32)

    proj, hnt, w_in_g, w_out_g, meta_tile, conv_w8 = _f1_gather_call(order, x, g1r, win_sh, wout_sh, small_sh, nt)
    w_out = w_out_g.reshape(D_MODEL, D_MODEL)
    conv_s, states, dh2, dmixed, g_wout, g_fg, loss = _f2_f3_call(proj, conv_w8, gretr, tb, x, w_out, fgr, target)
    dproj, g_cw8, g_gret, grad_x, g_meta, g_g1 = _b2_b1a_call(proj, dmixed, conv_s, states, conv_w8, gretr, tb,
                                                              w_in_g, x, meta_tile, g1r, dh2)
    return loss, grad_x, dict(w_out=g_wout, meta=g_meta, conv_w=g_cw8[0:3], norm1_g=g_g1,
                              ret_norm_g=g_gret, final_g=g_fg), hnt, dproj


def _adamw_update(w_ref, g_ref, m_ref, v_ref, d_ref, nm_ref, nv_ref):
    gg = g_ref[...]
    nm = ADAM_B1 * m_ref[...] + (1.0 - ADAM_B1) * gg
    nv = ADAM_B2 * v_ref[...] + (1.0 - ADAM_B2) * (gg * gg)
    m_hat = nm / (1.0 - ADAM_B1 ** ADAM_STEP)
    v_hat = nv / (1.0 - ADAM_B2 ** ADAM_STEP)
    d_ref[...] = -ADAM_LR * (m_hat / (jnp.sqrt(v_hat) + ADAM_EPS) + ADAM_WD * w_ref[...])
    nm_ref[...] = nm
    nv_ref[...] = nv


def _adamw_small_call(me, tot, ws, ms, vs):
    n = len(ws)

    def body(me_ref, tmeta_ref, tvec_ref, tconv_ref, *refs):
        ins, outs = refs[:3 * n], refs[3 * n:]
        g_refs, loss_ref, upd = outs[0:n], outs[n], outs[n + 1:]
        g_refs[0][...] = tmeta_ref[...]
        g_refs[1][...] = tvec_ref[0:1, :]
        g_refs[2][...] = tconv_ref[3:6, :]
        g_refs[3][...] = tvec_ref[2:3, 0:D_RET]
        g_refs[4][...] = tvec_ref[1:2, :]
        loss_ref[...] = tvec_ref[6:7, 0:1]
        for i in range(n):
            _adamw_update(ins[i], g_refs[i], ins[n + i], ins[2 * n + i], upd[i], upd[n + i], upd[2 * n + i])

    whole = lambda a: pl.BlockSpec(a.shape, lambda i, m: (0,) * a.ndim)
    shapes = [jax.ShapeDtypeStruct(w.shape, F32) for w in ws]
    out_shape = shapes + [jax.ShapeDtypeStruct((1, 1), F32)] + shapes * 3
    grid_spec = pltpu.PrefetchScalarGridSpec(
        num_scalar_prefetch=1, grid=(1,),
        in_specs=[pl.BlockSpec((N_META, 256), lambda i, m: (0, m[0])),
                  pl.BlockSpec((8, D_MODEL), lambda i, m: (N_META // 8, 0)),
                  pl.BlockSpec((8, 128), lambda i, m: (N_META // 8, m[0]))] + [whole(a) for a in ws + ms + vs],
        out_specs=[whole(s) for s in out_shape])
    outs = pl.pallas_call(body, name="adamw_small", grid_spec=grid_spec, out_shape=out_shape,
                          compiler_params=_cparams(("arbitrary",)))(me.reshape(1), tot, tot, tot, *ws, *ms, *vs)
    return outs[:n], outs[n], outs[n + 1:2 * n + 1], outs[2 * n + 1:3 * n + 1], outs[3 * n + 1:]


def _adamw_call(w, g, m, v, name):
    shape = w.shape
    w2, g2, m2, v2 = (a.reshape(-1, shape[-1]) for a in (w, g, m, v))
    rows, cols = w2.shape
    br = 256 if rows % 256 == 0 else rows
    body = functools.partial(_adamw_update)
    spec = pl.BlockSpec((br, cols), lambda i: (i, 0))
    outs = pl.pallas_call(
        body, name=name, grid=(rows // br,),
        in_specs=[spec] * 4, out_specs=[spec] * 3,
        out_shape=[jax.ShapeDtypeStruct((rows, cols), F32)] * 3,
        compiler_params=_cparams(("arbitrary",)),
    )(w2, g2, m2, v2)
    return tuple(o.reshape(shape) for o in outs)


def _pad_to(a, rows, cols):
    return jnp.pad(a, ((0, rows - a.shape[0]), (0, cols - a.shape[1])))


def kernel(x, meta, norm1_g, w_in, conv_w, ret_norm_g, w_out, final_g, loss_target, m_meta, m_norm1_g, m_w_in, m_conv_w, m_ret_norm_g, m_w_out, m_final_g, v_meta, v_norm1_g, v_w_in, v_conv_w, v_ret_norm_g, v_w_out, v_final_g):
    me = 2 * lax.axis_index("x") + lax.axis_index("y")

    small_sh = jnp.concatenate([meta, _pad_to(conv_w, 8, 256)], axis=0)
    loss, grad_x, g, hnt, dproj = _local_step(me, x[0], loss_target[0], norm1_g, ret_norm_g, final_g,
                                              w_in.astype(BF16), w_out.astype(BF16), small_sh)

    vec = jnp.concatenate([g["norm1_g"], g["final_g"], _pad_to(g["ret_norm_g"], 1, D_MODEL),
                           _pad_to(g["conv_w"], 3, D_MODEL), _pad_to(loss, 2, D_MODEL)], axis=0)
    pack = jnp.concatenate([g["meta"], vec], axis=0)
    order = jnp.stack([me ^ REL[0], me ^ REL[1], me ^ REL[2], me]).astype(jnp.int32)
    g_win, g_wout, tot = _b1b_reduce_call(order, hnt, dproj, g["w_out"], pack, x.shape[1] // TM)
    g_win, g_wout = g_win.reshape(D_MODEL, 1024), g_wout.reshape(256, D_MODEL)

    ws = [meta, norm1_g, w_in, conv_w, ret_norm_g, w_out, final_g]
    ms = [m_meta, m_norm1_g, m_w_in, m_conv_w, m_ret_norm_g, m_w_out, m_final_g]
    vs = [v_meta, v_norm1_g, v_w_in, v_conv_w, v_ret_norm_g, v_w_out, v_final_g]
    names = ["meta", "norm1_g", "w_in", "conv_w", "ret_norm_g", "w_out", "final_g"]
    as2d = lambda a: a.reshape(1, -1) if a.ndim == 1 else a
    big = {names.index("w_in"): g_win, names.index("w_out"): g_wout}
    small = [i for i in range(len(names)) if i not in big]
    grads, deltas, new_ms, new_vs = [None] * 7, [None] * 7, [None] * 7, [None] * 7
    for i, g_ in big.items():
        grads[i] = g_
        deltas[i], new_ms[i], new_vs[i] = _adamw_call(ws[i], g_, ms[i], vs[i], "adamw_" + names[i])
    sg, loss_tot, sd, sm_, sv = _adamw_small_call(me.astype(jnp.int32), tot,
                                                  *[[as2d(t[i]) for i in small] for t in (ws, ms, vs)])
    for j, i in enumerate(small):
        grads[i], deltas[i], new_ms[i], new_vs[i] = (o[j].reshape(ws[i].shape) for o in (sg, sd, sm_, sv))
    return (loss_tot.reshape(()), grad_x[None], *grads, *deltas, *new_ms, *new_vs)
```

```python
import functools

import jax
import jax.numpy as jnp
import numpy as np
from jax import lax
from jax.experimental import pallas as pl
from jax.experimental.pallas import tpu as pltpu

F32 = jnp.float32
BF16 = jnp.bfloat16

D_MODEL = 1024
N_META = 16
D_CONV = 512
D_RET = 512
RET_HEADS = 4
HEAD_DIM = 128
CHUNK = 128
N_PROJ_COLS = 4096
ROPE_BASE = 10000.0
EPS = 1e-6
N_CHIPS = 4
N_DEV = 8

ADAM_LR = 0.001
ADAM_B1 = 0.9
ADAM_B2 = 0.999
ADAM_EPS = 1e-08
ADAM_WD = 0.01
ADAM_STEP = 10

TM = 512
NCH = TM // CHUNK
CHUNK_GROUP = 2
VMEM_LIMIT = 56 * 1024 * 1024
VMEM_LIMIT_MAX = 63 * 1024 * 1024

CX, CB, CC, CG, CQ, CK, CV, CR = (i * 512 for i in range(8))

MESH_ID = pl.DeviceIdType.MESH


def _cparams(sem=None, vmem=VMEM_LIMIT, **kw):
    return pltpu.CompilerParams(dimension_semantics=sem, vmem_limit_bytes=vmem, **kw)


def _sigmoid(x):
    return 1.0 / (1.0 + jnp.exp(-x))


def _dot(a, b):
    return jnp.dot(a, b, preferred_element_type=F32)


def _dot_tb(a, b):
    return lax.dot_general(a, b, (((1,), (1,)), ((), ())), preferred_element_type=F32)


def _dot_ta(a, b):
    return lax.dot_general(a, b, (((0,), (0,)), ((), ())), preferred_element_type=F32)


def _resident(shape):
    nd = len(shape)
    return pl.BlockSpec(shape, lambda *_: (0,) * nd)


def _resident1(shape):
    nd = len(shape)
    return pl.BlockSpec(shape, lambda *_: (0,) * nd, pipeline_mode=pl.Buffered(1))


def _tables(seq):
    f32 = np.float32
    nt = seq // TM
    rows = seq + TM
    half = HEAD_DIM // 2
    freqs = (f32(1.0) / (f32(ROPE_BASE) ** (np.arange(half, dtype=f32) / f32(half)))).astype(f32)
    tile_start = np.concatenate([np.arange(nt, dtype=f32), -np.ones((1,), f32)]) * f32(TM)
    ang_t = tile_start[:, None] * freqs[None, :]
    ang_r = (np.arange(TM, dtype=f32) + f32(N_META))[:, None] * freqs[None, :]
    dup = lambda a: np.concatenate([a, a], axis=-1).astype(f32)
    tt = np.stack([dup(np.cos(ang_t)), dup(np.sin(ang_t))], axis=1)
    tt = np.pad(tt, ((0, 0), (0, 6), (0, 0)))
    cr2, sr2 = dup(np.cos(ang_r)), dup(np.sin(ang_r))
    sgn = np.concatenate([-np.ones((8, half), f32), np.ones((8, half), f32)], axis=-1)
    log_g = np.log(f32(1.0) - f32(2.0) ** (f32(-5.0) - np.arange(RET_HEADS, dtype=f32))).astype(f32)
    idx = np.arange(CHUNK, dtype=f32)
    diff = idx[:, None] - idx[None, :]
    decay = np.where(diff[None] >= 0, np.exp(diff[None] * log_g[:, None, None]), f32(0.0)).astype(f32)
    zeta = np.exp((f32(CHUNK - 1) - idx)[None, :] * log_g[:, None]).astype(f32)
    xi = np.exp((idx + f32(1.0))[None, :] * log_g[:, None]).astype(f32)
    cd = np.exp(f32(CHUNK) * log_g).astype(f32)
    zeta_b = np.broadcast_to(zeta[:, :, None], (RET_HEADS, CHUNK, HEAD_DIM))
    xi_b = np.broadcast_to(xi[:, :, None], (RET_HEADS, CHUNK, HEAD_DIM))
    cd_b = np.broadcast_to(cd[:, None, None], (RET_HEADS, 8, HEAD_DIM))
    tables = dict(tt=tt, cr2=cr2, sr2=sr2, sgn=sgn, decay=decay, zeta=zeta_b, xi=xi_b, cd=cd_b)
    return dict(nt=nt, rows=rows, **{k: jnp.asarray(np.ascontiguousarray(v, dtype=f32)) for k, v in tables.items()})


def _tile_rotary(tt_ref, cr_ref, sr_ref, sgn_ref):
    ct, st = tt_ref[0:1, :], tt_ref[1:2, :]
    cr, sr = cr_ref[...], sr_ref[...]
    return ct * cr - st * sr, (st * cr + ct * sr) * sgn_ref[0:1, :]


def _rot(t, rc, rs):
    return t * rc + pltpu.roll(t, HEAD_DIM // 2, 1) * rs


def _rot_t(dt, rc, rs):
    return dt * rc + pltpu.roll(dt * rs, HEAD_DIM // 2, 1)


def _f1_gather_call(order, x, g1, win_sh, wout_sh, small_sh, nt):
    nk = nt + 1
    rows = nk * TM
    any_spec = pl.BlockSpec(memory_space=pl.ANY)

    def body(order_ref, x_ref, g_ref, win_hbm, wout_hbm, sm_hbm,
             proj_ref, hnt_ref, wg_hbm, wog_hbm, mt_hbm, cw_hbm,
             wg, wog, smg, mt, cw, hbs, send_sems, recv_sems, loc_sems):
        jj, k = pl.program_id(0), pl.program_id(1)
        x, y, c = lax.axis_index("x"), lax.axis_index("y"), lax.axis_index("c")
        me, sib = 2 * x + y, (x, y, 1 - c)
        rc = functools.partial(_rcopy, send_sems=send_sems, recv_sems=recv_sems)
        peers = [((1 - x) if r & 2 else x, (1 - y) if r & 1 else y, c) for r in REL]
        kids = [jnp.bitwise_xor(me, r) for r in REL]
        hw, ho = pl.ds(c * 512, 512), pl.ds(c * 128, 128)
        hw2, ho2 = pl.ds((1 - c) * 512, 512), pl.ds((1 - c) * 128, 128)
        at = lambda j_, k_: jnp.logical_and(jj == j_, k == k_)

        sm_cp = [rc(sm_hbm, smg.at[me], k=p, to=peers[p]) for p in range(3)]
        win_cp = [rc(win_hbm.at[hw], wg.at[me, hw], k=3 + p, to=peers[p]) for p in range(3)]
        wout_cp = [rc(wout_hbm.at[ho], wog.at[me, ho], k=6 + p, to=peers[p]) for p in range(3)]
        sm_in = [rc(sm_hbm, smg.at[kids[p]], k=p, to=sib) for p in range(3)]
        win_in = [rc(win_hbm.at[hw], wg.at[kids[p], hw], k=3 + p, to=sib) for p in range(3)]
        wout_in = [rc(wout_hbm.at[ho], wog.at[kids[p], ho], k=6 + p, to=sib) for p in range(3)]
        win_fw = [rc(wg.at[kids[p], hw], wg.at[kids[p], hw], k=9 + p, to=sib) for p in range(3)]
        wout_fw = [rc(wog.at[kids[p], ho], wog.at[kids[p], ho], k=12 + p, to=sib) for p in range(3)]
        win_fw_in = [rc(wg.at[kids[p], hw2], wg.at[kids[p], hw2], k=9 + p, to=sib) for p in range(3)]
        wout_fw_in = [rc(wog.at[kids[p], ho2], wog.at[kids[p], ho2], k=12 + p, to=sib) for p in range(3)]
        own_w = pltpu.make_async_copy(win_hbm, wg.at[me], loc_sems.at[0])
        own_o = pltpu.make_async_copy(wout_hbm, wog.at[me], loc_sems.at[1])
        own_s = pltpu.make_async_copy(sm_hbm, smg.at[me], loc_sems.at[2])
        out_wg = pltpu.make_async_copy(wg, wg_hbm, loc_sems.at[3])
        out_wog = pltpu.make_async_copy(wog, wog_hbm, loc_sems.at[4])
        out_mt = pltpu.make_async_copy(mt, mt_hbm, loc_sems.at[5])
        out_cw = pltpu.make_async_copy(cw, cw_hbm, loc_sems.at[6])

        def pass_on(p):
            win_in[p].wait_recv()
            win_fw[p].start()

        @pl.when(k <= 1)
        def _():
            @pl.when(at(0, 0))
            def _():
                own_w.start()
                own_s.start()
                own_o.start()
                for cp in sm_cp + win_cp + wout_cp:
                    cp.start()
                own_w.wait()

            for p in range(3):
                @pl.when(at(p + 1, 0))
                def _(p=p):
                    win_fw_in[p].wait_recv()

            @pl.when(at(1, 1))
            def _():
                pass_on(1)

            @pl.when(at(3, 0))
            def _():
                out_wg.start()

            @pl.when(at(3, 1))
            def _():
                for p in range(3):
                    wout_in[p].wait_recv()
                    wout_fw[p].start()

        @pl.when(jnp.logical_and(jj == 0, k >= nk - 2))
        def _():
            @pl.when(k == nk - 2)
            def _():
                own_s.wait()
                for cp in sm_in:
                    cp.wait_recv()
                mt[...] = jnp.zeros_like(mt)
                cw[...] = jnp.zeros_like(cw)
                for j in range(N_CHIPS):
                    mt[TM - N_META:TM, j * 256:(j + 1) * 256] = smg[j, 0:N_META, :]
                    cw[0:3, j * 128:(j + 1) * 128] = smg[j, N_META:N_META + 3, 0:128]
                out_mt.start()
                out_cw.start()

            @pl.when(k == nk - 1)
            def _():
                pass_on(0)

        @pl.when(at(2, nk // 2))
        def _():
            pass_on(2)

        tile_rows = pl.ds(pl.multiple_of(k * TM, TM), TM)

        @pl.when(jj == 0)
        def _():
            h = jnp.where(k == nt, mt[...], x_ref[...])
            ms = jnp.mean(h * h, axis=-1, keepdims=True)
            hn = (h * lax.rsqrt(ms + EPS)) * g_ref[...]
            hb = hn.astype(BF16)
            hbs[tile_rows, :] = hb
            proj_ref[...] = _dot(hb, wg[order_ref[0]]).astype(BF16)
            hnt_ref[...] = hn.T.astype(BF16)

        @pl.when(jj > 0)
        def _():
            proj_ref[...] = _dot(hbs[tile_rows, :], wg[order_ref[jj]]).astype(BF16)

        @pl.when(at(3, nk - 1))
        def _():
            own_o.wait()
            for cp in wout_fw_in:
                cp.wait_recv()
            out_wog.start()
            for cp in sm_cp + win_cp + wout_cp + win_fw + wout_fw:
                cp.wait_send()
            for cp in (out_wg, out_wog, out_mt, out_cw):
                cp.wait()

    grid_spec = pltpu.PrefetchScalarGridSpec(
        num_scalar_prefetch=1,
        grid=(N_CHIPS, nk),
        in_specs=[pl.BlockSpec((TM, D_MODEL), lambda j, k, o: (jnp.where(j == 0, jnp.minimum(k, nt - 1), nt - 1), 0)),
                  pl.BlockSpec((1, D_MODEL), lambda j, k, o: (0, 0)),
                  any_spec, any_spec, any_spec],
        out_specs=[pl.BlockSpec((TM, 1024), lambda j, k, o: (k, o[j])),
                   pl.BlockSpec((None, D_MODEL, TM), lambda j, k, o: (jnp.where(j == 0, k, nk - 1), 0, 0)),
                   any_spec, any_spec, any_spec, any_spec],
        scratch_shapes=[
            pltpu.VMEM((N_CHIPS, D_MODEL, 1024), BF16),
            pltpu.VMEM((N_CHIPS, 256, D_MODEL), BF16),
            pltpu.VMEM((N_CHIPS, SMALL_ROWS, 256), F32),
            pltpu.VMEM((TM, D_MODEL), F32),
            pltpu.VMEM((8, D_CONV), F32),
            pltpu.VMEM((rows, D_MODEL), BF16),
            pltpu.SemaphoreType.DMA((15,)), pltpu.SemaphoreType.DMA((15,)), pltpu.SemaphoreType.DMA((7,))])
    return pl.pallas_call(
        body, name="f1_norm_inproj_gather",
        grid_spec=grid_spec,
        out_shape=[jax.ShapeDtypeStruct((rows, N_PROJ_COLS), BF16),
                   jax.ShapeDtypeStruct((nk, D_MODEL, TM), BF16),
                   jax.ShapeDtypeStruct((N_CHIPS, D_MODEL, 1024), BF16),
                   jax.ShapeDtypeStruct((N_CHIPS, 256, D_MODEL), BF16),
                   jax.ShapeDtypeStruct((TM, D_MODEL), F32),
                   jax.ShapeDtypeStruct((8, D_CONV), F32)],
        compiler_params=_cparams(("arbitrary", "arbitrary")),
    )(order, x, g1, win_sh, wout_sh, small_sh)


def _f2_f3_call(proj, conv_w8, gret, tb, x, w_out, fg, target):
    nt, rows = tb["nt"], tb["rows"]
    seq = nt * TM

    def pf(s):
        return jnp.where(s == 0, nt, jnp.minimum(s - 1, nt - 1))

    def xt(s):
        return jnp.clip(s - 2, 0, nt - 1)

    def body(proj_ref, cw_ref, g_ref, tt_ref, cr_ref, sr_ref, sgn_ref, dec_ref, xi_ref, zeta_ref, cd_ref,
             x_ref, w_ref, fg_ref, t_ref,
             conv_hbm, states_hbm, dh2_ref, dmx_ref, gwo_ref, gfg_ref, loss_ref,
             state, uhalo, mxs, convs, sts, lacc, out_sems):
        s = pl.program_id(0)
        slot = lax.rem(s, 2)
        mixed_ref = mxs.at[slot]
        conv_ref = convs.at[slot]
        states_ref = sts.at[slot]

        def conv_out(sl, tile):
            return pltpu.make_async_copy(convs.at[sl], conv_hbm.at[pl.ds(pl.multiple_of(tile * TM, TM), TM), :],
                                         out_sems.at[sl])

        def states_out(sl, tile):
            return pltpu.make_async_copy(sts.at[sl], states_hbm.at[pl.ds(pl.multiple_of(tile * NCH, NCH), NCH)],
                                         out_sems.at[2 + sl])

        @pl.when(s == 0)
        def _():
            state[...] = jnp.zeros_like(state)
            uhalo[...] = jnp.zeros_like(uhalo)
            mxs[...] = jnp.zeros_like(mxs)
            gwo_ref[...] = jnp.zeros_like(gwo_ref)
            gfg_ref[...] = jnp.zeros_like(gfg_ref)
            lacc[...] = jnp.zeros_like(lacc)

        @pl.when(s >= 2)
        def _():
            conv_out(slot, pf(s - 2)).wait()
            states_out(slot, pf(s - 2)).wait()

        valid = jnp.where(s >= 2, 1.0, 0.0)
        mx_prev = mxs.at[1 - slot]
        f3 = {}

        def f3_fwd():
            f3["h2"] = x_ref[...] + _dot(mx_prev[...], w_ref[...])

        def f3_loss():
            h2 = f3.pop("h2")
            ms = jnp.mean(h2 * h2, axis=-1, keepdims=True)
            rstd = lax.rsqrt(ms + EPS)
            yh = h2 * rstd
            g = fg_ref[...]
            e = (yh * g - t_ref[...]) * valid
            lacc[...] += jnp.sum(e * e, axis=0, keepdims=True)
            dy = e * (1.0 / D_MODEL)
            gfg_ref[...] += jnp.sum(dy * yh, axis=0, keepdims=True)
            dyh = dy * g
            dh2 = rstd * (dyh - yh * jnp.mean(dyh * yh, axis=-1, keepdims=True))
            dh2_ref[...] = dh2
            f3["db"] = dh2.astype(BF16)

        def f3_dmx():
            dmx_ref[...] = _dot_tb(f3["db"], w_ref[...]).astype(BF16)

        def f3_gw():
            gw = _dot_ta(mx_prev[...], f3["db"])
            for j in range(N_CHIPS):
                for hf in range(2):
                    r0 = j * 256 + hf * 128
                    gwo_ref[hf, j] += gw[r0:r0 + 128, :]

        cx = proj_ref[:, CX:CX + 512].astype(F32)
        cc = proj_ref[:, CC:CC + 512].astype(F32)
        u = cc * cx
        row = lax.broadcasted_iota(jnp.int32, (TM, D_CONV), 0)
        h7 = uhalo[7:8, :]
        h6 = uhalo[6:7, :]
        u1 = jnp.where(row == 0, h7, pltpu.roll(u, 1, 0))
        u2 = jnp.where(row == 0, h6, jnp.where(row == 1, h7, pltpu.roll(u, 2, 0)))
        conv = cw_ref[2:3, :] * u + cw_ref[1:2, :] * u1 + cw_ref[0:1, :] * u2
        uhalo[...] = u[TM - 8:TM, :]
        cb = proj_ref[:, CB:CB + 512].astype(F32)
        cg = proj_ref[:, CG:CG + 512].astype(F32)
        mixed_ref[:, 0:D_CONV] = (cb * conv * (cg * _sigmoid(cg))).astype(BF16)
        conv_ref[...] = conv.astype(BF16)
        f3_fwd()

        scale = HEAD_DIM ** -0.5
        H = range(RET_HEADS)
        st = [state[h] for h in H]
        between = [f3_loss, f3_dmx, f3_gw, None]
        rc_t, rs_t = _tile_rotary(tt_ref, cr_ref, sr_ref, sgn_ref)
        for c in range(NCH):
            r0 = c * CHUNK
            rc = rc_t[r0:r0 + CHUNK, :]
            rs = rs_t[r0:r0 + CHUNK, :]
            col = lambda base, h: slice(base + h * HEAD_DIM, base + (h + 1) * HEAD_DIM)
            rws = slice(r0, r0 + CHUNK)
            v = [proj_ref[rws, col(CV, h)] for h in H]
            qf = [_rot(proj_ref[rws, col(CQ, h)].astype(F32), rc, rs) * scale for h in H]
            kf = [_rot(proj_ref[rws, col(CK, h)].astype(F32), rc, rs) for h in H]
            stb = [t.astype(BF16) for t in st]
            for h in H:
                states_ref[c, h] = stb[h]
            a = [(_dot_tb(qf[h].astype(BF16), kf[h].astype(BF16)) * dec_ref[h]).astype(BF16) for h in H]
            o = [_dot(a[h], v[h]) + _dot((qf[h] * xi_ref[h]).astype(BF16), stb[h]) for h in H]
            st = [cd_ref[h, 0:1, :] * st[h] + _dot_ta((kf[h] * zeta_ref[h]).astype(BF16), v[h]) for h in H]
            for h in H:
                mu = jnp.mean(o[h], axis=-1, keepdims=True)
                d = o[h] - mu
                var = jnp.mean(d * d, axis=-1, keepdims=True)
                yh = d * lax.rsqrt(var + EPS)
                rg = proj_ref[rws, col(CR, h)].astype(F32)
                mixed_ref[rws, col(D_CONV, h)] = (yh * g_ref[:, col(0, h)] * (rg * _sigmoid(rg))).astype(BF16)
            if between[c] is not None:
                between[c]()
        for h in H:
            state[h] = st[h]

        @pl.when(s <= nt)
        def _():
            conv_out(slot, pf(s)).start()
            states_out(slot, pf(s)).start()

        @pl.when(s == nt + 1)
        def _():
            conv_out(1 - slot, pf(s - 1)).wait()
            states_out(1 - slot, pf(s - 1)).wait()
            tot = jnp.sum(lacc[...], axis=1, keepdims=True) * (0.5 / D_MODEL)
            loss_ref[...] = jnp.broadcast_to(tot, (1, 128))

    tile = lambda w: pl.BlockSpec((TM, w), lambda s: (pf(s), 0))
    xtile = lambda w: pl.BlockSpec((TM, w), lambda s: (xt(s), 0))
    any_spec = pl.BlockSpec(memory_space=pl.ANY)
    return pl.pallas_call(
        body, name="f2_mixer_fwd_f3_outproj_loss",
        grid=(nt + 2,),
        in_specs=[tile(N_PROJ_COLS), _resident((8, D_CONV)), _resident((1, D_RET)),
                  pl.BlockSpec((None, 8, HEAD_DIM), lambda s: (pf(s), 0, 0)),
                  _resident((TM, HEAD_DIM)), _resident((TM, HEAD_DIM)), _resident((8, HEAD_DIM)),
                  _resident((RET_HEADS, CHUNK, CHUNK)), _resident((RET_HEADS, CHUNK, HEAD_DIM)),
                  _resident((RET_HEADS, CHUNK, HEAD_DIM)), _resident((RET_HEADS, 8, HEAD_DIM)),
                  xtile(D_MODEL), _resident1((D_MODEL, D_MODEL)), _resident((1, D_MODEL)), xtile(D_MODEL)],
        out_specs=[any_spec, any_spec, xtile(D_MODEL), xtile(D_MODEL),
                   _resident((2, N_CHIPS, 128, D_MODEL)), _resident((1, D_MODEL)), _resident((1, 128))],
        out_shape=[jax.ShapeDtypeStruct((rows, D_CONV), BF16),
                   jax.ShapeDtypeStruct(((nt + 1) * NCH, RET_HEADS, HEAD_DIM, HEAD_DIM), BF16),
                   jax.ShapeDtypeStruct((seq, D_MODEL), F32),
                   jax.ShapeDtypeStruct((seq, D_MODEL), BF16),
                   jax.ShapeDtypeStruct((2, N_CHIPS, 128, D_MODEL), F32),
                   jax.ShapeDtypeStruct((1, D_MODEL), F32),
                   jax.ShapeDtypeStruct((1, 128), F32)],
        scratch_shapes=[pltpu.VMEM((RET_HEADS, HEAD_DIM, HEAD_DIM), F32), pltpu.VMEM((8, D_CONV), F32),
                        pltpu.VMEM((2, TM, D_MODEL), BF16), pltpu.VMEM((2, TM, D_CONV), BF16),
                        pltpu.VMEM((2, NCH, RET_HEADS, HEAD_DIM, HEAD_DIM), BF16),
                        pltpu.VMEM((1, D_MODEL), F32), pltpu.SemaphoreType.DMA((4,))],
        compiler_params=_cparams(("arbitrary",)),
    )(proj, conv_w8, gret, tb["tt"], tb["cr2"], tb["sr2"], tb["sgn"], tb["decay"], tb["xi"], tb["zeta"], tb["cd"],
      x, w_out, fg, target)


def _b2_b1a_call(proj, dmixed, conv_s, states, conv_w8, gret, tb, w_in_g, x, meta_tile, g1, dh2):
    nt, rows = tb["nt"], tb["rows"]
    seq = nt * TM

    def pb(r):
        return jnp.where(r == nt, nt, nt - 1 - r)

    def xprev(r):
        return jnp.clip(nt - r, 0, nt - 1)

    def body(proj_ref, dmx_ref, conv_ref, states_ref, cw_ref, g_ref, tt_ref, cr_ref, sr_ref, sgn_ref, dec_ref,
             xi_ref, zeta_ref, cd_ref, w_ref, x_ref, mt_ref, g1_ref, dh2_ref,
             dproj_hbm, gcw_ref, gg_ref, gx_ref, dmeta_ref, gn_ref,
             gstate, dchalo, dps, out_sems):
        r = pl.program_id(0)
        live = jnp.where(r == nt, 0.0, 1.0)
        slot = lax.rem(r, 2)
        dproj_ref = dps.at[slot]

        class to_hbm:
            def __init__(self, s, tile):
                self.copies = [pltpu.make_async_copy(dps.at[s, :, j * 1024:(j + 1) * 1024], dproj_hbm.at[tile, j],
                                                     out_sems.at[N_CHIPS * s + j]) for j in range(N_CHIPS)]

            def start(self):
                for cp in self.copies:
                    cp.start()

            def wait(self):
                for cp in self.copies:
                    cp.wait()

        @pl.when(r == 0)
        def _():
            gstate[...] = jnp.zeros_like(gstate)
            dchalo[...] = jnp.zeros_like(dchalo)
            gcw_ref[...] = jnp.zeros_like(gcw_ref)
            gg_ref[...] = jnp.zeros_like(gg_ref)
            gn_ref[...] = jnp.zeros_like(gn_ref)
            dps[...] = jnp.zeros_like(dps)

        @pl.when(r >= 2)
        def _():
            to_hbm(slot, pb(r - 2)).wait()

        dprev = dps.at[1 - slot]
        pieces = []

        def emit_piece():
            j = len(pieces)
            if j < N_CHIPS:
                p = _dot_tb(dprev[:, j * 1024:(j + 1) * 1024], w_ref[j])
                pieces.append(p if j == 0 else pieces[-1] + p)

        cx = proj_ref[:, CX:CX + 512].astype(F32)
        cb = proj_ref[:, CB:CB + 512].astype(F32)
        cc = proj_ref[:, CC:CC + 512].astype(F32)
        cg = proj_ref[:, CG:CG + 512].astype(F32)
        dco = dmx_ref[:, 0:D_CONV].astype(F32) * live
        conv = conv_ref[...].astype(F32)
        sg = _sigmoid(cg)
        sil = cg * sg
        t = dco * conv
        dproj_ref[:, CB:CB + 512] = (t * sil).astype(BF16)
        dproj_ref[:, CG:CG + 512] = (t * cb * (sg * (1.0 + cg * (1.0 - sg)))).astype(BF16)
        dconv = dco * cb * sil
        row = lax.broadcasted_iota(jnp.int32, (TM, D_CONV), 0)
        n0 = dchalo[0:1, :]
        n1 = dchalo[1:2, :]
        dc1 = jnp.where(row == TM - 1, n0, pltpu.roll(dconv, TM - 1, 0))
        dc2 = jnp.where(row == TM - 2, n0, jnp.where(row == TM - 1, n1, pltpu.roll(dconv, TM - 2, 0)))
        dchalo[...] = dconv[0:8, :]
        du = cw_ref[2:3, :] * dconv + cw_ref[1:2, :] * dc1 + cw_ref[0:1, :] * dc2
        u = cc * cx
        gcw_ref[2:3, :] += jnp.sum(u * dconv, axis=0, keepdims=True)
        gcw_ref[1:2, :] += jnp.sum(u * dc1, axis=0, keepdims=True)
        gcw_ref[0:1, :] += jnp.sum(u * dc2, axis=0, keepdims=True)
        dproj_ref[:, CC:CC + 512] = (du * cx).astype(BF16)
        dproj_ref[:, CX:CX + 512] = (du * cc).astype(BF16)
        emit_piece()
        emit_piece()

        scale = HEAD_DIM ** -0.5
        gs = {h: gstate[h] for h in range(RET_HEADS)}
        gg = {h: jnp.zeros((1, HEAD_DIM), F32) for h in range(RET_HEADS)}
        col = lambda base, h: slice(base + h * HEAD_DIM, base + (h + 1) * HEAD_DIM)
        rw = lambda c: slice(c * CHUNK, (c + 1) * CHUNK)
        rc_t, rs_t = _tile_rotary(tt_ref, cr_ref, sr_ref, sgn_ref)
        for c0 in range(NCH - CHUNK_GROUP, -1, -CHUNK_GROUP):
            cs = range(c0 + CHUNK_GROUP - 1, c0 - 1, -1)
            U = [(c, h) for c in cs for h in range(RET_HEADS)]
            rc = {c: rc_t[rw(c), :] for c in cs}
            rs = {c: rs_t[rw(c), :] for c in cs}
            v = {(c, h): proj_ref[rw(c), col(CV, h)] for c, h in U}
            stb = {(c, h): states_ref[c, h] for c, h in U}
            qf = {(c, h): _rot(proj_ref[rw(c), col(CQ, h)].astype(F32), rc[c], rs[c]) * scale for c, h in U}
            kf = {(c, h): _rot(proj_ref[rw(c), col(CK, h)].astype(F32), rc[c], rs[c]) for c, h in U}
            qb = {u: qf[u].astype(BF16) for u in U}
            kb = {u: kf[u].astype(BF16) for u in U}
            qxb = {(c, h): (qf[c, h] * xi_ref[h]).astype(BF16) for c, h in U}
            kzb = {(c, h): (kf[c, h] * zeta_ref[h]).astype(BF16) for c, h in U}
            ab = {(c, h): (_dot_tb(qb[c, h], kb[c, h]) * dec_ref[h]).astype(BF16) for c, h in U}
            o = {u: _dot(ab[u], v[u]) + _dot(qxb[u], stb[u]) for u in U}
            emit_piece()
            dob = {}
            for c, h in U:
                mu = jnp.mean(o[c, h], axis=-1, keepdims=True)
                d = o[c, h] - mu
                var = jnp.mean(d * d, axis=-1, keepdims=True)
                rstd = lax.rsqrt(var + EPS)
                yh = d * rstd
                g = g_ref[:, col(0, h)]
                rg = proj_ref[rw(c), col(CR, h)].astype(F32)
                dro = dmx_ref[rw(c), col(D_CONV, h)].astype(F32) * live
                sg = _sigmoid(rg)
                dproj_ref[rw(c), col(CR, h)] = (dro * (yh * g) * (sg * (1.0 + rg * (1.0 - sg)))).astype(BF16)
                dret = dro * (rg * sg)
                gg[h] = gg[h] + jnp.sum(dret * yh, axis=0, keepdims=True)
                dyh = dret * g
                do = rstd * (dyh - jnp.mean(dyh, axis=-1, keepdims=True)
                             - yh * jnp.mean(dyh * yh, axis=-1, keepdims=True))
                dob[c, h] = do.astype(BF16)
            dv1 = {u: _dot_ta(ab[u], dob[u]) for u in U}
            ds = {(c, h): (_dot_tb(dob[c, h], v[c, h]) * dec_ref[h]).astype(BF16) for c, h in U}
            gup = {u: _dot_ta(qxb[u], dob[u]) for u in U}
            dq = {(c, h): _dot(ds[c, h], kb[c, h]) + _dot_tb(dob[c, h], stb[c, h]) * xi_ref[h] for c, h in U}
            dk1 = {u: _dot_ta(ds[u], qb[u]) for u in U}
            emit_piece()
            for c, h in U:
                gsb = gs[h].astype(BF16)
                dv = dv1[c, h] + _dot(kzb[c, h], gsb)
                dk = dk1[c, h] + _dot_tb(v[c, h], gsb) * zeta_ref[h]
                gs[h] = cd_ref[h, 0:1, :] * gs[h] + gup[c, h]
                dproj_ref[rw(c), col(CQ, h)] = (_rot_t(dq[c, h], rc[c], rs[c]) * scale).astype(BF16)
                dproj_ref[rw(c), col(CK, h)] = _rot_t(dk, rc[c], rs[c]).astype(BF16)
                dproj_ref[rw(c), col(CV, h)] = dv.astype(BF16)
        for h in range(RET_HEADS):
            gstate[h] = gs[h]
            gg_ref[:, col(0, h)] += gg[h]

        while len(pieces) < N_CHIPS:
            emit_piece()

        def norm_bwd(dhn, hx):
            ms = jnp.mean(hx * hx, axis=-1, keepdims=True)
            rstd1 = lax.rsqrt(ms + EPS)
            xh = hx * rstd1
            gn_ref[...] += jnp.sum(dhn * xh, axis=0, keepdims=True)
            dxh = dhn * g1_ref[...]
            return rstd1 * (dxh - xh * jnp.mean(dxh * xh, axis=-1, keepdims=True))

        gx_ref[...] = norm_bwd(pieces[-1], x_ref[...]) + dh2_ref[...]

        @pl.when(r < nt)
        def _():
            to_hbm(slot, pb(r)).start()

        @pl.when(r == nt)
        def _():
            to_hbm(slot, pb(r)).start()
            mrows = slice(TM - N_META, TM)
            d16 = dproj_ref[mrows, :]
            dhn16 = _dot_tb(d16[:, 0:1024], w_ref[0])
            for j in range(1, N_CHIPS):
                dhn16 += _dot_tb(d16[:, j * 1024:(j + 1) * 1024], w_ref[j])
            dmeta_ref[...] = norm_bwd(dhn16, mt_ref[mrows, :])
            to_hbm(1 - slot, pb(r - 1)).wait()
            to_hbm(slot, pb(r)).wait()

    tile = lambda w: pl.BlockSpec((TM, w), lambda r: (pb(r), 0))
    xtile = pl.BlockSpec((TM, D_MODEL), lambda r: (xprev(r), 0))
    return pl.pallas_call(
        body, name="b2_mixer_bwd_b1a_inproj_bwd_x",
        grid=(nt + 1,),
        in_specs=[tile(N_PROJ_COLS),
                  pl.BlockSpec((TM, D_MODEL), lambda r: (jnp.minimum(pb(r), nt - 1), 0)),
                  tile(D_CONV),
                  pl.BlockSpec((NCH, RET_HEADS, HEAD_DIM, HEAD_DIM), lambda r: (pb(r), 0, 0, 0)),
                  _resident((8, D_CONV)), _resident((1, D_RET)),
                  pl.BlockSpec((None, 8, HEAD_DIM), lambda r: (pb(r), 0, 0)),
                  _resident((TM, HEAD_DIM)), _resident((TM, HEAD_DIM)), _resident((8, HEAD_DIM)),
                  _resident((RET_HEADS, CHUNK, CHUNK)),
                  _resident((RET_HEADS, CHUNK, HEAD_DIM)),
                  _resident((RET_HEADS, CHUNK, HEAD_DIM)), _resident((RET_HEADS, 8, HEAD_DIM)),
                  _resident1((N_CHIPS, D_MODEL, 1024)), xtile, _resident1((TM, D_MODEL)), _resident((1, D_MODEL)),
                  xtile],
        out_specs=[pl.BlockSpec(memory_space=pl.ANY), _resident((8, D_CONV)), _resident((1, D_RET)),
                   xtile, _resident((N_META, D_MODEL)), _resident((1, D_MODEL))],
        out_shape=[jax.ShapeDtypeStruct((nt + 1, N_CHIPS, TM, 1024), BF16),
                   jax.ShapeDtypeStruct((8, D_CONV), F32),
                   jax.ShapeDtypeStruct((1, D_RET), F32),
                   jax.ShapeDtypeStruct((seq, D_MODEL), F32),
                   jax.ShapeDtypeStruct((N_META, D_MODEL), F32),
                   jax.ShapeDtypeStruct((1, D_MODEL), F32)],
        scratch_shapes=[pltpu.VMEM((RET_HEADS, HEAD_DIM, HEAD_DIM), F32), pltpu.VMEM((8, D_CONV), F32),
                        pltpu.VMEM((2, TM, N_PROJ_COLS), BF16), pltpu.SemaphoreType.DMA((2 * N_CHIPS,))],
        compiler_params=_cparams(("arbitrary",), vmem=VMEM_LIMIT_MAX),
    )(proj, dmixed, conv_s, states, conv_w8, gret, tb["tt"], tb["cr2"], tb["sr2"], tb["sgn"], tb["decay"],
      tb["xi"], tb["zeta"], tb["cd"], w_in_g, x, meta_tile, g1, dh2)


REL = (2, 1, 3)
SMALL_ROWS = 24
HALF_STEP = 4


def _rcopy(src, dst, send_sems, recv_sems, k, to):
    return pltpu.make_async_remote_copy(src_ref=src, dst_ref=dst, send_sem=send_sems.at[k],
                                        recv_sem=recv_sems.at[k], device_id=to, device_id_type=MESH_ID)


def _b1b_reduce_call(order, hnt, dproj, gwo, pack, nt):
    nk = nt + 1
    last = nk - 1
    any_spec = pl.BlockSpec(memory_space=pl.ANY)

    def body(order_ref, a_ref, b_ref, gwo_hbm, pack_hbm, gwin_hbm, gwout_hbm, tot_hbm,
             acc, sb, abuf, pb, bbuf, fin, go, ao, pbo, bo, fino, slots, totv, send_sems, recv_sems, loc_sems):
        jj, k = pl.program_id(0), pl.program_id(1)
        x, y, c = lax.axis_index("x"), lax.axis_index("y"), lax.axis_index("c")
        me, myid, sib = 2 * x + y, 4 * x + 2 * y + c, (x, y, 1 - c)
        rc = functools.partial(_rcopy, send_sems=send_sems, recv_sems=recv_sems)
        peers = [((1 - x) if r & 2 else x, (1 - y) if r & 1 else y, c) for r in REL]
        kids = [jnp.bitwise_xor(me, r) for r in REL]

        def dev_peer(r):
            return ((1 - x) if r & 4 else x, (1 - y) if r & 2 else y, (1 - c) if r & 1 else c)

        own_go = pltpu.make_async_copy(gwo_hbm.at[c], go, loc_sems.at[0])
        own_pack = pltpu.make_async_copy(pack_hbm, slots.at[0], loc_sems.at[1])
        wo_half = rc(gwo_hbm.at[1 - c], ao, k=8, to=sib)
        wo_part = [rc(pbo.at[kids[p]], bo.at[p], k=9 + p, to=peers[p]) for p in range(3)]
        sm = [rc(pack_hbm, slots.at[r], k=12 + r, to=dev_peer(r)) for r in range(1, N_DEV)]
        half = [rc(sb.at[j % 2, 1 - c], abuf.at[j], k=j, to=sib) for j in range(N_CHIPS)]
        part = [rc(pb.at[p], bbuf.at[p], k=4 + p, to=peers[p]) for p in range(3)]

        @pl.when(jnp.logical_and(jj == 0, k == 0))
        def _():
            own_go.start()
            own_pack.start()
            wo_half.start()
            for cp in sm:
                cp.start()

        @pl.when(k == 0)
        def _():
            acc[...] = jnp.zeros_like(acc)

        acc[0] += _dot(a_ref[0:512, :], b_ref[...])
        acc[1] += _dot(a_ref[512:1024, :], b_ref[...])

        @pl.when(k == HALF_STEP)
        def _():
            @pl.when(jj == 0)
            def _():
                own_go.wait()
                wo_half.wait_recv()
                for j in range(N_CHIPS):
                    go[j] = go[j] + ao[j]
                pbo[...] = go[...].astype(BF16)
                for cp in wo_part:
                    cp.start()

            for p in range(3):
                @pl.when(jj == p + 1)
                def _(p=p):
                    half[p].wait_recv()
                    half[p].wait_send()
                    pb[p] = (sb[p % 2, c] + abuf[p]).astype(BF16)
                    part[p].start()

        @pl.when(k == last)
        def _():
            for j in range(N_CHIPS):
                @pl.when(jj == j)
                def _(j=j):
                    sb[j % 2] = acc[...]
                    half[j].start()

        @pl.when(jnp.logical_and(jj == N_CHIPS - 1, k == last))
        def _():
            half[3].wait_recv()
            own = sb[1, c] + abuf[3]
            for cp in part:
                cp.wait_recv()
            fin[c] = ((own + bbuf[0].astype(F32)) + bbuf[1].astype(F32)) + bbuf[2].astype(F32)
            done = rc(fin.at[c], fin.at[c], k=7, to=sib)
            done.start()
            for cp in wo_part:
                cp.wait_recv()
            fino[c] = ((go[me] + bo[0].astype(F32)) + bo[1].astype(F32)) + bo[2].astype(F32)
            done_o = rc(fino.at[c], fino.at[c], k=12, to=sib)
            done_o.start()
            own_pack.wait()
            for cp in sm:
                cp.wait_recv()
            tot = slots[myid]
            for a in range(1, N_DEV):
                tot = tot + slots[jnp.bitwise_xor(myid, a)]
            totv[...] = tot
            out_t = pltpu.make_async_copy(totv, tot_hbm, loc_sems.at[1])
            out_t.start()
            rc(fin.at[1 - c], fin.at[1 - c], k=7, to=sib).wait_recv()
            out_w = pltpu.make_async_copy(fin, gwin_hbm, loc_sems.at[0])
            out_w.start()
            rc(fino.at[1 - c], fino.at[1 - c], k=12, to=sib).wait_recv()
            out_o = pltpu.make_async_copy(fino, gwout_hbm, loc_sems.at[2])
            out_o.start()
            for cp in [half[3]] + part + [done, wo_half] + wo_part + [done_o] + sm:
                cp.wait_send()
            out_t.wait()
            out_w.wait()
            out_o.wait()

    grid_spec = pltpu.PrefetchScalarGridSpec(
        num_scalar_prefetch=1,
        grid=(N_CHIPS, nk),
        in_specs=[pl.BlockSpec((None, D_MODEL, TM), lambda j, k, o: (k, 0, 0)),
                  pl.BlockSpec((None, None, TM, 1024), lambda j, k, o: (k, o[j], 0, 0)),
                  any_spec, any_spec],
        out_specs=[any_spec, any_spec, any_spec],
        scratch_shapes=[
            pltpu.VMEM((2, 512, 1024), F32),
            pltpu.VMEM((2, 2, 512, 1024), F32),
            pltpu.VMEM((N_CHIPS, 512, 1024), F32),
            pltpu.VMEM((3, 512, 1024), BF16),
            pltpu.VMEM((3, 512, 1024), BF16),
            pltpu.VMEM((2, 512, 1024), F32),
            pltpu.VMEM((N_CHIPS, 128, D_MODEL), F32),
            pltpu.VMEM((N_CHIPS, 128, D_MODEL), F32),
            pltpu.VMEM((N_CHIPS, 128, D_MODEL), BF16),
            pltpu.VMEM((3, 128, D_MODEL), BF16),
            pltpu.VMEM((2, 128, D_MODEL), F32),
            pltpu.VMEM((N_DEV, SMALL_ROWS, D_MODEL), F32),
            pltpu.VMEM((SMALL_ROWS, D_MODEL), F32),
            pltpu.SemaphoreType.DMA((20,)), pltpu.SemaphoreType.DMA((20,)), pltpu.SemaphoreType.DMA((3,))])
    return pl.pallas_call(
        body, name="b1b_inproj_bwd_w_reduce",
        grid_spec=grid_spec,
        out_shape=[jax.ShapeDtypeStruct((2, 512, 1024), F32),
                   jax.ShapeDtypeStruct((2, 128, D_MODEL), F32),
                   jax.ShapeDtypeStruct((SMALL_ROWS, D_MODEL), F32)],
        compiler_params=_cparams(("arbitrary", "arbitrary")),
    )(order, hnt, dproj, gwo, pack)


def _local_step(me, x, target, g1, gret, fg, win_sh, wout_sh, small_sh):
    seq = x.shape[0]
    tb = _tables(seq)
    nt = tb["nt"]
    g1r, gretr, fgr = g1.reshape(1, -1), gret.reshape(1, -1), fg.reshape(1, -1)
    order = jnp.stack([me, me ^ REL[0], me ^ REL[1], me ^ REL[2]]).astype(jnp.int32)

    proj, hnt, w_in_g, w_out_g, meta_tile, conv_w8 = _f1_gather_call(order, x, g1r, win_sh, wout_sh, small_sh, nt)
    w_out = w_out_g.reshape(D_MODEL, D_MODEL)
    conv_s, states, dh2, dmixed, g_wout, g_fg, loss = _f2_f3_call(proj, conv_w8, gretr, tb, x, w_out, fgr, target)
    dproj, g_cw8, g_gret, grad_x, g_meta, g_g1 = _b2_b1a_call(proj, dmixed, conv_s, states, conv_w8, gretr, tb,
                                                              w_in_g, x, meta_tile, g1r, dh2)
    return loss, grad_x, dict(w_out=g_wout, meta=g_meta, conv_w=g_cw8[0:3], norm1_g=g_g1,
                              ret_norm_g=g_gret, final_g=g_fg), hnt, dproj


def _adamw_update(w_ref, g_ref, m_ref, v_ref, d_ref, nm_ref, nv_ref):
    gg = g_ref[...]
    nm = ADAM_B1 * m_ref[...] + (1.0 - ADAM_B1) * gg
    nv = ADAM_B2 * v_ref[...] + (1.0 - ADAM_B2) * (gg * gg)
    m_hat = nm / (1.0 - ADAM_B1 ** ADAM_STEP)
    v_hat = nv / (1.0 - ADAM_B2 ** ADAM_STEP)
    d_ref[...] = -ADAM_LR * (m_hat / (jnp.sqrt(v_hat) + ADAM_EPS) + ADAM_WD * w_ref[...])
    nm_ref[...] = nm
    nv_ref[...] = nv


def _adamw_small_call(me, tot, ws, ms, vs):
    n = len(ws)

    def body(me_ref, tmeta_ref, tvec_ref, tconv_ref, *refs):
        ins, outs = refs[:3 * n], refs[3 * n:]
        g_refs, loss_ref, upd = outs[0:n], outs[n], outs[n + 1:]
        g_refs[0][...] = tmeta_ref[...]
        g_refs[1][...] = tvec_ref[0:1, :]
        g_refs[2][...] = tconv_ref[3:6, :]
        g_refs[3][...] = tvec_ref[2:3, 0:D_RET]
        g_refs[4][...] = tvec_ref[1:2, :]
        loss_ref[...] = tvec_ref[6:7, 0:1]
        for i in range(n):
            _adamw_update(ins[i], g_refs[i], ins[n + i], ins[2 * n + i], upd[i], upd[n + i], upd[2 * n + i])

    whole = lambda a: pl.BlockSpec(a.shape, lambda i, m: (0,) * a.ndim)
    shapes = [jax.ShapeDtypeStruct(w.shape, F32) for w in ws]
    out_shape = shapes + [jax.ShapeDtypeStruct((1, 1), F32)] + shapes * 3
    grid_spec = pltpu.PrefetchScalarGridSpec(
        num_scalar_prefetch=1, grid=(1,),
        in_specs=[pl.BlockSpec((N_META, 256), lambda i, m: (0, m[0])),
                  pl.BlockSpec((8, D_MODEL), lambda i, m: (N_META // 8, 0)),
                  pl.BlockSpec((8, 128), lambda i, m: (N_META // 8, m[0]))] + [whole(a) for a in ws + ms + vs],
        out_specs=[whole(s) for s in out_shape])
    outs = pl.pallas_call(body, name="adamw_small", grid_spec=grid_spec, out_shape=out_shape,
                          compiler_params=_cparams(("arbitrary",)))(me.reshape(1), tot, tot, tot, *ws, *ms, *vs)
    return outs[:n], outs[n], outs[n + 1:2 * n + 1], outs[2 * n + 1:3 * n + 1], outs[3 * n + 1:]


def _adamw_call(w, g, m, v, name):
    shape = w.shape
    w2, g2, m2, v2 = (a.reshape(-1, shape[-1]) for a in (w, g, m, v))
    rows, cols = w2.shape
    br = 256 if rows % 256 == 0 else rows
    body = functools.partial(_adamw_update)
    spec = pl.BlockSpec((br, cols), lambda i: (i, 0))
    outs = pl.pallas_call(
        body, name=name, grid=(rows // br,),
        in_specs=[spec] * 4, out_specs=[spec] * 3,
        out_shape=[jax.ShapeDtypeStruct((rows, cols), F32)] * 3,
        compiler_params=_cparams(("arbitrary",)),
    )(w2, g2, m2, v2)
    return tuple(o.reshape(shape) for o in outs)


def _pad_to(a, rows, cols):
    return jnp.pad(a, ((0, rows - a.shape[0]), (0, cols - a.shape[1])))


def kernel(x, meta, norm1_g, w_in, conv_w, ret_norm_g, w_out, final_g, loss_target, m_meta, m_norm1_g, m_w_in, m_conv_w, m_ret_norm_g, m_w_out, m_final_g, v_meta, v_norm1_g, v_w_in, v_conv_w, v_ret_norm_g, v_w_out, v_final_g):
    me = 2 * lax.axis_index("x") + lax.axis_index("y")

    small_sh = jnp.concatenate([meta, _pad_to(conv_w, 8, 256)], axis=0)
    loss, grad_x, g, hnt, dproj = _local_step(me, x[0], loss_target[0], norm1_g, ret_norm_g, final_g,
                                              w_in.astype(BF16), w_out.astype(BF16), small_sh)

    vec = jnp.concatenate([g["norm1_g"], g["final_g"], _pad_to(g["ret_norm_g"], 1, D_MODEL),
                           _pad_to(g["conv_w"], 3, D_MODEL), _pad_to(loss, 2, D_MODEL)], axis=0)
    pack = jnp.concatenate([g["meta"], vec], axis=0)
    order = jnp.stack([me ^ REL[0], me ^ REL[1], me ^ REL[2], me]).astype(jnp.int32)
    g_win, g_wout, tot = _b1b_reduce_call(order, hnt, dproj, g["w_out"], pack, x.shape[1] // TM)
    g_win, g_wout = g_win.reshape(D_MODEL, 1024), g_wout.reshape(256, D_MODEL)

    ws = [meta, norm1_g, w_in, conv_w, ret_norm_g, w_out, final_g]
    ms = [m_meta, m_norm1_g, m_w_in, m_conv_w, m_ret_norm_g, m_w_out, m_final_g]
    vs = [v_meta, v_norm1_g, v_w_in, v_conv_w, v_ret_norm_g, v_w_out, v_final_g]
    names = ["meta", "norm1_g", "w_in", "conv_w", "ret_norm_g", "w_out", "final_g"]
    as2d = lambda a: a.reshape(1, -1) if a.ndim == 1 else a
    big = {names.index("w_in"): g_win, names.index("w_out"): g_wout}
    small = [i for i in range(len(names)) if i not in big]
    grads, deltas, new_ms, new_vs = [None] * 7, [None] * 7, [None] * 7, [None] * 7
    for i, g_ in big.items():
        grads[i] = g_
        deltas[i], new_ms[i], new_vs[i] = _adamw_call(ws[i], g_, ms[i], vs[i], "adamw_" + names[i])
    sg, loss_tot, sd, sm_, sv = _adamw_small_call(me.astype(jnp.int32), tot,
                                                  *[[as2d(t[i]) for i in small] for t in (ws, ms, vs)])
    for j, i in enumerate(small):
        grads[i], deltas[i], new_ms[i], new_vs[i] = (o[j].reshape(ws[i].shape) for o in (sg, sd, sm_, sv))
    return (loss_tot.reshape(()), grad_x[None], *grads, *deltas, *new_ms, *new_vs)
```

```python
import functools

import jax
import jax.numpy as jnp
import numpy as np
from jax import lax
from jax.experimental import pallas as pl
from jax.experimental.pallas import tpu as pltpu

F32 = jnp.float32
BF16 = jnp.bfloat16

D_MODEL = 1024
N_META = 16
D_CONV = 512
D_RET = 512
RET_HEADS = 4
HEAD_DIM = 128
CHUNK = 128
N_PROJ_COLS = 4096
ROPE_BASE = 10000.0
EPS = 1e-6
N_CHIPS = 4
N_DEV = 8

ADAM_LR = 0.001
ADAM_B1 = 0.9
ADAM_B2 = 0.999
ADAM_EPS = 1e-08
ADAM_WD = 0.01
ADAM_STEP = 10

TM = 512
NCH = TM // CHUNK
CHUNK_GROUP = 2
VMEM_LIMIT = 56 * 1024 * 1024
VMEM_LIMIT_MAX = 63 * 1024 * 1024

CX, CB, CC, CG, CQ, CK, CV, CR = (i * 512 for i in range(8))

MESH_ID = pl.DeviceIdType.MESH


def _cparams(sem=None, vmem=VMEM_LIMIT, **kw):
    return pltpu.CompilerParams(dimension_semantics=sem, vmem_limit_bytes=vmem, **kw)


def _sigmoid(x):
    return 1.0 / (1.0 + jnp.exp(-x))


def _dot(a, b):
    return jnp.dot(a, b, preferred_element_type=F32)


def _dot_tb(a, b):
    return lax.dot_general(a, b, (((1,), (1,)), ((), ())), preferred_element_type=F32)


def _dot_ta(a, b):
    return lax.dot_general(a, b, (((0,), (0,)), ((), ())), preferred_element_type=F32)


def _resident(shape):
    nd = len(shape)
    return pl.BlockSpec(shape, lambda *_: (0,) * nd)


def _resident1(shape):
    nd = len(shape)
    return pl.BlockSpec(shape, lambda *_: (0,) * nd, pipeline_mode=pl.Buffered(1))


def _tables(seq):
    f32 = np.float32
    nt = seq // TM
    rows = seq + TM
    half = HEAD_DIM // 2
    freqs = (f32(1.0) / (f32(ROPE_BASE) ** (np.arange(half, dtype=f32) / f32(half)))).astype(f32)
    tile_start = np.concatenate([np.arange(nt, dtype=f32), -np.ones((1,), f32)]) * f32(TM)
    ang_t = tile_start[:, None] * freqs[None, :]
    ang_r = (np.arange(TM, dtype=f32) + f32(N_META))[:, None] * freqs[None, :]
    dup = lambda a: np.concatenate([a, a], axis=-1).astype(f32)
    tt = np.stack([dup(np.cos(ang_t)), dup(np.sin(ang_t))], axis=1)
    tt = np.pad(tt, ((0, 0), (0, 6), (0, 0)))
    cr2, sr2 = dup(np.cos(ang_r)), dup(np.sin(ang_r))
    sgn = np.concatenate([-np.ones((8, half), f32), np.ones((8, half), f32)], axis=-1)
    log_g = np.log(f32(1.0) - f32(2.0) ** (f32(-5.0) - np.arange(RET_HEADS, dtype=f32))).astype(f32)
    idx = np.arange(CHUNK, dtype=f32)
    diff = idx[:, None] - idx[None, :]
    decay = np.where(diff[None] >= 0, np.exp(diff[None] * log_g[:, None, None]), f32(0.0)).astype(f32)
    zeta = np.exp((f32(CHUNK - 1) - idx)[None, :] * log_g[:, None]).astype(f32)
    xi = np.exp((idx + f32(1.0))[None, :] * log_g[:, None]).astype(f32)
    cd = np.exp(f32(CHUNK) * log_g).astype(f32)
    zeta_b = np.broadcast_to(zeta[:, :, None], (RET_HEADS, CHUNK, HEAD_DIM))
    xi_b = np.broadcast_to(xi[:, :, None], (RET_HEADS, CHUNK, HEAD_DIM))
    cd_b = np.broadcast_to(cd[:, None, None], (RET_HEADS, 8, HEAD_DIM))
    tables = dict(tt=tt, cr2=cr2, sr2=sr2, sgn=sgn, decay=decay, zeta=zeta_b, xi=xi_b, cd=cd_b)
    return dict(nt=nt, rows=rows, **{k: jnp.asarray(np.ascontiguousarray(v, dtype=f32)) for k, v in tables.items()})


def _tile_rotary(tt_ref, cr_ref, sr_ref, sgn_ref):
    ct, st = tt_ref[0:1, :], tt_ref[1:2, :]
    cr, sr = cr_ref[...], sr_ref[...]
    return ct * cr - st * sr, (st * cr + ct * sr) * sgn_ref[0:1, :]


def _rot(t, rc, rs):
    return t * rc + pltpu.roll(t, HEAD_DIM // 2, 1) * rs


def _rot_t(dt, rc, rs):
    return dt * rc + pltpu.roll(dt * rs, HEAD_DIM // 2, 1)


def _f1_gather_call(order, x, g1, win_sh, wout_sh, small_sh, nt):
    nk = nt + 1
    rows = nk * TM
    any_spec = pl.BlockSpec(memory_space=pl.ANY)

    def body(order_ref, x_ref, g_ref, win_hbm, wout_hbm, sm_hbm,
             proj_ref, hnt_ref, wg_hbm, wog_hbm, mt_hbm, cw_hbm,
             wg, wog, smg, mt, cw, hbs, send_sems, recv_sems, loc_sems):
        jj, k = pl.program_id(0), pl.program_id(1)
        x, y, c = lax.axis_index("x"), lax.axis_index("y"), lax.axis_index("c")
        me, sib = 2 * x + y, (x, y, 1 - c)
        rc = functools.partial(_rcopy, send_sems=send_sems, recv_sems=recv_sems)
        peers = [((1 - x) if r & 2 else x, (1 - y) if r & 1 else y, c) for r in REL]
        kids = [jnp.bitwise_xor(me, r) for r in REL]
        hw, ho = pl.ds(pl.multiple_of(c * 512, 512), 512), pl.ds(pl.multiple_of(c * 128, 128), 128)
        hw2 = pl.ds(pl.multiple_of((1 - c) * 512, 512), 512)
        ho2 = pl.ds(pl.multiple_of((1 - c) * 128, 128), 128)
        at = lambda j_, k_: jnp.logical_and(jj == j_, k == k_)

        sm_cp = [rc(sm_hbm, smg.at[me], k=p, to=peers[p]) for p in range(3)]
        win_cp = [rc(wg.at[me, hw], wg.at[me, hw], k=3 + p, to=peers[p]) for p in range(3)]
        wout_cp = [rc(wog.at[me, ho], wog.at[me, ho], k=6 + p, to=peers[p]) for p in range(3)]
        sm_in = [rc(sm_hbm, smg.at[kids[p]], k=p, to=sib) for p in range(3)]
        win_in = [rc(wg.at[me, hw], wg.at[kids[p], hw], k=3 + p, to=sib) for p in range(3)]
        wout_in = [rc(wog.at[me, ho], wog.at[kids[p], ho], k=6 + p, to=sib) for p in range(3)]
        win_fw = [rc(wg.at[kids[p], hw], wg.at[kids[p], hw], k=9 + p, to=sib) for p in range(3)]
        wout_fw = [rc(wog.at[kids[p], ho], wog.at[kids[p], ho], k=12 + p, to=sib) for p in range(3)]
        win_fw_in = [rc(wg.at[kids[p], hw2], wg.at[kids[p], hw2], k=9 + p, to=sib) for p in range(3)]
        wout_fw_in = [rc(wog.at[kids[p], ho2], wog.at[kids[p], ho2], k=12 + p, to=sib) for p in range(3)]
        own_s = pltpu.make_async_copy(sm_hbm, smg.at[me], loc_sems.at[2])

        def stage_own():
            for rows_, sends in ((hw, win_cp), (hw2, [])):
                cp = pltpu.make_async_copy(win_hbm.at[rows_], mt, loc_sems.at[0])
                cp.start()
                cp.wait()
                wg[me, rows_, :] = mt[...].astype(BF16)
                for s_ in sends:
                    s_.start()
            cp = pltpu.make_async_copy(wout_hbm, mt.at[0:256], loc_sems.at[1])
            cp.start()
            cp.wait()
            wog[me] = mt[0:256, :].astype(BF16)
            for s_ in wout_cp:
                s_.start()
        out_wg = pltpu.make_async_copy(wg, wg_hbm, loc_sems.at[3])
        out_wog = pltpu.make_async_copy(wog, wog_hbm, loc_sems.at[4])
        out_mt = pltpu.make_async_copy(mt, mt_hbm, loc_sems.at[5])
        out_cw = pltpu.make_async_copy(cw, cw_hbm, loc_sems.at[6])

        def pass_on(p):
            win_in[p].wait_recv()
            win_fw[p].start()

        @pl.when(k <= 1)
        def _():
            @pl.when(at(0, 0))
            def _():
                own_s.start()
                for cp in sm_cp:
                    cp.start()
                stage_own()

            for p in range(3):
                @pl.when(at(p + 1, 0))
                def _(p=p):
                    win_fw_in[p].wait_recv()

            @pl.when(at(1, 1))
            def _():
                pass_on(1)

            @pl.when(at(3, 0))
            def _():
                out_wg.start()

            @pl.when(at(3, 1))
            def _():
                for p in range(3):
                    wout_in[p].wait_recv()
                    wout_fw[p].start()

        @pl.when(jnp.logical_and(jj == 0, k >= nk - 2))
        def _():
            @pl.when(k == nk - 2)
            def _():
                own_s.wait()
                for cp in sm_in:
                    cp.wait_recv()
                mt[...] = jnp.zeros_like(mt)
                cw[...] = jnp.zeros_like(cw)
                for j in range(N_CHIPS):
                    mt[TM - N_META:TM, j * 256:(j + 1) * 256] = smg[j, 0:N_META, :]
                    cw[0:3, j * 128:(j + 1) * 128] = smg[j, N_META:N_META + 3, 0:128]
                out_mt.start()
                out_cw.start()

            @pl.when(k == nk - 1)
            def _():
                pass_on(0)

        @pl.when(at(2, nk // 2))
        def _():
            pass_on(2)

        tile_rows = pl.ds(pl.multiple_of(k * TM, TM), TM)

        @pl.when(jj == 0)
        def _():
            h = jnp.where(k == nt, mt[...], x_ref[...])
            ms = jnp.mean(h * h, axis=-1, keepdims=True)
            hn = (h * lax.rsqrt(ms + EPS)) * g_ref[...]
            hb = hn.astype(BF16)
            hbs[tile_rows, :] = hb
            proj_ref[...] = _dot(hb, wg[order_ref[0]]).astype(BF16)
            hnt_ref[...] = hn.T.astype(BF16)

        @pl.when(jj > 0)
        def _():
            proj_ref[...] = _dot(hbs[tile_rows, :], wg[order_ref[jj]]).astype(BF16)

        @pl.when(at(3, nk - 1))
        def _():
            for cp in wout_fw_in:
                cp.wait_recv()
            out_wog.start()
            for cp in sm_cp + win_cp + wout_cp + win_fw + wout_fw:
                cp.wait_send()
            for cp in (out_wg, out_wog, out_mt, out_cw):
                cp.wait()

    grid_spec = pltpu.PrefetchScalarGridSpec(
        num_scalar_prefetch=1,
        grid=(N_CHIPS, nk),
        in_specs=[pl.BlockSpec((TM, D_MODEL), lambda j, k, o: (jnp.where(j == 0, jnp.minimum(k, nt - 1), nt - 1), 0)),
                  pl.BlockSpec((1, D_MODEL), lambda j, k, o: (0, 0)),
                  any_spec, any_spec, any_spec],
        out_specs=[pl.BlockSpec((TM, 1024), lambda j, k, o: (k, o[j])),
                   pl.BlockSpec((None, D_MODEL, TM), lambda j, k, o: (jnp.where(j == 0, k, nk - 1), 0, 0)),
                   any_spec, any_spec, any_spec, any_spec],
        scratch_shapes=[
            pltpu.VMEM((N_CHIPS, D_MODEL, 1024), BF16),
            pltpu.VMEM((N_CHIPS, 256, D_MODEL), BF16),
            pltpu.VMEM((N_CHIPS, SMALL_ROWS, 256), F32),
            pltpu.VMEM((TM, D_MODEL), F32),
            pltpu.VMEM((8, D_CONV), F32),
            pltpu.VMEM((rows, D_MODEL), BF16),
            pltpu.SemaphoreType.DMA((15,)), pltpu.SemaphoreType.DMA((15,)), pltpu.SemaphoreType.DMA((7,))])
    return pl.pallas_call(
        body, name="f1_norm_inproj_gather",
        grid_spec=grid_spec,
        out_shape=[jax.ShapeDtypeStruct((rows, N_PROJ_COLS), BF16),
                   jax.ShapeDtypeStruct((nk, D_MODEL, TM), BF16),
                   jax.ShapeDtypeStruct((N_CHIPS, D_MODEL, 1024), BF16),
                   jax.ShapeDtypeStruct((N_CHIPS, 256, D_MODEL), BF16),
                   jax.ShapeDtypeStruct((TM, D_MODEL), F32),
                   jax.ShapeDtypeStruct((8, D_CONV), F32)],
        compiler_params=_cparams(("arbitrary", "arbitrary")),
    )(order, x, g1, win_sh, wout_sh, small_sh)


def _f2_f3_call(proj, conv_w8, gret, tb, x, w_out, fg, target):
    nt, rows = tb["nt"], tb["rows"]
    seq = nt * TM

    def pf(s):
        return jnp.where(s == 0, nt, jnp.minimum(s - 1, nt - 1))

    def xt(s):
        return jnp.clip(s - 2, 0, nt - 1)

    def body(proj_ref, cw_ref, g_ref, tt_ref, cr_ref, sr_ref, sgn_ref, dec_ref, xi_ref, zeta_ref, cd_ref,
             x_ref, w_ref, fg_ref, t_ref,
             conv_hbm, states_hbm, dh2_ref, dmx_ref, gwo_ref, gfg_ref, loss_ref,
             state, uhalo, mxs, convs, sts, lacc, out_sems):
        s = pl.program_id(0)
        slot = lax.rem(s, 2)
        mixed_ref = mxs.at[slot]
        conv_ref = convs.at[slot]
        states_ref = sts.at[slot]

        def conv_out(sl, tile):
            return pltpu.make_async_copy(convs.at[sl], conv_hbm.at[pl.ds(pl.multiple_of(tile * TM, TM), TM), :],
                                         out_sems.at[sl])

        def states_out(sl, tile):
            return pltpu.make_async_copy(sts.at[sl], states_hbm.at[pl.ds(pl.multiple_of(tile * NCH, NCH), NCH)],
                                         out_sems.at[2 + sl])

        @pl.when(s == 0)
        def _():
            state[...] = jnp.zeros_like(state)
            uhalo[...] = jnp.zeros_like(uhalo)
            mxs[...] = jnp.zeros_like(mxs)
            gwo_ref[...] = jnp.zeros_like(gwo_ref)
            gfg_ref[...] = jnp.zeros_like(gfg_ref)
            lacc[...] = jnp.zeros_like(lacc)

        @pl.when(s >= 2)
        def _():
            conv_out(slot, pf(s - 2)).wait()
            states_out(slot, pf(s - 2)).wait()

        valid = jnp.where(s >= 2, 1.0, 0.0)
        mx_prev = mxs.at[1 - slot]
        f3 = {}

        def f3_fwd():
            f3["h2"] = x_ref[...] + _dot(mx_prev[...], w_ref[...])

        def f3_loss():
            h2 = f3.pop("h2")
            ms = jnp.mean(h2 * h2, axis=-1, keepdims=True)
            rstd = lax.rsqrt(ms + EPS)
            yh = h2 * rstd
            g = fg_ref[...]
            e = (yh * g - t_ref[...]) * valid
            lacc[...] += jnp.sum(e * e, axis=0, keepdims=True)
            dy = e * (1.0 / D_MODEL)
            gfg_ref[...] += jnp.sum(dy * yh, axis=0, keepdims=True)
            dyh = dy * g
            dh2 = rstd * (dyh - yh * jnp.mean(dyh * yh, axis=-1, keepdims=True))
            dh2_ref[...] = dh2
            f3["db"] = dh2.astype(BF16)

        def f3_dmx():
            dmx_ref[...] = _dot_tb(f3["db"], w_ref[...]).astype(BF16)

        def f3_gw():
            gw = _dot_ta(mx_prev[...], f3["db"])
            for j in range(N_CHIPS):
                for hf in range(2):
                    r0 = j * 256 + hf * 128
                    gwo_ref[hf, j] += gw[r0:r0 + 128, :]

        cx = proj_ref[:, CX:CX + 512].astype(F32)
        cc = proj_ref[:, CC:CC + 512].astype(F32)
        u = cc * cx
        row = lax.broadcasted_iota(jnp.int32, (TM, D_CONV), 0)
        h7 = uhalo[7:8, :]
        h6 = uhalo[6:7, :]
        u1 = jnp.where(row == 0, h7, pltpu.roll(u, 1, 0))
        u2 = jnp.where(row == 0, h6, jnp.where(row == 1, h7, pltpu.roll(u, 2, 0)))
        conv = cw_ref[2:3, :] * u + cw_ref[1:2, :] * u1 + cw_ref[0:1, :] * u2
        uhalo[...] = u[TM - 8:TM, :]
        cb = proj_ref[:, CB:CB + 512].astype(F32)
        cg = proj_ref[:, CG:CG + 512].astype(F32)
        mixed_ref[:, 0:D_CONV] = (cb * conv * (cg * _sigmoid(cg))).astype(BF16)
        conv_ref[...] = conv.astype(BF16)
        f3_fwd()

        scale = HEAD_DIM ** -0.5
        H = range(RET_HEADS)
        st = [state[h] for h in H]
        between = [f3_loss, f3_dmx, f3_gw, None]
        rc_t, rs_t = _tile_rotary(tt_ref, cr_ref, sr_ref, sgn_ref)
        for c in range(NCH):
            r0 = c * CHUNK
            rc = rc_t[r0:r0 + CHUNK, :]
            rs = rs_t[r0:r0 + CHUNK, :]
            col = lambda base, h: slice(base + h * HEAD_DIM, base + (h + 1) * HEAD_DIM)
            rws = slice(r0, r0 + CHUNK)
            v = [proj_ref[rws, col(CV, h)] for h in H]
            qf = [_rot(proj_ref[rws, col(CQ, h)].astype(F32), rc, rs) * scale for h in H]
            kf = [_rot(proj_ref[rws, col(CK, h)].astype(F32), rc, rs) for h in H]
            stb = [t.astype(BF16) for t in st]
            for h in H:
                states_ref[c, h] = stb[h]
            a = [(_dot_tb(qf[h].astype(BF16), kf[h].astype(BF16)) * dec_ref[h]).astype(BF16) for h in H]
            o = [_dot(a[h], v[h]) + _dot((qf[h] * xi_ref[h]).astype(BF16), stb[h]) for h in H]
            st = [cd_ref[h, 0:1, :] * st[h] + _dot_ta((kf[h] * zeta_ref[h]).astype(BF16), v[h]) for h in H]
            for h in H:
                mu = jnp.mean(o[h], axis=-1, keepdims=True)
                d = o[h] - mu
                var = jnp.mean(d * d, axis=-1, keepdims=True)
                yh = d * lax.rsqrt(var + EPS)
                rg = proj_ref[rws, col(CR, h)].astype(F32)
                mixed_ref[rws, col(D_CONV, h)] = (yh * g_ref[:, col(0, h)] * (rg * _sigmoid(rg))).astype(BF16)
            if between[c] is not None:
                between[c]()
        for h in H:
            state[h] = st[h]

        @pl.when(s <= nt)
        def _():
            conv_out(slot, pf(s)).start()
            states_out(slot, pf(s)).start()

        @pl.when(s == nt + 1)
        def _():
            conv_out(1 - slot, pf(s - 1)).wait()
            states_out(1 - slot, pf(s - 1)).wait()
            tot = jnp.sum(lacc[...], axis=1, keepdims=True) * (0.5 / D_MODEL)
            loss_ref[...] = jnp.broadcast_to(tot, (1, 128))

    tile = lambda w: pl.BlockSpec((TM, w), lambda s: (pf(s), 0))
    xtile = lambda w: pl.BlockSpec((TM, w), lambda s: (xt(s), 0))
    any_spec = pl.BlockSpec(memory_space=pl.ANY)
    return pl.pallas_call(
        body, name="f2_mixer_fwd_f3_outproj_loss",
        grid=(nt + 2,),
        in_specs=[tile(N_PROJ_COLS), _resident((8, D_CONV)), _resident((1, D_RET)),
                  pl.BlockSpec((None, 8, HEAD_DIM), lambda s: (pf(s), 0, 0)),
                  _resident((TM, HEAD_DIM)), _resident((TM, HEAD_DIM)), _resident((8, HEAD_DIM)),
                  _resident((RET_HEADS, CHUNK, CHUNK)), _resident((RET_HEADS, CHUNK, HEAD_DIM)),
                  _resident((RET_HEADS, CHUNK, HEAD_DIM)), _resident((RET_HEADS, 8, HEAD_DIM)),
                  xtile(D_MODEL), _resident1((D_MODEL, D_MODEL)), _resident((1, D_MODEL)), xtile(D_MODEL)],
        out_specs=[any_spec, any_spec, xtile(D_MODEL), xtile(D_MODEL),
                   _resident((2, N_CHIPS, 128, D_MODEL)), _resident((1, D_MODEL)), _resident((1, 128))],
        out_shape=[jax.ShapeDtypeStruct((rows, D_CONV), BF16),
                   jax.ShapeDtypeStruct(((nt + 1) * NCH, RET_HEADS, HEAD_DIM, HEAD_DIM), BF16),
                   jax.ShapeDtypeStruct((seq, D_MODEL), F32),
                   jax.ShapeDtypeStruct((seq, D_MODEL), BF16),
                   jax.ShapeDtypeStruct((2, N_CHIPS, 128, D_MODEL), F32),
                   jax.ShapeDtypeStruct((1, D_MODEL), F32),
                   jax.ShapeDtypeStruct((1, 128), F32)],
        scratch_shapes=[pltpu.VMEM((RET_HEADS, HEAD_DIM, HEAD_DIM), F32), pltpu.VMEM((8, D_CONV), F32),
                        pltpu.VMEM((2, TM, D_MODEL), BF16), pltpu.VMEM((2, TM, D_CONV), BF16),
                        pltpu.VMEM((2, NCH, RET_HEADS, HEAD_DIM, HEAD_DIM), BF16),
                        pltpu.VMEM((1, D_MODEL), F32), pltpu.SemaphoreType.DMA((4,))],
        compiler_params=_cparams(("arbitrary",)),
    )(proj, conv_w8, gret, tb["tt"], tb["cr2"], tb["sr2"], tb["sgn"], tb["decay"], tb["xi"], tb["zeta"], tb["cd"],
      x, w_out, fg, target)


def _b2_b1a_call(proj, dmixed, conv_s, states, conv_w8, gret, tb, w_in_g, x, meta_tile, g1, dh2):
    nt, rows = tb["nt"], tb["rows"]
    seq = nt * TM

    def pb(r):
        return jnp.where(r == nt, nt, nt - 1 - r)

    def xprev(r):
        return jnp.clip(nt - r, 0, nt - 1)

    def body(proj_ref, dmx_ref, conv_ref, states_ref, cw_ref, g_ref, tt_ref, cr_ref, sr_ref, sgn_ref, dec_ref,
             xi_ref, zeta_ref, cd_ref, w_ref, x_ref, mt_ref, g1_ref, dh2_ref,
             dproj_hbm, gcw_ref, gg_ref, gx_ref, dmeta_ref, gn_ref,
             gstate, dchalo, dps, out_sems):
        r = pl.program_id(0)
        live = jnp.where(r == nt, 0.0, 1.0)
        slot = lax.rem(r, 2)
        dproj_ref = dps.at[slot]

        class to_hbm:
            def __init__(self, s, tile):
                self.copies = [pltpu.make_async_copy(dps.at[s, :, j * 1024:(j + 1) * 1024], dproj_hbm.at[tile, j],
                                                     out_sems.at[N_CHIPS * s + j]) for j in range(N_CHIPS)]

            def start(self):
                for cp in self.copies:
                    cp.start()

            def wait(self):
                for cp in self.copies:
                    cp.wait()

        @pl.when(r == 0)
        def _():
            gstate[...] = jnp.zeros_like(gstate)
            dchalo[...] = jnp.zeros_like(dchalo)
            gcw_ref[...] = jnp.zeros_like(gcw_ref)
            gg_ref[...] = jnp.zeros_like(gg_ref)
            gn_ref[...] = jnp.zeros_like(gn_ref)
            dps[...] = jnp.zeros_like(dps)

        @pl.when(r >= 2)
        def _():
            to_hbm(slot, pb(r - 2)).wait()

        dprev = dps.at[1 - slot]
        pieces = []

        def emit_piece():
            j = len(pieces)
            if j < N_CHIPS:
                p = _dot_tb(dprev[:, j * 1024:(j + 1) * 1024], w_ref[j])
                pieces.append(p if j == 0 else pieces[-1] + p)

        cx = proj_ref[:, CX:CX + 512].astype(F32)
        cb = proj_ref[:, CB:CB + 512].astype(F32)
        cc = proj_ref[:, CC:CC + 512].astype(F32)
        cg = proj_ref[:, CG:CG + 512].astype(F32)
        dco = dmx_ref[:, 0:D_CONV].astype(F32) * live
        conv = conv_ref[...].astype(F32)
        sg = _sigmoid(cg)
        sil = cg * sg
        t = dco * conv
        dproj_ref[:, CB:CB + 512] = (t * sil).astype(BF16)
        dproj_ref[:, CG:CG + 512] = (t * cb * (sg * (1.0 + cg * (1.0 - sg)))).astype(BF16)
        dconv = dco * cb * sil
        row = lax.broadcasted_iota(jnp.int32, (TM, D_CONV), 0)
        n0 = dchalo[0:1, :]
        n1 = dchalo[1:2, :]
        dc1 = jnp.where(row == TM - 1, n0, pltpu.roll(dconv, TM - 1, 0))
        dc2 = jnp.where(row == TM - 2, n0, jnp.where(row == TM - 1, n1, pltpu.roll(dconv, TM - 2, 0)))
        dchalo[...] = dconv[0:8, :]
        du = cw_ref[2:3, :] * dconv + cw_ref[1:2, :] * dc1 + cw_ref[0:1, :] * dc2
        u = cc * cx
        gcw_ref[2:3, :] += jnp.sum(u * dconv, axis=0, keepdims=True)
        gcw_ref[1:2, :] += jnp.sum(u * dc1, axis=0, keepdims=True)
        gcw_ref[0:1, :] += jnp.sum(u * dc2, axis=0, keepdims=True)
        dproj_ref[:, CC:CC + 512] = (du * cx).astype(BF16)
        dproj_ref[:, CX:CX + 512] = (du * cc).astype(BF16)
        emit_piece()
        emit_piece()

        scale = HEAD_DIM ** -0.5
        gs = {h: gstate[h] for h in range(RET_HEADS)}
        gg = {h: jnp.zeros((1, HEAD_DIM), F32) for h in range(RET_HEADS)}
        col = lambda base, h: slice(base + h * HEAD_DIM, base + (h + 1) * HEAD_DIM)
        rw = lambda c: slice(c * CHUNK, (c + 1) * CHUNK)
        rc_t, rs_t = _tile_rotary(tt_ref, cr_ref, sr_ref, sgn_ref)
        for c0 in range(NCH - CHUNK_GROUP, -1, -CHUNK_GROUP):
            cs = range(c0 + CHUNK_GROUP - 1, c0 - 1, -1)
            U = [(c, h) for c in cs for h in range(RET_HEADS)]
            rc = {c: rc_t[rw(c), :] for c in cs}
            rs = {c: rs_t[rw(c), :] for c in cs}
            v = {(c, h): proj_ref[rw(c), col(CV, h)] for c, h in U}
            stb = {(c, h): states_ref[c, h] for c, h in U}
            qf = {(c, h): _rot(proj_ref[rw(c), col(CQ, h)].astype(F32), rc[c], rs[c]) * scale for c, h in U}
            kf = {(c, h): _rot(proj_ref[rw(c), col(CK, h)].astype(F32), rc[c], rs[c]) for c, h in U}
            qb = {u: qf[u].astype(BF16) for u in U}
            kb = {u: kf[u].astype(BF16) for u in U}
            qxb = {(c, h): (qf[c, h] * xi_ref[h]).astype(BF16) for c, h in U}
            kzb = {(c, h): (kf[c, h] * zeta_ref[h]).astype(BF16) for c, h in U}
            ab = {(c, h): (_dot_tb(qb[c, h], kb[c, h]) * dec_ref[h]).astype(BF16) for c, h in U}
            o = {u: _dot(ab[u], v[u]) + _dot(qxb[u], stb[u]) for u in U}
            emit_piece()
            dob = {}
            for c, h in U:
                mu = jnp.mean(o[c, h], axis=-1, keepdims=True)
                d = o[c, h] - mu
                var = jnp.mean(d * d, axis=-1, keepdims=True)
                rstd = lax.rsqrt(var + EPS)
                yh = d * rstd
                g = g_ref[:, col(0, h)]
                rg = proj_ref[rw(c), col(CR, h)].astype(F32)
                dro = dmx_ref[rw(c), col(D_CONV, h)].astype(F32) * live
                sg = _sigmoid(rg)
                dproj_ref[rw(c), col(CR, h)] = (dro * (yh * g) * (sg * (1.0 + rg * (1.0 - sg)))).astype(BF16)
                dret = dro * (rg * sg)
                gg[h] = gg[h] + jnp.sum(dret * yh, axis=0, keepdims=True)
                dyh = dret * g
                do = rstd * (dyh - jnp.mean(dyh, axis=-1, keepdims=True)
                             - yh * jnp.mean(dyh * yh, axis=-1, keepdims=True))
                dob[c, h] = do.astype(BF16)
            dv1 = {u: _dot_ta(ab[u], dob[u]) for u in U}
            ds = {(c, h): (_dot_tb(dob[c, h], v[c, h]) * dec_ref[h]).astype(BF16) for c, h in U}
            gup = {u: _dot_ta(qxb[u], dob[u]) for u in U}
            dq = {(c, h): _dot(ds[c, h], kb[c, h]) + _dot_tb(dob[c, h], stb[c, h]) * xi_ref[h] for c, h in U}
            dk1 = {u: _dot_ta(ds[u], qb[u]) for u in U}
            emit_piece()
            for c, h in U:
                gsb = gs[h].astype(BF16)
                dv = dv1[c, h] + _dot(kzb[c, h], gsb)
                dk = dk1[c, h] + _dot_tb(v[c, h], gsb) * zeta_ref[h]
                gs[h] = cd_ref[h, 0:1, :] * gs[h] + gup[c, h]
                dproj_ref[rw(c), col(CQ, h)] = (_rot_t(dq[c, h], rc[c], rs[c]) * scale).astype(BF16)
                dproj_ref[rw(c), col(CK, h)] = _rot_t(dk, rc[c], rs[c]).astype(BF16)
                dproj_ref[rw(c), col(CV, h)] = dv.astype(BF16)
        for h in range(RET_HEADS):
            gstate[h] = gs[h]
            gg_ref[:, col(0, h)] += gg[h]

        while len(pieces) < N_CHIPS:
            emit_piece()

        def norm_bwd(dhn, hx):
            ms = jnp.mean(hx * hx, axis=-1, keepdims=True)
            rstd1 = lax.rsqrt(ms + EPS)
            xh = hx * rstd1
            gn_ref[...] += jnp.sum(dhn * xh, axis=0, keepdims=True)
            dxh = dhn * g1_ref[...]
            return rstd1 * (dxh - xh * jnp.mean(dxh * xh, axis=-1, keepdims=True))

        gx_ref[...] = norm_bwd(pieces[-1], x_ref[...]) + dh2_ref[...]

        @pl.when(r < nt)
        def _():
            to_hbm(slot, pb(r)).start()

        @pl.when(r == nt)
        def _():
            to_hbm(slot, pb(r)).start()
            mrows = slice(TM - N_META, TM)
            d16 = dproj_ref[mrows, :]
            dhn16 = _dot_tb(d16[:, 0:1024], w_ref[0])
            for j in range(1, N_CHIPS):
                dhn16 += _dot_tb(d16[:, j * 1024:(j + 1) * 1024], w_ref[j])
            dmeta_ref[...] = norm_bwd(dhn16, mt_ref[mrows, :])
            to_hbm(1 - slot, pb(r - 1)).wait()
            to_hbm(slot, pb(r)).wait()

    tile = lambda w: pl.BlockSpec((TM, w), lambda r: (pb(r), 0))
    xtile = pl.BlockSpec((TM, D_MODEL), lambda r: (xprev(r), 0))
    return pl.pallas_call(
        body, name="b2_mixer_bwd_b1a_inproj_bwd_x",
        grid=(nt + 1,),
        in_specs=[tile(N_PROJ_COLS),
                  pl.BlockSpec((TM, D_MODEL), lambda r: (jnp.minimum(pb(r), nt - 1), 0)),
                  tile(D_CONV),
                  pl.BlockSpec((NCH, RET_HEADS, HEAD_DIM, HEAD_DIM), lambda r: (pb(r), 0, 0, 0)),
                  _resident((8, D_CONV)), _resident((1, D_RET)),
                  pl.BlockSpec((None, 8, HEAD_DIM), lambda r: (pb(r), 0, 0)),
                  _resident((TM, HEAD_DIM)), _resident((TM, HEAD_DIM)), _resident((8, HEAD_DIM)),
                  _resident((RET_HEADS, CHUNK, CHUNK)),
                  _resident((RET_HEADS, CHUNK, HEAD_DIM)),
                  _resident((RET_HEADS, CHUNK, HEAD_DIM)), _resident((RET_HEADS, 8, HEAD_DIM)),
                  _resident1((N_CHIPS, D_MODEL, 1024)), xtile, _resident1((TM, D_MODEL)), _resident((1, D_MODEL)),
                  xtile],
        out_specs=[pl.BlockSpec(memory_space=pl.ANY), _resident((8, D_CONV)), _resident((1, D_RET)),
                   xtile, _resident((N_META, D_MODEL)), _resident((1, D_MODEL))],
        out_shape=[jax.ShapeDtypeStruct((nt + 1, N_CHIPS, TM, 1024), BF16),
                   jax.ShapeDtypeStruct((8, D_CONV), F32),
                   jax.ShapeDtypeStruct((1, D_RET), F32),
                   jax.ShapeDtypeStruct((seq, D_MODEL), F32),
                   jax.ShapeDtypeStruct((N_META, D_MODEL), F32),
                   jax.ShapeDtypeStruct((1, D_MODEL), F32)],
        scratch_shapes=[pltpu.VMEM((RET_HEADS, HEAD_DIM, HEAD_DIM), F32), pltpu.VMEM((8, D_CONV), F32),
                        pltpu.VMEM((2, TM, N_PROJ_COLS), BF16), pltpu.SemaphoreType.DMA((2 * N_CHIPS,))],
        compiler_params=_cparams(("arbitrary",), vmem=VMEM_LIMIT_MAX),
    )(proj, dmixed, conv_s, states, conv_w8, gret, tb["tt"], tb["cr2"], tb["sr2"], tb["sgn"], tb["decay"],
      tb["xi"], tb["zeta"], tb["cd"], w_in_g, x, meta_tile, g1, dh2)


REL = (2, 1, 3)
SMALL_ROWS = 24
HALF_STEP = 4


def _rcopy(src, dst, send_sems, recv_sems, k, to):
    return pltpu.make_async_remote_copy(src_ref=src, dst_ref=dst, send_sem=send_sems.at[k],
                                        recv_sem=recv_sems.at[k], device_id=to, device_id_type=MESH_ID)


def _b1b_reduce_call(order, hnt, dproj, gwo, pack, nt):
    nk = nt + 1
    last = nk - 1
    any_spec = pl.BlockSpec(memory_space=pl.ANY)

    def body(order_ref, a_ref, b_ref, gwo_hbm, pack_hbm, gwin_hbm, gwout_hbm, tot_hbm,
             acc, sb, abuf, pb, bbuf, fin, go, ao, pbo, bo, fino, slots, totv, send_sems, recv_sems, loc_sems):
        jj, k = pl.program_id(0), pl.program_id(1)
        x, y, c = lax.axis_index("x"), lax.axis_index("y"), lax.axis_index("c")
        me, myid, sib = 2 * x + y, 4 * x + 2 * y + c, (x, y, 1 - c)
        rc = functools.partial(_rcopy, send_sems=send_sems, recv_sems=recv_sems)
        peers = [((1 - x) if r & 2 else x, (1 - y) if r & 1 else y, c) for r in REL]
        kids = [jnp.bitwise_xor(me, r) for r in REL]

        def dev_peer(r):
            return ((1 - x) if r & 4 else x, (1 - y) if r & 2 else y, (1 - c) if r & 1 else c)

        own_go = pltpu.make_async_copy(gwo_hbm.at[c], go, loc_sems.at[0])
        own_pack = pltpu.make_async_copy(pack_hbm, slots.at[0], loc_sems.at[1])
        wo_half = rc(gwo_hbm.at[1 - c], ao, k=8, to=sib)
        wo_part = [rc(pbo.at[kids[p]], bo.at[p], k=9 + p, to=peers[p]) for p in range(3)]
        sm = [rc(pack_hbm, slots.at[r], k=12 + r, to=dev_peer(r)) for r in range(1, N_DEV)]
        half = [rc(sb.at[j % 2, 1 - c], abuf.at[j], k=j, to=sib) for j in range(N_CHIPS)]
        part = [rc(pb.at[p], bbuf.at[p], k=4 + p, to=peers[p]) for p in range(3)]

        @pl.when(jnp.logical_and(jj == 0, k == 0))
        def _():
            own_go.start()
            own_pack.start()
            wo_half.start()
            for cp in sm:
                cp.start()

        @pl.when(k == 0)
        def _():
            acc[...] = jnp.zeros_like(acc)

        acc[0] += _dot(a_ref[0:512, :], b_ref[...])
        acc[1] += _dot(a_ref[512:1024, :], b_ref[...])

        @pl.when(k == HALF_STEP)
        def _():
            @pl.when(jj == 0)
            def _():
                own_go.wait()
                wo_half.wait_recv()
                for j in range(N_CHIPS):
                    go[j] = go[j] + ao[j]
                pbo[...] = go[...].astype(BF16)
                for cp in wo_part:
                    cp.start()

            for p in range(3):
                @pl.when(jj == p + 1)
                def _(p=p):
                    half[p].wait_recv()
                    half[p].wait_send()
                    pb[p] = (sb[p % 2, c] + abuf[p]).astype(BF16)
                    part[p].start()

        @pl.when(k == last)
        def _():
            for j in range(N_CHIPS):
                @pl.when(jj == j)
                def _(j=j):
                    sb[j % 2] = acc[...]
                    half[j].start()

        @pl.when(jnp.logical_and(jj == N_CHIPS - 1, k == last))
        def _():
            half[3].wait_recv()
            own = sb[1, c] + abuf[3]
            for cp in part:
                cp.wait_recv()
            fin[c] = ((own + bbuf[0].astype(F32)) + bbuf[1].astype(F32)) + bbuf[2].astype(F32)
            done = rc(fin.at[c], fin.at[c], k=7, to=sib)
            done.start()
            for cp in wo_part:
                cp.wait_recv()
            fino[c] = ((go[me] + bo[0].astype(F32)) + bo[1].astype(F32)) + bo[2].astype(F32)
            done_o = rc(fino.at[c], fino.at[c], k=12, to=sib)
            done_o.start()
            own_pack.wait()
            for cp in sm:
                cp.wait_recv()
            tot = slots[myid]
            for a in range(1, N_DEV):
                tot = tot + slots[jnp.bitwise_xor(myid, a)]
            totv[...] = tot
            out_t = pltpu.make_async_copy(totv, tot_hbm, loc_sems.at[1])
            out_t.start()
            rc(fin.at[1 - c], fin.at[1 - c], k=7, to=sib).wait_recv()
            out_w = pltpu.make_async_copy(fin, gwin_hbm, loc_sems.at[0])
            out_w.start()
            rc(fino.at[1 - c], fino.at[1 - c], k=12, to=sib).wait_recv()
            out_o = pltpu.make_async_copy(fino, gwout_hbm, loc_sems.at[2])
            out_o.start()
            for cp in [half[3]] + part + [done, wo_half] + wo_part + [done_o] + sm:
                cp.wait_send()
            out_t.wait()
            out_w.wait()
            out_o.wait()

    grid_spec = pltpu.PrefetchScalarGridSpec(
        num_scalar_prefetch=1,
        grid=(N_CHIPS, nk),
        in_specs=[pl.BlockSpec((None, D_MODEL, TM), lambda j, k, o: (k, 0, 0)),
                  pl.BlockSpec((None, None, TM, 1024), lambda j, k, o: (k, o[j], 0, 0)),
                  any_spec, any_spec],
        out_specs=[any_spec, any_spec, any_spec],
        scratch_shapes=[
            pltpu.VMEM((2, 512, 1024), F32),
            pltpu.VMEM((2, 2, 512, 1024), F32),
            pltpu.VMEM((N_CHIPS, 512, 1024), F32),
            pltpu.VMEM((3, 512, 1024), BF16),
            pltpu.VMEM((3, 512, 1024), BF16),
            pltpu.VMEM((2, 512, 1024), F32),
            pltpu.VMEM((N_CHIPS, 128, D_MODEL), F32),
            pltpu.VMEM((N_CHIPS, 128, D_MODEL), F32),
            pltpu.VMEM((N_CHIPS, 128, D_MODEL), BF16),
            pltpu.VMEM((3, 128, D_MODEL), BF16),
            pltpu.VMEM((2, 128, D_MODEL), F32),
            pltpu.VMEM((N_DEV, SMALL_ROWS, D_MODEL), F32),
            pltpu.VMEM((SMALL_ROWS, D_MODEL), F32),
            pltpu.SemaphoreType.DMA((20,)), pltpu.SemaphoreType.DMA((20,)), pltpu.SemaphoreType.DMA((3,))])
    return pl.pallas_call(
        body, name="b1b_inproj_bwd_w_reduce",
        grid_spec=grid_spec,
        out_shape=[jax.ShapeDtypeStruct((2, 512, 1024), F32),
                   jax.ShapeDtypeStruct((2, 128, D_MODEL), F32),
                   jax.ShapeDtypeStruct((SMALL_ROWS, D_MODEL), F32)],
        compiler_params=_cparams(("arbitrary", "arbitrary")),
    )(order, hnt, dproj, gwo, pack)


def _local_step(me, x, target, g1, gret, fg, win_sh, wout_sh, small_sh):
    seq = x.shape[0]
    tb = _tables(seq)
    nt = tb["nt"]
    g1r, gretr, fgr = g1.reshape(1, -1), gret.reshape(1, -1), fg.reshape(1, -1)
    order = jnp.stack([me, me ^ REL[0], me ^ REL[1], me ^ REL[2]]).astype(jnp.int32)

    proj, hnt, w_in_g, w_out_g, meta_tile, conv_w8 = _f1_gather_call(order, x, g1r, win_sh, wout_sh, small_sh, nt)
    w_out = w_out_g.reshape(D_MODEL, D_MODEL)
    conv_s, states, dh2, dmixed, g_wout, g_fg, loss = _f2_f3_call(proj, conv_w8, gretr, tb, x, w_out, fgr, target)
    dproj, g_cw8, g_gret, grad_x, g_meta, g_g1 = _b2_b1a_call(proj, dmixed, conv_s, states, conv_w8, gretr, tb,
                                                              w_in_g, x, meta_tile, g1r, dh2)
    return loss, grad_x, dict(w_out=g_wout, meta=g_meta, conv_w=g_cw8[0:3], norm1_g=g_g1,
                              ret_norm_g=g_gret, final_g=g_fg), hnt, dproj


def _adamw_update(w_ref, g_ref, m_ref, v_ref, d_ref, nm_ref, nv_ref):
    gg = g_ref[...]
    nm = ADAM_B1 * m_ref[...] + (1.0 - ADAM_B1) * gg
    nv = ADAM_B2 * v_ref[...] + (1.0 - ADAM_B2) * (gg * gg)
    m_hat = nm / (1.0 - ADAM_B1 ** ADAM_STEP)
    v_hat = nv / (1.0 - ADAM_B2 ** ADAM_STEP)
    d_ref[...] = -ADAM_LR * (m_hat / (jnp.sqrt(v_hat) + ADAM_EPS) + ADAM_WD * w_ref[...])
    nm_ref[...] = nm
    nv_ref[...] = nv


def _adamw_small_call(me, tot, ws, ms, vs):
    n = len(ws)

    def body(me_ref, tmeta_ref, tvec_ref, tconv_ref, *refs):
        ins, outs = refs[:3 * n], refs[3 * n:]
        g_refs, loss_ref, upd = outs[0:n], outs[n], outs[n + 1:]
        g_refs[0][...] = tmeta_ref[...]
        g_refs[1][...] = tvec_ref[0:1, :]
        g_refs[2][...] = tconv_ref[3:6, :]
        g_refs[3][...] = tvec_ref[2:3, 0:D_RET]
        g_refs[4][...] = tvec_ref[1:2, :]
        loss_ref[...] = tvec_ref[6:7, 0:1]
        for i in range(n):
            _adamw_update(ins[i], g_refs[i], ins[n + i], ins[2 * n + i], upd[i], upd[n + i], upd[2 * n + i])

    whole = lambda a: pl.BlockSpec(a.shape, lambda i, m: (0,) * a.ndim)
    shapes = [jax.ShapeDtypeStruct(w.shape, F32) for w in ws]
    out_shape = shapes + [jax.ShapeDtypeStruct((1, 1), F32)] + shapes * 3
    grid_spec = pltpu.PrefetchScalarGridSpec(
        num_scalar_prefetch=1, grid=(1,),
        in_specs=[pl.BlockSpec((N_META, 256), lambda i, m: (0, m[0])),
                  pl.BlockSpec((8, D_MODEL), lambda i, m: (N_META // 8, 0)),
                  pl.BlockSpec((8, 128), lambda i, m: (N_META // 8, m[0]))] + [whole(a) for a in ws + ms + vs],
        out_specs=[whole(s) for s in out_shape])
    outs = pl.pallas_call(body, name="adamw_small", grid_spec=grid_spec, out_shape=out_shape,
                          compiler_params=_cparams(("arbitrary",)))(me.reshape(1), tot, tot, tot, *ws, *ms, *vs)
    return outs[:n], outs[n], outs[n + 1:2 * n + 1], outs[2 * n + 1:3 * n + 1], outs[3 * n + 1:]


def _adamw_call(w, g, m, v, name):
    shape = w.shape
    w2, g2, m2, v2 = (a.reshape(-1, shape[-1]) for a in (w, g, m, v))
    rows, cols = w2.shape
    br = 256 if rows % 256 == 0 else rows
    body = functools.partial(_adamw_update)
    spec = pl.BlockSpec((br, cols), lambda i: (i, 0))
    outs = pl.pallas_call(
        body, name=name, grid=(rows // br,),
        in_specs=[spec] * 4, out_specs=[spec] * 3,
        out_shape=[jax.ShapeDtypeStruct((rows, cols), F32)] * 3,
        compiler_params=_cparams(("arbitrary",)),
    )(w2, g2, m2, v2)
    return tuple(o.reshape(shape) for o in outs)


def _pad_to(a, rows, cols):
    return jnp.pad(a, ((0, rows - a.shape[0]), (0, cols - a.shape[1])))


def kernel(x, meta, norm1_g, w_in, conv_w, ret_norm_g, w_out, final_g, loss_target, m_meta, m_norm1_g, m_w_in, m_conv_w, m_ret_norm_g, m_w_out, m_final_g, v_meta, v_norm1_g, v_w_in, v_conv_w, v_ret_norm_g, v_w_out, v_final_g):
    me = 2 * lax.axis_index("x") + lax.axis_index("y")

    small_sh = jnp.concatenate([meta, _pad_to(conv_w, 8, 256)], axis=0)
    loss, grad_x, g, hnt, dproj = _local_step(me, x[0], loss_target[0], norm1_g, ret_norm_g, final_g,
                                              w_in, w_out, small_sh)

    vec = jnp.concatenate([g["norm1_g"], g["final_g"], _pad_to(g["ret_norm_g"], 1, D_MODEL),
                           _pad_to(g["conv_w"], 3, D_MODEL), _pad_to(loss, 2, D_MODEL)], axis=0)
    pack = jnp.concatenate([g["meta"], vec], axis=0)
    order = jnp.stack([me ^ REL[0], me ^ REL[1], me ^ REL[2], me]).astype(jnp.int32)
    g_win, g_wout, tot = _b1b_reduce_call(order, hnt, dproj, g["w_out"], pack, x.shape[1] // TM)
    g_win, g_wout = g_win.reshape(D_MODEL, 1024), g_wout.reshape(256, D_MODEL)

    ws = [meta, norm1_g, w_in, conv_w, ret_norm_g, w_out, final_g]
    ms = [m_meta, m_norm1_g, m_w_in, m_conv_w, m_ret_norm_g, m_w_out, m_final_g]
    vs = [v_meta, v_norm1_g, v_w_in, v_conv_w, v_ret_norm_g, v_w_out, v_final_g]
    names = ["meta", "norm1_g", "w_in", "conv_w", "ret_norm_g", "w_out", "final_g"]
    as2d = lambda a: a.reshape(1, -1) if a.ndim == 1 else a
    big = {names.index("w_in"): g_win, names.index("w_out"): g_wout}
    small = [i for i in range(len(names)) if i not in big]
    grads, deltas, new_ms, new_vs = [None] * 7, [None] * 7, [None] * 7, [None] * 7
    for i, g_ in big.items():
        grads[i] = g_
        deltas[i], new_ms[i], new_vs[i] = _adamw_call(ws[i], g_, ms[i], vs[i], "adamw_" + names[i])
    sg, loss_tot, sd, sm_, sv = _adamw_small_call(me.astype(jnp.int32), tot,
                                                  *[[as2d(t[i]) for i in small] for t in (ws, ms, vs)])
    for j, i in enumerate(small):
        grads[i], deltas[i], new_ms[i], new_vs[i] = (o[j].reshape(ws[i].shape) for o in (sg, sd, sm_, sv))
    return (loss_tot.reshape(()), grad_x[None], *grads, *deltas, *new_ms, *new_vs)
```

```python
import functools

import jax
import jax.numpy as jnp
import numpy as np
from jax import lax
from jax.experimental import pallas as pl
from jax.experimental.pallas import tpu as pltpu

F32 = jnp.float32
BF16 = jnp.bfloat16

D_MODEL = 1024
N_META = 16
D_CONV = 512
D_RET = 512
RET_HEADS = 4
HEAD_DIM = 128
CHUNK = 128
N_PROJ_COLS = 4096
ROPE_BASE = 10000.0
EPS = 1e-6
N_CHIPS = 4
N_DEV = 8

ADAM_LR = 0.001
ADAM_B1 = 0.9
ADAM_B2 = 0.999
ADAM_EPS = 1e-08
ADAM_WD = 0.01
ADAM_STEP = 10

TM = 512
NCH = TM // CHUNK
CHUNK_GROUP = 2
VMEM_LIMIT = 56 * 1024 * 1024
VMEM_LIMIT_MAX = 63 * 1024 * 1024

CX, CB, CC, CG, CQ, CK, CV, CR = (i * 512 for i in range(8))

MESH_ID = pl.DeviceIdType.MESH


def _cparams(sem=None, vmem=VMEM_LIMIT, **kw):
    return pltpu.CompilerParams(dimension_semantics=sem, vmem_limit_bytes=vmem, **kw)


def _sigmoid(x):
    return 1.0 / (1.0 + jnp.exp(-x))


def _dot(a, b):
    return jnp.dot(a, b, preferred_element_type=F32)


def _dot_tb(a, b):
    return lax.dot_general(a, b, (((1,), (1,)), ((), ())), preferred_element_type=F32)


def _dot_ta(a, b):
    return lax.dot_general(a, b, (((0,), (0,)), ((), ())), preferred_element_type=F32)


def _resident(shape):
    nd = len(shape)
    return pl.BlockSpec(shape, lambda *_: (0,) * nd)


def _resident1(shape):
    nd = len(shape)
    return pl.BlockSpec(shape, lambda *_: (0,) * nd, pipeline_mode=pl.Buffered(1))


def _tables(seq):
    f32 = np.float32
    nt = seq // TM
    rows = seq + TM
    half = HEAD_DIM // 2
    freqs = (f32(1.0) / (f32(ROPE_BASE) ** (np.arange(half, dtype=f32) / f32(half)))).astype(f32)
    tile_start = np.concatenate([np.arange(nt, dtype=f32), -np.ones((1,), f32)]) * f32(TM)
    ang_t = tile_start[:, None] * freqs[None, :]
    ang_r = (np.arange(TM, dtype=f32) + f32(N_META))[:, None] * freqs[None, :]
    dup = lambda a: np.concatenate([a, a], axis=-1).astype(f32)
    tt = np.stack([dup(np.cos(ang_t)), dup(np.sin(ang_t))], axis=1)
    tt = np.pad(tt, ((0, 0), (0, 6), (0, 0)))
    cr2, sr2 = dup(np.cos(ang_r)), dup(np.sin(ang_r))
    sgn = np.concatenate([-np.ones((8, half), f32), np.ones((8, half), f32)], axis=-1)
    log_g = np.log(f32(1.0) - f32(2.0) ** (f32(-5.0) - np.arange(RET_HEADS, dtype=f32))).astype(f32)
    idx = np.arange(CHUNK, dtype=f32)
    diff = idx[:, None] - idx[None, :]
    decay = np.where(diff[None] >= 0, np.exp(diff[None] * log_g[:, None, None]), f32(0.0)).astype(f32)
    zeta = np.exp((f32(CHUNK - 1) - idx)[None, :] * log_g[:, None]).astype(f32)
    xi = np.exp((idx + f32(1.0))[None, :] * log_g[:, None]).astype(f32)
    cd = np.exp(f32(CHUNK) * log_g).astype(f32)
    zeta_b = np.broadcast_to(zeta[:, :, None], (RET_HEADS, CHUNK, HEAD_DIM))
    xi_b = np.broadcast_to(xi[:, :, None], (RET_HEADS, CHUNK, HEAD_DIM))
    cd_b = np.broadcast_to(cd[:, None, None], (RET_HEADS, 8, HEAD_DIM))
    tables = dict(tt=tt, cr2=cr2, sr2=sr2, sgn=sgn, decay=decay, zeta=zeta_b, xi=xi_b, cd=cd_b)
    return dict(nt=nt, rows=rows, **{k: jnp.asarray(np.ascontiguousarray(v, dtype=f32)) for k, v in tables.items()})


def _tile_rotary(tt_ref, cr_ref, sr_ref, sgn_ref):
    ct, st = tt_ref[0:1, :], tt_ref[1:2, :]
    cr, sr = cr_ref[...], sr_ref[...]
    return ct * cr - st * sr, (st * cr + ct * sr) * sgn_ref[0:1, :]


def _rot(t, rc, rs):
    return t * rc + pltpu.roll(t, HEAD_DIM // 2, 1) * rs


def _rot_t(dt, rc, rs):
    return dt * rc + pltpu.roll(dt * rs, HEAD_DIM // 2, 1)


def _f1_gather_call(order, x, g1, win_sh, wout_sh, meta_sh, convw_sh, nt):
    nk = nt + 1
    rows = nk * TM
    any_spec = pl.BlockSpec(memory_space=pl.ANY)

    def body(order_ref, x_ref, g_ref, meta_ref, convw_ref, win_hbm, wout_hbm,
             proj_ref, hnt_ref, wg_hbm, wog_hbm, mt_hbm, cw_hbm,
             wg, wog, smg, mt, cw, hbs, send_sems, recv_sems, loc_sems):
        jj, k = pl.program_id(0), pl.program_id(1)
        x, y, c = lax.axis_index("x"), lax.axis_index("y"), lax.axis_index("c")
        me, sib = 2 * x + y, (x, y, 1 - c)
        rc = functools.partial(_rcopy, send_sems=send_sems, recv_sems=recv_sems)
        peers = [((1 - x) if r & 2 else x, (1 - y) if r & 1 else y, c) for r in REL]
        kids = [jnp.bitwise_xor(me, r) for r in REL]
        hw, ho = pl.ds(pl.multiple_of(c * 512, 512), 512), pl.ds(pl.multiple_of(c * 128, 128), 128)
        hw2 = pl.ds(pl.multiple_of((1 - c) * 512, 512), 512)
        ho2 = pl.ds(pl.multiple_of((1 - c) * 128, 128), 128)
        at = lambda j_, k_: jnp.logical_and(jj == j_, k == k_)

        sm_cp = [rc(smg.at[me], smg.at[me], k=p, to=peers[p]) for p in range(3)]
        win_cp = [rc(wg.at[me, hw], wg.at[me, hw], k=3 + p, to=peers[p]) for p in range(3)]
        wout_cp = [rc(wog.at[me, ho], wog.at[me, ho], k=6 + p, to=peers[p]) for p in range(3)]
        sm_in = [rc(smg.at[me], smg.at[kids[p]], k=p, to=sib) for p in range(3)]
        win_in = [rc(wg.at[me, hw], wg.at[kids[p], hw], k=3 + p, to=sib) for p in range(3)]
        wout_in = [rc(wog.at[me, ho], wog.at[kids[p], ho], k=6 + p, to=sib) for p in range(3)]
        win_fw = [rc(wg.at[kids[p], hw], wg.at[kids[p], hw], k=9 + p, to=sib) for p in range(3)]
        wout_fw = [rc(wog.at[kids[p], ho], wog.at[kids[p], ho], k=12 + p, to=sib) for p in range(3)]
        win_fw_in = [rc(wg.at[kids[p], hw2], wg.at[kids[p], hw2], k=9 + p, to=sib) for p in range(3)]
        wout_fw_in = [rc(wog.at[kids[p], ho2], wog.at[kids[p], ho2], k=12 + p, to=sib) for p in range(3)]

        def stage_own():
            for rows_, sends in ((hw, win_cp), (hw2, [])):
                cp = pltpu.make_async_copy(win_hbm.at[rows_], mt, loc_sems.at[0])
                cp.start()
                cp.wait()
                wg[me, rows_, :] = mt[...].astype(BF16)
                for s_ in sends:
                    s_.start()
            cp = pltpu.make_async_copy(wout_hbm, mt.at[0:256], loc_sems.at[1])
            cp.start()
            cp.wait()
            wog[me] = mt[0:256, :].astype(BF16)
            for s_ in wout_cp:
                s_.start()
        out_wg = pltpu.make_async_copy(wg, wg_hbm, loc_sems.at[3])
        out_wog = pltpu.make_async_copy(wog, wog_hbm, loc_sems.at[4])
        out_mt = pltpu.make_async_copy(mt, mt_hbm, loc_sems.at[5])
        out_cw = pltpu.make_async_copy(cw, cw_hbm, loc_sems.at[6])

        def pass_on(p):
            win_in[p].wait_recv()
            win_fw[p].start()

        @pl.when(k <= 1)
        def _():
            @pl.when(at(0, 0))
            def _():
                smg[me] = jnp.zeros((SMALL_ROWS, 256), F32)
                smg[me, 0:N_META, :] = meta_ref[...]
                smg[me, N_META:N_META + 3, 0:128] = convw_ref[...]
                for cp in sm_cp:
                    cp.start()
                stage_own()

            for p in range(3):
                @pl.when(at(p + 1, 0))
                def _(p=p):
                    win_fw_in[p].wait_recv()

            @pl.when(at(1, 1))
            def _():
                pass_on(1)

            @pl.when(at(3, 0))
            def _():
                out_wg.start()

            @pl.when(at(3, 1))
            def _():
                for p in range(3):
                    wout_in[p].wait_recv()
                    wout_fw[p].start()

        @pl.when(jnp.logical_and(jj == 0, k >= nk - 2))
        def _():
            @pl.when(k == nk - 2)
            def _():
                for cp in sm_in:
                    cp.wait_recv()
                mt[...] = jnp.zeros_like(mt)
                cw[...] = jnp.zeros_like(cw)
                for j in range(N_CHIPS):
                    mt[TM - N_META:TM, j * 256:(j + 1) * 256] = smg[j, 0:N_META, :]
                    cw[0:3, j * 128:(j + 1) * 128] = smg[j, N_META:N_META + 3, 0:128]
                out_mt.start()
                out_cw.start()

            @pl.when(k == nk - 1)
            def _():
                pass_on(0)

        @pl.when(at(2, nk // 2))
        def _():
            pass_on(2)

        tile_rows = pl.ds(pl.multiple_of(k * TM, TM), TM)

        @pl.when(jj == 0)
        def _():
            h = jnp.where(k == nt, mt[...], x_ref[...])
            ms = jnp.mean(h * h, axis=-1, keepdims=True)
            hn = (h * lax.rsqrt(ms + EPS)) * g_ref[...]
            hb = hn.astype(BF16)
            hbs[tile_rows, :] = hb
            proj_ref[...] = _dot(hb, wg[order_ref[0]]).astype(BF16)
            hnt_ref[...] = hn.T.astype(BF16)

        @pl.when(jj > 0)
        def _():
            proj_ref[...] = _dot(hbs[tile_rows, :], wg[order_ref[jj]]).astype(BF16)

        @pl.when(at(3, nk - 1))
        def _():
            for cp in wout_fw_in:
                cp.wait_recv()
            out_wog.start()
            for cp in sm_cp + win_cp + wout_cp + win_fw + wout_fw:
                cp.wait_send()
            for cp in (out_wg, out_wog, out_mt, out_cw):
                cp.wait()

    grid_spec = pltpu.PrefetchScalarGridSpec(
        num_scalar_prefetch=1,
        grid=(N_CHIPS, nk),
        in_specs=[pl.BlockSpec((TM, D_MODEL), lambda j, k, o: (jnp.where(j == 0, jnp.minimum(k, nt - 1), nt - 1), 0)),
                  pl.BlockSpec((1, D_MODEL), lambda j, k, o: (0, 0)),
                  pl.BlockSpec((N_META, 256), lambda j, k, o: (0, 0)),
                  pl.BlockSpec((3, 128), lambda j, k, o: (0, 0)),
                  any_spec, any_spec],
        out_specs=[pl.BlockSpec((TM, 1024), lambda j, k, o: (k, o[j])),
                   pl.BlockSpec((None, D_MODEL, TM), lambda j, k, o: (jnp.where(j == 0, k, nk - 1), 0, 0)),
                   any_spec, any_spec, any_spec, any_spec],
        scratch_shapes=[
            pltpu.VMEM((N_CHIPS, D_MODEL, 1024), BF16),
            pltpu.VMEM((N_CHIPS, 256, D_MODEL), BF16),
            pltpu.VMEM((N_CHIPS, SMALL_ROWS, 256), F32),
            pltpu.VMEM((TM, D_MODEL), F32),
            pltpu.VMEM((8, D_CONV), F32),
            pltpu.VMEM((rows, D_MODEL), BF16),
            pltpu.SemaphoreType.DMA((15,)), pltpu.SemaphoreType.DMA((15,)), pltpu.SemaphoreType.DMA((7,))])
    return pl.pallas_call(
        body, name="f1_norm_inproj_gather",
        grid_spec=grid_spec,
        out_shape=[jax.ShapeDtypeStruct((rows, N_PROJ_COLS), BF16),
                   jax.ShapeDtypeStruct((nk, D_MODEL, TM), BF16),
                   jax.ShapeDtypeStruct((N_CHIPS, D_MODEL, 1024), BF16),
                   jax.ShapeDtypeStruct((N_CHIPS, 256, D_MODEL), BF16),
                   jax.ShapeDtypeStruct((TM, D_MODEL), F32),
                   jax.ShapeDtypeStruct((8, D_CONV), F32)],
        compiler_params=_cparams(("arbitrary", "arbitrary")),
    )(order, x, g1, meta_sh, convw_sh, win_sh, wout_sh)


def _f2_f3_call(proj, conv_w8, gret, tb, x, w_out, fg, target):
    nt, rows = tb["nt"], tb["rows"]
    seq = nt * TM

    def pf(s):
        return jnp.where(s == 0, nt, jnp.minimum(s - 1, nt - 1))

    def xt(s):
        return jnp.clip(s - 2, 0, nt - 1)

    def body(proj_ref, cw_ref, g_ref, tt_ref, cr_ref, sr_ref, sgn_ref, dec_ref, xi_ref, zeta_ref, cd_ref,
             x_ref, w_ref, fg_ref, t_ref,
             conv_hbm, states_hbm, dh2_ref, dmx_ref, gwo_ref, gfg_ref, loss_ref,
             state, uhalo, mxs, convs, sts, lacc, out_sems):
        s = pl.program_id(0)
        slot = lax.rem(s, 2)
        mixed_ref = mxs.at[slot]
        conv_ref = convs.at[slot]
        states_ref = sts.at[slot]

        def conv_out(sl, tile):
            return pltpu.make_async_copy(convs.at[sl], conv_hbm.at[pl.ds(pl.multiple_of(tile * TM, TM), TM), :],
                                         out_sems.at[sl])

        def states_out(sl, tile):
            return pltpu.make_async_copy(sts.at[sl], states_hbm.at[pl.ds(pl.multiple_of(tile * NCH, NCH), NCH)],
                                         out_sems.at[2 + sl])

        @pl.when(s == 0)
        def _():
            state[...] = jnp.zeros_like(state)
            uhalo[...] = jnp.zeros_like(uhalo)
            mxs[...] = jnp.zeros_like(mxs)
            gwo_ref[...] = jnp.zeros_like(gwo_ref)
            gfg_ref[...] = jnp.zeros_like(gfg_ref)
            lacc[...] = jnp.zeros_like(lacc)

        @pl.when(s >= 2)
        def _():
            conv_out(slot, pf(s - 2)).wait()
            states_out(slot, pf(s - 2)).wait()

        valid = jnp.where(s >= 2, 1.0, 0.0)
        mx_prev = mxs.at[1 - slot]
        f3 = {}

        def f3_fwd():
            f3["h2"] = x_ref[...] + _dot(mx_prev[...], w_ref[...])

        def f3_loss():
            h2 = f3.pop("h2")
            ms = jnp.mean(h2 * h2, axis=-1, keepdims=True)
            rstd = lax.rsqrt(ms + EPS)
            yh = h2 * rstd
            g = fg_ref[...]
            e = (yh * g - t_ref[...]) * valid
            lacc[...] += jnp.sum(e * e, axis=0, keepdims=True)
            dy = e * (1.0 / D_MODEL)
            gfg_ref[...] += jnp.sum(dy * yh, axis=0, keepdims=True)
            dyh = dy * g
            dh2 = rstd * (dyh - yh * jnp.mean(dyh * yh, axis=-1, keepdims=True))
            dh2_ref[...] = dh2
            f3["db"] = dh2.astype(BF16)

        def f3_dmx():
            dmx_ref[...] = _dot_tb(f3["db"], w_ref[...]).astype(BF16)

        def f3_gw():
            gw = _dot_ta(mx_prev[...], f3["db"])
            for j in range(N_CHIPS):
                for hf in range(2):
                    r0 = j * 256 + hf * 128
                    gwo_ref[hf, j] += gw[r0:r0 + 128, :]

        cx = proj_ref[:, CX:CX + 512].astype(F32)
        cc = proj_ref[:, CC:CC + 512].astype(F32)
        u = cc * cx
        row = lax.broadcasted_iota(jnp.int32, (TM, D_CONV), 0)
        h7 = uhalo[7:8, :]
        h6 = uhalo[6:7, :]
        u1 = jnp.where(row == 0, h7, pltpu.roll(u, 1, 0))
        u2 = jnp.where(row == 0, h6, jnp.where(row == 1, h7, pltpu.roll(u, 2, 0)))
        conv = cw_ref[2:3, :] * u + cw_ref[1:2, :] * u1 + cw_ref[0:1, :] * u2
        uhalo[...] = u[TM - 8:TM, :]
        cb = proj_ref[:, CB:CB + 512].astype(F32)
        cg = proj_ref[:, CG:CG + 512].astype(F32)
        mixed_ref[:, 0:D_CONV] = (cb * conv * (cg * _sigmoid(cg))).astype(BF16)
        conv_ref[...] = conv.astype(BF16)
        f3_fwd()

        scale = HEAD_DIM ** -0.5
        H = range(RET_HEADS)
        st = [state[h] for h in H]
        between = [f3_loss, f3_dmx, f3_gw, None]
        rc_t, rs_t = _tile_rotary(tt_ref, cr_ref, sr_ref, sgn_ref)
        for c in range(NCH):
            r0 = c * CHUNK
            rc = rc_t[r0:r0 + CHUNK, :]
            rs = rs_t[r0:r0 + CHUNK, :]
            col = lambda base, h: slice(base + h * HEAD_DIM, base + (h + 1) * HEAD_DIM)
            rws = slice(r0, r0 + CHUNK)
            v = [proj_ref[rws, col(CV, h)] for h in H]
            qf = [_rot(proj_ref[rws, col(CQ, h)].astype(F32), rc, rs) * scale for h in H]
            kf = [_rot(proj_ref[rws, col(CK, h)].astype(F32), rc, rs) for h in H]
            stb = [t.astype(BF16) for t in st]
            for h in H:
                states_ref[c, h] = stb[h]
            a = [(_dot_tb(qf[h].astype(BF16), kf[h].astype(BF16)) * dec_ref[h]).astype(BF16) for h in H]
            o = [_dot(a[h], v[h]) + _dot((qf[h] * xi_ref[h]).astype(BF16), stb[h]) for h in H]
            st = [cd_ref[h, 0:1, :] * st[h] + _dot_ta((kf[h] * zeta_ref[h]).astype(BF16), v[h]) for h in H]
            for h in H:
                mu = jnp.mean(o[h], axis=-1, keepdims=True)
                d = o[h] - mu
                var = jnp.mean(d * d, axis=-1, keepdims=True)
                yh = d * lax.rsqrt(var + EPS)
                rg = proj_ref[rws, col(CR, h)].astype(F32)
                mixed_ref[rws, col(D_CONV, h)] = (yh * g_ref[:, col(0, h)] * (rg * _sigmoid(rg))).astype(BF16)
            if between[c] is not None:
                between[c]()
        for h in H:
            state[h] = st[h]

        @pl.when(s <= nt)
        def _():
            conv_out(slot, pf(s)).start()
            states_out(slot, pf(s)).start()

        @pl.when(s == nt + 1)
        def _():
            conv_out(1 - slot, pf(s - 1)).wait()
            states_out(1 - slot, pf(s - 1)).wait()
            tot = jnp.sum(lacc[...], axis=1, keepdims=True) * (0.5 / D_MODEL)
            loss_ref[...] = jnp.broadcast_to(tot, (1, 128))

    tile = lambda w: pl.BlockSpec((TM, w), lambda s: (pf(s), 0))
    xtile = lambda w: pl.BlockSpec((TM, w), lambda s: (xt(s), 0))
    any_spec = pl.BlockSpec(memory_space=pl.ANY)
    return pl.pallas_call(
        body, name="f2_mixer_fwd_f3_outproj_loss",
        grid=(nt + 2,),
        in_specs=[tile(N_PROJ_COLS), _resident((8, D_CONV)), _resident((1, D_RET)),
                  pl.BlockSpec((None, 8, HEAD_DIM), lambda s: (pf(s), 0, 0)),
                  _resident((TM, HEAD_DIM)), _resident((TM, HEAD_DIM)), _resident((8, HEAD_DIM)),
                  _resident((RET_HEADS, CHUNK, CHUNK)), _resident((RET_HEADS, CHUNK, HEAD_DIM)),
                  _resident((RET_HEADS, CHUNK, HEAD_DIM)), _resident((RET_HEADS, 8, HEAD_DIM)),
                  xtile(D_MODEL), _resident1((D_MODEL, D_MODEL)), _resident((1, D_MODEL)), xtile(D_MODEL)],
        out_specs=[any_spec, any_spec, xtile(D_MODEL), xtile(D_MODEL),
                   _resident((2, N_CHIPS, 128, D_MODEL)), _resident((1, D_MODEL)), _resident((1, 128))],
        out_shape=[jax.ShapeDtypeStruct((rows, D_CONV), BF16),
                   jax.ShapeDtypeStruct(((nt + 1) * NCH, RET_HEADS, HEAD_DIM, HEAD_DIM), BF16),
                   jax.ShapeDtypeStruct((seq, D_MODEL), F32),
                   jax.ShapeDtypeStruct((seq, D_MODEL), BF16),
                   jax.ShapeDtypeStruct((2, N_CHIPS, 128, D_MODEL), F32),
                   jax.ShapeDtypeStruct((1, D_MODEL), F32),
                   jax.ShapeDtypeStruct((1, 128), F32)],
        scratch_shapes=[pltpu.VMEM((RET_HEADS, HEAD_DIM, HEAD_DIM), F32), pltpu.VMEM((8, D_CONV), F32),
                        pltpu.VMEM((2, TM, D_MODEL), BF16), pltpu.VMEM((2, TM, D_CONV), BF16),
                        pltpu.VMEM((2, NCH, RET_HEADS, HEAD_DIM, HEAD_DIM), BF16),
                        pltpu.VMEM((1, D_MODEL), F32), pltpu.SemaphoreType.DMA((4,))],
        compiler_params=_cparams(("arbitrary",)),
    )(proj, conv_w8, gret, tb["tt"], tb["cr2"], tb["sr2"], tb["sgn"], tb["decay"], tb["xi"], tb["zeta"], tb["cd"],
      x, w_out, fg, target)


def _b2_b1a_call(proj, dmixed, conv_s, states, conv_w8, gret, tb, w_in_g, x, meta_tile, g1, dh2):
    nt, rows = tb["nt"], tb["rows"]
    seq = nt * TM

    def pb(r):
        return jnp.where(r == nt, nt, nt - 1 - r)

    def xprev(r):
        return jnp.clip(nt - r, 0, nt - 1)

    def body(proj_ref, dmx_ref, conv_ref, states_ref, cw_ref, g_ref, tt_ref, cr_ref, sr_ref, sgn_ref, dec_ref,
             xi_ref, zeta_ref, cd_ref, w_ref, x_ref, mt_ref, g1_ref, dh2_ref,
             dproj_hbm, gcw_ref, gg_ref, gx_ref, dmeta_ref, gn_ref,
             gstate, dchalo, dps, out_sems):
        r = pl.program_id(0)
        live = jnp.where(r == nt, 0.0, 1.0)
        slot = lax.rem(r, 2)
        dproj_ref = dps.at[slot]

        class to_hbm:
            def __init__(self, s, tile):
                self.copies = [pltpu.make_async_copy(dps.at[s, :, j * 1024:(j + 1) * 1024], dproj_hbm.at[tile, j],
                                                     out_sems.at[N_CHIPS * s + j]) for j in range(N_CHIPS)]

            def start(self):
                for cp in self.copies:
                    cp.start()

            def wait(self):
                for cp in self.copies:
                    cp.wait()

        @pl.when(r == 0)
        def _():
            gstate[...] = jnp.zeros_like(gstate)
            dchalo[...] = jnp.zeros_like(dchalo)
            gcw_ref[...] = jnp.zeros_like(gcw_ref)
            gg_ref[...] = jnp.zeros_like(gg_ref)
            gn_ref[...] = jnp.zeros_like(gn_ref)
            dps[...] = jnp.zeros_like(dps)

        @pl.when(r >= 2)
        def _():
            to_hbm(slot, pb(r - 2)).wait()

        dprev = dps.at[1 - slot]
        pieces = []

        def emit_piece():
            j = len(pieces)
            if j < N_CHIPS:
                p = _dot_tb(dprev[:, j * 1024:(j + 1) * 1024], w_ref[j])
                pieces.append(p if j == 0 else pieces[-1] + p)

        cx = proj_ref[:, CX:CX + 512].astype(F32)
        cb = proj_ref[:, CB:CB + 512].astype(F32)
        cc = proj_ref[:, CC:CC + 512].astype(F32)
        cg = proj_ref[:, CG:CG + 512].astype(F32)
        dco = dmx_ref[:, 0:D_CONV].astype(F32) * live
        conv = conv_ref[...].astype(F32)
        sg = _sigmoid(cg)
        sil = cg * sg
        t = dco * conv
        dproj_ref[:, CB:CB + 512] = (t * sil).astype(BF16)
        dproj_ref[:, CG:CG + 512] = (t * cb * (sg * (1.0 + cg * (1.0 - sg)))).astype(BF16)
        dconv = dco * cb * sil
        row = lax.broadcasted_iota(jnp.int32, (TM, D_CONV), 0)
        n0 = dchalo[0:1, :]
        n1 = dchalo[1:2, :]
        dc1 = jnp.where(row == TM - 1, n0, pltpu.roll(dconv, TM - 1, 0))
        dc2 = jnp.where(row == TM - 2, n0, jnp.where(row == TM - 1, n1, pltpu.roll(dconv, TM - 2, 0)))
        dchalo[...] = dconv[0:8, :]
        du = cw_ref[2:3, :] * dconv + cw_ref[1:2, :] * dc1 + cw_ref[0:1, :] * dc2
        u = cc * cx
        gcw_ref[2:3, :] += jnp.sum(u * dconv, axis=0, keepdims=True)
        gcw_ref[1:2, :] += jnp.sum(u * dc1, axis=0, keepdims=True)
        gcw_ref[0:1, :] += jnp.sum(u * dc2, axis=0, keepdims=True)
        dproj_ref[:, CC:CC + 512] = (du * cx).astype(BF16)
        dproj_ref[:, CX:CX + 512] = (du * cc).astype(BF16)
        emit_piece()
        emit_piece()

        scale = HEAD_DIM ** -0.5
        gs = {h: gstate[h] for h in range(RET_HEADS)}
        gg = {h: jnp.zeros((1, HEAD_DIM), F32) for h in range(RET_HEADS)}
        col = lambda base, h: slice(base + h * HEAD_DIM, base + (h + 1) * HEAD_DIM)
        rw = lambda c: slice(c * CHUNK, (c + 1) * CHUNK)
        rc_t, rs_t = _tile_rotary(tt_ref, cr_ref, sr_ref, sgn_ref)
        for c0 in range(NCH - CHUNK_GROUP, -1, -CHUNK_GROUP):
            cs = range(c0 + CHUNK_GROUP - 1, c0 - 1, -1)
            U = [(c, h) for c in cs for h in range(RET_HEADS)]
            rc = {c: rc_t[rw(c), :] for c in cs}
            rs = {c: rs_t[rw(c), :] for c in cs}
            v = {(c, h): proj_ref[rw(c), col(CV, h)] for c, h in U}
            stb = {(c, h): states_ref[c, h] for c, h in U}
            qf = {(c, h): _rot(proj_ref[rw(c), col(CQ, h)].astype(F32), rc[c], rs[c]) * scale for c, h in U}
            kf = {(c, h): _rot(proj_ref[rw(c), col(CK, h)].astype(F32), rc[c], rs[c]) for c, h in U}
            qb = {u: qf[u].astype(BF16) for u in U}
            kb = {u: kf[u].astype(BF16) for u in U}
            qxb = {(c, h): (qf[c, h] * xi_ref[h]).astype(BF16) for c, h in U}
            kzb = {(c, h): (kf[c, h] * zeta_ref[h]).astype(BF16) for c, h in U}
            ab = {(c, h): (_dot_tb(qb[c, h], kb[c, h]) * dec_ref[h]).astype(BF16) for c, h in U}
            o = {u: _dot(ab[u], v[u]) + _dot(qxb[u], stb[u]) for u in U}
            emit_piece()
            dob = {}
            for c, h in U:
                mu = jnp.mean(o[c, h], axis=-1, keepdims=True)
                d = o[c, h] - mu
                var = jnp.mean(d * d, axis=-1, keepdims=True)
                rstd = lax.rsqrt(var + EPS)
                yh = d * rstd
                g = g_ref[:, col(0, h)]
                rg = proj_ref[rw(c), col(CR, h)].astype(F32)
                dro = dmx_ref[rw(c), col(D_CONV, h)].astype(F32) * live
                sg = _sigmoid(rg)
                dproj_ref[rw(c), col(CR, h)] = (dro * (yh * g) * (sg * (1.0 + rg * (1.0 - sg)))).astype(BF16)
                dret = dro * (rg * sg)
                gg[h] = gg[h] + jnp.sum(dret * yh, axis=0, keepdims=True)
                dyh = dret * g
                do = rstd * (dyh - jnp.mean(dyh, axis=-1, keepdims=True)
                             - yh * jnp.mean(dyh * yh, axis=-1, keepdims=True))
                dob[c, h] = do.astype(BF16)
            dv1 = {u: _dot_ta(ab[u], dob[u]) for u in U}
            ds = {(c, h): (_dot_tb(dob[c, h], v[c, h]) * dec_ref[h]).astype(BF16) for c, h in U}
            gup = {u: _dot_ta(qxb[u], dob[u]) for u in U}
            dq = {(c, h): _dot(ds[c, h], kb[c, h]) + _dot_tb(dob[c, h], stb[c, h]) * xi_ref[h] for c, h in U}
            dk1 = {u: _dot_ta(ds[u], qb[u]) for u in U}
            emit_piece()
            for c, h in U:
                gsb = gs[h].astype(BF16)
                dv = dv1[c, h] + _dot(kzb[c, h], gsb)
                dk = dk1[c, h] + _dot_tb(v[c, h], gsb) * zeta_ref[h]
                gs[h] = cd_ref[h, 0:1, :] * gs[h] + gup[c, h]
                dproj_ref[rw(c), col(CQ, h)] = (_rot_t(dq[c, h], rc[c], rs[c]) * scale).astype(BF16)
                dproj_ref[rw(c), col(CK, h)] = _rot_t(dk, rc[c], rs[c]).astype(BF16)
                dproj_ref[rw(c), col(CV, h)] = dv.astype(BF16)
        for h in range(RET_HEADS):
            gstate[h] = gs[h]
            gg_ref[:, col(0, h)] += gg[h]

        while len(pieces) < N_CHIPS:
            emit_piece()

        def norm_bwd(dhn, hx):
            ms = jnp.mean(hx * hx, axis=-1, keepdims=True)
            rstd1 = lax.rsqrt(ms + EPS)
            xh = hx * rstd1
            gn_ref[...] += jnp.sum(dhn * xh, axis=0, keepdims=True)
            dxh = dhn * g1_ref[...]
            return rstd1 * (dxh - xh * jnp.mean(dxh * xh, axis=-1, keepdims=True))

        gx_ref[...] = norm_bwd(pieces[-1], x_ref[...]) + dh2_ref[...]

        @pl.when(r < nt)
        def _():
            to_hbm(slot, pb(r)).start()

        @pl.when(r == nt)
        def _():
            to_hbm(slot, pb(r)).start()
            mrows = slice(TM - N_META, TM)
            d16 = dproj_ref[mrows, :]
            dhn16 = _dot_tb(d16[:, 0:1024], w_ref[0])
            for j in range(1, N_CHIPS):
                dhn16 += _dot_tb(d16[:, j * 1024:(j + 1) * 1024], w_ref[j])
            dmeta_ref[...] = norm_bwd(dhn16, mt_ref[mrows, :])
            to_hbm(1 - slot, pb(r - 1)).wait()
            to_hbm(slot, pb(r)).wait()

    tile = lambda w: pl.BlockSpec((TM, w), lambda r: (pb(r), 0))
    xtile = pl.BlockSpec((TM, D_MODEL), lambda r: (xprev(r), 0))
    return pl.pallas_call(
        body, name="b2_mixer_bwd_b1a_inproj_bwd_x",
        grid=(nt + 1,),
        in_specs=[tile(N_PROJ_COLS),
                  pl.BlockSpec((TM, D_MODEL), lambda r: (jnp.minimum(pb(r), nt - 1), 0)),
                  tile(D_CONV),
                  pl.BlockSpec((NCH, RET_HEADS, HEAD_DIM, HEAD_DIM), lambda r: (pb(r), 0, 0, 0)),
                  _resident((8, D_CONV)), _resident((1, D_RET)),
                  pl.BlockSpec((None, 8, HEAD_DIM), lambda r: (pb(r), 0, 0)),
                  _resident((TM, HEAD_DIM)), _resident((TM, HEAD_DIM)), _resident((8, HEAD_DIM)),
                  _resident((RET_HEADS, CHUNK, CHUNK)),
                  _resident((RET_HEADS, CHUNK, HEAD_DIM)),
                  _resident((RET_HEADS, CHUNK, HEAD_DIM)), _resident((RET_HEADS, 8, HEAD_DIM)),
                  _resident1((N_CHIPS, D_MODEL, 1024)), xtile, _resident1((TM, D_MODEL)), _resident((1, D_MODEL)),
                  xtile],
        out_specs=[pl.BlockSpec(memory_space=pl.ANY), _resident((8, D_CONV)), _resident((1, D_RET)),
                   xtile, _resident((N_META, D_MODEL)), _resident((1, D_MODEL))],
        out_shape=[jax.ShapeDtypeStruct((nt + 1, N_CHIPS, TM, 1024), BF16),
                   jax.ShapeDtypeStruct((8, D_CONV), F32),
                   jax.ShapeDtypeStruct((1, D_RET), F32),
                   jax.ShapeDtypeStruct((seq, D_MODEL), F32),
                   jax.ShapeDtypeStruct((N_META, D_MODEL), F32),
                   jax.ShapeDtypeStruct((1, D_MODEL), F32)],
        scratch_shapes=[pltpu.VMEM((RET_HEADS, HEAD_DIM, HEAD_DIM), F32), pltpu.VMEM((8, D_CONV), F32),
                        pltpu.VMEM((2, TM, N_PROJ_COLS), BF16), pltpu.SemaphoreType.DMA((2 * N_CHIPS,))],
        compiler_params=_cparams(("arbitrary",), vmem=VMEM_LIMIT_MAX),
    )(proj, dmixed, conv_s, states, conv_w8, gret, tb["tt"], tb["cr2"], tb["sr2"], tb["sgn"], tb["decay"],
      tb["xi"], tb["zeta"], tb["cd"], w_in_g, x, meta_tile, g1, dh2)


REL = (2, 1, 3)
SMALL_ROWS = 24
HALF_STEP = 4


def _rcopy(src, dst, send_sems, recv_sems, k, to):
    return pltpu.make_async_remote_copy(src_ref=src, dst_ref=dst, send_sem=send_sems.at[k],
                                        recv_sem=recv_sems.at[k], device_id=to, device_id_type=MESH_ID)


def _b1b_reduce_call(order, hnt, dproj, gwo, small, nt):
    nk = nt + 1
    last = nk - 1
    any_spec = pl.BlockSpec(memory_space=pl.ANY)

    def body(order_ref, a_ref, b_ref, gm_ref, gn1_ref, gfg_ref, ggr_ref, gcw_ref, ls_ref, gwo_hbm,
             gwin_hbm, gwout_hbm, tot_hbm,
             acc, sb, abuf, pb, bbuf, fin, go, ao, pbo, bo, fino, slots, totv, send_sems, recv_sems, loc_sems):
        jj, k = pl.program_id(0), pl.program_id(1)
        x, y, c = lax.axis_index("x"), lax.axis_index("y"), lax.axis_index("c")
        me, myid, sib = 2 * x + y, 4 * x + 2 * y + c, (x, y, 1 - c)
        rc = functools.partial(_rcopy, send_sems=send_sems, recv_sems=recv_sems)
        peers = [((1 - x) if r & 2 else x, (1 - y) if r & 1 else y, c) for r in REL]
        kids = [jnp.bitwise_xor(me, r) for r in REL]

        def dev_peer(r):
            return ((1 - x) if r & 4 else x, (1 - y) if r & 2 else y, (1 - c) if r & 1 else c)

        own_go = pltpu.make_async_copy(gwo_hbm.at[c], go, loc_sems.at[0])
        wo_half = rc(gwo_hbm.at[1 - c], ao, k=8, to=sib)
        wo_part = [rc(pbo.at[kids[p]], bo.at[p], k=9 + p, to=peers[p]) for p in range(3)]
        sm = [rc(slots.at[0], slots.at[r], k=12 + r, to=dev_peer(r)) for r in range(1, N_DEV)]
        half = [rc(sb.at[j % 2, 1 - c], abuf.at[j], k=j, to=sib) for j in range(N_CHIPS)]
        part = [rc(pb.at[p], bbuf.at[p], k=4 + p, to=peers[p]) for p in range(3)]

        @pl.when(jnp.logical_and(jj == 0, k == 0))
        def _():
            own_go.start()
            slots[0, 0:N_META, :] = gm_ref[...]
            slots[0, N_META:SMALL_ROWS, :] = jnp.zeros((SMALL_ROWS - N_META, D_MODEL), F32)
            slots[0, N_META:N_META + 1, :] = gn1_ref[...]
            slots[0, N_META + 1:N_META + 2, :] = gfg_ref[...]
            slots[0, N_META + 2:N_META + 3, 0:D_RET] = ggr_ref[...]
            slots[0, N_META + 3:N_META + 6, 0:D_CONV] = gcw_ref[0:3, :]
            slots[0, N_META + 6:N_META + 7, 0:128] = ls_ref[...]
            wo_half.start()
            for cp in sm:
                cp.start()

        @pl.when(k == 0)
        def _():
            acc[...] = jnp.zeros_like(acc)

        acc[0] += _dot(a_ref[0:512, :], b_ref[...])
        acc[1] += _dot(a_ref[512:1024, :], b_ref[...])

        @pl.when(k == HALF_STEP)
        def _():
            @pl.when(jj == 0)
            def _():
                own_go.wait()
                wo_half.wait_recv()
                for j in range(N_CHIPS):
                    go[j] = go[j] + ao[j]
                pbo[...] = go[...].astype(BF16)
                for cp in wo_part:
                    cp.start()

            for p in range(3):
                @pl.when(jj == p + 1)
                def _(p=p):
                    half[p].wait_recv()
                    half[p].wait_send()
                    pb[p] = (sb[p % 2, c] + abuf[p]).astype(BF16)
                    part[p].start()

        @pl.when(k == last)
        def _():
            for j in range(N_CHIPS):
                @pl.when(jj == j)
                def _(j=j):
                    sb[j % 2] = acc[...]
                    half[j].start()

        @pl.when(jnp.logical_and(jj == N_CHIPS - 1, k == last))
        def _():
            half[3].wait_recv()
            own = sb[1, c] + abuf[3]
            for cp in part:
                cp.wait_recv()
            fin[c] = ((own + bbuf[0].astype(F32)) + bbuf[1].astype(F32)) + bbuf[2].astype(F32)
            done = rc(fin.at[c], fin.at[c], k=7, to=sib)
            done.start()
            for cp in wo_part:
                cp.wait_recv()
            fino[c] = ((go[me] + bo[0].astype(F32)) + bo[1].astype(F32)) + bo[2].astype(F32)
            done_o = rc(fino.at[c], fino.at[c], k=12, to=sib)
            done_o.start()
            for cp in sm:
                cp.wait_recv()
            tot = slots[myid]
            for a in range(1, N_DEV):
                tot = tot + slots[jnp.bitwise_xor(myid, a)]
            totv[...] = tot
            out_t = pltpu.make_async_copy(totv, tot_hbm, loc_sems.at[1])
            out_t.start()
            rc(fin.at[1 - c], fin.at[1 - c], k=7, to=sib).wait_recv()
            out_w = pltpu.make_async_copy(fin, gwin_hbm, loc_sems.at[0])
            out_w.start()
            rc(fino.at[1 - c], fino.at[1 - c], k=12, to=sib).wait_recv()
            out_o = pltpu.make_async_copy(fino, gwout_hbm, loc_sems.at[2])
            out_o.start()
            for cp in [half[3]] + part + [done, wo_half] + wo_part + [done_o] + sm:
                cp.wait_send()
            out_t.wait()
            out_w.wait()
            out_o.wait()

    grid_spec = pltpu.PrefetchScalarGridSpec(
        num_scalar_prefetch=1,
        grid=(N_CHIPS, nk),
        in_specs=[pl.BlockSpec((None, D_MODEL, TM), lambda j, k, o: (k, 0, 0)),
                  pl.BlockSpec((None, None, TM, 1024), lambda j, k, o: (k, o[j], 0, 0))]
                 + [pl.BlockSpec(s.shape, lambda j, k, o: (0, 0)) for s in small] + [any_spec],
        out_specs=[any_spec, any_spec, any_spec],
        scratch_shapes=[
            pltpu.VMEM((2, 512, 1024), F32),
            pltpu.VMEM((2, 2, 512, 1024), F32),
            pltpu.VMEM((N_CHIPS, 512, 1024), F32),
            pltpu.VMEM((3, 512, 1024), BF16),
            pltpu.VMEM((3, 512, 1024), BF16),
            pltpu.VMEM((2, 512, 1024), F32),
            pltpu.VMEM((N_CHIPS, 128, D_MODEL), F32),
            pltpu.VMEM((N_CHIPS, 128, D_MODEL), F32),
            pltpu.VMEM((N_CHIPS, 128, D_MODEL), BF16),
            pltpu.VMEM((3, 128, D_MODEL), BF16),
            pltpu.VMEM((2, 128, D_MODEL), F32),
            pltpu.VMEM((N_DEV, SMALL_ROWS, D_MODEL), F32),
            pltpu.VMEM((SMALL_ROWS, D_MODEL), F32),
            pltpu.SemaphoreType.DMA((20,)), pltpu.SemaphoreType.DMA((20,)), pltpu.SemaphoreType.DMA((3,))])
    return pl.pallas_call(
        body, name="b1b_inproj_bwd_w_reduce",
        grid_spec=grid_spec,
        out_shape=[jax.ShapeDtypeStruct((2, 512, 1024), F32),
                   jax.ShapeDtypeStruct((2, 128, D_MODEL), F32),
                   jax.ShapeDtypeStruct((SMALL_ROWS, D_MODEL), F32)],
        compiler_params=_cparams(("arbitrary", "arbitrary")),
    )(order, hnt, dproj, *small, gwo)


def _local_step(me, x, target, g1, gret, fg, win_sh, wout_sh, meta_sh, convw_sh):
    seq = x.shape[0]
    tb = _tables(seq)
    nt = tb["nt"]
    g1r, gretr, fgr = g1.reshape(1, -1), gret.reshape(1, -1), fg.reshape(1, -1)
    order = jnp.stack([me, me ^ REL[0], me ^ REL[1], me ^ REL[2]]).astype(jnp.int32)

    proj, hnt, w_in_g, w_out_g, meta_tile, conv_w8 = _f1_gather_call(order, x, g1r, win_sh, wout_sh, meta_sh,
                                                                     convw_sh, nt)
    w_out = w_out_g.reshape(D_MODEL, D_MODEL)
    conv_s, states, dh2, dmixed, g_wout, g_fg, loss = _f2_f3_call(proj, conv_w8, gretr, tb, x, w_out, fgr, target)
    dproj, g_cw8, g_gret, grad_x, g_meta, g_g1 = _b2_b1a_call(proj, dmixed, conv_s, states, conv_w8, gretr, tb,
                                                              w_in_g, x, meta_tile, g1r, dh2)
    return grad_x, g_wout, (g_meta, g_g1, g_fg, g_gret, g_cw8, loss), hnt, dproj


def _adamw_update(w_ref, g_ref, m_ref, v_ref, d_ref, nm_ref, nv_ref):
    gg = g_ref[...]
    nm = ADAM_B1 * m_ref[...] + (1.0 - ADAM_B1) * gg
    nv = ADAM_B2 * v_ref[...] + (1.0 - ADAM_B2) * (gg * gg)
    m_hat = nm / (1.0 - ADAM_B1 ** ADAM_STEP)
    v_hat = nv / (1.0 - ADAM_B2 ** ADAM_STEP)
    d_ref[...] = -ADAM_LR * (m_hat / (jnp.sqrt(v_hat) + ADAM_EPS) + ADAM_WD * w_ref[...])
    nm_ref[...] = nm
    nv_ref[...] = nv


def _adamw_small_call(me, tot, ws, ms, vs):
    n = len(ws)

    def body(me_ref, tmeta_ref, tvec_ref, tconv_ref, *refs):
        ins, outs = refs[:3 * n], refs[3 * n:]
        g_refs, loss_ref, upd = outs[0:n], outs[n], outs[n + 1:]
        g_refs[0][...] = tmeta_ref[...]
        g_refs[1][...] = tvec_ref[0:1, :]
        g_refs[2][...] = tconv_ref[3:6, :]
        g_refs[3][...] = tvec_ref[2:3, 0:D_RET]
        g_refs[4][...] = tvec_ref[1:2, :]
        loss_ref[...] = tvec_ref[6:7, 0:1]
        for i in range(n):
            _adamw_update(ins[i], g_refs[i], ins[n + i], ins[2 * n + i], upd[i], upd[n + i], upd[2 * n + i])

    whole = lambda a: pl.BlockSpec(a.shape, lambda i, m: (0,) * a.ndim)
    shapes = [jax.ShapeDtypeStruct(w.shape, F32) for w in ws]
    out_shape = shapes + [jax.ShapeDtypeStruct((1, 1), F32)] + shapes * 3
    grid_spec = pltpu.PrefetchScalarGridSpec(
        num_scalar_prefetch=1, grid=(1,),
        in_specs=[pl.BlockSpec((N_META, 256), lambda i, m: (0, m[0])),
                  pl.BlockSpec((8, D_MODEL), lambda i, m: (N_META // 8, 0)),
                  pl.BlockSpec((8, 128), lambda i, m: (N_META // 8, m[0]))] + [whole(a) for a in ws + ms + vs],
        out_specs=[whole(s) for s in out_shape])
    outs = pl.pallas_call(body, name="adamw_small", grid_spec=grid_spec, out_shape=out_shape,
                          compiler_params=_cparams(("arbitrary",)))(me.reshape(1), tot, tot, tot, *ws, *ms, *vs)
    return outs[:n], outs[n], outs[n + 1:2 * n + 1], outs[2 * n + 1:3 * n + 1], outs[3 * n + 1:]


def _adamw_call(w, g, m, v, name):
    shape = w.shape
    w2, g2, m2, v2 = (a.reshape(-1, shape[-1]) for a in (w, g, m, v))
    rows, cols = w2.shape
    br = 256 if rows % 256 == 0 else rows
    body = functools.partial(_adamw_update)
    spec = pl.BlockSpec((br, cols), lambda i: (i, 0))
    outs = pl.pallas_call(
        body, name=name, grid=(rows // br,),
        in_specs=[spec] * 4, out_specs=[spec] * 3,
        out_shape=[jax.ShapeDtypeStruct((rows, cols), F32)] * 3,
        compiler_params=_cparams(("arbitrary",)),
    )(w2, g2, m2, v2)
    return tuple(o.reshape(shape) for o in outs)


def kernel(x, meta, norm1_g, w_in, conv_w, ret_norm_g, w_out, final_g, loss_target, m_meta, m_norm1_g, m_w_in, m_conv_w, m_ret_norm_g, m_w_out, m_final_g, v_meta, v_norm1_g, v_w_in, v_conv_w, v_ret_norm_g, v_w_out, v_final_g):
    me = 2 * lax.axis_index("x") + lax.axis_index("y")

    grad_x, g_wo, small, hnt, dproj = _local_step(me, x[0], loss_target[0], norm1_g, ret_norm_g, final_g,
                                                  w_in, w_out, meta, conv_w)

    order = jnp.stack([me ^ REL[0], me ^ REL[1], me ^ REL[2], me]).astype(jnp.int32)
    g_win, g_wout, tot = _b1b_reduce_call(order, hnt, dproj, g_wo, small, x.shape[1] // TM)
    g_win, g_wout = g_win.reshape(D_MODEL, 1024), g_wout.reshape(256, D_MODEL)

    ws = [meta, norm1_g, w_in, conv_w, ret_norm_g, w_out, final_g]
    ms = [m_meta, m_norm1_g, m_w_in, m_conv_w, m_ret_norm_g, m_w_out, m_final_g]
    vs = [v_meta, v_norm1_g, v_w_in, v_conv_w, v_ret_norm_g, v_w_out, v_final_g]
    names = ["meta", "norm1_g", "w_in", "conv_w", "ret_norm_g", "w_out", "final_g"]
    as2d = lambda a: a.reshape(1, -1) if a.ndim == 1 else a
    big = {names.index("w_in"): g_win, names.index("w_out"): g_wout}
    small = [i for i in range(len(names)) if i not in big]
    grads, deltas, new_ms, new_vs = [None] * 7, [None] * 7, [None] * 7, [None] * 7
    for i, g_ in big.items():
        grads[i] = g_
        deltas[i], new_ms[i], new_vs[i] = _adamw_call(ws[i], g_, ms[i], vs[i], "adamw_" + names[i])
    sg, loss_tot, sd, sm_, sv = _adamw_small_call(me.astype(jnp.int32), tot,
                                                  *[[as2d(t[i]) for i in small] for t in (ws, ms, vs)])
    for j, i in enumerate(small):
        grads[i], deltas[i], new_ms[i], new_vs[i] = (o[j].reshape(ws[i].shape) for o in (sg, sd, sm_, sv))
    return (loss_tot.reshape(()), grad_x[None], *grads, *deltas, *new_ms, *new_vs)
```

```python
import functools

import jax
import jax.numpy as jnp
import numpy as np
from jax import lax
from jax.experimental import pallas as pl
from jax.experimental.pallas import tpu as pltpu

F32 = jnp.float32
BF16 = jnp.bfloat16

D_MODEL = 1024
N_META = 16
D_CONV = 512
D_RET = 512
RET_HEADS = 4
HEAD_DIM = 128
CHUNK = 128
N_PROJ_COLS = 4096
ROPE_BASE = 10000.0
EPS = 1e-6
N_CHIPS = 4
N_DEV = 8

ADAM_LR = 0.001
ADAM_B1 = 0.9
ADAM_B2 = 0.999
ADAM_EPS = 1e-08
ADAM_WD = 0.01
ADAM_STEP = 10

TM = 512
NCH = TM // CHUNK
CHUNK_GROUP = 2
VMEM_LIMIT = 56 * 1024 * 1024
VMEM_LIMIT_MAX = 63 * 1024 * 1024

CX, CB, CC, CG, CQ, CK, CV, CR = (i * 512 for i in range(8))

MESH_ID = pl.DeviceIdType.MESH


def _cparams(sem=None, vmem=VMEM_LIMIT, **kw):
    return pltpu.CompilerParams(dimension_semantics=sem, vmem_limit_bytes=vmem, **kw)


def _sigmoid(x):
    return 1.0 / (1.0 + jnp.exp(-x))


def _dot(a, b):
    return jnp.dot(a, b, preferred_element_type=F32)


def _dot_tb(a, b):
    return lax.dot_general(a, b, (((1,), (1,)), ((), ())), preferred_element_type=F32)


def _dot_ta(a, b):
    return lax.dot_general(a, b, (((0,), (0,)), ((), ())), preferred_element_type=F32)


def _resident(shape):
    nd = len(shape)
    return pl.BlockSpec(shape, lambda *_: (0,) * nd)


def _resident1(shape):
    nd = len(shape)
    return pl.BlockSpec(shape, lambda *_: (0,) * nd, pipeline_mode=pl.Buffered(1))


def _tables(seq):
    f32 = np.float32
    nt = seq // TM
    rows = seq + TM
    half = HEAD_DIM // 2
    freqs = (f32(1.0) / (f32(ROPE_BASE) ** (np.arange(half, dtype=f32) / f32(half)))).astype(f32)
    tile_start = np.concatenate([np.arange(nt, dtype=f32), -np.ones((1,), f32)]) * f32(TM)
    ang_t = tile_start[:, None] * freqs[None, :]
    ang_r = (np.arange(TM, dtype=f32) + f32(N_META))[:, None] * freqs[None, :]
    dup = lambda a: np.concatenate([a, a], axis=-1).astype(f32)
    tt = np.stack([dup(np.cos(ang_t)), dup(np.sin(ang_t))], axis=1)
    tt = np.pad(tt, ((0, 0), (0, 6), (0, 0)))
    cr2, sr2 = dup(np.cos(ang_r)), dup(np.sin(ang_r))
    sgn = np.concatenate([-np.ones((8, half), f32), np.ones((8, half), f32)], axis=-1)
    log_g = np.log(f32(1.0) - f32(2.0) ** (f32(-5.0) - np.arange(RET_HEADS, dtype=f32))).astype(f32)
    idx = np.arange(CHUNK, dtype=f32)
    diff = idx[:, None] - idx[None, :]
    decay = np.where(diff[None] >= 0, np.exp(diff[None] * log_g[:, None, None]), f32(0.0)).astype(f32)
    zeta = np.exp((f32(CHUNK - 1) - idx)[None, :] * log_g[:, None]).astype(f32)
    xi = np.exp((idx + f32(1.0))[None, :] * log_g[:, None]).astype(f32)
    cd = np.exp(f32(CHUNK) * log_g).astype(f32)
    zeta_b = np.broadcast_to(zeta[:, :, None], (RET_HEADS, CHUNK, HEAD_DIM))
    xi_b = np.broadcast_to(xi[:, :, None], (RET_HEADS, CHUNK, HEAD_DIM))
    cd_b = np.broadcast_to(cd[:, None, None], (RET_HEADS, 8, HEAD_DIM))
    tables = dict(tt=tt, cr2=cr2, sr2=sr2, sgn=sgn, decay=decay, zeta=zeta_b, xi=xi_b, cd=cd_b)
    return dict(nt=nt, rows=rows, **{k: jnp.asarray(np.ascontiguousarray(v, dtype=f32)) for k, v in tables.items()})


def _tile_rotary(tt_ref, cr_ref, sr_ref, sgn_ref):
    ct, st = tt_ref[0:1, :], tt_ref[1:2, :]
    cr, sr = cr_ref[...], sr_ref[...]
    return ct * cr - st * sr, (st * cr + ct * sr) * sgn_ref[0:1, :]


def _rot(t, rc, rs):
    return t * rc + pltpu.roll(t, HEAD_DIM // 2, 1) * rs


def _rot_t(dt, rc, rs):
    return dt * rc + pltpu.roll(dt * rs, HEAD_DIM // 2, 1)


def _f1_gather_call(order, x, g1, win_sh, wout_sh, meta_sh, convw_sh, nt):
    nk = nt + 1
    rows = nk * TM
    any_spec = pl.BlockSpec(memory_space=pl.ANY)

    def body(order_ref, x_ref, g_ref, meta_ref, convw_ref, win_hbm, wout_hbm,
             proj_ref, hnt_ref, wg_hbm, wog_hbm, mt_hbm, cw_hbm,
             wg, wog, smg, mt, cw, hbs, send_sems, recv_sems, loc_sems):
        jj, k = pl.program_id(0), pl.program_id(1)
        x, y, c = lax.axis_index("x"), lax.axis_index("y"), lax.axis_index("c")
        me, sib = 2 * x + y, (x, y, 1 - c)
        rc = functools.partial(_rcopy, send_sems=send_sems, recv_sems=recv_sems)
        peers = [((1 - x) if r & 2 else x, (1 - y) if r & 1 else y, c) for r in REL]
        kids = [jnp.bitwise_xor(me, r) for r in REL]
        hw, ho = pl.ds(pl.multiple_of(c * 512, 512), 512), pl.ds(pl.multiple_of(c * 128, 128), 128)
        hw2 = pl.ds(pl.multiple_of((1 - c) * 512, 512), 512)
        ho2 = pl.ds(pl.multiple_of((1 - c) * 128, 128), 128)
        at = lambda j_, k_: jnp.logical_and(jj == j_, k == k_)

        sm_cp = [rc(smg.at[me], smg.at[me], k=p, to=peers[p]) for p in range(3)]
        win_cp = [rc(wg.at[me, hw], wg.at[me, hw], k=3 + p, to=peers[p]) for p in range(3)]
        wout_cp = [rc(wog.at[me, ho], wog.at[me, ho], k=6 + p, to=peers[p]) for p in range(3)]
        sm_in = [rc(smg.at[me], smg.at[kids[p]], k=p, to=sib) for p in range(3)]
        win_in = [rc(wg.at[me, hw], wg.at[kids[p], hw], k=3 + p, to=sib) for p in range(3)]
        wout_in = [rc(wog.at[me, ho], wog.at[kids[p], ho], k=6 + p, to=sib) for p in range(3)]
        win_fw = [rc(wg.at[kids[p], hw], wg.at[kids[p], hw], k=9 + p, to=sib) for p in range(3)]
        wout_fw = [rc(wog.at[kids[p], ho], wog.at[kids[p], ho], k=12 + p, to=sib) for p in range(3)]
        win_fw_in = [rc(wg.at[kids[p], hw2], wg.at[kids[p], hw2], k=9 + p, to=sib) for p in range(3)]
        wout_fw_in = [rc(wog.at[kids[p], ho2], wog.at[kids[p], ho2], k=12 + p, to=sib) for p in range(3)]

        def stage_own():
            for rows_, sends in ((hw, win_cp), (hw2, [])):
                cp = pltpu.make_async_copy(win_hbm.at[rows_], mt, loc_sems.at[0])
                cp.start()
                cp.wait()
                wg[me, rows_, :] = mt[...].astype(BF16)
                for s_ in sends:
                    s_.start()
            cp = pltpu.make_async_copy(wout_hbm, mt.at[0:256], loc_sems.at[1])
            cp.start()
            cp.wait()
            wog[me] = mt[0:256, :].astype(BF16)
            for s_ in wout_cp:
                s_.start()
        out_wg = pltpu.make_async_copy(wg, wg_hbm, loc_sems.at[3])
        out_wog = pltpu.make_async_copy(wog, wog_hbm, loc_sems.at[4])
        out_mt = pltpu.make_async_copy(mt, mt_hbm, loc_sems.at[5])
        out_cw = pltpu.make_async_copy(cw, cw_hbm, loc_sems.at[6])

        def pass_on(p):
            win_in[p].wait_recv()
            win_fw[p].start()

        @pl.when(k <= 1)
        def _():
            @pl.when(at(0, 0))
            def _():
                smg[me] = jnp.zeros((SMALL_ROWS, 256), F32)
                smg[me, 0:N_META, :] = meta_ref[...]
                smg[me, N_META:N_META + 3, 0:128] = convw_ref[...]
                for cp in sm_cp:
                    cp.start()
                stage_own()

            for p in range(3):
                @pl.when(at(p + 1, 0))
                def _(p=p):
                    win_fw_in[p].wait_recv()

            @pl.when(at(1, 1))
            def _():
                pass_on(1)

            @pl.when(at(3, 0))
            def _():
                out_wg.start()

            @pl.when(at(3, 1))
            def _():
                for p in range(3):
                    wout_in[p].wait_recv()
                    wout_fw[p].start()

        @pl.when(jnp.logical_and(jj == 0, k >= nk - 2))
        def _():
            @pl.when(k == nk - 2)
            def _():
                for cp in sm_in:
                    cp.wait_recv()
                mt[...] = jnp.zeros_like(mt)
                cw[...] = jnp.zeros_like(cw)
                for j in range(N_CHIPS):
                    mt[TM - N_META:TM, j * 256:(j + 1) * 256] = smg[j, 0:N_META, :]
                    cw[0:3, j * 128:(j + 1) * 128] = smg[j, N_META:N_META + 3, 0:128]
                out_mt.start()
                out_cw.start()

            @pl.when(k == nk - 1)
            def _():
                pass_on(0)

        @pl.when(at(2, nk // 2))
        def _():
            pass_on(2)

        tile_rows = pl.ds(pl.multiple_of(k * TM, TM), TM)

        @pl.when(jj == 0)
        def _():
            h = jnp.where(k == nt, mt[...], x_ref[...])
            ms = jnp.mean(h * h, axis=-1, keepdims=True)
            hn = (h * lax.rsqrt(ms + EPS)) * g_ref[...]
            hb = hn.astype(BF16)
            hbs[tile_rows, :] = hb
            proj_ref[...] = _dot(hb, wg[order_ref[0]]).astype(BF16)
            hnt_ref[...] = hn.T.astype(BF16)

        @pl.when(jj > 0)
        def _():
            proj_ref[...] = _dot(hbs[tile_rows, :], wg[order_ref[jj]]).astype(BF16)

        @pl.when(at(3, nk - 1))
        def _():
            for cp in wout_fw_in:
                cp.wait_recv()
            out_wog.start()
            for cp in sm_cp + win_cp + wout_cp + win_fw + wout_fw:
                cp.wait_send()
            for cp in (out_wg, out_wog, out_mt, out_cw):
                cp.wait()

    grid_spec = pltpu.PrefetchScalarGridSpec(
        num_scalar_prefetch=1,
        grid=(N_CHIPS, nk),
        in_specs=[pl.BlockSpec((TM, D_MODEL), lambda j, k, o: (jnp.where(j == 0, jnp.minimum(k, nt - 1), nt - 1), 0)),
                  pl.BlockSpec((1, D_MODEL), lambda j, k, o: (0, 0)),
                  pl.BlockSpec((N_META, 256), lambda j, k, o: (0, 0)),
                  pl.BlockSpec((3, 128), lambda j, k, o: (0, 0)),
                  any_spec, any_spec],
        out_specs=[pl.BlockSpec((TM, 1024), lambda j, k, o: (k, o[j])),
                   pl.BlockSpec((None, D_MODEL, TM), lambda j, k, o: (jnp.where(j == 0, k, nk - 1), 0, 0)),
                   any_spec, any_spec, any_spec, any_spec],
        scratch_shapes=[
            pltpu.VMEM((N_CHIPS, D_MODEL, 1024), BF16),
            pltpu.VMEM((N_CHIPS, 256, D_MODEL), BF16),
            pltpu.VMEM((N_CHIPS, SMALL_ROWS, 256), F32),
            pltpu.VMEM((TM, D_MODEL), F32),
            pltpu.VMEM((8, D_CONV), F32),
            pltpu.VMEM((rows, D_MODEL), BF16),
            pltpu.SemaphoreType.DMA((15,)), pltpu.SemaphoreType.DMA((15,)), pltpu.SemaphoreType.DMA((7,))])
    return pl.pallas_call(
        body, name="f1_norm_inproj_gather",
        grid_spec=grid_spec,
        out_shape=[jax.ShapeDtypeStruct((rows, N_PROJ_COLS), BF16),
                   jax.ShapeDtypeStruct((nk, D_MODEL, TM), BF16),
                   jax.ShapeDtypeStruct((N_CHIPS, D_MODEL, 1024), BF16),
                   jax.ShapeDtypeStruct((N_CHIPS, 256, D_MODEL), BF16),
                   jax.ShapeDtypeStruct((TM, D_MODEL), F32),
                   jax.ShapeDtypeStruct((8, D_CONV), F32)],
        compiler_params=_cparams(("arbitrary", "arbitrary")),
    )(order, x, g1, meta_sh, convw_sh, win_sh, wout_sh)


def _f2_f3_call(proj, conv_w8, gret, tb, x, w_out, fg, target):
    nt, rows = tb["nt"], tb["rows"]
    seq = nt * TM

    def pf(s):
        return jnp.where(s == 0, nt, jnp.minimum(s - 1, nt - 1))

    def xt(s):
        return jnp.clip(s - 2, 0, nt - 1)

    def body(proj_ref, cw_ref, g_ref, tt_ref, cr_ref, sr_ref, sgn_ref, dec_ref, xi_ref, zeta_ref, cd_ref,
             x_ref, w_ref, fg_ref, t_ref,
             conv_hbm, states_hbm, dh2_ref, dmx_ref, gwo_ref, gfg_ref, loss_ref,
             state, uhalo, mxs, convs, sts, lacc, out_sems):
        s = pl.program_id(0)
        slot = lax.rem(s, 2)
        mixed_ref = mxs.at[slot]
        conv_ref = convs.at[slot]
        states_ref = sts.at[slot]

        def conv_out(sl, tile):
            return pltpu.make_async_copy(convs.at[sl], conv_hbm.at[pl.ds(pl.multiple_of(tile * TM, TM), TM), :],
                                         out_sems.at[sl])

        def states_out(sl, tile):
            return pltpu.make_async_copy(sts.at[sl], states_hbm.at[pl.ds(pl.multiple_of(tile * NCH, NCH), NCH)],
                                         out_sems.at[2 + sl])

        @pl.when(s == 0)
        def _():
            state[...] = jnp.zeros_like(state)
            uhalo[...] = jnp.zeros_like(uhalo)
            mxs[...] = jnp.zeros_like(mxs)
            gwo_ref[...] = jnp.zeros_like(gwo_ref)
            gfg_ref[...] = jnp.zeros_like(gfg_ref)
            lacc[...] = jnp.zeros_like(lacc)

        @pl.when(s >= 2)
        def _():
            conv_out(slot, pf(s - 2)).wait()
            states_out(slot, pf(s - 2)).wait()

        valid = jnp.where(s >= 2, 1.0, 0.0)
        mx_prev = mxs.at[1 - slot]
        f3 = {}

        def f3_fwd():
            f3["h2"] = x_ref[...] + _dot(mx_prev[...], w_ref[...])

        def f3_loss():
            h2 = f3.pop("h2")
            ms = jnp.mean(h2 * h2, axis=-1, keepdims=True)
            rstd = lax.rsqrt(ms + EPS)
            yh = h2 * rstd
            g = fg_ref[...]
            e = (yh * g - t_ref[...]) * valid
            lacc[...] += jnp.sum(e * e, axis=0, keepdims=True)
            dy = e * (1.0 / D_MODEL)
            gfg_ref[...] += jnp.sum(dy * yh, axis=0, keepdims=True)
            dyh = dy * g
            dh2 = rstd * (dyh - yh * jnp.mean(dyh * yh, axis=-1, keepdims=True))
            dh2_ref[...] = dh2
            f3["db"] = dh2.astype(BF16)

        def f3_dmx():
            dmx_ref[...] = _dot_tb(f3["db"], w_ref[...]).astype(BF16)

        def f3_gw():
            gw = _dot_ta(mx_prev[...], f3["db"])
            for j in range(N_CHIPS):
                for hf in range(2):
                    r0 = j * 256 + hf * 128
                    gwo_ref[hf, j] += gw[r0:r0 + 128, :]

        cx = proj_ref[:, CX:CX + 512].astype(F32)
        cc = proj_ref[:, CC:CC + 512].astype(F32)
        u = cc * cx
        row = lax.broadcasted_iota(jnp.int32, (TM, D_CONV), 0)
        h7 = uhalo[7:8, :]
        h6 = uhalo[6:7, :]
        u1 = jnp.where(row == 0, h7, pltpu.roll(u, 1, 0))
        u2 = jnp.where(row == 0, h6, jnp.where(row == 1, h7, pltpu.roll(u, 2, 0)))
        conv = cw_ref[2:3, :] * u + cw_ref[1:2, :] * u1 + cw_ref[0:1, :] * u2
        uhalo[...] = u[TM - 8:TM, :]
        cb = proj_ref[:, CB:CB + 512].astype(F32)
        cg = proj_ref[:, CG:CG + 512].astype(F32)
        mixed_ref[:, 0:D_CONV] = (cb * conv * (cg * _sigmoid(cg))).astype(BF16)
        conv_ref[...] = conv.astype(BF16)
        f3_fwd()

        scale = HEAD_DIM ** -0.5
        H = range(RET_HEADS)
        st = [state[h] for h in H]
        between = [f3_loss, f3_dmx, f3_gw, None]
        rc_t, rs_t = _tile_rotary(tt_ref, cr_ref, sr_ref, sgn_ref)
        for c in range(NCH):
            r0 = c * CHUNK
            rc = rc_t[r0:r0 + CHUNK, :]
            rs = rs_t[r0:r0 + CHUNK, :]
            col = lambda base, h: slice(base + h * HEAD_DIM, base + (h + 1) * HEAD_DIM)
            rws = slice(r0, r0 + CHUNK)
            v = [proj_ref[rws, col(CV, h)] for h in H]
            qf = [_rot(proj_ref[rws, col(CQ, h)].astype(F32), rc, rs) * scale for h in H]
            kf = [_rot(proj_ref[rws, col(CK, h)].astype(F32), rc, rs) for h in H]
            stb = [t.astype(BF16) for t in st]
            for h in H:
                states_ref[c, h] = stb[h]
            a = [(_dot_tb(qf[h].astype(BF16), kf[h].astype(BF16)) * dec_ref[h]).astype(BF16) for h in H]
            o = [_dot(a[h], v[h]) + _dot((qf[h] * xi_ref[h]).astype(BF16), stb[h]) for h in H]
            st = [cd_ref[h, 0:1, :] * st[h] + _dot_ta((kf[h] * zeta_ref[h]).astype(BF16), v[h]) for h in H]
            for h in H:
                mu = jnp.mean(o[h], axis=-1, keepdims=True)
                d = o[h] - mu
                var = jnp.mean(d * d, axis=-1, keepdims=True)
                yh = d * lax.rsqrt(var + EPS)
                rg = proj_ref[rws, col(CR, h)].astype(F32)
                mixed_ref[rws, col(D_CONV, h)] = (yh * g_ref[:, col(0, h)] * (rg * _sigmoid(rg))).astype(BF16)
            if between[c] is not None:
                between[c]()
        for h in H:
            state[h] = st[h]

        @pl.when(s <= nt)
        def _():
            conv_out(slot, pf(s)).start()
            states_out(slot, pf(s)).start()

        @pl.when(s == nt + 1)
        def _():
            conv_out(1 - slot, pf(s - 1)).wait()
            states_out(1 - slot, pf(s - 1)).wait()
            tot = jnp.sum(lacc[...], axis=1, keepdims=True) * (0.5 / D_MODEL)
            loss_ref[...] = jnp.broadcast_to(tot, (1, 128))

    tile = lambda w: pl.BlockSpec((TM, w), lambda s: (pf(s), 0))
    xtile = lambda w: pl.BlockSpec((TM, w), lambda s: (xt(s), 0))
    any_spec = pl.BlockSpec(memory_space=pl.ANY)
    return pl.pallas_call(
        body, name="f2_mixer_fwd_f3_outproj_loss",
        grid=(nt + 2,),
        in_specs=[tile(N_PROJ_COLS), _resident((8, D_CONV)), _resident((1, D_RET)),
                  pl.BlockSpec((None, 8, HEAD_DIM), lambda s: (pf(s), 0, 0)),
                  _resident((TM, HEAD_DIM)), _resident((TM, HEAD_DIM)), _resident((8, HEAD_DIM)),
                  _resident((RET_HEADS, CHUNK, CHUNK)), _resident((RET_HEADS, CHUNK, HEAD_DIM)),
                  _resident((RET_HEADS, CHUNK, HEAD_DIM)), _resident((RET_HEADS, 8, HEAD_DIM)),
                  xtile(D_MODEL), _resident1((D_MODEL, D_MODEL)), _resident((1, D_MODEL)), xtile(D_MODEL)],
        out_specs=[any_spec, any_spec, xtile(D_MODEL), xtile(D_MODEL),
                   _resident((2, N_CHIPS, 128, D_MODEL)), _resident((1, D_MODEL)), _resident((1, 128))],
        out_shape=[jax.ShapeDtypeStruct((rows, D_CONV), BF16),
                   jax.ShapeDtypeStruct(((nt + 1) * NCH, RET_HEADS, HEAD_DIM, HEAD_DIM), BF16),
                   jax.ShapeDtypeStruct((seq, D_MODEL), F32),
                   jax.ShapeDtypeStruct((seq, D_MODEL), BF16),
                   jax.ShapeDtypeStruct((2, N_CHIPS, 128, D_MODEL), F32),
                   jax.ShapeDtypeStruct((1, D_MODEL), F32),
                   jax.ShapeDtypeStruct((1, 128), F32)],
        scratch_shapes=[pltpu.VMEM((RET_HEADS, HEAD_DIM, HEAD_DIM), F32), pltpu.VMEM((8, D_CONV), F32),
                        pltpu.VMEM((2, TM, D_MODEL), BF16), pltpu.VMEM((2, TM, D_CONV), BF16),
                        pltpu.VMEM((2, NCH, RET_HEADS, HEAD_DIM, HEAD_DIM), BF16),
                        pltpu.VMEM((1, D_MODEL), F32), pltpu.SemaphoreType.DMA((4,))],
        compiler_params=_cparams(("arbitrary",)),
    )(proj, conv_w8, gret, tb["tt"], tb["cr2"], tb["sr2"], tb["sgn"], tb["decay"], tb["xi"], tb["zeta"], tb["cd"],
      x, w_out, fg, target)


def _b2_b1a_call(proj, dmixed, conv_s, states, conv_w8, gret, tb, w_in_g, x, meta_tile, g1, dh2):
    nt, rows = tb["nt"], tb["rows"]
    seq = nt * TM

    def pb(r):
        return jnp.where(r == nt, nt, nt - 1 - r)

    def xprev(r):
        return jnp.clip(nt - r, 0, nt - 1)

    def body(proj_ref, dmx_ref, conv_ref, states_ref, cw_ref, g_ref, tt_ref, cr_ref, sr_ref, sgn_ref, dec_ref,
             xi_ref, zeta_ref, cd_ref, w_ref, x_ref, mt_ref, g1_ref, dh2_ref,
             dproj_hbm, gcw_ref, gg_ref, gx_ref, dmeta_ref, gn_ref,
             gstate, dchalo, dps, out_sems):
        r = pl.program_id(0)
        live = jnp.where(r == nt, 0.0, 1.0)
        slot = lax.rem(r, 2)
        dproj_ref = dps.at[slot]

        class to_hbm:
            def __init__(self, s, tile):
                self.copies = [pltpu.make_async_copy(dps.at[s, :, j * 1024:(j + 1) * 1024], dproj_hbm.at[tile, j],
                                                     out_sems.at[N_CHIPS * s + j]) for j in range(N_CHIPS)]

            def start(self):
                for cp in self.copies:
                    cp.start()

            def wait(self):
                for cp in self.copies:
                    cp.wait()

        @pl.when(r == 0)
        def _():
            gstate[...] = jnp.zeros_like(gstate)
            dchalo[...] = jnp.zeros_like(dchalo)
            gcw_ref[...] = jnp.zeros_like(gcw_ref)
            gg_ref[...] = jnp.zeros_like(gg_ref)
            gn_ref[...] = jnp.zeros_like(gn_ref)
            dps[...] = jnp.zeros_like(dps)

        @pl.when(r >= 2)
        def _():
            to_hbm(slot, pb(r - 2)).wait()

        dprev = dps.at[1 - slot]
        pieces = []

        def emit_piece():
            j = len(pieces)
            if j < N_CHIPS:
                p = _dot_tb(dprev[:, j * 1024:(j + 1) * 1024], w_ref[j])
                pieces.append(p if j == 0 else pieces[-1] + p)

        cx = proj_ref[:, CX:CX + 512].astype(F32)
        cb = proj_ref[:, CB:CB + 512].astype(F32)
        cc = proj_ref[:, CC:CC + 512].astype(F32)
        cg = proj_ref[:, CG:CG + 512].astype(F32)
        dco = dmx_ref[:, 0:D_CONV].astype(F32) * live
        conv = conv_ref[...].astype(F32)
        sg = _sigmoid(cg)
        sil = cg * sg
        t = dco * conv
        dproj_ref[:, CB:CB + 512] = (t * sil).astype(BF16)
        dproj_ref[:, CG:CG + 512] = (t * cb * (sg * (1.0 + cg * (1.0 - sg)))).astype(BF16)
        dconv = dco * cb * sil
        row = lax.broadcasted_iota(jnp.int32, (TM, D_CONV), 0)
        n0 = dchalo[0:1, :]
        n1 = dchalo[1:2, :]
        dc1 = jnp.where(row == TM - 1, n0, pltpu.roll(dconv, TM - 1, 0))
        dc2 = jnp.where(row == TM - 2, n0, jnp.where(row == TM - 1, n1, pltpu.roll(dconv, TM - 2, 0)))
        dchalo[...] = dconv[0:8, :]
        du = cw_ref[2:3, :] * dconv + cw_ref[1:2, :] * dc1 + cw_ref[0:1, :] * dc2
        u = cc * cx
        gcw_ref[2:3, :] += jnp.sum(u * dconv, axis=0, keepdims=True)
        gcw_ref[1:2, :] += jnp.sum(u * dc1, axis=0, keepdims=True)
        gcw_ref[0:1, :] += jnp.sum(u * dc2, axis=0, keepdims=True)
        dproj_ref[:, CC:CC + 512] = (du * cx).astype(BF16)
        dproj_ref[:, CX:CX + 512] = (du * cc).astype(BF16)
        emit_piece()
        emit_piece()

        scale = HEAD_DIM ** -0.5
        gs = {h: gstate[h] for h in range(RET_HEADS)}
        gg = {h: jnp.zeros((1, HEAD_DIM), F32) for h in range(RET_HEADS)}
        col = lambda base, h: slice(base + h * HEAD_DIM, base + (h + 1) * HEAD_DIM)
        rw = lambda c: slice(c * CHUNK, (c + 1) * CHUNK)
        rc_t, rs_t = _tile_rotary(tt_ref, cr_ref, sr_ref, sgn_ref)
        for c0 in range(NCH - CHUNK_GROUP, -1, -CHUNK_GROUP):
            cs = range(c0 + CHUNK_GROUP - 1, c0 - 1, -1)
            U = [(c, h) for c in cs for h in range(RET_HEADS)]
            rc = {c: rc_t[rw(c), :] for c in cs}
            rs = {c: rs_t[rw(c), :] for c in cs}
            v = {(c, h): proj_ref[rw(c), col(CV, h)] for c, h in U}
            stb = {(c, h): states_ref[c, h] for c, h in U}
            qf = {(c, h): _rot(proj_ref[rw(c), col(CQ, h)].astype(F32), rc[c], rs[c]) * scale for c, h in U}
            kf = {(c, h): _rot(proj_ref[rw(c), col(CK, h)].astype(F32), rc[c], rs[c]) for c, h in U}
            qb = {u: qf[u].astype(BF16) for u in U}
            kb = {u: kf[u].astype(BF16) for u in U}
            qxb = {(c, h): (qf[c, h] * xi_ref[h]).astype(BF16) for c, h in U}
            kzb = {(c, h): (kf[c, h] * zeta_ref[h]).astype(BF16) for c, h in U}
            ab = {(c, h): (_dot_tb(qb[c, h], kb[c, h]) * dec_ref[h]).astype(BF16) for c, h in U}
            o = {u: _dot(ab[u], v[u]) + _dot(qxb[u], stb[u]) for u in U}
            emit_piece()
            dob = {}
            for c, h in U:
                mu = jnp.mean(o[c, h], axis=-1, keepdims=True)
                d = o[c, h] - mu
                var = jnp.mean(d * d, axis=-1, keepdims=True)
                rstd = lax.rsqrt(var + EPS)
                yh = d * rstd
                g = g_ref[:, col(0, h)]
                rg = proj_ref[rw(c), col(CR, h)].astype(F32)
                dro = dmx_ref[rw(c), col(D_CONV, h)].astype(F32) * live
                sg = _sigmoid(rg)
                dproj_ref[rw(c), col(CR, h)] = (dro * (yh * g) * (sg * (1.0 + rg * (1.0 - sg)))).astype(BF16)
                dret = dro * (rg * sg)
                gg[h] = gg[h] + jnp.sum(dret * yh, axis=0, keepdims=True)
                dyh = dret * g
                do = rstd * (dyh - jnp.mean(dyh, axis=-1, keepdims=True)
                             - yh * jnp.mean(dyh * yh, axis=-1, keepdims=True))
                dob[c, h] = do.astype(BF16)
            dv1 = {u: _dot_ta(ab[u], dob[u]) for u in U}
            ds = {(c, h): (_dot_tb(dob[c, h], v[c, h]) * dec_ref[h]).astype(BF16) for c, h in U}
            gup = {u: _dot_ta(qxb[u], dob[u]) for u in U}
            dq = {(c, h): _dot(ds[c, h], kb[c, h]) + _dot_tb(dob[c, h], stb[c, h]) * xi_ref[h] for c, h in U}
            dk1 = {u: _dot_ta(ds[u], qb[u]) for u in U}
            emit_piece()
            for c, h in U:
                gsb = gs[h].astype(BF16)
                dv = dv1[c, h] + _dot(kzb[c, h], gsb)
                dk = dk1[c, h] + _dot_tb(v[c, h], gsb) * zeta_ref[h]
                gs[h] = cd_ref[h, 0:1, :] * gs[h] + gup[c, h]
                dproj_ref[rw(c), col(CQ, h)] = (_rot_t(dq[c, h], rc[c], rs[c]) * scale).astype(BF16)
                dproj_ref[rw(c), col(CK, h)] = _rot_t(dk, rc[c], rs[c]).astype(BF16)
                dproj_ref[rw(c), col(CV, h)] = dv.astype(BF16)
        for h in range(RET_HEADS):
            gstate[h] = gs[h]
            gg_ref[:, col(0, h)] += gg[h]

        while len(pieces) < N_CHIPS:
            emit_piece()

        def norm_bwd(dhn, hx):
            ms = jnp.mean(hx * hx, axis=-1, keepdims=True)
            rstd1 = lax.rsqrt(ms + EPS)
            xh = hx * rstd1
            gn_ref[...] += jnp.sum(dhn * xh, axis=0, keepdims=True)
            dxh = dhn * g1_ref[...]
            return rstd1 * (dxh - xh * jnp.mean(dxh * xh, axis=-1, keepdims=True))

        gx_ref[...] = norm_bwd(pieces[-1], x_ref[...]) + dh2_ref[...]

        @pl.when(r < nt)
        def _():
            to_hbm(slot, pb(r)).start()

        @pl.when(r == nt)
        def _():
            to_hbm(slot, pb(r)).start()
            mrows = slice(TM - N_META, TM)
            d16 = dproj_ref[mrows, :]
            dhn16 = _dot_tb(d16[:, 0:1024], w_ref[0])
            for j in range(1, N_CHIPS):
                dhn16 += _dot_tb(d16[:, j * 1024:(j + 1) * 1024], w_ref[j])
            dmeta_ref[...] = norm_bwd(dhn16, mt_ref[mrows, :])
            to_hbm(1 - slot, pb(r - 1)).wait()
            to_hbm(slot, pb(r)).wait()

    tile = lambda w: pl.BlockSpec((TM, w), lambda r: (pb(r), 0))
    xtile = pl.BlockSpec((TM, D_MODEL), lambda r: (xprev(r), 0))
    return pl.pallas_call(
        body, name="b2_mixer_bwd_b1a_inproj_bwd_x",
        grid=(nt + 1,),
        in_specs=[tile(N_PROJ_COLS),
                  pl.BlockSpec((TM, D_MODEL), lambda r: (jnp.minimum(pb(r), nt - 1), 0)),
                  tile(D_CONV),
                  pl.BlockSpec((NCH, RET_HEADS, HEAD_DIM, HEAD_DIM), lambda r: (pb(r), 0, 0, 0)),
                  _resident((8, D_CONV)), _resident((1, D_RET)),
                  pl.BlockSpec((None, 8, HEAD_DIM), lambda r: (pb(r), 0, 0)),
                  _resident((TM, HEAD_DIM)), _resident((TM, HEAD_DIM)), _resident((8, HEAD_DIM)),
                  _resident((RET_HEADS, CHUNK, CHUNK)),
                  _resident((RET_HEADS, CHUNK, HEAD_DIM)),
                  _resident((RET_HEADS, CHUNK, HEAD_DIM)), _resident((RET_HEADS, 8, HEAD_DIM)),
                  _resident1((N_CHIPS, D_MODEL, 1024)), xtile, _resident1((TM, D_MODEL)), _resident((1, D_MODEL)),
                  xtile],
        out_specs=[pl.BlockSpec(memory_space=pl.ANY), _resident((8, D_CONV)), _resident((1, D_RET)),
                   xtile, _resident((N_META, D_MODEL)), _resident((1, D_MODEL))],
        out_shape=[jax.ShapeDtypeStruct((nt + 1, N_CHIPS, TM, 1024), BF16),
                   jax.ShapeDtypeStruct((8, D_CONV), F32),
                   jax.ShapeDtypeStruct((1, D_RET), F32),
                   jax.ShapeDtypeStruct((seq, D_MODEL), F32),
                   jax.ShapeDtypeStruct((N_META, D_MODEL), F32),
                   jax.ShapeDtypeStruct((1, D_MODEL), F32)],
        scratch_shapes=[pltpu.VMEM((RET_HEADS, HEAD_DIM, HEAD_DIM), F32), pltpu.VMEM((8, D_CONV), F32),
                        pltpu.VMEM((2, TM, N_PROJ_COLS), BF16), pltpu.SemaphoreType.DMA((2 * N_CHIPS,))],
        compiler_params=_cparams(("arbitrary",), vmem=VMEM_LIMIT_MAX),
    )(proj, dmixed, conv_s, states, conv_w8, gret, tb["tt"], tb["cr2"], tb["sr2"], tb["sgn"], tb["decay"],
      tb["xi"], tb["zeta"], tb["cd"], w_in_g, x, meta_tile, g1, dh2)


REL = (2, 1, 3)
SMALL_ROWS = 24
HALF_STEP = 4


def _rcopy(src, dst, send_sems, recv_sems, k, to):
    return pltpu.make_async_remote_copy(src_ref=src, dst_ref=dst, send_sem=send_sems.at[k],
                                        recv_sem=recv_sems.at[k], device_id=to, device_id_type=MESH_ID)


def _b1b_reduce_call(order, hnt, dproj, gwo, small, nt):
    nk = nt + 1
    last = nk - 1
    any_spec = pl.BlockSpec(memory_space=pl.ANY)

    def body(order_ref, a_ref, b_ref, gm_ref, gn1_ref, gfg_ref, ggr_ref, gcw_ref, ls_ref, gwo_hbm,
             gwin_hbm, gwout_hbm, tot_hbm,
             acc, sb, abuf, pb, bbuf, fin, go, ao, pbo, bo, fino, slots, totv, send_sems, recv_sems, loc_sems):
        jj, k = pl.program_id(0), pl.program_id(1)
        x, y, c = lax.axis_index("x"), lax.axis_index("y"), lax.axis_index("c")
        me, myid, sib = 2 * x + y, 4 * x + 2 * y + c, (x, y, 1 - c)
        rc = functools.partial(_rcopy, send_sems=send_sems, recv_sems=recv_sems)
        peers = [((1 - x) if r & 2 else x, (1 - y) if r & 1 else y, c) for r in REL]
        kids = [jnp.bitwise_xor(me, r) for r in REL]

        def dev_peer(r):
            return ((1 - x) if r & 4 else x, (1 - y) if r & 2 else y, (1 - c) if r & 1 else c)

        own_go = pltpu.make_async_copy(gwo_hbm.at[c], go, loc_sems.at[0])
        wo_half = rc(gwo_hbm.at[1 - c], ao, k=8, to=sib)
        wo_part = [rc(pbo.at[kids[p]], bo.at[p], k=9 + p, to=peers[p]) for p in range(3)]
        sm = [rc(slots.at[0], slots.at[r], k=12 + r, to=dev_peer(r)) for r in range(1, N_DEV)]
        half = [rc(sb.at[j % 2, 1 - c], abuf.at[j], k=j, to=sib) for j in range(N_CHIPS)]
        part = [rc(pb.at[p], bbuf.at[p], k=4 + p, to=peers[p]) for p in range(3)]

        @pl.when(jnp.logical_and(jj == 0, k == 0))
        def _():
            own_go.start()
            slots[0, 0:N_META, :] = gm_ref[...]
            slots[0, N_META:SMALL_ROWS, :] = jnp.zeros((SMALL_ROWS - N_META, D_MODEL), F32)
            slots[0, N_META:N_META + 1, :] = gn1_ref[...]
            slots[0, N_META + 1:N_META + 2, :] = gfg_ref[...]
            slots[0, N_META + 2:N_META + 3, 0:D_RET] = ggr_ref[...]
            slots[0, N_META + 3:N_META + 6, 0:D_CONV] = gcw_ref[0:3, :]
            slots[0, N_META + 6:N_META + 7, 0:128] = ls_ref[...]
            wo_half.start()
            for cp in sm:
                cp.start()

        @pl.when(k == 0)
        def _():
            acc[...] = jnp.zeros_like(acc)

        acc[0] += _dot(a_ref[0:512, :], b_ref[...])
        acc[1] += _dot(a_ref[512:1024, :], b_ref[...])

        @pl.when(k == HALF_STEP)
        def _():
            @pl.when(jj == 0)
            def _():
                own_go.wait()
                wo_half.wait_recv()
                for j in range(N_CHIPS):
                    go[j] = go[j] + ao[j]
                pbo[...] = go[...].astype(BF16)
                for cp in wo_part:
                    cp.start()

            for p in range(3):
                @pl.when(jj == p + 1)
                def _(p=p):
                    half[p].wait_recv()
                    half[p].wait_send()
                    pb[p] = (sb[p % 2, c] + abuf[p]).astype(BF16)
                    part[p].start()

        @pl.when(k == last)
        def _():
            for j in range(N_CHIPS):
                @pl.when(jj == j)
                def _(j=j):
                    sb[j % 2] = acc[...]
                    half[j].start()

        @pl.when(jnp.logical_and(jj == N_CHIPS - 1, k == last))
        def _():
            half[3].wait_recv()
            own = sb[1, c] + abuf[3]
            for cp in part:
                cp.wait_recv()
            fin[c] = ((own + bbuf[0].astype(F32)) + bbuf[1].astype(F32)) + bbuf[2].astype(F32)
            done = rc(fin.at[c], fin.at[c], k=7, to=sib)
            done.start()
            for cp in wo_part:
                cp.wait_recv()
            fino[c] = ((go[me] + bo[0].astype(F32)) + bo[1].astype(F32)) + bo[2].astype(F32)
            done_o = rc(fino.at[c], fino.at[c], k=12, to=sib)
            done_o.start()
            for cp in sm:
                cp.wait_recv()
            tot = slots[myid]
            for a in range(1, N_DEV):
                tot = tot + slots[jnp.bitwise_xor(myid, a)]
            totv[...] = tot
            out_t = pltpu.make_async_copy(totv, tot_hbm, loc_sems.at[1])
            out_t.start()
            rc(fin.at[1 - c], fin.at[1 - c], k=7, to=sib).wait_recv()
            out_w = pltpu.make_async_copy(fin, gwin_hbm, loc_sems.at[0])
            out_w.start()
            rc(fino.at[1 - c], fino.at[1 - c], k=12, to=sib).wait_recv()
            out_o = pltpu.make_async_copy(fino, gwout_hbm, loc_sems.at[2])
            out_o.start()
            for cp in [half[3]] + part + [done, wo_half] + wo_part + [done_o] + sm:
                cp.wait_send()
            out_t.wait()
            out_w.wait()
            out_o.wait()

    grid_spec = pltpu.PrefetchScalarGridSpec(
        num_scalar_prefetch=1,
        grid=(N_CHIPS, nk),
        in_specs=[pl.BlockSpec((None, D_MODEL, TM), lambda j, k, o: (k, 0, 0)),
                  pl.BlockSpec((None, None, TM, 1024), lambda j, k, o: (k, o[j], 0, 0))]
                 + [pl.BlockSpec(s.shape, lambda j, k, o: (0, 0)) for s in small] + [any_spec],
        out_specs=[any_spec, any_spec, any_spec],
        scratch_shapes=[
            pltpu.VMEM((2, 512, 1024), F32),
            pltpu.VMEM((2, 2, 512, 1024), F32),
            pltpu.VMEM((N_CHIPS, 512, 1024), F32),
            pltpu.VMEM((3, 512, 1024), BF16),
            pltpu.VMEM((3, 512, 1024), BF16),
            pltpu.VMEM((2, 512, 1024), F32),
            pltpu.VMEM((N_CHIPS, 128, D_MODEL), F32),
            pltpu.VMEM((N_CHIPS, 128, D_MODEL), F32),
            pltpu.VMEM((N_CHIPS, 128, D_MODEL), BF16),
            pltpu.VMEM((3, 128, D_MODEL), BF16),
            pltpu.VMEM((2, 128, D_MODEL), F32),
            pltpu.VMEM((N_DEV, SMALL_ROWS, D_MODEL), F32),
            pltpu.VMEM((SMALL_ROWS, D_MODEL), F32),
            pltpu.SemaphoreType.DMA((20,)), pltpu.SemaphoreType.DMA((20,)), pltpu.SemaphoreType.DMA((3,))])
    return pl.pallas_call(
        body, name="b1b_inproj_bwd_w_reduce",
        grid_spec=grid_spec,
        out_shape=[jax.ShapeDtypeStruct((2, 512, 1024), F32),
                   jax.ShapeDtypeStruct((2, 128, D_MODEL), F32),
                   jax.ShapeDtypeStruct((SMALL_ROWS, D_MODEL), F32)],
        compiler_params=_cparams(("arbitrary", "arbitrary")),
    )(order, hnt, dproj, *small, gwo)


def _local_step(me, x, target, g1, gret, fg, win_sh, wout_sh, meta_sh, convw_sh):
    seq = x.shape[0]
    tb = _tables(seq)
    nt = tb["nt"]
    g1r, gretr, fgr = g1.reshape(1, -1), gret.reshape(1, -1), fg.reshape(1, -1)
    order = jnp.stack([me, me ^ REL[0], me ^ REL[1], me ^ REL[2]]).astype(jnp.int32)

    proj, hnt, w_in_g, w_out_g, meta_tile, conv_w8 = _f1_gather_call(order, x, g1r, win_sh, wout_sh, meta_sh,
                                                                     convw_sh, nt)
    w_out = w_out_g.reshape(D_MODEL, D_MODEL)
    conv_s, states, dh2, dmixed, g_wout, g_fg, loss = _f2_f3_call(proj, conv_w8, gretr, tb, x, w_out, fgr, target)
    dproj, g_cw8, g_gret, grad_x, g_meta, g_g1 = _b2_b1a_call(proj, dmixed, conv_s, states, conv_w8, gretr, tb,
                                                              w_in_g, x, meta_tile, g1r, dh2)
    return grad_x, g_wout, (g_meta, g_g1, g_fg, g_gret, g_cw8, loss), hnt, dproj


def _adamw_update(w_ref, g_ref, m_ref, v_ref, d_ref, nm_ref, nv_ref):
    gg = g_ref[...]
    nm = ADAM_B1 * m_ref[...] + (1.0 - ADAM_B1) * gg
    nv = ADAM_B2 * v_ref[...] + (1.0 - ADAM_B2) * (gg * gg)
    m_hat = nm / (1.0 - ADAM_B1 ** ADAM_STEP)
    v_hat = nv / (1.0 - ADAM_B2 ** ADAM_STEP)
    d_ref[...] = -ADAM_LR * (m_hat / (jnp.sqrt(v_hat) + ADAM_EPS) + ADAM_WD * w_ref[...])
    nm_ref[...] = nm
    nv_ref[...] = nv


def _adamw_small_call(me, tot, ws, ms, vs):
    n = len(ws)

    def body(me_ref, tmeta_ref, tvec_ref, tconv_ref, *refs):
        ins, outs = refs[:3 * n], refs[3 * n:]
        g_refs, loss_ref, upd = outs[0:n], outs[n], outs[n + 1:]
        g_refs[0][...] = tmeta_ref[...]
        g_refs[1][...] = tvec_ref[0:1, :]
        g_refs[2][...] = tconv_ref[3:6, :]
        g_refs[3][...] = tvec_ref[2:3, 0:D_RET]
        g_refs[4][...] = tvec_ref[1:2, :]
        loss_ref[...] = tvec_ref[6:7, 0:1]
        for i in range(n):
            _adamw_update(ins[i], g_refs[i], ins[n + i], ins[2 * n + i], upd[i], upd[n + i], upd[2 * n + i])

    whole = lambda a: pl.BlockSpec(a.shape, lambda i, m: (0,) * a.ndim)
    shapes = [jax.ShapeDtypeStruct(w.shape, F32) for w in ws]
    out_shape = shapes + [jax.ShapeDtypeStruct((1, 1), F32)] + shapes * 3
    grid_spec = pltpu.PrefetchScalarGridSpec(
        num_scalar_prefetch=1, grid=(1,),
        in_specs=[pl.BlockSpec((N_META, 256), lambda i, m: (0, m[0])),
                  pl.BlockSpec((8, D_MODEL), lambda i, m: (N_META // 8, 0)),
                  pl.BlockSpec((8, 128), lambda i, m: (N_META // 8, m[0]))] + [whole(a) for a in ws + ms + vs],
        out_specs=[whole(s) for s in out_shape])
    outs = pl.pallas_call(body, name="adamw_small", grid_spec=grid_spec, out_shape=out_shape,
                          compiler_params=_cparams(("arbitrary",)))(me.reshape(1), tot, tot, tot, *ws, *ms, *vs)
    return outs[:n], outs[n], outs[n + 1:2 * n + 1], outs[2 * n + 1:3 * n + 1], outs[3 * n + 1:]


def _adamw_call(w, g, m, v, name):
    shape = w.shape
    w2, g2, m2, v2 = (a.reshape(-1, shape[-1]) for a in (w, g, m, v))
    rows, cols = w2.shape
    br = 256 if rows % 256 == 0 else rows
    def body(w_ref, g_ref, m_ref, v_ref, d_ref, nm_ref, nv_ref, g_out):
        del g_out
        _adamw_update(w_ref, g_ref, m_ref, v_ref, d_ref, nm_ref, nv_ref)

    spec = pl.BlockSpec((br, cols), lambda i: (i, 0))
    outs = pl.pallas_call(
        body, name=name, grid=(rows // br,),
        in_specs=[spec] * 4, out_specs=[spec] * 3 + [pl.BlockSpec(memory_space=pl.ANY)],
        out_shape=[jax.ShapeDtypeStruct((rows, cols), F32)] * 4,
        input_output_aliases={1: 3},
        compiler_params=_cparams(("arbitrary",)),
    )(w2, g2, m2, v2)
    return tuple(o.reshape(shape) for o in outs)


def kernel(x, meta, norm1_g, w_in, conv_w, ret_norm_g, w_out, final_g, loss_target, m_meta, m_norm1_g, m_w_in, m_conv_w, m_ret_norm_g, m_w_out, m_final_g, v_meta, v_norm1_g, v_w_in, v_conv_w, v_ret_norm_g, v_w_out, v_final_g):
    me = 2 * lax.axis_index("x") + lax.axis_index("y")

    grad_x, g_wo, small, hnt, dproj = _local_step(me, x[0], loss_target[0], norm1_g, ret_norm_g, final_g,
                                                  w_in, w_out, meta, conv_w)

    order = jnp.stack([me ^ REL[0], me ^ REL[1], me ^ REL[2], me]).astype(jnp.int32)
    g_win, g_wout, tot = _b1b_reduce_call(order, hnt, dproj, g_wo, small, x.shape[1] // TM)
    g_win, g_wout = g_win.reshape(D_MODEL, 1024), g_wout.reshape(256, D_MODEL)

    ws = [meta, norm1_g, w_in, conv_w, ret_norm_g, w_out, final_g]
    ms = [m_meta, m_norm1_g, m_w_in, m_conv_w, m_ret_norm_g, m_w_out, m_final_g]
    vs = [v_meta, v_norm1_g, v_w_in, v_conv_w, v_ret_norm_g, v_w_out, v_final_g]
    names = ["meta", "norm1_g", "w_in", "conv_w", "ret_norm_g", "w_out", "final_g"]
    as2d = lambda a: a.reshape(1, -1) if a.ndim == 1 else a
    big = {names.index("w_in"): g_win, names.index("w_out"): g_wout}
    small = [i for i in range(len(names)) if i not in big]
    grads, deltas, new_ms, new_vs = [None] * 7, [None] * 7, [None] * 7, [None] * 7
    for i, g_ in big.items():
        deltas[i], new_ms[i], new_vs[i], grads[i] = _adamw_call(ws[i], g_, ms[i], vs[i], "adamw_" + names[i])
    sg, loss_tot, sd, sm_, sv = _adamw_small_call(me.astype(jnp.int32), tot,
                                                  *[[as2d(t[i]) for i in small] for t in (ws, ms, vs)])
    for j, i in enumerate(small):
        grads[i], deltas[i], new_ms[i], new_vs[i] = (o[j].reshape(ws[i].shape) for o in (sg, sd, sm_, sv))
    return (loss_tot.reshape(()), grad_x[None], *grads, *deltas, *new_ms, *new_vs)
```

```python
import functools

import jax
import jax.numpy as jnp
import numpy as np
from jax import lax
from jax.experimental import pallas as pl
from jax.experimental.pallas import tpu as pltpu

F32 = jnp.float32
BF16 = jnp.bfloat16

D_MODEL = 1024
N_META = 16
D_CONV = 512
D_RET = 512
RET_HEADS = 4
HEAD_DIM = 128
CHUNK = 128
N_PROJ_COLS = 4096
ROPE_BASE = 10000.0
EPS = 1e-6
N_CHIPS = 4
N_DEV = 8

ADAM_LR = 0.001
ADAM_B1 = 0.9
ADAM_B2 = 0.999
ADAM_EPS = 1e-08
ADAM_WD = 0.01
ADAM_STEP = 10

TM = 512
NCH = TM // CHUNK
CHUNK_GROUP = 2
VMEM_LIMIT = 56 * 1024 * 1024
VMEM_LIMIT_MAX = 63 * 1024 * 1024

CX, CB, CC, CG, CQ, CK, CV, CR = (i * 512 for i in range(8))

MESH_ID = pl.DeviceIdType.MESH


def _cparams(sem=None, vmem=VMEM_LIMIT, **kw):
    return pltpu.CompilerParams(dimension_semantics=sem, vmem_limit_bytes=vmem, **kw)


def _sigmoid(x):
    return 1.0 / (1.0 + jnp.exp(-x))


def _dot(a, b):
    return jnp.dot(a, b, preferred_element_type=F32)


def _dot_tb(a, b):
    return lax.dot_general(a, b, (((1,), (1,)), ((), ())), preferred_element_type=F32)


def _dot_ta(a, b):
    return lax.dot_general(a, b, (((0,), (0,)), ((), ())), preferred_element_type=F32)


def _resident(shape):
    nd = len(shape)
    return pl.BlockSpec(shape, lambda *_: (0,) * nd)


def _resident1(shape):
    nd = len(shape)
    return pl.BlockSpec(shape, lambda *_: (0,) * nd, pipeline_mode=pl.Buffered(1))


def _tables(seq):
    f32 = np.float32
    nt = seq // TM
    rows = seq + TM
    half = HEAD_DIM // 2
    freqs = (f32(1.0) / (f32(ROPE_BASE) ** (np.arange(half, dtype=f32) / f32(half)))).astype(f32)
    tile_start = np.concatenate([np.arange(nt, dtype=f32), -np.ones((1,), f32)]) * f32(TM)
    ang_t = tile_start[:, None] * freqs[None, :]
    ang_r = (np.arange(TM, dtype=f32) + f32(N_META))[:, None] * freqs[None, :]
    dup = lambda a: np.concatenate([a, a], axis=-1).astype(f32)
    tt = np.stack([dup(np.cos(ang_t)), dup(np.sin(ang_t))], axis=1)
    tt = np.pad(tt, ((0, 0), (0, 6), (0, 0)))
    cr2, sr2 = dup(np.cos(ang_r)), dup(np.sin(ang_r))
    sgn = np.concatenate([-np.ones((8, half), f32), np.ones((8, half), f32)], axis=-1)
    log_g = np.log(f32(1.0) - f32(2.0) ** (f32(-5.0) - np.arange(RET_HEADS, dtype=f32))).astype(f32)
    idx = np.arange(CHUNK, dtype=f32)
    diff = idx[:, None] - idx[None, :]
    decay = np.where(diff[None] >= 0, np.exp(diff[None] * log_g[:, None, None]), f32(0.0)).astype(f32)
    zeta = np.exp((f32(CHUNK - 1) - idx)[None, :] * log_g[:, None]).astype(f32)
    xi = np.exp((idx + f32(1.0))[None, :] * log_g[:, None]).astype(f32)
    cd = np.exp(f32(CHUNK) * log_g).astype(f32)
    zeta_b = np.broadcast_to(zeta[:, :, None], (RET_HEADS, CHUNK, HEAD_DIM))
    xi_b = np.broadcast_to(xi[:, :, None], (RET_HEADS, CHUNK, HEAD_DIM))
    cd_b = np.broadcast_to(cd[:, None, None], (RET_HEADS, 8, HEAD_DIM))
    tables = dict(tt=tt, cr2=cr2, sr2=sr2, sgn=sgn, decay=decay, zeta=zeta_b, xi=xi_b, cd=cd_b)
    return dict(nt=nt, rows=rows, **{k: jnp.asarray(np.ascontiguousarray(v, dtype=f32)) for k, v in tables.items()})


def _tile_rotary(tt_ref, cr_ref, sr_ref, sgn_ref):
    ct, st = tt_ref[0:1, :], tt_ref[1:2, :]
    cr, sr = cr_ref[...], sr_ref[...]
    return ct * cr - st * sr, (st * cr + ct * sr) * sgn_ref[0:1, :]


def _rot(t, rc, rs):
    return t * rc + pltpu.roll(t, HEAD_DIM // 2, 1) * rs


def _rot_t(dt, rc, rs):
    return dt * rc + pltpu.roll(dt * rs, HEAD_DIM // 2, 1)


def _f1_gather_call(order, x, g1, win_sh, wout_sh, meta_sh, convw_sh, nt):
    nk = nt + 1
    rows = nk * TM
    any_spec = pl.BlockSpec(memory_space=pl.ANY)

    def body(order_ref, x_ref, g_ref, meta_ref, convw_ref, win_hbm, wout_hbm,
             proj_ref, hnt_ref, wg_hbm, wog_hbm, mt_hbm, cw_hbm,
             wg, wog, smg, mt, cw, hbs, st2, st3, send_sems, recv_sems, loc_sems):
        jj, k = pl.program_id(0), pl.program_id(1)
        x, y, c = lax.axis_index("x"), lax.axis_index("y"), lax.axis_index("c")
        me, sib = 2 * x + y, (x, y, 1 - c)
        rc = functools.partial(_rcopy, send_sems=send_sems, recv_sems=recv_sems)
        peers = [((1 - x) if r & 2 else x, (1 - y) if r & 1 else y, c) for r in REL]
        kids = [jnp.bitwise_xor(me, r) for r in REL]
        hw, ho = pl.ds(pl.multiple_of(c * 512, 512), 512), pl.ds(pl.multiple_of(c * 128, 128), 128)
        hw2 = pl.ds(pl.multiple_of((1 - c) * 512, 512), 512)
        ho2 = pl.ds(pl.multiple_of((1 - c) * 128, 128), 128)
        at = lambda j_, k_: jnp.logical_and(jj == j_, k == k_)

        sm_cp = [rc(smg.at[me], smg.at[me], k=p, to=peers[p]) for p in range(3)]
        win_cp = [rc(wg.at[me, hw], wg.at[me, hw], k=3 + p, to=peers[p]) for p in range(3)]
        wout_cp = [rc(wog.at[me, ho], wog.at[me, ho], k=6 + p, to=peers[p]) for p in range(3)]
        sm_in = [rc(smg.at[me], smg.at[kids[p]], k=p, to=sib) for p in range(3)]
        win_in = [rc(wg.at[me, hw], wg.at[kids[p], hw], k=3 + p, to=sib) for p in range(3)]
        wout_in = [rc(wog.at[me, ho], wog.at[kids[p], ho], k=6 + p, to=sib) for p in range(3)]
        win_fw = [rc(wg.at[kids[p], hw], wg.at[kids[p], hw], k=9 + p, to=sib) for p in range(3)]
        wout_fw = [rc(wog.at[kids[p], ho], wog.at[kids[p], ho], k=12 + p, to=sib) for p in range(3)]
        win_fw_in = [rc(wg.at[kids[p], hw2], wg.at[kids[p], hw2], k=9 + p, to=sib) for p in range(3)]
        wout_fw_in = [rc(wog.at[kids[p], ho2], wog.at[kids[p], ho2], k=12 + p, to=sib) for p in range(3)]

        def stage_own():
            cp_a = pltpu.make_async_copy(win_hbm.at[hw], mt, loc_sems.at[0])
            cp_b = pltpu.make_async_copy(win_hbm.at[hw2], st2, loc_sems.at[1])
            cp_c = pltpu.make_async_copy(wout_hbm, st3, loc_sems.at[2])
            cp_a.start()
            cp_c.start()
            cp_b.start()
            cp_a.wait()
            wg[me, hw, :] = mt[...].astype(BF16)
            for s_ in win_cp:
                s_.start()
            cp_c.wait()
            wog[me] = st3[...].astype(BF16)
            for s_ in wout_cp:
                s_.start()
            cp_b.wait()
            wg[me, hw2, :] = st2[...].astype(BF16)
        out_wg = pltpu.make_async_copy(wg, wg_hbm, loc_sems.at[3])
        out_wog = pltpu.make_async_copy(wog, wog_hbm, loc_sems.at[4])
        out_mt = pltpu.make_async_copy(mt, mt_hbm, loc_sems.at[5])
        out_cw = pltpu.make_async_copy(cw, cw_hbm, loc_sems.at[6])

        def pass_on(p):
            win_in[p].wait_recv()
            win_fw[p].start()

        @pl.when(k <= 1)
        def _():
            @pl.when(at(0, 0))
            def _():
                smg[me] = jnp.zeros((SMALL_ROWS, 256), F32)
                smg[me, 0:N_META, :] = meta_ref[...]
                smg[me, N_META:N_META + 3, 0:128] = convw_ref[...]
                for cp in sm_cp:
                    cp.start()
                stage_own()

            for p in range(3):
                @pl.when(at(p + 1, 0))
                def _(p=p):
                    win_fw_in[p].wait_recv()

            @pl.when(at(1, 1))
            def _():
                pass_on(1)

            @pl.when(at(3, 0))
            def _():
                out_wg.start()

            @pl.when(at(3, 1))
            def _():
                for p in range(3):
                    wout_in[p].wait_recv()
                    wout_fw[p].start()

        @pl.when(jnp.logical_and(jj == 0, k >= nk - 2))
        def _():
            @pl.when(k == nk - 2)
            def _():
                for cp in sm_in:
                    cp.wait_recv()
                mt[...] = jnp.zeros_like(mt)
                cw[...] = jnp.zeros_like(cw)
                for j in range(N_CHIPS):
                    mt[TM - N_META:TM, j * 256:(j + 1) * 256] = smg[j, 0:N_META, :]
                    cw[0:3, j * 128:(j + 1) * 128] = smg[j, N_META:N_META + 3, 0:128]
                out_mt.start()
                out_cw.start()

            @pl.when(k == nk - 1)
            def _():
                pass_on(0)

        @pl.when(at(2, nk // 2))
        def _():
            pass_on(2)

        tile_rows = pl.ds(pl.multiple_of(k * TM, TM), TM)

        @pl.when(jj == 0)
        def _():
            h = jnp.where(k == nt, mt[...], x_ref[...])
            ms = jnp.mean(h * h, axis=-1, keepdims=True)
            hn = (h * lax.rsqrt(ms + EPS)) * g_ref[...]
            hb = hn.astype(BF16)
            hbs[tile_rows, :] = hb
            proj_ref[...] = _dot(hb, wg[order_ref[0]]).astype(BF16)
            hnt_ref[...] = hn.T.astype(BF16)

        @pl.when(jj > 0)
        def _():
            proj_ref[...] = _dot(hbs[tile_rows, :], wg[order_ref[jj]]).astype(BF16)

        @pl.when(at(3, nk - 1))
        def _():
            for cp in wout_fw_in:
                cp.wait_recv()
            out_wog.start()
            for cp in sm_cp + win_cp + wout_cp + win_fw + wout_fw:
                cp.wait_send()
            for cp in (out_wg, out_wog, out_mt, out_cw):
                cp.wait()

    grid_spec = pltpu.PrefetchScalarGridSpec(
        num_scalar_prefetch=1,
        grid=(N_CHIPS, nk),
        in_specs=[pl.BlockSpec((TM, D_MODEL), lambda j, k, o: (jnp.where(j == 0, jnp.minimum(k, nt - 1), nt - 1), 0)),
                  pl.BlockSpec((1, D_MODEL), lambda j, k, o: (0, 0)),
                  pl.BlockSpec((N_META, 256), lambda j, k, o: (0, 0)),
                  pl.BlockSpec((3, 128), lambda j, k, o: (0, 0)),
                  any_spec, any_spec],
        out_specs=[pl.BlockSpec((TM, 1024), lambda j, k, o: (k, o[j])),
                   pl.BlockSpec((None, D_MODEL, TM), lambda j, k, o: (jnp.where(j == 0, k, nk - 1), 0, 0)),
                   any_spec, any_spec, any_spec, any_spec],
        scratch_shapes=[
            pltpu.VMEM((N_CHIPS, D_MODEL, 1024), BF16),
            pltpu.VMEM((N_CHIPS, 256, D_MODEL), BF16),
            pltpu.VMEM((N_CHIPS, SMALL_ROWS, 256), F32),
            pltpu.VMEM((TM, D_MODEL), F32),
            pltpu.VMEM((8, D_CONV), F32),
            pltpu.VMEM((rows, D_MODEL), BF16),
            pltpu.VMEM((TM, D_MODEL), F32),
            pltpu.VMEM((256, D_MODEL), F32),
            pltpu.SemaphoreType.DMA((15,)), pltpu.SemaphoreType.DMA((15,)), pltpu.SemaphoreType.DMA((7,))])
    return pl.pallas_call(
        body, name="f1_norm_inproj_gather",
        grid_spec=grid_spec,
        out_shape=[jax.ShapeDtypeStruct((rows, N_PROJ_COLS), BF16),
                   jax.ShapeDtypeStruct((nk, D_MODEL, TM), BF16),
                   jax.ShapeDtypeStruct((N_CHIPS, D_MODEL, 1024), BF16),
                   jax.ShapeDtypeStruct((N_CHIPS, 256, D_MODEL), BF16),
                   jax.ShapeDtypeStruct((TM, D_MODEL), F32),
                   jax.ShapeDtypeStruct((8, D_CONV), F32)],
        compiler_params=_cparams(("arbitrary", "arbitrary")),
    )(order, x, g1, meta_sh, convw_sh, win_sh, wout_sh)


def _f2_f3_call(proj, conv_w8, gret, tb, x, w_out, fg, target):
    nt, rows = tb["nt"], tb["rows"]
    seq = nt * TM

    def pf(s):
        return jnp.where(s == 0, nt, jnp.minimum(s - 1, nt - 1))

    def xt(s):
        return jnp.clip(s - 2, 0, nt - 1)

    def body(proj_ref, cw_ref, g_ref, tt_ref, cr_ref, sr_ref, sgn_ref, dec_ref, xi_ref, zeta_ref, cd_ref,
             x_ref, w_ref, fg_ref, t_ref,
             conv_hbm, states_hbm, dh2_ref, dmx_ref, gwo_ref, gfg_ref, loss_ref,
             state, uhalo, mxs, convs, sts, lacc, out_sems):
        s = pl.program_id(0)
        slot = lax.rem(s, 2)
        mixed_ref = mxs.at[slot]
        conv_ref = convs.at[slot]
        states_ref = sts.at[slot]

        def conv_out(sl, tile):
            return pltpu.make_async_copy(convs.at[sl], conv_hbm.at[pl.ds(pl.multiple_of(tile * TM, TM), TM), :],
                                         out_sems.at[sl])

        def states_out(sl, tile):
            return pltpu.make_async_copy(sts.at[sl], states_hbm.at[pl.ds(pl.multiple_of(tile * NCH, NCH), NCH)],
                                         out_sems.at[2 + sl])

        @pl.when(s == 0)
        def _():
            state[...] = jnp.zeros_like(state)
            uhalo[...] = jnp.zeros_like(uhalo)
            mxs[...] = jnp.zeros_like(mxs)
            gwo_ref[...] = jnp.zeros_like(gwo_ref)
            gfg_ref[...] = jnp.zeros_like(gfg_ref)
            lacc[...] = jnp.zeros_like(lacc)

        @pl.when(s >= 2)
        def _():
            conv_out(slot, pf(s - 2)).wait()
            states_out(slot, pf(s - 2)).wait()

        valid = jnp.where(s >= 2, 1.0, 0.0)
        mx_prev = mxs.at[1 - slot]
        f3 = {}

        def f3_fwd():
            f3["h2"] = x_ref[...] + _dot(mx_prev[...], w_ref[...])

        def f3_loss():
            h2 = f3.pop("h2")
            ms = jnp.mean(h2 * h2, axis=-1, keepdims=True)
            rstd = lax.rsqrt(ms + EPS)
            yh = h2 * rstd
            g = fg_ref[...]
            e = (yh * g - t_ref[...]) * valid
            lacc[...] += jnp.sum(e * e, axis=0, keepdims=True)
            dy = e * (1.0 / D_MODEL)
            gfg_ref[...] += jnp.sum(dy * yh, axis=0, keepdims=True)
            dyh = dy * g
            dh2 = rstd * (dyh - yh * jnp.mean(dyh * yh, axis=-1, keepdims=True))
            dh2_ref[...] = dh2
            f3["db"] = dh2.astype(BF16)

        def f3_dmx():
            dmx_ref[...] = _dot_tb(f3["db"], w_ref[...]).astype(BF16)

        def f3_gw():
            gw = _dot_ta(mx_prev[...], f3["db"])
            for j in range(N_CHIPS):
                for hf in range(2):
                    r0 = j * 256 + hf * 128
                    gwo_ref[hf, j] += gw[r0:r0 + 128, :]

        cx = proj_ref[:, CX:CX + 512].astype(F32)
        cc = proj_ref[:, CC:CC + 512].astype(F32)
        u = cc * cx
        row = lax.broadcasted_iota(jnp.int32, (TM, D_CONV), 0)
        h7 = uhalo[7:8, :]
        h6 = uhalo[6:7, :]
        u1 = jnp.where(row == 0, h7, pltpu.roll(u, 1, 0))
        u2 = jnp.where(row == 0, h6, jnp.where(row == 1, h7, pltpu.roll(u, 2, 0)))
        conv = cw_ref[2:3, :] * u + cw_ref[1:2, :] * u1 + cw_ref[0:1, :] * u2
        uhalo[...] = u[TM - 8:TM, :]
        cb = proj_ref[:, CB:CB + 512].astype(F32)
        cg = proj_ref[:, CG:CG + 512].astype(F32)
        mixed_ref[:, 0:D_CONV] = (cb * conv * (cg * _sigmoid(cg))).astype(BF16)
        conv_ref[...] = conv.astype(BF16)
        f3_fwd()

        scale = HEAD_DIM ** -0.5
        H = range(RET_HEADS)
        st = [state[h] for h in H]
        between = [f3_loss, f3_dmx, f3_gw, None]
        rc_t, rs_t = _tile_rotary(tt_ref, cr_ref, sr_ref, sgn_ref)
        for c in range(NCH):
            r0 = c * CHUNK
            rc = rc_t[r0:r0 + CHUNK, :]
            rs = rs_t[r0:r0 + CHUNK, :]
            col = lambda base, h: slice(base + h * HEAD_DIM, base + (h + 1) * HEAD_DIM)
            rws = slice(r0, r0 + CHUNK)
            v = [proj_ref[rws, col(CV, h)] for h in H]
            qf = [_rot(proj_ref[rws, col(CQ, h)].astype(F32), rc, rs) * scale for h in H]
            kf = [_rot(proj_ref[rws, col(CK, h)].astype(F32), rc, rs) for h in H]
            stb = [t.astype(BF16) for t in st]
            for h in H:
                states_ref[c, h] = stb[h]
            a = [(_dot_tb(qf[h].astype(BF16), kf[h].astype(BF16)) * dec_ref[h]).astype(BF16) for h in H]
            o = [_dot(a[h], v[h]) + _dot((qf[h] * xi_ref[h]).astype(BF16), stb[h]) for h in H]
            st = [cd_ref[h, 0:1, :] * st[h] + _dot_ta((kf[h] * zeta_ref[h]).astype(BF16), v[h]) for h in H]
            for h in H:
                mu = jnp.mean(o[h], axis=-1, keepdims=True)
                d = o[h] - mu
                var = jnp.mean(d * d, axis=-1, keepdims=True)
                yh = d * lax.rsqrt(var + EPS)
                rg = proj_ref[rws, col(CR, h)].astype(F32)
                mixed_ref[rws, col(D_CONV, h)] = (yh * g_ref[:, col(0, h)] * (rg * _sigmoid(rg))).astype(BF16)
            if between[c] is not None:
                between[c]()
        for h in H:
            state[h] = st[h]

        @pl.when(s <= nt)
        def _():
            conv_out(slot, pf(s)).start()
            states_out(slot, pf(s)).start()

        @pl.when(s == nt + 1)
        def _():
            conv_out(1 - slot, pf(s - 1)).wait()
            states_out(1 - slot, pf(s - 1)).wait()
            tot = jnp.sum(lacc[...], axis=1, keepdims=True) * (0.5 / D_MODEL)
            loss_ref[...] = jnp.broadcast_to(tot, (1, 128))

    tile = lambda w: pl.BlockSpec((TM, w), lambda s: (pf(s), 0))
    xtile = lambda w: pl.BlockSpec((TM, w), lambda s: (xt(s), 0))
    any_spec = pl.BlockSpec(memory_space=pl.ANY)
    return pl.pallas_call(
        body, name="f2_mixer_fwd_f3_outproj_loss",
        grid=(nt + 2,),
        in_specs=[tile(N_PROJ_COLS), _resident((8, D_CONV)), _resident((1, D_RET)),
                  pl.BlockSpec((None, 8, HEAD_DIM), lambda s: (pf(s), 0, 0)),
                  _resident((TM, HEAD_DIM)), _resident((TM, HEAD_DIM)), _resident((8, HEAD_DIM)),
                  _resident((RET_HEADS, CHUNK, CHUNK)), _resident((RET_HEADS, CHUNK, HEAD_DIM)),
                  _resident((RET_HEADS, CHUNK, HEAD_DIM)), _resident((RET_HEADS, 8, HEAD_DIM)),
                  xtile(D_MODEL), _resident1((D_MODEL, D_MODEL)), _resident((1, D_MODEL)), xtile(D_MODEL)],
        out_specs=[any_spec, any_spec, xtile(D_MODEL), xtile(D_MODEL),
                   _resident((2, N_CHIPS, 128, D_MODEL)), _resident((1, D_MODEL)), _resident((1, 128))],
        out_shape=[jax.ShapeDtypeStruct((rows, D_CONV), BF16),
                   jax.ShapeDtypeStruct(((nt + 1) * NCH, RET_HEADS, HEAD_DIM, HEAD_DIM), BF16),
                   jax.ShapeDtypeStruct((seq, D_MODEL), F32),
                   jax.ShapeDtypeStruct((seq, D_MODEL), BF16),
                   jax.ShapeDtypeStruct((2, N_CHIPS, 128, D_MODEL), F32),
                   jax.ShapeDtypeStruct((1, D_MODEL), F32),
                   jax.ShapeDtypeStruct((1, 128), F32)],
        scratch_shapes=[pltpu.VMEM((RET_HEADS, HEAD_DIM, HEAD_DIM), F32), pltpu.VMEM((8, D_CONV), F32),
                        pltpu.VMEM((2, TM, D_MODEL), BF16), pltpu.VMEM((2, TM, D_CONV), BF16),
                        pltpu.VMEM((2, NCH, RET_HEADS, HEAD_DIM, HEAD_DIM), BF16),
                        pltpu.VMEM((1, D_MODEL), F32), pltpu.SemaphoreType.DMA((4,))],
        compiler_params=_cparams(("arbitrary",)),
    )(proj, conv_w8, gret, tb["tt"], tb["cr2"], tb["sr2"], tb["sgn"], tb["decay"], tb["xi"], tb["zeta"], tb["cd"],
      x, w_out, fg, target)


def _b2_b1a_call(proj, dmixed, conv_s, states, conv_w8, gret, tb, w_in_g, x, meta_tile, g1, dh2):
    nt, rows = tb["nt"], tb["rows"]
    seq = nt * TM

    def pb(r):
        return jnp.where(r == nt, nt, nt - 1 - r)

    def xprev(r):
        return jnp.clip(nt - r, 0, nt - 1)

    def body(proj_ref, dmx_ref, conv_ref, states_ref, cw_ref, g_ref, tt_ref, cr_ref, sr_ref, sgn_ref, dec_ref,
             xi_ref, zeta_ref, cd_ref, w_ref, x_ref, mt_ref, g1_ref, dh2_ref,
             dproj_hbm, gcw_ref, gg_ref, gx_ref, dmeta_ref, gn_ref,
             gstate, dchalo, dps, out_sems):
        r = pl.program_id(0)
        live = jnp.where(r == nt, 0.0, 1.0)
        slot = lax.rem(r, 2)
        dproj_ref = dps.at[slot]

        class to_hbm:
            def __init__(self, s, tile):
                self.copies = [pltpu.make_async_copy(dps.at[s, :, j * 1024:(j + 1) * 1024], dproj_hbm.at[tile, j],
                                                     out_sems.at[N_CHIPS * s + j]) for j in range(N_CHIPS)]

            def start(self):
                for cp in self.copies:
                    cp.start()

            def wait(self):
                for cp in self.copies:
                    cp.wait()

        @pl.when(r == 0)
        def _():
            gstate[...] = jnp.zeros_like(gstate)
            dchalo[...] = jnp.zeros_like(dchalo)
            gcw_ref[...] = jnp.zeros_like(gcw_ref)
            gg_ref[...] = jnp.zeros_like(gg_ref)
            gn_ref[...] = jnp.zeros_like(gn_ref)
            dps[...] = jnp.zeros_like(dps)

        @pl.when(r >= 2)
        def _():
            to_hbm(slot, pb(r - 2)).wait()

        dprev = dps.at[1 - slot]
        pieces = []

        def emit_piece():
            j = len(pieces)
            if j < N_CHIPS:
                p = _dot_tb(dprev[:, j * 1024:(j + 1) * 1024], w_ref[j])
                pieces.append(p if j == 0 else pieces[-1] + p)

        cx = proj_ref[:, CX:CX + 512].astype(F32)
        cb = proj_ref[:, CB:CB + 512].astype(F32)
        cc = proj_ref[:, CC:CC + 512].astype(F32)
        cg = proj_ref[:, CG:CG + 512].astype(F32)
        dco = dmx_ref[:, 0:D_CONV].astype(F32) * live
        conv = conv_ref[...].astype(F32)
        sg = _sigmoid(cg)
        sil = cg * sg
        t = dco * conv
        dproj_ref[:, CB:CB + 512] = (t * sil).astype(BF16)
        dproj_ref[:, CG:CG + 512] = (t * cb * (sg * (1.0 + cg * (1.0 - sg)))).astype(BF16)
        dconv = dco * cb * sil
        row = lax.broadcasted_iota(jnp.int32, (TM, D_CONV), 0)
        n0 = dchalo[0:1, :]
        n1 = dchalo[1:2, :]
        dc1 = jnp.where(row == TM - 1, n0, pltpu.roll(dconv, TM - 1, 0))
        dc2 = jnp.where(row == TM - 2, n0, jnp.where(row == TM - 1, n1, pltpu.roll(dconv, TM - 2, 0)))
        dchalo[...] = dconv[0:8, :]
        du = cw_ref[2:3, :] * dconv + cw_ref[1:2, :] * dc1 + cw_ref[0:1, :] * dc2
        u = cc * cx
        gcw_ref[2:3, :] += jnp.sum(u * dconv, axis=0, keepdims=True)
        gcw_ref[1:2, :] += jnp.sum(u * dc1, axis=0, keepdims=True)
        gcw_ref[0:1, :] += jnp.sum(u * dc2, axis=0, keepdims=True)
        dproj_ref[:, CC:CC + 512] = (du * cx).astype(BF16)
        dproj_ref[:, CX:CX + 512] = (du * cc).astype(BF16)
        emit_piece()
        emit_piece()

        scale = HEAD_DIM ** -0.5
        gs = {h: gstate[h] for h in range(RET_HEADS)}
        gg = {h: jnp.zeros((1, HEAD_DIM), F32) for h in range(RET_HEADS)}
        col = lambda base, h: slice(base + h * HEAD_DIM, base + (h + 1) * HEAD_DIM)
        rw = lambda c: slice(c * CHUNK, (c + 1) * CHUNK)
        rc_t, rs_t = _tile_rotary(tt_ref, cr_ref, sr_ref, sgn_ref)
        for c0 in range(NCH - CHUNK_GROUP, -1, -CHUNK_GROUP):
            cs = range(c0 + CHUNK_GROUP - 1, c0 - 1, -1)
            U = [(c, h) for c in cs for h in range(RET_HEADS)]
            rc = {c: rc_t[rw(c), :] for c in cs}
            rs = {c: rs_t[rw(c), :] for c in cs}
            v = {(c, h): proj_ref[rw(c), col(CV, h)] for c, h in U}
            stb = {(c, h): states_ref[c, h] for c, h in U}
            qf = {(c, h): _rot(proj_ref[rw(c), col(CQ, h)].astype(F32), rc[c], rs[c]) * scale for c, h in U}
            kf = {(c, h): _rot(proj_ref[rw(c), col(CK, h)].astype(F32), rc[c], rs[c]) for c, h in U}
            qb = {u: qf[u].astype(BF16) for u in U}
            kb = {u: kf[u].astype(BF16) for u in U}
            qxb = {(c, h): (qf[c, h] * xi_ref[h]).astype(BF16) for c, h in U}
            kzb = {(c, h): (kf[c, h] * zeta_ref[h]).astype(BF16) for c, h in U}
            ab = {(c, h): (_dot_tb(qb[c, h], kb[c, h]) * dec_ref[h]).astype(BF16) for c, h in U}
            o = {u: _dot(ab[u], v[u]) + _dot(qxb[u], stb[u]) for u in U}
            emit_piece()
            dob = {}
            for c, h in U:
                mu = jnp.mean(o[c, h], axis=-1, keepdims=True)
                d = o[c, h] - mu
                var = jnp.mean(d * d, axis=-1, keepdims=True)
                rstd = lax.rsqrt(var + EPS)
                yh = d * rstd
                g = g_ref[:, col(0, h)]
                rg = proj_ref[rw(c), col(CR, h)].astype(F32)
                dro = dmx_ref[rw(c), col(D_CONV, h)].astype(F32) * live
                sg = _sigmoid(rg)
                dproj_ref[rw(c), col(CR, h)] = (dro * (yh * g) * (sg * (1.0 + rg * (1.0 - sg)))).astype(BF16)
                dret = dro * (rg * sg)
                gg[h] = gg[h] + jnp.sum(dret * yh, axis=0, keepdims=True)
                dyh = dret * g
                do = rstd * (dyh - jnp.mean(dyh, axis=-1, keepdims=True)
                             - yh * jnp.mean(dyh * yh, axis=-1, keepdims=True))
                dob[c, h] = do.astype(BF16)
            dv1 = {u: _dot_ta(ab[u], dob[u]) for u in U}
            ds = {(c, h): (_dot_tb(dob[c, h], v[c, h]) * dec_ref[h]).astype(BF16) for c, h in U}
            gup = {u: _dot_ta(qxb[u], dob[u]) for u in U}
            dq = {(c, h): _dot(ds[c, h], kb[c, h]) + _dot_tb(dob[c, h], stb[c, h]) * xi_ref[h] for c, h in U}
            dk1 = {u: _dot_ta(ds[u], qb[u]) for u in U}
            emit_piece()
            for c, h in U:
                gsb = gs[h].astype(BF16)
                dv = dv1[c, h] + _dot(kzb[c, h], gsb)
                dk = dk1[c, h] + _dot_tb(v[c, h], gsb) * zeta_ref[h]
                gs[h] = cd_ref[h, 0:1, :] * gs[h] + gup[c, h]
                dproj_ref[rw(c), col(CQ, h)] = (_rot_t(dq[c, h], rc[c], rs[c]) * scale).astype(BF16)
                dproj_ref[rw(c), col(CK, h)] = _rot_t(dk, rc[c], rs[c]).astype(BF16)
                dproj_ref[rw(c), col(CV, h)] = dv.astype(BF16)
        for h in range(RET_HEADS):
            gstate[h] = gs[h]
            gg_ref[:, col(0, h)] += gg[h]

        while len(pieces) < N_CHIPS:
            emit_piece()

        def norm_bwd(dhn, hx):
            ms = jnp.mean(hx * hx, axis=-1, keepdims=True)
            rstd1 = lax.rsqrt(ms + EPS)
            xh = hx * rstd1
            gn_ref[...] += jnp.sum(dhn * xh, axis=0, keepdims=True)
            dxh = dhn * g1_ref[...]
            return rstd1 * (dxh - xh * jnp.mean(dxh * xh, axis=-1, keepdims=True))

        gx_ref[...] = norm_bwd(pieces[-1], x_ref[...]) + dh2_ref[...]

        @pl.when(r < nt)
        def _():
            to_hbm(slot, pb(r)).start()

        @pl.when(r == nt)
        def _():
            to_hbm(slot, pb(r)).start()
            mrows = slice(TM - N_META, TM)
            d16 = dproj_ref[mrows, :]
            dhn16 = _dot_tb(d16[:, 0:1024], w_ref[0])
            for j in range(1, N_CHIPS):
                dhn16 += _dot_tb(d16[:, j * 1024:(j + 1) * 1024], w_ref[j])
            dmeta_ref[...] = norm_bwd(dhn16, mt_ref[mrows, :])
            to_hbm(1 - slot, pb(r - 1)).wait()
            to_hbm(slot, pb(r)).wait()

    tile = lambda w: pl.BlockSpec((TM, w), lambda r: (pb(r), 0))
    xtile = pl.BlockSpec((TM, D_MODEL), lambda r: (xprev(r), 0))
    return pl.pallas_call(
        body, name="b2_mixer_bwd_b1a_inproj_bwd_x",
        grid=(nt + 1,),
        in_specs=[tile(N_PROJ_COLS),
                  pl.BlockSpec((TM, D_MODEL), lambda r: (jnp.minimum(pb(r), nt - 1), 0)),
                  tile(D_CONV),
                  pl.BlockSpec((NCH, RET_HEADS, HEAD_DIM, HEAD_DIM), lambda r: (pb(r), 0, 0, 0)),
                  _resident((8, D_CONV)), _resident((1, D_RET)),
                  pl.BlockSpec((None, 8, HEAD_DIM), lambda r: (pb(r), 0, 0)),
                  _resident((TM, HEAD_DIM)), _resident((TM, HEAD_DIM)), _resident((8, HEAD_DIM)),
                  _resident((RET_HEADS, CHUNK, CHUNK)),
                  _resident((RET_HEADS, CHUNK, HEAD_DIM)),
                  _resident((RET_HEADS, CHUNK, HEAD_DIM)), _resident((RET_HEADS, 8, HEAD_DIM)),
                  _resident1((N_CHIPS, D_MODEL, 1024)), xtile, _resident1((TM, D_MODEL)), _resident((1, D_MODEL)),
                  xtile],
        out_specs=[pl.BlockSpec(memory_space=pl.ANY), _resident((8, D_CONV)), _resident((1, D_RET)),
                   xtile, _resident((N_META, D_MODEL)), _resident((1, D_MODEL))],
        out_shape=[jax.ShapeDtypeStruct((nt + 1, N_CHIPS, TM, 1024), BF16),
                   jax.ShapeDtypeStruct((8, D_CONV), F32),
                   jax.ShapeDtypeStruct((1, D_RET), F32),
                   jax.ShapeDtypeStruct((seq, D_MODEL), F32),
                   jax.ShapeDtypeStruct((N_META, D_MODEL), F32),
                   jax.ShapeDtypeStruct((1, D_MODEL), F32)],
        scratch_shapes=[pltpu.VMEM((RET_HEADS, HEAD_DIM, HEAD_DIM), F32), pltpu.VMEM((8, D_CONV), F32),
                        pltpu.VMEM((2, TM, N_PROJ_COLS), BF16), pltpu.SemaphoreType.DMA((2 * N_CHIPS,))],
        compiler_params=_cparams(("arbitrary",), vmem=VMEM_LIMIT_MAX),
    )(proj, dmixed, conv_s, states, conv_w8, gret, tb["tt"], tb["cr2"], tb["sr2"], tb["sgn"], tb["decay"],
      tb["xi"], tb["zeta"], tb["cd"], w_in_g, x, meta_tile, g1, dh2)


REL = (2, 1, 3)
SMALL_ROWS = 24
HALF_STEP = 4


def _rcopy(src, dst, send_sems, recv_sems, k, to):
    return pltpu.make_async_remote_copy(src_ref=src, dst_ref=dst, send_sem=send_sems.at[k],
                                        recv_sem=recv_sems.at[k], device_id=to, device_id_type=MESH_ID)


def _b1b_reduce_call(order, hnt, dproj, gwo, small, nt):
    nk = nt + 1
    last = nk - 1
    any_spec = pl.BlockSpec(memory_space=pl.ANY)

    def body(order_ref, a_ref, b_ref, gm_ref, gn1_ref, gfg_ref, ggr_ref, gcw_ref, ls_ref, gwo_hbm,
             gwin_hbm, gwout_hbm, tot_hbm,
             acc, sb, abuf, pb, bbuf, fin, go, ao, pbo, bo, fino, slots, totv, send_sems, recv_sems, loc_sems):
        jj, k = pl.program_id(0), pl.program_id(1)
        x, y, c = lax.axis_index("x"), lax.axis_index("y"), lax.axis_index("c")
        me, myid, sib = 2 * x + y, 4 * x + 2 * y + c, (x, y, 1 - c)
        rc = functools.partial(_rcopy, send_sems=send_sems, recv_sems=recv_sems)
        peers = [((1 - x) if r & 2 else x, (1 - y) if r & 1 else y, c) for r in REL]
        kids = [jnp.bitwise_xor(me, r) for r in REL]

        def dev_peer(r):
            return ((1 - x) if r & 4 else x, (1 - y) if r & 2 else y, (1 - c) if r & 1 else c)

        own_go = pltpu.make_async_copy(gwo_hbm.at[c], go, loc_sems.at[0])
        wo_half = rc(gwo_hbm.at[1 - c], ao, k=8, to=sib)
        wo_part = [rc(pbo.at[kids[p]], bo.at[p], k=9 + p, to=peers[p]) for p in range(3)]
        sm = [rc(slots.at[0], slots.at[r], k=12 + r, to=dev_peer(r)) for r in range(1, N_DEV)]
        half = [rc(sb.at[j % 2, 1 - c], abuf.at[j], k=j, to=sib) for j in range(N_CHIPS)]
        part = [rc(pb.at[p], bbuf.at[p], k=4 + p, to=peers[p]) for p in range(3)]

        @pl.when(jnp.logical_and(jj == 0, k == 0))
        def _():
            own_go.start()
            slots[0, 0:N_META, :] = gm_ref[...]
            slots[0, N_META:SMALL_ROWS, :] = jnp.zeros((SMALL_ROWS - N_META, D_MODEL), F32)
            slots[0, N_META:N_META + 1, :] = gn1_ref[...]
            slots[0, N_META + 1:N_META + 2, :] = gfg_ref[...]
            slots[0, N_META + 2:N_META + 3, 0:D_RET] = ggr_ref[...]
            slots[0, N_META + 3:N_META + 6, 0:D_CONV] = gcw_ref[0:3, :]
            slots[0, N_META + 6:N_META + 7, 0:128] = ls_ref[...]
            wo_half.start()
            for cp in sm:
                cp.start()

        @pl.when(k == 0)
        def _():
            acc[...] = jnp.zeros_like(acc)

        acc[0] += _dot(a_ref[0:512, :], b_ref[...])
        acc[1] += _dot(a_ref[512:1024, :], b_ref[...])

        @pl.when(k == HALF_STEP)
        def _():
            @pl.when(jj == 0)
            def _():
                own_go.wait()
                wo_half.wait_recv()
                for j in range(N_CHIPS):
                    go[j] = go[j] + ao[j]
                pbo[...] = go[...].astype(BF16)
                for cp in wo_part:
                    cp.start()

            for p in range(3):
                @pl.when(jj == p + 1)
                def _(p=p):
                    half[p].wait_recv()
                    half[p].wait_send()
                    pb[p] = (sb[p % 2, c] + abuf[p]).astype(BF16)
                    part[p].start()

        @pl.when(k == last)
        def _():
            for j in range(N_CHIPS):
                @pl.when(jj == j)
                def _(j=j):
                    sb[j % 2] = acc[...]
                    half[j].start()

        @pl.when(jnp.logical_and(jj == N_CHIPS - 1, k == last))
        def _():
            half[3].wait_recv()
            own = sb[1, c] + abuf[3]
            for cp in part:
                cp.wait_recv()
            fin[c] = ((own + bbuf[0].astype(F32)) + bbuf[1].astype(F32)) + bbuf[2].astype(F32)
            done = rc(fin.at[c], fin.at[c], k=7, to=sib)
            done.start()
            for cp in wo_part:
                cp.wait_recv()
            fino[c] = ((go[me] + bo[0].astype(F32)) + bo[1].astype(F32)) + bo[2].astype(F32)
            done_o = rc(fino.at[c], fino.at[c], k=12, to=sib)
            done_o.start()
            for cp in sm:
                cp.wait_recv()
            tot = slots[myid]
            for a in range(1, N_DEV):
                tot = tot + slots[jnp.bitwise_xor(myid, a)]
            totv[...] = tot
            out_t = pltpu.make_async_copy(totv, tot_hbm, loc_sems.at[1])
            out_t.start()
            rc(fin.at[1 - c], fin.at[1 - c], k=7, to=sib).wait_recv()
            out_w = pltpu.make_async_copy(fin, gwin_hbm, loc_sems.at[0])
            out_w.start()
            rc(fino.at[1 - c], fino.at[1 - c], k=12, to=sib).wait_recv()
            out_o = pltpu.make_async_copy(fino, gwout_hbm, loc_sems.at[2])
            out_o.start()
            for cp in [half[3]] + part + [done, wo_half] + wo_part + [done_o] + sm:
                cp.wait_send()
            out_t.wait()
            out_w.wait()
            out_o.wait()

    grid_spec = pltpu.PrefetchScalarGridSpec(
        num_scalar_prefetch=1,
        grid=(N_CHIPS, nk),
        in_specs=[pl.BlockSpec((None, D_MODEL, TM), lambda j, k, o: (k, 0, 0)),
                  pl.BlockSpec((None, None, TM, 1024), lambda j, k, o: (k, o[j], 0, 0))]
                 + [pl.BlockSpec(s.shape, lambda j, k, o: (0, 0)) for s in small] + [any_spec],
        out_specs=[any_spec, any_spec, any_spec],
        scratch_shapes=[
            pltpu.VMEM((2, 512, 1024), F32),
            pltpu.VMEM((2, 2, 512, 1024), F32),
            pltpu.VMEM((N_CHIPS, 512, 1024), F32),
            pltpu.VMEM((3, 512, 1024), BF16),
            pltpu.VMEM((3, 512, 1024), BF16),
            pltpu.VMEM((2, 512, 1024), F32),
            pltpu.VMEM((N_CHIPS, 128, D_MODEL), F32),
            pltpu.VMEM((N_CHIPS, 128, D_MODEL), F32),
            pltpu.VMEM((N_CHIPS, 128, D_MODEL), BF16),
            pltpu.VMEM((3, 128, D_MODEL), BF16),
            pltpu.VMEM((2, 128, D_MODEL), F32),
            pltpu.VMEM((N_DEV, SMALL_ROWS, D_MODEL), F32),
            pltpu.VMEM((SMALL_ROWS, D_MODEL), F32),
            pltpu.SemaphoreType.DMA((20,)), pltpu.SemaphoreType.DMA((20,)), pltpu.SemaphoreType.DMA((3,))])
    return pl.pallas_call(
        body, name="b1b_inproj_bwd_w_reduce",
        grid_spec=grid_spec,
        out_shape=[jax.ShapeDtypeStruct((2, 512, 1024), F32),
                   jax.ShapeDtypeStruct((2, 128, D_MODEL), F32),
                   jax.ShapeDtypeStruct((SMALL_ROWS, D_MODEL), F32)],
        compiler_params=_cparams(("arbitrary", "arbitrary")),
    )(order, hnt, dproj, *small, gwo)


def _local_step(me, x, target, g1, gret, fg, win_sh, wout_sh, meta_sh, convw_sh):
    seq = x.shape[0]
    tb = _tables(seq)
    nt = tb["nt"]
    g1r, gretr, fgr = g1.reshape(1, -1), gret.reshape(1, -1), fg.reshape(1, -1)
    order = jnp.stack([me, me ^ REL[0], me ^ REL[1], me ^ REL[2]]).astype(jnp.int32)

    proj, hnt, w_in_g, w_out_g, meta_tile, conv_w8 = _f1_gather_call(order, x, g1r, win_sh, wout_sh, meta_sh,
                                                                     convw_sh, nt)
    w_out = w_out_g.reshape(D_MODEL, D_MODEL)
    conv_s, states, dh2, dmixed, g_wout, g_fg, loss = _f2_f3_call(proj, conv_w8, gretr, tb, x, w_out, fgr, target)
    dproj, g_cw8, g_gret, grad_x, g_meta, g_g1 = _b2_b1a_call(proj, dmixed, conv_s, states, conv_w8, gretr, tb,
                                                              w_in_g, x, meta_tile, g1r, dh2)
    return grad_x, g_wout, (g_meta, g_g1, g_fg, g_gret, g_cw8, loss), hnt, dproj


def _adamw_update(w_ref, g_ref, m_ref, v_ref, d_ref, nm_ref, nv_ref):
    gg = g_ref[...]
    nm = ADAM_B1 * m_ref[...] + (1.0 - ADAM_B1) * gg
    nv = ADAM_B2 * v_ref[...] + (1.0 - ADAM_B2) * (gg * gg)
    m_hat = nm / (1.0 - ADAM_B1 ** ADAM_STEP)
    v_hat = nv / (1.0 - ADAM_B2 ** ADAM_STEP)
    d_ref[...] = -ADAM_LR * (m_hat / (jnp.sqrt(v_hat) + ADAM_EPS) + ADAM_WD * w_ref[...])
    nm_ref[...] = nm
    nv_ref[...] = nv


def _adamw_small_call(me, tot, ws, ms, vs):
    n = len(ws)

    def body(me_ref, tmeta_ref, tvec_ref, tconv_ref, *refs):
        ins, outs = refs[:3 * n], refs[3 * n:]
        g_refs, loss_ref, upd = outs[0:n], outs[n], outs[n + 1:]
        g_refs[0][...] = tmeta_ref[...]
        g_refs[1][...] = tvec_ref[0:1, :]
        g_refs[2][...] = tconv_ref[3:6, :]
        g_refs[3][...] = tvec_ref[2:3, 0:D_RET]
        g_refs[4][...] = tvec_ref[1:2, :]
        loss_ref[...] = tvec_ref[6:7, 0:1]
        for i in range(n):
            _adamw_update(ins[i], g_refs[i], ins[n + i], ins[2 * n + i], upd[i], upd[n + i], upd[2 * n + i])

    whole = lambda a: pl.BlockSpec(a.shape, lambda i, m: (0,) * a.ndim)
    shapes = [jax.ShapeDtypeStruct(w.shape, F32) for w in ws]
    out_shape = shapes + [jax.ShapeDtypeStruct((1, 1), F32)] + shapes * 3
    grid_spec = pltpu.PrefetchScalarGridSpec(
        num_scalar_prefetch=1, grid=(1,),
        in_specs=[pl.BlockSpec((N_META, 256), lambda i, m: (0, m[0])),
                  pl.BlockSpec((8, D_MODEL), lambda i, m: (N_META // 8, 0)),
                  pl.BlockSpec((8, 128), lambda i, m: (N_META // 8, m[0]))] + [whole(a) for a in ws + ms + vs],
        out_specs=[whole(s) for s in out_shape])
    outs = pl.pallas_call(body, name="adamw_small", grid_spec=grid_spec, out_shape=out_shape,
                          compiler_params=_cparams(("arbitrary",)))(me.reshape(1), tot, tot, tot, *ws, *ms, *vs)
    return outs[:n], outs[n], outs[n + 1:2 * n + 1], outs[2 * n + 1:3 * n + 1], outs[3 * n + 1:]


def _adamw_big_call(a4, b4):
    br = 256
    na = a4[0].shape[0] // br
    assert all(t.shape == (na * br, 1024) for t in a4) and all(t.shape == (br, 1024) for t in b4)

    def body(*refs):
        i = pl.program_id(0)
        a_in, b_in, a_out, b_out = refs[0:4], refs[4:8], refs[8:11], refs[11:14]

        @pl.when(i < na)
        def _():
            _adamw_update(*a_in, *a_out)

        @pl.when(i == na)
        def _():
            _adamw_update(*b_in, *b_out)

    spec_a = pl.BlockSpec((br, 1024), lambda i: (jnp.minimum(i, na - 1), 0))
    spec_b = pl.BlockSpec((br, 1024), lambda i: (0, 0))
    outs = pl.pallas_call(
        body, name="adamw_w_in_w_out", grid=(na + 1,),
        in_specs=[spec_a] * 4 + [spec_b] * 4, out_specs=[spec_a] * 3 + [spec_b] * 3,
        out_shape=[jax.ShapeDtypeStruct(a4[0].shape, F32)] * 3 + [jax.ShapeDtypeStruct(b4[0].shape, F32)] * 3,
        compiler_params=_cparams(("arbitrary",)),
    )(*a4, *b4)
    return outs[0:3], outs[3:6]


def kernel(x, meta, norm1_g, w_in, conv_w, ret_norm_g, w_out, final_g, loss_target, m_meta, m_norm1_g, m_w_in, m_conv_w, m_ret_norm_g, m_w_out, m_final_g, v_meta, v_norm1_g, v_w_in, v_conv_w, v_ret_norm_g, v_w_out, v_final_g):
    me = 2 * lax.axis_index("x") + lax.axis_index("y")

    grad_x, g_wo, small, hnt, dproj = _local_step(me, x[0], loss_target[0], norm1_g, ret_norm_g, final_g,
                                                  w_in, w_out, meta, conv_w)

    order = jnp.stack([me ^ REL[0], me ^ REL[1], me ^ REL[2], me]).astype(jnp.int32)
    g_win, g_wout, tot = _b1b_reduce_call(order, hnt, dproj, g_wo, small, x.shape[1] // TM)
    g_win, g_wout = g_win.reshape(D_MODEL, 1024), g_wout.reshape(256, D_MODEL)

    ws = [meta, norm1_g, w_in, conv_w, ret_norm_g, w_out, final_g]
    ms = [m_meta, m_norm1_g, m_w_in, m_conv_w, m_ret_norm_g, m_w_out, m_final_g]
    vs = [v_meta, v_norm1_g, v_w_in, v_conv_w, v_ret_norm_g, v_w_out, v_final_g]
    names = ["meta", "norm1_g", "w_in", "conv_w", "ret_norm_g", "w_out", "final_g"]
    as2d = lambda a: a.reshape(1, -1) if a.ndim == 1 else a
    big = {names.index("w_in"): g_win, names.index("w_out"): g_wout}
    small = [i for i in range(len(names)) if i not in big]
    grads, deltas, new_ms, new_vs = [None] * 7, [None] * 7, [None] * 7, [None] * 7
    ia, ib = names.index("w_in"), names.index("w_out")
    upd = _adamw_big_call(*[(ws[i], big[i], ms[i], vs[i]) for i in (ia, ib)])
    for i, u in zip((ia, ib), upd):
        grads[i] = big[i]
        deltas[i], new_ms[i], new_vs[i] = u
    sg, loss_tot, sd, sm_, sv = _adamw_small_call(me.astype(jnp.int32), tot,
                                                  *[[as2d(t[i]) for i in small] for t in (ws, ms, vs)])
    for j, i in enumerate(small):
        grads[i], deltas[i], new_ms[i], new_vs[i] = (o[j].reshape(ws[i].shape) for o in (sg, sd, sm_, sv))
    return (loss_tot.reshape(()), grad_x[None], *grads, *deltas, *new_ms, *new_vs)
```

```python
import functools

import jax
import jax.numpy as jnp
import numpy as np
from jax import lax
from jax.experimental import pallas as pl
from jax.experimental.pallas import tpu as pltpu

F32 = jnp.float32
BF16 = jnp.bfloat16

D_MODEL = 1024
N_META = 16
D_CONV = 512
D_RET = 512
RET_HEADS = 4
HEAD_DIM = 128
CHUNK = 128
N_PROJ_COLS = 4096
ROPE_BASE = 10000.0
EPS = 1e-6
N_CHIPS = 4
N_DEV = 8

ADAM_LR = 0.001
ADAM_B1 = 0.9
ADAM_B2 = 0.999
ADAM_EPS = 1e-08
ADAM_WD = 0.01
ADAM_STEP = 10

TM = 512
NCH = TM // CHUNK
CHUNK_GROUP = 2
VMEM_LIMIT = 56 * 1024 * 1024
VMEM_LIMIT_MAX = 63 * 1024 * 1024

CX, CB, CC, CG, CQ, CK, CV, CR = (i * 512 for i in range(8))

MESH_ID = pl.DeviceIdType.MESH


def _cparams(sem=None, vmem=VMEM_LIMIT, **kw):
    return pltpu.CompilerParams(dimension_semantics=sem, vmem_limit_bytes=vmem, **kw)


def _sigmoid(x):
    return 1.0 / (1.0 + jnp.exp(-x))


def _dot(a, b):
    return jnp.dot(a, b, preferred_element_type=F32)


def _dot_tb(a, b):
    return lax.dot_general(a, b, (((1,), (1,)), ((), ())), preferred_element_type=F32)


def _dot_ta(a, b):
    return lax.dot_general(a, b, (((0,), (0,)), ((), ())), preferred_element_type=F32)


def _resident(shape):
    nd = len(shape)
    return pl.BlockSpec(shape, lambda *_: (0,) * nd)


def _resident1(shape):
    nd = len(shape)
    return pl.BlockSpec(shape, lambda *_: (0,) * nd, pipeline_mode=pl.Buffered(1))


def _tables(seq):
    f32 = np.float32
    nt = seq // TM
    rows = seq + TM
    half = HEAD_DIM // 2
    freqs = (f32(1.0) / (f32(ROPE_BASE) ** (np.arange(half, dtype=f32) / f32(half)))).astype(f32)
    tile_start = np.concatenate([np.arange(nt, dtype=f32), -np.ones((1,), f32)]) * f32(TM)
    ang_t = tile_start[:, None] * freqs[None, :]
    ang_r = (np.arange(TM, dtype=f32) + f32(N_META))[:, None] * freqs[None, :]
    dup = lambda a: np.concatenate([a, a], axis=-1).astype(f32)
    tt = np.stack([dup(np.cos(ang_t)), dup(np.sin(ang_t))], axis=1)
    tt = np.pad(tt, ((0, 0), (0, 6), (0, 0)))
    cr2, sr2 = dup(np.cos(ang_r)), dup(np.sin(ang_r))
    sgn = np.concatenate([-np.ones((8, half), f32), np.ones((8, half), f32)], axis=-1)
    log_g = np.log(f32(1.0) - f32(2.0) ** (f32(-5.0) - np.arange(RET_HEADS, dtype=f32))).astype(f32)
    idx = np.arange(CHUNK, dtype=f32)
    diff = idx[:, None] - idx[None, :]
    decay = np.where(diff[None] >= 0, np.exp(diff[None] * log_g[:, None, None]), f32(0.0)).astype(f32)
    zeta = np.exp((f32(CHUNK - 1) - idx)[None, :] * log_g[:, None]).astype(f32)
    xi = np.exp((idx + f32(1.0))[None, :] * log_g[:, None]).astype(f32)
    cd = np.exp(f32(CHUNK) * log_g).astype(f32)
    zeta_b = np.broadcast_to(zeta[:, :, None], (RET_HEADS, CHUNK, HEAD_DIM))
    xi_b = np.broadcast_to(xi[:, :, None], (RET_HEADS, CHUNK, HEAD_DIM))
    cd_b = np.broadcast_to(cd[:, None, None], (RET_HEADS, 8, HEAD_DIM))
    tables = dict(tt=tt, cr2=cr2, sr2=sr2, sgn=sgn, decay=decay, zeta=zeta_b, xi=xi_b, cd=cd_b)
    return dict(nt=nt, rows=rows, **{k: jnp.asarray(np.ascontiguousarray(v, dtype=f32)) for k, v in tables.items()})


def _tile_rotary(tt_ref, cr_ref, sr_ref, sgn_ref):
    ct, st = tt_ref[0:1, :], tt_ref[1:2, :]
    cr, sr = cr_ref[...], sr_ref[...]
    return ct * cr - st * sr, (st * cr + ct * sr) * sgn_ref[0:1, :]


def _rot(t, rc, rs):
    return t * rc + pltpu.roll(t, HEAD_DIM // 2, 1) * rs


def _rot_t(dt, rc, rs):
    return dt * rc + pltpu.roll(dt * rs, HEAD_DIM // 2, 1)


def _f1_gather_call(order, x, g1, win_sh, wout_sh, meta_sh, convw_sh, nt):
    nk = nt + 1
    rows = nk * TM
    any_spec = pl.BlockSpec(memory_space=pl.ANY)

    def body(order_ref, x_ref, g_ref, meta_ref, convw_ref, win_hbm, wout_hbm,
             proj_ref, hnt_ref, wg_hbm, wog_hbm, mt_hbm, cw_hbm,
             wg, wog, smg, mt, cw, hbs, st2, st3, send_sems, recv_sems, loc_sems):
        jj, k = pl.program_id(0), pl.program_id(1)
        x, y, c = lax.axis_index("x"), lax.axis_index("y"), lax.axis_index("c")
        me, sib = 2 * x + y, (x, y, 1 - c)
        rc = functools.partial(_rcopy, send_sems=send_sems, recv_sems=recv_sems)
        peers = [((1 - x) if r & 2 else x, (1 - y) if r & 1 else y, c) for r in REL]
        kids = [jnp.bitwise_xor(me, r) for r in REL]
        hw, ho = pl.ds(pl.multiple_of(c * 512, 512), 512), pl.ds(pl.multiple_of(c * 128, 128), 128)
        hw2 = pl.ds(pl.multiple_of((1 - c) * 512, 512), 512)
        ho2 = pl.ds(pl.multiple_of((1 - c) * 128, 128), 128)
        at = lambda j_, k_: jnp.logical_and(jj == j_, k == k_)

        sm_cp = [rc(smg.at[me], smg.at[me], k=p, to=peers[p]) for p in range(3)]
        win_cp = [rc(wg.at[me, hw], wg.at[me, hw], k=3 + p, to=peers[p]) for p in range(3)]
        wout_cp = [rc(wog.at[me, ho], wog.at[me, ho], k=6 + p, to=peers[p]) for p in range(3)]
        sm_in = [rc(smg.at[me], smg.at[kids[p]], k=p, to=sib) for p in range(3)]
        win_in = [rc(wg.at[me, hw], wg.at[kids[p], hw], k=3 + p, to=sib) for p in range(3)]
        wout_in = [rc(wog.at[me, ho], wog.at[kids[p], ho], k=6 + p, to=sib) for p in range(3)]
        win_fw = [rc(wg.at[kids[p], hw], wg.at[kids[p], hw], k=9 + p, to=sib) for p in range(3)]
        wout_fw = [rc(wog.at[kids[p], ho], wog.at[kids[p], ho], k=12 + p, to=sib) for p in range(3)]
        win_fw_in = [rc(wg.at[kids[p], hw2], wg.at[kids[p], hw2], k=9 + p, to=sib) for p in range(3)]
        wout_fw_in = [rc(wog.at[kids[p], ho2], wog.at[kids[p], ho2], k=12 + p, to=sib) for p in range(3)]

        def stage_own():
            cp_a = pltpu.make_async_copy(win_hbm.at[hw], mt, loc_sems.at[0])
            cp_b = pltpu.make_async_copy(win_hbm.at[hw2], st2, loc_sems.at[1])
            cp_c = pltpu.make_async_copy(wout_hbm, st3, loc_sems.at[2])
            cp_a.start()
            cp_c.start()
            cp_b.start()
            cp_a.wait()
            wg[me, hw, :] = mt[...].astype(BF16)
            for s_ in win_cp:
                s_.start()
            cp_c.wait()
            wog[me] = st3[...].astype(BF16)
            for s_ in wout_cp:
                s_.start()
            cp_b.wait()
            wg[me, hw2, :] = st2[...].astype(BF16)
        out_wg = pltpu.make_async_copy(wg, wg_hbm, loc_sems.at[3])
        out_wog = pltpu.make_async_copy(wog, wog_hbm, loc_sems.at[4])
        out_mt = pltpu.make_async_copy(mt, mt_hbm, loc_sems.at[5])
        out_cw = pltpu.make_async_copy(cw, cw_hbm, loc_sems.at[6])

        def pass_on(p):
            win_in[p].wait_recv()
            win_fw[p].start()

        @pl.when(k <= 1)
        def _():
            @pl.when(at(0, 0))
            def _():
                smg[me] = jnp.zeros((SMALL_ROWS, 256), F32)
                smg[me, 0:N_META, :] = meta_ref[...]
                smg[me, N_META:N_META + 3, 0:128] = convw_ref[...]
                for cp in sm_cp:
                    cp.start()
                stage_own()

            for p in range(3):
                @pl.when(at(p + 1, 0))
                def _(p=p):
                    win_fw_in[p].wait_recv()

            @pl.when(at(1, 1))
            def _():
                pass_on(1)

            @pl.when(at(3, 0))
            def _():
                out_wg.start()

            @pl.when(at(3, 1))
            def _():
                for p in range(3):
                    wout_in[p].wait_recv()
                    wout_fw[p].start()

        @pl.when(jnp.logical_and(jj == 0, k >= nk - 2))
        def _():
            @pl.when(k == nk - 2)
            def _():
                for cp in sm_in:
                    cp.wait_recv()
                mt[...] = jnp.zeros_like(mt)
                cw[...] = jnp.zeros_like(cw)
                for j in range(N_CHIPS):
                    mt[TM - N_META:TM, j * 256:(j + 1) * 256] = smg[j, 0:N_META, :]
                    cw[0:3, j * 128:(j + 1) * 128] = smg[j, N_META:N_META + 3, 0:128]
                out_mt.start()
                out_cw.start()

            @pl.when(k == nk - 1)
            def _():
                pass_on(0)

        @pl.when(at(2, nk // 2))
        def _():
            pass_on(2)

        tile_rows = pl.ds(pl.multiple_of(k * TM, TM), TM)

        @pl.when(jj == 0)
        def _():
            h = jnp.where(k == nt, mt[...], x_ref[...])
            ms = jnp.mean(h * h, axis=-1, keepdims=True)
            hn = (h * lax.rsqrt(ms + EPS)) * g_ref[...]
            hb = hn.astype(BF16)
            hbs[tile_rows, :] = hb
            proj_ref[...] = _dot(hb, wg[order_ref[0]]).astype(BF16)
            hnt_ref[...] = hn.T.astype(BF16)

        @pl.when(jj > 0)
        def _():
            proj_ref[...] = _dot(hbs[tile_rows, :], wg[order_ref[jj]]).astype(BF16)

        @pl.when(at(3, nk - 1))
        def _():
            for cp in wout_fw_in:
                cp.wait_recv()
            out_wog.start()
            for cp in sm_cp + win_cp + wout_cp + win_fw + wout_fw:
                cp.wait_send()
            for cp in (out_wg, out_wog, out_mt, out_cw):
                cp.wait()

    grid_spec = pltpu.PrefetchScalarGridSpec(
        num_scalar_prefetch=1,
        grid=(N_CHIPS, nk),
        in_specs=[pl.BlockSpec((TM, D_MODEL), lambda j, k, o: (jnp.where(j == 0, jnp.minimum(k, nt - 1), nt - 1), 0)),
                  pl.BlockSpec((1, D_MODEL), lambda j, k, o: (0, 0)),
                  pl.BlockSpec((N_META, 256), lambda j, k, o: (0, 0)),
                  pl.BlockSpec((3, 128), lambda j, k, o: (0, 0)),
                  any_spec, any_spec],
        out_specs=[pl.BlockSpec((TM, 1024), lambda j, k, o: (k, o[j])),
                   pl.BlockSpec((None, D_MODEL, TM), lambda j, k, o: (jnp.where(j == 0, k, nk - 1), 0, 0)),
                   any_spec, any_spec, any_spec, any_spec],
        scratch_shapes=[
            pltpu.VMEM((N_CHIPS, D_MODEL, 1024), BF16),
            pltpu.VMEM((N_CHIPS, 256, D_MODEL), BF16),
            pltpu.VMEM((N_CHIPS, SMALL_ROWS, 256), F32),
            pltpu.VMEM((TM, D_MODEL), F32),
            pltpu.VMEM((8, D_CONV), F32),
            pltpu.VMEM((rows, D_MODEL), BF16),
            pltpu.VMEM((TM, D_MODEL), F32),
            pltpu.VMEM((256, D_MODEL), F32),
            pltpu.SemaphoreType.DMA((15,)), pltpu.SemaphoreType.DMA((15,)), pltpu.SemaphoreType.DMA((7,))])
    return pl.pallas_call(
        body, name="f1_norm_inproj_gather",
        grid_spec=grid_spec,
        out_shape=[jax.ShapeDtypeStruct((rows, N_PROJ_COLS), BF16),
                   jax.ShapeDtypeStruct((nk, D_MODEL, TM), BF16),
                   jax.ShapeDtypeStruct((N_CHIPS, D_MODEL, 1024), BF16),
                   jax.ShapeDtypeStruct((N_CHIPS, 256, D_MODEL), BF16),
                   jax.ShapeDtypeStruct((TM, D_MODEL), F32),
                   jax.ShapeDtypeStruct((8, D_CONV), F32)],
        compiler_params=_cparams(("arbitrary", "arbitrary")),
    )(order, x, g1, meta_sh, convw_sh, win_sh, wout_sh)


def _f2_f3_call(proj, conv_w8, gret, tb, x, w_out, fg, target):
    nt, rows = tb["nt"], tb["rows"]
    seq = nt * TM

    def pf(s):
        return jnp.where(s == 0, nt, jnp.minimum(s - 1, nt - 1))

    def xt(s):
        return jnp.clip(s - 2, 0, nt - 1)

    def body(proj_ref, cw_ref, g_ref, tt_ref, cr_ref, sr_ref, sgn_ref, dec_ref, xi_ref, zeta_ref, cd_ref,
             x_ref, w_ref, fg_ref, t_ref,
             conv_hbm, states_hbm, dh2_ref, dmx_ref, gwo_ref, gfg_ref, loss_ref,
             state, uhalo, mxs, convs, sts, lacc, out_sems):
        s = pl.program_id(0)
        slot = lax.rem(s, 2)
        mixed_ref = mxs.at[slot]
        conv_ref = convs.at[slot]
        states_ref = sts.at[slot]

        def conv_out(sl, tile):
            return pltpu.make_async_copy(convs.at[sl], conv_hbm.at[pl.ds(pl.multiple_of(tile * TM, TM), TM), :],
                                         out_sems.at[sl])

        def states_out(sl, tile):
            return pltpu.make_async_copy(sts.at[sl], states_hbm.at[pl.ds(pl.multiple_of(tile * NCH, NCH), NCH)],
                                         out_sems.at[2 + sl])

        @pl.when(s == 0)
        def _():
            state[...] = jnp.zeros_like(state)
            uhalo[...] = jnp.zeros_like(uhalo)
            mxs[...] = jnp.zeros_like(mxs)
            gwo_ref[...] = jnp.zeros_like(gwo_ref)
            gfg_ref[...] = jnp.zeros_like(gfg_ref)
            lacc[...] = jnp.zeros_like(lacc)

        @pl.when(s >= 2)
        def _():
            conv_out(slot, pf(s - 2)).wait()
            states_out(slot, pf(s - 2)).wait()

        valid = jnp.where(s >= 2, 1.0, 0.0)
        mx_prev = mxs.at[1 - slot]
        f3 = {}

        def f3_fwd():
            f3["h2"] = x_ref[...] + _dot(mx_prev[...], w_ref[...])

        def f3_loss():
            h2 = f3.pop("h2")
            ms = jnp.mean(h2 * h2, axis=-1, keepdims=True)
            rstd = lax.rsqrt(ms + EPS)
            yh = h2 * rstd
            g = fg_ref[...]
            e = (yh * g - t_ref[...]) * valid
            lacc[...] += jnp.sum(e * e, axis=0, keepdims=True)
            dy = e * (1.0 / D_MODEL)
            gfg_ref[...] += jnp.sum(dy * yh, axis=0, keepdims=True)
            dyh = dy * g
            dh2 = rstd * (dyh - yh * jnp.mean(dyh * yh, axis=-1, keepdims=True))
            dh2_ref[...] = dh2
            f3["db"] = dh2.astype(BF16)

        def f3_dmx():
            dmx_ref[...] = _dot_tb(f3["db"], w_ref[...]).astype(BF16)

        def f3_gw():
            gw = _dot_ta(mx_prev[...], f3["db"])
            for j in range(N_CHIPS):
                for hf in range(2):
                    r0 = j * 256 + hf * 128
                    gwo_ref[hf, j] += gw[r0:r0 + 128, :]

        cx = proj_ref[:, CX:CX + 512].astype(F32)
        cc = proj_ref[:, CC:CC + 512].astype(F32)
        u = cc * cx
        row = lax.broadcasted_iota(jnp.int32, (TM, D_CONV), 0)
        h7 = uhalo[7:8, :]
        h6 = uhalo[6:7, :]
        u1 = jnp.where(row == 0, h7, pltpu.roll(u, 1, 0))
        u2 = jnp.where(row == 0, h6, jnp.where(row == 1, h7, pltpu.roll(u, 2, 0)))
        conv = cw_ref[2:3, :] * u + cw_ref[1:2, :] * u1 + cw_ref[0:1, :] * u2
        uhalo[...] = u[TM - 8:TM, :]
        cb = proj_ref[:, CB:CB + 512].astype(F32)
        cg = proj_ref[:, CG:CG + 512].astype(F32)
        mixed_ref[:, 0:D_CONV] = (cb * conv * (cg * _sigmoid(cg))).astype(BF16)
        conv_ref[...] = conv.astype(BF16)
        f3_fwd()

        scale = HEAD_DIM ** -0.5
        H = range(RET_HEADS)
        st = [state[h] for h in H]
        between = [f3_loss, f3_dmx, f3_gw, None]
        rc_t, rs_t = _tile_rotary(tt_ref, cr_ref, sr_ref, sgn_ref)
        for c in range(NCH):
            r0 = c * CHUNK
            rc = rc_t[r0:r0 + CHUNK, :]
            rs = rs_t[r0:r0 + CHUNK, :]
            col = lambda base, h: slice(base + h * HEAD_DIM, base + (h + 1) * HEAD_DIM)
            rws = slice(r0, r0 + CHUNK)
            v = [proj_ref[rws, col(CV, h)] for h in H]
            qf = [_rot(proj_ref[rws, col(CQ, h)].astype(F32), rc, rs) * scale for h in H]
            kf = [_rot(proj_ref[rws, col(CK, h)].astype(F32), rc, rs) for h in H]
            stb = [t.astype(BF16) for t in st]
            for h in H:
                states_ref[c, h] = stb[h]
            a = [(_dot_tb(qf[h].astype(BF16), kf[h].astype(BF16)) * dec_ref[h]).astype(BF16) for h in H]
            o = [_dot(a[h], v[h]) + _dot((qf[h] * xi_ref[h]).astype(BF16), stb[h]) for h in H]
            st = [cd_ref[h, 0:1, :] * st[h] + _dot_ta((kf[h] * zeta_ref[h]).astype(BF16), v[h]) for h in H]
            for h in H:
                mu = jnp.mean(o[h], axis=-1, keepdims=True)
                d = o[h] - mu
                var = jnp.mean(d * d, axis=-1, keepdims=True)
                yh = d * lax.rsqrt(var + EPS)
                rg = proj_ref[rws, col(CR, h)].astype(F32)
                mixed_ref[rws, col(D_CONV, h)] = (yh * g_ref[:, col(0, h)] * (rg * _sigmoid(rg))).astype(BF16)
            if between[c] is not None:
                between[c]()
        for h in H:
            state[h] = st[h]

        @pl.when(s <= nt)
        def _():
            conv_out(slot, pf(s)).start()
            states_out(slot, pf(s)).start()

        @pl.when(s == nt + 1)
        def _():
            conv_out(1 - slot, pf(s - 1)).wait()
            states_out(1 - slot, pf(s - 1)).wait()
            tot = jnp.sum(lacc[...], axis=1, keepdims=True) * (0.5 / D_MODEL)
            loss_ref[...] = jnp.broadcast_to(tot, (1, 128))

    tile = lambda w: pl.BlockSpec((TM, w), lambda s: (pf(s), 0))
    xtile = lambda w: pl.BlockSpec((TM, w), lambda s: (xt(s), 0))
    any_spec = pl.BlockSpec(memory_space=pl.ANY)
    return pl.pallas_call(
        body, name="f2_mixer_fwd_f3_outproj_loss",
        grid=(nt + 2,),
        in_specs=[tile(N_PROJ_COLS), _resident((8, D_CONV)), _resident((1, D_RET)),
                  pl.BlockSpec((None, 8, HEAD_DIM), lambda s: (pf(s), 0, 0)),
                  _resident((TM, HEAD_DIM)), _resident((TM, HEAD_DIM)), _resident((8, HEAD_DIM)),
                  _resident((RET_HEADS, CHUNK, CHUNK)), _resident((RET_HEADS, CHUNK, HEAD_DIM)),
                  _resident((RET_HEADS, CHUNK, HEAD_DIM)), _resident((RET_HEADS, 8, HEAD_DIM)),
                  xtile(D_MODEL), _resident1((D_MODEL, D_MODEL)), _resident((1, D_MODEL)), xtile(D_MODEL)],
        out_specs=[any_spec, any_spec, xtile(D_MODEL), xtile(D_MODEL),
                   _resident((2, N_CHIPS, 128, D_MODEL)), _resident((1, D_MODEL)), _resident((1, 128))],
        out_shape=[jax.ShapeDtypeStruct((rows, D_CONV), BF16),
                   jax.ShapeDtypeStruct(((nt + 1) * NCH, RET_HEADS, HEAD_DIM, HEAD_DIM), BF16),
                   jax.ShapeDtypeStruct((seq, D_MODEL), F32),
                   jax.ShapeDtypeStruct((seq, D_MODEL), BF16),
                   jax.ShapeDtypeStruct((2, N_CHIPS, 128, D_MODEL), F32),
                   jax.ShapeDtypeStruct((1, D_MODEL), F32),
                   jax.ShapeDtypeStruct((1, 128), F32)],
        scratch_shapes=[pltpu.VMEM((RET_HEADS, HEAD_DIM, HEAD_DIM), F32), pltpu.VMEM((8, D_CONV), F32),
                        pltpu.VMEM((2, TM, D_MODEL), BF16), pltpu.VMEM((2, TM, D_CONV), BF16),
                        pltpu.VMEM((2, NCH, RET_HEADS, HEAD_DIM, HEAD_DIM), BF16),
                        pltpu.VMEM((1, D_MODEL), F32), pltpu.SemaphoreType.DMA((4,))],
        compiler_params=_cparams(("arbitrary",)),
    )(proj, conv_w8, gret, tb["tt"], tb["cr2"], tb["sr2"], tb["sgn"], tb["decay"], tb["xi"], tb["zeta"], tb["cd"],
      x, w_out, fg, target)


def _b2_b1a_call(proj, dmixed, conv_s, states, conv_w8, gret, tb, w_in_g, x, meta_tile, g1, dh2):
    nt, rows = tb["nt"], tb["rows"]
    seq = nt * TM

    def pb(r):
        return jnp.where(r == nt, nt, nt - 1 - r)

    def xprev(r):
        return jnp.clip(nt - r, 0, nt - 1)

    def body(proj_ref, dmx_ref, conv_ref, states_ref, cw_ref, g_ref, tt_ref, cr_ref, sr_ref, sgn_ref, dec_ref,
             xi_ref, zeta_ref, cd_ref, w_ref, x_ref, mt_ref, g1_ref, dh2_ref,
             dproj_hbm, gcw_ref, gg_ref, gx_ref, dmeta_ref, gn_ref,
             gstate, dchalo, dps, out_sems):
        r = pl.program_id(0)
        live = jnp.where(r == nt, 0.0, 1.0)
        slot = lax.rem(r, 2)
        dproj_ref = dps.at[slot]

        class to_hbm:
            def __init__(self, s, tile):
                self.copies = [pltpu.make_async_copy(dps.at[s, :, j * 1024:(j + 1) * 1024], dproj_hbm.at[tile, j],
                                                     out_sems.at[N_CHIPS * s + j]) for j in range(N_CHIPS)]

            def start(self):
                for cp in self.copies:
                    cp.start()

            def wait(self):
                for cp in self.copies:
                    cp.wait()

        @pl.when(r == 0)
        def _():
            gstate[...] = jnp.zeros_like(gstate)
            dchalo[...] = jnp.zeros_like(dchalo)
            gcw_ref[...] = jnp.zeros_like(gcw_ref)
            gg_ref[...] = jnp.zeros_like(gg_ref)
            gn_ref[...] = jnp.zeros_like(gn_ref)
            dps[...] = jnp.zeros_like(dps)

        @pl.when(r >= 2)
        def _():
            to_hbm(slot, pb(r - 2)).wait()

        dprev = dps.at[1 - slot]
        pieces = []

        def emit_piece():
            j = len(pieces)
            if j < N_CHIPS:
                p = _dot_tb(dprev[:, j * 1024:(j + 1) * 1024], w_ref[j])
                pieces.append(p if j == 0 else pieces[-1] + p)

        cx = proj_ref[:, CX:CX + 512].astype(F32)
        cb = proj_ref[:, CB:CB + 512].astype(F32)
        cc = proj_ref[:, CC:CC + 512].astype(F32)
        cg = proj_ref[:, CG:CG + 512].astype(F32)
        dco = dmx_ref[:, 0:D_CONV].astype(F32) * live
        conv = conv_ref[...].astype(F32)
        sg = _sigmoid(cg)
        sil = cg * sg
        t = dco * conv
        dproj_ref[:, CB:CB + 512] = (t * sil).astype(BF16)
        dproj_ref[:, CG:CG + 512] = (t * cb * (sg * (1.0 + cg * (1.0 - sg)))).astype(BF16)
        dconv = dco * cb * sil
        row = lax.broadcasted_iota(jnp.int32, (TM, D_CONV), 0)
        n0 = dchalo[0:1, :]
        n1 = dchalo[1:2, :]
        dc1 = jnp.where(row == TM - 1, n0, pltpu.roll(dconv, TM - 1, 0))
        dc2 = jnp.where(row == TM - 2, n0, jnp.where(row == TM - 1, n1, pltpu.roll(dconv, TM - 2, 0)))
        dchalo[...] = dconv[0:8, :]
        du = cw_ref[2:3, :] * dconv + cw_ref[1:2, :] * dc1 + cw_ref[0:1, :] * dc2
        u = cc * cx
        gcw_ref[2:3, :] += jnp.sum(u * dconv, axis=0, keepdims=True)
        gcw_ref[1:2, :] += jnp.sum(u * dc1, axis=0, keepdims=True)
        gcw_ref[0:1, :] += jnp.sum(u * dc2, axis=0, keepdims=True)
        dproj_ref[:, CC:CC + 512] = (du * cx).astype(BF16)
        dproj_ref[:, CX:CX + 512] = (du * cc).astype(BF16)
        emit_piece()
        emit_piece()

        scale = HEAD_DIM ** -0.5
        gs = {h: gstate[h] for h in range(RET_HEADS)}
        gg = {h: jnp.zeros((1, HEAD_DIM), F32) for h in range(RET_HEADS)}
        col = lambda base, h: slice(base + h * HEAD_DIM, base + (h + 1) * HEAD_DIM)
        rw = lambda c: slice(c * CHUNK, (c + 1) * CHUNK)
        rc_t, rs_t = _tile_rotary(tt_ref, cr_ref, sr_ref, sgn_ref)
        for c0 in range(NCH - CHUNK_GROUP, -1, -CHUNK_GROUP):
            cs = range(c0 + CHUNK_GROUP - 1, c0 - 1, -1)
            U = [(c, h) for c in cs for h in range(RET_HEADS)]
            rc = {c: rc_t[rw(c), :] for c in cs}
            rs = {c: rs_t[rw(c), :] for c in cs}
            v = {(c, h): proj_ref[rw(c), col(CV, h)] for c, h in U}
            stb = {(c, h): states_ref[c, h] for c, h in U}
            qf = {(c, h): _rot(proj_ref[rw(c), col(CQ, h)].astype(F32), rc[c], rs[c]) * scale for c, h in U}
            kf = {(c, h): _rot(proj_ref[rw(c), col(CK, h)].astype(F32), rc[c], rs[c]) for c, h in U}
            qb = {u: qf[u].astype(BF16) for u in U}
            kb = {u: kf[u].astype(BF16) for u in U}
            qxb = {(c, h): (qf[c, h] * xi_ref[h]).astype(BF16) for c, h in U}
            kzb = {(c, h): (kf[c, h] * zeta_ref[h]).astype(BF16) for c, h in U}
            ab = {(c, h): (_dot_tb(qb[c, h], kb[c, h]) * dec_ref[h]).astype(BF16) for c, h in U}
            o = {u: _dot(ab[u], v[u]) + _dot(qxb[u], stb[u]) for u in U}
            emit_piece()
            dob = {}
            for c, h in U:
                mu = jnp.mean(o[c, h], axis=-1, keepdims=True)
                d = o[c, h] - mu
                var = jnp.mean(d * d, axis=-1, keepdims=True)
                rstd = lax.rsqrt(var + EPS)
                yh = d * rstd
                g = g_ref[:, col(0, h)]
                rg = proj_ref[rw(c), col(CR, h)].astype(F32)
                dro = dmx_ref[rw(c), col(D_CONV, h)].astype(F32) * live
                sg = _sigmoid(rg)
                dproj_ref[rw(c), col(CR, h)] = (dro * (yh * g) * (sg * (1.0 + rg * (1.0 - sg)))).astype(BF16)
                dret = dro * (rg * sg)
                gg[h] = gg[h] + jnp.sum(dret * yh, axis=0, keepdims=True)
                dyh = dret * g
                do = rstd * (dyh - jnp.mean(dyh, axis=-1, keepdims=True)
                             - yh * jnp.mean(dyh * yh, axis=-1, keepdims=True))
                dob[c, h] = do.astype(BF16)
            dv1 = {u: _dot_ta(ab[u], dob[u]) for u in U}
            ds = {(c, h): (_dot_tb(dob[c, h], v[c, h]) * dec_ref[h]).astype(BF16) for c, h in U}
            gup = {u: _dot_ta(qxb[u], dob[u]) for u in U}
            dq = {(c, h): _dot(ds[c, h], kb[c, h]) + _dot_tb(dob[c, h], stb[c, h]) * xi_ref[h] for c, h in U}
            dk1 = {u: _dot_ta(ds[u], qb[u]) for u in U}
            emit_piece()
            for c, h in U:
                gsb = gs[h].astype(BF16)
                dv = dv1[c, h] + _dot(kzb[c, h], gsb)
                dk = dk1[c, h] + _dot_tb(v[c, h], gsb) * zeta_ref[h]
                gs[h] = cd_ref[h, 0:1, :] * gs[h] + gup[c, h]
                dproj_ref[rw(c), col(CQ, h)] = (_rot_t(dq[c, h], rc[c], rs[c]) * scale).astype(BF16)
                dproj_ref[rw(c), col(CK, h)] = _rot_t(dk, rc[c], rs[c]).astype(BF16)
                dproj_ref[rw(c), col(CV, h)] = dv.astype(BF16)
        for h in range(RET_HEADS):
            gstate[h] = gs[h]
            gg_ref[:, col(0, h)] += gg[h]

        while len(pieces) < N_CHIPS:
            emit_piece()

        def norm_bwd(dhn, hx):
            ms = jnp.mean(hx * hx, axis=-1, keepdims=True)
            rstd1 = lax.rsqrt(ms + EPS)
            xh = hx * rstd1
            gn_ref[...] += jnp.sum(dhn * xh, axis=0, keepdims=True)
            dxh = dhn * g1_ref[...]
            return rstd1 * (dxh - xh * jnp.mean(dxh * xh, axis=-1, keepdims=True))

        gx_ref[...] = norm_bwd(pieces[-1], x_ref[...]) + dh2_ref[...]

        @pl.when(r < nt)
        def _():
            to_hbm(slot, pb(r)).start()

        @pl.when(r == nt)
        def _():
            to_hbm(slot, pb(r)).start()
            mrows = slice(TM - N_META, TM)
            d16 = dproj_ref[mrows, :]
            dhn16 = _dot_tb(d16[:, 0:1024], w_ref[0])
            for j in range(1, N_CHIPS):
                dhn16 += _dot_tb(d16[:, j * 1024:(j + 1) * 1024], w_ref[j])
            dmeta_ref[...] = norm_bwd(dhn16, mt_ref[mrows, :])
            to_hbm(1 - slot, pb(r - 1)).wait()
            to_hbm(slot, pb(r)).wait()

    tile = lambda w: pl.BlockSpec((TM, w), lambda r: (pb(r), 0))
    xtile = pl.BlockSpec((TM, D_MODEL), lambda r: (xprev(r), 0))
    return pl.pallas_call(
        body, name="b2_mixer_bwd_b1a_inproj_bwd_x",
        grid=(nt + 1,),
        in_specs=[tile(N_PROJ_COLS),
                  pl.BlockSpec((TM, D_MODEL), lambda r: (jnp.minimum(pb(r), nt - 1), 0)),
                  tile(D_CONV),
                  pl.BlockSpec((NCH, RET_HEADS, HEAD_DIM, HEAD_DIM), lambda r: (pb(r), 0, 0, 0)),
                  _resident((8, D_CONV)), _resident((1, D_RET)),
                  pl.BlockSpec((None, 8, HEAD_DIM), lambda r: (pb(r), 0, 0)),
                  _resident((TM, HEAD_DIM)), _resident((TM, HEAD_DIM)), _resident((8, HEAD_DIM)),
                  _resident((RET_HEADS, CHUNK, CHUNK)),
                  _resident((RET_HEADS, CHUNK, HEAD_DIM)),
                  _resident((RET_HEADS, CHUNK, HEAD_DIM)), _resident((RET_HEADS, 8, HEAD_DIM)),
                  _resident1((N_CHIPS, D_MODEL, 1024)), xtile, _resident1((TM, D_MODEL)), _resident((1, D_MODEL)),
                  xtile],
        out_specs=[pl.BlockSpec(memory_space=pl.ANY), _resident((8, D_CONV)), _resident((1, D_RET)),
                   xtile, _resident((N_META, D_MODEL)), _resident((1, D_MODEL))],
        out_shape=[jax.ShapeDtypeStruct((nt + 1, N_CHIPS, TM, 1024), BF16),
                   jax.ShapeDtypeStruct((8, D_CONV), F32),
                   jax.ShapeDtypeStruct((1, D_RET), F32),
                   jax.ShapeDtypeStruct((seq, D_MODEL), F32),
                   jax.ShapeDtypeStruct((N_META, D_MODEL), F32),
                   jax.ShapeDtypeStruct((1, D_MODEL), F32)],
        scratch_shapes=[pltpu.VMEM((RET_HEADS, HEAD_DIM, HEAD_DIM), F32), pltpu.VMEM((8, D_CONV), F32),
                        pltpu.VMEM((2, TM, N_PROJ_COLS), BF16), pltpu.SemaphoreType.DMA((2 * N_CHIPS,))],
        compiler_params=_cparams(("arbitrary",), vmem=VMEM_LIMIT_MAX),
    )(proj, dmixed, conv_s, states, conv_w8, gret, tb["tt"], tb["cr2"], tb["sr2"], tb["sgn"], tb["decay"],
      tb["xi"], tb["zeta"], tb["cd"], w_in_g, x, meta_tile, g1, dh2)


REL = (2, 1, 3)
SMALL_ROWS = 24
HALF_STEP = 2


def _rcopy(src, dst, send_sems, recv_sems, k, to):
    return pltpu.make_async_remote_copy(src_ref=src, dst_ref=dst, send_sem=send_sems.at[k],
                                        recv_sem=recv_sems.at[k], device_id=to, device_id_type=MESH_ID)


def _b1b_reduce_call(order, hnt, dproj, gwo, small, nt):
    nk = nt + 1
    last = nk - 1
    any_spec = pl.BlockSpec(memory_space=pl.ANY)

    def body(order_ref, a_ref, b_ref, gm_ref, gn1_ref, gfg_ref, ggr_ref, gcw_ref, ls_ref, gwo_hbm,
             gwin_hbm, gwout_hbm, tot_hbm,
             acc, sb, abuf, pb, bbuf, fin, go, ao, pbo, bo, fino, slots, totv, send_sems, recv_sems, loc_sems):
        jj, k = pl.program_id(0), pl.program_id(1)
        x, y, c = lax.axis_index("x"), lax.axis_index("y"), lax.axis_index("c")
        me, myid, sib = 2 * x + y, 4 * x + 2 * y + c, (x, y, 1 - c)
        rc = functools.partial(_rcopy, send_sems=send_sems, recv_sems=recv_sems)
        peers = [((1 - x) if r & 2 else x, (1 - y) if r & 1 else y, c) for r in REL]
        kids = [jnp.bitwise_xor(me, r) for r in REL]

        def dev_peer(r):
            return ((1 - x) if r & 4 else x, (1 - y) if r & 2 else y, (1 - c) if r & 1 else c)

        own_go = pltpu.make_async_copy(gwo_hbm.at[c], go, loc_sems.at[0])
        wo_half = rc(gwo_hbm.at[1 - c], ao, k=8, to=sib)
        wo_part = [rc(pbo.at[kids[p]], bo.at[p], k=9 + p, to=peers[p]) for p in range(3)]
        sm = [rc(slots.at[0], slots.at[r], k=12 + r, to=dev_peer(r)) for r in range(1, N_DEV)]
        half = [rc(sb.at[j % 2, 1 - c], abuf.at[j], k=j, to=sib) for j in range(N_CHIPS)]
        part = [rc(pb.at[p], bbuf.at[p], k=4 + p, to=peers[p]) for p in range(3)]

        @pl.when(jnp.logical_and(jj == 0, k == 0))
        def _():
            own_go.start()
            slots[0, 0:N_META, :] = gm_ref[...]
            slots[0, N_META:SMALL_ROWS, :] = jnp.zeros((SMALL_ROWS - N_META, D_MODEL), F32)
            slots[0, N_META:N_META + 1, :] = gn1_ref[...]
            slots[0, N_META + 1:N_META + 2, :] = gfg_ref[...]
            slots[0, N_META + 2:N_META + 3, 0:D_RET] = ggr_ref[...]
            slots[0, N_META + 3:N_META + 6, 0:D_CONV] = gcw_ref[0:3, :]
            slots[0, N_META + 6:N_META + 7, 0:128] = ls_ref[...]
            wo_half.start()
            for cp in sm:
                cp.start()

        @pl.when(k == 0)
        def _():
            acc[...] = jnp.zeros_like(acc)

        acc[0] += _dot(a_ref[0:512, :], b_ref[...])
        acc[1] += _dot(a_ref[512:1024, :], b_ref[...])

        @pl.when(k == HALF_STEP)
        def _():
            @pl.when(jj == 0)
            def _():
                own_go.wait()
                wo_half.wait_recv()
                for j in range(N_CHIPS):
                    go[j] = go[j] + ao[j]
                pbo[...] = go[...].astype(BF16)
                for cp in wo_part:
                    cp.start()

            for p in range(3):
                @pl.when(jj == p + 1)
                def _(p=p):
                    half[p].wait_recv()
                    half[p].wait_send()
                    pb[p] = (sb[p % 2, c] + abuf[p]).astype(BF16)
                    part[p].start()

        @pl.when(k == last)
        def _():
            for j in range(N_CHIPS):
                @pl.when(jj == j)
                def _(j=j):
                    sb[j % 2] = acc[...]
                    half[j].start()

        @pl.when(jnp.logical_and(jj == N_CHIPS - 1, k == last))
        def _():
            half[3].wait_recv()
            own = sb[1, c] + abuf[3]
            for cp in part:
                cp.wait_recv()
            fin[c] = ((own + bbuf[0].astype(F32)) + bbuf[1].astype(F32)) + bbuf[2].astype(F32)
            done = rc(fin.at[c], fin.at[c], k=7, to=sib)
            done.start()
            for cp in wo_part:
                cp.wait_recv()
            fino[c] = ((go[me] + bo[0].astype(F32)) + bo[1].astype(F32)) + bo[2].astype(F32)
            done_o = rc(fino.at[c], fino.at[c], k=12, to=sib)
            done_o.start()
            for cp in sm:
                cp.wait_recv()
            tot = slots[myid]
            for a in range(1, N_DEV):
                tot = tot + slots[jnp.bitwise_xor(myid, a)]
            totv[...] = tot
            out_t = pltpu.make_async_copy(totv, tot_hbm, loc_sems.at[1])
            out_t.start()
            rc(fin.at[1 - c], fin.at[1 - c], k=7, to=sib).wait_recv()
            out_w = pltpu.make_async_copy(fin, gwin_hbm, loc_sems.at[0])
            out_w.start()
            rc(fino.at[1 - c], fino.at[1 - c], k=12, to=sib).wait_recv()
            out_o = pltpu.make_async_copy(fino, gwout_hbm, loc_sems.at[2])
            out_o.start()
            for cp in [half[3]] + part + [done, wo_half] + wo_part + [done_o] + sm:
                cp.wait_send()
            out_t.wait()
            out_w.wait()
            out_o.wait()

    grid_spec = pltpu.PrefetchScalarGridSpec(
        num_scalar_prefetch=1,
        grid=(N_CHIPS, nk),
        in_specs=[pl.BlockSpec((None, D_MODEL, TM), lambda j, k, o: (k, 0, 0)),
                  pl.BlockSpec((None, None, TM, 1024), lambda j, k, o: (k, o[j], 0, 0))]
                 + [pl.BlockSpec(s.shape, lambda j, k, o: (0, 0)) for s in small] + [any_spec],
        out_specs=[any_spec, any_spec, any_spec],
        scratch_shapes=[
            pltpu.VMEM((2, 512, 1024), F32),
            pltpu.VMEM((2, 2, 512, 1024), F32),
            pltpu.VMEM((N_CHIPS, 512, 1024), F32),
            pltpu.VMEM((3, 512, 1024), BF16),
            pltpu.VMEM((3, 512, 1024), BF16),
            pltpu.VMEM((2, 512, 1024), F32),
            pltpu.VMEM((N_CHIPS, 128, D_MODEL), F32),
            pltpu.VMEM((N_CHIPS, 128, D_MODEL), F32),
            pltpu.VMEM((N_CHIPS, 128, D_MODEL), BF16),
            pltpu.VMEM((3, 128, D_MODEL), BF16),
            pltpu.VMEM((2, 128, D_MODEL), F32),
            pltpu.VMEM((N_DEV, SMALL_ROWS, D_MODEL), F32),
            pltpu.VMEM((SMALL_ROWS, D_MODEL), F32),
            pltpu.SemaphoreType.DMA((20,)), pltpu.SemaphoreType.DMA((20,)), pltpu.SemaphoreType.DMA((3,))])
    return pl.pallas_call(
        body, name="b1b_inproj_bwd_w_reduce",
        grid_spec=grid_spec,
        out_shape=[jax.ShapeDtypeStruct((2, 512, 1024), F32),
                   jax.ShapeDtypeStruct((2, 128, D_MODEL), F32),
                   jax.ShapeDtypeStruct((SMALL_ROWS, D_MODEL), F32)],
        compiler_params=_cparams(("arbitrary", "arbitrary")),
    )(order, hnt, dproj, *small, gwo)


def _local_step(me, x, target, g1, gret, fg, win_sh, wout_sh, meta_sh, convw_sh):
    seq = x.shape[0]
    tb = _tables(seq)
    nt = tb["nt"]
    g1r, gretr, fgr = g1.reshape(1, -1), gret.reshape(1, -1), fg.reshape(1, -1)
    order = jnp.stack([me, me ^ REL[0], me ^ REL[1], me ^ REL[2]]).astype(jnp.int32)

    proj, hnt, w_in_g, w_out_g, meta_tile, conv_w8 = _f1_gather_call(order, x, g1r, win_sh, wout_sh, meta_sh,
                                                                     convw_sh, nt)
    w_out = w_out_g.reshape(D_MODEL, D_MODEL)
    conv_s, states, dh2, dmixed, g_wout, g_fg, loss = _f2_f3_call(proj, conv_w8, gretr, tb, x, w_out, fgr, target)
    dproj, g_cw8, g_gret, grad_x, g_meta, g_g1 = _b2_b1a_call(proj, dmixed, conv_s, states, conv_w8, gretr, tb,
                                                              w_in_g, x, meta_tile, g1r, dh2)
    return grad_x, g_wout, (g_meta, g_g1, g_fg, g_gret, g_cw8, loss), hnt, dproj


def _adamw_update(w_ref, g_ref, m_ref, v_ref, d_ref, nm_ref, nv_ref):
    gg = g_ref[...]
    nm = ADAM_B1 * m_ref[...] + (1.0 - ADAM_B1) * gg
    nv = ADAM_B2 * v_ref[...] + (1.0 - ADAM_B2) * (gg * gg)
    m_hat = nm / (1.0 - ADAM_B1 ** ADAM_STEP)
    v_hat = nv / (1.0 - ADAM_B2 ** ADAM_STEP)
    d_ref[...] = -ADAM_LR * (m_hat / (jnp.sqrt(v_hat) + ADAM_EPS) + ADAM_WD * w_ref[...])
    nm_ref[...] = nm
    nv_ref[...] = nv


def _adamw_small_call(me, tot, ws, ms, vs):
    n = len(ws)

    def body(me_ref, tmeta_ref, tvec_ref, tconv_ref, *refs):
        ins, outs = refs[:3 * n], refs[3 * n:]
        g_refs, loss_ref, upd = outs[0:n], outs[n], outs[n + 1:]
        g_refs[0][...] = tmeta_ref[...]
        g_refs[1][...] = tvec_ref[0:1, :]
        g_refs[2][...] = tconv_ref[3:6, :]
        g_refs[3][...] = tvec_ref[2:3, 0:D_RET]
        g_refs[4][...] = tvec_ref[1:2, :]
        loss_ref[...] = tvec_ref[6:7, 0:1]
        for i in range(n):
            _adamw_update(ins[i], g_refs[i], ins[n + i], ins[2 * n + i], upd[i], upd[n + i], upd[2 * n + i])

    whole = lambda a: pl.BlockSpec(a.shape, lambda i, m: (0,) * a.ndim)
    shapes = [jax.ShapeDtypeStruct(w.shape, F32) for w in ws]
    out_shape = shapes + [jax.ShapeDtypeStruct((1, 1), F32)] + shapes * 3
    grid_spec = pltpu.PrefetchScalarGridSpec(
        num_scalar_prefetch=1, grid=(1,),
        in_specs=[pl.BlockSpec((N_META, 256), lambda i, m: (0, m[0])),
                  pl.BlockSpec((8, D_MODEL), lambda i, m: (N_META // 8, 0)),
                  pl.BlockSpec((8, 128), lambda i, m: (N_META // 8, m[0]))] + [whole(a) for a in ws + ms + vs],
        out_specs=[whole(s) for s in out_shape])
    outs = pl.pallas_call(body, name="adamw_small", grid_spec=grid_spec, out_shape=out_shape,
                          compiler_params=_cparams(("arbitrary",)))(me.reshape(1), tot, tot, tot, *ws, *ms, *vs)
    return outs[:n], outs[n], outs[n + 1:2 * n + 1], outs[2 * n + 1:3 * n + 1], outs[3 * n + 1:]


def _adamw_big_call(a4, b4):
    br = 256
    na = a4[0].shape[0] // br
    assert all(t.shape == (na * br, 1024) for t in a4) and all(t.shape == (br, 1024) for t in b4)

    def body(*refs):
        i = pl.program_id(0)
        a_in, b_in, a_out, b_out = refs[0:4], refs[4:8], refs[8:11], refs[11:14]

        @pl.when(i < na)
        def _():
            _adamw_update(*a_in, *a_out)

        @pl.when(i == na)
        def _():
            _adamw_update(*b_in, *b_out)

    spec_a = pl.BlockSpec((br, 1024), lambda i: (jnp.minimum(i, na - 1), 0))
    spec_b = pl.BlockSpec((br, 1024), lambda i: (0, 0))
    outs = pl.pallas_call(
        body, name="adamw_w_in_w_out", grid=(na + 1,),
        in_specs=[spec_a] * 4 + [spec_b] * 4, out_specs=[spec_a] * 3 + [spec_b] * 3,
        out_shape=[jax.ShapeDtypeStruct(a4[0].shape, F32)] * 3 + [jax.ShapeDtypeStruct(b4[0].shape, F32)] * 3,
        compiler_params=_cparams(("arbitrary",)),
    )(*a4, *b4)
    return outs[0:3], outs[3:6]


def kernel(x, meta, norm1_g, w_in, conv_w, ret_norm_g, w_out, final_g, loss_target, m_meta, m_norm1_g, m_w_in, m_conv_w, m_ret_norm_g, m_w_out, m_final_g, v_meta, v_norm1_g, v_w_in, v_conv_w, v_ret_norm_g, v_w_out, v_final_g):
    me = 2 * lax.axis_index("x") + lax.axis_index("y")

    grad_x, g_wo, small, hnt, dproj = _local_step(me, x[0], loss_target[0], norm1_g, ret_norm_g, final_g,
                                                  w_in, w_out, meta, conv_w)

    order = jnp.stack([me ^ REL[0], me ^ REL[1], me ^ REL[2], me]).astype(jnp.int32)
    g_win, g_wout, tot = _b1b_reduce_call(order, hnt, dproj, g_wo, small, x.shape[1] // TM)
    g_win, g_wout = g_win.reshape(D_MODEL, 1024), g_wout.reshape(256, D_MODEL)

    ws = [meta, norm1_g, w_in, conv_w, ret_norm_g, w_out, final_g]
    ms = [m_meta, m_norm1_g, m_w_in, m_conv_w, m_ret_norm_g, m_w_out, m_final_g]
    vs = [v_meta, v_norm1_g, v_w_in, v_conv_w, v_ret_norm_g, v_w_out, v_final_g]
    names = ["meta", "norm1_g", "w_in", "conv_w", "ret_norm_g", "w_out", "final_g"]
    as2d = lambda a: a.reshape(1, -1) if a.ndim == 1 else a
    big = {names.index("w_in"): g_win, names.index("w_out"): g_wout}
    small = [i for i in range(len(names)) if i not in big]
    grads, deltas, new_ms, new_vs = [None] * 7, [None] * 7, [None] * 7, [None] * 7
    ia, ib = names.index("w_in"), names.index("w_out")
    upd = _adamw_big_call(*[(ws[i], big[i], ms[i], vs[i]) for i in (ia, ib)])
    for i, u in zip((ia, ib), upd):
        grads[i] = big[i]
        deltas[i], new_ms[i], new_vs[i] = u
    sg, loss_tot, sd, sm_, sv = _adamw_small_call(me.astype(jnp.int32), tot,
                                                  *[[as2d(t[i]) for i in small] for t in (ws, ms, vs)])
    for j, i in enumerate(small):
        grads[i], deltas[i], new_ms[i], new_vs[i] = (o[j].reshape(ws[i].shape) for o in (sg, sd, sm_, sv))
    return (loss_tot.reshape(()), grad_x[None], *grads, *deltas, *new_ms, *new_vs)
```

```python
import functools

import jax
import jax.numpy as jnp
import numpy as np
from jax import lax
from jax.experimental import pallas as pl
from jax.experimental.pallas import tpu as pltpu

F32 = jnp.float32
BF16 = jnp.bfloat16

D_MODEL = 1024
N_META = 16
D_CONV = 512
D_RET = 512
RET_HEADS = 4
HEAD_DIM = 128
CHUNK = 128
N_PROJ_COLS = 4096
ROPE_BASE = 10000.0
EPS = 1e-6
N_CHIPS = 4
N_DEV = 8

ADAM_LR = 0.001
ADAM_B1 = 0.9
ADAM_B2 = 0.999
ADAM_EPS = 1e-08
ADAM_WD = 0.01
ADAM_STEP = 10

TM = 512
NCH = TM // CHUNK
CHUNK_GROUP = 2
VMEM_LIMIT = 56 * 1024 * 1024
VMEM_LIMIT_MAX = 63 * 1024 * 1024

CX, CB, CC, CG, CQ, CK, CV, CR = (i * 512 for i in range(8))

MESH_ID = pl.DeviceIdType.MESH


def _cparams(sem=None, vmem=VMEM_LIMIT, **kw):
    return pltpu.CompilerParams(dimension_semantics=sem, vmem_limit_bytes=vmem, **kw)


def _sigmoid(x):
    return 1.0 / (1.0 + jnp.exp(-x))


def _dot(a, b):
    return jnp.dot(a, b, preferred_element_type=F32)


def _dot_tb(a, b):
    return lax.dot_general(a, b, (((1,), (1,)), ((), ())), preferred_element_type=F32)


def _dot_ta(a, b):
    return lax.dot_general(a, b, (((0,), (0,)), ((), ())), preferred_element_type=F32)


def _resident(shape):
    nd = len(shape)
    return pl.BlockSpec(shape, lambda *_: (0,) * nd)


def _resident1(shape):
    nd = len(shape)
    return pl.BlockSpec(shape, lambda *_: (0,) * nd, pipeline_mode=pl.Buffered(1))


def _tables(seq):
    f32 = np.float32
    nt = seq // TM
    rows = seq + TM
    half = HEAD_DIM // 2
    freqs = (f32(1.0) / (f32(ROPE_BASE) ** (np.arange(half, dtype=f32) / f32(half)))).astype(f32)
    tile_start = np.concatenate([np.arange(nt, dtype=f32), -np.ones((1,), f32)]) * f32(TM)
    ang_t = tile_start[:, None] * freqs[None, :]
    ang_r = (np.arange(TM, dtype=f32) + f32(N_META))[:, None] * freqs[None, :]
    dup = lambda a: np.concatenate([a, a], axis=-1).astype(f32)
    tt = np.stack([dup(np.cos(ang_t)), dup(np.sin(ang_t))], axis=1)
    tt = np.pad(tt, ((0, 0), (0, 6), (0, 0)))
    cr2, sr2 = dup(np.cos(ang_r)), dup(np.sin(ang_r))
    sgn = np.concatenate([-np.ones((8, half), f32), np.ones((8, half), f32)], axis=-1)
    log_g = np.log(f32(1.0) - f32(2.0) ** (f32(-5.0) - np.arange(RET_HEADS, dtype=f32))).astype(f32)
    idx = np.arange(CHUNK, dtype=f32)
    diff = idx[:, None] - idx[None, :]
    decay = np.where(diff[None] >= 0, np.exp(diff[None] * log_g[:, None, None]), f32(0.0)).astype(f32)
    zeta = np.exp((f32(CHUNK - 1) - idx)[None, :] * log_g[:, None]).astype(f32)
    xi = np.exp((idx + f32(1.0))[None, :] * log_g[:, None]).astype(f32)
    cd = np.exp(f32(CHUNK) * log_g).astype(f32)
    zeta_b = np.broadcast_to(zeta[:, :, None], (RET_HEADS, CHUNK, HEAD_DIM))
    xi_b = np.broadcast_to(xi[:, :, None], (RET_HEADS, CHUNK, HEAD_DIM))
    cd_b = np.broadcast_to(cd[:, None, None], (RET_HEADS, 8, HEAD_DIM))
    tables = dict(tt=tt, cr2=cr2, sr2=sr2, sgn=sgn, decay=decay, zeta=zeta_b, xi=xi_b, cd=cd_b)
    return dict(nt=nt, rows=rows, **{k: jnp.asarray(np.ascontiguousarray(v, dtype=f32)) for k, v in tables.items()})


def _tile_rotary(tt_ref, cr_ref, sr_ref, sgn_ref):
    ct, st = tt_ref[0:1, :], tt_ref[1:2, :]
    cr, sr = cr_ref[...], sr_ref[...]
    return ct * cr - st * sr, (st * cr + ct * sr) * sgn_ref[0:1, :]


def _rot(t, rc, rs):
    return t * rc + pltpu.roll(t, HEAD_DIM // 2, 1) * rs


def _rot_t(dt, rc, rs):
    return dt * rc + pltpu.roll(dt * rs, HEAD_DIM // 2, 1)


def _f1_gather_call(order, x, g1, win_sh, wout_sh, meta_sh, convw_sh, nt):
    nk = nt + 1
    rows = nk * TM
    any_spec = pl.BlockSpec(memory_space=pl.ANY)

    def body(order_ref, x_ref, g_ref, meta_ref, convw_ref, win_hbm, wout_hbm,
             proj_ref, hnt_ref, wg_hbm, wog_hbm, mt_hbm, cw_hbm,
             wg, wog, smg, mt, cw, hbs, st2, st3, send_sems, recv_sems, loc_sems):
        jj, k = pl.program_id(0), pl.program_id(1)
        x, y, c = lax.axis_index("x"), lax.axis_index("y"), lax.axis_index("c")
        me, sib = 2 * x + y, (x, y, 1 - c)
        rc = functools.partial(_rcopy, send_sems=send_sems, recv_sems=recv_sems)
        peers = [((1 - x) if r & 2 else x, (1 - y) if r & 1 else y, c) for r in REL]
        kids = [jnp.bitwise_xor(me, r) for r in REL]
        hw, ho = pl.ds(pl.multiple_of(c * 512, 512), 512), pl.ds(pl.multiple_of(c * 128, 128), 128)
        hw2 = pl.ds(pl.multiple_of((1 - c) * 512, 512), 512)
        ho2 = pl.ds(pl.multiple_of((1 - c) * 128, 128), 128)
        at = lambda j_, k_: jnp.logical_and(jj == j_, k == k_)

        sm_cp = [rc(smg.at[me], smg.at[me], k=p, to=peers[p]) for p in range(3)]
        win_cp = [rc(wg.at[me, hw], wg.at[me, hw], k=3 + p, to=peers[p]) for p in range(3)]
        wout_cp = [rc(wog.at[me, ho], wog.at[me, ho], k=6 + p, to=peers[p]) for p in range(3)]
        sm_in = [rc(smg.at[me], smg.at[kids[p]], k=p, to=sib) for p in range(3)]
        win_in = [rc(wg.at[me, hw], wg.at[kids[p], hw], k=3 + p, to=sib) for p in range(3)]
        wout_in = [rc(wog.at[me, ho], wog.at[kids[p], ho], k=6 + p, to=sib) for p in range(3)]
        win_fw = [rc(wg.at[kids[p], hw], wg.at[kids[p], hw], k=9 + p, to=sib) for p in range(3)]
        wout_fw = [rc(wog.at[kids[p], ho], wog.at[kids[p], ho], k=12 + p, to=sib) for p in range(3)]
        win_fw_in = [rc(wg.at[kids[p], hw2], wg.at[kids[p], hw2], k=9 + p, to=sib) for p in range(3)]
        wout_fw_in = [rc(wog.at[kids[p], ho2], wog.at[kids[p], ho2], k=12 + p, to=sib) for p in range(3)]

        def stage_own():
            cp_a = pltpu.make_async_copy(win_hbm.at[hw], mt, loc_sems.at[0])
            cp_b = pltpu.make_async_copy(win_hbm.at[hw2], st2, loc_sems.at[1])
            cp_c = pltpu.make_async_copy(wout_hbm, st3, loc_sems.at[2])
            cp_a.start()
            cp_c.start()
            cp_b.start()
            cp_a.wait()
            wg[me, hw, :] = mt[...].astype(BF16)
            for s_ in win_cp:
                s_.start()
            cp_c.wait()
            wog[me] = st3[...].astype(BF16)
            for s_ in wout_cp:
                s_.start()
            cp_b.wait()
            wg[me, hw2, :] = st2[...].astype(BF16)
        out_wg = pltpu.make_async_copy(wg, wg_hbm, loc_sems.at[3])
        out_wog = pltpu.make_async_copy(wog, wog_hbm, loc_sems.at[4])
        out_mt = pltpu.make_async_copy(mt, mt_hbm, loc_sems.at[5])
        out_cw = pltpu.make_async_copy(cw, cw_hbm, loc_sems.at[6])

        def pass_on(p):
            win_in[p].wait_recv()
            win_fw[p].start()

        @pl.when(k <= 1)
        def _():
            @pl.when(at(0, 0))
            def _():
                smg[me] = jnp.zeros((SMALL_ROWS, 256), F32)
                smg[me, 0:N_META, :] = meta_ref[...]
                smg[me, N_META:N_META + 3, 0:128] = convw_ref[...]
                for cp in sm_cp:
                    cp.start()
                stage_own()

            for p in range(3):
                @pl.when(at(p + 1, 0))
                def _(p=p):
                    win_fw_in[p].wait_recv()

            @pl.when(at(1, 1))
            def _():
                pass_on(1)

            @pl.when(at(3, 0))
            def _():
                out_wg.start()

            @pl.when(at(3, 1))
            def _():
                for p in range(3):
                    wout_in[p].wait_recv()
                    wout_fw[p].start()

        @pl.when(jnp.logical_and(jj == 0, k >= nk - 2))
        def _():
            @pl.when(k == nk - 2)
            def _():
                for cp in sm_in:
                    cp.wait_recv()
                mt[...] = jnp.zeros_like(mt)
                cw[...] = jnp.zeros_like(cw)
                for j in range(N_CHIPS):
                    mt[TM - N_META:TM, j * 256:(j + 1) * 256] = smg[j, 0:N_META, :]
                    cw[0:3, j * 128:(j + 1) * 128] = smg[j, N_META:N_META + 3, 0:128]
                out_mt.start()
                out_cw.start()

            @pl.when(k == nk - 1)
            def _():
                pass_on(0)

        @pl.when(at(2, nk // 2))
        def _():
            pass_on(2)

        @pl.when(at(3, nk // 2))
        def _():
            for cp in wout_fw_in:
                cp.wait_recv()
            out_wog.start()

        tile_rows = pl.ds(pl.multiple_of(k * TM, TM), TM)

        @pl.when(jj == 0)
        def _():
            h = jnp.where(k == nt, mt[...], x_ref[...])
            ms = jnp.mean(h * h, axis=-1, keepdims=True)
            hn = (h * lax.rsqrt(ms + EPS)) * g_ref[...]
            hb = hn.astype(BF16)
            hbs[tile_rows, :] = hb
            proj_ref[...] = _dot(hb, wg[order_ref[0]]).astype(BF16)
            hnt_ref[...] = hn.T.astype(BF16)

        @pl.when(jj > 0)
        def _():
            proj_ref[...] = _dot(hbs[tile_rows, :], wg[order_ref[jj]]).astype(BF16)

        @pl.when(at(3, nk - 1))
        def _():
            for cp in sm_cp + win_cp + wout_cp + win_fw + wout_fw:
                cp.wait_send()
            for cp in (out_wg, out_wog, out_mt, out_cw):
                cp.wait()

    grid_spec = pltpu.PrefetchScalarGridSpec(
        num_scalar_prefetch=1,
        grid=(N_CHIPS, nk),
        in_specs=[pl.BlockSpec((TM, D_MODEL), lambda j, k, o: (jnp.where(j == 0, jnp.minimum(k, nt - 1), nt - 1), 0)),
                  pl.BlockSpec((1, D_MODEL), lambda j, k, o: (0, 0)),
                  pl.BlockSpec((N_META, 256), lambda j, k, o: (0, 0)),
                  pl.BlockSpec((3, 128), lambda j, k, o: (0, 0)),
                  any_spec, any_spec],
        out_specs=[pl.BlockSpec((TM, 1024), lambda j, k, o: (k, o[j])),
                   pl.BlockSpec((None, D_MODEL, TM), lambda j, k, o: (jnp.where(j == 0, k, nk - 1), 0, 0)),
                   any_spec, any_spec, any_spec, any_spec],
        scratch_shapes=[
            pltpu.VMEM((N_CHIPS, D_MODEL, 1024), BF16),
            pltpu.VMEM((N_CHIPS, 256, D_MODEL), BF16),
            pltpu.VMEM((N_CHIPS, SMALL_ROWS, 256), F32),
            pltpu.VMEM((TM, D_MODEL), F32),
            pltpu.VMEM((8, D_CONV), F32),
            pltpu.VMEM((rows, D_MODEL), BF16),
            pltpu.VMEM((TM, D_MODEL), F32),
            pltpu.VMEM((256, D_MODEL), F32),
            pltpu.SemaphoreType.DMA((15,)), pltpu.SemaphoreType.DMA((15,)), pltpu.SemaphoreType.DMA((7,))])
    return pl.pallas_call(
        body, name="f1_norm_inproj_gather",
        grid_spec=grid_spec,
        out_shape=[jax.ShapeDtypeStruct((rows, N_PROJ_COLS), BF16),
                   jax.ShapeDtypeStruct((nk, D_MODEL, TM), BF16),
                   jax.ShapeDtypeStruct((N_CHIPS, D_MODEL, 1024), BF16),
                   jax.ShapeDtypeStruct((N_CHIPS, 256, D_MODEL), BF16),
                   jax.ShapeDtypeStruct((TM, D_MODEL), F32),
                   jax.ShapeDtypeStruct((8, D_CONV), F32)],
        compiler_params=_cparams(("arbitrary", "arbitrary")),
    )(order, x, g1, meta_sh, convw_sh, win_sh, wout_sh)


def _f2_f3_call(proj, conv_w8, gret, tb, x, w_out, fg, target):
    nt, rows = tb["nt"], tb["rows"]
    seq = nt * TM

    def pf(s):
        return jnp.where(s == 0, nt, jnp.minimum(s - 1, nt - 1))

    def xt(s):
        return jnp.clip(s - 2, 0, nt - 1)

    def body(proj_ref, cw_ref, g_ref, tt_ref, cr_ref, sr_ref, sgn_ref, dec_ref, xi_ref, zeta_ref, cd_ref,
             x_ref, w_ref, fg_ref, t_ref,
             conv_hbm, states_hbm, dh2_ref, dmx_ref, gwo_ref, gfg_ref, loss_ref,
             state, uhalo, mxs, convs, sts, lacc, out_sems):
        s = pl.program_id(0)
        slot = lax.rem(s, 2)
        mixed_ref = mxs.at[slot]
        conv_ref = convs.at[slot]
        states_ref = sts.at[slot]

        def conv_out(sl, tile):
            return pltpu.make_async_copy(convs.at[sl], conv_hbm.at[pl.ds(pl.multiple_of(tile * TM, TM), TM), :],
                                         out_sems.at[sl])

        def states_out(sl, tile):
            return pltpu.make_async_copy(sts.at[sl], states_hbm.at[pl.ds(pl.multiple_of(tile * NCH, NCH), NCH)],
                                         out_sems.at[2 + sl])

        @pl.when(s == 0)
        def _():
            state[...] = jnp.zeros_like(state)
            uhalo[...] = jnp.zeros_like(uhalo)
            mxs[...] = jnp.zeros_like(mxs)
            gwo_ref[...] = jnp.zeros_like(gwo_ref)
            gfg_ref[...] = jnp.zeros_like(gfg_ref)
            lacc[...] = jnp.zeros_like(lacc)

        @pl.when(s >= 2)
        def _():
            conv_out(slot, pf(s - 2)).wait()
            states_out(slot, pf(s - 2)).wait()

        valid = jnp.where(s >= 2, 1.0, 0.0)
        mx_prev = mxs.at[1 - slot]
        f3 = {}

        def f3_fwd():
            f3["h2"] = x_ref[...] + _dot(mx_prev[...], w_ref[...])

        def f3_loss():
            h2 = f3.pop("h2")
            ms = jnp.mean(h2 * h2, axis=-1, keepdims=True)
            rstd = lax.rsqrt(ms + EPS)
            yh = h2 * rstd
            g = fg_ref[...]
            e = (yh * g - t_ref[...]) * valid
            lacc[...] += jnp.sum(e * e, axis=0, keepdims=True)
            dy = e * (1.0 / D_MODEL)
            gfg_ref[...] += jnp.sum(dy * yh, axis=0, keepdims=True)
            dyh = dy * g
            dh2 = rstd * (dyh - yh * jnp.mean(dyh * yh, axis=-1, keepdims=True))
            dh2_ref[...] = dh2
            f3["db"] = dh2.astype(BF16)

        def f3_dmx():
            dmx_ref[...] = _dot_tb(f3["db"], w_ref[...]).astype(BF16)

        def f3_gw():
            gw = _dot_ta(mx_prev[...], f3["db"])
            for j in range(N_CHIPS):
                for hf in range(2):
                    r0 = j * 256 + hf * 128
                    gwo_ref[hf, j] += gw[r0:r0 + 128, :]

        cx = proj_ref[:, CX:CX + 512].astype(F32)
        cc = proj_ref[:, CC:CC + 512].astype(F32)
        u = cc * cx
        row = lax.broadcasted_iota(jnp.int32, (TM, D_CONV), 0)
        h7 = uhalo[7:8, :]
        h6 = uhalo[6:7, :]
        u1 = jnp.where(row == 0, h7, pltpu.roll(u, 1, 0))
        u2 = jnp.where(row == 0, h6, jnp.where(row == 1, h7, pltpu.roll(u, 2, 0)))
        conv = cw_ref[2:3, :] * u + cw_ref[1:2, :] * u1 + cw_ref[0:1, :] * u2
        uhalo[...] = u[TM - 8:TM, :]
        cb = proj_ref[:, CB:CB + 512].astype(F32)
        cg = proj_ref[:, CG:CG + 512].astype(F32)
        mixed_ref[:, 0:D_CONV] = (cb * conv * (cg * _sigmoid(cg))).astype(BF16)
        conv_ref[...] = conv.astype(BF16)
        f3_fwd()

        scale = HEAD_DIM ** -0.5
        H = range(RET_HEADS)
        st = [state[h] for h in H]
        between = [f3_loss, f3_dmx, f3_gw, None]
        rc_t, rs_t = _tile_rotary(tt_ref, cr_ref, sr_ref, sgn_ref)
        for c in range(NCH):
            r0 = c * CHUNK
            rc = rc_t[r0:r0 + CHUNK, :]
            rs = rs_t[r0:r0 + CHUNK, :]
            col = lambda base, h: slice(base + h * HEAD_DIM, base + (h + 1) * HEAD_DIM)
            rws = slice(r0, r0 + CHUNK)
            v = [proj_ref[rws, col(CV, h)] for h in H]
            qf = [_rot(proj_ref[rws, col(CQ, h)].astype(F32), rc, rs) * scale for h in H]
            kf = [_rot(proj_ref[rws, col(CK, h)].astype(F32), rc, rs) for h in H]
            stb = [t.astype(BF16) for t in st]
            for h in H:
                states_ref[c, h] = stb[h]
            a = [(_dot_tb(qf[h].astype(BF16), kf[h].astype(BF16)) * dec_ref[h]).astype(BF16) for h in H]
            o = [_dot(a[h], v[h]) + _dot((qf[h] * xi_ref[h]).astype(BF16), stb[h]) for h in H]
            st = [cd_ref[h, 0:1, :] * st[h] + _dot_ta((kf[h] * zeta_ref[h]).astype(BF16), v[h]) for h in H]
            for h in H:
                mu = jnp.mean(o[h], axis=-1, keepdims=True)
                d = o[h] - mu
                var = jnp.mean(d * d, axis=-1, keepdims=True)
                yh = d * lax.rsqrt(var + EPS)
                rg = proj_ref[rws, col(CR, h)].astype(F32)
                mixed_ref[rws, col(D_CONV, h)] = (yh * g_ref[:, col(0, h)] * (rg * _sigmoid(rg))).astype(BF16)
            if between[c] is not None:
                between[c]()
        for h in H:
            state[h] = st[h]

        @pl.when(s <= nt)
        def _():
            conv_out(slot, pf(s)).start()
            states_out(slot, pf(s)).start()

        @pl.when(s == nt + 1)
        def _():
            conv_out(1 - slot, pf(s - 1)).wait()
            states_out(1 - slot, pf(s - 1)).wait()
            tot = jnp.sum(lacc[...], axis=1, keepdims=True) * (0.5 / D_MODEL)
            loss_ref[...] = jnp.broadcast_to(tot, (1, 128))

    tile = lambda w: pl.BlockSpec((TM, w), lambda s: (pf(s), 0))
    xtile = lambda w: pl.BlockSpec((TM, w), lambda s: (xt(s), 0))
    any_spec = pl.BlockSpec(memory_space=pl.ANY)
    return pl.pallas_call(
        body, name="f2_mixer_fwd_f3_outproj_loss",
        grid=(nt + 2,),
        in_specs=[tile(N_PROJ_COLS), _resident((8, D_CONV)), _resident((1, D_RET)),
                  pl.BlockSpec((None, 8, HEAD_DIM), lambda s: (pf(s), 0, 0)),
                  _resident((TM, HEAD_DIM)), _resident((TM, HEAD_DIM)), _resident((8, HEAD_DIM)),
                  _resident((RET_HEADS, CHUNK, CHUNK)), _resident((RET_HEADS, CHUNK, HEAD_DIM)),
                  _resident((RET_HEADS, CHUNK, HEAD_DIM)), _resident((RET_HEADS, 8, HEAD_DIM)),
                  xtile(D_MODEL), _resident1((D_MODEL, D_MODEL)), _resident((1, D_MODEL)), xtile(D_MODEL)],
        out_specs=[any_spec, any_spec, xtile(D_MODEL), xtile(D_MODEL),
                   _resident((2, N_CHIPS, 128, D_MODEL)), _resident((1, D_MODEL)), _resident((1, 128))],
        out_shape=[jax.ShapeDtypeStruct((rows, D_CONV), BF16),
                   jax.ShapeDtypeStruct(((nt + 1) * NCH, RET_HEADS, HEAD_DIM, HEAD_DIM), BF16),
                   jax.ShapeDtypeStruct((seq, D_MODEL), F32),
                   jax.ShapeDtypeStruct((seq, D_MODEL), BF16),
                   jax.ShapeDtypeStruct((2, N_CHIPS, 128, D_MODEL), F32),
                   jax.ShapeDtypeStruct((1, D_MODEL), F32),
                   jax.ShapeDtypeStruct((1, 128), F32)],
        scratch_shapes=[pltpu.VMEM((RET_HEADS, HEAD_DIM, HEAD_DIM), F32), pltpu.VMEM((8, D_CONV), F32),
                        pltpu.VMEM((2, TM, D_MODEL), BF16), pltpu.VMEM((2, TM, D_CONV), BF16),
                        pltpu.VMEM((2, NCH, RET_HEADS, HEAD_DIM, HEAD_DIM), BF16),
                        pltpu.VMEM((1, D_MODEL), F32), pltpu.SemaphoreType.DMA((4,))],
        compiler_params=_cparams(("arbitrary",)),
    )(proj, conv_w8, gret, tb["tt"], tb["cr2"], tb["sr2"], tb["sgn"], tb["decay"], tb["xi"], tb["zeta"], tb["cd"],
      x, w_out, fg, target)


def _b2_b1a_call(proj, dmixed, conv_s, states, conv_w8, gret, tb, w_in_g, x, meta_tile, g1, dh2):
    nt, rows = tb["nt"], tb["rows"]
    seq = nt * TM

    def pb(r):
        return jnp.where(r == nt, nt, nt - 1 - r)

    def xprev(r):
        return jnp.clip(nt - r, 0, nt - 1)

    def body(proj_ref, dmx_ref, conv_ref, states_ref, cw_ref, g_ref, tt_ref, cr_ref, sr_ref, sgn_ref, dec_ref,
             xi_ref, zeta_ref, cd_ref, w_ref, x_ref, mt_ref, g1_ref, dh2_ref,
             dproj_hbm, gcw_ref, gg_ref, gx_ref, dmeta_ref, gn_ref,
             gstate, dchalo, dps, out_sems):
        r = pl.program_id(0)
        live = jnp.where(r == nt, 0.0, 1.0)
        slot = lax.rem(r, 2)
        dproj_ref = dps.at[slot]

        class to_hbm:
            def __init__(self, s, tile):
                self.copies = [pltpu.make_async_copy(dps.at[s, :, j * 1024:(j + 1) * 1024], dproj_hbm.at[tile, j],
                                                     out_sems.at[N_CHIPS * s + j]) for j in range(N_CHIPS)]

            def start(self):
                for cp in self.copies:
                    cp.start()

            def wait(self):
                for cp in self.copies:
                    cp.wait()

        @pl.when(r == 0)
        def _():
            gstate[...] = jnp.zeros_like(gstate)
            dchalo[...] = jnp.zeros_like(dchalo)
            gcw_ref[...] = jnp.zeros_like(gcw_ref)
            gg_ref[...] = jnp.zeros_like(gg_ref)
            gn_ref[...] = jnp.zeros_like(gn_ref)
            dps[...] = jnp.zeros_like(dps)

        @pl.when(r >= 2)
        def _():
            to_hbm(slot, pb(r - 2)).wait()

        dprev = dps.at[1 - slot]
        pieces = []

        def emit_piece():
            j = len(pieces)
            if j < N_CHIPS:
                p = _dot_tb(dprev[:, j * 1024:(j + 1) * 1024], w_ref[j])
                pieces.append(p if j == 0 else pieces[-1] + p)

        cx = proj_ref[:, CX:CX + 512].astype(F32)
        cb = proj_ref[:, CB:CB + 512].astype(F32)
        cc = proj_ref[:, CC:CC + 512].astype(F32)
        cg = proj_ref[:, CG:CG + 512].astype(F32)
        dco = dmx_ref[:, 0:D_CONV].astype(F32) * live
        conv = conv_ref[...].astype(F32)
        sg = _sigmoid(cg)
        sil = cg * sg
        t = dco * conv
        dproj_ref[:, CB:CB + 512] = (t * sil).astype(BF16)
        dproj_ref[:, CG:CG + 512] = (t * cb * (sg * (1.0 + cg * (1.0 - sg)))).astype(BF16)
        dconv = dco * cb * sil
        row = lax.broadcasted_iota(jnp.int32, (TM, D_CONV), 0)
        n0 = dchalo[0:1, :]
        n1 = dchalo[1:2, :]
        dc1 = jnp.where(row == TM - 1, n0, pltpu.roll(dconv, TM - 1, 0))
        dc2 = jnp.where(row == TM - 2, n0, jnp.where(row == TM - 1, n1, pltpu.roll(dconv, TM - 2, 0)))
        dchalo[...] = dconv[0:8, :]
        du = cw_ref[2:3, :] * dconv + cw_ref[1:2, :] * dc1 + cw_ref[0:1, :] * dc2
        u = cc * cx
        gcw_ref[2:3, :] += jnp.sum(u * dconv, axis=0, keepdims=True)
        gcw_ref[1:2, :] += jnp.sum(u * dc1, axis=0, keepdims=True)
        gcw_ref[0:1, :] += jnp.sum(u * dc2, axis=0, keepdims=True)
        dproj_ref[:, CC:CC + 512] = (du * cx).astype(BF16)
        dproj_ref[:, CX:CX + 512] = (du * cc).astype(BF16)
        emit_piece()
        emit_piece()

        scale = HEAD_DIM ** -0.5
        gs = {h: gstate[h] for h in range(RET_HEADS)}
        gg = {h: jnp.zeros((1, HEAD_DIM), F32) for h in range(RET_HEADS)}
        col = lambda base, h: slice(base + h * HEAD_DIM, base + (h + 1) * HEAD_DIM)
        rw = lambda c: slice(c * CHUNK, (c + 1) * CHUNK)
        rc_t, rs_t = _tile_rotary(tt_ref, cr_ref, sr_ref, sgn_ref)
        for c0 in range(NCH - CHUNK_GROUP, -1, -CHUNK_GROUP):
            cs = range(c0 + CHUNK_GROUP - 1, c0 - 1, -1)
            U = [(c, h) for c in cs for h in range(RET_HEADS)]
            rc = {c: rc_t[rw(c), :] for c in cs}
            rs = {c: rs_t[rw(c), :] for c in cs}
            v = {(c, h): proj_ref[rw(c), col(CV, h)] for c, h in U}
            stb = {(c, h): states_ref[c, h] for c, h in U}
            qf = {(c, h): _rot(proj_ref[rw(c), col(CQ, h)].astype(F32), rc[c], rs[c]) * scale for c, h in U}
            kf = {(c, h): _rot(proj_ref[rw(c), col(CK, h)].astype(F32), rc[c], rs[c]) for c, h in U}
            qb = {u: qf[u].astype(BF16) for u in U}
            kb = {u: kf[u].astype(BF16) for u in U}
            qxb = {(c, h): (qf[c, h] * xi_ref[h]).astype(BF16) for c, h in U}
            kzb = {(c, h): (kf[c, h] * zeta_ref[h]).astype(BF16) for c, h in U}
            ab = {(c, h): (_dot_tb(qb[c, h], kb[c, h]) * dec_ref[h]).astype(BF16) for c, h in U}
            o = {u: _dot(ab[u], v[u]) + _dot(qxb[u], stb[u]) for u in U}
            emit_piece()
            dob = {}
            for c, h in U:
                mu = jnp.mean(o[c, h], axis=-1, keepdims=True)
                d = o[c, h] - mu
                var = jnp.mean(d * d, axis=-1, keepdims=True)
                rstd = lax.rsqrt(var + EPS)
                yh = d * rstd
                g = g_ref[:, col(0, h)]
                rg = proj_ref[rw(c), col(CR, h)].astype(F32)
                dro = dmx_ref[rw(c), col(D_CONV, h)].astype(F32) * live
                sg = _sigmoid(rg)
                dproj_ref[rw(c), col(CR, h)] = (dro * (yh * g) * (sg * (1.0 + rg * (1.0 - sg)))).astype(BF16)
                dret = dro * (rg * sg)
                gg[h] = gg[h] + jnp.sum(dret * yh, axis=0, keepdims=True)
                dyh = dret * g
                do = rstd * (dyh - jnp.mean(dyh, axis=-1, keepdims=True)
                             - yh * jnp.mean(dyh * yh, axis=-1, keepdims=True))
                dob[c, h] = do.astype(BF16)
            dv1 = {u: _dot_ta(ab[u], dob[u]) for u in U}
            ds = {(c, h): (_dot_tb(dob[c, h], v[c, h]) * dec_ref[h]).astype(BF16) for c, h in U}
            gup = {u: _dot_ta(qxb[u], dob[u]) for u in U}
            dq = {(c, h): _dot(ds[c, h], kb[c, h]) + _dot_tb(dob[c, h], stb[c, h]) * xi_ref[h] for c, h in U}
            dk1 = {u: _dot_ta(ds[u], qb[u]) for u in U}
            emit_piece()
            for c, h in U:
                gsb = gs[h].astype(BF16)
                dv = dv1[c, h] + _dot(kzb[c, h], gsb)
                dk = dk1[c, h] + _dot_tb(v[c, h], gsb) * zeta_ref[h]
                gs[h] = cd_ref[h, 0:1, :] * gs[h] + gup[c, h]
                dproj_ref[rw(c), col(CQ, h)] = (_rot_t(dq[c, h], rc[c], rs[c]) * scale).astype(BF16)
                dproj_ref[rw(c), col(CK, h)] = _rot_t(dk, rc[c], rs[c]).astype(BF16)
                dproj_ref[rw(c), col(CV, h)] = dv.astype(BF16)
        for h in range(RET_HEADS):
            gstate[h] = gs[h]
            gg_ref[:, col(0, h)] += gg[h]

        while len(pieces) < N_CHIPS:
            emit_piece()

        def norm_bwd(dhn, hx):
            ms = jnp.mean(hx * hx, axis=-1, keepdims=True)
            rstd1 = lax.rsqrt(ms + EPS)
            xh = hx * rstd1
            gn_ref[...] += jnp.sum(dhn * xh, axis=0, keepdims=True)
            dxh = dhn * g1_ref[...]
            return rstd1 * (dxh - xh * jnp.mean(dxh * xh, axis=-1, keepdims=True))

        gx_ref[...] = norm_bwd(pieces[-1], x_ref[...]) + dh2_ref[...]

        @pl.when(r < nt)
        def _():
            to_hbm(slot, pb(r)).start()

        @pl.when(r == nt)
        def _():
            to_hbm(slot, pb(r)).start()
            mrows = slice(TM - N_META, TM)
            d16 = dproj_ref[mrows, :]
            dhn16 = _dot_tb(d16[:, 0:1024], w_ref[0])
            for j in range(1, N_CHIPS):
                dhn16 += _dot_tb(d16[:, j * 1024:(j + 1) * 1024], w_ref[j])
            dmeta_ref[...] = norm_bwd(dhn16, mt_ref[mrows, :])
            to_hbm(1 - slot, pb(r - 1)).wait()
            to_hbm(slot, pb(r)).wait()

    tile = lambda w: pl.BlockSpec((TM, w), lambda r: (pb(r), 0))
    xtile = pl.BlockSpec((TM, D_MODEL), lambda r: (xprev(r), 0))
    return pl.pallas_call(
        body, name="b2_mixer_bwd_b1a_inproj_bwd_x",
        grid=(nt + 1,),
        in_specs=[tile(N_PROJ_COLS),
                  pl.BlockSpec((TM, D_MODEL), lambda r: (jnp.minimum(pb(r), nt - 1), 0)),
                  tile(D_CONV),
                  pl.BlockSpec((NCH, RET_HEADS, HEAD_DIM, HEAD_DIM), lambda r: (pb(r), 0, 0, 0)),
                  _resident((8, D_CONV)), _resident((1, D_RET)),
                  pl.BlockSpec((None, 8, HEAD_DIM), lambda r: (pb(r), 0, 0)),
                  _resident((TM, HEAD_DIM)), _resident((TM, HEAD_DIM)), _resident((8, HEAD_DIM)),
                  _resident((RET_HEADS, CHUNK, CHUNK)),
                  _resident((RET_HEADS, CHUNK, HEAD_DIM)),
                  _resident((RET_HEADS, CHUNK, HEAD_DIM)), _resident((RET_HEADS, 8, HEAD_DIM)),
                  _resident1((N_CHIPS, D_MODEL, 1024)), xtile, _resident1((TM, D_MODEL)), _resident((1, D_MODEL)),
                  xtile],
        out_specs=[pl.BlockSpec(memory_space=pl.ANY), _resident((8, D_CONV)), _resident((1, D_RET)),
                   xtile, _resident((N_META, D_MODEL)), _resident((1, D_MODEL))],
        out_shape=[jax.ShapeDtypeStruct((nt + 1, N_CHIPS, TM, 1024), BF16),
                   jax.ShapeDtypeStruct((8, D_CONV), F32),
                   jax.ShapeDtypeStruct((1, D_RET), F32),
                   jax.ShapeDtypeStruct((seq, D_MODEL), F32),
                   jax.ShapeDtypeStruct((N_META, D_MODEL), F32),
                   jax.ShapeDtypeStruct((1, D_MODEL), F32)],
        scratch_shapes=[pltpu.VMEM((RET_HEADS, HEAD_DIM, HEAD_DIM), F32), pltpu.VMEM((8, D_CONV), F32),
                        pltpu.VMEM((2, TM, N_PROJ_COLS), BF16), pltpu.SemaphoreType.DMA((2 * N_CHIPS,))],
        compiler_params=_cparams(("arbitrary",), vmem=VMEM_LIMIT_MAX),
    )(proj, dmixed, conv_s, states, conv_w8, gret, tb["tt"], tb["cr2"], tb["sr2"], tb["sgn"], tb["decay"],
      tb["xi"], tb["zeta"], tb["cd"], w_in_g, x, meta_tile, g1, dh2)


REL = (2, 1, 3)
SMALL_ROWS = 24
HALF_STEP = 4


def _rcopy(src, dst, send_sems, recv_sems, k, to):
    return pltpu.make_async_remote_copy(src_ref=src, dst_ref=dst, send_sem=send_sems.at[k],
                                        recv_sem=recv_sems.at[k], device_id=to, device_id_type=MESH_ID)


def _b1b_reduce_call(order, hnt, dproj, gwo, small, nt):
    nk = nt + 1
    last = nk - 1
    any_spec = pl.BlockSpec(memory_space=pl.ANY)

    def body(order_ref, a_ref, b_ref, gm_ref, gn1_ref, gfg_ref, ggr_ref, gcw_ref, ls_ref, gwo_hbm,
             gwin_hbm, gwout_hbm, tot_hbm,
             acc, sb, abuf, pb, bbuf, fin, go, ao, pbo, bo, fino, slots, totv, send_sems, recv_sems, loc_sems):
        jj, k = pl.program_id(0), pl.program_id(1)
        x, y, c = lax.axis_index("x"), lax.axis_index("y"), lax.axis_index("c")
        me, myid, sib = 2 * x + y, 4 * x + 2 * y + c, (x, y, 1 - c)
        rc = functools.partial(_rcopy, send_sems=send_sems, recv_sems=recv_sems)
        peers = [((1 - x) if r & 2 else x, (1 - y) if r & 1 else y, c) for r in REL]
        kids = [jnp.bitwise_xor(me, r) for r in REL]

        def dev_peer(r):
            return ((1 - x) if r & 4 else x, (1 - y) if r & 2 else y, (1 - c) if r & 1 else c)

        own_go = pltpu.make_async_copy(gwo_hbm.at[c], go, loc_sems.at[0])
        wo_half = rc(gwo_hbm.at[1 - c], ao, k=8, to=sib)
        wo_part = [rc(pbo.at[kids[p]], bo.at[p], k=9 + p, to=peers[p]) for p in range(3)]
        sm = [rc(slots.at[0], slots.at[r], k=12 + r, to=dev_peer(r)) for r in range(1, N_DEV)]
        half = [rc(sb.at[j % 2, 1 - c], abuf.at[j], k=j, to=sib) for j in range(N_CHIPS)]
        part = [rc(pb.at[p], bbuf.at[p], k=4 + p, to=peers[p]) for p in range(3)]

        @pl.when(jnp.logical_and(jj == 0, k == 0))
        def _():
            own_go.start()
            slots[0, 0:N_META, :] = gm_ref[...]
            slots[0, N_META:SMALL_ROWS, :] = jnp.zeros((SMALL_ROWS - N_META, D_MODEL), F32)
            slots[0, N_META:N_META + 1, :] = gn1_ref[...]
            slots[0, N_META + 1:N_META + 2, :] = gfg_ref[...]
            slots[0, N_META + 2:N_META + 3, 0:D_RET] = ggr_ref[...]
            slots[0, N_META + 3:N_META + 6, 0:D_CONV] = gcw_ref[0:3, :]
            slots[0, N_META + 6:N_META + 7, 0:128] = ls_ref[...]
            wo_half.start()
            for cp in sm:
                cp.start()

        @pl.when(k == 0)
        def _():
            acc[...] = jnp.zeros_like(acc)

        acc[0] += _dot(a_ref[0:512, :], b_ref[...])
        acc[1] += _dot(a_ref[512:1024, :], b_ref[...])

        @pl.when(k == HALF_STEP)
        def _():
            @pl.when(jj == 0)
            def _():
                own_go.wait()
                wo_half.wait_recv()
                for j in range(N_CHIPS):
                    go[j] = go[j] + ao[j]
                pbo[...] = go[...].astype(BF16)
                for cp in wo_part:
                    cp.start()

            for p in range(3):
                @pl.when(jj == p + 1)
                def _(p=p):
                    half[p].wait_recv()
                    half[p].wait_send()
                    pb[p] = (sb[p % 2, c] + abuf[p]).astype(BF16)
                    part[p].start()

        @pl.when(k == last)
        def _():
            for j in range(N_CHIPS):
                @pl.when(jj == j)
                def _(j=j):
                    sb[j % 2] = acc[...]
                    half[j].start()

        @pl.when(jnp.logical_and(jj == N_CHIPS - 1, k == last))
        def _():
            half[3].wait_recv()
            own = sb[1, c] + abuf[3]
            for cp in part:
                cp.wait_recv()
            fin[c] = ((own + bbuf[0].astype(F32)) + bbuf[1].astype(F32)) + bbuf[2].astype(F32)
            done = rc(fin.at[c], fin.at[c], k=7, to=sib)
            done.start()
            out_w = [pltpu.make_async_copy(fin.at[h], gwin_hbm.at[h], loc_sems.at[s]) for h, s in ((c, 0), (1 - c, 3))]
            out_w[0].start()
            for cp in wo_part:
                cp.wait_recv()
            fino[c] = ((go[me] + bo[0].astype(F32)) + bo[1].astype(F32)) + bo[2].astype(F32)
            done_o = rc(fino.at[c], fino.at[c], k=12, to=sib)
            done_o.start()
            out_o = [pltpu.make_async_copy(fino.at[h], gwout_hbm.at[h], loc_sems.at[s]) for h, s in ((c, 2), (1 - c, 4))]
            out_o[0].start()
            for cp in sm:
                cp.wait_recv()
            tot = slots[myid]
            for a in range(1, N_DEV):
                tot = tot + slots[jnp.bitwise_xor(myid, a)]
            totv[...] = tot
            out_t = pltpu.make_async_copy(totv, tot_hbm, loc_sems.at[1])
            out_t.start()
            rc(fin.at[1 - c], fin.at[1 - c], k=7, to=sib).wait_recv()
            out_w[1].start()
            rc(fino.at[1 - c], fino.at[1 - c], k=12, to=sib).wait_recv()
            out_o[1].start()
            for cp in [half[3]] + part + [done, wo_half] + wo_part + [done_o] + sm:
                cp.wait_send()
            for cp in [out_t] + out_w + out_o:
                cp.wait()

    grid_spec = pltpu.PrefetchScalarGridSpec(
        num_scalar_prefetch=1,
        grid=(N_CHIPS, nk),
        in_specs=[pl.BlockSpec((None, D_MODEL, TM), lambda j, k, o: (k, 0, 0)),
                  pl.BlockSpec((None, None, TM, 1024), lambda j, k, o: (k, o[j], 0, 0))]
                 + [pl.BlockSpec(s.shape, lambda j, k, o: (0, 0)) for s in small] + [any_spec],
        out_specs=[any_spec, any_spec, any_spec],
        scratch_shapes=[
            pltpu.VMEM((2, 512, 1024), F32),
            pltpu.VMEM((2, 2, 512, 1024), F32),
            pltpu.VMEM((N_CHIPS, 512, 1024), F32),
            pltpu.VMEM((3, 512, 1024), BF16),
            pltpu.VMEM((3, 512, 1024), BF16),
            pltpu.VMEM((2, 512, 1024), F32),
            pltpu.VMEM((N_CHIPS, 128, D_MODEL), F32),
            pltpu.VMEM((N_CHIPS, 128, D_MODEL), F32),
            pltpu.VMEM((N_CHIPS, 128, D_MODEL), BF16),
            pltpu.VMEM((3, 128, D_MODEL), BF16),
            pltpu.VMEM((2, 128, D_MODEL), F32),
            pltpu.VMEM((N_DEV, SMALL_ROWS, D_MODEL), F32),
            pltpu.VMEM((SMALL_ROWS, D_MODEL), F32),
            pltpu.SemaphoreType.DMA((20,)), pltpu.SemaphoreType.DMA((20,)), pltpu.SemaphoreType.DMA((5,))])
    return pl.pallas_call(
        body, name="b1b_inproj_bwd_w_reduce",
        grid_spec=grid_spec,
        out_shape=[jax.ShapeDtypeStruct((2, 512, 1024), F32),
                   jax.ShapeDtypeStruct((2, 128, D_MODEL), F32),
                   jax.ShapeDtypeStruct((SMALL_ROWS, D_MODEL), F32)],
        compiler_params=_cparams(("arbitrary", "arbitrary")),
    )(order, hnt, dproj, *small, gwo)


def _local_step(me, x, target, g1, gret, fg, win_sh, wout_sh, meta_sh, convw_sh):
    seq = x.shape[0]
    tb = _tables(seq)
    nt = tb["nt"]
    g1r, gretr, fgr = g1.reshape(1, -1), gret.reshape(1, -1), fg.reshape(1, -1)
    order = jnp.stack([me, me ^ REL[0], me ^ REL[1], me ^ REL[2]]).astype(jnp.int32)

    proj, hnt, w_in_g, w_out_g, meta_tile, conv_w8 = _f1_gather_call(order, x, g1r, win_sh, wout_sh, meta_sh,
                                                                     convw_sh, nt)
    w_out = w_out_g.reshape(D_MODEL, D_MODEL)
    conv_s, states, dh2, dmixed, g_wout, g_fg, loss = _f2_f3_call(proj, conv_w8, gretr, tb, x, w_out, fgr, target)
    dproj, g_cw8, g_gret, grad_x, g_meta, g_g1 = _b2_b1a_call(proj, dmixed, conv_s, states, conv_w8, gretr, tb,
                                                              w_in_g, x, meta_tile, g1r, dh2)
    return grad_x, g_wout, (g_meta, g_g1, g_fg, g_gret, g_cw8, loss), hnt, dproj


def _adamw_update(w_ref, g_ref, m_ref, v_ref, d_ref, nm_ref, nv_ref):
    gg = g_ref[...]
    nm = ADAM_B1 * m_ref[...] + (1.0 - ADAM_B1) * gg
    nv = ADAM_B2 * v_ref[...] + (1.0 - ADAM_B2) * (gg * gg)
    m_hat = nm / (1.0 - ADAM_B1 ** ADAM_STEP)
    v_hat = nv / (1.0 - ADAM_B2 ** ADAM_STEP)
    d_ref[...] = -ADAM_LR * (m_hat / (jnp.sqrt(v_hat) + ADAM_EPS) + ADAM_WD * w_ref[...])
    nm_ref[...] = nm
    nv_ref[...] = nv


def _adamw_small_call(me, tot, ws, ms, vs):
    n = len(ws)

    def body(me_ref, tmeta_ref, tvec_ref, tconv_ref, *refs):
        ins, outs = refs[:3 * n], refs[3 * n:]
        g_refs, loss_ref, upd = outs[0:n], outs[n], outs[n + 1:]
        g_refs[0][...] = tmeta_ref[...]
        g_refs[1][...] = tvec_ref[0:1, :]
        g_refs[2][...] = tconv_ref[3:6, :]
        g_refs[3][...] = tvec_ref[2:3, 0:D_RET]
        g_refs[4][...] = tvec_ref[1:2, :]
        loss_ref[...] = tvec_ref[6:7, 0:1]
        for i in range(n):
            _adamw_update(ins[i], g_refs[i], ins[n + i], ins[2 * n + i], upd[i], upd[n + i], upd[2 * n + i])

    whole = lambda a: pl.BlockSpec(a.shape, lambda i, m: (0,) * a.ndim)
    shapes = [jax.ShapeDtypeStruct(w.shape, F32) for w in ws]
    out_shape = shapes + [jax.ShapeDtypeStruct((1, 1), F32)] + shapes * 3
    grid_spec = pltpu.PrefetchScalarGridSpec(
        num_scalar_prefetch=1, grid=(1,),
        in_specs=[pl.BlockSpec((N_META, 256), lambda i, m: (0, m[0])),
                  pl.BlockSpec((8, D_MODEL), lambda i, m: (N_META // 8, 0)),
                  pl.BlockSpec((8, 128), lambda i, m: (N_META // 8, m[0]))] + [whole(a) for a in ws + ms + vs],
        out_specs=[whole(s) for s in out_shape])
    outs = pl.pallas_call(body, name="adamw_small", grid_spec=grid_spec, out_shape=out_shape,
                          compiler_params=_cparams(("arbitrary",)))(me.reshape(1), tot, tot, tot, *ws, *ms, *vs)
    return outs[:n], outs[n], outs[n + 1:2 * n + 1], outs[2 * n + 1:3 * n + 1], outs[3 * n + 1:]


def _adamw_big_call(a4, b4):
    br = 256
    na = a4[0].shape[0] // br
    assert all(t.shape == (na * br, 1024) for t in a4) and all(t.shape == (br, 1024) for t in b4)

    def body(*refs):
        i = pl.program_id(0)
        a_in, b_in, a_out, b_out = refs[0:4], refs[4:8], refs[8:11], refs[11:14]

        @pl.when(i < na)
        def _():
            _adamw_update(*a_in, *a_out)

        @pl.when(i == na)
        def _():
            _adamw_update(*b_in, *b_out)

    spec_a = pl.BlockSpec((br, 1024), lambda i: (jnp.minimum(i, na - 1), 0))
    spec_b = pl.BlockSpec((br, 1024), lambda i: (0, 0))
    outs = pl.pallas_call(
        body, name="adamw_w_in_w_out", grid=(na + 1,),
        in_specs=[spec_a] * 4 + [spec_b] * 4, out_specs=[spec_a] * 3 + [spec_b] * 3,
        out_shape=[jax.ShapeDtypeStruct(a4[0].shape, F32)] * 3 + [jax.ShapeDtypeStruct(b4[0].shape, F32)] * 3,
        compiler_params=_cparams(("arbitrary",)),
    )(*a4, *b4)
    return outs[0:3], outs[3:6]


def kernel(x, meta, norm1_g, w_in, conv_w, ret_norm_g, w_out, final_g, loss_target, m_meta, m_norm1_g, m_w_in, m_conv_w, m_ret_norm_g, m_w_out, m_final_g, v_meta, v_norm1_g, v_w_in, v_conv_w, v_ret_norm_g, v_w_out, v_final_g):
    me = 2 * lax.axis_index("x") + lax.axis_index("y")

    grad_x, g_wo, small, hnt, dproj = _local_step(me, x[0], loss_target[0], norm1_g, ret_norm_g, final_g,
                                                  w_in, w_out, meta, conv_w)

    order = jnp.stack([me ^ REL[0], me ^ REL[1], me ^ REL[2], me]).astype(jnp.int32)
    g_win, g_wout, tot = _b1b_reduce_call(order, hnt, dproj, g_wo, small, x.shape[1] // TM)
    g_win, g_wout = g_win.reshape(D_MODEL, 1024), g_wout.reshape(256, D_MODEL)

    ws = [meta, norm1_g, w_in, conv_w, ret_norm_g, w_out, final_g]
    ms = [m_meta, m_norm1_g, m_w_in, m_conv_w, m_ret_norm_g, m_w_out, m_final_g]
    vs = [v_meta, v_norm1_g, v_w_in, v_conv_w, v_ret_norm_g, v_w_out, v_final_g]
    names = ["meta", "norm1_g", "w_in", "conv_w", "ret_norm_g", "w_out", "final_g"]
    as2d = lambda a: a.reshape(1, -1) if a.ndim == 1 else a
    big = {names.index("w_in"): g_win, names.index("w_out"): g_wout}
    small = [i for i in range(len(names)) if i not in big]
    grads, deltas, new_ms, new_vs = [None] * 7, [None] * 7, [None] * 7, [None] * 7
    ia, ib = names.index("w_in"), names.index("w_out")
    upd = _adamw_big_call(*[(ws[i], big[i], ms[i], vs[i]) for i in (ia, ib)])
    for i, u in zip((ia, ib), upd):
        grads[i] = big[i]
        deltas[i], new_ms[i], new_vs[i] = u
    sg, loss_tot, sd, sm_, sv = _adamw_small_call(me.astype(jnp.int32), tot,
                                                  *[[as2d(t[i]) for i in small] for t in (ws, ms, vs)])
    for j, i in enumerate(small):
        grads[i], deltas[i], new_ms[i], new_vs[i] = (o[j].reshape(ws[i].shape) for o in (sg, sd, sm_, sv))
    return (loss_tot.reshape(()), grad_x[None], *grads, *deltas, *new_ms, *new_vs)
```

```python
import functools

import jax
import jax.numpy as jnp
import numpy as np
from jax import lax
from jax.experimental import pallas as pl
from jax.experimental.pallas import tpu as pltpu

F32 = jnp.float32
BF16 = jnp.bfloat16

D_MODEL = 1024
N_META = 16
D_CONV = 512
D_RET = 512
RET_HEADS = 4
HEAD_DIM = 128
CHUNK = 128
N_PROJ_COLS = 4096
ROPE_BASE = 10000.0
EPS = 1e-6
N_CHIPS = 4
N_DEV = 8

ADAM_LR = 0.001
ADAM_B1 = 0.9
ADAM_B2 = 0.999
ADAM_EPS = 1e-08
ADAM_WD = 0.01
ADAM_STEP = 10

TM = 512
NCH = TM // CHUNK
CHUNK_GROUP = 2
VMEM_LIMIT = 56 * 1024 * 1024
VMEM_LIMIT_MAX = 63 * 1024 * 1024

CX, CB, CC, CG, CQ, CK, CV, CR = (i * 512 for i in range(8))

MESH_ID = pl.DeviceIdType.MESH


def _cparams(sem=None, vmem=VMEM_LIMIT, **kw):
    return pltpu.CompilerParams(dimension_semantics=sem, vmem_limit_bytes=vmem, **kw)


def _sigmoid(x):
    return 1.0 / (1.0 + jnp.exp(-x))


def _dot(a, b):
    return jnp.dot(a, b, preferred_element_type=F32)


def _dot_tb(a, b):
    return lax.dot_general(a, b, (((1,), (1,)), ((), ())), preferred_element_type=F32)


def _dot_ta(a, b):
    return lax.dot_general(a, b, (((0,), (0,)), ((), ())), preferred_element_type=F32)


def _resident(shape):
    nd = len(shape)
    return pl.BlockSpec(shape, lambda *_: (0,) * nd)


def _resident1(shape):
    nd = len(shape)
    return pl.BlockSpec(shape, lambda *_: (0,) * nd, pipeline_mode=pl.Buffered(1))


def _tables(seq):
    f32 = np.float32
    nt = seq // TM
    rows = seq + TM
    half = HEAD_DIM // 2
    freqs = (f32(1.0) / (f32(ROPE_BASE) ** (np.arange(half, dtype=f32) / f32(half)))).astype(f32)
    tile_start = np.concatenate([np.arange(nt, dtype=f32), -np.ones((1,), f32)]) * f32(TM)
    ang_t = tile_start[:, None] * freqs[None, :]
    ang_r = (np.arange(TM, dtype=f32) + f32(N_META))[:, None] * freqs[None, :]
    dup = lambda a: np.concatenate([a, a], axis=-1).astype(f32)
    tt = np.stack([dup(np.cos(ang_t)), dup(np.sin(ang_t))], axis=1)
    tt = np.pad(tt, ((0, 0), (0, 6), (0, 0)))
    cr2, sr2 = dup(np.cos(ang_r)), dup(np.sin(ang_r))
    sgn = np.concatenate([-np.ones((8, half), f32), np.ones((8, half), f32)], axis=-1)
    log_g = np.log(f32(1.0) - f32(2.0) ** (f32(-5.0) - np.arange(RET_HEADS, dtype=f32))).astype(f32)
    idx = np.arange(CHUNK, dtype=f32)
    diff = idx[:, None] - idx[None, :]
    decay = np.where(diff[None] >= 0, np.exp(diff[None] * log_g[:, None, None]), f32(0.0)).astype(f32)
    zeta = np.exp((f32(CHUNK - 1) - idx)[None, :] * log_g[:, None]).astype(f32)
    xi = np.exp((idx + f32(1.0))[None, :] * log_g[:, None]).astype(f32)
    cd = np.exp(f32(CHUNK) * log_g).astype(f32)
    zeta_b = np.broadcast_to(zeta[:, :, None], (RET_HEADS, CHUNK, HEAD_DIM))
    xi_b = np.broadcast_to(xi[:, :, None], (RET_HEADS, CHUNK, HEAD_DIM))
    cd_b = np.broadcast_to(cd[:, None, None], (RET_HEADS, 8, HEAD_DIM))
    tables = dict(tt=tt, cr2=cr2, sr2=sr2, sgn=sgn, decay=decay, zeta=zeta_b, xi=xi_b, cd=cd_b)
    return dict(nt=nt, rows=rows, **{k: jnp.asarray(np.ascontiguousarray(v, dtype=f32)) for k, v in tables.items()})


def _tile_rotary(tt_ref, cr_ref, sr_ref, sgn_ref):
    ct, st = tt_ref[0:1, :], tt_ref[1:2, :]
    cr, sr = cr_ref[...], sr_ref[...]
    return ct * cr - st * sr, (st * cr + ct * sr) * sgn_ref[0:1, :]


def _rot(t, rc, rs):
    return t * rc + pltpu.roll(t, HEAD_DIM // 2, 1) * rs


def _rot_t(dt, rc, rs):
    return dt * rc + pltpu.roll(dt * rs, HEAD_DIM // 2, 1)


def _f1_gather_call(order, x, g1, win_sh, wout_sh, meta_sh, convw_sh, nt):
    nk = nt + 1
    rows = nk * TM
    any_spec = pl.BlockSpec(memory_space=pl.ANY)

    def body(order_ref, x_ref, g_ref, meta_ref, convw_ref, win_hbm, wout_hbm,
             proj_ref, hnt_ref, wg_hbm, wog_hbm, mt_hbm, cw_hbm,
             wg, wog, smg, mt, cw, hbs, st2, st3, send_sems, recv_sems, loc_sems):
        jj, k = pl.program_id(0), pl.program_id(1)
        x, y, c = lax.axis_index("x"), lax.axis_index("y"), lax.axis_index("c")
        me, sib = 2 * x + y, (x, y, 1 - c)
        rc = functools.partial(_rcopy, send_sems=send_sems, recv_sems=recv_sems)
        peers = [((1 - x) if r & 2 else x, (1 - y) if r & 1 else y, c) for r in REL]
        kids = [jnp.bitwise_xor(me, r) for r in REL]
        hw, ho = pl.ds(pl.multiple_of(c * 512, 512), 512), pl.ds(pl.multiple_of(c * 128, 128), 128)
        hw2 = pl.ds(pl.multiple_of((1 - c) * 512, 512), 512)
        ho2 = pl.ds(pl.multiple_of((1 - c) * 128, 128), 128)
        at = lambda j_, k_: jnp.logical_and(jj == j_, k == k_)

        sm_cp = [rc(smg.at[me], smg.at[me], k=p, to=peers[p]) for p in range(3)]
        win_cp = [rc(wg.at[me, hw], wg.at[me, hw], k=3 + p, to=peers[p]) for p in range(3)]
        wout_cp = [rc(wog.at[me, ho], wog.at[me, ho], k=6 + p, to=peers[p]) for p in range(3)]
        sm_in = [rc(smg.at[me], smg.at[kids[p]], k=p, to=sib) for p in range(3)]
        win_in = [rc(wg.at[me, hw], wg.at[kids[p], hw], k=3 + p, to=sib) for p in range(3)]
        wout_in = [rc(wog.at[me, ho], wog.at[kids[p], ho], k=6 + p, to=sib) for p in range(3)]
        win_fw = [rc(wg.at[kids[p], hw], wg.at[kids[p], hw], k=9 + p, to=sib) for p in range(3)]
        wout_fw = [rc(wog.at[kids[p], ho], wog.at[kids[p], ho], k=12 + p, to=sib) for p in range(3)]
        win_fw_in = [rc(wg.at[kids[p], hw2], wg.at[kids[p], hw2], k=9 + p, to=sib) for p in range(3)]
        wout_fw_in = [rc(wog.at[kids[p], ho2], wog.at[kids[p], ho2], k=12 + p, to=sib) for p in range(3)]

        def stage_own():
            cp_a = pltpu.make_async_copy(win_hbm.at[hw], mt, loc_sems.at[0])
            cp_b = pltpu.make_async_copy(win_hbm.at[hw2], st2, loc_sems.at[1])
            cp_c = pltpu.make_async_copy(wout_hbm, st3, loc_sems.at[2])
            cp_a.start()
            cp_c.start()
            cp_b.start()
            cp_a.wait()
            wg[me, hw, :] = mt[...].astype(BF16)
            for s_ in win_cp:
                s_.start()
            cp_c.wait()
            wog[me] = st3[...].astype(BF16)
            for s_ in wout_cp:
                s_.start()
            cp_b.wait()
            wg[me, hw2, :] = st2[...].astype(BF16)
        out_wg = pltpu.make_async_copy(wg, wg_hbm, loc_sems.at[3])
        out_wog = pltpu.make_async_copy(wog, wog_hbm, loc_sems.at[4])
        out_mt = pltpu.make_async_copy(mt, mt_hbm, loc_sems.at[5])
        out_cw = pltpu.make_async_copy(cw, cw_hbm, loc_sems.at[6])

        def pass_on(p):
            win_in[p].wait_recv()
            win_fw[p].start()

        @pl.when(k <= 1)
        def _():
            @pl.when(at(0, 0))
            def _():
                smg[me] = jnp.zeros((SMALL_ROWS, 256), F32)
                smg[me, 0:N_META, :] = meta_ref[...]
                smg[me, N_META:N_META + 3, 0:128] = convw_ref[...]
                for cp in sm_cp:
                    cp.start()
                stage_own()

            for p in range(3):
                @pl.when(at(p + 1, 0))
                def _(p=p):
                    win_fw_in[p].wait_recv()

            @pl.when(at(1, 1))
            def _():
                pass_on(1)

            @pl.when(at(3, 0))
            def _():
                out_wg.start()

            @pl.when(at(3, 1))
            def _():
                for p in range(3):
                    wout_in[p].wait_recv()
                    wout_fw[p].start()

        @pl.when(jnp.logical_and(jj == 0, k >= nk - 2))
        def _():
            @pl.when(k == nk - 2)
            def _():
                for cp in sm_in:
                    cp.wait_recv()
                mt[...] = jnp.zeros_like(mt)
                cw[...] = jnp.zeros_like(cw)
                for j in range(N_CHIPS):
                    mt[TM - N_META:TM, j * 256:(j + 1) * 256] = smg[j, 0:N_META, :]
                    cw[0:3, j * 128:(j + 1) * 128] = smg[j, N_META:N_META + 3, 0:128]
                out_mt.start()
                out_cw.start()

            @pl.when(k == nk - 1)
            def _():
                pass_on(0)

        @pl.when(at(2, nk // 2))
        def _():
            pass_on(2)

        tile_rows = pl.ds(pl.multiple_of(k * TM, TM), TM)

        @pl.when(jj == 0)
        def _():
            h = jnp.where(k == nt, mt[...], x_ref[...])
            ms = jnp.mean(h * h, axis=-1, keepdims=True)
            hn = (h * lax.rsqrt(ms + EPS)) * g_ref[...]
            hb = hn.astype(BF16)
            hbs[tile_rows, :] = hb
            proj_ref[...] = _dot(hb, wg[order_ref[0]]).astype(BF16)
            hnt_ref[...] = hn.T.astype(BF16)

        @pl.when(jj > 0)
        def _():
            proj_ref[...] = _dot(hbs[tile_rows, :], wg[order_ref[jj]]).astype(BF16)

        @pl.when(at(3, nk - 1))
        def _():
            for cp in wout_fw_in:
                cp.wait_recv()
            out_wog.start()
            for cp in sm_cp + win_cp + wout_cp + win_fw + wout_fw:
                cp.wait_send()
            for cp in (out_wg, out_wog, out_mt, out_cw):
                cp.wait()

    grid_spec = pltpu.PrefetchScalarGridSpec(
        num_scalar_prefetch=1,
        grid=(N_CHIPS, nk),
        in_specs=[pl.BlockSpec((TM, D_MODEL), lambda j, k, o: (jnp.where(j == 0, jnp.minimum(k, nt - 1), nt - 1), 0)),
                  pl.BlockSpec((1, D_MODEL), lambda j, k, o: (0, 0)),
                  pl.BlockSpec((N_META, 256), lambda j, k, o: (0, 0)),
                  pl.BlockSpec((3, 128), lambda j, k, o: (0, 0)),
                  any_spec, any_spec],
        out_specs=[pl.BlockSpec((TM, 1024), lambda j, k, o: (k, o[j])),
                   pl.BlockSpec((None, D_MODEL, TM), lambda j, k, o: (jnp.where(j == 0, k, nk - 1), 0, 0)),
                   any_spec, any_spec, any_spec, any_spec],
        scratch_shapes=[
            pltpu.VMEM((N_CHIPS, D_MODEL, 1024), BF16),
            pltpu.VMEM((N_CHIPS, 256, D_MODEL), BF16),
            pltpu.VMEM((N_CHIPS, SMALL_ROWS, 256), F32),
            pltpu.VMEM((TM, D_MODEL), F32),
            pltpu.VMEM((8, D_CONV), F32),
            pltpu.VMEM((rows, D_MODEL), BF16),
            pltpu.VMEM((TM, D_MODEL), F32),
            pltpu.VMEM((256, D_MODEL), F32),
            pltpu.SemaphoreType.DMA((15,)), pltpu.SemaphoreType.DMA((15,)), pltpu.SemaphoreType.DMA((7,))])
    return pl.pallas_call(
        body, name="f1_norm_inproj_gather",
        grid_spec=grid_spec,
        out_shape=[jax.ShapeDtypeStruct((rows, N_PROJ_COLS), BF16),
                   jax.ShapeDtypeStruct((nk, D_MODEL, TM), BF16),
                   jax.ShapeDtypeStruct((N_CHIPS, D_MODEL, 1024), BF16),
                   jax.ShapeDtypeStruct((N_CHIPS, 256, D_MODEL), BF16),
                   jax.ShapeDtypeStruct((TM, D_MODEL), F32),
                   jax.ShapeDtypeStruct((8, D_CONV), F32)],
        compiler_params=_cparams(("arbitrary", "arbitrary")),
    )(order, x, g1, meta_sh, convw_sh, win_sh, wout_sh)


def _f2_f3_call(proj, conv_w8, gret, tb, x, w_out, fg, target):
    nt, rows = tb["nt"], tb["rows"]
    seq = nt * TM

    def pf(s):
        return jnp.where(s == 0, nt, jnp.minimum(s - 1, nt - 1))

    def xt(s):
        return jnp.clip(s - 2, 0, nt - 1)

    def body(proj_ref, cw_ref, g_ref, tt_ref, cr_ref, sr_ref, sgn_ref, dec_ref, xi_ref, zeta_ref, cd_ref,
             x_ref, w_ref, fg_ref, t_ref,
             conv_hbm, states_hbm, dh2_ref, dmx_ref, gwo_ref, gfg_ref, loss_ref,
             state, uhalo, mxs, convs, sts, lacc, out_sems):
        s = pl.program_id(0)
        slot = lax.rem(s, 2)
        mixed_ref = mxs.at[slot]
        conv_ref = convs.at[slot]
        states_ref = sts.at[slot]

        def conv_out(sl, tile):
            return pltpu.make_async_copy(convs.at[sl], conv_hbm.at[pl.ds(pl.multiple_of(tile * TM, TM), TM), :],
                                         out_sems.at[sl])

        def states_out(sl, tile):
            return pltpu.make_async_copy(sts.at[sl], states_hbm.at[pl.ds(pl.multiple_of(tile * NCH, NCH), NCH)],
                                         out_sems.at[2 + sl])

        @pl.when(s == 0)
        def _():
            state[...] = jnp.zeros_like(state)
            uhalo[...] = jnp.zeros_like(uhalo)
            mxs[...] = jnp.zeros_like(mxs)
            gwo_ref[...] = jnp.zeros_like(gwo_ref)
            gfg_ref[...] = jnp.zeros_like(gfg_ref)
            lacc[...] = jnp.zeros_like(lacc)

        @pl.when(s >= 2)
        def _():
            conv_out(slot, pf(s - 2)).wait()
            states_out(slot, pf(s - 2)).wait()

        valid = jnp.where(s >= 2, 1.0, 0.0)
        mx_prev = mxs.at[1 - slot]
        f3 = {}

        def f3_fwd():
            f3["h2"] = x_ref[...] + _dot(mx_prev[...], w_ref[...])

        def f3_loss():
            h2 = f3.pop("h2")
            ms = jnp.mean(h2 * h2, axis=-1, keepdims=True)
            rstd = lax.rsqrt(ms + EPS)
            yh = h2 * rstd
            g = fg_ref[...]
            e = (yh * g - t_ref[...]) * valid
            lacc[...] += jnp.sum(e * e, axis=0, keepdims=True)
            dy = e * (1.0 / D_MODEL)
            gfg_ref[...] += jnp.sum(dy * yh, axis=0, keepdims=True)
            dyh = dy * g
            dh2 = rstd * (dyh - yh * jnp.mean(dyh * yh, axis=-1, keepdims=True))
            dh2_ref[...] = dh2
            f3["db"] = dh2.astype(BF16)

        def f3_dmx():
            dmx_ref[...] = _dot_tb(f3["db"], w_ref[...]).astype(BF16)

        def f3_gw():
            gw = _dot_ta(mx_prev[...], f3["db"])
            for j in range(N_CHIPS):
                for hf in range(2):
                    r0 = j * 256 + hf * 128
                    gwo_ref[hf, j] += gw[r0:r0 + 128, :]

        cx = proj_ref[:, CX:CX + 512].astype(F32)
        cc = proj_ref[:, CC:CC + 512].astype(F32)
        u = cc * cx
        row = lax.broadcasted_iota(jnp.int32, (TM, D_CONV), 0)
        h7 = uhalo[7:8, :]
        h6 = uhalo[6:7, :]
        u1 = jnp.where(row == 0, h7, pltpu.roll(u, 1, 0))
        u2 = jnp.where(row == 0, h6, jnp.where(row == 1, h7, pltpu.roll(u, 2, 0)))
        conv = cw_ref[2:3, :] * u + cw_ref[1:2, :] * u1 + cw_ref[0:1, :] * u2
        uhalo[...] = u[TM - 8:TM, :]
        cb = proj_ref[:, CB:CB + 512].astype(F32)
        cg = proj_ref[:, CG:CG + 512].astype(F32)
        mixed_ref[:, 0:D_CONV] = (cb * conv * (cg * _sigmoid(cg))).astype(BF16)
        conv_ref[...] = conv.astype(BF16)
        f3_fwd()

        scale = HEAD_DIM ** -0.5
        H = range(RET_HEADS)
        st = [state[h] for h in H]
        between = [f3_loss, f3_dmx, f3_gw, None]
        rc_t, rs_t = _tile_rotary(tt_ref, cr_ref, sr_ref, sgn_ref)
        for c in range(NCH):
            r0 = c * CHUNK
            rc = rc_t[r0:r0 + CHUNK, :]
            rs = rs_t[r0:r0 + CHUNK, :]
            col = lambda base, h: slice(base + h * HEAD_DIM, base + (h + 1) * HEAD_DIM)
            rws = slice(r0, r0 + CHUNK)
            v = [proj_ref[rws, col(CV, h)] for h in H]
            qf = [_rot(proj_ref[rws, col(CQ, h)].astype(F32), rc, rs) * scale for h in H]
            kf = [_rot(proj_ref[rws, col(CK, h)].astype(F32), rc, rs) for h in H]
            stb = [t.astype(BF16) for t in st]
            for h in H:
                states_ref[c, h] = stb[h]
            a = [(_dot_tb(qf[h].astype(BF16), kf[h].astype(BF16)) * dec_ref[h]).astype(BF16) for h in H]
            o = [_dot(a[h], v[h]) + _dot((qf[h] * xi_ref[h]).astype(BF16), stb[h]) for h in H]
            st = [cd_ref[h, 0:1, :] * st[h] + _dot_ta((kf[h] * zeta_ref[h]).astype(BF16), v[h]) for h in H]
            for h in H:
                mu = jnp.mean(o[h], axis=-1, keepdims=True)
                d = o[h] - mu
                var = jnp.mean(d * d, axis=-1, keepdims=True)
                yh = d * lax.rsqrt(var + EPS)
                rg = proj_ref[rws, col(CR, h)].astype(F32)
                mixed_ref[rws, col(D_CONV, h)] = (yh * g_ref[:, col(0, h)] * (rg * _sigmoid(rg))).astype(BF16)
            if between[c] is not None:
                between[c]()
        for h in H:
            state[h] = st[h]

        @pl.when(s <= nt)
        def _():
            conv_out(slot, pf(s)).start()
            states_out(slot, pf(s)).start()

        @pl.when(s == nt + 1)
        def _():
            conv_out(1 - slot, pf(s - 1)).wait()
            states_out(1 - slot, pf(s - 1)).wait()
            tot = jnp.sum(lacc[...], axis=1, keepdims=True) * (0.5 / D_MODEL)
            loss_ref[...] = jnp.broadcast_to(tot, (1, 128))

    tile = lambda w: pl.BlockSpec((TM, w), lambda s: (pf(s), 0))
    xtile = lambda w: pl.BlockSpec((TM, w), lambda s: (xt(s), 0))
    any_spec = pl.BlockSpec(memory_space=pl.ANY)
    return pl.pallas_call(
        body, name="f2_mixer_fwd_f3_outproj_loss",
        grid=(nt + 2,),
        in_specs=[tile(N_PROJ_COLS), _resident((8, D_CONV)), _resident((1, D_RET)),
                  pl.BlockSpec((None, 8, HEAD_DIM), lambda s: (pf(s), 0, 0)),
                  _resident((TM, HEAD_DIM)), _resident((TM, HEAD_DIM)), _resident((8, HEAD_DIM)),
                  _resident((RET_HEADS, CHUNK, CHUNK)), _resident((RET_HEADS, CHUNK, HEAD_DIM)),
                  _resident((RET_HEADS, CHUNK, HEAD_DIM)), _resident((RET_HEADS, 8, HEAD_DIM)),
                  xtile(D_MODEL), _resident1((D_MODEL, D_MODEL)), _resident((1, D_MODEL)), xtile(D_MODEL)],
        out_specs=[any_spec, any_spec, xtile(D_MODEL), xtile(D_MODEL),
                   _resident((2, N_CHIPS, 128, D_MODEL)), _resident((1, D_MODEL)), _resident((1, 128))],
        out_shape=[jax.ShapeDtypeStruct((rows, D_CONV), BF16),
                   jax.ShapeDtypeStruct(((nt + 1) * NCH, RET_HEADS, HEAD_DIM, HEAD_DIM), BF16),
                   jax.ShapeDtypeStruct((seq, D_MODEL), F32),
                   jax.ShapeDtypeStruct((seq, D_MODEL), BF16),
                   jax.ShapeDtypeStruct((2, N_CHIPS, 128, D_MODEL), F32),
                   jax.ShapeDtypeStruct((1, D_MODEL), F32),
                   jax.ShapeDtypeStruct((1, 128), F32)],
        scratch_shapes=[pltpu.VMEM((RET_HEADS, HEAD_DIM, HEAD_DIM), F32), pltpu.VMEM((8, D_CONV), F32),
                        pltpu.VMEM((2, TM, D_MODEL), BF16), pltpu.VMEM((2, TM, D_CONV), BF16),
                        pltpu.VMEM((2, NCH, RET_HEADS, HEAD_DIM, HEAD_DIM), BF16),
                        pltpu.VMEM((1, D_MODEL), F32), pltpu.SemaphoreType.DMA((4,))],
        compiler_params=_cparams(("arbitrary",)),
    )(proj, conv_w8, gret, tb["tt"], tb["cr2"], tb["sr2"], tb["sgn"], tb["decay"], tb["xi"], tb["zeta"], tb["cd"],
      x, w_out, fg, target)


def _b2_b1a_call(proj, dmixed, conv_s, states, conv_w8, gret, tb, w_in_g, x, meta_tile, g1, dh2):
    nt, rows = tb["nt"], tb["rows"]
    seq = nt * TM

    def pb(r):
        return jnp.where(r == nt, nt, nt - 1 - r)

    def xprev(r):
        return jnp.clip(nt - r, 0, nt - 1)

    def body(proj_ref, dmx_ref, conv_ref, states_ref, cw_ref, g_ref, tt_ref, cr_ref, sr_ref, sgn_ref, dec_ref,
             xi_ref, zeta_ref, cd_ref, w_ref, x_ref, mt_ref, g1_ref, dh2_ref,
             dproj_hbm, gcw_ref, gg_ref, gx_ref, dmeta_ref, gn_ref,
             gstate, dchalo, dps, out_sems):
        r = pl.program_id(0)
        live = jnp.where(r == nt, 0.0, 1.0)
        slot = lax.rem(r, 2)
        dproj_ref = dps.at[slot]

        class to_hbm:
            def __init__(self, s, tile):
                self.copies = [pltpu.make_async_copy(dps.at[s, :, j * 1024:(j + 1) * 1024], dproj_hbm.at[tile, j],
                                                     out_sems.at[N_CHIPS * s + j]) for j in range(N_CHIPS)]

            def start(self):
                for cp in self.copies:
                    cp.start()

            def wait(self):
                for cp in self.copies:
                    cp.wait()

        @pl.when(r == 0)
        def _():
            gstate[...] = jnp.zeros_like(gstate)
            dchalo[...] = jnp.zeros_like(dchalo)
            gcw_ref[...] = jnp.zeros_like(gcw_ref)
            gg_ref[...] = jnp.zeros_like(gg_ref)
            gn_ref[...] = jnp.zeros_like(gn_ref)
            dps[...] = jnp.zeros_like(dps)

        @pl.when(r >= 2)
        def _():
            to_hbm(slot, pb(r - 2)).wait()

        dprev = dps.at[1 - slot]
        pieces = []

        def emit_piece():
            j = len(pieces)
            if j < N_CHIPS:
                p = _dot_tb(dprev[:, j * 1024:(j + 1) * 1024], w_ref[j])
                pieces.append(p if j == 0 else pieces[-1] + p)

        cx = proj_ref[:, CX:CX + 512].astype(F32)
        cb = proj_ref[:, CB:CB + 512].astype(F32)
        cc = proj_ref[:, CC:CC + 512].astype(F32)
        cg = proj_ref[:, CG:CG + 512].astype(F32)
        dco = dmx_ref[:, 0:D_CONV].astype(F32) * live
        conv = conv_ref[...].astype(F32)
        sg = _sigmoid(cg)
        sil = cg * sg
        t = dco * conv
        dproj_ref[:, CB:CB + 512] = (t * sil).astype(BF16)
        dproj_ref[:, CG:CG + 512] = (t * cb * (sg * (1.0 + cg * (1.0 - sg)))).astype(BF16)
        dconv = dco * cb * sil
        row = lax.broadcasted_iota(jnp.int32, (TM, D_CONV), 0)
        n0 = dchalo[0:1, :]
        n1 = dchalo[1:2, :]
        dc1 = jnp.where(row == TM - 1, n0, pltpu.roll(dconv, TM - 1, 0))
        dc2 = jnp.where(row == TM - 2, n0, jnp.where(row == TM - 1, n1, pltpu.roll(dconv, TM - 2, 0)))
        dchalo[...] = dconv[0:8, :]
        du = cw_ref[2:3, :] * dconv + cw_ref[1:2, :] * dc1 + cw_ref[0:1, :] * dc2
        u = cc * cx
        gcw_ref[2:3, :] += jnp.sum(u * dconv, axis=0, keepdims=True)
        gcw_ref[1:2, :] += jnp.sum(u * dc1, axis=0, keepdims=True)
        gcw_ref[0:1, :] += jnp.sum(u * dc2, axis=0, keepdims=True)
        dproj_ref[:, CC:CC + 512] = (du * cx).astype(BF16)
        dproj_ref[:, CX:CX + 512] = (du * cc).astype(BF16)
        emit_piece()
        emit_piece()

        scale = HEAD_DIM ** -0.5
        gs = {h: gstate[h] for h in range(RET_HEADS)}
        gg = {h: jnp.zeros((1, HEAD_DIM), F32) for h in range(RET_HEADS)}
        col = lambda base, h: slice(base + h * HEAD_DIM, base + (h + 1) * HEAD_DIM)
        rw = lambda c: slice(c * CHUNK, (c + 1) * CHUNK)
        rc_t, rs_t = _tile_rotary(tt_ref, cr_ref, sr_ref, sgn_ref)
        for c0 in range(NCH - CHUNK_GROUP, -1, -CHUNK_GROUP):
            cs = range(c0 + CHUNK_GROUP - 1, c0 - 1, -1)
            U = [(c, h) for c in cs for h in range(RET_HEADS)]
            rc = {c: rc_t[rw(c), :] for c in cs}
            rs = {c: rs_t[rw(c), :] for c in cs}
            v = {(c, h): proj_ref[rw(c), col(CV, h)] for c, h in U}
            stb = {(c, h): states_ref[c, h] for c, h in U}
            qf = {(c, h): _rot(proj_ref[rw(c), col(CQ, h)].astype(F32), rc[c], rs[c]) * scale for c, h in U}
            kf = {(c, h): _rot(proj_ref[rw(c), col(CK, h)].astype(F32), rc[c], rs[c]) for c, h in U}
            qb = {u: qf[u].astype(BF16) for u in U}
            kb = {u: kf[u].astype(BF16) for u in U}
            qxb = {(c, h): (qf[c, h] * xi_ref[h]).astype(BF16) for c, h in U}
            kzb = {(c, h): (kf[c, h] * zeta_ref[h]).astype(BF16) for c, h in U}
            ab = {(c, h): (_dot_tb(qb[c, h], kb[c, h]) * dec_ref[h]).astype(BF16) for c, h in U}
            o = {u: _dot(ab[u], v[u]) + _dot(qxb[u], stb[u]) for u in U}
            emit_piece()
            dob = {}
            for c, h in U:
                mu = jnp.mean(o[c, h], axis=-1, keepdims=True)
                d = o[c, h] - mu
                var = jnp.mean(d * d, axis=-1, keepdims=True)
                rstd = lax.rsqrt(var + EPS)
                yh = d * rstd
                g = g_ref[:, col(0, h)]
                rg = proj_ref[rw(c), col(CR, h)].astype(F32)
                dro = dmx_ref[rw(c), col(D_CONV, h)].astype(F32) * live
                sg = _sigmoid(rg)
                dproj_ref[rw(c), col(CR, h)] = (dro * (yh * g) * (sg * (1.0 + rg * (1.0 - sg)))).astype(BF16)
                dret = dro * (rg * sg)
                gg[h] = gg[h] + jnp.sum(dret * yh, axis=0, keepdims=True)
                dyh = dret * g
                do = rstd * (dyh - jnp.mean(dyh, axis=-1, keepdims=True)
                             - yh * jnp.mean(dyh * yh, axis=-1, keepdims=True))
                dob[c, h] = do.astype(BF16)
            dv1 = {u: _dot_ta(ab[u], dob[u]) for u in U}
            ds = {(c, h): (_dot_tb(dob[c, h], v[c, h]) * dec_ref[h]).astype(BF16) for c, h in U}
            gup = {u: _dot_ta(qxb[u], dob[u]) for u in U}
            dq = {(c, h): _dot(ds[c, h], kb[c, h]) + _dot_tb(dob[c, h], stb[c, h]) * xi_ref[h] for c, h in U}
            dk1 = {u: _dot_ta(ds[u], qb[u]) for u in U}
            emit_piece()
            for c, h in U:
                gsb = gs[h].astype(BF16)
                dv = dv1[c, h] + _dot(kzb[c, h], gsb)
                dk = dk1[c, h] + _dot_tb(v[c, h], gsb) * zeta_ref[h]
                gs[h] = cd_ref[h, 0:1, :] * gs[h] + gup[c, h]
                dproj_ref[rw(c), col(CQ, h)] = (_rot_t(dq[c, h], rc[c], rs[c]) * scale).astype(BF16)
                dproj_ref[rw(c), col(CK, h)] = _rot_t(dk, rc[c], rs[c]).astype(BF16)
                dproj_ref[rw(c), col(CV, h)] = dv.astype(BF16)
        for h in range(RET_HEADS):
            gstate[h] = gs[h]
            gg_ref[:, col(0, h)] += gg[h]

        while len(pieces) < N_CHIPS:
            emit_piece()

        def norm_bwd(dhn, hx):
            ms = jnp.mean(hx * hx, axis=-1, keepdims=True)
            rstd1 = lax.rsqrt(ms + EPS)
            xh = hx * rstd1
            gn_ref[...] += jnp.sum(dhn * xh, axis=0, keepdims=True)
            dxh = dhn * g1_ref[...]
            return rstd1 * (dxh - xh * jnp.mean(dxh * xh, axis=-1, keepdims=True))

        gx_ref[...] = norm_bwd(pieces[-1], x_ref[...]) + dh2_ref[...]

        @pl.when(r < nt)
        def _():
            to_hbm(slot, pb(r)).start()

        @pl.when(r == nt)
        def _():
            to_hbm(slot, pb(r)).start()
            mrows = slice(TM - N_META, TM)
            d16 = dproj_ref[mrows, :]
            dhn16 = _dot_tb(d16[:, 0:1024], w_ref[0])
            for j in range(1, N_CHIPS):
                dhn16 += _dot_tb(d16[:, j * 1024:(j + 1) * 1024], w_ref[j])
            dmeta_ref[...] = norm_bwd(dhn16, mt_ref[mrows, :])
            to_hbm(1 - slot, pb(r - 1)).wait()
            to_hbm(slot, pb(r)).wait()

    tile = lambda w: pl.BlockSpec((TM, w), lambda r: (pb(r), 0))
    xtile = pl.BlockSpec((TM, D_MODEL), lambda r: (xprev(r), 0))
    return pl.pallas_call(
        body, name="b2_mixer_bwd_b1a_inproj_bwd_x",
        grid=(nt + 1,),
        in_specs=[tile(N_PROJ_COLS),
                  pl.BlockSpec((TM, D_MODEL), lambda r: (jnp.minimum(pb(r), nt - 1), 0)),
                  tile(D_CONV),
                  pl.BlockSpec((NCH, RET_HEADS, HEAD_DIM, HEAD_DIM), lambda r: (pb(r), 0, 0, 0)),
                  _resident((8, D_CONV)), _resident((1, D_RET)),
                  pl.BlockSpec((None, 8, HEAD_DIM), lambda r: (pb(r), 0, 0)),
                  _resident((TM, HEAD_DIM)), _resident((TM, HEAD_DIM)), _resident((8, HEAD_DIM)),
                  _resident((RET_HEADS, CHUNK, CHUNK)),
                  _resident((RET_HEADS, CHUNK, HEAD_DIM)),
                  _resident((RET_HEADS, CHUNK, HEAD_DIM)), _resident((RET_HEADS, 8, HEAD_DIM)),
                  _resident1((N_CHIPS, D_MODEL, 1024)), xtile, _resident1((TM, D_MODEL)), _resident((1, D_MODEL)),
                  xtile],
        out_specs=[pl.BlockSpec(memory_space=pl.ANY), _resident((8, D_CONV)), _resident((1, D_RET)),
                   xtile, _resident((N_META, D_MODEL)), _resident((1, D_MODEL))],
        out_shape=[jax.ShapeDtypeStruct((nt + 1, N_CHIPS, TM, 1024), BF16),
                   jax.ShapeDtypeStruct((8, D_CONV), F32),
                   jax.ShapeDtypeStruct((1, D_RET), F32),
                   jax.ShapeDtypeStruct((seq, D_MODEL), F32),
                   jax.ShapeDtypeStruct((N_META, D_MODEL), F32),
                   jax.ShapeDtypeStruct((1, D_MODEL), F32)],
        scratch_shapes=[pltpu.VMEM((RET_HEADS, HEAD_DIM, HEAD_DIM), F32), pltpu.VMEM((8, D_CONV), F32),
                        pltpu.VMEM((2, TM, N_PROJ_COLS), BF16), pltpu.SemaphoreType.DMA((2 * N_CHIPS,))],
        compiler_params=_cparams(("arbitrary",), vmem=VMEM_LIMIT_MAX),
    )(proj, dmixed, conv_s, states, conv_w8, gret, tb["tt"], tb["cr2"], tb["sr2"], tb["sgn"], tb["decay"],
      tb["xi"], tb["zeta"], tb["cd"], w_in_g, x, meta_tile, g1, dh2)


REL = (2, 1, 3)
SMALL_ROWS = 24
HALF_STEP = 4


def _rcopy(src, dst, send_sems, recv_sems, k, to):
    return pltpu.make_async_remote_copy(src_ref=src, dst_ref=dst, send_sem=send_sems.at[k],
                                        recv_sem=recv_sems.at[k], device_id=to, device_id_type=MESH_ID)


def _b1b_reduce_call(order, hnt, dproj, gwo, small, nt):
    nk = nt + 1
    last = nk - 1
    any_spec = pl.BlockSpec(memory_space=pl.ANY)

    def body(order_ref, a_ref, b_ref, gm_ref, gn1_ref, gfg_ref, ggr_ref, gcw_ref, ls_ref, gwo_hbm,
             gwin_hbm, gwout_hbm, tot_hbm,
             acc, sb, abuf, pb, bbuf, fin, go, ao, pbo, bo, fino, slots, totv, send_sems, recv_sems, loc_sems):
        jj, k = pl.program_id(0), pl.program_id(1)
        x, y, c = lax.axis_index("x"), lax.axis_index("y"), lax.axis_index("c")
        me, myid, sib = 2 * x + y, 4 * x + 2 * y + c, (x, y, 1 - c)
        rc = functools.partial(_rcopy, send_sems=send_sems, recv_sems=recv_sems)
        peers = [((1 - x) if r & 2 else x, (1 - y) if r & 1 else y, c) for r in REL]
        kids = [jnp.bitwise_xor(me, r) for r in REL]

        def dev_peer(r):
            return ((1 - x) if r & 4 else x, (1 - y) if r & 2 else y, (1 - c) if r & 1 else c)

        own_go = pltpu.make_async_copy(gwo_hbm.at[c], go, loc_sems.at[0])
        wo_half = rc(gwo_hbm.at[1 - c], ao, k=8, to=sib)
        wo_part = [rc(pbo.at[kids[p]], bo.at[p], k=9 + p, to=peers[p]) for p in range(3)]
        sm = [rc(slots.at[0], slots.at[r], k=12 + r, to=dev_peer(r)) for r in range(1, N_DEV)]
        half = [rc(sb.at[j % 2, 1 - c], abuf.at[j], k=j, to=sib) for j in range(N_CHIPS)]
        part = [rc(pb.at[p], bbuf.at[p], k=4 + p, to=peers[p]) for p in range(3)]

        @pl.when(jnp.logical_and(jj == 0, k == 0))
        def _():
            own_go.start()
            slots[0, 0:N_META, :] = gm_ref[...]
            slots[0, N_META:SMALL_ROWS, :] = jnp.zeros((SMALL_ROWS - N_META, D_MODEL), F32)
            slots[0, N_META:N_META + 1, :] = gn1_ref[...]
            slots[0, N_META + 1:N_META + 2, :] = gfg_ref[...]
            slots[0, N_META + 2:N_META + 3, 0:D_RET] = ggr_ref[...]
            slots[0, N_META + 3:N_META + 6, 0:D_CONV] = gcw_ref[0:3, :]
            slots[0, N_META + 6:N_META + 7, 0:128] = ls_ref[...]
            wo_half.start()
            for cp in sm:
                cp.start()

        @pl.when(k == 0)
        def _():
            acc[...] = jnp.zeros_like(acc)

        acc[0] += _dot(a_ref[0:512, :], b_ref[...])
        acc[1] += _dot(a_ref[512:1024, :], b_ref[...])

        @pl.when(k == HALF_STEP)
        def _():
            @pl.when(jj == 0)
            def _():
                own_go.wait()
                wo_half.wait_recv()
                for j in range(N_CHIPS):
                    go[j] = go[j] + ao[j]
                pbo[...] = go[...].astype(BF16)
                for cp in wo_part:
                    cp.start()

            for p in range(3):
                @pl.when(jj == p + 1)
                def _(p=p):
                    half[p].wait_recv()
                    half[p].wait_send()
                    pb[p] = (sb[p % 2, c] + abuf[p]).astype(BF16)
                    part[p].start()

        @pl.when(k == last)
        def _():
            for j in range(N_CHIPS):
                @pl.when(jj == j)
                def _(j=j):
                    sb[j % 2] = acc[...]
                    half[j].start()

        @pl.when(jnp.logical_and(jj == N_CHIPS - 1, k == last))
        def _():
            half[3].wait_recv()
            own = sb[1, c] + abuf[3]
            for cp in part:
                cp.wait_recv()
            fin[c] = ((own + bbuf[0].astype(F32)) + bbuf[1].astype(F32)) + bbuf[2].astype(F32)
            done = rc(fin.at[c], fin.at[c], k=7, to=sib)
            done.start()
            for cp in wo_part:
                cp.wait_recv()
            fino[c] = ((go[me] + bo[0].astype(F32)) + bo[1].astype(F32)) + bo[2].astype(F32)
            done_o = rc(fino.at[c], fino.at[c], k=12, to=sib)
            done_o.start()
            for cp in sm:
                cp.wait_recv()
            tot = slots[myid]
            for a in range(1, N_DEV):
                tot = tot + slots[jnp.bitwise_xor(myid, a)]
            totv[...] = tot
            out_t = pltpu.make_async_copy(totv, tot_hbm, loc_sems.at[1])
            out_t.start()
            rc(fin.at[1 - c], fin.at[1 - c], k=7, to=sib).wait_recv()
            out_w = pltpu.make_async_copy(fin, gwin_hbm, loc_sems.at[0])
            out_w.start()
            rc(fino.at[1 - c], fino.at[1 - c], k=12, to=sib).wait_recv()
            out_o = pltpu.make_async_copy(fino, gwout_hbm, loc_sems.at[2])
            out_o.start()
            for cp in [half[3]] + part + [done, wo_half] + wo_part + [done_o] + sm:
                cp.wait_send()
            out_t.wait()
            out_w.wait()
            out_o.wait()

    grid_spec = pltpu.PrefetchScalarGridSpec(
        num_scalar_prefetch=1,
        grid=(N_CHIPS, nk),
        in_specs=[pl.BlockSpec((None, D_MODEL, TM), lambda j, k, o: (k, 0, 0)),
                  pl.BlockSpec((None, None, TM, 1024), lambda j, k, o: (k, o[j], 0, 0))]
                 + [pl.BlockSpec(s.shape, lambda j, k, o: (0, 0)) for s in small] + [any_spec],
        out_specs=[any_spec, any_spec, any_spec],
        scratch_shapes=[
            pltpu.VMEM((2, 512, 1024), F32),
            pltpu.VMEM((2, 2, 512, 1024), F32),
            pltpu.VMEM((N_CHIPS, 512, 1024), F32),
            pltpu.VMEM((3, 512, 1024), BF16),
            pltpu.VMEM((3, 512, 1024), BF16),
            pltpu.VMEM((2, 512, 1024), F32),
            pltpu.VMEM((N_CHIPS, 128, D_MODEL), F32),
            pltpu.VMEM((N_CHIPS, 128, D_MODEL), F32),
            pltpu.VMEM((N_CHIPS, 128, D_MODEL), BF16),
            pltpu.VMEM((3, 128, D_MODEL), BF16),
            pltpu.VMEM((2, 128, D_MODEL), F32),
            pltpu.VMEM((N_DEV, SMALL_ROWS, D_MODEL), F32),
            pltpu.VMEM((SMALL_ROWS, D_MODEL), F32),
            pltpu.SemaphoreType.DMA((20,)), pltpu.SemaphoreType.DMA((20,)), pltpu.SemaphoreType.DMA((3,))])
    return pl.pallas_call(
        body, name="b1b_inproj_bwd_w_reduce",
        grid_spec=grid_spec,
        out_shape=[jax.ShapeDtypeStruct((2, 512, 1024), F32),
                   jax.ShapeDtypeStruct((2, 128, D_MODEL), F32),
                   jax.ShapeDtypeStruct((SMALL_ROWS, D_MODEL), F32)],
        compiler_params=_cparams(("arbitrary", "arbitrary")),
    )(order, hnt, dproj, *small, gwo)


def _local_step(me, x, target, g1, gret, fg, win_sh, wout_sh, meta_sh, convw_sh):
    seq = x.shape[0]
    tb = _tables(seq)
    nt = tb["nt"]
    g1r, gretr, fgr = g1.reshape(1, -1), gret.reshape(1, -1), fg.reshape(1, -1)
    order = jnp.stack([me, me ^ REL[0], me ^ REL[1], me ^ REL[2]]).astype(jnp.int32)

    proj, hnt, w_in_g, w_out_g, meta_tile, conv_w8 = _f1_gather_call(order, x, g1r, win_sh, wout_sh, meta_sh,
                                                                     convw_sh, nt)
    w_out = w_out_g.reshape(D_MODEL, D_MODEL)
    conv_s, states, dh2, dmixed, g_wout, g_fg, loss = _f2_f3_call(proj, conv_w8, gretr, tb, x, w_out, fgr, target)
    dproj, g_cw8, g_gret, grad_x, g_meta, g_g1 = _b2_b1a_call(proj, dmixed, conv_s, states, conv_w8, gretr, tb,
                                                              w_in_g, x, meta_tile, g1r, dh2)
    return grad_x, g_wout, (g_meta, g_g1, g_fg, g_gret, g_cw8, loss), hnt, dproj


def _adamw_update(w_ref, g_ref, m_ref, v_ref, d_ref, nm_ref, nv_ref):
    gg = g_ref[...]
    nm = ADAM_B1 * m_ref[...] + (1.0 - ADAM_B1) * gg
    nv = ADAM_B2 * v_ref[...] + (1.0 - ADAM_B2) * (gg * gg)
    m_hat = nm / (1.0 - ADAM_B1 ** ADAM_STEP)
    v_hat = nv / (1.0 - ADAM_B2 ** ADAM_STEP)
    d_ref[...] = -ADAM_LR * (m_hat / (jnp.sqrt(v_hat) + ADAM_EPS) + ADAM_WD * w_ref[...])
    nm_ref[...] = nm
    nv_ref[...] = nv


def _adamw_small_call(me, tot, ws, ms, vs):
    n = len(ws)

    def body(me_ref, tmeta_ref, tvec_ref, tconv_ref, *refs):
        ins, outs = refs[:3 * n], refs[3 * n:]
        g_refs, loss_ref, upd = outs[0:n], outs[n], outs[n + 1:]
        g_refs[0][...] = tmeta_ref[...]
        g_refs[1][...] = tvec_ref[0:1, :]
        g_refs[2][...] = tconv_ref[3:6, :]
        g_refs[3][...] = tvec_ref[2:3, 0:D_RET]
        g_refs[4][...] = tvec_ref[1:2, :]
        loss_ref[...] = tvec_ref[6:7, 0:1]
        for i in range(n):
            _adamw_update(ins[i], g_refs[i], ins[n + i], ins[2 * n + i], upd[i], upd[n + i], upd[2 * n + i])

    whole = lambda a: pl.BlockSpec(a.shape, lambda i, m: (0,) * a.ndim)
    shapes = [jax.ShapeDtypeStruct(w.shape, F32) for w in ws]
    out_shape = shapes + [jax.ShapeDtypeStruct((1, 1), F32)] + shapes * 3
    grid_spec = pltpu.PrefetchScalarGridSpec(
        num_scalar_prefetch=1, grid=(1,),
        in_specs=[pl.BlockSpec((N_META, 256), lambda i, m: (0, m[0])),
                  pl.BlockSpec((8, D_MODEL), lambda i, m: (N_META // 8, 0)),
                  pl.BlockSpec((8, 128), lambda i, m: (N_META // 8, m[0]))] + [whole(a) for a in ws + ms + vs],
        out_specs=[whole(s) for s in out_shape])
    outs = pl.pallas_call(body, name="adamw_small", grid_spec=grid_spec, out_shape=out_shape,
                          compiler_params=_cparams(("arbitrary",)))(me.reshape(1), tot, tot, tot, *ws, *ms, *vs)
    return outs[:n], outs[n], outs[n + 1:2 * n + 1], outs[2 * n + 1:3 * n + 1], outs[3 * n + 1:]


def _adamw_big_call(a4, b4):
    br = 256
    na = a4[0].shape[0] // br
    assert all(t.shape == (na * br, 1024) for t in a4) and all(t.shape == (br, 1024) for t in b4)

    def body(*refs):
        i = pl.program_id(0)
        a_in, b_in, a_out, b_out = refs[0:4], refs[4:8], refs[8:12], refs[12:16]

        @pl.when(i < na)
        def _():
            _adamw_update(*a_in, *a_out[0:3])
            a_out[3][...] = a_in[1][...]

        @pl.when(i == na)
        def _():
            _adamw_update(*b_in, *b_out[0:3])
            b_out[3][...] = b_in[1][...]

    spec_a = pl.BlockSpec((br, 1024), lambda i: (jnp.minimum(i, na - 1), 0))
    spec_b = pl.BlockSpec((br, 1024), lambda i: (0, 0))
    outs = pl.pallas_call(
        body, name="adamw_w_in_w_out", grid=(na + 1,),
        in_specs=[spec_a] * 4 + [spec_b] * 4, out_specs=[spec_a] * 4 + [spec_b] * 4,
        out_shape=[jax.ShapeDtypeStruct(a4[0].shape, F32)] * 4 + [jax.ShapeDtypeStruct(b4[0].shape, F32)] * 4,
        compiler_params=_cparams(("arbitrary",)),
    )(*a4, *b4)
    return outs[0:4], outs[4:8]


def kernel(x, meta, norm1_g, w_in, conv_w, ret_norm_g, w_out, final_g, loss_target, m_meta, m_norm1_g, m_w_in, m_conv_w, m_ret_norm_g, m_w_out, m_final_g, v_meta, v_norm1_g, v_w_in, v_conv_w, v_ret_norm_g, v_w_out, v_final_g):
    me = 2 * lax.axis_index("x") + lax.axis_index("y")

    grad_x, g_wo, small, hnt, dproj = _local_step(me, x[0], loss_target[0], norm1_g, ret_norm_g, final_g,
                                                  w_in, w_out, meta, conv_w)

    order = jnp.stack([me ^ REL[0], me ^ REL[1], me ^ REL[2], me]).astype(jnp.int32)
    g_win, g_wout, tot = _b1b_reduce_call(order, hnt, dproj, g_wo, small, x.shape[1] // TM)
    g_win, g_wout = g_win.reshape(D_MODEL, 1024), g_wout.reshape(256, D_MODEL)

    ws = [meta, norm1_g, w_in, conv_w, ret_norm_g, w_out, final_g]
    ms = [m_meta, m_norm1_g, m_w_in, m_conv_w, m_ret_norm_g, m_w_out, m_final_g]
    vs = [v_meta, v_norm1_g, v_w_in, v_conv_w, v_ret_norm_g, v_w_out, v_final_g]
    names = ["meta", "norm1_g", "w_in", "conv_w", "ret_norm_g", "w_out", "final_g"]
    as2d = lambda a: a.reshape(1, -1) if a.ndim == 1 else a
    big = {names.index("w_in"): g_win, names.index("w_out"): g_wout}
    small = [i for i in range(len(names)) if i not in big]
    grads, deltas, new_ms, new_vs = [None] * 7, [None] * 7, [None] * 7, [None] * 7
    ia, ib = names.index("w_in"), names.index("w_out")
    upd = _adamw_big_call(*[(ws[i], big[i], ms[i], vs[i]) for i in (ia, ib)])
    for i, u in zip((ia, ib), upd):
        deltas[i], new_ms[i], new_vs[i], grads[i] = u
    sg, loss_tot, sd, sm_, sv = _adamw_small_call(me.astype(jnp.int32), tot,
                                                  *[[as2d(t[i]) for i in small] for t in (ws, ms, vs)])
    for j, i in enumerate(small):
        grads[i], deltas[i], new_ms[i], new_vs[i] = (o[j].reshape(ws[i].shape) for o in (sg, sd, sm_, sv))
    return (loss_tot.reshape(()), grad_x[None], *grads, *deltas, *new_ms, *new_vs)
```
